```python
import jax
import jax.numpy as jnp
from jax import lax
import numpy as np

D_MODEL = 1024
BATCH = 32
SEQ = 256
DEPTH = 2
DEC_BATCH = 4
DEC_SEQ = 1024
PAST_LEN = 256

GRID_W = 64
RET_HEADS = 4
RET_DK = 64
RET_DV = 128
RET_CHUNK = 128
RET_DECAY_OFF_FWD = 5.0
RET_DECAY_OFF_BWD = 5.5
GLA_HEADS = 4
GLA_DK = 64
GLA_DV = 128
GLA_RANK = 16
GLA_TAU = 16.0
GLA_CHUNK = 32
MLA_HEADS = 8
MLA_Q_LORA = 256
MLA_KV_LORA = 128
MLA_NOPE = 64
MLA_ROPE = 32
MLA_V = 64
ROPE_BASE = 10000.0
Q_BLOCK = 128
N_BRANCH = 3
BRANCH_W = 512
N_EXPERTS = 32
TOP_K = 4
D_EXPERT = 1024
SWIGLU_LIMIT = 7.0
SWIGLU_ALPHA = 1.702
MOE_BLOCK = 128
EPS = 1e-6

IN_SPLIT_SIZES = (RET_HEADS * RET_DK, RET_HEADS * RET_DK, RET_HEADS * RET_DV, RET_HEADS * RET_DV,
                  GLA_HEADS * GLA_DK, GLA_HEADS * GLA_DK, GLA_HEADS * GLA_DV, GLA_HEADS * GLA_DV,
                  GLA_RANK, GLA_RANK, MLA_Q_LORA, MLA_KV_LORA, MLA_ROPE,
                  D_MODEL, D_MODEL, D_MODEL)
D_IN = 6592

kernel_name = "hybrid_ret_gla_mla_moe_diffusion_step"


def rmsnorm(x, g):
    xf = x.astype(jnp.float32)
    y = xf * lax.rsqrt(jnp.mean(xf * xf, axis=-1, keepdims=True) + EPS)
    return (y * g.astype(jnp.float32)).astype(x.dtype)


def to_heads(t, n):
    B, S, _ = t.shape
    return t.reshape(B, S, n, -1).transpose(0, 2, 1, 3)


def from_heads(t):
    B, n, S, d = t.shape
    return t.transpose(0, 2, 1, 3).reshape(B, S, n * d)


def flip(t):
    return t[:, :, ::-1]


def ret_log_decay(offset):
    return jnp.log1p(-jnp.exp2(-(offset + jnp.arange(RET_HEADS, dtype=jnp.float32))))


def retention_scan(q, k, v, log_g, s0):
    f = jnp.float32
    B, H, S, Dk = q.shape
    Dv = v.shape[-1]
    C = RET_CHUNK
    n = S // C
    qc = q.astype(f).reshape(B, H, n, C, Dk)
    kc = k.astype(f).reshape(B, H, n, C, Dk)
    vc = v.astype(f).reshape(B, H, n, C, Dv)
    idx = jnp.arange(C, dtype=f)
    diff = idx[:, None] - idx[None, :]
    dmask = jnp.exp(jnp.where(diff >= 0, log_g[:, None, None] * diff, -jnp.inf))
    scores = jnp.einsum('bhnid,bhnjd->bhnij', qc, kc) * dmask[None, :, None]
    o_intra = jnp.einsum('bhnij,bhnjv->bhniv', scores, vc)
    q_dec = jnp.exp(log_g[:, None] * (idx + 1.0))
    k_dec = jnp.exp(log_g[:, None] * (C - 1.0 - idx))
    c_dec = jnp.exp(log_g * C)
    kv = jnp.einsum('bhnjd,bhnjv->bhndv', kc * k_dec[None, :, None, :, None], vc)

    def step(s, kv_i):
        return c_dec[None, :, None, None] * s + kv_i, s

    s_final, s_prev = lax.scan(step, s0.astype(f), jnp.moveaxis(kv, 2, 0))
    s_prev = jnp.moveaxis(s_prev, 0, 2)
    o_inter = jnp.einsum('bhnid,bhndv->bhniv', qc * q_dec[None, :, None, :, None], s_prev)
    return (o_intra + o_inter).reshape(B, H, S, Dv), s_final


def gla_scan(q, k, v, log_a, s0):
    f = jnp.float32
    B, H, S, Dk = q.shape
    Dv = v.shape[-1]
    C = GLA_CHUNK
    n = S // C
    qc = q.astype(f).reshape(B, H, n, C, Dk)
    kc = k.astype(f).reshape(B, H, n, C, Dk)
    vc = v.astype(f).reshape(B, H, n, C, Dv)
    b = jnp.cumsum(log_a.astype(f).reshape(B, H, n, C, Dk), axis=3)
    idx = jnp.arange(C)
    causal = (idx[:, None] >= idx[None, :])[:, :, None]
    w = jnp.exp(jnp.where(causal, b[:, :, :, :, None, :] - b[:, :, :, None, :, :], -jnp.inf))
    attn = jnp.einsum('bhnid,bhnjd,bhnijd->bhnij', qc, kc, w)
    o_intra = jnp.einsum('bhnij,bhnjv->bhniv', attn, vc)
    b_last = b[:, :, :, -1:, :]
    kv = jnp.einsum('bhnjd,bhnjv->bhndv', kc * jnp.exp(b_last - b), vc)
    decay = jnp.exp(b_last[:, :, :, 0, :])

    def step(s, inp):
        kv_i, d_i = inp
        return d_i[..., None] * s + kv_i, s

    s_final, s_prev = lax.scan(step, s0.astype(f), (jnp.moveaxis(kv, 2, 0), jnp.moveaxis(decay, 2, 0)))
    s_prev = jnp.moveaxis(s_prev, 0, 2)
    o_inter = jnp.einsum('bhnid,bhndv->bhniv', qc * jnp.exp(b), s_prev)
    return (o_intra + o_inter).reshape(B, H, S, Dv), s_final


def axial_angles(S):
    rows = S // GRID_W
    r = jnp.repeat(jnp.arange(rows), GRID_W).astype(jnp.float32)
    col = jnp.tile(jnp.arange(GRID_W), rows).astype(jnp.float32)
    nf = MLA_ROPE // 4
    freqs = ROPE_BASE ** (-jnp.arange(nf, dtype=jnp.float32) / nf)
    return r[:, None] * freqs, col[:, None] * freqs


def rotate(x, ang):
    nf = ang.shape[-1]
    x1, x2 = x[..., :nf], x[..., nf:]
    c, s = jnp.cos(ang).astype(x.dtype), jnp.sin(ang).astype(x.dtype)
    return jnp.concatenate([x1 * c - x2 * s, x1 * s + x2 * c], axis=-1)


def apply_axial_rope(t, ang_r, ang_c):
    half = MLA_ROPE // 2
    nope = t[..., :MLA_NOPE]
    tr = t[..., MLA_NOPE:MLA_NOPE + half]
    tc = t[..., MLA_NOPE + half:]
    return jnp.concatenate([nope, rotate(tr, ang_r), rotate(tc, ang_c)], axis=-1)


def mla_kv(ckv, krope, w_ukv, kn_g):
    B, L, _ = ckv.shape
    kv = (ckv @ w_ukv).reshape(B, L, MLA_HEADS, MLA_NOPE + MLA_V)
    k_nope, v = kv[..., :MLA_NOPE], kv[..., MLA_NOPE:]
    k = jnp.concatenate([k_nope, jnp.broadcast_to(krope[:, :, None, :], (B, L, MLA_HEADS, MLA_ROPE))], axis=-1)
    k = rmsnorm(k, kn_g)
    return k.transpose(0, 2, 1, 3), v.transpose(0, 2, 1, 3)


def block_attention(q, k, v, scale):
    B, H, Sq, Dk = q.shape
    nb = Sq // Q_BLOCK
    qb = q.reshape(B, H, nb, Q_BLOCK, Dk).transpose(2, 0, 1, 3, 4)

    def one(qblk):
        s = jnp.einsum('bhqd,bhkd->bhqk', qblk, k).astype(jnp.float32) * scale
        p = jax.nn.softmax(s, axis=-1).astype(v.dtype)
        return jnp.einsum('bhqk,bhkd->bhqd', p, v)

    out = lax.map(one, qb)
    return out.transpose(1, 2, 0, 3, 4).reshape(B, H, Sq, -1)


def token_mixers(h, p, ctx):
    B, S, _ = h.shape
    z = h @ p['w_in']
    splits = np.cumsum(IN_SPLIT_SIZES)[:-1].tolist()
    (rq, rk, rv, rg, gq, gk, gv, gr, ga_f, ga_b, cq, ckv_raw, krope,
     m_ret, m_gla, m_mla) = jnp.split(z, splits, axis=-1)
    if ctx is None:
        ret0 = jnp.zeros((B, 2, RET_HEADS, RET_DK, RET_DV), jnp.float32)
        gla0 = jnp.zeros((B, 2, GLA_HEADS, GLA_DK, GLA_DV), jnp.float32)
    else:
        ctx_ckv, ctx_krope, ret0, gla0 = ctx

    q = to_heads(rq, RET_HEADS)
    k = to_heads(rk, RET_HEADS) * (RET_DK ** -0.5)
    v = to_heads(rv, RET_HEADS)
    o_f, ret_sf = retention_scan(q, k, v, ret_log_decay(RET_DECAY_OFF_FWD), ret0[:, 0])
    o_b, ret_sb = retention_scan(flip(q), flip(k), flip(v), ret_log_decay(RET_DECAY_OFF_BWD), ret0[:, 1])
    o = o_f + flip(o_b)
    mu = jnp.mean(o, axis=-1, keepdims=True)
    var = jnp.mean(jnp.square(o - mu), axis=-1, keepdims=True)
    o = (o - mu) * lax.rsqrt(var + EPS)
    ret_out = from_heads(o).astype(h.dtype) * p['ret_gn_g'] * jax.nn.silu(rg)

    q = to_heads(gq, GLA_HEADS) * (GLA_DK ** -0.5)
    k = to_heads(gk, GLA_HEADS)
    v = to_heads(gv, GLA_HEADS)
    la_f = to_heads(jax.nn.log_sigmoid((ga_f @ p['gla_wa2'][0] + p['gla_ba'][0]).astype(jnp.float32)) / GLA_TAU, GLA_HEADS)
    la_b = to_heads(jax.nn.log_sigmoid((ga_b @ p['gla_wa2'][1] + p['gla_ba'][1]).astype(jnp.float32)) / GLA_TAU, GLA_HEADS)
    o_f, gla_sf = gla_scan(q, k, v, la_f, gla0[:, 0])
    o_b, gla_sb = gla_scan(flip(q), flip(k), flip(v), flip(la_b), gla0[:, 1])
    o = o_f + flip(o_b)
    o = o * lax.rsqrt(jnp.mean(o * o, axis=-1, keepdims=True) + EPS)
    gla_out = from_heads(o).astype(h.dtype) * p['gla_norm_g'] * jax.nn.silu(gr)

    cq_n = rmsnorm(cq, p['mla_qa_g'])
    q = (cq_n @ p['mla_wuq']).reshape(B, S, MLA_HEADS, MLA_NOPE + MLA_ROPE)
    q = rmsnorm(q, p['mla_qn_g']).transpose(0, 2, 1, 3)
    ckv = rmsnorm(ckv_raw, p['mla_kva_g'])
    k, v = mla_kv(ckv, krope, p['mla_wukv'], p['mla_kn_g'])
    if ctx is None:
        new_ctx = (ckv, krope, jnp.stack([ret_sf, ret_sb], axis=1), jnp.stack([gla_sf, gla_sb], axis=1))
    else:
        ang_r, ang_c = axial_angles(S)
        q = apply_axial_rope(q, ang_r, ang_c)
        k = apply_axial_rope(k, ang_r, ang_c)
        k_c, v_c = mla_kv(ctx_ckv, ctx_krope, p['mla_wukv'], p['mla_kn_g'])
        k = jnp.concatenate([k_c, k], axis=2)
        v = jnp.concatenate([v_c, v], axis=2)
        new_ctx = None
    o = block_attention(q, k, v, (MLA_NOPE + MLA_ROPE) ** -0.5)
    mla_out = from_heads(o)

    branches = jnp.stack([ret_out, gla_out, mla_out], axis=2)
    proj = jnp.einsum('bsnc,ncd->bsnd', branches, p['w_branch'])
    gates = jax.nn.sigmoid(jnp.stack([m_ret, m_gla, m_mla], axis=2))
    out = jnp.sum(gates * proj, axis=2) @ p['w_out']
    return out, new_ctx


def moe(h, router_w, router_b, w_gu, b_gu, w_down, b_down):
    T, D = h.shape
    logits = h.astype(jnp.float32) @ router_w.astype(jnp.float32) + router_b.astype(jnp.float32)
    top_val, top_idx = lax.top_k(logits, TOP_K)
    gate_w = jax.nn.softmax(top_val, axis=-1)
    tk = T * TOP_K
    n_blocks = (tk + MOE_BLOCK - 1) // MOE_BLOCK + N_EXPERTS
    rows = n_blocks * MOE_BLOCK
    flat_e = top_idx.reshape(tk).astype(jnp.int32)
    flat_tok = jnp.arange(tk, dtype=jnp.int32) // TOP_K
    order = jnp.argsort(flat_e)
    sorted_e = flat_e[order]
    counts = jnp.zeros((N_EXPERTS,), jnp.int32).at[flat_e].add(1)
    padded = (counts + MOE_BLOCK - 1) // MOE_BLOCK * MOE_BLOCK
    pad_end = jnp.cumsum(padded)
    pad_start = pad_end - padded
    grp_start = jnp.cumsum(counts) - counts
    dest = pad_start[sorted_e] + jnp.arange(tk, dtype=jnp.int32) - grp_start[sorted_e]
    row_tok = jnp.full((rows,), T, jnp.int32).at[dest].set(flat_tok[order])
    row_w = jnp.zeros((rows,), h.dtype).at[dest].set(gate_w.reshape(tk)[order].astype(h.dtype))
    block_e = jnp.minimum(jnp.searchsorted(pad_end, jnp.arange(n_blocks, dtype=jnp.int32) * MOE_BLOCK, side='right'),
                          N_EXPERTS - 1)
    xs = jnp.concatenate([h, jnp.zeros((1, D), h.dtype)], axis=0)[row_tok].reshape(n_blocks, MOE_BLOCK, D)

    def expert_block(args):
        xb, e = args
        gu = xb @ w_gu[e] + b_gu[e]
        gate, up = jnp.split(gu, 2, axis=-1)
        gate = jnp.minimum(gate, SWIGLU_LIMIT)
        up = jnp.clip(up, -SWIGLU_LIMIT, SWIGLU_LIMIT)
        act = (up + 1.0) * gate * jax.nn.sigmoid(SWIGLU_ALPHA * gate)
        return act @ w_down[e] + b_down[e]

    out = lax.map(expert_block, (xs, block_e)).reshape(rows, D) * row_w[:, None]
    return jnp.zeros((T + 1, D), out.dtype).at[row_tok].add(out)[:T]


def layer(x, mod, p, ctx):
    sh1, sc1, g1, sh2, sc2, g2 = jnp.split(mod, 6, axis=-1)
    h = rmsnorm(x, p['norm1_g']) * (1.0 + sc1) + sh1
    mix, new_ctx = token_mixers(h, p, ctx)
    x = x + g1 * mix
    h = rmsnorm(x, p['norm2_g']) * (1.0 + sc2) + sh2
    B, S, D = x.shape
    ff = moe(h.reshape(B * S, D), p['router_w'], p['router_b'], p['moe_w_gu'], p['moe_b_gu'],
             p['moe_w_down'], p['moe_b_down']).reshape(B, S, D)
    x = x + g2 * ff.astype(x.dtype)
    return x, new_ctx


def setup_inputs(seed: int = 0) -> dict:
    key = jax.random.key(seed)
    ks = iter(jax.random.split(key, 40))
    nrm = lambda shape, s: jax.random.normal(next(ks), shape, jnp.float32) * s
    gain = lambda shape: 1.0 + 0.1 * jax.random.normal(next(ks), shape, jnp.float32)
    L = DEPTH
    return {
        'x_prompt': nrm((BATCH, SEQ, D_MODEL), 1.0),
        'x_sample': nrm((DEC_BATCH, DEC_SEQ, D_MODEL), 1.0),
        'cache_mla_ckv': nrm((DEC_BATCH, L, PAST_LEN, MLA_KV_LORA), 1.0),
        'cache_mla_krope': nrm((DEC_BATCH, L, PAST_LEN, MLA_ROPE), 1.0),
        'state_ret': nrm((DEC_BATCH, L, 2, RET_HEADS, RET_DK, RET_DV), 1.0),
        'state_gla': nrm((DEC_BATCH, L, 2, GLA_HEADS, GLA_DK, GLA_DV), 1.0),
        'c': nrm((DEC_BATCH, D_MODEL), 1.0),
        'c_ctx': nrm((D_MODEL,), 1.0),
        'w_mod': nrm((L, D_MODEL, 6 * D_MODEL), 0.5 * D_MODEL ** -0.5),
        'b_mod': nrm((L, 6 * D_MODEL), 0.01),
        'norm1_g': gain((L, D_MODEL)),
        'norm2_g': gain((L, D_MODEL)),
        'w_in': nrm((L, D_MODEL, D_IN), D_MODEL ** -0.5),
        'ret_gn_g': gain((L, RET_HEADS * RET_DV)),
        'gla_wa2': nrm((L, 2, GLA_RANK, GLA_HEADS * GLA_DK), GLA_RANK ** -0.5),
        'gla_ba': nrm((L, 2, GLA_HEADS * GLA_DK), 0.1),
        'gla_norm_g': gain((L, GLA_HEADS * GLA_DV)),
        'mla_qa_g': gain((L, MLA_Q_LORA)),
        'mla_wuq': nrm((L, MLA_Q_LORA, MLA_HEADS * (MLA_NOPE + MLA_ROPE)), MLA_Q_LORA ** -0.5),
        'mla_kva_g': gain((L, MLA_KV_LORA)),
        'mla_wukv': nrm((L, MLA_KV_LORA, MLA_HEADS * (MLA_NOPE + MLA_V)), MLA_KV_LORA ** -0.5),
        'mla_qn_g': gain((L, MLA_NOPE + MLA_ROPE)),
        'mla_kn_g': gain((L, MLA_NOPE + MLA_ROPE)),
        'w_branch': nrm((L, N_BRANCH, BRANCH_W, D_MODEL), BRANCH_W ** -0.5),
        'w_out': nrm((L, D_MODEL, D_MODEL), D_MODEL ** -0.5),
        'router_w': nrm((L, D_MODEL, N_EXPERTS), D_MODEL ** -0.5),
        'router_b': nrm((L, N_EXPERTS), 0.01),
        'moe_w_gu': nrm((L, N_EXPERTS, D_MODEL, 2 * D_EXPERT), D_MODEL ** -0.5),
        'moe_b_gu': nrm((L, N_EXPERTS, 2 * D_EXPERT), 0.01),
        'moe_w_down': nrm((L, N_EXPERTS, D_EXPERT, D_MODEL), D_EXPERT ** -0.5),
        'moe_b_down': nrm((L, N_EXPERTS, D_MODEL), 0.01),
    }


def reference(x_prompt, x_sample, cache_mla_ckv, cache_mla_krope, state_ret, state_gla, c, c_ctx,
              w_mod, b_mod, norm1_g, norm2_g, w_in, ret_gn_g, gla_wa2, gla_ba, gla_norm_g,
              mla_qa_g, mla_wuq, mla_kva_g, mla_wukv, mla_qn_g, mla_kn_g, w_branch, w_out,
              router_w, router_b, moe_w_gu, moe_b_gu, moe_w_down, moe_b_down):
    y_p = x_prompt
    y_s = x_sample
    ckv_list, krope_list, ret_list, gla_list = [], [], [], []
    for l in range(DEPTH):
        p = {
            'norm1_g': norm1_g[l], 'norm2_g': norm2_g[l], 'w_in': w_in[l],
            'ret_gn_g': ret_gn_g[l], 'gla_wa2': gla_wa2[l], 'gla_ba': gla_ba[l], 'gla_norm_g': gla_norm_g[l],
            'mla_qa_g': mla_qa_g[l], 'mla_wuq': mla_wuq[l], 'mla_kva_g': mla_kva_g[l],
            'mla_wukv': mla_wukv[l], 'mla_qn_g': mla_qn_g[l], 'mla_kn_g': mla_kn_g[l],
            'w_branch': w_branch[l], 'w_out': w_out[l],
            'router_w': router_w[l], 'router_b': router_b[l], 'moe_w_gu': moe_w_gu[l],
            'moe_b_gu': moe_b_gu[l], 'moe_w_down': moe_w_down[l], 'moe_b_down': moe_b_down[l],
        }
        mod_ctx = (jax.nn.silu(c_ctx) @ w_mod[l] + b_mod[l])[None, None, :]
        y_p, (ckv_l, krope_l, ret_l, gla_l) = layer(y_p, mod_ctx, p, None)
        ckv_list.append(ckv_l)
        krope_list.append(krope_l)
        ret_list.append(ret_l)
        gla_list.append(gla_l)
        mod_lat = (jax.nn.silu(c) @ w_mod[l] + b_mod[l])[:, None, :]
        ctx = (cache_mla_ckv[:, l], cache_mla_krope[:, l], state_ret[:, l], state_gla[:, l])
        y_s, _ = layer(y_s, mod_lat, p, ctx)
    new_mla_ckv = jnp.stack(ckv_list, axis=1)
    new_mla_krope = jnp.stack(krope_list, axis=1)
    new_state_ret = jnp.stack(ret_list, axis=1)
    new_state_gla = jnp.stack(gla_list, axis=1)
    return (y_p, y_s, new_mla_ckv, new_mla_krope, new_state_ret, new_state_gla)
```

```python
import functools

import jax
import jax.numpy as jnp
import numpy as np
from jax import lax
from jax.experimental import pallas as pl
from jax.experimental.pallas import tpu as pltpu

F32 = jnp.float32
BF16 = jnp.bfloat16

D_MODEL = 1024
DEPTH = 2
N_CTX_SEQ, CTX_LEN = 32, 256
N_LAT_SEQ, LAT_LEN = 4, 1024
T_CTX = N_CTX_SEQ * CTX_LEN
T_LAT = N_LAT_SEQ * LAT_LEN
T_ALL = T_CTX + T_LAT
TM = 256
N_TILES = T_ALL // TM
N_CTX_TILES = T_CTX // TM
LAT_TILES = LAT_LEN // TM
N_MOD = 8
EPS = 1e-6

N_HEADS = 4
DK, DV = 64, 128
GRID_W = 64
MLA_HEADS, MLA_NOPE, MLA_ROPE, MLA_V = 8, 64, 32, 64
MLA_QK = MLA_NOPE + MLA_ROPE
HEAD_PAD = 128
GLA_TAU = 16.0
N_EXPERTS, TOP_K, D_EXPERT = 32, 4, 1024
SWIGLU_LIMIT, SWIGLU_ALPHA = 7.0, 1.702
MOE_ROWS = 256
N_SLOTS = T_ALL * TOP_K
N_MOE_BLOCKS = N_SLOTS // MOE_ROWS + N_EXPERTS
N_MOE_ROWS = N_MOE_BLOCKS * MOE_ROWS

DZ = 6656
IN_TILE = 512
VMEM_LIMIT = 56 * 1024 * 1024

RET_LOG_F = [float(np.log1p(-np.exp2(-(5.0 + h)))) for h in range(N_HEADS)]
RET_LOG_B = [float(np.log1p(-np.exp2(-(5.5 + h)))) for h in range(N_HEADS)]


def _params(n_axes, vmem=VMEM_LIMIT):
    return pltpu.CompilerParams(dimension_semantics=("arbitrary",) * n_axes, vmem_limit_bytes=vmem)


def _sigmoid(x):
    return 1.0 / (1.0 + jnp.exp(-x))


def _dot(a, b):
    return jnp.dot(a, b, preferred_element_type=F32)


def _dot_t(a, b):
    return lax.dot_general(a, b, (((1,), (1,)), ((), ())), preferred_element_type=F32)


def _mod_row(tile_rows):
    def f(i):
        r0 = i * tile_rows
        return jnp.where(r0 < T_CTX, 0, 1 + (r0 - T_CTX) // LAT_LEN)
    return f


def _mod_kernel(c_ref, w_ref, b_ref, o_ref):
    c = c_ref[...]
    s = c * _sigmoid(c)
    o_ref[0] = jnp.dot(s, w_ref[0], preferred_element_type=F32, precision=lax.Precision.HIGHEST) + b_ref[0]


def _modulation(cc, w_mod, b_mod):
    n = 6 * D_MODEL
    blk = 1024
    return pl.pallas_call(
        _mod_kernel,
        out_shape=jax.ShapeDtypeStruct((DEPTH, N_MOD, n), F32),
        grid=(DEPTH, n // blk),
        in_specs=[pl.BlockSpec((N_MOD, D_MODEL), lambda l, j: (0, 0)),
                  pl.BlockSpec((1, D_MODEL, blk), lambda l, j: (l, 0, j)),
                  pl.BlockSpec((1, 1, blk), lambda l, j: (l, 0, j))],
        out_specs=pl.BlockSpec((1, N_MOD, blk), lambda l, j: (l, 0, j)),
        compiler_params=_params(2),
        name="modulation",
    )(cc, w_mod, b_mod.reshape(DEPTH, 1, n))


def _in_kernel(x_ref, g_ref, sh_ref, sc_ref, w_ref, o_ref):
    x = x_ref[...]
    h = x * lax.rsqrt(jnp.mean(x * x, axis=-1, keepdims=True) + EPS) * g_ref[...]
    h = h * (1.0 + sc_ref[0]) + sh_ref[0]
    hb = h.astype(BF16)
    for n0 in range(0, DZ, 512):
        o_ref[:, n0:n0 + 512] = _dot(hb, w_ref[:, n0:n0 + 512]).astype(BF16)


def _in_proj(x, g, modr, w_in_p, layer):
    mrow = _mod_row(IN_TILE)
    base = layer * N_MOD

    def mod_spec(part):
        return pl.BlockSpec((1, 1, D_MODEL), lambda i: ((base + mrow(i)) * 6 + part, 0, 0))

    return pl.pallas_call(
        _in_kernel,
        out_shape=jax.ShapeDtypeStruct((T_ALL, DZ), BF16),
        grid=(T_ALL // IN_TILE,),
        in_specs=[pl.BlockSpec((IN_TILE, D_MODEL), lambda i: (i, 0)),
                  pl.BlockSpec((1, D_MODEL), lambda i: (0, 0)),
                  mod_spec(0), mod_spec(1),
                  pl.BlockSpec((D_MODEL, DZ), lambda i: (0, 0))],
        out_specs=pl.BlockSpec((IN_TILE, DZ), lambda i: (i, 0)),
        compiler_params=_params(1),
        name="in_proj",
    )(x, g, modr, modr, w_in_p)


def _lane_half_mask(hh):
    lane = lax.broadcasted_iota(jnp.int32, (1, 128), 1)
    return (lane < 64) if hh == 0 else (lane >= 64)


def _ret_kernel(*refs, seq, has_state, emit_state):
    q_ref, k_ref, v_ref, g_ref, gn_ref = refs[:5]
    pos = 5
    if has_state:
        s0_ref = refs[pos]
        pos += 1
    o_ref = refs[pos]
    pos += 1
    if emit_state:
        st_ref = refs[pos]

    r0 = pl.multiple_of(pl.program_id(1) * TM, TM)
    qb = q_ref[pl.ds(r0, TM), :]
    rowi = lax.broadcasted_iota(jnp.int32, (TM, seq), 0) + r0
    colj = lax.broadcasted_iota(jnp.int32, (TM, seq), 1)
    diff = (rowi - colj).astype(F32)
    on_diag = rowi == colj
    ri = (lax.broadcasted_iota(jnp.int32, (TM, 1), 0) + r0).astype(F32)

    for h in range(N_HEADS):
        p, hh = h // 2, h % 2
        lanes = slice(128 * p, 128 * p + 128)
        qp = qb[:, lanes]
        qh = jnp.where(_lane_half_mask(hh), qp, jnp.zeros_like(qp))
        sc = _dot_t(qh, k_ref[:, lanes])
        ex = jnp.where(diff > 0, RET_LOG_F[h] * diff, -RET_LOG_B[h] * diff)
        dec = jnp.exp(ex) * jnp.where(on_diag, 2.0 * DK ** -0.5, DK ** -0.5)
        o = _dot((sc * dec).astype(BF16), v_ref[:, 128 * h:128 * h + 128])
        if has_state:
            qf = qh.astype(F32)
            o += _dot((qf * jnp.exp(RET_LOG_F[h] * (ri + 1.0))).astype(BF16), s0_ref[0, 0, p].astype(BF16))
            o += _dot((qf * jnp.exp(RET_LOG_B[h] * (seq - ri))).astype(BF16), s0_ref[0, 1, p].astype(BF16))
        mu = jnp.mean(o, axis=-1, keepdims=True)
        d = o - mu
        var = jnp.mean(d * d, axis=-1, keepdims=True)
        on = d * lax.rsqrt(var + EPS)
        g = g_ref[:, 128 * h:128 * h + 128].astype(F32)
        out = on * gn_ref[:, 128 * h:128 * h + 128] * (g * _sigmoid(g))
        o_ref[:, 128 * h:128 * h + 128] = out.astype(BF16)

    if emit_state:
        jc = lax.broadcasted_iota(jnp.int32, (seq, 1), 0).astype(F32)
        lane = lax.broadcasted_iota(jnp.int32, (1, 128), 1)
        for p in range(2):
            kp = k_ref[:, 128 * p:128 * p + 128].astype(F32) * DK ** -0.5
            lgf = jnp.where(lane < 64, RET_LOG_F[2 * p], RET_LOG_F[2 * p + 1])
            lgb = jnp.where(lane < 64, RET_LOG_B[2 * p], RET_LOG_B[2 * p + 1])
            kdf = (kp * jnp.exp(lgf * (seq - 1.0 - jc))).T.astype(BF16)
            kdb = (kp * jnp.exp(lgb * jc)).T.astype(BF16)
            for hh in range(2):
                h = 2 * p + hh
                vh = v_ref[:, 128 * h:128 * h + 128]
                st_ref[0, 0, h] = _dot(kdf, vh)[64 * hh:64 * hh + 64, :]
                st_ref[0, 1, h] = _dot(kdb, vh)[64 * hh:64 * hh + 64, :]


def _retention(z, gn_g, s0, *, ctx):
    if ctx:
        nb, seq, row_blk, tile0 = N_CTX_SEQ, CTX_LEN, 0, 0
    else:
        nb, seq, row_blk, tile0 = N_LAT_SEQ, LAT_LEN, T_CTX // LAT_LEN, N_CTX_TILES
    nq = seq // TM
    in_specs = [pl.BlockSpec((seq, 256), lambda b, i: (row_blk + b, 0)),
                pl.BlockSpec((seq, 256), lambda b, i: (row_blk + b, 1)),
                pl.BlockSpec((seq, 512), lambda b, i: (row_blk + b, 1)),
                pl.BlockSpec((TM, 512), lambda b, i: (tile0 + b * nq + i, 2)),
                pl.BlockSpec((1, 512), lambda b, i: (0, 0))]
    args = [z, z, z, z, gn_g]
    out_shape = [jax.ShapeDtypeStruct((nb * seq, 512), BF16)]
    out_specs = [pl.BlockSpec((TM, 512), lambda b, i: (b * nq + i, 0))]
    if not ctx:
        in_specs.append(pl.BlockSpec((1, 2, 2, 128, 128), lambda b, i: (b, 0, 0, 0, 0)))
        args.append(s0.reshape(N_LAT_SEQ, 2, 2, 128, 128))
    else:
        out_shape.append(jax.ShapeDtypeStruct((nb, 2, N_HEADS, DK, DV), F32))
        out_specs.append(pl.BlockSpec((1, 2, N_HEADS, DK, DV), lambda b, i: (b, 0, 0, 0, 0)))
    return pl.pallas_call(
        functools.partial(_ret_kernel, seq=seq, has_state=not ctx, emit_state=ctx),
        out_shape=out_shape, grid=(nb, nq), in_specs=in_specs, out_specs=out_specs,
        compiler_params=_params(2),
        name="retention_ctx" if ctx else "retention_lat",
    )(*args)


def _gla_decay(small_ref, wa_ref, ba_ref):
    x = _dot(small_ref[...], wa_ref[...]) + ba_ref[...]
    la = -(jnp.maximum(-x, 0.0) + jnp.log(1.0 + jnp.exp(-jnp.abs(x)))) * (1.0 / GLA_TAU)
    ri = lax.broadcasted_iota(jnp.int32, (TM, TM), 0)
    ci = lax.broadcasted_iota(jnp.int32, (TM, TM), 1)
    ltri = jnp.where(ri >= ci, 1.0, 0.0).astype(BF16)
    hi = la.astype(BF16)
    r1 = la - hi.astype(F32)
    mid = r1.astype(BF16)
    lo = (r1 - mid.astype(F32)).astype(BF16)
    cum = _dot(ltri, hi) + _dot(ltri, mid) + _dot(ltri, lo)
    return la, cum


def _gla_state_kernel(k_ref, v_ref, small_ref, wa_ref, ba_ref, kv_ref, tot_ref):
    la, cum = _gla_decay(small_ref, wa_ref, ba_ref)
    bf, bb = cum[:, :256], cum[:, 256:]
    xb = bb - la[:, 256:]
    k = k_ref[...].astype(F32)
    kdf = k * jnp.exp(bf[TM - 1:TM, :] - bf)
    kdb = k * jnp.exp(xb)
    for p in range(2):
        kf_t = kdf[:, 128 * p:128 * p + 128].T.astype(BF16)
        kb_t = kdb[:, 128 * p:128 * p + 128].T.astype(BF16)
        for hh in range(2):
            h = 2 * p + hh
            vh = v_ref[:, 128 * h:128 * h + 128]
            kv_ref[0, 0, 0, h] = _dot(kf_t, vh)[64 * hh:64 * hh + 64, :]
            kv_ref[0, 0, 1, h] = _dot(kb_t, vh)[64 * hh:64 * hh + 64, :]
    tot_ref[0, 0] = jnp.sum(la.T, axis=-1, keepdims=True)


def _mid_bcast(x, s, r):
    w = 2 * s
    if w >= 8:
        n = TM // w
        x3 = x.reshape(n, w, 256)
        return jnp.broadcast_to(x3[:, r:r + 1, :], (n, w, 256)).reshape(TM, 256)
    x3 = x.reshape(TM // 8, 8, 256)
    sub = lax.broadcasted_iota(jnp.int32, (1, 8, 1), 1)
    out = None
    for blk in range(8 // w):
        rowv = jnp.broadcast_to(x3[:, blk * w + r:blk * w + r + 1, :], (TM // 8, 8, 256))
        out = rowv if out is None else jnp.where(sub >= blk * w, rowv, out)
    return out.reshape(TM, 256)


def _gla_kernel(*refs, n_blk, has_state):
    q_ref, k_ref, v_ref, g_ref, small_ref, wa_ref, ba_ref, gn_ref = refs[:8]
    pos = 8
    if has_state:
        kv_ref, tot_ref, s0_ref = refs[pos:pos + 3]
        pos += 3
    o_ref = refs[pos]

    la, cum = _gla_decay(small_ref, wa_ref, ba_ref)
    bf, bb = cum[:, :256], cum[:, 256:]
    xb = bb - la[:, 256:]
    q = q_ref[...].astype(F32) * DK ** -0.5
    k = k_ref[...].astype(F32)
    row = lax.broadcasted_iota(jnp.int32, (TM, 1), 0)
    rowi = lax.broadcasted_iota(jnp.int32, (TM, TM), 0)
    colj = lax.broadcasted_iota(jnp.int32, (TM, TM), 1)
    masks = [_lane_half_mask(0), _lane_half_mask(1)]

    def head_q(x, h):
        xp = x[:, 128 * (h // 2):128 * (h // 2) + 128]
        return jnp.where(masks[h % 2], xp, jnp.zeros_like(xp))

    qb16, kb16 = q.astype(BF16), k.astype(BF16)
    acc = []
    for h in range(N_HEADS):
        p = h // 2
        sd = _dot_t(head_q(qb16, h), kb16[:, 128 * p:128 * p + 128])
        acc.append(jnp.where(rowi == colj, 2.0 * sd, 0.0))

    s = 1
    while s < TM:
        upper = ((row // s) % 2) == 1
        mf = _mid_bcast(bf, s, s - 1)
        mb = _mid_bcast(xb, s, s)
        ef = jnp.exp(jnp.where(upper, bf - mf, mf - bf))
        eb = jnp.exp(jnp.where(upper, xb - mb, mb - xb))
        qf = jnp.where(upper, q * ef, 0.0).astype(BF16)
        kf = jnp.where(upper, 0.0, k * ef).astype(BF16)
        qbk = jnp.where(upper, 0.0, q * eb).astype(BF16)
        kbk = jnp.where(upper, k * eb, 0.0).astype(BF16)
        same = (rowi // (2 * s)) == (colj // (2 * s))
        for h in range(N_HEADS):
            p = h // 2
            lanes = slice(128 * p, 128 * p + 128)
            sl = _dot_t(head_q(qf, h), kf[:, lanes]) + _dot_t(head_q(qbk, h), kbk[:, lanes])
            acc[h] = acc[h] + jnp.where(same, sl, 0.0)
        s *= 2

    if has_state:
        n = pl.program_id(1)
        qsf = (q * jnp.exp(bf)).astype(BF16)
        qsb = (q * jnp.exp(bb[TM - 1:TM, :] - xb)).astype(BF16)

    for h in range(N_HEADS):
        o = _dot(acc[h].astype(BF16), v_ref[:, 128 * h:128 * h + 128])
        if has_state:
            sf = s0_ref[0, 0, h]
            for m in range(n_blk - 1):
                dec = jnp.exp(tot_ref[0, m, 64 * h:64 * h + 64, :])
                sf = jnp.where(m < n, dec * sf + kv_ref[0, m, 0, h], sf)
            sb = s0_ref[0, 1, h]
            for m in range(n_blk - 1, 0, -1):
                dec = jnp.exp(tot_ref[0, m, 256 + 64 * h:256 + 64 * h + 64, :])
                sb = jnp.where(m > n, dec * sb + kv_ref[0, m, 1, h], sb)
            zero = jnp.zeros((DK, DV), F32)
            hh = h % 2
            sf2 = jnp.concatenate([sf, zero] if hh == 0 else [zero, sf], axis=0).astype(BF16)
            sb2 = jnp.concatenate([sb, zero] if hh == 0 else [zero, sb], axis=0).astype(BF16)
            lanes = slice(128 * (h // 2), 128 * (h // 2) + 128)
            o += _dot(qsf[:, lanes], sf2) + _dot(qsb[:, lanes], sb2)
        on = o * lax.rsqrt(jnp.mean(o * o, axis=-1, keepdims=True) + EPS)
        g = g_ref[:, 128 * h:128 * h + 128].astype(F32)
        out = on * gn_ref[:, 128 * h:128 * h + 128] * (g * _sigmoid(g))
        o_ref[:, 128 * h:128 * h + 128] = out.astype(BF16)


def _gla(z, wa_p, ba_p, gn_g, s0, *, ctx):
    if ctx:
        nb, n_blk, tile0 = N_CTX_SEQ, 1, 0
    else:
        nb, n_blk, tile0 = N_LAT_SEQ, LAT_TILES, N_CTX_TILES

    def zspec(width, col):
        return pl.BlockSpec((TM, width), lambda b, n: (tile0 + b * n_blk + n, col))

    w_specs = [pl.BlockSpec((512, 512), lambda b, n: (0, 0)), pl.BlockSpec((1, 512), lambda b, n: (0, 0))]
    kv, tot = pl.pallas_call(
        _gla_state_kernel,
        out_shape=[jax.ShapeDtypeStruct((nb, n_blk, 2, N_HEADS, DK, DV), F32),
                   jax.ShapeDtypeStruct((nb, n_blk, 512, 1), F32)],
        grid=(nb, n_blk),
        in_specs=[zspec(256, 7), zspec(512, 4), zspec(512, 12)] + w_specs,
        out_specs=[pl.BlockSpec((1, 1, 2, N_HEADS, DK, DV), lambda b, n: (b, n, 0, 0, 0, 0)),
                   pl.BlockSpec((1, 1, 512, 1), lambda b, n: (b, n, 0, 0))],
        compiler_params=_params(2),
        name="gla_state_ctx" if ctx else "gla_state_lat",
    )(z, z, z, wa_p, ba_p)

    in_specs = [zspec(256, 6), zspec(256, 7), zspec(512, 4), zspec(512, 5), zspec(512, 12)] + w_specs
    in_specs.append(pl.BlockSpec((1, 512), lambda b, n: (0, 0)))
    args = [z, z, z, z, z, wa_p, ba_p, gn_g]
    if not ctx:
        in_specs += [pl.BlockSpec((1, n_blk, 2, N_HEADS, DK, DV), lambda b, n: (b, 0, 0, 0, 0, 0)),
                     pl.BlockSpec((1, n_blk, 512, 1), lambda b, n: (b, 0, 0, 0)),
                     pl.BlockSpec((1, 2, N_HEADS, DK, DV), lambda b, n: (b, 0, 0, 0, 0))]
        args += [kv, tot, s0]
    out = pl.pallas_call(
        functools.partial(_gla_kernel, n_blk=n_blk, has_state=not ctx),
        out_shape=jax.ShapeDtypeStruct((nb * n_blk * TM, 512), BF16),
        grid=(nb, n_blk), in_specs=in_specs,
        out_specs=pl.BlockSpec((TM, 512), lambda b, n: (b * n_blk + n, 0)),
        compiler_params=_params(2),
        name="gla_ctx" if ctx else "gla_lat",
    )(*args)
    return out, kv


def _rope_tables():
    nf = MLA_ROPE // 4
    pos = np.arange(LAT_LEN)
    freqs = (10000.0 ** (-np.arange(nf, dtype=np.float32) / nf)).astype(np.float32)
    ang_r = ((pos // GRID_W).astype(np.float32)[:, None] * freqs).astype(np.float32)
    ang_c = ((pos % GRID_W).astype(np.float32)[:, None] * freqs).astype(np.float32)
    cos = np.ones((TM + LAT_LEN, HEAD_PAD), np.float32)
    sa = np.zeros((TM + LAT_LEN, HEAD_PAD), np.float32)
    sb = np.zeros((TM + LAT_LEN, HEAD_PAD), np.float32)
    o = MLA_NOPE
    for base, ang in ((o, ang_r), (o + 2 * nf, ang_c)):
        cos[TM:, base:base + nf] = np.cos(ang)
        cos[TM:, base + nf:base + 2 * nf] = np.cos(ang)
        sa[TM:, base:base + nf] = -np.sin(ang)
        sb[TM:, base + nf:base + 2 * nf] = np.sin(ang)
    return jnp.asarray(cos), jnp.asarray(sa), jnp.asarray(sb)


def _rope(x, cos, sa, sb):
    return x * cos + pltpu.roll(x, 128 - 8, 1) * sa + pltpu.roll(x, 8, 1) * sb


def _head_norm(x, g):
    return x * lax.rsqrt(jnp.sum(x * x, axis=-1, keepdims=True) * (1.0 / MLA_QK) + EPS) * g


def _mla_keys(ckv, kr_tile, wk_ref, wv_ref, kn_ref, k_ref, v_ref, rope=None):
    cb = ckv.astype(BF16)
    kpre = _dot(cb, wk_ref[...])
    v_ref[...] = _dot(cb, wv_ref[...]).astype(BF16)
    for h in range(MLA_HEADS):
        kh = _head_norm(kpre[:, 128 * h:128 * h + 128] + kr_tile, kn_ref[...])
        if rope is not None:
            kh = _rope(kh, *rope)
        k_ref[:, 128 * h:128 * h + 128] = kh.astype(BF16)


def _mla_prep_kernel(small_ref, qa_ref, wuq_ref, qn_ref, kva_ref, wk_ref, wv_ref, kn_ref,
                     cos_ref, sa_ref, sb_ref, q_ref, k_ref, v_ref, ckv_ref, kr_ref):
    sm = small_ref[...].astype(F32)
    cq, ckv_raw, g3 = sm[:, 0:256], sm[:, 256:384], sm[:, 384:512]
    rope = (cos_ref[...], sa_ref[...], sb_ref[...])
    cqn = cq * lax.rsqrt(jnp.mean(cq * cq, axis=-1, keepdims=True) + EPS) * qa_ref[...]
    q = _dot(cqn.astype(BF16), wuq_ref[...])
    scale = MLA_QK ** -0.5
    for h in range(MLA_HEADS):
        qh = _rope(_head_norm(q[:, 128 * h:128 * h + 128], qn_ref[...]), *rope)
        q_ref[:, 128 * h:128 * h + 128] = (qh * scale).astype(BF16)
    ckv = ckv_raw * lax.rsqrt(jnp.mean(ckv_raw * ckv_raw, axis=-1, keepdims=True) + EPS) * kva_ref[...]
    ckv_ref[...] = ckv
    lane = lax.broadcasted_iota(jnp.int32, (1, 128), 1)
    kr = jnp.where(lane < MLA_ROPE, g3, 0.0)
    kr_ref[...] = kr
    _mla_keys(ckv, pltpu.roll(kr, MLA_NOPE, 1), wk_ref, wv_ref, kn_ref, k_ref, v_ref, rope)


def _mla_cache_kernel(ckv_ref, kr_ref, wk_ref, wv_ref, kn_ref, k_ref, v_ref):
    _mla_keys(ckv_ref[...], pltpu.roll(kr_ref[...], MLA_NOPE, 1), wk_ref, wv_ref, kn_ref, k_ref, v_ref)


def _mla_prep(z, w, rope_tabs):
    def rope_blk(i):
        return jnp.where(i < N_CTX_TILES, 0, 1 + (i - N_CTX_TILES) % LAT_TILES)

    const = lambda shape: pl.BlockSpec(shape, lambda i: (0,) * len(shape))
    rope_spec = pl.BlockSpec((TM, HEAD_PAD), lambda i: (rope_blk(i), 0))
    row = lambda width: pl.BlockSpec((TM, width), lambda i: (i, 0))
    return pl.pallas_call(
        _mla_prep_kernel,
        out_shape=[jax.ShapeDtypeStruct((T_ALL, 1024), BF16), jax.ShapeDtypeStruct((T_ALL, 1024), BF16),
                   jax.ShapeDtypeStruct((T_ALL, 512), BF16), jax.ShapeDtypeStruct((T_ALL, 128), F32),
                   jax.ShapeDtypeStruct((T_ALL, 128), F32)],
        grid=(N_TILES,),
        in_specs=[pl.BlockSpec((TM, 512), lambda i: (i, 12)),
                  const((1, 256)), const((256, 1024)), const((1, 128)), const((1, 128)),
                  const((128, 1024)), const((128, 512)), const((1, 128)),
                  rope_spec, rope_spec, rope_spec],
        out_specs=[row(1024), row(1024), row(512), row(128), row(128)],
        compiler_params=_params(1),
        name="mla_prep",
    )(z, w["qa_g"], w["wuq"], w["qn_g"], w["kva_g"], w["wk"], w["wv"], w["kn_g"], *rope_tabs)


def _mla_cache(ckv, kr_pad, w):
    const = lambda shape: pl.BlockSpec(shape, lambda i: (0,) * len(shape))
    row = lambda width: pl.BlockSpec((TM, width), lambda i: (i, 0))
    n = ckv.shape[0]
    return pl.pallas_call(
        _mla_cache_kernel,
        out_shape=[jax.ShapeDtypeStruct((n, 1024), BF16), jax.ShapeDtypeStruct((n, 512), BF16)],
        grid=(n // TM,),
        in_specs=[row(128), row(128), const((128, 1024)), const((128, 512)), const((1, 128))],
        out_specs=[row(1024), row(512)],
        compiler_params=_params(1),
        name="mla_cache",
    )(ckv, kr_pad, w["wk"], w["wv"], w["kn_g"])


def _mla_attn_kernel(*refs, has_cache):
    q_ref, k_ref, v_ref = refs[:3]
    pos = 3
    if has_cache:
        kc_ref, vc_ref = refs[3:5]
        pos = 5
    o_ref = refs[pos]
    for p in range(MLA_HEADS // 2):
        acc = jnp.zeros((TM, 128), F32)
        for hh in range(2):
            h = 2 * p + hh
            lanes = slice(128 * h, 128 * h + 128)
            qh = q_ref[:, lanes]
            l1 = _dot_t(qh, k_ref[:, lanes])
            m = jnp.max(l1, axis=-1, keepdims=True)
            if has_cache:
                l0 = _dot_t(qh, kc_ref[:, lanes])
                m = jnp.maximum(m, jnp.max(l0, axis=-1, keepdims=True))
                p0 = jnp.exp(l0 - m)
            p1 = jnp.exp(l1 - m)
            den = jnp.sum(p1, axis=-1, keepdims=True)
            if has_cache:
                den = den + jnp.sum(p0, axis=-1, keepdims=True)
            inv = 1.0 / den
            mask = _lane_half_mask(hh)
            vp = v_ref[:, 128 * p:128 * p + 128]
            acc += _dot((p1 * inv).astype(BF16), jnp.where(mask, vp, jnp.zeros_like(vp)))
            if has_cache:
                vcp = vc_ref[:, 128 * p:128 * p + 128]
                acc += _dot((p0 * inv).astype(BF16), jnp.where(mask, vcp, jnp.zeros_like(vcp)))
        o_ref[:, 128 * p:128 * p + 128] = acc.astype(BF16)


def _mla_attn(q, k, v, kc, vc, *, ctx):
    if ctx:
        nb, seq, row_blk, tile0 = N_CTX_SEQ, CTX_LEN, 0, 0
    else:
        nb, seq, row_blk, tile0 = N_LAT_SEQ, LAT_LEN, T_CTX // LAT_LEN, N_CTX_TILES
    nq = seq // TM
    in_specs = [pl.BlockSpec((TM, 1024), lambda b, i: (tile0 + b * nq + i, 0)),
                pl.BlockSpec((seq, 1024), lambda b, i: (row_blk + b, 0)),
                pl.BlockSpec((seq, 512), lambda b, i: (row_blk + b, 0))]
    args = [q, k, v]
    if not ctx:
        in_specs += [pl.BlockSpec((TM, 1024), lambda b, i: (b, 0)), pl.BlockSpec((TM, 512), lambda b, i: (b, 0))]
        args += [kc, vc]
    return pl.pallas_call(
        functools.partial(_mla_attn_kernel, has_cache=not ctx),
        out_shape=jax.ShapeDtypeStruct((nb * seq, 512), BF16),
        grid=(nb, nq), in_specs=in_specs,
        out_specs=pl.BlockSpec((TM, 512), lambda b, i: (b * nq + i, 0)),
        compiler_params=_params(2),
        name="mla_attn_ctx" if ctx else "mla_attn_lat",
    )(*args)


def _merge_kernel(ret_ref, gla_ref, mla_ref, m0_ref, m1_ref, m2_ref, x_ref, wb_ref, wo_ref,
                  g1_ref, n2_ref, sh2_ref, sc2_ref, rwh_ref, rwl_ref, rb_ref,
                  x1_ref, h_ref, idx_ref, w_ref):
    mix = None
    for br_ref, m_ref, n in ((ret_ref, m0_ref, 0), (gla_ref, m1_ref, 1), (mla_ref, m2_ref, 2)):
        term = _sigmoid(m_ref[...].astype(F32)) * _dot(br_ref[...], wb_ref[n])
        mix = term if mix is None else mix + term
    out = _dot(mix.astype(BF16), wo_ref[...])
    x1 = x_ref[...] + g1_ref[0] * out
    x1_ref[...] = x1
    h = x1 * lax.rsqrt(jnp.mean(x1 * x1, axis=-1, keepdims=True) + EPS) * n2_ref[...]
    h = h * (1.0 + sc2_ref[0]) + sh2_ref[0]
    h_ref[...] = h
    hh = h.astype(BF16)
    hl = (h - hh.astype(F32)).astype(BF16)
    logits = _dot(hh, rwh_ref[...]) + _dot(hh, rwl_ref[...]) + _dot(hl, rwh_ref[...]) + rb_ref[...]
    lane = lax.broadcasted_iota(jnp.int32, (TM, 128), 1)
    lanef = lane.astype(F32)
    l = jnp.where(lane < N_EXPERTS, logits, -jnp.inf)
    vals, idxs = [], []
    for _ in range(TOP_K):
        m = jnp.max(l, axis=-1, keepdims=True)
        ix = jnp.min(jnp.where(l == m, lanef, 128.0), axis=-1, keepdims=True)
        vals.append(m)
        idxs.append(ix)
        l = jnp.where(lanef == ix, -jnp.inf, l)
    es = [jnp.exp(v - vals[0]) for v in vals]
    inv = 1.0 / (es[0] + es[1] + es[2] + es[3])
    idx_out = jnp.zeros((TM, 128), F32)
    w_out = jnp.zeros((TM, 128), F32)
    for kk in range(TOP_K):
        idx_out = jnp.where(lane == kk, idxs[kk], idx_out)
        w_out = jnp.where(lane == kk, es[kk] * inv, w_out)
    idx_ref[...] = idx_out.astype(jnp.int32)
    w_ref[...] = w_out


def _merge(ret_o, gla_o, mla_o, z, x, modr, w, layer):
    mrow = _mod_row(TM)
    base = layer * N_MOD

    def mod_spec(part):
        return pl.BlockSpec((1, 1, D_MODEL), lambda i: ((base + mrow(i)) * 6 + part, 0, 0))

    const = lambda shape: pl.BlockSpec(shape, lambda i: (0,) * len(shape))
    row = lambda width: pl.BlockSpec((TM, width), lambda i: (i, 0))
    gate = lambda col: pl.BlockSpec((TM, 1024), lambda i: (i, col))
    return pl.pallas_call(
        _merge_kernel,
        out_shape=[jax.ShapeDtypeStruct((T_ALL, D_MODEL), F32), jax.ShapeDtypeStruct((T_ALL, D_MODEL), F32),
                   jax.ShapeDtypeStruct((T_ALL, 128), jnp.int32), jax.ShapeDtypeStruct((T_ALL, 128), F32)],
        grid=(N_TILES,),
        in_specs=[row(512), row(512), row(512), gate(3), gate(4), gate(5), row(1024),
                  const((3, 512, 1024)), const((1024, 1024)),
                  mod_spec(2), const((1, 1024)), mod_spec(3), mod_spec(4),
                  const((1024, 128)), const((1024, 128)), const((1, 128))],
        out_specs=[row(1024), row(1024), row(128), row(128)],
        compiler_params=_params(1),
        name="merge",
    )(ret_o, gla_o, mla_o, z, z, z, x, w["wb"], w["wo"], modr, w["n2_g"], modr, modr,
      w["rw_hi"], w["rw_lo"], w["rb"])


def _route(top_idx, top_w):
    flat_e = top_idx.reshape(N_SLOTS)
    onehot = (flat_e[:, None] == jnp.arange(N_EXPERTS, dtype=jnp.int32)[None, :]).astype(jnp.int32)
    csum = jnp.cumsum(onehot, axis=0)
    counts = csum[-1]
    padded = (counts + MOE_ROWS - 1) // MOE_ROWS * MOE_ROWS
    pad_end = jnp.cumsum(padded)
    pad_start = pad_end - padded
    dest = jnp.sum(onehot * (csum - 1 + pad_start[None, :]), axis=1).astype(jnp.int32)
    flat_tok = jnp.arange(N_SLOTS, dtype=jnp.int32) // TOP_K
    row_tok = jnp.zeros((N_MOE_ROWS,), jnp.int32).at[dest].set(flat_tok)
    row_w = jnp.zeros((N_MOE_ROWS,), F32).at[dest].set(top_w.reshape(N_SLOTS))
    blk_start = jnp.arange(N_MOE_BLOCKS, dtype=jnp.int32) * MOE_ROWS
    block_e = jnp.minimum(jnp.sum((pad_end[None, :] <= blk_start[:, None]).astype(jnp.int32), axis=1),
                          N_EXPERTS - 1).astype(jnp.int32)
    n_used = (pad_end[-1] // MOE_ROWS).astype(jnp.int32).reshape(1)
    return dest, row_tok, row_w, block_e, n_used


def _gather_kernel(tok_ref, nb_ref, h_ref, xs_ref, sem):
    i = pl.program_id(0)
    base = i * MOE_ROWS

    def row_copy(r):
        return pltpu.make_async_copy(h_ref.at[pl.ds(tok_ref[base + r], 1), :],
                                     xs_ref.at[pl.ds(base + r, 1), :], sem)

    @pl.when(i < nb_ref[0])
    def _():
        def issue(r, c):
            row_copy(r).start()
            return c
        lax.fori_loop(0, MOE_ROWS, issue, 0)
        pltpu.make_async_copy(h_ref.at[pl.ds(0, MOE_ROWS), :], xs_ref.at[pl.ds(base, MOE_ROWS), :], sem).wait()

    @pl.when(i >= nb_ref[0])
    def _():
        fill = pltpu.make_async_copy(h_ref.at[pl.ds(0, MOE_ROWS), :], xs_ref.at[pl.ds(base, MOE_ROWS), :], sem)
        fill.start()
        fill.wait()


def _moe_gather(h, row_tok, n_used):
    return pl.pallas_call(
        _gather_kernel,
        out_shape=jax.ShapeDtypeStruct((N_MOE_ROWS, D_MODEL), F32),
        grid_spec=pltpu.PrefetchScalarGridSpec(
            num_scalar_prefetch=2, grid=(N_MOE_BLOCKS,),
            in_specs=[pl.BlockSpec(memory_space=pl.ANY)],
            out_specs=pl.BlockSpec(memory_space=pl.ANY),
            scratch_shapes=[pltpu.SemaphoreType.DMA]),
        compiler_params=_params(1),
        name="moe_gather",
    )(row_tok, n_used, h)


def _expert_kernel(be_ref, nb_ref, x_ref, wgu_ref, bgu_ref, wd_ref, bd_ref, rw_ref, o_ref, wgu_bf, wd_bf):
    i = pl.program_id(0)
    e = be_ref[i]
    prev = be_ref[jnp.maximum(i - 1, 0)]

    @pl.when((i == 0) | (e != prev))
    def _():
        wgu_bf[...] = wgu_ref[0].astype(BF16)
        wd_bf[...] = wd_ref[0].astype(BF16)

    @pl.when(i < nb_ref[0])
    def _():
        gu = _dot(x_ref[...].astype(BF16), wgu_bf[...]) + bgu_ref[0]
        gate = jnp.minimum(gu[:, :D_EXPERT], SWIGLU_LIMIT)
        up = jnp.clip(gu[:, D_EXPERT:], -SWIGLU_LIMIT, SWIGLU_LIMIT)
        act = (up + 1.0) * gate * _sigmoid(SWIGLU_ALPHA * gate)
        out = _dot(act.astype(BF16), wd_bf[...]) + bd_ref[0]
        o_ref[...] = out * rw_ref[...]

    @pl.when(i >= nb_ref[0])
    def _():
        o_ref[...] = jnp.zeros_like(o_ref)


def _moe_experts(xs, row_w, block_e, n_used, w_gu, b_gu, w_down, b_down, layer):
    e_of = lambda i, be, nb: be[i]
    return pl.pallas_call(
        _expert_kernel,
        out_shape=jax.ShapeDtypeStruct((N_MOE_ROWS, D_MODEL), F32),
        grid_spec=pltpu.PrefetchScalarGridSpec(
            num_scalar_prefetch=2, grid=(N_MOE_BLOCKS,),
            in_specs=[pl.BlockSpec((MOE_ROWS, D_MODEL), lambda i, be, nb: (i, 0)),
                      pl.BlockSpec((1, D_MODEL, 2 * D_EXPERT), lambda i, be, nb: (layer * N_EXPERTS + be[i], 0, 0)),
                      pl.BlockSpec((1, 1, 2 * D_EXPERT), lambda i, be, nb: (layer * N_EXPERTS + be[i], 0, 0)),
                      pl.BlockSpec((1, D_EXPERT, D_MODEL), lambda i, be, nb: (layer * N_EXPERTS + be[i], 0, 0)),
                      pl.BlockSpec((1, 1, D_MODEL), lambda i, be, nb: (layer * N_EXPERTS + be[i], 0, 0)),
                      pl.BlockSpec((MOE_ROWS, 1), lambda i, be, nb: (i, 0))],
            out_specs=pl.BlockSpec((MOE_ROWS, D_MODEL), lambda i, be, nb: (i, 0)),
            scratch_shapes=[pltpu.VMEM((D_MODEL, 2 * D_EXPERT), BF16), pltpu.VMEM((D_EXPERT, D_MODEL), BF16)]),
        compiler_params=_params(1),
        name="moe_experts",
    )(block_e, n_used, xs, w_gu, b_gu, w_down, b_down, row_w.reshape(N_MOE_ROWS, 1))


def _combine_kernel(dest_ref, x_ref, g2_ref, eo_ref, y_ref, buf, sem):
    i = pl.program_id(0)
    base = i * TM * TOP_K

    def row_copy(t, kk):
        return pltpu.make_async_copy(eo_ref.at[pl.ds(dest_ref[base + t * TOP_K + kk], 1), :],
                                     buf.at[kk, pl.ds(t, 1), :], sem)

    def issue(t, c):
        for kk in range(TOP_K):
            row_copy(t, kk).start()
        return c
    lax.fori_loop(0, TM, issue, 0)

    for kk in range(TOP_K):
        pltpu.make_async_copy(eo_ref.at[pl.ds(0, TM), :], buf.at[kk], sem).wait()
    ff = (buf[0] + buf[1]) + (buf[2] + buf[3])
    y_ref[...] = x_ref[...] + g2_ref[0] * ff


def _moe_combine(dest, x1, modr, eo, layer):
    mrow = _mod_row(TM)
    base = layer * N_MOD
    return pl.pallas_call(
        _combine_kernel,
        out_shape=jax.ShapeDtypeStruct((T_ALL, D_MODEL), F32),
        grid_spec=pltpu.PrefetchScalarGridSpec(
            num_scalar_prefetch=1, grid=(N_TILES,),
            in_specs=[pl.BlockSpec((TM, D_MODEL), lambda i, d: (i, 0)),
                      pl.BlockSpec((1, 1, D_MODEL), lambda i, d: ((base + mrow(i)) * 6 + 5, 0, 0)),
                      pl.BlockSpec(memory_space=pl.ANY)],
            out_specs=pl.BlockSpec((TM, D_MODEL), lambda i, d: (i, 0)),
            scratch_shapes=[pltpu.VMEM((TOP_K, TM, D_MODEL), F32), pltpu.SemaphoreType.DMA]),
        compiler_params=_params(1),
        name="moe_combine",
    )(dest, x1, modr, eo)


def _pad_heads(w, n_heads, width):
    lead = w.shape[:-1]
    w = w.reshape(lead + (n_heads, width))
    w = jnp.pad(w, [(0, 0)] * len(lead) + [(0, 0), (0, HEAD_PAD - width)])
    return w.reshape(lead + (n_heads * HEAD_PAD,))


def _layer_weights(l, w_in, gla_wa2, gla_ba, mla_qa_g, mla_wuq, mla_kva_g, mla_wukv, mla_qn_g, mla_kn_g,
                   w_branch, w_out, router_w, router_b, norm2_g):
    wi = w_in[l]
    w_in_p = jnp.concatenate([wi[:, :3072], wi[:, 3520:], wi[:, 3104:3520], wi[:, 3072:3104],
                              jnp.zeros((D_MODEL, 64), F32)], axis=1).astype(BF16)
    wa_p = jnp.zeros((512, 512), F32)
    wa_p = wa_p.at[416:432, 0:256].set(gla_wa2[l, 0]).at[432:448, 256:512].set(gla_wa2[l, 1]).astype(BF16)
    ba_p = gla_ba[l].reshape(1, 512)
    wukv = mla_wukv[l].reshape(128, MLA_HEADS, MLA_NOPE + MLA_V)
    rw = jnp.pad(router_w[l], ((0, 0), (0, 128 - N_EXPERTS)))
    rw_hi = rw.astype(BF16)
    return {
        "w_in": w_in_p, "wa": wa_p, "ba": ba_p,
        "qa_g": mla_qa_g[l].reshape(1, 256),
        "wuq": _pad_heads(mla_wuq[l], MLA_HEADS, MLA_QK).astype(BF16),
        "qn_g": jnp.pad(mla_qn_g[l], (0, HEAD_PAD - MLA_QK)).reshape(1, 128),
        "kn_g": jnp.pad(mla_kn_g[l], (0, HEAD_PAD - MLA_QK)).reshape(1, 128),
        "kva_g": mla_kva_g[l].reshape(1, 128),
        "wk": _pad_heads(wukv[:, :, :MLA_NOPE].reshape(128, MLA_HEADS * MLA_NOPE), MLA_HEADS, MLA_NOPE).astype(BF16),
        "wv": wukv[:, :, MLA_NOPE:].reshape(128, MLA_HEADS * MLA_V).astype(BF16),
        "wb": w_branch[l].astype(BF16), "wo": w_out[l].astype(BF16),
        "rw_hi": rw_hi, "rw_lo": (rw - rw_hi.astype(F32)).astype(BF16),
        "rb": jnp.pad(router_b[l], (0, 128 - N_EXPERTS)).reshape(1, 128),
        "n2_g": norm2_g[l].reshape(1, D_MODEL),
    }


def kernel(x_prompt, x_sample, cache_mla_ckv, cache_mla_krope, state_ret, state_gla, c, c_ctx, w_mod, b_mod, norm1_g, norm2_g, w_in, ret_gn_g, gla_wa2, gla_ba, gla_norm_g, mla_qa_g, mla_wuq, mla_kva_g, mla_wukv, mla_qn_g, mla_kn_g, w_branch, w_out, router_w, router_b, moe_w_gu, moe_b_gu, moe_w_down, moe_b_down):
    x = jnp.concatenate([x_prompt.reshape(T_CTX, D_MODEL), x_sample.reshape(T_LAT, D_MODEL)], axis=0)
    cc = jnp.concatenate([c_ctx[None, :], c, jnp.zeros((N_MOD - 1 - N_LAT_SEQ, D_MODEL), F32)], axis=0)
    modr = _modulation(cc, w_mod, b_mod).reshape(DEPTH * N_MOD * 6, 1, D_MODEL)
    rope_tabs = _rope_tables()
    w_gu = moe_w_gu.reshape(DEPTH * N_EXPERTS, D_MODEL, 2 * D_EXPERT)
    b_gu = moe_b_gu.reshape(DEPTH * N_EXPERTS, 1, 2 * D_EXPERT)
    w_dn = moe_w_down.reshape(DEPTH * N_EXPERTS, D_EXPERT, D_MODEL)
    b_dn = moe_b_down.reshape(DEPTH * N_EXPERTS, 1, D_MODEL)

    ckv_l, krope_l, ret_l, gla_l = [], [], [], []
    for l in range(DEPTH):
        w = _layer_weights(l, w_in, gla_wa2, gla_ba, mla_qa_g, mla_wuq, mla_kva_g, mla_wukv, mla_qn_g, mla_kn_g,
                           w_branch, w_out, router_w, router_b, norm2_g)
        z = _in_proj(x, norm1_g[l].reshape(1, D_MODEL), modr, w["w_in"], l)

        gn = ret_gn_g[l].reshape(1, 512)
        ret_c, ret_state = _retention(z, gn, None, ctx=True)
        (ret_s,) = _retention(z, gn, state_ret[:, l], ctx=False)
        gng = gla_norm_g[l].reshape(1, 512)
        gla_c, gla_state = _gla(z, w["wa"], w["ba"], gng, None, ctx=True)
        gla_s, _ = _gla(z, w["wa"], w["ba"], gng, state_gla[:, l], ctx=False)

        q, k, v, ckv, kr = _mla_prep(z, w, rope_tabs)
        kc, vc = _mla_cache(cache_mla_ckv[:, l].reshape(N_LAT_SEQ * CTX_LEN, 128),
                            jnp.pad(cache_mla_krope[:, l].reshape(N_LAT_SEQ * CTX_LEN, MLA_ROPE),
                                    ((0, 0), (0, 128 - MLA_ROPE))), w)
        mla_c = _mla_attn(q, k, v, None, None, ctx=True)
        mla_s = _mla_attn(q, k, v, kc, vc, ctx=False)

        x1, h2, top_idx, top_w = _merge(jnp.concatenate([ret_c, ret_s], axis=0),
                                        jnp.concatenate([gla_c, gla_s], axis=0),
                                        jnp.concatenate([mla_c, mla_s], axis=0), z, x, modr, w, l)
        dest, row_tok, row_w, block_e, n_used = _route(top_idx[:, :TOP_K], top_w[:, :TOP_K])
        xs = _moe_gather(h2, row_tok, n_used)
        eo = _moe_experts(xs, row_w, block_e, n_used, w_gu, b_gu, w_dn, b_dn, l)
        x = _moe_combine(dest, x1, modr, eo, l)

        ckv_l.append(ckv[:T_CTX].reshape(N_CTX_SEQ, CTX_LEN, 128))
        krope_l.append(kr[:T_CTX, :MLA_ROPE].reshape(N_CTX_SEQ, CTX_LEN, MLA_ROPE))
        ret_l.append(ret_state)
        gla_l.append(gla_state.reshape(N_CTX_SEQ, 2, N_HEADS, DK, DV))

    y_p = x[:T_CTX].reshape(N_CTX_SEQ, CTX_LEN, D_MODEL)
    y_s = x[T_CTX:].reshape(N_LAT_SEQ, LAT_LEN, D_MODEL)
    return (y_p, y_s, jnp.stack(ckv_l, axis=1), jnp.stack(krope_l, axis=1),
            jnp.stack(ret_l, axis=1), jnp.stack(gla_l, axis=1))
```

```python
import functools

import jax
import jax.numpy as jnp
import numpy as np
from jax import lax
from jax.experimental import pallas as pl
from jax.experimental.pallas import tpu as pltpu

F32 = jnp.float32
BF16 = jnp.bfloat16

D_MODEL = 1024
DEPTH = 2
N_CTX_SEQ, CTX_LEN = 32, 256
N_LAT_SEQ, LAT_LEN = 4, 1024
T_CTX = N_CTX_SEQ * CTX_LEN
T_LAT = N_LAT_SEQ * LAT_LEN
T_ALL = T_CTX + T_LAT
TM = 256
N_TILES = T_ALL // TM
N_CTX_TILES = T_CTX // TM
LAT_TILES = LAT_LEN // TM
N_MOD = 8
EPS = 1e-6

N_HEADS = 4
DK, DV = 64, 128
GRID_W = 64
MLA_HEADS, MLA_NOPE, MLA_ROPE, MLA_V = 8, 64, 32, 64
MLA_QK = MLA_NOPE + MLA_ROPE
HEAD_PAD = 128
GLA_TAU = 16.0
N_EXPERTS, TOP_K, D_EXPERT = 32, 4, 1024
SWIGLU_LIMIT, SWIGLU_ALPHA = 7.0, 1.702
MOE_ROWS = 256
N_SLOTS = T_ALL * TOP_K
N_MOE_BLOCKS = N_SLOTS // MOE_ROWS + N_EXPERTS
N_MOE_ROWS = N_MOE_BLOCKS * MOE_ROWS
TAIL_FILL = MOE_ROWS + 8
XS_EXTRA = 2 * MOE_ROWS

DZ = 6656
IN_TILE = 512
VMEM_LIMIT = 56 * 1024 * 1024

RET_LOG_F = [float(np.log1p(-np.exp2(-(5.0 + h)))) for h in range(N_HEADS)]
RET_LOG_B = [float(np.log1p(-np.exp2(-(5.5 + h)))) for h in range(N_HEADS)]


def _params(n_axes, vmem=VMEM_LIMIT):
    return pltpu.CompilerParams(dimension_semantics=("arbitrary",) * n_axes, vmem_limit_bytes=vmem)


def _sigmoid(x):
    return 1.0 / (1.0 + jnp.exp(-x))


def _dot(a, b):
    return jnp.dot(a, b, preferred_element_type=F32)


def _dot_t(a, b):
    return lax.dot_general(a, b, (((1,), (1,)), ((), ())), preferred_element_type=F32)


def _mod_row(tile_rows):
    def f(i):
        r0 = i * tile_rows
        return jnp.where(r0 < T_CTX, 0, 1 + (r0 - T_CTX) // LAT_LEN)
    return f


def _mod_kernel(c_ref, w_ref, b_ref, o_ref):
    c = c_ref[...]
    s = c * _sigmoid(c)
    o_ref[0] = jnp.dot(s, w_ref[0], preferred_element_type=F32, precision=lax.Precision.HIGHEST) + b_ref[0]


def _modulation(cc, w_mod, b_mod):
    n = 6 * D_MODEL
    blk = 1024
    return pl.pallas_call(
        _mod_kernel,
        out_shape=jax.ShapeDtypeStruct((DEPTH, N_MOD, n), F32),
        grid=(DEPTH, n // blk),
        in_specs=[pl.BlockSpec((N_MOD, D_MODEL), lambda l, j: (0, 0)),
                  pl.BlockSpec((1, D_MODEL, blk), lambda l, j: (l, 0, j)),
                  pl.BlockSpec((1, 1, blk), lambda l, j: (l, 0, j))],
        out_specs=pl.BlockSpec((1, N_MOD, blk), lambda l, j: (l, 0, j)),
        compiler_params=_params(2),
        name="modulation",
    )(cc, w_mod, b_mod.reshape(DEPTH, 1, n))


def _in_kernel(x_ref, g_ref, sh_ref, sc_ref, w_ref, o_ref):
    x = x_ref[...]
    h = x * lax.rsqrt(jnp.mean(x * x, axis=-1, keepdims=True) + EPS) * g_ref[...]
    h = h * (1.0 + sc_ref[0]) + sh_ref[0]
    hb = h.astype(BF16)
    for n0 in range(0, DZ, 512):
        o_ref[:, n0:n0 + 512] = _dot(hb, w_ref[:, n0:n0 + 512]).astype(BF16)


def _in_proj(x, g, modr, w_in_p, layer):
    mrow = _mod_row(IN_TILE)
    base = layer * N_MOD

    def mod_spec(part):
        return pl.BlockSpec((1, 1, D_MODEL), lambda i: ((base + mrow(i)) * 6 + part, 0, 0))

    return pl.pallas_call(
        _in_kernel,
        out_shape=jax.ShapeDtypeStruct((T_ALL, DZ), BF16),
        grid=(T_ALL // IN_TILE,),
        in_specs=[pl.BlockSpec((IN_TILE, D_MODEL), lambda i: (i, 0)),
                  pl.BlockSpec((1, D_MODEL), lambda i: (0, 0)),
                  mod_spec(0), mod_spec(1),
                  pl.BlockSpec((D_MODEL, DZ), lambda i: (0, 0))],
        out_specs=pl.BlockSpec((IN_TILE, DZ), lambda i: (i, 0)),
        compiler_params=_params(1),
        name="in_proj",
    )(x, g, modr, modr, w_in_p)


def _lane_half_mask(hh):
    lane = lax.broadcasted_iota(jnp.int32, (1, 128), 1)
    return (lane < 64) if hh == 0 else (lane >= 64)


def _ret_kernel(*refs, seq, has_state, emit_state):
    q_ref, k_ref, v_ref, g_ref, gn_ref = refs[:5]
    pos = 5
    if has_state:
        s0_ref = refs[pos]
        pos += 1
    o_ref = refs[pos]
    pos += 1
    if emit_state:
        st_ref = refs[pos]

    r0 = pl.multiple_of(pl.program_id(1) * TM, TM)
    qb = q_ref[pl.ds(r0, TM), :]
    rowi = lax.broadcasted_iota(jnp.int32, (TM, seq), 0) + r0
    colj = lax.broadcasted_iota(jnp.int32, (TM, seq), 1)
    diff = (rowi - colj).astype(F32)
    on_diag = rowi == colj
    ri = (lax.broadcasted_iota(jnp.int32, (TM, 1), 0) + r0).astype(F32)

    for h in range(N_HEADS):
        p, hh = h // 2, h % 2
        lanes = slice(128 * p, 128 * p + 128)
        qp = qb[:, lanes]
        qh = jnp.where(_lane_half_mask(hh), qp, jnp.zeros_like(qp))
        sc = _dot_t(qh, k_ref[:, lanes])
        ex = jnp.where(diff > 0, RET_LOG_F[h] * diff, -RET_LOG_B[h] * diff)
        dec = jnp.exp(ex) * jnp.where(on_diag, 2.0 * DK ** -0.5, DK ** -0.5)
        o = _dot((sc * dec).astype(BF16), v_ref[:, 128 * h:128 * h + 128])
        if has_state:
            qf = qh.astype(F32)
            o += _dot((qf * jnp.exp(RET_LOG_F[h] * (ri + 1.0))).astype(BF16), s0_ref[0, 0, p].astype(BF16))
            o += _dot((qf * jnp.exp(RET_LOG_B[h] * (seq - ri))).astype(BF16), s0_ref[0, 1, p].astype(BF16))
        mu = jnp.mean(o, axis=-1, keepdims=True)
        d = o - mu
        var = jnp.mean(d * d, axis=-1, keepdims=True)
        on = d * lax.rsqrt(var + EPS)
        g = g_ref[:, 128 * h:128 * h + 128].astype(F32)
        out = on * gn_ref[:, 128 * h:128 * h + 128] * (g * _sigmoid(g))
        o_ref[:, 128 * h:128 * h + 128] = out.astype(BF16)

    if emit_state:
        jc = lax.broadcasted_iota(jnp.int32, (seq, 1), 0).astype(F32)
        lane = lax.broadcasted_iota(jnp.int32, (1, 128), 1)
        for p in range(2):
            kp = k_ref[:, 128 * p:128 * p + 128].astype(F32) * DK ** -0.5
            lgf = jnp.where(lane < 64, RET_LOG_F[2 * p], RET_LOG_F[2 * p + 1])
            lgb = jnp.where(lane < 64, RET_LOG_B[2 * p], RET_LOG_B[2 * p + 1])
            kdf = (kp * jnp.exp(lgf * (seq - 1.0 - jc))).T.astype(BF16)
            kdb = (kp * jnp.exp(lgb * jc)).T.astype(BF16)
            for hh in range(2):
                h = 2 * p + hh
                vh = v_ref[:, 128 * h:128 * h + 128]
                st_ref[0, 0, h] = _dot(kdf, vh)[64 * hh:64 * hh + 64, :]
                st_ref[0, 1, h] = _dot(kdb, vh)[64 * hh:64 * hh + 64, :]


def _retention(z, gn_g, s0, *, ctx):
    if ctx:
        nb, seq, row_blk, tile0 = N_CTX_SEQ, CTX_LEN, 0, 0
    else:
        nb, seq, row_blk, tile0 = N_LAT_SEQ, LAT_LEN, T_CTX // LAT_LEN, N_CTX_TILES
    nq = seq // TM
    in_specs = [pl.BlockSpec((seq, 256), lambda b, i: (row_blk + b, 0)),
                pl.BlockSpec((seq, 256), lambda b, i: (row_blk + b, 1)),
                pl.BlockSpec((seq, 512), lambda b, i: (row_blk + b, 1)),
                pl.BlockSpec((TM, 512), lambda b, i: (tile0 + b * nq + i, 2)),
                pl.BlockSpec((1, 512), lambda b, i: (0, 0))]
    args = [z, z, z, z, gn_g]
    out_shape = [jax.ShapeDtypeStruct((nb * seq, 512), BF16)]
    out_specs = [pl.BlockSpec((TM, 512), lambda b, i: (b * nq + i, 0))]
    if not ctx:
        in_specs.append(pl.BlockSpec((1, 2, 2, 128, 128), lambda b, i: (b, 0, 0, 0, 0)))
        args.append(s0.reshape(N_LAT_SEQ, 2, 2, 128, 128))
    else:
        out_shape.append(jax.ShapeDtypeStruct((nb, 2, N_HEADS, DK, DV), F32))
        out_specs.append(pl.BlockSpec((1, 2, N_HEADS, DK, DV), lambda b, i: (b, 0, 0, 0, 0)))
    return pl.pallas_call(
        functools.partial(_ret_kernel, seq=seq, has_state=not ctx, emit_state=ctx),
        out_shape=out_shape, grid=(nb, nq), in_specs=in_specs, out_specs=out_specs,
        compiler_params=_params(2),
        name="retention_ctx" if ctx else "retention_lat",
    )(*args)


def _gla_decay(small_ref, wa_ref, ba_ref):
    x = _dot(small_ref[...], wa_ref[...]) + ba_ref[...]
    la = -(jnp.maximum(-x, 0.0) + jnp.log(1.0 + jnp.exp(-jnp.abs(x)))) * (1.0 / GLA_TAU)
    ri = lax.broadcasted_iota(jnp.int32, (TM, TM), 0)
    ci = lax.broadcasted_iota(jnp.int32, (TM, TM), 1)
    ltri = jnp.where(ri >= ci, 1.0, 0.0).astype(BF16)
    hi = la.astype(BF16)
    r1 = la - hi.astype(F32)
    mid = r1.astype(BF16)
    lo = (r1 - mid.astype(F32)).astype(BF16)
    cum = _dot(ltri, hi) + _dot(ltri, mid) + _dot(ltri, lo)
    return la, cum


def _gla_state_kernel(k_ref, v_ref, small_ref, wa_ref, ba_ref, kv_ref, tot_ref):
    la, cum = _gla_decay(small_ref, wa_ref, ba_ref)
    bf, bb = cum[:, :256], cum[:, 256:]
    xb = bb - la[:, 256:]
    k = k_ref[...].astype(F32)
    kdf = k * jnp.exp(bf[TM - 1:TM, :] - bf)
    kdb = k * jnp.exp(xb)
    for p in range(2):
        kf_t = kdf[:, 128 * p:128 * p + 128].T.astype(BF16)
        kb_t = kdb[:, 128 * p:128 * p + 128].T.astype(BF16)
        for hh in range(2):
            h = 2 * p + hh
            vh = v_ref[:, 128 * h:128 * h + 128]
            kv_ref[0, 0, 0, h] = _dot(kf_t, vh)[64 * hh:64 * hh + 64, :]
            kv_ref[0, 0, 1, h] = _dot(kb_t, vh)[64 * hh:64 * hh + 64, :]
    tot_ref[0, 0] = jnp.sum(la.T, axis=-1, keepdims=True)


def _mid_bcast(x, s, r):
    w = 2 * s
    if w >= 8:
        n = TM // w
        x3 = x.reshape(n, w, 256)
        return jnp.broadcast_to(x3[:, r:r + 1, :], (n, w, 256)).reshape(TM, 256)
    x3 = x.reshape(TM // 8, 8, 256)
    sub = lax.broadcasted_iota(jnp.int32, (1, 8, 1), 1)
    out = None
    for blk in range(8 // w):
        rowv = jnp.broadcast_to(x3[:, blk * w + r:blk * w + r + 1, :], (TM // 8, 8, 256))
        out = rowv if out is None else jnp.where(sub >= blk * w, rowv, out)
    return out.reshape(TM, 256)


def _gla_kernel(*refs, n_blk, has_state):
    q_ref, k_ref, v_ref, g_ref, small_ref, wa_ref, ba_ref, gn_ref = refs[:8]
    pos = 8
    if has_state:
        kv_ref, tot_ref, s0_ref = refs[pos:pos + 3]
        pos += 3
    o_ref = refs[pos]

    la, cum = _gla_decay(small_ref, wa_ref, ba_ref)
    bf, bb = cum[:, :256], cum[:, 256:]
    xb = bb - la[:, 256:]
    q = q_ref[...].astype(F32) * DK ** -0.5
    k = k_ref[...].astype(F32)
    row = lax.broadcasted_iota(jnp.int32, (TM, 1), 0)
    rowi = lax.broadcasted_iota(jnp.int32, (TM, TM), 0)
    colj = lax.broadcasted_iota(jnp.int32, (TM, TM), 1)
    masks = [_lane_half_mask(0), _lane_half_mask(1)]

    def head_q(x, h):
        xp = x[:, 128 * (h // 2):128 * (h // 2) + 128]
        return jnp.where(masks[h % 2], xp, jnp.zeros_like(xp))

    qb16, kb16 = q.astype(BF16), k.astype(BF16)
    acc = []
    for h in range(N_HEADS):
        p = h // 2
        sd = _dot_t(head_q(qb16, h), kb16[:, 128 * p:128 * p + 128])
        acc.append(jnp.where(rowi == colj, 2.0 * sd, 0.0))

    s = 1
    while s < TM:
        upper = ((row // s) % 2) == 1
        mf = _mid_bcast(bf, s, s - 1)
        mb = _mid_bcast(xb, s, s)
        ef = jnp.exp(jnp.where(upper, bf - mf, mf - bf))
        eb = jnp.exp(jnp.where(upper, xb - mb, mb - xb))
        qf = jnp.where(upper, q * ef, 0.0).astype(BF16)
        kf = jnp.where(upper, 0.0, k * ef).astype(BF16)
        qbk = jnp.where(upper, 0.0, q * eb).astype(BF16)
        kbk = jnp.where(upper, k * eb, 0.0).astype(BF16)
        same = (rowi // (2 * s)) == (colj // (2 * s))
        for h in range(N_HEADS):
            p = h // 2
            lanes = slice(128 * p, 128 * p + 128)
            sl = _dot_t(head_q(qf, h), kf[:, lanes]) + _dot_t(head_q(qbk, h), kbk[:, lanes])
            acc[h] = acc[h] + jnp.where(same, sl, 0.0)
        s *= 2

    if has_state:
        n = pl.program_id(1)
        qsf = (q * jnp.exp(bf)).astype(BF16)
        qsb = (q * jnp.exp(bb[TM - 1:TM, :] - xb)).astype(BF16)

    for h in range(N_HEADS):
        o = _dot(acc[h].astype(BF16), v_ref[:, 128 * h:128 * h + 128])
        if has_state:
            sf = s0_ref[0, 0, h]
            for m in range(n_blk - 1):
                dec = jnp.exp(tot_ref[0, m, 64 * h:64 * h + 64, :])
                sf = jnp.where(m < n, dec * sf + kv_ref[0, m, 0, h], sf)
            sb = s0_ref[0, 1, h]
            for m in range(n_blk - 1, 0, -1):
                dec = jnp.exp(tot_ref[0, m, 256 + 64 * h:256 + 64 * h + 64, :])
                sb = jnp.where(m > n, dec * sb + kv_ref[0, m, 1, h], sb)
            zero = jnp.zeros((DK, DV), F32)
            hh = h % 2
            sf2 = jnp.concatenate([sf, zero] if hh == 0 else [zero, sf], axis=0).astype(BF16)
            sb2 = jnp.concatenate([sb, zero] if hh == 0 else [zero, sb], axis=0).astype(BF16)
            lanes = slice(128 * (h // 2), 128 * (h // 2) + 128)
            o += _dot(qsf[:, lanes], sf2) + _dot(qsb[:, lanes], sb2)
        on = o * lax.rsqrt(jnp.mean(o * o, axis=-1, keepdims=True) + EPS)
        g = g_ref[:, 128 * h:128 * h + 128].astype(F32)
        out = on * gn_ref[:, 128 * h:128 * h + 128] * (g * _sigmoid(g))
        o_ref[:, 128 * h:128 * h + 128] = out.astype(BF16)


def _gla(z, wa_p, ba_p, gn_g, s0, *, ctx):
    if ctx:
        nb, n_blk, tile0 = N_CTX_SEQ, 1, 0
    else:
        nb, n_blk, tile0 = N_LAT_SEQ, LAT_TILES, N_CTX_TILES

    def zspec(width, col):
        return pl.BlockSpec((TM, width), lambda b, n: (tile0 + b * n_blk + n, col))

    w_specs = [pl.BlockSpec((512, 512), lambda b, n: (0, 0)), pl.BlockSpec((1, 512), lambda b, n: (0, 0))]
    kv, tot = pl.pallas_call(
        _gla_state_kernel,
        out_shape=[jax.ShapeDtypeStruct((nb, n_blk, 2, N_HEADS, DK, DV), F32),
                   jax.ShapeDtypeStruct((nb, n_blk, 512, 1), F32)],
        grid=(nb, n_blk),
        in_specs=[zspec(256, 7), zspec(512, 4), zspec(512, 12)] + w_specs,
        out_specs=[pl.BlockSpec((1, 1, 2, N_HEADS, DK, DV), lambda b, n: (b, n, 0, 0, 0, 0)),
                   pl.BlockSpec((1, 1, 512, 1), lambda b, n: (b, n, 0, 0))],
        compiler_params=_params(2),
        name="gla_state_ctx" if ctx else "gla_state_lat",
    )(z, z, z, wa_p, ba_p)

    in_specs = [zspec(256, 6), zspec(256, 7), zspec(512, 4), zspec(512, 5), zspec(512, 12)] + w_specs
    in_specs.append(pl.BlockSpec((1, 512), lambda b, n: (0, 0)))
    args = [z, z, z, z, z, wa_p, ba_p, gn_g]
    if not ctx:
        in_specs += [pl.BlockSpec((1, n_blk, 2, N_HEADS, DK, DV), lambda b, n: (b, 0, 0, 0, 0, 0)),
                     pl.BlockSpec((1, n_blk, 512, 1), lambda b, n: (b, 0, 0, 0)),
                     pl.BlockSpec((1, 2, N_HEADS, DK, DV), lambda b, n: (b, 0, 0, 0, 0))]
        args += [kv, tot, s0]
    out = pl.pallas_call(
        functools.partial(_gla_kernel, n_blk=n_blk, has_state=not ctx),
        out_shape=jax.ShapeDtypeStruct((nb * n_blk * TM, 512), BF16),
        grid=(nb, n_blk), in_specs=in_specs,
        out_specs=pl.BlockSpec((TM, 512), lambda b, n: (b * n_blk + n, 0)),
        compiler_params=_params(2),
        name="gla_ctx" if ctx else "gla_lat",
    )(*args)
    return out, kv


def _rope_tables():
    nf = MLA_ROPE // 4
    pos = np.arange(LAT_LEN)
    freqs = (10000.0 ** (-np.arange(nf, dtype=np.float32) / nf)).astype(np.float32)
    ang_r = ((pos // GRID_W).astype(np.float32)[:, None] * freqs).astype(np.float32)
    ang_c = ((pos % GRID_W).astype(np.float32)[:, None] * freqs).astype(np.float32)
    cos = np.ones((TM + LAT_LEN, HEAD_PAD), np.float32)
    sa = np.zeros((TM + LAT_LEN, HEAD_PAD), np.float32)
    sb = np.zeros((TM + LAT_LEN, HEAD_PAD), np.float32)
    o = MLA_NOPE
    for base, ang in ((o, ang_r), (o + 2 * nf, ang_c)):
        cos[TM:, base:base + nf] = np.cos(ang)
        cos[TM:, base + nf:base + 2 * nf] = np.cos(ang)
        sa[TM:, base:base + nf] = -np.sin(ang)
        sb[TM:, base + nf:base + 2 * nf] = np.sin(ang)
    return jnp.asarray(cos), jnp.asarray(sa), jnp.asarray(sb)


def _rope(x, cos, sa, sb):
    return x * cos + pltpu.roll(x, 128 - 8, 1) * sa + pltpu.roll(x, 8, 1) * sb


def _head_norm(x, g):
    return x * lax.rsqrt(jnp.sum(x * x, axis=-1, keepdims=True) * (1.0 / MLA_QK) + EPS) * g


def _mla_keys(ckv, kr_tile, wk_ref, wv_ref, kn_ref, k_ref, v_ref, rope=None):
    cb = ckv.astype(BF16)
    kpre = _dot(cb, wk_ref[...])
    v_ref[...] = _dot(cb, wv_ref[...]).astype(BF16)
    for h in range(MLA_HEADS):
        kh = _head_norm(kpre[:, 128 * h:128 * h + 128] + kr_tile, kn_ref[...])
        if rope is not None:
            kh = _rope(kh, *rope)
        k_ref[:, 128 * h:128 * h + 128] = kh.astype(BF16)


def _mla_prep_kernel(small_ref, qa_ref, wuq_ref, qn_ref, kva_ref, wk_ref, wv_ref, kn_ref,
                     cos_ref, sa_ref, sb_ref, q_ref, k_ref, v_ref, ckv_ref, kr_ref):
    sm = small_ref[...].astype(F32)
    cq, ckv_raw, g3 = sm[:, 0:256], sm[:, 256:384], sm[:, 384:512]
    rope = (cos_ref[...], sa_ref[...], sb_ref[...])
    cqn = cq * lax.rsqrt(jnp.mean(cq * cq, axis=-1, keepdims=True) + EPS) * qa_ref[...]
    q = _dot(cqn.astype(BF16), wuq_ref[...])
    scale = MLA_QK ** -0.5
    for h in range(MLA_HEADS):
        qh = _rope(_head_norm(q[:, 128 * h:128 * h + 128], qn_ref[...]), *rope)
        q_ref[:, 128 * h:128 * h + 128] = (qh * scale).astype(BF16)
    ckv = ckv_raw * lax.rsqrt(jnp.mean(ckv_raw * ckv_raw, axis=-1, keepdims=True) + EPS) * kva_ref[...]
    ckv_ref[...] = ckv
    lane = lax.broadcasted_iota(jnp.int32, (1, 128), 1)
    kr = jnp.where(lane < MLA_ROPE, g3, 0.0)
    kr_ref[...] = kr
    _mla_keys(ckv, pltpu.roll(kr, MLA_NOPE, 1), wk_ref, wv_ref, kn_ref, k_ref, v_ref, rope)


def _mla_cache_kernel(ckv_ref, kr_ref, wk_ref, wv_ref, kn_ref, k_ref, v_ref):
    _mla_keys(ckv_ref[...], pltpu.roll(kr_ref[...], MLA_NOPE, 1), wk_ref, wv_ref, kn_ref, k_ref, v_ref)


def _mla_prep(z, w, rope_tabs):
    def rope_blk(i):
        return jnp.where(i < N_CTX_TILES, 0, 1 + (i - N_CTX_TILES) % LAT_TILES)

    const = lambda shape: pl.BlockSpec(shape, lambda i: (0,) * len(shape))
    rope_spec = pl.BlockSpec((TM, HEAD_PAD), lambda i: (rope_blk(i), 0))
    row = lambda width: pl.BlockSpec((TM, width), lambda i: (i, 0))
    return pl.pallas_call(
        _mla_prep_kernel,
        out_shape=[jax.ShapeDtypeStruct((T_ALL, 1024), BF16), jax.ShapeDtypeStruct((T_ALL, 1024), BF16),
                   jax.ShapeDtypeStruct((T_ALL, 512), BF16), jax.ShapeDtypeStruct((T_ALL, 128), F32),
                   jax.ShapeDtypeStruct((T_ALL, 128), F32)],
        grid=(N_TILES,),
        in_specs=[pl.BlockSpec((TM, 512), lambda i: (i, 12)),
                  const((1, 256)), const((256, 1024)), const((1, 128)), const((1, 128)),
                  const((128, 1024)), const((128, 512)), const((1, 128)),
                  rope_spec, rope_spec, rope_spec],
        out_specs=[row(1024), row(1024), row(512), row(128), row(128)],
        compiler_params=_params(1),
        name="mla_prep",
    )(z, w["qa_g"], w["wuq"], w["qn_g"], w["kva_g"], w["wk"], w["wv"], w["kn_g"], *rope_tabs)


def _mla_cache(ckv, kr_pad, w):
    const = lambda shape: pl.BlockSpec(shape, lambda i: (0,) * len(shape))
    row = lambda width: pl.BlockSpec((TM, width), lambda i: (i, 0))
    n = ckv.shape[0]
    return pl.pallas_call(
        _mla_cache_kernel,
        out_shape=[jax.ShapeDtypeStruct((n, 1024), BF16), jax.ShapeDtypeStruct((n, 512), BF16)],
        grid=(n // TM,),
        in_specs=[row(128), row(128), const((128, 1024)), const((128, 512)), const((1, 128))],
        out_specs=[row(1024), row(512)],
        compiler_params=_params(1),
        name="mla_cache",
    )(ckv, kr_pad, w["wk"], w["wv"], w["kn_g"])


def _mla_attn_kernel(*refs, has_cache):
    q_ref, k_ref, v_ref = refs[:3]
    pos = 3
    if has_cache:
        kc_ref, vc_ref = refs[3:5]
        pos = 5
    o_ref = refs[pos]
    for p in range(MLA_HEADS // 2):
        acc = jnp.zeros((TM, 128), F32)
        for hh in range(2):
            h = 2 * p + hh
            lanes = slice(128 * h, 128 * h + 128)
            qh = q_ref[:, lanes]
            l1 = _dot_t(qh, k_ref[:, lanes])
            m = jnp.max(l1, axis=-1, keepdims=True)
            if has_cache:
                l0 = _dot_t(qh, kc_ref[:, lanes])
                m = jnp.maximum(m, jnp.max(l0, axis=-1, keepdims=True))
                p0 = jnp.exp(l0 - m)
            p1 = jnp.exp(l1 - m)
            den = jnp.sum(p1, axis=-1, keepdims=True)
            if has_cache:
                den = den + jnp.sum(p0, axis=-1, keepdims=True)
            inv = 1.0 / den
            mask = _lane_half_mask(hh)
            vp = v_ref[:, 128 * p:128 * p + 128]
            acc += _dot((p1 * inv).astype(BF16), jnp.where(mask, vp, jnp.zeros_like(vp)))
            if has_cache:
                vcp = vc_ref[:, 128 * p:128 * p + 128]
                acc += _dot((p0 * inv).astype(BF16), jnp.where(mask, vcp, jnp.zeros_like(vcp)))
        o_ref[:, 128 * p:128 * p + 128] = acc.astype(BF16)


def _mla_attn(q, k, v, kc, vc, *, ctx):
    if ctx:
        nb, seq, row_blk, tile0 = N_CTX_SEQ, CTX_LEN, 0, 0
    else:
        nb, seq, row_blk, tile0 = N_LAT_SEQ, LAT_LEN, T_CTX // LAT_LEN, N_CTX_TILES
    nq = seq // TM
    in_specs = [pl.BlockSpec((TM, 1024), lambda b, i: (tile0 + b * nq + i, 0)),
                pl.BlockSpec((seq, 1024), lambda b, i: (row_blk + b, 0)),
                pl.BlockSpec((seq, 512), lambda b, i: (row_blk + b, 0))]
    args = [q, k, v]
    if not ctx:
        in_specs += [pl.BlockSpec((TM, 1024), lambda b, i: (b, 0)), pl.BlockSpec((TM, 512), lambda b, i: (b, 0))]
        args += [kc, vc]
    return pl.pallas_call(
        functools.partial(_mla_attn_kernel, has_cache=not ctx),
        out_shape=jax.ShapeDtypeStruct((nb * seq, 512), BF16),
        grid=(nb, nq), in_specs=in_specs,
        out_specs=pl.BlockSpec((TM, 512), lambda b, i: (b * nq + i, 0)),
        compiler_params=_params(2),
        name="mla_attn_ctx" if ctx else "mla_attn_lat",
    )(*args)


def _merge_kernel(ret_ref, gla_ref, mla_ref, m0_ref, m1_ref, m2_ref, x_ref, wb_ref, wo_ref,
                  g1_ref, n2_ref, sh2_ref, sc2_ref, rwh_ref, rwl_ref, rb_ref,
                  x1_ref, h_ref, idx_ref, w_ref):
    mix = None
    for br_ref, m_ref, n in ((ret_ref, m0_ref, 0), (gla_ref, m1_ref, 1), (mla_ref, m2_ref, 2)):
        term = _sigmoid(m_ref[...].astype(F32)) * _dot(br_ref[...], wb_ref[n])
        mix = term if mix is None else mix + term
    out = _dot(mix.astype(BF16), wo_ref[...])
    x1 = x_ref[...] + g1_ref[0] * out
    x1_ref[...] = x1
    h = x1 * lax.rsqrt(jnp.mean(x1 * x1, axis=-1, keepdims=True) + EPS) * n2_ref[...]
    h = h * (1.0 + sc2_ref[0]) + sh2_ref[0]
    h_ref[...] = h
    hh = h.astype(BF16)
    hl = (h - hh.astype(F32)).astype(BF16)
    logits = _dot(hh, rwh_ref[...]) + _dot(hh, rwl_ref[...]) + _dot(hl, rwh_ref[...]) + rb_ref[...]
    lane = lax.broadcasted_iota(jnp.int32, (TM, 128), 1)
    lanef = lane.astype(F32)
    l = jnp.where(lane < N_EXPERTS, logits, -jnp.inf)
    vals, idxs = [], []
    for _ in range(TOP_K):
        m = jnp.max(l, axis=-1, keepdims=True)
        ix = jnp.min(jnp.where(l == m, lanef, 128.0), axis=-1, keepdims=True)
        vals.append(m)
        idxs.append(ix)
        l = jnp.where(lanef == ix, -jnp.inf, l)
    es = [jnp.exp(v - vals[0]) for v in vals]
    inv = 1.0 / (es[0] + es[1] + es[2] + es[3])
    idx_out = jnp.zeros((TM, 128), F32)
    w_out = jnp.zeros((TM, 128), F32)
    for kk in range(TOP_K):
        idx_out = jnp.where(lane == kk, idxs[kk], idx_out)
        w_out = jnp.where(lane == kk, es[kk] * inv, w_out)
    idx_ref[...] = idx_out.astype(jnp.int32)
    w_ref[...] = w_out


def _merge(ret_o, gla_o, mla_o, z, x, modr, w, layer):
    mrow = _mod_row(TM)
    base = layer * N_MOD

    def mod_spec(part):
        return pl.BlockSpec((1, 1, D_MODEL), lambda i: ((base + mrow(i)) * 6 + part, 0, 0))

    const = lambda shape: pl.BlockSpec(shape, lambda i: (0,) * len(shape))
    row = lambda width: pl.BlockSpec((TM, width), lambda i: (i, 0))
    gate = lambda col: pl.BlockSpec((TM, 1024), lambda i: (i, col))
    return pl.pallas_call(
        _merge_kernel,
        out_shape=[jax.ShapeDtypeStruct((T_ALL, D_MODEL), F32), jax.ShapeDtypeStruct((T_ALL, D_MODEL), F32),
                   jax.ShapeDtypeStruct((T_ALL, 128), jnp.int32), jax.ShapeDtypeStruct((T_ALL, 128), F32)],
        grid=(N_TILES,),
        in_specs=[row(512), row(512), row(512), gate(3), gate(4), gate(5), row(1024),
                  const((3, 512, 1024)), const((1024, 1024)),
                  mod_spec(2), const((1, 1024)), mod_spec(3), mod_spec(4),
                  const((1024, 128)), const((1024, 128)), const((1, 128))],
        out_specs=[row(1024), row(1024), row(128), row(128)],
        compiler_params=_params(1),
        name="merge",
    )(ret_o, gla_o, mla_o, z, z, z, x, w["wb"], w["wo"], modr, w["n2_g"], modr, modr,
      w["rw_hi"], w["rw_lo"], w["rb"])


def _route(top_idx):
    flat_e = top_idx.reshape(N_SLOTS)
    onehot = (flat_e[:, None] == jnp.arange(N_EXPERTS, dtype=jnp.int32)[None, :]).astype(jnp.int32)
    csum = jnp.cumsum(onehot, axis=0)
    counts = csum[-1]
    padded = (counts + MOE_ROWS - 1) // MOE_ROWS * MOE_ROWS
    pad_end = jnp.cumsum(padded)
    pad_start = pad_end - padded
    dest = jnp.sum(onehot * (csum - 1 + pad_start[None, :]), axis=1).astype(jnp.int32)
    blk_start = jnp.arange(N_MOE_BLOCKS, dtype=jnp.int32) * MOE_ROWS
    block_e = jnp.minimum(jnp.sum((pad_end[None, :] <= blk_start[:, None]).astype(jnp.int32), axis=1),
                          N_EXPERTS - 1).astype(jnp.int32)
    n_used = (pad_end[-1] // MOE_ROWS).astype(jnp.int32).reshape(1)
    tail_start = (pad_start + counts).astype(jnp.int32)
    return dest, tail_start, block_e, n_used


def _dispatch_kernel(dest_ref, tail_ref, nb_ref, h_ref, xs_ref, zero_buf, sem):
    i = pl.program_id(0)

    @pl.when(i == 0)
    def _():
        zero_buf[...] = jnp.zeros_like(zero_buf)
        fills = [pltpu.make_async_copy(
            zero_buf, xs_ref.at[pl.ds(pl.multiple_of((tail_ref[e] // 8) * 8, 8), TAIL_FILL), :], sem)
            for e in range(N_EXPERTS)]
        for f in fills:
            f.start()
        for f in fills:
            f.wait()

        def fill_block(b, c):
            f = pltpu.make_async_copy(zero_buf.at[pl.ds(0, MOE_ROWS), :],
                                      xs_ref.at[pl.ds(pl.multiple_of(b * MOE_ROWS, MOE_ROWS), MOE_ROWS), :], sem)
            f.start()
            f.wait()
            return c
        lax.fori_loop(nb_ref[0], (N_MOE_ROWS + XS_EXTRA) // MOE_ROWS, fill_block, 0)

    base = i * TM * TOP_K

    def issue(t, c):
        for kk in range(TOP_K):
            pltpu.make_async_copy(h_ref.at[pl.ds(t, 1), :],
                                  xs_ref.at[pl.ds(dest_ref[base + t * TOP_K + kk], 1), :], sem).start()
        return c
    lax.fori_loop(0, TM, issue, 0, unroll=8)
    for kk in range(TOP_K):
        pltpu.make_async_copy(h_ref, xs_ref.at[pl.ds(0, TM), :], sem).wait()


def _moe_dispatch(h, dest, tail_start, n_used):
    return pl.pallas_call(
        _dispatch_kernel,
        out_shape=jax.ShapeDtypeStruct((N_MOE_ROWS + XS_EXTRA, D_MODEL), F32),
        grid_spec=pltpu.PrefetchScalarGridSpec(
            num_scalar_prefetch=3, grid=(N_TILES,),
            in_specs=[pl.BlockSpec((TM, D_MODEL), lambda i, d, t, nb: (i, 0))],
            out_specs=pl.BlockSpec(memory_space=pl.ANY),
            scratch_shapes=[pltpu.VMEM((TAIL_FILL, D_MODEL), F32), pltpu.SemaphoreType.DMA]),
        compiler_params=_params(1),
        name="moe_dispatch",
    )(dest, tail_start, n_used, h)


def _expert_kernel(be_ref, nb_ref, x_ref, wgu_ref, bgu_ref, wd_ref, bd_ref, o_ref, wgu_bf, wd_bf):
    i = pl.program_id(0)
    e = be_ref[i]
    prev = be_ref[jnp.maximum(i - 1, 0)]

    @pl.when((i == 0) | (e != prev))
    def _():
        wgu_bf[...] = wgu_ref[0].astype(BF16)
        wd_bf[...] = wd_ref[0].astype(BF16)

    @pl.when(i < nb_ref[0])
    def _():
        gu = _dot(x_ref[...].astype(BF16), wgu_bf[...]) + bgu_ref[0]
        gate = jnp.minimum(gu[:, :D_EXPERT], SWIGLU_LIMIT)
        up = jnp.clip(gu[:, D_EXPERT:], -SWIGLU_LIMIT, SWIGLU_LIMIT)
        act = (up + 1.0) * gate * _sigmoid(SWIGLU_ALPHA * gate)
        out = _dot(act.astype(BF16), wd_bf[...]) + bd_ref[0]
        o_ref[...] = out

    @pl.when(i >= nb_ref[0])
    def _():
        o_ref[...] = jnp.zeros_like(o_ref)


def _moe_experts(xs, block_e, n_used, w_gu, b_gu, w_down, b_down, layer):
    w_idx = lambda i, be, nb: (layer * N_EXPERTS + be[i], 0, 0)
    return pl.pallas_call(
        _expert_kernel,
        out_shape=jax.ShapeDtypeStruct((N_MOE_ROWS, D_MODEL), F32),
        grid_spec=pltpu.PrefetchScalarGridSpec(
            num_scalar_prefetch=2, grid=(N_MOE_BLOCKS,),
            in_specs=[pl.BlockSpec((MOE_ROWS, D_MODEL), lambda i, be, nb: (jnp.minimum(i, nb[0] - 1), 0)),
                      pl.BlockSpec((1, D_MODEL, 2 * D_EXPERT), w_idx),
                      pl.BlockSpec((1, 1, 2 * D_EXPERT), w_idx),
                      pl.BlockSpec((1, D_EXPERT, D_MODEL), w_idx),
                      pl.BlockSpec((1, 1, D_MODEL), w_idx)],
            out_specs=pl.BlockSpec((MOE_ROWS, D_MODEL), lambda i, be, nb: (i, 0)),
            scratch_shapes=[pltpu.VMEM((D_MODEL, 2 * D_EXPERT), BF16), pltpu.VMEM((D_EXPERT, D_MODEL), BF16)]),
        compiler_params=_params(1),
        name="moe_experts",
    )(block_e, n_used, xs, w_gu, b_gu, w_down, b_down)


def _combine_kernel(dest_ref, x_ref, g2_ref, w_ref, eo_ref, y_ref, buf, sem):
    i = pl.program_id(0)
    base = i * TM * TOP_K

    def issue(t, c):
        for kk in range(TOP_K):
            pltpu.make_async_copy(eo_ref.at[pl.ds(dest_ref[base + t * TOP_K + kk], 1), :],
                                  buf.at[kk, pl.ds(t, 1), :], sem).start()
        return c
    lax.fori_loop(0, TM, issue, 0, unroll=8)

    for kk in range(TOP_K):
        pltpu.make_async_copy(eo_ref.at[pl.ds(0, TM), :], buf.at[kk], sem).wait()
    w = w_ref[...]
    ff = None
    for kk in range(TOP_K):
        term = buf[kk] * w[:, kk:kk + 1]
        ff = term if ff is None else ff + term
    y_ref[...] = x_ref[...] + g2_ref[0] * ff


def _moe_combine(dest, x1, modr, top_w, eo, layer):
    mrow = _mod_row(TM)
    base = layer * N_MOD
    return pl.pallas_call(
        _combine_kernel,
        out_shape=jax.ShapeDtypeStruct((T_ALL, D_MODEL), F32),
        grid_spec=pltpu.PrefetchScalarGridSpec(
            num_scalar_prefetch=1, grid=(N_TILES,),
            in_specs=[pl.BlockSpec((TM, D_MODEL), lambda i, d: (i, 0)),
                      pl.BlockSpec((1, 1, D_MODEL), lambda i, d: ((base + mrow(i)) * 6 + 5, 0, 0)),
                      pl.BlockSpec((TM, 128), lambda i, d: (i, 0)),
                      pl.BlockSpec(memory_space=pl.ANY)],
            out_specs=pl.BlockSpec((TM, D_MODEL), lambda i, d: (i, 0)),
            scratch_shapes=[pltpu.VMEM((TOP_K, TM, D_MODEL), F32), pltpu.SemaphoreType.DMA]),
        compiler_params=_params(1),
        name="moe_combine",
    )(dest, x1, modr, top_w, eo)


def _pad_heads(w, n_heads, width):
    lead = w.shape[:-1]
    w = w.reshape(lead + (n_heads, width))
    w = jnp.pad(w, [(0, 0)] * len(lead) + [(0, 0), (0, HEAD_PAD - width)])
    return w.reshape(lead + (n_heads * HEAD_PAD,))


def _layer_weights(l, w_in, gla_wa2, gla_ba, mla_qa_g, mla_wuq, mla_kva_g, mla_wukv, mla_qn_g, mla_kn_g,
                   w_branch, w_out, router_w, router_b, norm2_g):
    wi = w_in[l]
    w_in_p = jnp.concatenate([wi[:, :3072], wi[:, 3520:], wi[:, 3104:3520], wi[:, 3072:3104],
                              jnp.zeros((D_MODEL, 64), F32)], axis=1).astype(BF16)
    wa_p = jnp.zeros((512, 512), F32)
    wa_p = wa_p.at[416:432, 0:256].set(gla_wa2[l, 0]).at[432:448, 256:512].set(gla_wa2[l, 1]).astype(BF16)
    ba_p = gla_ba[l].reshape(1, 512)
    wukv = mla_wukv[l].reshape(128, MLA_HEADS, MLA_NOPE + MLA_V)
    rw = jnp.pad(router_w[l], ((0, 0), (0, 128 - N_EXPERTS)))
    rw_hi = rw.astype(BF16)
    return {
        "w_in": w_in_p, "wa": wa_p, "ba": ba_p,
        "qa_g": mla_qa_g[l].reshape(1, 256),
        "wuq": _pad_heads(mla_wuq[l], MLA_HEADS, MLA_QK).astype(BF16),
        "qn_g": jnp.pad(mla_qn_g[l], (0, HEAD_PAD - MLA_QK)).reshape(1, 128),
        "kn_g": jnp.pad(mla_kn_g[l], (0, HEAD_PAD - MLA_QK)).reshape(1, 128),
        "kva_g": mla_kva_g[l].reshape(1, 128),
        "wk": _pad_heads(wukv[:, :, :MLA_NOPE].reshape(128, MLA_HEADS * MLA_NOPE), MLA_HEADS, MLA_NOPE).astype(BF16),
        "wv": wukv[:, :, MLA_NOPE:].reshape(128, MLA_HEADS * MLA_V).astype(BF16),
        "wb": w_branch[l].astype(BF16), "wo": w_out[l].astype(BF16),
        "rw_hi": rw_hi, "rw_lo": (rw - rw_hi.astype(F32)).astype(BF16),
        "rb": jnp.pad(router_b[l], (0, 128 - N_EXPERTS)).reshape(1, 128),
        "n2_g": norm2_g[l].reshape(1, D_MODEL),
    }


def kernel(x_prompt, x_sample, cache_mla_ckv, cache_mla_krope, state_ret, state_gla, c, c_ctx, w_mod, b_mod, norm1_g, norm2_g, w_in, ret_gn_g, gla_wa2, gla_ba, gla_norm_g, mla_qa_g, mla_wuq, mla_kva_g, mla_wukv, mla_qn_g, mla_kn_g, w_branch, w_out, router_w, router_b, moe_w_gu, moe_b_gu, moe_w_down, moe_b_down):
    x = jnp.concatenate([x_prompt.reshape(T_CTX, D_MODEL), x_sample.reshape(T_LAT, D_MODEL)], axis=0)
    cc = jnp.concatenate([c_ctx[None, :], c, jnp.zeros((N_MOD - 1 - N_LAT_SEQ, D_MODEL), F32)], axis=0)
    modr = _modulation(cc, w_mod, b_mod).reshape(DEPTH * N_MOD * 6, 1, D_MODEL)
    rope_tabs = _rope_tables()
    w_gu = moe_w_gu.reshape(DEPTH * N_EXPERTS, D_MODEL, 2 * D_EXPERT)
    b_gu = moe_b_gu.reshape(DEPTH * N_EXPERTS, 1, 2 * D_EXPERT)
    w_dn = moe_w_down.reshape(DEPTH * N_EXPERTS, D_EXPERT, D_MODEL)
    b_dn = moe_b_down.reshape(DEPTH * N_EXPERTS, 1, D_MODEL)

    ckv_l, krope_l, ret_l, gla_l = [], [], [], []
    for l in range(DEPTH):
        w = _layer_weights(l, w_in, gla_wa2, gla_ba, mla_qa_g, mla_wuq, mla_kva_g, mla_wukv, mla_qn_g, mla_kn_g,
                           w_branch, w_out, router_w, router_b, norm2_g)
        z = _in_proj(x, norm1_g[l].reshape(1, D_MODEL), modr, w["w_in"], l)

        gn = ret_gn_g[l].reshape(1, 512)
        ret_c, ret_state = _retention(z, gn, None, ctx=True)
        (ret_s,) = _retention(z, gn, state_ret[:, l], ctx=False)
        gng = gla_norm_g[l].reshape(1, 512)
        gla_c, gla_state = _gla(z, w["wa"], w["ba"], gng, None, ctx=True)
        gla_s, _ = _gla(z, w["wa"], w["ba"], gng, state_gla[:, l], ctx=False)

        q, k, v, ckv, kr = _mla_prep(z, w, rope_tabs)
        kc, vc = _mla_cache(cache_mla_ckv[:, l].reshape(N_LAT_SEQ * CTX_LEN, 128),
                            jnp.pad(cache_mla_krope[:, l].reshape(N_LAT_SEQ * CTX_LEN, MLA_ROPE),
                                    ((0, 0), (0, 128 - MLA_ROPE))), w)
        mla_c = _mla_attn(q, k, v, None, None, ctx=True)
        mla_s = _mla_attn(q, k, v, kc, vc, ctx=False)

        x1, h2, top_idx, top_w = _merge(jnp.concatenate([ret_c, ret_s], axis=0),
                                        jnp.concatenate([gla_c, gla_s], axis=0),
                                        jnp.concatenate([mla_c, mla_s], axis=0), z, x, modr, w, l)
        dest, tail_start, block_e, n_used = _route(top_idx[:, :TOP_K])
        xs = _moe_dispatch(h2, dest, tail_start, n_used)
        eo = _moe_experts(xs, block_e, n_used, w_gu, b_gu, w_dn, b_dn, l)
        x = _moe_combine(dest, x1, modr, top_w, eo, l)

        ckv_l.append(ckv[:T_CTX].reshape(N_CTX_SEQ, CTX_LEN, 128))
        krope_l.append(kr[:T_CTX, :MLA_ROPE].reshape(N_CTX_SEQ, CTX_LEN, MLA_ROPE))
        ret_l.append(ret_state)
        gla_l.append(gla_state.reshape(N_CTX_SEQ, 2, N_HEADS, DK, DV))

    y_p = x[:T_CTX].reshape(N_CTX_SEQ, CTX_LEN, D_MODEL)
    y_s = x[T_CTX:].reshape(N_LAT_SEQ, LAT_LEN, D_MODEL)
    return (y_p, y_s, jnp.stack(ckv_l, axis=1), jnp.stack(krope_l, axis=1),
            jnp.stack(ret_l, axis=1), jnp.stack(gla_l, axis=1))
```

```python
import functools

import jax
import jax.numpy as jnp
import numpy as np
from jax import lax
from jax.experimental import pallas as pl
from jax.experimental.pallas import tpu as pltpu

F32 = jnp.float32
BF16 = jnp.bfloat16

D_MODEL = 1024
DEPTH = 2
N_CTX_SEQ, CTX_LEN = 32, 256
N_LAT_SEQ, LAT_LEN = 4, 1024
T_CTX = N_CTX_SEQ * CTX_LEN
T_LAT = N_LAT_SEQ * LAT_LEN
T_ALL = T_CTX + T_LAT
TM = 256
N_TILES = T_ALL // TM
N_CTX_TILES = T_CTX // TM
LAT_TILES = LAT_LEN // TM
N_MOD = 8
EPS = 1e-6

N_HEADS = 4
DK, DV = 64, 128
GRID_W = 64
MLA_HEADS, MLA_NOPE, MLA_ROPE, MLA_V = 8, 64, 32, 64
MLA_QK = MLA_NOPE + MLA_ROPE
HEAD_PAD = 128
GLA_TAU = 16.0
N_EXPERTS, TOP_K, D_EXPERT = 32, 4, 1024
SWIGLU_LIMIT, SWIGLU_ALPHA = 7.0, 1.702
MOE_ROWS = 256
N_SLOTS = T_ALL * TOP_K
N_MOE_BLOCKS = N_SLOTS // MOE_ROWS + N_EXPERTS
N_MOE_ROWS = N_MOE_BLOCKS * MOE_ROWS
TAIL_FILL = MOE_ROWS + 8
XS_EXTRA = 2 * MOE_ROWS

DZ = 6656
IN_TILE = 512
VMEM_LIMIT = 56 * 1024 * 1024

RET_LOG_F = [float(np.log1p(-np.exp2(-(5.0 + h)))) for h in range(N_HEADS)]
RET_LOG_B = [float(np.log1p(-np.exp2(-(5.5 + h)))) for h in range(N_HEADS)]


def _params(n_axes, vmem=VMEM_LIMIT):
    return pltpu.CompilerParams(dimension_semantics=("arbitrary",) * n_axes, vmem_limit_bytes=vmem)


def _sigmoid(x):
    return 1.0 / (1.0 + jnp.exp(-x))


def _dot(a, b):
    return jnp.dot(a, b, preferred_element_type=F32)


def _dot_t(a, b):
    return lax.dot_general(a, b, (((1,), (1,)), ((), ())), preferred_element_type=F32)


def _mod_row(tile_rows):
    def f(i):
        r0 = i * tile_rows
        return jnp.where(r0 < T_CTX, 0, 1 + (r0 - T_CTX) // LAT_LEN)
    return f


def _mod_kernel(c_ref, w_ref, b_ref, o_ref):
    c = c_ref[...]
    s = c * _sigmoid(c)
    o_ref[0] = jnp.dot(s, w_ref[0], preferred_element_type=F32, precision=lax.Precision.HIGHEST) + b_ref[0]


def _modulation(cc, w_mod, b_mod):
    n = 6 * D_MODEL
    blk = 1024
    return pl.pallas_call(
        _mod_kernel,
        out_shape=jax.ShapeDtypeStruct((DEPTH, N_MOD, n), F32),
        grid=(DEPTH, n // blk),
        in_specs=[pl.BlockSpec((N_MOD, D_MODEL), lambda l, j: (0, 0)),
                  pl.BlockSpec((1, D_MODEL, blk), lambda l, j: (l, 0, j)),
                  pl.BlockSpec((1, 1, blk), lambda l, j: (l, 0, j))],
        out_specs=pl.BlockSpec((1, N_MOD, blk), lambda l, j: (l, 0, j)),
        compiler_params=_params(2),
        name="modulation",
    )(cc, w_mod, b_mod.reshape(DEPTH, 1, n))


def _in_kernel(x_ref, g_ref, sh_ref, sc_ref, w_ref, o_ref):
    x = x_ref[...]
    h = x * lax.rsqrt(jnp.mean(x * x, axis=-1, keepdims=True) + EPS) * g_ref[...]
    h = h * (1.0 + sc_ref[0]) + sh_ref[0]
    hb = h.astype(BF16)
    for n0 in range(0, DZ, 512):
        o_ref[:, n0:n0 + 512] = _dot(hb, w_ref[:, n0:n0 + 512]).astype(BF16)


def _in_proj(x, g, modr, w_in_p, layer):
    mrow = _mod_row(IN_TILE)
    base = layer * N_MOD

    def mod_spec(part):
        return pl.BlockSpec((1, 1, D_MODEL), lambda i: ((base + mrow(i)) * 6 + part, 0, 0))

    return pl.pallas_call(
        _in_kernel,
        out_shape=jax.ShapeDtypeStruct((T_ALL, DZ), BF16),
        grid=(T_ALL // IN_TILE,),
        in_specs=[pl.BlockSpec((IN_TILE, D_MODEL), lambda i: (i, 0)),
                  pl.BlockSpec((1, D_MODEL), lambda i: (0, 0)),
                  mod_spec(0), mod_spec(1),
                  pl.BlockSpec((D_MODEL, DZ), lambda i: (0, 0))],
        out_specs=pl.BlockSpec((IN_TILE, DZ), lambda i: (i, 0)),
        compiler_params=_params(1),
        name="in_proj",
    )(x, g, modr, modr, w_in_p)


def _lane_half_mask(hh):
    lane = lax.broadcasted_iota(jnp.int32, (1, 128), 1)
    return (lane < 64) if hh == 0 else (lane >= 64)


def _ret_kernel(*refs, seq, has_state, emit_state):
    q_ref, k_ref, v_ref, g_ref, gn_ref = refs[:5]
    pos = 5
    if has_state:
        s0_ref = refs[pos]
        pos += 1
    o_ref = refs[pos]
    pos += 1
    if emit_state:
        st_ref = refs[pos]

    r0 = pl.multiple_of(pl.program_id(1) * TM, TM)
    qb = q_ref[pl.ds(r0, TM), :]
    rowi = lax.broadcasted_iota(jnp.int32, (TM, seq), 0) + r0
    colj = lax.broadcasted_iota(jnp.int32, (TM, seq), 1)
    diff = (rowi - colj).astype(F32)
    on_diag = rowi == colj
    ri = (lax.broadcasted_iota(jnp.int32, (TM, 1), 0) + r0).astype(F32)

    for h in range(N_HEADS):
        p, hh = h // 2, h % 2
        lanes = slice(128 * p, 128 * p + 128)
        qp = qb[:, lanes]
        qh = jnp.where(_lane_half_mask(hh), qp, jnp.zeros_like(qp))
        sc = _dot_t(qh, k_ref[:, lanes])
        ex = jnp.where(diff > 0, RET_LOG_F[h] * diff, -RET_LOG_B[h] * diff)
        dec = jnp.exp(ex) * jnp.where(on_diag, 2.0 * DK ** -0.5, DK ** -0.5)
        o = _dot((sc * dec).astype(BF16), v_ref[:, 128 * h:128 * h + 128])
        if has_state:
            qf = qh.astype(F32)
            o += _dot((qf * jnp.exp(RET_LOG_F[h] * (ri + 1.0))).astype(BF16), s0_ref[0, 0, p].astype(BF16))
            o += _dot((qf * jnp.exp(RET_LOG_B[h] * (seq - ri))).astype(BF16), s0_ref[0, 1, p].astype(BF16))
        mu = jnp.mean(o, axis=-1, keepdims=True)
        d = o - mu
        var = jnp.mean(d * d, axis=-1, keepdims=True)
        on = d * lax.rsqrt(var + EPS)
        g = g_ref[:, 128 * h:128 * h + 128].astype(F32)
        out = on * gn_ref[:, 128 * h:128 * h + 128] * (g * _sigmoid(g))
        o_ref[:, 128 * h:128 * h + 128] = out.astype(BF16)

    if emit_state:
        jc = lax.broadcasted_iota(jnp.int32, (seq, 1), 0).astype(F32)
        lane = lax.broadcasted_iota(jnp.int32, (1, 128), 1)
        for p in range(2):
            kp = k_ref[:, 128 * p:128 * p + 128].astype(F32) * DK ** -0.5
            lgf = jnp.where(lane < 64, RET_LOG_F[2 * p], RET_LOG_F[2 * p + 1])
            lgb = jnp.where(lane < 64, RET_LOG_B[2 * p], RET_LOG_B[2 * p + 1])
            kdf = (kp * jnp.exp(lgf * (seq - 1.0 - jc))).T.astype(BF16)
            kdb = (kp * jnp.exp(lgb * jc)).T.astype(BF16)
            for hh in range(2):
                h = 2 * p + hh
                vh = v_ref[:, 128 * h:128 * h + 128]
                st_ref[0, 0, h] = _dot(kdf, vh)[64 * hh:64 * hh + 64, :]
                st_ref[0, 1, h] = _dot(kdb, vh)[64 * hh:64 * hh + 64, :]


def _retention(z, gn_g, s0, *, ctx):
    if ctx:
        nb, seq, row_blk, tile0 = N_CTX_SEQ, CTX_LEN, 0, 0
    else:
        nb, seq, row_blk, tile0 = N_LAT_SEQ, LAT_LEN, T_CTX // LAT_LEN, N_CTX_TILES
    nq = seq // TM
    in_specs = [pl.BlockSpec((seq, 256), lambda b, i: (row_blk + b, 0)),
                pl.BlockSpec((seq, 256), lambda b, i: (row_blk + b, 1)),
                pl.BlockSpec((seq, 512), lambda b, i: (row_blk + b, 1)),
                pl.BlockSpec((TM, 512), lambda b, i: (tile0 + b * nq + i, 2)),
                pl.BlockSpec((1, 512), lambda b, i: (0, 0))]
    args = [z, z, z, z, gn_g]
    out_shape = [jax.ShapeDtypeStruct((nb * seq, 512), BF16)]
    out_specs = [pl.BlockSpec((TM, 512), lambda b, i: (b * nq + i, 0))]
    if not ctx:
        in_specs.append(pl.BlockSpec((1, 2, 2, 128, 128), lambda b, i: (b, 0, 0, 0, 0)))
        args.append(s0.reshape(N_LAT_SEQ, 2, 2, 128, 128))
    else:
        out_shape.append(jax.ShapeDtypeStruct((nb, 2, N_HEADS, DK, DV), F32))
        out_specs.append(pl.BlockSpec((1, 2, N_HEADS, DK, DV), lambda b, i: (b, 0, 0, 0, 0)))
    return pl.pallas_call(
        functools.partial(_ret_kernel, seq=seq, has_state=not ctx, emit_state=ctx),
        out_shape=out_shape, grid=(nb, nq), in_specs=in_specs, out_specs=out_specs,
        compiler_params=_params(2),
        name="retention_ctx" if ctx else "retention_lat",
    )(*args)


def _gla_decay(small_ref, wa_ref, ba_ref):
    x = _dot(small_ref[...], wa_ref[...]) + ba_ref[...]
    la = -(jnp.maximum(-x, 0.0) + jnp.log(1.0 + jnp.exp(-jnp.abs(x)))) * (1.0 / GLA_TAU)
    ri = lax.broadcasted_iota(jnp.int32, (TM, TM), 0)
    ci = lax.broadcasted_iota(jnp.int32, (TM, TM), 1)
    ltri = jnp.where(ri >= ci, 1.0, 0.0).astype(BF16)
    hi = la.astype(BF16)
    r1 = la - hi.astype(F32)
    mid = r1.astype(BF16)
    lo = (r1 - mid.astype(F32)).astype(BF16)
    cum = _dot(ltri, hi) + _dot(ltri, mid) + _dot(ltri, lo)
    return la, cum


def _gla_state_kernel(k_ref, v_ref, small_ref, wa_ref, ba_ref, kv_ref, tot_ref):
    la, cum = _gla_decay(small_ref, wa_ref, ba_ref)
    bf, bb = cum[:, :256], cum[:, 256:]
    xb = bb - la[:, 256:]
    k = k_ref[...].astype(F32)
    kdf = k * jnp.exp(bf[TM - 1:TM, :] - bf)
    kdb = k * jnp.exp(xb)
    for p in range(2):
        kf_t = kdf[:, 128 * p:128 * p + 128].T.astype(BF16)
        kb_t = kdb[:, 128 * p:128 * p + 128].T.astype(BF16)
        for hh in range(2):
            h = 2 * p + hh
            vh = v_ref[:, 128 * h:128 * h + 128]
            kv_ref[0, 0, 0, h] = _dot(kf_t, vh)[64 * hh:64 * hh + 64, :]
            kv_ref[0, 0, 1, h] = _dot(kb_t, vh)[64 * hh:64 * hh + 64, :]
    tot_ref[0, 0] = jnp.sum(la.T, axis=-1, keepdims=True)


def _mid_bcast(x, s, r):
    w = 2 * s
    if w >= 8:
        n = TM // w
        x3 = x.reshape(n, w, 256)
        return jnp.broadcast_to(x3[:, r:r + 1, :], (n, w, 256)).reshape(TM, 256)
    x3 = x.reshape(TM // 8, 8, 256)
    sub = lax.broadcasted_iota(jnp.int32, (1, 8, 1), 1)
    out = None
    for blk in range(8 // w):
        rowv = jnp.broadcast_to(x3[:, blk * w + r:blk * w + r + 1, :], (TM // 8, 8, 256))
        out = rowv if out is None else jnp.where(sub >= blk * w, rowv, out)
    return out.reshape(TM, 256)


def _gla_kernel(*refs, n_blk, has_state):
    q_ref, k_ref, v_ref, g_ref, small_ref, wa_ref, ba_ref, gn_ref = refs[:8]
    pos = 8
    if has_state:
        kv_ref, tot_ref, s0_ref = refs[pos:pos + 3]
        pos += 3
    o_ref = refs[pos]

    la, cum = _gla_decay(small_ref, wa_ref, ba_ref)
    bf, bb = cum[:, :256], cum[:, 256:]
    xb = bb - la[:, 256:]
    q = q_ref[...].astype(F32) * DK ** -0.5
    k = k_ref[...].astype(F32)
    row = lax.broadcasted_iota(jnp.int32, (TM, 1), 0)
    rowi = lax.broadcasted_iota(jnp.int32, (TM, TM), 0)
    colj = lax.broadcasted_iota(jnp.int32, (TM, TM), 1)
    masks = [_lane_half_mask(0), _lane_half_mask(1)]

    def head_q(x, h):
        xp = x[:, 128 * (h // 2):128 * (h // 2) + 128]
        return jnp.where(masks[h % 2], xp, jnp.zeros_like(xp))

    qb16, kb16 = q.astype(BF16), k.astype(BF16)
    acc = []
    for h in range(N_HEADS):
        p = h // 2
        sd = _dot_t(head_q(qb16, h), kb16[:, 128 * p:128 * p + 128])
        acc.append(jnp.where(rowi == colj, 2.0 * sd, 0.0))

    s = 1
    while s < TM:
        upper = ((row // s) % 2) == 1
        mf = _mid_bcast(bf, s, s - 1)
        mb = _mid_bcast(xb, s, s)
        ef = jnp.exp(jnp.where(upper, bf - mf, mf - bf))
        eb = jnp.exp(jnp.where(upper, xb - mb, mb - xb))
        qf = jnp.where(upper, q * ef, 0.0).astype(BF16)
        kf = jnp.where(upper, 0.0, k * ef).astype(BF16)
        qbk = jnp.where(upper, 0.0, q * eb).astype(BF16)
        kbk = jnp.where(upper, k * eb, 0.0).astype(BF16)
        same = (rowi // (2 * s)) == (colj // (2 * s))
        for h in range(N_HEADS):
            p = h // 2
            lanes = slice(128 * p, 128 * p + 128)
            sl = _dot_t(head_q(qf, h), kf[:, lanes]) + _dot_t(head_q(qbk, h), kbk[:, lanes])
            acc[h] = acc[h] + jnp.where(same, sl, 0.0)
        s *= 2

    if has_state:
        n = pl.program_id(1)
        qsf = (q * jnp.exp(bf)).astype(BF16)
        qsb = (q * jnp.exp(bb[TM - 1:TM, :] - xb)).astype(BF16)

    for h in range(N_HEADS):
        o = _dot(acc[h].astype(BF16), v_ref[:, 128 * h:128 * h + 128])
        if has_state:
            sf = s0_ref[0, 0, h]
            for m in range(n_blk - 1):
                dec = jnp.exp(tot_ref[0, m, 64 * h:64 * h + 64, :])
                sf = jnp.where(m < n, dec * sf + kv_ref[0, m, 0, h], sf)
            sb = s0_ref[0, 1, h]
            for m in range(n_blk - 1, 0, -1):
                dec = jnp.exp(tot_ref[0, m, 256 + 64 * h:256 + 64 * h + 64, :])
                sb = jnp.where(m > n, dec * sb + kv_ref[0, m, 1, h], sb)
            zero = jnp.zeros((DK, DV), F32)
            hh = h % 2
            sf2 = jnp.concatenate([sf, zero] if hh == 0 else [zero, sf], axis=0).astype(BF16)
            sb2 = jnp.concatenate([sb, zero] if hh == 0 else [zero, sb], axis=0).astype(BF16)
            lanes = slice(128 * (h // 2), 128 * (h // 2) + 128)
            o += _dot(qsf[:, lanes], sf2) + _dot(qsb[:, lanes], sb2)
        on = o * lax.rsqrt(jnp.mean(o * o, axis=-1, keepdims=True) + EPS)
        g = g_ref[:, 128 * h:128 * h + 128].astype(F32)
        out = on * gn_ref[:, 128 * h:128 * h + 128] * (g * _sigmoid(g))
        o_ref[:, 128 * h:128 * h + 128] = out.astype(BF16)


def _gla(z, wa_p, ba_p, gn_g, s0, *, ctx):
    if ctx:
        nb, n_blk, tile0 = N_CTX_SEQ, 1, 0
    else:
        nb, n_blk, tile0 = N_LAT_SEQ, LAT_TILES, N_CTX_TILES

    def zspec(width, col):
        return pl.BlockSpec((TM, width), lambda b, n: (tile0 + b * n_blk + n, col))

    w_specs = [pl.BlockSpec((512, 512), lambda b, n: (0, 0)), pl.BlockSpec((1, 512), lambda b, n: (0, 0))]
    kv, tot = pl.pallas_call(
        _gla_state_kernel,
        out_shape=[jax.ShapeDtypeStruct((nb, n_blk, 2, N_HEADS, DK, DV), F32),
                   jax.ShapeDtypeStruct((nb, n_blk, 512, 1), F32)],
        grid=(nb, n_blk),
        in_specs=[zspec(256, 7), zspec(512, 4), zspec(512, 12)] + w_specs,
        out_specs=[pl.BlockSpec((1, 1, 2, N_HEADS, DK, DV), lambda b, n: (b, n, 0, 0, 0, 0)),
                   pl.BlockSpec((1, 1, 512, 1), lambda b, n: (b, n, 0, 0))],
        compiler_params=_params(2),
        name="gla_state_ctx" if ctx else "gla_state_lat",
    )(z, z, z, wa_p, ba_p)

    in_specs = [zspec(256, 6), zspec(256, 7), zspec(512, 4), zspec(512, 5), zspec(512, 12)] + w_specs
    in_specs.append(pl.BlockSpec((1, 512), lambda b, n: (0, 0)))
    args = [z, z, z, z, z, wa_p, ba_p, gn_g]
    if not ctx:
        in_specs += [pl.BlockSpec((1, n_blk, 2, N_HEADS, DK, DV), lambda b, n: (b, 0, 0, 0, 0, 0)),
                     pl.BlockSpec((1, n_blk, 512, 1), lambda b, n: (b, 0, 0, 0)),
                     pl.BlockSpec((1, 2, N_HEADS, DK, DV), lambda b, n: (b, 0, 0, 0, 0))]
        args += [kv, tot, s0]
    out = pl.pallas_call(
        functools.partial(_gla_kernel, n_blk=n_blk, has_state=not ctx),
        out_shape=jax.ShapeDtypeStruct((nb * n_blk * TM, 512), BF16),
        grid=(nb, n_blk), in_specs=in_specs,
        out_specs=pl.BlockSpec((TM, 512), lambda b, n: (b * n_blk + n, 0)),
        compiler_params=_params(2),
        name="gla_ctx" if ctx else "gla_lat",
    )(*args)
    return out, kv


def _rope_tables():
    nf = MLA_ROPE // 4
    pos = np.arange(LAT_LEN)
    freqs = (10000.0 ** (-np.arange(nf, dtype=np.float32) / nf)).astype(np.float32)
    ang_r = ((pos // GRID_W).astype(np.float32)[:, None] * freqs).astype(np.float32)
    ang_c = ((pos % GRID_W).astype(np.float32)[:, None] * freqs).astype(np.float32)
    cos = np.ones((TM + LAT_LEN, HEAD_PAD), np.float32)
    sa = np.zeros((TM + LAT_LEN, HEAD_PAD), np.float32)
    sb = np.zeros((TM + LAT_LEN, HEAD_PAD), np.float32)
    o = MLA_NOPE
    for base, ang in ((o, ang_r), (o + 2 * nf, ang_c)):
        cos[TM:, base:base + nf] = np.cos(ang)
        cos[TM:, base + nf:base + 2 * nf] = np.cos(ang)
        sa[TM:, base:base + nf] = -np.sin(ang)
        sb[TM:, base + nf:base + 2 * nf] = np.sin(ang)
    return jnp.asarray(cos), jnp.asarray(sa), jnp.asarray(sb)


def _rope(x, cos, sa, sb):
    return x * cos + pltpu.roll(x, 128 - 8, 1) * sa + pltpu.roll(x, 8, 1) * sb


def _head_norm(x, g):
    return x * lax.rsqrt(jnp.sum(x * x, axis=-1, keepdims=True) * (1.0 / MLA_QK) + EPS) * g


def _mla_keys(ckv, kr_tile, wk_ref, wv_ref, kn_ref, k_ref, v_ref, rope=None):
    cb = ckv.astype(BF16)
    kpre = _dot(cb, wk_ref[...])
    v_ref[...] = _dot(cb, wv_ref[...]).astype(BF16)
    for h in range(MLA_HEADS):
        kh = _head_norm(kpre[:, 128 * h:128 * h + 128] + kr_tile, kn_ref[...])
        if rope is not None:
            kh = _rope(kh, *rope)
        k_ref[:, 128 * h:128 * h + 128] = kh.astype(BF16)


def _mla_prep_kernel(small_ref, qa_ref, wuq_ref, qn_ref, kva_ref, wk_ref, wv_ref, kn_ref,
                     cos_ref, sa_ref, sb_ref, q_ref, k_ref, v_ref, ckv_ref, kr_ref):
    def body(rope):
        sm = small_ref[...].astype(F32)
        cq, ckv_raw, g3 = sm[:, 0:256], sm[:, 256:384], sm[:, 384:512]
        cqn = cq * lax.rsqrt(jnp.mean(cq * cq, axis=-1, keepdims=True) + EPS) * qa_ref[...]
        q = _dot(cqn.astype(BF16), wuq_ref[...])
        scale = MLA_QK ** -0.5
        for h in range(MLA_HEADS):
            qh = _head_norm(q[:, 128 * h:128 * h + 128], qn_ref[...])
            if rope is not None:
                qh = _rope(qh, *rope)
            q_ref[:, 128 * h:128 * h + 128] = (qh * scale).astype(BF16)
        ckv = ckv_raw * lax.rsqrt(jnp.mean(ckv_raw * ckv_raw, axis=-1, keepdims=True) + EPS) * kva_ref[...]
        ckv_ref[...] = ckv
        lane = lax.broadcasted_iota(jnp.int32, (1, 128), 1)
        kr = jnp.where(lane < MLA_ROPE, g3, 0.0)
        kr_ref[...] = kr
        _mla_keys(ckv, pltpu.roll(kr, MLA_NOPE, 1), wk_ref, wv_ref, kn_ref, k_ref, v_ref, rope)

    is_ctx = pl.program_id(0) < N_CTX_TILES

    @pl.when(is_ctx)
    def _():
        body(None)

    @pl.when(jnp.logical_not(is_ctx))
    def _():
        body((cos_ref[...], sa_ref[...], sb_ref[...]))


def _mla_cache_kernel(ckv_ref, kr_ref, wk_ref, wv_ref, kn_ref, k_ref, v_ref):
    _mla_keys(ckv_ref[...], pltpu.roll(kr_ref[...], MLA_NOPE, 1), wk_ref, wv_ref, kn_ref, k_ref, v_ref)


def _mla_prep(z, w, rope_tabs):
    def rope_blk(i):
        return jnp.where(i < N_CTX_TILES, 0, 1 + (i - N_CTX_TILES) % LAT_TILES)

    const = lambda shape: pl.BlockSpec(shape, lambda i: (0,) * len(shape))
    rope_spec = pl.BlockSpec((TM, HEAD_PAD), lambda i: (rope_blk(i), 0))
    row = lambda width: pl.BlockSpec((TM, width), lambda i: (i, 0))
    return pl.pallas_call(
        _mla_prep_kernel,
        out_shape=[jax.ShapeDtypeStruct((T_ALL, 1024), BF16), jax.ShapeDtypeStruct((T_ALL, 1024), BF16),
                   jax.ShapeDtypeStruct((T_ALL, 512), BF16), jax.ShapeDtypeStruct((T_ALL, 128), F32),
                   jax.ShapeDtypeStruct((T_ALL, 128), F32)],
        grid=(N_TILES,),
        in_specs=[pl.BlockSpec((TM, 512), lambda i: (i, 12)),
                  const((1, 256)), const((256, 1024)), const((1, 128)), const((1, 128)),
                  const((128, 1024)), const((128, 512)), const((1, 128)),
                  rope_spec, rope_spec, rope_spec],
        out_specs=[row(1024), row(1024), row(512), row(128), row(128)],
        compiler_params=_params(1),
        name="mla_prep",
    )(z, w["qa_g"], w["wuq"], w["qn_g"], w["kva_g"], w["wk"], w["wv"], w["kn_g"], *rope_tabs)


def _mla_cache(ckv, kr_pad, w):
    const = lambda shape: pl.BlockSpec(shape, lambda i: (0,) * len(shape))
    row = lambda width: pl.BlockSpec((TM, width), lambda i: (i, 0))
    n = ckv.shape[0]
    return pl.pallas_call(
        _mla_cache_kernel,
        out_shape=[jax.ShapeDtypeStruct((n, 1024), BF16), jax.ShapeDtypeStruct((n, 512), BF16)],
        grid=(n // TM,),
        in_specs=[row(128), row(128), const((128, 1024)), const((128, 512)), const((1, 128))],
        out_specs=[row(1024), row(512)],
        compiler_params=_params(1),
        name="mla_cache",
    )(ckv, kr_pad, w["wk"], w["wv"], w["kn_g"])


def _mla_attn_kernel(*refs, has_cache):
    q_ref, k_ref, v_ref = refs[:3]
    pos = 3
    if has_cache:
        kc_ref, vc_ref = refs[3:5]
        pos = 5
    o_ref = refs[pos]
    for p in range(MLA_HEADS // 2):
        acc = jnp.zeros((TM, 128), F32)
        for hh in range(2):
            h = 2 * p + hh
            lanes = slice(128 * h, 128 * h + 128)
            qh = q_ref[:, lanes]
            l1 = _dot_t(qh, k_ref[:, lanes])
            m = jnp.max(l1, axis=-1, keepdims=True)
            if has_cache:
                l0 = _dot_t(qh, kc_ref[:, lanes])
                m = jnp.maximum(m, jnp.max(l0, axis=-1, keepdims=True))
                p0 = jnp.exp(l0 - m)
            p1 = jnp.exp(l1 - m)
            den = jnp.sum(p1, axis=-1, keepdims=True)
            if has_cache:
                den = den + jnp.sum(p0, axis=-1, keepdims=True)
            inv = 1.0 / den
            mask = _lane_half_mask(hh)
            vp = v_ref[:, 128 * p:128 * p + 128]
            acc += _dot((p1 * inv).astype(BF16), jnp.where(mask, vp, jnp.zeros_like(vp)))
            if has_cache:
                vcp = vc_ref[:, 128 * p:128 * p + 128]
                acc += _dot((p0 * inv).astype(BF16), jnp.where(mask, vcp, jnp.zeros_like(vcp)))
        o_ref[:, 128 * p:128 * p + 128] = acc.astype(BF16)


def _mla_attn(q, k, v, kc, vc, *, ctx):
    if ctx:
        nb, seq, row_blk, tile0 = N_CTX_SEQ, CTX_LEN, 0, 0
    else:
        nb, seq, row_blk, tile0 = N_LAT_SEQ, LAT_LEN, T_CTX // LAT_LEN, N_CTX_TILES
    nq = seq // TM
    in_specs = [pl.BlockSpec((TM, 1024), lambda b, i: (tile0 + b * nq + i, 0)),
                pl.BlockSpec((seq, 1024), lambda b, i: (row_blk + b, 0)),
                pl.BlockSpec((seq, 512), lambda b, i: (row_blk + b, 0))]
    args = [q, k, v]
    if not ctx:
        in_specs += [pl.BlockSpec((TM, 1024), lambda b, i: (b, 0)), pl.BlockSpec((TM, 512), lambda b, i: (b, 0))]
        args += [kc, vc]
    return pl.pallas_call(
        functools.partial(_mla_attn_kernel, has_cache=not ctx),
        out_shape=jax.ShapeDtypeStruct((nb * seq, 512), BF16),
        grid=(nb, nq), in_specs=in_specs,
        out_specs=pl.BlockSpec((TM, 512), lambda b, i: (b * nq + i, 0)),
        compiler_params=_params(2),
        name="mla_attn_ctx" if ctx else "mla_attn_lat",
    )(*args)


def _merge_kernel(ret_ref, gla_ref, mla_ref, m0_ref, m1_ref, m2_ref, x_ref, wb_ref, wo_ref,
                  g1_ref, n2_ref, sh2_ref, sc2_ref, rwh_ref, rwl_ref, rb_ref,
                  x1_ref, h_ref, idx_ref, w_ref, rank_ref, cnt_ref):
    mix = None
    for br_ref, m_ref, n in ((ret_ref, m0_ref, 0), (gla_ref, m1_ref, 1), (mla_ref, m2_ref, 2)):
        term = _sigmoid(m_ref[...].astype(F32)) * _dot(br_ref[...], wb_ref[n])
        mix = term if mix is None else mix + term
    out = _dot(mix.astype(BF16), wo_ref[...])
    x1 = x_ref[...] + g1_ref[0] * out
    x1_ref[...] = x1
    h = x1 * lax.rsqrt(jnp.mean(x1 * x1, axis=-1, keepdims=True) + EPS) * n2_ref[...]
    h = h * (1.0 + sc2_ref[0]) + sh2_ref[0]
    h_ref[...] = h
    hh = h.astype(BF16)
    hl = (h - hh.astype(F32)).astype(BF16)
    logits = _dot(hh, rwh_ref[...]) + _dot(hh, rwl_ref[...]) + _dot(hl, rwh_ref[...]) + rb_ref[...]
    lane = lax.broadcasted_iota(jnp.int32, (TM, 128), 1)
    lanef = lane.astype(F32)
    l = jnp.where(lane < N_EXPERTS, logits, -jnp.inf)
    vals, idxs = [], []
    for _ in range(TOP_K):
        m = jnp.max(l, axis=-1, keepdims=True)
        ix = jnp.min(jnp.where(l == m, lanef, 128.0), axis=-1, keepdims=True)
        vals.append(m)
        idxs.append(ix)
        l = jnp.where(lanef == ix, -jnp.inf, l)
    es = [jnp.exp(v - vals[0]) for v in vals]
    inv = 1.0 / (es[0] + es[1] + es[2] + es[3])
    idx_out = jnp.zeros((TM, 128), F32)
    w_out = jnp.zeros((TM, 128), F32)
    for kk in range(TOP_K):
        idx_out = jnp.where(lane == kk, idxs[kk], idx_out)
        w_out = jnp.where(lane == kk, es[kk] * inv, w_out)
    idx_ref[...] = idx_out.astype(jnp.int32)
    w_ref[...] = w_out

    @pl.when(pl.program_id(0) == 0)
    def _():
        cnt_ref[...] = jnp.zeros_like(cnt_ref)
    onehots = [jnp.where(lanef == ix, 1.0, 0.0) for ix in idxs]
    osum = (onehots[0] + onehots[1]) + (onehots[2] + onehots[3])
    ri = lax.broadcasted_iota(jnp.int32, (TM, TM), 0)
    ci = lax.broadcasted_iota(jnp.int32, (TM, TM), 1)
    before = jnp.where(ri > ci, 1.0, 0.0).astype(BF16)
    prior = _dot(before, osum.astype(BF16)) + cnt_ref[0:1, :]
    rank_out = jnp.zeros((TM, 128), F32)
    for kk in range(TOP_K):
        rank_out = jnp.where(lane == kk, jnp.sum(onehots[kk] * prior, axis=-1, keepdims=True), rank_out)
    rank_ref[...] = rank_out.astype(jnp.int32)
    cnt_ref[...] = cnt_ref[...] + jnp.sum(osum, axis=0, keepdims=True)


def _merge(ret_o, gla_o, mla_o, z, x, modr, w, layer):
    mrow = _mod_row(TM)
    base = layer * N_MOD

    def mod_spec(part):
        return pl.BlockSpec((1, 1, D_MODEL), lambda i: ((base + mrow(i)) * 6 + part, 0, 0))

    const = lambda shape: pl.BlockSpec(shape, lambda i: (0,) * len(shape))
    row = lambda width: pl.BlockSpec((TM, width), lambda i: (i, 0))
    gate = lambda col: pl.BlockSpec((TM, 1024), lambda i: (i, col))
    return pl.pallas_call(
        _merge_kernel,
        out_shape=[jax.ShapeDtypeStruct((T_ALL, D_MODEL), F32), jax.ShapeDtypeStruct((T_ALL, D_MODEL), F32),
                   jax.ShapeDtypeStruct((T_ALL, 128), jnp.int32), jax.ShapeDtypeStruct((T_ALL, 128), F32),
                   jax.ShapeDtypeStruct((T_ALL, 128), jnp.int32), jax.ShapeDtypeStruct((8, 128), F32)],
        grid=(N_TILES,),
        in_specs=[row(512), row(512), row(512), gate(3), gate(4), gate(5), row(1024),
                  const((3, 512, 1024)), const((1024, 1024)),
                  mod_spec(2), const((1, 1024)), mod_spec(3), mod_spec(4),
                  const((1024, 128)), const((1024, 128)), const((1, 128))],
        out_specs=[row(1024), row(1024), row(128), row(128), row(128), const((8, 128))],
        compiler_params=_params(1),
        name="merge",
    )(ret_o, gla_o, mla_o, z, z, z, x, w["wb"], w["wo"], modr, w["n2_g"], modr, modr,
      w["rw_hi"], w["rw_lo"], w["rb"])


def _route(top_idx, rank, counts):
    flat_e = top_idx.reshape(N_SLOTS)
    onehot = (flat_e[:, None] == jnp.arange(N_EXPERTS, dtype=jnp.int32)[None, :]).astype(jnp.int32)
    padded = (counts + MOE_ROWS - 1) // MOE_ROWS * MOE_ROWS
    pad_end = jnp.cumsum(padded)
    pad_start = pad_end - padded
    dest = (rank.reshape(N_SLOTS) + jnp.sum(onehot * pad_start[None, :], axis=1)).astype(jnp.int32)
    blk_start = jnp.arange(N_MOE_BLOCKS, dtype=jnp.int32) * MOE_ROWS
    block_e = jnp.minimum(jnp.sum((pad_end[None, :] <= blk_start[:, None]).astype(jnp.int32), axis=1),
                          N_EXPERTS - 1).astype(jnp.int32)
    n_used = (pad_end[-1] // MOE_ROWS).astype(jnp.int32).reshape(1)
    tail_start = (pad_start + counts).astype(jnp.int32)
    return dest, tail_start, block_e, n_used


def _dispatch_kernel(dest_ref, tail_ref, nb_ref, h_ref, xs_ref, zero_buf, sem):
    i = pl.program_id(0)

    @pl.when(i == 0)
    def _():
        zero_buf[...] = jnp.zeros_like(zero_buf)
        fills = [pltpu.make_async_copy(
            zero_buf, xs_ref.at[pl.ds(pl.multiple_of((tail_ref[e] // 8) * 8, 8), TAIL_FILL), :], sem)
            for e in range(N_EXPERTS)]
        for f in fills:
            f.start()
        for f in fills:
            f.wait()

        def fill_block(b, c):
            f = pltpu.make_async_copy(zero_buf.at[pl.ds(0, MOE_ROWS), :],
                                      xs_ref.at[pl.ds(pl.multiple_of(b * MOE_ROWS, MOE_ROWS), MOE_ROWS), :], sem)
            f.start()
            f.wait()
            return c
        lax.fori_loop(nb_ref[0], (N_MOE_ROWS + XS_EXTRA) // MOE_ROWS, fill_block, 0)

    base = i * TM * TOP_K

    def issue(t, c):
        for kk in range(TOP_K):
            pltpu.make_async_copy(h_ref.at[pl.ds(t, 1), :],
                                  xs_ref.at[pl.ds(dest_ref[base + t * TOP_K + kk], 1), :], sem).start()
        return c
    lax.fori_loop(0, TM, issue, 0, unroll=8)
    for kk in range(TOP_K):
        pltpu.make_async_copy(h_ref, xs_ref.at[pl.ds(0, TM), :], sem).wait()


def _moe_dispatch(h, dest, tail_start, n_used):
    return pl.pallas_call(
        _dispatch_kernel,
        out_shape=jax.ShapeDtypeStruct((N_MOE_ROWS + XS_EXTRA, D_MODEL), F32),
        grid_spec=pltpu.PrefetchScalarGridSpec(
            num_scalar_prefetch=3, grid=(N_TILES,),
            in_specs=[pl.BlockSpec((TM, D_MODEL), lambda i, d, t, nb: (i, 0))],
            out_specs=pl.BlockSpec(memory_space=pl.ANY),
            scratch_shapes=[pltpu.VMEM((TAIL_FILL, D_MODEL), F32), pltpu.SemaphoreType.DMA]),
        compiler_params=_params(1),
        name="moe_dispatch",
    )(dest, tail_start, n_used, h)


def _expert_kernel(be_ref, nb_ref, x_ref, wgu_ref, bgu_ref, wd_ref, bd_ref, o_ref, wgu_bf, wd_bf):
    i = pl.program_id(0)
    e = be_ref[i]
    prev = be_ref[jnp.maximum(i - 1, 0)]

    @pl.when((i == 0) | (e != prev))
    def _():
        wgu_bf[...] = wgu_ref[0].astype(BF16)
        wd_bf[...] = wd_ref[0].astype(BF16)

    @pl.when(i < nb_ref[0])
    def _():
        gu = _dot(x_ref[...].astype(BF16), wgu_bf[...]) + bgu_ref[0]
        gate = jnp.minimum(gu[:, :D_EXPERT], SWIGLU_LIMIT)
        up = jnp.clip(gu[:, D_EXPERT:], -SWIGLU_LIMIT, SWIGLU_LIMIT)
        act = (up + 1.0) * gate * _sigmoid(SWIGLU_ALPHA * gate)
        out = _dot(act.astype(BF16), wd_bf[...]) + bd_ref[0]
        o_ref[...] = out

    @pl.when(i >= nb_ref[0])
    def _():
        o_ref[...] = jnp.zeros_like(o_ref)


def _moe_experts(xs, block_e, n_used, w_gu, b_gu, w_down, b_down, layer):
    w_idx = lambda i, be, nb: (layer * N_EXPERTS + be[i], 0, 0)
    return pl.pallas_call(
        _expert_kernel,
        out_shape=jax.ShapeDtypeStruct((N_MOE_ROWS, D_MODEL), F32),
        grid_spec=pltpu.PrefetchScalarGridSpec(
            num_scalar_prefetch=2, grid=(N_MOE_BLOCKS,),
            in_specs=[pl.BlockSpec((MOE_ROWS, D_MODEL), lambda i, be, nb: (jnp.minimum(i, nb[0] - 1), 0)),
                      pl.BlockSpec((1, D_MODEL, 2 * D_EXPERT), w_idx),
                      pl.BlockSpec((1, 1, 2 * D_EXPERT), w_idx),
                      pl.BlockSpec((1, D_EXPERT, D_MODEL), w_idx),
                      pl.BlockSpec((1, 1, D_MODEL), w_idx)],
            out_specs=pl.BlockSpec((MOE_ROWS, D_MODEL), lambda i, be, nb: (i, 0)),
            scratch_shapes=[pltpu.VMEM((D_MODEL, 2 * D_EXPERT), BF16), pltpu.VMEM((D_EXPERT, D_MODEL), BF16)]),
        compiler_params=_params(1),
        name="moe_experts",
    )(block_e, n_used, xs, w_gu, b_gu, w_down, b_down)


def _combine_kernel(dest_ref, x_ref, g2_ref, w_ref, eo_ref, y_ref, buf, sem):
    i = pl.program_id(0)
    base = i * TM * TOP_K

    def issue(t, c):
        for kk in range(TOP_K):
            pltpu.make_async_copy(eo_ref.at[pl.ds(dest_ref[base + t * TOP_K + kk], 1), :],
                                  buf.at[kk, pl.ds(t, 1), :], sem).start()
        return c
    lax.fori_loop(0, TM, issue, 0, unroll=8)

    for kk in range(TOP_K):
        pltpu.make_async_copy(eo_ref.at[pl.ds(0, TM), :], buf.at[kk], sem).wait()
    w = w_ref[...]
    ff = None
    for kk in range(TOP_K):
        term = buf[kk] * w[:, kk:kk + 1]
        ff = term if ff is None else ff + term
    y_ref[...] = x_ref[...] + g2_ref[0] * ff


def _moe_combine(dest, x1, modr, top_w, eo, layer):
    mrow = _mod_row(TM)
    base = layer * N_MOD
    return pl.pallas_call(
        _combine_kernel,
        out_shape=jax.ShapeDtypeStruct((T_ALL, D_MODEL), F32),
        grid_spec=pltpu.PrefetchScalarGridSpec(
            num_scalar_prefetch=1, grid=(N_TILES,),
            in_specs=[pl.BlockSpec((TM, D_MODEL), lambda i, d: (i, 0)),
                      pl.BlockSpec((1, 1, D_MODEL), lambda i, d: ((base + mrow(i)) * 6 + 5, 0, 0)),
                      pl.BlockSpec((TM, 128), lambda i, d: (i, 0)),
                      pl.BlockSpec(memory_space=pl.ANY)],
            out_specs=pl.BlockSpec((TM, D_MODEL), lambda i, d: (i, 0)),
            scratch_shapes=[pltpu.VMEM((TOP_K, TM, D_MODEL), F32), pltpu.SemaphoreType.DMA]),
        compiler_params=_params(1),
        name="moe_combine",
    )(dest, x1, modr, top_w, eo)


def _pad_heads(w, n_heads, width):
    lead = w.shape[:-1]
    w = w.reshape(lead + (n_heads, width))
    w = jnp.pad(w, [(0, 0)] * len(lead) + [(0, 0), (0, HEAD_PAD - width)])
    return w.reshape(lead + (n_heads * HEAD_PAD,))


def _layer_weights(l, w_in, gla_wa2, gla_ba, mla_qa_g, mla_wuq, mla_kva_g, mla_wukv, mla_qn_g, mla_kn_g,
                   w_branch, w_out, router_w, router_b, norm2_g):
    wi = w_in[l]
    w_in_p = jnp.concatenate([wi[:, :3072], wi[:, 3520:], wi[:, 3104:3520], wi[:, 3072:3104],
                              jnp.zeros((D_MODEL, 64), F32)], axis=1).astype(BF16)
    wa_p = jnp.zeros((512, 512), F32)
    wa_p = wa_p.at[416:432, 0:256].set(gla_wa2[l, 0]).at[432:448, 256:512].set(gla_wa2[l, 1]).astype(BF16)
    ba_p = gla_ba[l].reshape(1, 512)
    wukv = mla_wukv[l].reshape(128, MLA_HEADS, MLA_NOPE + MLA_V)
    rw = jnp.pad(router_w[l], ((0, 0), (0, 128 - N_EXPERTS)))
    rw_hi = rw.astype(BF16)
    return {
        "w_in": w_in_p, "wa": wa_p, "ba": ba_p,
        "qa_g": mla_qa_g[l].reshape(1, 256),
        "wuq": _pad_heads(mla_wuq[l], MLA_HEADS, MLA_QK).astype(BF16),
        "qn_g": jnp.pad(mla_qn_g[l], (0, HEAD_PAD - MLA_QK)).reshape(1, 128),
        "kn_g": jnp.pad(mla_kn_g[l], (0, HEAD_PAD - MLA_QK)).reshape(1, 128),
        "kva_g": mla_kva_g[l].reshape(1, 128),
        "wk": _pad_heads(wukv[:, :, :MLA_NOPE].reshape(128, MLA_HEADS * MLA_NOPE), MLA_HEADS, MLA_NOPE).astype(BF16),
        "wv": wukv[:, :, MLA_NOPE:].reshape(128, MLA_HEADS * MLA_V).astype(BF16),
        "wb": w_branch[l].astype(BF16), "wo": w_out[l].astype(BF16),
        "rw_hi": rw_hi, "rw_lo": (rw - rw_hi.astype(F32)).astype(BF16),
        "rb": jnp.pad(router_b[l], (0, 128 - N_EXPERTS)).reshape(1, 128),
        "n2_g": norm2_g[l].reshape(1, D_MODEL),
    }


def kernel(x_prompt, x_sample, cache_mla_ckv, cache_mla_krope, state_ret, state_gla, c, c_ctx, w_mod, b_mod, norm1_g, norm2_g, w_in, ret_gn_g, gla_wa2, gla_ba, gla_norm_g, mla_qa_g, mla_wuq, mla_kva_g, mla_wukv, mla_qn_g, mla_kn_g, w_branch, w_out, router_w, router_b, moe_w_gu, moe_b_gu, moe_w_down, moe_b_down):
    x = jnp.concatenate([x_prompt.reshape(T_CTX, D_MODEL), x_sample.reshape(T_LAT, D_MODEL)], axis=0)
    cc = jnp.concatenate([c_ctx[None, :], c, jnp.zeros((N_MOD - 1 - N_LAT_SEQ, D_MODEL), F32)], axis=0)
    modr = _modulation(cc, w_mod, b_mod).reshape(DEPTH * N_MOD * 6, 1, D_MODEL)
    rope_tabs = _rope_tables()
    w_gu = moe_w_gu.reshape(DEPTH * N_EXPERTS, D_MODEL, 2 * D_EXPERT)
    b_gu = moe_b_gu.reshape(DEPTH * N_EXPERTS, 1, 2 * D_EXPERT)
    w_dn = moe_w_down.reshape(DEPTH * N_EXPERTS, D_EXPERT, D_MODEL)
    b_dn = moe_b_down.reshape(DEPTH * N_EXPERTS, 1, D_MODEL)

    ckv_l, krope_l, ret_l, gla_l = [], [], [], []
    for l in range(DEPTH):
        w = _layer_weights(l, w_in, gla_wa2, gla_ba, mla_qa_g, mla_wuq, mla_kva_g, mla_wukv, mla_qn_g, mla_kn_g,
                           w_branch, w_out, router_w, router_b, norm2_g)
        z = _in_proj(x, norm1_g[l].reshape(1, D_MODEL), modr, w["w_in"], l)

        gn = ret_gn_g[l].reshape(1, 512)
        ret_c, ret_state = _retention(z, gn, None, ctx=True)
        (ret_s,) = _retention(z, gn, state_ret[:, l], ctx=False)
        gng = gla_norm_g[l].reshape(1, 512)
        gla_c, gla_state = _gla(z, w["wa"], w["ba"], gng, None, ctx=True)
        gla_s, _ = _gla(z, w["wa"], w["ba"], gng, state_gla[:, l], ctx=False)

        q, k, v, ckv, kr = _mla_prep(z, w, rope_tabs)
        kc, vc = _mla_cache(cache_mla_ckv[:, l].reshape(N_LAT_SEQ * CTX_LEN, 128),
                            jnp.pad(cache_mla_krope[:, l].reshape(N_LAT_SEQ * CTX_LEN, MLA_ROPE),
                                    ((0, 0), (0, 128 - MLA_ROPE))), w)
        mla_c = _mla_attn(q, k, v, None, None, ctx=True)
        mla_s = _mla_attn(q, k, v, kc, vc, ctx=False)

        x1, h2, top_idx, top_w, rank, cnt = _merge(jnp.concatenate([ret_c, ret_s], axis=0),
                                        jnp.concatenate([gla_c, gla_s], axis=0),
                                        jnp.concatenate([mla_c, mla_s], axis=0), z, x, modr, w, l)
        dest, tail_start, block_e, n_used = _route(top_idx[:, :TOP_K], rank[:, :TOP_K],
                                                   cnt[0, :N_EXPERTS].astype(jnp.int32))
        xs = _moe_dispatch(h2, dest, tail_start, n_used)
        eo = _moe_experts(xs, block_e, n_used, w_gu, b_gu, w_dn, b_dn, l)
        x = _moe_combine(dest, x1, modr, top_w, eo, l)

        ckv_l.append(ckv[:T_CTX].reshape(N_CTX_SEQ, CTX_LEN, 128))
        krope_l.append(kr[:T_CTX, :MLA_ROPE].reshape(N_CTX_SEQ, CTX_LEN, MLA_ROPE))
        ret_l.append(ret_state)
        gla_l.append(gla_state.reshape(N_CTX_SEQ, 2, N_HEADS, DK, DV))

    y_p = x[:T_CTX].reshape(N_CTX_SEQ, CTX_LEN, D_MODEL)
    y_s = x[T_CTX:].reshape(N_LAT_SEQ, LAT_LEN, D_MODEL)
    return (y_p, y_s, jnp.stack(ckv_l, axis=1), jnp.stack(krope_l, axis=1),
            jnp.stack(ret_l, axis=1), jnp.stack(gla_l, axis=1))
```

```python
import functools

import jax
import jax.numpy as jnp
import numpy as np
from jax import lax
from jax.experimental import pallas as pl
from jax.experimental.pallas import tpu as pltpu

F32 = jnp.float32
BF16 = jnp.bfloat16

D_MODEL = 1024
DEPTH = 2
N_CTX_SEQ, CTX_LEN = 32, 256
N_LAT_SEQ, LAT_LEN = 4, 1024
T_CTX = N_CTX_SEQ * CTX_LEN
T_LAT = N_LAT_SEQ * LAT_LEN
T_ALL = T_CTX + T_LAT
TM = 256
N_TILES = T_ALL // TM
N_CTX_TILES = T_CTX // TM
LAT_TILES = LAT_LEN // TM
N_MOD = 8
EPS = 1e-6

N_HEADS = 4
DK, DV = 64, 128
GRID_W = 64
MLA_HEADS, MLA_NOPE, MLA_ROPE, MLA_V = 8, 64, 32, 64
MLA_QK = MLA_NOPE + MLA_ROPE
HEAD_PAD = 128
GLA_TAU = 16.0
N_EXPERTS, TOP_K, D_EXPERT = 32, 4, 1024
SWIGLU_LIMIT, SWIGLU_ALPHA = 7.0, 1.702
MOE_ROWS = 256
N_SLOTS = T_ALL * TOP_K
N_MOE_BLOCKS = N_SLOTS // MOE_ROWS + N_EXPERTS
N_MOE_ROWS = N_MOE_BLOCKS * MOE_ROWS
TAIL_FILL = MOE_ROWS + 8
XS_EXTRA = 2 * MOE_ROWS

DZ = 6656
IN_TILE = 512
VMEM_LIMIT = 56 * 1024 * 1024

RET_LOG_F = [float(np.log1p(-np.exp2(-(5.0 + h)))) for h in range(N_HEADS)]
RET_LOG_B = [float(np.log1p(-np.exp2(-(5.5 + h)))) for h in range(N_HEADS)]


def _params(n_axes, vmem=VMEM_LIMIT):
    return pltpu.CompilerParams(dimension_semantics=("arbitrary",) * n_axes, vmem_limit_bytes=vmem)


def _sigmoid(x):
    return 1.0 / (1.0 + jnp.exp(-x))


def _dot(a, b):
    return jnp.dot(a, b, preferred_element_type=F32)


def _dot_t(a, b):
    return lax.dot_general(a, b, (((1,), (1,)), ((), ())), preferred_element_type=F32)


def _mod_row(tile_rows):
    def f(i):
        r0 = i * tile_rows
        return jnp.where(r0 < T_CTX, 0, 1 + (r0 - T_CTX) // LAT_LEN)
    return f


def _ctx_spec(rows, width):
    n_ctx = T_CTX // rows
    return pl.BlockSpec((rows, width), lambda i: (jnp.minimum(i, n_ctx - 1), 0))


def _lat_spec(rows, width):
    n_ctx = T_CTX // rows
    return pl.BlockSpec((rows, width), lambda i: (jnp.maximum(i - n_ctx, 0), 0))


def _mod_kernel(c_ref, w_ref, b_ref, o_ref):
    c = c_ref[...]
    s = c * _sigmoid(c)
    o_ref[0] = jnp.dot(s, w_ref[0], preferred_element_type=F32, precision=lax.Precision.HIGHEST) + b_ref[0]


def _modulation(cc, w_mod, b_mod):
    n = 6 * D_MODEL
    blk = 1024
    return pl.pallas_call(
        _mod_kernel,
        out_shape=jax.ShapeDtypeStruct((DEPTH, N_MOD, n), F32),
        grid=(DEPTH, n // blk),
        in_specs=[pl.BlockSpec((N_MOD, D_MODEL), lambda l, j: (0, 0)),
                  pl.BlockSpec((1, D_MODEL, blk), lambda l, j: (l, 0, j)),
                  pl.BlockSpec((1, 1, blk), lambda l, j: (l, 0, j))],
        out_specs=pl.BlockSpec((1, N_MOD, blk), lambda l, j: (l, 0, j)),
        compiler_params=_params(2),
        name="modulation",
    )(cc, w_mod, b_mod.reshape(DEPTH, 1, n))


def _in_kernel(xc_ref, xl_ref, g_ref, sh_ref, sc_ref, w_ref, o_ref):
    x = jnp.where(pl.program_id(0) < T_CTX // IN_TILE, xc_ref[...], xl_ref[...])
    h = x * lax.rsqrt(jnp.mean(x * x, axis=-1, keepdims=True) + EPS) * g_ref[...]
    h = h * (1.0 + sc_ref[0]) + sh_ref[0]
    hb = h.astype(BF16)
    for n0 in range(0, DZ, 512):
        o_ref[:, n0:n0 + 512] = _dot(hb, w_ref[:, n0:n0 + 512]).astype(BF16)


def _in_proj(xc, xl, g, modr, w_in_p, layer):
    mrow = _mod_row(IN_TILE)
    base = layer * N_MOD

    def mod_spec(part):
        return pl.BlockSpec((1, 1, D_MODEL), lambda i: ((base + mrow(i)) * 6 + part, 0, 0))

    return pl.pallas_call(
        _in_kernel,
        out_shape=jax.ShapeDtypeStruct((T_ALL, DZ), BF16),
        grid=(T_ALL // IN_TILE,),
        in_specs=[_ctx_spec(IN_TILE, D_MODEL), _lat_spec(IN_TILE, D_MODEL),
                  pl.BlockSpec((1, D_MODEL), lambda i: (0, 0)),
                  mod_spec(0), mod_spec(1),
                  pl.BlockSpec((D_MODEL, DZ), lambda i: (0, 0))],
        out_specs=pl.BlockSpec((IN_TILE, DZ), lambda i: (i, 0)),
        compiler_params=_params(1),
        name="in_proj",
    )(xc, xl, g, modr, modr, w_in_p)


def _lane_half_mask(hh):
    lane = lax.broadcasted_iota(jnp.int32, (1, 128), 1)
    return (lane < 64) if hh == 0 else (lane >= 64)


def _ret_kernel(*refs, seq, has_state, emit_state):
    q_ref, k_ref, v_ref, g_ref, gn_ref = refs[:5]
    pos = 5
    if has_state:
        s0_ref = refs[pos]
        pos += 1
    o_ref = refs[pos]
    pos += 1
    if emit_state:
        st_ref = refs[pos]

    r0 = pl.multiple_of(pl.program_id(1) * TM, TM)
    qb = q_ref[pl.ds(r0, TM), :]
    rowi = lax.broadcasted_iota(jnp.int32, (TM, seq), 0) + r0
    colj = lax.broadcasted_iota(jnp.int32, (TM, seq), 1)
    diff = (rowi - colj).astype(F32)
    on_diag = rowi == colj
    ri = (lax.broadcasted_iota(jnp.int32, (TM, 1), 0) + r0).astype(F32)

    for h in range(N_HEADS):
        p, hh = h // 2, h % 2
        lanes = slice(128 * p, 128 * p + 128)
        qp = qb[:, lanes]
        qh = jnp.where(_lane_half_mask(hh), qp, jnp.zeros_like(qp))
        sc = _dot_t(qh, k_ref[:, lanes])
        ex = jnp.where(diff > 0, RET_LOG_F[h] * diff, -RET_LOG_B[h] * diff)
        dec = jnp.exp(ex) * jnp.where(on_diag, 2.0 * DK ** -0.5, DK ** -0.5)
        o = _dot((sc * dec).astype(BF16), v_ref[:, 128 * h:128 * h + 128])
        if has_state:
            qf = qh.astype(F32)
            o += _dot((qf * jnp.exp(RET_LOG_F[h] * (ri + 1.0))).astype(BF16), s0_ref[0, 0, p].astype(BF16))
            o += _dot((qf * jnp.exp(RET_LOG_B[h] * (seq - ri))).astype(BF16), s0_ref[0, 1, p].astype(BF16))
        mu = jnp.mean(o, axis=-1, keepdims=True)
        d = o - mu
        var = jnp.mean(d * d, axis=-1, keepdims=True)
        on = d * lax.rsqrt(var + EPS)
        g = g_ref[:, 128 * h:128 * h + 128].astype(F32)
        out = on * gn_ref[:, 128 * h:128 * h + 128] * (g * _sigmoid(g))
        o_ref[:, 128 * h:128 * h + 128] = out.astype(BF16)

    if emit_state:
        jc = lax.broadcasted_iota(jnp.int32, (seq, 1), 0).astype(F32)
        lane = lax.broadcasted_iota(jnp.int32, (1, 128), 1)
        for p in range(2):
            kp = k_ref[:, 128 * p:128 * p + 128].astype(F32) * DK ** -0.5
            lgf = jnp.where(lane < 64, RET_LOG_F[2 * p], RET_LOG_F[2 * p + 1])
            lgb = jnp.where(lane < 64, RET_LOG_B[2 * p], RET_LOG_B[2 * p + 1])
            kdf = (kp * jnp.exp(lgf * (seq - 1.0 - jc))).T.astype(BF16)
            kdb = (kp * jnp.exp(lgb * jc)).T.astype(BF16)
            for hh in range(2):
                h = 2 * p + hh
                vh = v_ref[:, 128 * h:128 * h + 128]
                st_ref[0, 0, h] = _dot(kdf, vh)[64 * hh:64 * hh + 64, :]
                st_ref[0, 1, h] = _dot(kdb, vh)[64 * hh:64 * hh + 64, :]


def _retention(z, gn_g, s0, *, ctx):
    if ctx:
        nb, seq, row_blk, tile0 = N_CTX_SEQ, CTX_LEN, 0, 0
    else:
        nb, seq, row_blk, tile0 = N_LAT_SEQ, LAT_LEN, T_CTX // LAT_LEN, N_CTX_TILES
    nq = seq // TM
    in_specs = [pl.BlockSpec((seq, 256), lambda b, i: (row_blk + b, 0)),
                pl.BlockSpec((seq, 256), lambda b, i: (row_blk + b, 1)),
                pl.BlockSpec((seq, 512), lambda b, i: (row_blk + b, 1)),
                pl.BlockSpec((TM, 512), lambda b, i: (tile0 + b * nq + i, 2)),
                pl.BlockSpec((1, 512), lambda b, i: (0, 0))]
    args = [z, z, z, z, gn_g]
    out_shape = [jax.ShapeDtypeStruct((nb * seq, 512), BF16)]
    out_specs = [pl.BlockSpec((TM, 512), lambda b, i: (b * nq + i, 0))]
    if not ctx:
        in_specs.append(pl.BlockSpec((1, 2, 2, 128, 128), lambda b, i: (b, 0, 0, 0, 0)))
        args.append(s0.reshape(N_LAT_SEQ, 2, 2, 128, 128))
    else:
        out_shape.append(jax.ShapeDtypeStruct((nb, 2, N_HEADS, DK, DV), F32))
        out_specs.append(pl.BlockSpec((1, 2, N_HEADS, DK, DV), lambda b, i: (b, 0, 0, 0, 0)))
    return pl.pallas_call(
        functools.partial(_ret_kernel, seq=seq, has_state=not ctx, emit_state=ctx),
        out_shape=out_shape, grid=(nb, nq), in_specs=in_specs, out_specs=out_specs,
        compiler_params=_params(2),
        name="retention_ctx" if ctx else "retention_lat",
    )(*args)


def _gla_decay(small_ref, wa_ref, ba_ref):
    x = _dot(small_ref[...], wa_ref[...]) + ba_ref[...]
    la = -(jnp.maximum(-x, 0.0) + jnp.log(1.0 + jnp.exp(-jnp.abs(x)))) * (1.0 / GLA_TAU)
    ri = lax.broadcasted_iota(jnp.int32, (TM, TM), 0)
    ci = lax.broadcasted_iota(jnp.int32, (TM, TM), 1)
    ltri = jnp.where(ri >= ci, 1.0, 0.0).astype(BF16)
    hi = la.astype(BF16)
    r1 = la - hi.astype(F32)
    mid = r1.astype(BF16)
    lo = (r1 - mid.astype(F32)).astype(BF16)
    cum = _dot(ltri, hi) + _dot(ltri, mid) + _dot(ltri, lo)
    return la, cum


def _gla_state_kernel(k_ref, v_ref, small_ref, wa_ref, ba_ref, kv_ref, tot_ref):
    la, cum = _gla_decay(small_ref, wa_ref, ba_ref)
    bf, bb = cum[:, :256], cum[:, 256:]
    xb = bb - la[:, 256:]
    k = k_ref[...].astype(F32)
    kdf = k * jnp.exp(bf[TM - 1:TM, :] - bf)
    kdb = k * jnp.exp(xb)
    for p in range(2):
        kf_t = kdf[:, 128 * p:128 * p + 128].T.astype(BF16)
        kb_t = kdb[:, 128 * p:128 * p + 128].T.astype(BF16)
        for hh in range(2):
            h = 2 * p + hh
            vh = v_ref[:, 128 * h:128 * h + 128]
            kv_ref[0, 0, 0, h] = _dot(kf_t, vh)[64 * hh:64 * hh + 64, :]
            kv_ref[0, 0, 1, h] = _dot(kb_t, vh)[64 * hh:64 * hh + 64, :]
    tot_ref[0, 0] = jnp.sum(la.T, axis=-1, keepdims=True)


def _mid_bcast(x, s, r):
    w = 2 * s
    if w >= 8:
        n = TM // w
        x3 = x.reshape(n, w, 256)
        return jnp.broadcast_to(x3[:, r:r + 1, :], (n, w, 256)).reshape(TM, 256)
    x3 = x.reshape(TM // 8, 8, 256)
    sub = lax.broadcasted_iota(jnp.int32, (1, 8, 1), 1)
    out = None
    for blk in range(8 // w):
        rowv = jnp.broadcast_to(x3[:, blk * w + r:blk * w + r + 1, :], (TM // 8, 8, 256))
        out = rowv if out is None else jnp.where(sub >= blk * w, rowv, out)
    return out.reshape(TM, 256)


def _gla_kernel(*refs, n_blk, has_state):
    q_ref, k_ref, v_ref, g_ref, small_ref, wa_ref, ba_ref, gn_ref = refs[:8]
    pos = 8
    if has_state:
        kv_ref, tot_ref, s0_ref = refs[pos:pos + 3]
        pos += 3
    o_ref = refs[pos]

    la, cum = _gla_decay(small_ref, wa_ref, ba_ref)
    bf, bb = cum[:, :256], cum[:, 256:]
    xb = bb - la[:, 256:]
    q = q_ref[...].astype(F32) * DK ** -0.5
    k = k_ref[...].astype(F32)
    row = lax.broadcasted_iota(jnp.int32, (TM, 1), 0)
    rowi = lax.broadcasted_iota(jnp.int32, (TM, TM), 0)
    colj = lax.broadcasted_iota(jnp.int32, (TM, TM), 1)
    masks = [_lane_half_mask(0), _lane_half_mask(1)]

    def head_q(x, h):
        xp = x[:, 128 * (h // 2):128 * (h // 2) + 128]
        return jnp.where(masks[h % 2], xp, jnp.zeros_like(xp))

    qb16, kb16 = q.astype(BF16), k.astype(BF16)
    acc = []
    for h in range(N_HEADS):
        p = h // 2
        sd = _dot_t(head_q(qb16, h), kb16[:, 128 * p:128 * p + 128])
        acc.append(jnp.where(rowi == colj, 2.0 * sd, 0.0))

    s = 1
    while s < TM:
        upper = ((row // s) % 2) == 1
        mf = _mid_bcast(bf, s, s - 1)
        mb = _mid_bcast(xb, s, s)
        ef = jnp.exp(jnp.where(upper, bf - mf, mf - bf))
        eb = jnp.exp(jnp.where(upper, xb - mb, mb - xb))
        qf = jnp.where(upper, q * ef, 0.0).astype(BF16)
        kf = jnp.where(upper, 0.0, k * ef).astype(BF16)
        qbk = jnp.where(upper, 0.0, q * eb).astype(BF16)
        kbk = jnp.where(upper, k * eb, 0.0).astype(BF16)
        same = (rowi // (2 * s)) == (colj // (2 * s))
        for h in range(N_HEADS):
            p = h // 2
            lanes = slice(128 * p, 128 * p + 128)
            sl = _dot_t(head_q(qf, h), kf[:, lanes]) + _dot_t(head_q(qbk, h), kbk[:, lanes])
            acc[h] = acc[h] + jnp.where(same, sl, 0.0)
        s *= 2

    if has_state:
        n = pl.program_id(1)
        qsf = (q * jnp.exp(bf)).astype(BF16)
        qsb = (q * jnp.exp(bb[TM - 1:TM, :] - xb)).astype(BF16)

    for h in range(N_HEADS):
        o = _dot(acc[h].astype(BF16), v_ref[:, 128 * h:128 * h + 128])
        if has_state:
            sf = s0_ref[0, 0, h]
            for m in range(n_blk - 1):
                dec = jnp.exp(tot_ref[0, m, 64 * h:64 * h + 64, :])
                sf = jnp.where(m < n, dec * sf + kv_ref[0, m, 0, h], sf)
            sb = s0_ref[0, 1, h]
            for m in range(n_blk - 1, 0, -1):
                dec = jnp.exp(tot_ref[0, m, 256 + 64 * h:256 + 64 * h + 64, :])
                sb = jnp.where(m > n, dec * sb + kv_ref[0, m, 1, h], sb)
            zero = jnp.zeros((DK, DV), F32)
            hh = h % 2
            sf2 = jnp.concatenate([sf, zero] if hh == 0 else [zero, sf], axis=0).astype(BF16)
            sb2 = jnp.concatenate([sb, zero] if hh == 0 else [zero, sb], axis=0).astype(BF16)
            lanes = slice(128 * (h // 2), 128 * (h // 2) + 128)
            o += _dot(qsf[:, lanes], sf2) + _dot(qsb[:, lanes], sb2)
        on = o * lax.rsqrt(jnp.mean(o * o, axis=-1, keepdims=True) + EPS)
        g = g_ref[:, 128 * h:128 * h + 128].astype(F32)
        out = on * gn_ref[:, 128 * h:128 * h + 128] * (g * _sigmoid(g))
        o_ref[:, 128 * h:128 * h + 128] = out.astype(BF16)


def _gla(z, wa_p, ba_p, gn_g, s0, *, ctx):
    if ctx:
        nb, n_blk, tile0 = N_CTX_SEQ, 1, 0
    else:
        nb, n_blk, tile0 = N_LAT_SEQ, LAT_TILES, N_CTX_TILES

    def zspec(width, col):
        return pl.BlockSpec((TM, width), lambda b, n: (tile0 + b * n_blk + n, col))

    w_specs = [pl.BlockSpec((512, 512), lambda b, n: (0, 0)), pl.BlockSpec((1, 512), lambda b, n: (0, 0))]
    kv, tot = pl.pallas_call(
        _gla_state_kernel,
        out_shape=[jax.ShapeDtypeStruct((nb, n_blk, 2, N_HEADS, DK, DV), F32),
                   jax.ShapeDtypeStruct((nb, n_blk, 512, 1), F32)],
        grid=(nb, n_blk),
        in_specs=[zspec(256, 7), zspec(512, 4), zspec(512, 12)] + w_specs,
        out_specs=[pl.BlockSpec((1, 1, 2, N_HEADS, DK, DV), lambda b, n: (b, n, 0, 0, 0, 0)),
                   pl.BlockSpec((1, 1, 512, 1), lambda b, n: (b, n, 0, 0))],
        compiler_params=_params(2),
        name="gla_state_ctx" if ctx else "gla_state_lat",
    )(z, z, z, wa_p, ba_p)

    in_specs = [zspec(256, 6), zspec(256, 7), zspec(512, 4), zspec(512, 5), zspec(512, 12)] + w_specs
    in_specs.append(pl.BlockSpec((1, 512), lambda b, n: (0, 0)))
    args = [z, z, z, z, z, wa_p, ba_p, gn_g]
    if not ctx:
        in_specs += [pl.BlockSpec((1, n_blk, 2, N_HEADS, DK, DV), lambda b, n: (b, 0, 0, 0, 0, 0)),
                     pl.BlockSpec((1, n_blk, 512, 1), lambda b, n: (b, 0, 0, 0)),
                     pl.BlockSpec((1, 2, N_HEADS, DK, DV), lambda b, n: (b, 0, 0, 0, 0))]
        args += [kv, tot, s0]
    out = pl.pallas_call(
        functools.partial(_gla_kernel, n_blk=n_blk, has_state=not ctx),
        out_shape=jax.ShapeDtypeStruct((nb * n_blk * TM, 512), BF16),
        grid=(nb, n_blk), in_specs=in_specs,
        out_specs=pl.BlockSpec((TM, 512), lambda b, n: (b * n_blk + n, 0)),
        compiler_params=_params(2),
        name="gla_ctx" if ctx else "gla_lat",
    )(*args)
    return out, kv


def _rope_tables():
    nf = MLA_ROPE // 4
    pos = np.arange(LAT_LEN)
    freqs = (10000.0 ** (-np.arange(nf, dtype=np.float32) / nf)).astype(np.float32)
    ang_r = ((pos // GRID_W).astype(np.float32)[:, None] * freqs).astype(np.float32)
    ang_c = ((pos % GRID_W).astype(np.float32)[:, None] * freqs).astype(np.float32)
    cos = np.ones((TM + LAT_LEN, HEAD_PAD), np.float32)
    sa = np.zeros((TM + LAT_LEN, HEAD_PAD), np.float32)
    sb = np.zeros((TM + LAT_LEN, HEAD_PAD), np.float32)
    o = MLA_NOPE
    for base, ang in ((o, ang_r), (o + 2 * nf, ang_c)):
        cos[TM:, base:base + nf] = np.cos(ang)
        cos[TM:, base + nf:base + 2 * nf] = np.cos(ang)
        sa[TM:, base:base + nf] = -np.sin(ang)
        sb[TM:, base + nf:base + 2 * nf] = np.sin(ang)
    return jnp.asarray(cos), jnp.asarray(sa), jnp.asarray(sb)


def _rope(x, cos, sa, sb):
    return x * cos + pltpu.roll(x, 128 - 8, 1) * sa + pltpu.roll(x, 8, 1) * sb


def _head_norm(x, g):
    return x * lax.rsqrt(jnp.sum(x * x, axis=-1, keepdims=True) * (1.0 / MLA_QK) + EPS) * g


def _mla_keys(ckv, kr_tile, wk_ref, wv_ref, kn_ref, k_ref, v_ref, rope=None):
    cb = ckv.astype(BF16)
    kpre = _dot(cb, wk_ref[...])
    v_ref[...] = _dot(cb, wv_ref[...]).astype(BF16)
    for h in range(MLA_HEADS):
        kh = _head_norm(kpre[:, 128 * h:128 * h + 128] + kr_tile, kn_ref[...])
        if rope is not None:
            kh = _rope(kh, *rope)
        k_ref[:, 128 * h:128 * h + 128] = kh.astype(BF16)


def _mla_prep_kernel(small_ref, qa_ref, wuq_ref, qn_ref, kva_ref, wk_ref, wv_ref, kn_ref,
                     cos_ref, sa_ref, sb_ref, q_ref, k_ref, v_ref, ckv_ref, kr_ref):
    def body(rope):
        sm = small_ref[...].astype(F32)
        cq, ckv_raw, g3 = sm[:, 0:256], sm[:, 256:384], sm[:, 384:512]
        cqn = cq * lax.rsqrt(jnp.mean(cq * cq, axis=-1, keepdims=True) + EPS) * qa_ref[...]
        q = _dot(cqn.astype(BF16), wuq_ref[...])
        scale = MLA_QK ** -0.5
        for h in range(MLA_HEADS):
            qh = _head_norm(q[:, 128 * h:128 * h + 128], qn_ref[...])
            if rope is not None:
                qh = _rope(qh, *rope)
            q_ref[:, 128 * h:128 * h + 128] = (qh * scale).astype(BF16)
        ckv = ckv_raw * lax.rsqrt(jnp.mean(ckv_raw * ckv_raw, axis=-1, keepdims=True) + EPS) * kva_ref[...]
        ckv_ref[...] = ckv
        lane = lax.broadcasted_iota(jnp.int32, (1, 128), 1)
        kr = jnp.where(lane < MLA_ROPE, g3, 0.0)
        kr_ref[...] = kr
        _mla_keys(ckv, pltpu.roll(kr, MLA_NOPE, 1), wk_ref, wv_ref, kn_ref, k_ref, v_ref, rope)

    is_ctx = pl.program_id(0) < N_CTX_TILES

    @pl.when(is_ctx)
    def _():
        body(None)

    @pl.when(jnp.logical_not(is_ctx))
    def _():
        body((cos_ref[...], sa_ref[...], sb_ref[...]))


def _mla_cache_kernel(ckv_ref, kr_ref, wk_ref, wv_ref, kn_ref, k_ref, v_ref):
    _mla_keys(ckv_ref[...], pltpu.roll(kr_ref[...], MLA_NOPE, 1), wk_ref, wv_ref, kn_ref, k_ref, v_ref)


def _mla_prep(z, w, rope_tabs):
    def rope_blk(i):
        return jnp.where(i < N_CTX_TILES, 0, 1 + (i - N_CTX_TILES) % LAT_TILES)

    const = lambda shape: pl.BlockSpec(shape, lambda i: (0,) * len(shape))
    rope_spec = pl.BlockSpec((TM, HEAD_PAD), lambda i: (rope_blk(i), 0))
    row = lambda width: pl.BlockSpec((TM, width), lambda i: (i, 0))
    return pl.pallas_call(
        _mla_prep_kernel,
        out_shape=[jax.ShapeDtypeStruct((T_ALL, 1024), BF16), jax.ShapeDtypeStruct((T_ALL, 1024), BF16),
                   jax.ShapeDtypeStruct((T_ALL, 512), BF16), jax.ShapeDtypeStruct((T_ALL, 128), F32),
                   jax.ShapeDtypeStruct((T_ALL, 128), F32)],
        grid=(N_TILES,),
        in_specs=[pl.BlockSpec((TM, 512), lambda i: (i, 12)),
                  const((1, 256)), const((256, 1024)), const((1, 128)), const((1, 128)),
                  const((128, 1024)), const((128, 512)), const((1, 128)),
                  rope_spec, rope_spec, rope_spec],
        out_specs=[row(1024), row(1024), row(512), row(128), row(128)],
        compiler_params=_params(1),
        name="mla_prep",
    )(z, w["qa_g"], w["wuq"], w["qn_g"], w["kva_g"], w["wk"], w["wv"], w["kn_g"], *rope_tabs)


def _mla_cache(ckv, kr_pad, w):
    const = lambda shape: pl.BlockSpec(shape, lambda i: (0,) * len(shape))
    row = lambda width: pl.BlockSpec((TM, width), lambda i: (i, 0))
    n = ckv.shape[0]
    return pl.pallas_call(
        _mla_cache_kernel,
        out_shape=[jax.ShapeDtypeStruct((n, 1024), BF16), jax.ShapeDtypeStruct((n, 512), BF16)],
        grid=(n // TM,),
        in_specs=[row(128), row(128), const((128, 1024)), const((128, 512)), const((1, 128))],
        out_specs=[row(1024), row(512)],
        compiler_params=_params(1),
        name="mla_cache",
    )(ckv, kr_pad, w["wk"], w["wv"], w["kn_g"])


def _mla_attn_kernel(*refs, has_cache):
    q_ref, k_ref, v_ref = refs[:3]
    pos = 3
    if has_cache:
        kc_ref, vc_ref = refs[3:5]
        pos = 5
    o_ref = refs[pos]
    for p in range(MLA_HEADS // 2):
        acc = jnp.zeros((TM, 128), F32)
        for hh in range(2):
            h = 2 * p + hh
            lanes = slice(128 * h, 128 * h + 128)
            qh = q_ref[:, lanes]
            l1 = _dot_t(qh, k_ref[:, lanes])
            m = jnp.max(l1, axis=-1, keepdims=True)
            if has_cache:
                l0 = _dot_t(qh, kc_ref[:, lanes])
                m = jnp.maximum(m, jnp.max(l0, axis=-1, keepdims=True))
                p0 = jnp.exp(l0 - m)
            p1 = jnp.exp(l1 - m)
            den = jnp.sum(p1, axis=-1, keepdims=True)
            if has_cache:
                den = den + jnp.sum(p0, axis=-1, keepdims=True)
            inv = 1.0 / den
            mask = _lane_half_mask(hh)
            vp = v_ref[:, 128 * p:128 * p + 128]
            acc += _dot((p1 * inv).astype(BF16), jnp.where(mask, vp, jnp.zeros_like(vp)))
            if has_cache:
                vcp = vc_ref[:, 128 * p:128 * p + 128]
                acc += _dot((p0 * inv).astype(BF16), jnp.where(mask, vcp, jnp.zeros_like(vcp)))
        o_ref[:, 128 * p:128 * p + 128] = acc.astype(BF16)


def _mla_attn(q, k, v, kc, vc, *, ctx):
    if ctx:
        nb, seq, row_blk, tile0 = N_CTX_SEQ, CTX_LEN, 0, 0
    else:
        nb, seq, row_blk, tile0 = N_LAT_SEQ, LAT_LEN, T_CTX // LAT_LEN, N_CTX_TILES
    nq = seq // TM
    in_specs = [pl.BlockSpec((TM, 1024), lambda b, i: (tile0 + b * nq + i, 0)),
                pl.BlockSpec((seq, 1024), lambda b, i: (row_blk + b, 0)),
                pl.BlockSpec((seq, 512), lambda b, i: (row_blk + b, 0))]
    args = [q, k, v]
    if not ctx:
        in_specs += [pl.BlockSpec((TM, 1024), lambda b, i: (b, 0)), pl.BlockSpec((TM, 512), lambda b, i: (b, 0))]
        args += [kc, vc]
    return pl.pallas_call(
        functools.partial(_mla_attn_kernel, has_cache=not ctx),
        out_shape=jax.ShapeDtypeStruct((nb * seq, 512), BF16),
        grid=(nb, nq), in_specs=in_specs,
        out_specs=pl.BlockSpec((TM, 512), lambda b, i: (b * nq + i, 0)),
        compiler_params=_params(2),
        name="mla_attn_ctx" if ctx else "mla_attn_lat",
    )(*args)


def _merge_kernel(retc_ref, retl_ref, glac_ref, glal_ref, mlac_ref, mlal_ref, m0_ref, m1_ref, m2_ref,
                  xc_ref, xl_ref, wb_ref, wo_ref,
                  g1_ref, n2_ref, sh2_ref, sc2_ref, rwh_ref, rwl_ref, rb_ref,
                  x1_ref, h_ref, idx_ref, w_ref, rank_ref, cnt_ref):
    is_ctx = pl.program_id(0) < N_CTX_TILES
    mix = None
    for c_ref, l_ref, m_ref, n in ((retc_ref, retl_ref, m0_ref, 0), (glac_ref, glal_ref, m1_ref, 1),
                                   (mlac_ref, mlal_ref, m2_ref, 2)):
        branch = jnp.where(is_ctx, c_ref[...], l_ref[...])
        term = _sigmoid(m_ref[...].astype(F32)) * _dot(branch, wb_ref[n])
        mix = term if mix is None else mix + term
    out = _dot(mix.astype(BF16), wo_ref[...])
    x1 = jnp.where(is_ctx, xc_ref[...], xl_ref[...]) + g1_ref[0] * out
    x1_ref[...] = x1
    h = x1 * lax.rsqrt(jnp.mean(x1 * x1, axis=-1, keepdims=True) + EPS) * n2_ref[...]
    h = h * (1.0 + sc2_ref[0]) + sh2_ref[0]
    h_ref[...] = h
    hh = h.astype(BF16)
    hl = (h - hh.astype(F32)).astype(BF16)
    logits = _dot(hh, rwh_ref[...]) + _dot(hh, rwl_ref[...]) + _dot(hl, rwh_ref[...]) + rb_ref[...]
    lane = lax.broadcasted_iota(jnp.int32, (TM, 128), 1)
    lanef = lane.astype(F32)
    l = jnp.where(lane < N_EXPERTS, logits, -jnp.inf)
    vals, idxs = [], []
    for _ in range(TOP_K):
        m = jnp.max(l, axis=-1, keepdims=True)
        ix = jnp.min(jnp.where(l == m, lanef, 128.0), axis=-1, keepdims=True)
        vals.append(m)
        idxs.append(ix)
        l = jnp.where(lanef == ix, -jnp.inf, l)
    es = [jnp.exp(v - vals[0]) for v in vals]
    inv = 1.0 / (es[0] + es[1] + es[2] + es[3])
    idx_out = jnp.zeros((TM, 128), F32)
    w_out = jnp.zeros((TM, 128), F32)
    for kk in range(TOP_K):
        idx_out = jnp.where(lane == kk, idxs[kk], idx_out)
        w_out = jnp.where(lane == kk, es[kk] * inv, w_out)
    idx_ref[...] = idx_out.astype(jnp.int32)
    w_ref[...] = w_out

    @pl.when(pl.program_id(0) == 0)
    def _():
        cnt_ref[...] = jnp.zeros_like(cnt_ref)
    onehots = [jnp.where(lanef == ix, 1.0, 0.0) for ix in idxs]
    osum = (onehots[0] + onehots[1]) + (onehots[2] + onehots[3])
    ri = lax.broadcasted_iota(jnp.int32, (TM, TM), 0)
    ci = lax.broadcasted_iota(jnp.int32, (TM, TM), 1)
    before = jnp.where(ri > ci, 1.0, 0.0).astype(BF16)
    prior = _dot(before, osum.astype(BF16)) + cnt_ref[0:1, :]
    rank_out = jnp.zeros((TM, 128), F32)
    for kk in range(TOP_K):
        rank_out = jnp.where(lane == kk, jnp.sum(onehots[kk] * prior, axis=-1, keepdims=True), rank_out)
    rank_ref[...] = rank_out.astype(jnp.int32)
    cnt_ref[...] = cnt_ref[...] + jnp.sum(osum, axis=0, keepdims=True)


def _merge(branches, z, xc, xl, modr, w, layer):
    mrow = _mod_row(TM)
    base = layer * N_MOD

    def mod_spec(part):
        return pl.BlockSpec((1, 1, D_MODEL), lambda i: ((base + mrow(i)) * 6 + part, 0, 0))

    const = lambda shape: pl.BlockSpec(shape, lambda i: (0,) * len(shape))
    row = lambda width: pl.BlockSpec((TM, width), lambda i: (i, 0))
    gate = lambda col: pl.BlockSpec((TM, 1024), lambda i: (i, col))
    return pl.pallas_call(
        _merge_kernel,
        out_shape=[jax.ShapeDtypeStruct((T_ALL, D_MODEL), F32), jax.ShapeDtypeStruct((T_ALL, D_MODEL), F32),
                   jax.ShapeDtypeStruct((T_ALL, 128), jnp.int32), jax.ShapeDtypeStruct((T_ALL, 128), F32),
                   jax.ShapeDtypeStruct((T_ALL, 128), jnp.int32), jax.ShapeDtypeStruct((8, 128), F32)],
        grid=(N_TILES,),
        in_specs=[_ctx_spec(TM, 512), _lat_spec(TM, 512)] * 3 + [gate(3), gate(4), gate(5),
                  _ctx_spec(TM, D_MODEL), _lat_spec(TM, D_MODEL),
                  const((3, 512, 1024)), const((1024, 1024)),
                  mod_spec(2), const((1, 1024)), mod_spec(3), mod_spec(4),
                  const((1024, 128)), const((1024, 128)), const((1, 128))],
        out_specs=[row(1024), row(1024), row(128), row(128), row(128), const((8, 128))],
        compiler_params=_params(1),
        name="merge",
    )(*branches, z, z, z, xc, xl, w["wb"], w["wo"], modr, w["n2_g"], modr, modr,
      w["rw_hi"], w["rw_lo"], w["rb"])


def _route(top_idx, rank, counts):
    flat_e = top_idx.reshape(N_SLOTS)
    onehot = (flat_e[:, None] == jnp.arange(N_EXPERTS, dtype=jnp.int32)[None, :]).astype(jnp.int32)
    padded = (counts + MOE_ROWS - 1) // MOE_ROWS * MOE_ROWS
    pad_end = jnp.cumsum(padded)
    pad_start = pad_end - padded
    dest = (rank.reshape(N_SLOTS) + jnp.sum(onehot * pad_start[None, :], axis=1)).astype(jnp.int32)
    blk_start = jnp.arange(N_MOE_BLOCKS, dtype=jnp.int32) * MOE_ROWS
    block_e = jnp.minimum(jnp.sum((pad_end[None, :] <= blk_start[:, None]).astype(jnp.int32), axis=1),
                          N_EXPERTS - 1).astype(jnp.int32)
    n_used = (pad_end[-1] // MOE_ROWS).astype(jnp.int32)
    nxt_blk = pad_end[block_e] // MOE_ROWS
    next_e = jnp.where(nxt_blk < n_used, block_e[jnp.minimum(nxt_blk, N_MOE_BLOCKS - 1)], block_e).astype(jnp.int32)
    tail_start = (pad_start + counts).astype(jnp.int32)
    return dest, tail_start, block_e, n_used.reshape(1), next_e


def _dispatch_kernel(dest_ref, tail_ref, nb_ref, h_ref, xs_ref, zero_buf, sem):
    i = pl.program_id(0)

    @pl.when(i == 0)
    def _():
        zero_buf[...] = jnp.zeros_like(zero_buf)
        fills = [pltpu.make_async_copy(
            zero_buf, xs_ref.at[pl.ds(pl.multiple_of((tail_ref[e] // 8) * 8, 8), TAIL_FILL), :], sem)
            for e in range(N_EXPERTS)]
        for f in fills:
            f.start()
        for f in fills:
            f.wait()

        def fill_block(b, c):
            f = pltpu.make_async_copy(zero_buf.at[pl.ds(0, MOE_ROWS), :],
                                      xs_ref.at[pl.ds(pl.multiple_of(b * MOE_ROWS, MOE_ROWS), MOE_ROWS), :], sem)
            f.start()
            f.wait()
            return c
        lax.fori_loop(nb_ref[0], (N_MOE_ROWS + XS_EXTRA) // MOE_ROWS, fill_block, 0)

    base = i * TM * TOP_K

    def issue(t, c):
        for kk in range(TOP_K):
            pltpu.make_async_copy(h_ref.at[pl.ds(t, 1), :],
                                  xs_ref.at[pl.ds(dest_ref[base + t * TOP_K + kk], 1), :], sem).start()
        return c
    lax.fori_loop(0, TM, issue, 0, unroll=8)
    for kk in range(TOP_K):
        pltpu.make_async_copy(h_ref, xs_ref.at[pl.ds(0, TM), :], sem).wait()


def _moe_dispatch(h, dest, tail_start, n_used):
    return pl.pallas_call(
        _dispatch_kernel,
        out_shape=jax.ShapeDtypeStruct((N_MOE_ROWS + XS_EXTRA, D_MODEL), F32),
        grid_spec=pltpu.PrefetchScalarGridSpec(
            num_scalar_prefetch=3, grid=(N_TILES,),
            in_specs=[pl.BlockSpec((TM, D_MODEL), lambda i, d, t, nb: (i, 0))],
            out_specs=pl.BlockSpec(memory_space=pl.ANY),
            scratch_shapes=[pltpu.VMEM((TAIL_FILL, D_MODEL), F32), pltpu.SemaphoreType.DMA]),
        compiler_params=_params(1),
        name="moe_dispatch",
    )(dest, tail_start, n_used, h)


def _expert_kernel(be_ref, nb_ref, nxt_ref, x_ref, wgu_hbm, bgu_ref, wd_hbm, bd_ref, o_ref,
                   wgu_st, wd_st, wgu_bf, wd_bf, sem, *, layer):
    i = pl.program_id(0)
    e = be_ref[i]
    prev = be_ref[jnp.maximum(i - 1, 0)]

    def weight_copies(expert):
        idx = layer * N_EXPERTS + expert
        return (pltpu.make_async_copy(wgu_hbm.at[idx], wgu_st, sem.at[0]),
                pltpu.make_async_copy(wd_hbm.at[idx], wd_st, sem.at[1]))

    @pl.when(i == 0)
    def _():
        for cp in weight_copies(e):
            cp.start()

    @pl.when(((i == 0) | (e != prev)) & (i < nb_ref[0]))
    def _():
        for cp in weight_copies(e):
            cp.wait()
        wgu_bf[...] = wgu_st[...].astype(BF16)
        wd_bf[...] = wd_st[...].astype(BF16)
        nxt = nxt_ref[i]

        @pl.when(nxt != e)
        def _():
            for cp in weight_copies(nxt):
                cp.start()

    @pl.when(i < nb_ref[0])
    def _():
        gu = _dot(x_ref[...].astype(BF16), wgu_bf[...]) + bgu_ref[0]
        gate = jnp.minimum(gu[:, :D_EXPERT], SWIGLU_LIMIT)
        up = jnp.clip(gu[:, D_EXPERT:], -SWIGLU_LIMIT, SWIGLU_LIMIT)
        act = (up + 1.0) * gate * _sigmoid(SWIGLU_ALPHA * gate)
        out = _dot(act.astype(BF16), wd_bf[...]) + bd_ref[0]
        o_ref[...] = out

    @pl.when(i >= nb_ref[0])
    def _():
        o_ref[...] = jnp.zeros_like(o_ref)


def _moe_experts(xs, block_e, n_used, next_e, w_gu, b_gu, w_down, b_down, layer):
    w_idx = lambda i, be, nb, nx: (layer * N_EXPERTS + be[i], 0, 0)
    return pl.pallas_call(
        functools.partial(_expert_kernel, layer=layer),
        out_shape=jax.ShapeDtypeStruct((N_MOE_ROWS, D_MODEL), F32),
        grid_spec=pltpu.PrefetchScalarGridSpec(
            num_scalar_prefetch=3, grid=(N_MOE_BLOCKS,),
            in_specs=[pl.BlockSpec((MOE_ROWS, D_MODEL), lambda i, be, nb, nx: (jnp.minimum(i, nb[0] - 1), 0)),
                      pl.BlockSpec(memory_space=pl.ANY),
                      pl.BlockSpec((1, 1, 2 * D_EXPERT), w_idx),
                      pl.BlockSpec(memory_space=pl.ANY),
                      pl.BlockSpec((1, 1, D_MODEL), w_idx)],
            out_specs=pl.BlockSpec((MOE_ROWS, D_MODEL), lambda i, be, nb, nx: (i, 0)),
            scratch_shapes=[pltpu.VMEM((D_MODEL, 2 * D_EXPERT), F32), pltpu.VMEM((D_EXPERT, D_MODEL), F32),
                            pltpu.VMEM((D_MODEL, 2 * D_EXPERT), BF16), pltpu.VMEM((D_EXPERT, D_MODEL), BF16),
                            pltpu.SemaphoreType.DMA((2,))]),
        compiler_params=_params(1),
        name="moe_experts",
    )(block_e, n_used, next_e, xs, w_gu, b_gu, w_down, b_down)


def _combine_kernel(dest_ref, x_ref, g2_ref, w_ref, eo_ref, yc_ref, yl_ref, buf, sem):
    i = pl.program_id(0)
    base = i * TM * TOP_K

    def issue(t, c):
        for kk in range(TOP_K):
            pltpu.make_async_copy(eo_ref.at[pl.ds(dest_ref[base + t * TOP_K + kk], 1), :],
                                  buf.at[kk, pl.ds(t, 1), :], sem).start()
        return c
    lax.fori_loop(0, TM, issue, 0, unroll=8)

    for kk in range(TOP_K):
        pltpu.make_async_copy(eo_ref.at[pl.ds(0, TM), :], buf.at[kk], sem).wait()
    w = w_ref[...]
    ff = None
    for kk in range(TOP_K):
        term = buf[kk] * w[:, kk:kk + 1]
        ff = term if ff is None else ff + term
    y = x_ref[...] + g2_ref[0] * ff
    is_ctx = i < N_CTX_TILES

    @pl.when(is_ctx)
    def _():
        yc_ref[...] = y

    @pl.when(jnp.logical_not(is_ctx))
    def _():
        yl_ref[...] = y


def _moe_combine(dest, x1, modr, top_w, eo, layer):
    mrow = _mod_row(TM)
    base = layer * N_MOD
    return pl.pallas_call(
        _combine_kernel,
        out_shape=[jax.ShapeDtypeStruct((T_CTX, D_MODEL), F32), jax.ShapeDtypeStruct((T_LAT, D_MODEL), F32)],
        grid_spec=pltpu.PrefetchScalarGridSpec(
            num_scalar_prefetch=1, grid=(N_TILES,),
            in_specs=[pl.BlockSpec((TM, D_MODEL), lambda i, d: (i, 0)),
                      pl.BlockSpec((1, 1, D_MODEL), lambda i, d: ((base + mrow(i)) * 6 + 5, 0, 0)),
                      pl.BlockSpec((TM, 128), lambda i, d: (i, 0)),
                      pl.BlockSpec(memory_space=pl.ANY)],
            out_specs=[pl.BlockSpec((TM, D_MODEL), lambda i, d: (jnp.minimum(i, N_CTX_TILES - 1), 0)),
                       pl.BlockSpec((TM, D_MODEL), lambda i, d: (jnp.maximum(i - N_CTX_TILES, 0), 0))],
            scratch_shapes=[pltpu.VMEM((TOP_K, TM, D_MODEL), F32), pltpu.SemaphoreType.DMA]),
        compiler_params=_params(1),
        name="moe_combine",
    )(dest, x1, modr, top_w, eo)


def _pad_heads(w, n_heads, width):
    lead = w.shape[:-1]
    w = w.reshape(lead + (n_heads, width))
    w = jnp.pad(w, [(0, 0)] * len(lead) + [(0, 0), (0, HEAD_PAD - width)])
    return w.reshape(lead + (n_heads * HEAD_PAD,))


def _layer_weights(l, w_in, gla_wa2, gla_ba, mla_qa_g, mla_wuq, mla_kva_g, mla_wukv, mla_qn_g, mla_kn_g,
                   w_branch, w_out, router_w, router_b, norm2_g):
    wi = w_in[l]
    w_in_p = jnp.concatenate([wi[:, :3072], wi[:, 3520:], wi[:, 3104:3520], wi[:, 3072:3104],
                              jnp.zeros((D_MODEL, 64), F32)], axis=1).astype(BF16)
    wa_p = jnp.zeros((512, 512), F32)
    wa_p = wa_p.at[416:432, 0:256].set(gla_wa2[l, 0]).at[432:448, 256:512].set(gla_wa2[l, 1]).astype(BF16)
    ba_p = gla_ba[l].reshape(1, 512)
    wukv = mla_wukv[l].reshape(128, MLA_HEADS, MLA_NOPE + MLA_V)
    rw = jnp.pad(router_w[l], ((0, 0), (0, 128 - N_EXPERTS)))
    rw_hi = rw.astype(BF16)
    return {
        "w_in": w_in_p, "wa": wa_p, "ba": ba_p,
        "qa_g": mla_qa_g[l].reshape(1, 256),
        "wuq": _pad_heads(mla_wuq[l], MLA_HEADS, MLA_QK).astype(BF16),
        "qn_g": jnp.pad(mla_qn_g[l], (0, HEAD_PAD - MLA_QK)).reshape(1, 128),
        "kn_g": jnp.pad(mla_kn_g[l], (0, HEAD_PAD - MLA_QK)).reshape(1, 128),
        "kva_g": mla_kva_g[l].reshape(1, 128),
        "wk": _pad_heads(wukv[:, :, :MLA_NOPE].reshape(128, MLA_HEADS * MLA_NOPE), MLA_HEADS, MLA_NOPE).astype(BF16),
        "wv": wukv[:, :, MLA_NOPE:].reshape(128, MLA_HEADS * MLA_V).astype(BF16),
        "wb": w_branch[l].astype(BF16), "wo": w_out[l].astype(BF16),
        "rw_hi": rw_hi, "rw_lo": (rw - rw_hi.astype(F32)).astype(BF16),
        "rb": jnp.pad(router_b[l], (0, 128 - N_EXPERTS)).reshape(1, 128),
        "n2_g": norm2_g[l].reshape(1, D_MODEL),
    }


def kernel(x_prompt, x_sample, cache_mla_ckv, cache_mla_krope, state_ret, state_gla, c, c_ctx, w_mod, b_mod, norm1_g, norm2_g, w_in, ret_gn_g, gla_wa2, gla_ba, gla_norm_g, mla_qa_g, mla_wuq, mla_kva_g, mla_wukv, mla_qn_g, mla_kn_g, w_branch, w_out, router_w, router_b, moe_w_gu, moe_b_gu, moe_w_down, moe_b_down):
    xc, xl = x_prompt.reshape(T_CTX, D_MODEL), x_sample.reshape(T_LAT, D_MODEL)
    cc = jnp.concatenate([c_ctx[None, :], c, jnp.zeros((N_MOD - 1 - N_LAT_SEQ, D_MODEL), F32)], axis=0)
    modr = _modulation(cc, w_mod, b_mod).reshape(DEPTH * N_MOD * 6, 1, D_MODEL)
    rope_tabs = _rope_tables()
    w_gu = moe_w_gu.reshape(DEPTH * N_EXPERTS, D_MODEL, 2 * D_EXPERT)
    b_gu = moe_b_gu.reshape(DEPTH * N_EXPERTS, 1, 2 * D_EXPERT)
    w_dn = moe_w_down.reshape(DEPTH * N_EXPERTS, D_EXPERT, D_MODEL)
    b_dn = moe_b_down.reshape(DEPTH * N_EXPERTS, 1, D_MODEL)

    ckv_l, krope_l, ret_l, gla_l = [], [], [], []
    for l in range(DEPTH):
        w = _layer_weights(l, w_in, gla_wa2, gla_ba, mla_qa_g, mla_wuq, mla_kva_g, mla_wukv, mla_qn_g, mla_kn_g,
                           w_branch, w_out, router_w, router_b, norm2_g)
        z = _in_proj(xc, xl, norm1_g[l].reshape(1, D_MODEL), modr, w["w_in"], l)

        gn = ret_gn_g[l].reshape(1, 512)
        ret_c, ret_state = _retention(z, gn, None, ctx=True)
        (ret_s,) = _retention(z, gn, state_ret[:, l], ctx=False)
        gng = gla_norm_g[l].reshape(1, 512)
        gla_c, gla_state = _gla(z, w["wa"], w["ba"], gng, None, ctx=True)
        gla_s, _ = _gla(z, w["wa"], w["ba"], gng, state_gla[:, l], ctx=False)

        q, k, v, ckv, kr = _mla_prep(z, w, rope_tabs)
        kc, vc = _mla_cache(cache_mla_ckv[:, l].reshape(N_LAT_SEQ * CTX_LEN, 128),
                            jnp.pad(cache_mla_krope[:, l].reshape(N_LAT_SEQ * CTX_LEN, MLA_ROPE),
                                    ((0, 0), (0, 128 - MLA_ROPE))), w)
        mla_c = _mla_attn(q, k, v, None, None, ctx=True)
        mla_s = _mla_attn(q, k, v, kc, vc, ctx=False)

        x1, h2, top_idx, top_w, rank, cnt = _merge((ret_c, ret_s, gla_c, gla_s, mla_c, mla_s), z, xc, xl, modr, w, l)
        dest, tail_start, block_e, n_used, next_e = _route(top_idx[:, :TOP_K], rank[:, :TOP_K],
                                                           cnt[0, :N_EXPERTS].astype(jnp.int32))
        xs = _moe_dispatch(h2, dest, tail_start, n_used)
        eo = _moe_experts(xs, block_e, n_used, next_e, w_gu, b_gu, w_dn, b_dn, l)
        xc, xl = _moe_combine(dest, x1, modr, top_w, eo, l)

        ckv_l.append(ckv[:T_CTX].reshape(N_CTX_SEQ, CTX_LEN, 128))
        krope_l.append(kr[:T_CTX, :MLA_ROPE].reshape(N_CTX_SEQ, CTX_LEN, MLA_ROPE))
        ret_l.append(ret_state)
        gla_l.append(gla_state.reshape(N_CTX_SEQ, 2, N_HEADS, DK, DV))

    y_p = xc.reshape(N_CTX_SEQ, CTX_LEN, D_MODEL)
    y_s = xl.reshape(N_LAT_SEQ, LAT_LEN, D_MODEL)
    return (y_p, y_s, jnp.stack(ckv_l, axis=1), jnp.stack(krope_l, axis=1),
            jnp.stack(ret_l, axis=1), jnp.stack(gla_l, axis=1))
```

```python
import functools

import jax
import jax.numpy as jnp
import numpy as np
from jax import lax
from jax.experimental import pallas as pl
from jax.experimental.pallas import tpu as pltpu

F32 = jnp.float32
BF16 = jnp.bfloat16

D_MODEL = 1024
DEPTH = 2
N_CTX_SEQ, CTX_LEN = 32, 256
N_LAT_SEQ, LAT_LEN = 4, 1024
T_CTX = N_CTX_SEQ * CTX_LEN
T_LAT = N_LAT_SEQ * LAT_LEN
T_ALL = T_CTX + T_LAT
TM = 256
N_TILES = T_ALL // TM
N_CTX_TILES = T_CTX // TM
LAT_TILES = LAT_LEN // TM
N_MOD = 8
EPS = 1e-6

N_HEADS = 4
DK, DV = 64, 128
GRID_W = 64
MLA_HEADS, MLA_NOPE, MLA_ROPE, MLA_V = 8, 64, 32, 64
MLA_QK = MLA_NOPE + MLA_ROPE
HEAD_PAD = 128
GLA_TAU = 16.0
N_EXPERTS, TOP_K, D_EXPERT = 32, 4, 1024
SWIGLU_LIMIT, SWIGLU_ALPHA = 7.0, 1.702
MOE_ROWS = 256
N_SLOTS = T_ALL * TOP_K
N_MOE_BLOCKS = N_SLOTS // MOE_ROWS + N_EXPERTS
N_MOE_ROWS = N_MOE_BLOCKS * MOE_ROWS
ROUTE_CHUNKS = 4
TAIL_FILL = MOE_ROWS + 8
XS_EXTRA = 2 * MOE_ROWS

DZ = 6656
IN_TILE = 512
VMEM_LIMIT = 56 * 1024 * 1024

RET_LOG_F = [float(np.log1p(-np.exp2(-(5.0 + h)))) for h in range(N_HEADS)]
RET_LOG_B = [float(np.log1p(-np.exp2(-(5.5 + h)))) for h in range(N_HEADS)]


def _params(n_axes, vmem=VMEM_LIMIT):
    return pltpu.CompilerParams(dimension_semantics=("arbitrary",) * n_axes, vmem_limit_bytes=vmem)


def _sigmoid(x):
    return 1.0 / (1.0 + jnp.exp(-x))


def _dot(a, b):
    return jnp.dot(a, b, preferred_element_type=F32)


def _dot_t(a, b):
    return lax.dot_general(a, b, (((1,), (1,)), ((), ())), preferred_element_type=F32)


def _mod_row(tile_rows):
    def f(i):
        r0 = i * tile_rows
        return jnp.where(r0 < T_CTX, 0, 1 + (r0 - T_CTX) // LAT_LEN)
    return f


def _ctx_spec(rows, width):
    n_ctx = T_CTX // rows
    return pl.BlockSpec((rows, width), lambda i: (jnp.minimum(i, n_ctx - 1), 0))


def _lat_spec(rows, width):
    n_ctx = T_CTX // rows
    return pl.BlockSpec((rows, width), lambda i: (jnp.maximum(i - n_ctx, 0), 0))


def _mod_kernel(c_ref, w_ref, b_ref, o_ref):
    c = c_ref[...]
    s = c * _sigmoid(c)
    o_ref[0] = jnp.dot(s, w_ref[0], preferred_element_type=F32, precision=lax.Precision.HIGHEST) + b_ref[0]


def _modulation(cc, w_mod, b_mod):
    n = 6 * D_MODEL
    blk = 1024
    return pl.pallas_call(
        _mod_kernel,
        out_shape=jax.ShapeDtypeStruct((DEPTH, N_MOD, n), F32),
        grid=(DEPTH, n // blk),
        in_specs=[pl.BlockSpec((N_MOD, D_MODEL), lambda l, j: (0, 0)),
                  pl.BlockSpec((1, D_MODEL, blk), lambda l, j: (l, 0, j)),
                  pl.BlockSpec((1, 1, blk), lambda l, j: (l, 0, j))],
        out_specs=pl.BlockSpec((1, N_MOD, blk), lambda l, j: (l, 0, j)),
        compiler_params=_params(2),
        name="modulation",
    )(cc, w_mod, b_mod.reshape(DEPTH, 1, n))


def _in_kernel(xc_ref, xl_ref, g_ref, sh_ref, sc_ref, w_ref, o_ref):
    x = jnp.where(pl.program_id(0) < T_CTX // IN_TILE, xc_ref[...], xl_ref[...])
    h = x * lax.rsqrt(jnp.mean(x * x, axis=-1, keepdims=True) + EPS) * g_ref[...]
    h = h * (1.0 + sc_ref[0]) + sh_ref[0]
    hb = h.astype(BF16)
    for n0 in range(0, DZ, 512):
        o_ref[:, n0:n0 + 512] = _dot(hb, w_ref[:, n0:n0 + 512]).astype(BF16)


def _in_proj(xc, xl, g, modr, w_in_p, layer):
    mrow = _mod_row(IN_TILE)
    base = layer * N_MOD

    def mod_spec(part):
        return pl.BlockSpec((1, 1, D_MODEL), lambda i: ((base + mrow(i)) * 6 + part, 0, 0))

    return pl.pallas_call(
        _in_kernel,
        out_shape=jax.ShapeDtypeStruct((T_ALL, DZ), BF16),
        grid=(T_ALL // IN_TILE,),
        in_specs=[_ctx_spec(IN_TILE, D_MODEL), _lat_spec(IN_TILE, D_MODEL),
                  pl.BlockSpec((1, D_MODEL), lambda i: (0, 0)),
                  mod_spec(0), mod_spec(1),
                  pl.BlockSpec((D_MODEL, DZ), lambda i: (0, 0))],
        out_specs=pl.BlockSpec((IN_TILE, DZ), lambda i: (i, 0)),
        compiler_params=_params(1),
        name="in_proj",
    )(xc, xl, g, modr, modr, w_in_p)


def _lane_half_mask(hh):
    lane = lax.broadcasted_iota(jnp.int32, (1, 128), 1)
    return (lane < 64) if hh == 0 else (lane >= 64)


def _ret_kernel(*refs, seq, has_state, emit_state):
    q_ref, k_ref, v_ref, g_ref, gn_ref = refs[:5]
    pos = 5
    if has_state:
        s0_ref = refs[pos]
        pos += 1
    o_ref = refs[pos]
    pos += 1
    if emit_state:
        st_ref = refs[pos]

    r0 = pl.multiple_of(pl.program_id(1) * TM, TM)
    qb = q_ref[pl.ds(r0, TM), :]
    rowi = lax.broadcasted_iota(jnp.int32, (TM, seq), 0) + r0
    colj = lax.broadcasted_iota(jnp.int32, (TM, seq), 1)
    diff = (rowi - colj).astype(F32)
    on_diag = rowi == colj
    ri = (lax.broadcasted_iota(jnp.int32, (TM, 1), 0) + r0).astype(F32)

    for h in range(N_HEADS):
        p, hh = h // 2, h % 2
        lanes = slice(128 * p, 128 * p + 128)
        qp = qb[:, lanes]
        qh = jnp.where(_lane_half_mask(hh), qp, jnp.zeros_like(qp))
        sc = _dot_t(qh, k_ref[:, lanes])
        ex = jnp.where(diff > 0, RET_LOG_F[h] * diff, -RET_LOG_B[h] * diff)
        dec = jnp.exp(ex) * jnp.where(on_diag, 2.0 * DK ** -0.5, DK ** -0.5)
        o = _dot((sc * dec).astype(BF16), v_ref[:, 128 * h:128 * h + 128])
        if has_state:
            qf = qh.astype(F32)
            o += _dot((qf * jnp.exp(RET_LOG_F[h] * (ri + 1.0))).astype(BF16), s0_ref[0, 0, p].astype(BF16))
            o += _dot((qf * jnp.exp(RET_LOG_B[h] * (seq - ri))).astype(BF16), s0_ref[0, 1, p].astype(BF16))
        mu = jnp.mean(o, axis=-1, keepdims=True)
        d = o - mu
        var = jnp.mean(d * d, axis=-1, keepdims=True)
        on = d * lax.rsqrt(var + EPS)
        g = g_ref[:, 128 * h:128 * h + 128].astype(F32)
        out = on * gn_ref[:, 128 * h:128 * h + 128] * (g * _sigmoid(g))
        o_ref[:, 128 * h:128 * h + 128] = out.astype(BF16)

    if emit_state:
        jc = lax.broadcasted_iota(jnp.int32, (seq, 1), 0).astype(F32)
        lane = lax.broadcasted_iota(jnp.int32, (1, 128), 1)
        for p in range(2):
            kp = k_ref[:, 128 * p:128 * p + 128].astype(F32) * DK ** -0.5
            lgf = jnp.where(lane < 64, RET_LOG_F[2 * p], RET_LOG_F[2 * p + 1])
            lgb = jnp.where(lane < 64, RET_LOG_B[2 * p], RET_LOG_B[2 * p + 1])
            kdf = (kp * jnp.exp(lgf * (seq - 1.0 - jc))).T.astype(BF16)
            kdb = (kp * jnp.exp(lgb * jc)).T.astype(BF16)
            for hh in range(2):
                h = 2 * p + hh
                vh = v_ref[:, 128 * h:128 * h + 128]
                st_ref[0, 0, h] = _dot(kdf, vh)[64 * hh:64 * hh + 64, :]
                st_ref[0, 1, h] = _dot(kdb, vh)[64 * hh:64 * hh + 64, :]


def _retention(z, gn_g, s0, *, ctx):
    if ctx:
        nb, seq, row_blk, tile0 = N_CTX_SEQ, CTX_LEN, 0, 0
    else:
        nb, seq, row_blk, tile0 = N_LAT_SEQ, LAT_LEN, T_CTX // LAT_LEN, N_CTX_TILES
    nq = seq // TM
    in_specs = [pl.BlockSpec((seq, 256), lambda b, i: (row_blk + b, 0)),
                pl.BlockSpec((seq, 256), lambda b, i: (row_blk + b, 1)),
                pl.BlockSpec((seq, 512), lambda b, i: (row_blk + b, 1)),
                pl.BlockSpec((TM, 512), lambda b, i: (tile0 + b * nq + i, 2)),
                pl.BlockSpec((1, 512), lambda b, i: (0, 0))]
    args = [z, z, z, z, gn_g]
    out_shape = [jax.ShapeDtypeStruct((nb * seq, 512), BF16)]
    out_specs = [pl.BlockSpec((TM, 512), lambda b, i: (b * nq + i, 0))]
    if not ctx:
        in_specs.append(pl.BlockSpec((1, 2, 2, 128, 128), lambda b, i: (b, 0, 0, 0, 0)))
        args.append(s0.reshape(N_LAT_SEQ, 2, 2, 128, 128))
    else:
        out_shape.append(jax.ShapeDtypeStruct((nb, 2, N_HEADS, DK, DV), F32))
        out_specs.append(pl.BlockSpec((1, 2, N_HEADS, DK, DV), lambda b, i: (b, 0, 0, 0, 0)))
    return pl.pallas_call(
        functools.partial(_ret_kernel, seq=seq, has_state=not ctx, emit_state=ctx),
        out_shape=out_shape, grid=(nb, nq), in_specs=in_specs, out_specs=out_specs,
        compiler_params=_params(2),
        name="retention_ctx" if ctx else "retention_lat",
    )(*args)


def _gla_decay(small_ref, wa_ref, ba_ref):
    x = _dot(small_ref[...], wa_ref[...]) + ba_ref[...]
    la = -(jnp.maximum(-x, 0.0) + jnp.log(1.0 + jnp.exp(-jnp.abs(x)))) * (1.0 / GLA_TAU)
    ri = lax.broadcasted_iota(jnp.int32, (TM, TM), 0)
    ci = lax.broadcasted_iota(jnp.int32, (TM, TM), 1)
    ltri = jnp.where(ri >= ci, 1.0, 0.0).astype(BF16)
    hi = la.astype(BF16)
    r1 = la - hi.astype(F32)
    mid = r1.astype(BF16)
    lo = (r1 - mid.astype(F32)).astype(BF16)
    cum = _dot(ltri, hi) + _dot(ltri, mid) + _dot(ltri, lo)
    return la, cum


def _gla_state_kernel(k_ref, v_ref, small_ref, wa_ref, ba_ref, kv_ref, tot_ref):
    la, cum = _gla_decay(small_ref, wa_ref, ba_ref)
    bf, bb = cum[:, :256], cum[:, 256:]
    xb = bb - la[:, 256:]
    k = k_ref[...].astype(F32)
    kdf = k * jnp.exp(bf[TM - 1:TM, :] - bf)
    kdb = k * jnp.exp(xb)
    for p in range(2):
        kf_t = kdf[:, 128 * p:128 * p + 128].T.astype(BF16)
        kb_t = kdb[:, 128 * p:128 * p + 128].T.astype(BF16)
        for hh in range(2):
            h = 2 * p + hh
            vh = v_ref[:, 128 * h:128 * h + 128]
            kv_ref[0, 0, 0, h] = _dot(kf_t, vh)[64 * hh:64 * hh + 64, :]
            kv_ref[0, 0, 1, h] = _dot(kb_t, vh)[64 * hh:64 * hh + 64, :]
    tot_ref[0, 0] = jnp.sum(la.T, axis=-1, keepdims=True)


def _mid_bcast(x, s, r):
    w = 2 * s
    if w >= 8:
        n = TM // w
        x3 = x.reshape(n, w, 256)
        return jnp.broadcast_to(x3[:, r:r + 1, :], (n, w, 256)).reshape(TM, 256)
    x3 = x.reshape(TM // 8, 8, 256)
    sub = lax.broadcasted_iota(jnp.int32, (1, 8, 1), 1)
    out = None
    for blk in range(8 // w):
        rowv = jnp.broadcast_to(x3[:, blk * w + r:blk * w + r + 1, :], (TM // 8, 8, 256))
        out = rowv if out is None else jnp.where(sub >= blk * w, rowv, out)
    return out.reshape(TM, 256)


def _gla_kernel(*refs, n_blk, has_state):
    q_ref, k_ref, v_ref, g_ref, small_ref, wa_ref, ba_ref, gn_ref = refs[:8]
    pos = 8
    if has_state:
        kv_ref, tot_ref, s0_ref = refs[pos:pos + 3]
        pos += 3
    o_ref = refs[pos]

    la, cum = _gla_decay(small_ref, wa_ref, ba_ref)
    bf, bb = cum[:, :256], cum[:, 256:]
    xb = bb - la[:, 256:]
    q = q_ref[...].astype(F32) * DK ** -0.5
    k = k_ref[...].astype(F32)
    row = lax.broadcasted_iota(jnp.int32, (TM, 1), 0)
    rowi = lax.broadcasted_iota(jnp.int32, (TM, TM), 0)
    colj = lax.broadcasted_iota(jnp.int32, (TM, TM), 1)
    masks = [_lane_half_mask(0), _lane_half_mask(1)]

    def head_q(x, h):
        xp = x[:, 128 * (h // 2):128 * (h // 2) + 128]
        return jnp.where(masks[h % 2], xp, jnp.zeros_like(xp))

    qb16, kb16 = q.astype(BF16), k.astype(BF16)
    acc = []
    for h in range(N_HEADS):
        p = h // 2
        sd = _dot_t(head_q(qb16, h), kb16[:, 128 * p:128 * p + 128])
        acc.append(jnp.where(rowi == colj, 2.0 * sd, 0.0))

    s = 1
    while s < TM:
        upper = ((row // s) % 2) == 1
        mf = _mid_bcast(bf, s, s - 1)
        mb = _mid_bcast(xb, s, s)
        ef = jnp.exp(jnp.where(upper, bf - mf, mf - bf))
        eb = jnp.exp(jnp.where(upper, xb - mb, mb - xb))
        qf = jnp.where(upper, q * ef, 0.0).astype(BF16)
        kf = jnp.where(upper, 0.0, k * ef).astype(BF16)
        qbk = jnp.where(upper, 0.0, q * eb).astype(BF16)
        kbk = jnp.where(upper, k * eb, 0.0).astype(BF16)
        same = (rowi // (2 * s)) == (colj // (2 * s))
        for h in range(N_HEADS):
            p = h // 2
            lanes = slice(128 * p, 128 * p + 128)
            sl = _dot_t(head_q(qf, h), kf[:, lanes]) + _dot_t(head_q(qbk, h), kbk[:, lanes])
            acc[h] = acc[h] + (jnp.where(same, sl, 0.0) if 2 * s < TM else sl)
        s *= 2

    if has_state:
        n = pl.program_id(1)
        qsf = (q * jnp.exp(bf)).astype(BF16)
        qsb = (q * jnp.exp(bb[TM - 1:TM, :] - xb)).astype(BF16)

    for h in range(N_HEADS):
        o = _dot(acc[h].astype(BF16), v_ref[:, 128 * h:128 * h + 128])
        if has_state:
            sf = s0_ref[0, 0, h]
            for m in range(n_blk - 1):
                dec = jnp.exp(tot_ref[0, m, 64 * h:64 * h + 64, :])
                sf = jnp.where(m < n, dec * sf + kv_ref[0, m, 0, h], sf)
            sb = s0_ref[0, 1, h]
            for m in range(n_blk - 1, 0, -1):
                dec = jnp.exp(tot_ref[0, m, 256 + 64 * h:256 + 64 * h + 64, :])
                sb = jnp.where(m > n, dec * sb + kv_ref[0, m, 1, h], sb)
            zero = jnp.zeros((DK, DV), F32)
            hh = h % 2
            sf2 = jnp.concatenate([sf, zero] if hh == 0 else [zero, sf], axis=0).astype(BF16)
            sb2 = jnp.concatenate([sb, zero] if hh == 0 else [zero, sb], axis=0).astype(BF16)
            lanes = slice(128 * (h // 2), 128 * (h // 2) + 128)
            o += _dot(qsf[:, lanes], sf2) + _dot(qsb[:, lanes], sb2)
        on = o * lax.rsqrt(jnp.mean(o * o, axis=-1, keepdims=True) + EPS)
        g = g_ref[:, 128 * h:128 * h + 128].astype(F32)
        out = on * gn_ref[:, 128 * h:128 * h + 128] * (g * _sigmoid(g))
        o_ref[:, 128 * h:128 * h + 128] = out.astype(BF16)


def _gla(z, wa_p, ba_p, gn_g, s0, *, ctx):
    if ctx:
        nb, n_blk, tile0 = N_CTX_SEQ, 1, 0
    else:
        nb, n_blk, tile0 = N_LAT_SEQ, LAT_TILES, N_CTX_TILES

    def zspec(width, col):
        return pl.BlockSpec((TM, width), lambda b, n: (tile0 + b * n_blk + n, col))

    w_specs = [pl.BlockSpec((512, 512), lambda b, n: (0, 0)), pl.BlockSpec((1, 512), lambda b, n: (0, 0))]
    kv, tot = pl.pallas_call(
        _gla_state_kernel,
        out_shape=[jax.ShapeDtypeStruct((nb, n_blk, 2, N_HEADS, DK, DV), F32),
                   jax.ShapeDtypeStruct((nb, n_blk, 512, 1), F32)],
        grid=(nb, n_blk),
        in_specs=[zspec(256, 7), zspec(512, 4), zspec(512, 12)] + w_specs,
        out_specs=[pl.BlockSpec((1, 1, 2, N_HEADS, DK, DV), lambda b, n: (b, n, 0, 0, 0, 0)),
                   pl.BlockSpec((1, 1, 512, 1), lambda b, n: (b, n, 0, 0))],
        compiler_params=_params(2),
        name="gla_state_ctx" if ctx else "gla_state_lat",
    )(z, z, z, wa_p, ba_p)

    in_specs = [zspec(256, 6), zspec(256, 7), zspec(512, 4), zspec(512, 5), zspec(512, 12)] + w_specs
    in_specs.append(pl.BlockSpec((1, 512), lambda b, n: (0, 0)))
    args = [z, z, z, z, z, wa_p, ba_p, gn_g]
    if not ctx:
        in_specs += [pl.BlockSpec((1, n_blk, 2, N_HEADS, DK, DV), lambda b, n: (b, 0, 0, 0, 0, 0)),
                     pl.BlockSpec((1, n_blk, 512, 1), lambda b, n: (b, 0, 0, 0)),
                     pl.BlockSpec((1, 2, N_HEADS, DK, DV), lambda b, n: (b, 0, 0, 0, 0))]
        args += [kv, tot, s0]
    out = pl.pallas_call(
        functools.partial(_gla_kernel, n_blk=n_blk, has_state=not ctx),
        out_shape=jax.ShapeDtypeStruct((nb * n_blk * TM, 512), BF16),
        grid=(nb, n_blk), in_specs=in_specs,
        out_specs=pl.BlockSpec((TM, 512), lambda b, n: (b * n_blk + n, 0)),
        compiler_params=_params(2),
        name="gla_ctx" if ctx else "gla_lat",
    )(*args)
    return out, kv


def _rope_tables():
    nf = MLA_ROPE // 4
    pos = np.arange(LAT_LEN)
    freqs = (10000.0 ** (-np.arange(nf, dtype=np.float32) / nf)).astype(np.float32)
    ang_r = ((pos // GRID_W).astype(np.float32)[:, None] * freqs).astype(np.float32)
    ang_c = ((pos % GRID_W).astype(np.float32)[:, None] * freqs).astype(np.float32)
    cos = np.ones((TM + LAT_LEN, HEAD_PAD), np.float32)
    sa = np.zeros((TM + LAT_LEN, HEAD_PAD), np.float32)
    sb = np.zeros((TM + LAT_LEN, HEAD_PAD), np.float32)
    o = MLA_NOPE
    for base, ang in ((o, ang_r), (o + 2 * nf, ang_c)):
        cos[TM:, base:base + nf] = np.cos(ang)
        cos[TM:, base + nf:base + 2 * nf] = np.cos(ang)
        sa[TM:, base:base + nf] = -np.sin(ang)
        sb[TM:, base + nf:base + 2 * nf] = np.sin(ang)
    return jnp.asarray(cos), jnp.asarray(sa), jnp.asarray(sb)


def _rope(x, cos, sa, sb):
    return x * cos + pltpu.roll(x, 128 - 8, 1) * sa + pltpu.roll(x, 8, 1) * sb


def _head_norm(x, g):
    return x * lax.rsqrt(jnp.sum(x * x, axis=-1, keepdims=True) * (1.0 / MLA_QK) + EPS) * g


def _mla_keys(ckv, kr_tile, wk_ref, wv_ref, kn_ref, k_ref, v_ref, rope=None):
    cb = ckv.astype(BF16)
    kpre = _dot(cb, wk_ref[...])
    v_ref[...] = _dot(cb, wv_ref[...]).astype(BF16)
    for h in range(MLA_HEADS):
        kh = _head_norm(kpre[:, 128 * h:128 * h + 128] + kr_tile, kn_ref[...])
        if rope is not None:
            kh = _rope(kh, *rope)
        k_ref[:, 128 * h:128 * h + 128] = kh.astype(BF16)


def _mla_prep_kernel(small_ref, qa_ref, wuq_ref, qn_ref, kva_ref, wk_ref, wv_ref, kn_ref,
                     cos_ref, sa_ref, sb_ref, q_ref, k_ref, v_ref, ckv_ref, kr_ref):
    def body(rope):
        sm = small_ref[...].astype(F32)
        cq, ckv_raw, g3 = sm[:, 0:256], sm[:, 256:384], sm[:, 384:512]
        cqn = cq * lax.rsqrt(jnp.mean(cq * cq, axis=-1, keepdims=True) + EPS) * qa_ref[...]
        q = _dot(cqn.astype(BF16), wuq_ref[...])
        scale = MLA_QK ** -0.5
        for h in range(MLA_HEADS):
            qh = _head_norm(q[:, 128 * h:128 * h + 128], qn_ref[...])
            if rope is not None:
                qh = _rope(qh, *rope)
            q_ref[:, 128 * h:128 * h + 128] = (qh * scale).astype(BF16)
        ckv = ckv_raw * lax.rsqrt(jnp.mean(ckv_raw * ckv_raw, axis=-1, keepdims=True) + EPS) * kva_ref[...]
        ckv_ref[...] = ckv
        lane = lax.broadcasted_iota(jnp.int32, (1, 128), 1)
        kr = jnp.where(lane < MLA_ROPE, g3, 0.0)
        kr_ref[...] = kr
        _mla_keys(ckv, pltpu.roll(kr, MLA_NOPE, 1), wk_ref, wv_ref, kn_ref, k_ref, v_ref, rope)

    is_ctx = pl.program_id(0) < N_CTX_TILES

    @pl.when(is_ctx)
    def _():
        body(None)

    @pl.when(jnp.logical_not(is_ctx))
    def _():
        body((cos_ref[...], sa_ref[...], sb_ref[...]))


def _mla_cache_kernel(ckv_ref, kr_ref, wk_ref, wv_ref, kn_ref, k_ref, v_ref):
    _mla_keys(ckv_ref[...], pltpu.roll(kr_ref[...], MLA_NOPE, 1), wk_ref, wv_ref, kn_ref, k_ref, v_ref)


def _mla_prep(z, w, rope_tabs):
    def rope_blk(i):
        return jnp.where(i < N_CTX_TILES, 0, 1 + (i - N_CTX_TILES) % LAT_TILES)

    const = lambda shape: pl.BlockSpec(shape, lambda i: (0,) * len(shape))
    rope_spec = pl.BlockSpec((TM, HEAD_PAD), lambda i: (rope_blk(i), 0))
    row = lambda width: pl.BlockSpec((TM, width), lambda i: (i, 0))
    return pl.pallas_call(
        _mla_prep_kernel,
        out_shape=[jax.ShapeDtypeStruct((T_ALL, 1024), BF16), jax.ShapeDtypeStruct((T_ALL, 1024), BF16),
                   jax.ShapeDtypeStruct((T_ALL, 512), BF16), jax.ShapeDtypeStruct((T_ALL, 128), F32),
                   jax.ShapeDtypeStruct((T_ALL, 128), F32)],
        grid=(N_TILES,),
        in_specs=[pl.BlockSpec((TM, 512), lambda i: (i, 12)),
                  const((1, 256)), const((256, 1024)), const((1, 128)), const((1, 128)),
                  const((128, 1024)), const((128, 512)), const((1, 128)),
                  rope_spec, rope_spec, rope_spec],
        out_specs=[row(1024), row(1024), row(512), row(128), row(128)],
        compiler_params=_params(1),
        name="mla_prep",
    )(z, w["qa_g"], w["wuq"], w["qn_g"], w["kva_g"], w["wk"], w["wv"], w["kn_g"], *rope_tabs)


def _mla_cache(ckv, kr_pad, w):
    const = lambda shape: pl.BlockSpec(shape, lambda i: (0,) * len(shape))
    row = lambda width: pl.BlockSpec((TM, width), lambda i: (i, 0))
    n = ckv.shape[0]
    return pl.pallas_call(
        _mla_cache_kernel,
        out_shape=[jax.ShapeDtypeStruct((n, 1024), BF16), jax.ShapeDtypeStruct((n, 512), BF16)],
        grid=(n // TM,),
        in_specs=[row(128), row(128), const((128, 1024)), const((128, 512)), const((1, 128))],
        out_specs=[row(1024), row(512)],
        compiler_params=_params(1),
        name="mla_cache",
    )(ckv, kr_pad, w["wk"], w["wv"], w["kn_g"])


def _mla_attn_kernel(*refs, has_cache):
    q_ref, k_ref, v_ref = refs[:3]
    pos = 3
    if has_cache:
        kc_ref, vc_ref = refs[3:5]
        pos = 5
    o_ref = refs[pos]
    for p in range(MLA_HEADS // 2):
        acc = jnp.zeros((TM, 128), F32)
        for hh in range(2):
            h = 2 * p + hh
            lanes = slice(128 * h, 128 * h + 128)
            qh = q_ref[:, lanes]
            l1 = _dot_t(qh, k_ref[:, lanes])
            m = jnp.max(l1, axis=-1, keepdims=True)
            if has_cache:
                l0 = _dot_t(qh, kc_ref[:, lanes])
                m = jnp.maximum(m, jnp.max(l0, axis=-1, keepdims=True))
                p0 = jnp.exp(l0 - m)
            p1 = jnp.exp(l1 - m)
            den = jnp.sum(p1, axis=-1, keepdims=True)
            if has_cache:
                den = den + jnp.sum(p0, axis=-1, keepdims=True)
            inv = 1.0 / den
            mask = _lane_half_mask(hh)
            vp = v_ref[:, 128 * p:128 * p + 128]
            acc += _dot((p1 * inv).astype(BF16), jnp.where(mask, vp, jnp.zeros_like(vp)))
            if has_cache:
                vcp = vc_ref[:, 128 * p:128 * p + 128]
                acc += _dot((p0 * inv).astype(BF16), jnp.where(mask, vcp, jnp.zeros_like(vcp)))
        o_ref[:, 128 * p:128 * p + 128] = acc.astype(BF16)


def _mla_attn(q, k, v, kc, vc, *, ctx):
    if ctx:
        nb, seq, row_blk, tile0 = N_CTX_SEQ, CTX_LEN, 0, 0
    else:
        nb, seq, row_blk, tile0 = N_LAT_SEQ, LAT_LEN, T_CTX // LAT_LEN, N_CTX_TILES
    nq = seq // TM
    in_specs = [pl.BlockSpec((TM, 1024), lambda b, i: (tile0 + b * nq + i, 0)),
                pl.BlockSpec((seq, 1024), lambda b, i: (row_blk + b, 0)),
                pl.BlockSpec((seq, 512), lambda b, i: (row_blk + b, 0))]
    args = [q, k, v]
    if not ctx:
        in_specs += [pl.BlockSpec((TM, 1024), lambda b, i: (b, 0)), pl.BlockSpec((TM, 512), lambda b, i: (b, 0))]
        args += [kc, vc]
    return pl.pallas_call(
        functools.partial(_mla_attn_kernel, has_cache=not ctx),
        out_shape=jax.ShapeDtypeStruct((nb * seq, 512), BF16),
        grid=(nb, nq), in_specs=in_specs,
        out_specs=pl.BlockSpec((TM, 512), lambda b, i: (b * nq + i, 0)),
        compiler_params=_params(2),
        name="mla_attn_ctx" if ctx else "mla_attn_lat",
    )(*args)


def _merge_kernel(retc_ref, retl_ref, glac_ref, glal_ref, mlac_ref, mlal_ref, m0_ref, m1_ref, m2_ref,
                  xc_ref, xl_ref, wb_ref, wo_ref,
                  g1_ref, n2_ref, sh2_ref, sc2_ref, rwh_ref, rwl_ref, rb_ref,
                  x1_ref, h_ref, idx_ref, w_ref, rank_ref, cnt_ref):
    @pl.when(pl.program_id(0) == 0)
    def _():
        cnt_ref[...] = jnp.zeros_like(cnt_ref)

    is_ctx = pl.program_id(0) < N_CTX_TILES
    mix = None
    for c_ref, l_ref, m_ref, n in ((retc_ref, retl_ref, m0_ref, 0), (glac_ref, glal_ref, m1_ref, 1),
                                   (mlac_ref, mlal_ref, m2_ref, 2)):
        branch = jnp.where(is_ctx, c_ref[...], l_ref[...])
        term = _sigmoid(m_ref[...].astype(F32)) * _dot(branch, wb_ref[n])
        mix = term if mix is None else mix + term
    out = _dot(mix.astype(BF16), wo_ref[...])
    x1 = jnp.where(is_ctx, xc_ref[...], xl_ref[...]) + g1_ref[0] * out
    x1_ref[...] = x1
    h = x1 * lax.rsqrt(jnp.mean(x1 * x1, axis=-1, keepdims=True) + EPS) * n2_ref[...]
    h = h * (1.0 + sc2_ref[0]) + sh2_ref[0]
    h_ref[...] = h
    hh = h.astype(BF16)
    hl = (h - hh.astype(F32)).astype(BF16)
    logits = _dot(hh, rwh_ref[...]) + _dot(hh, rwl_ref[...]) + _dot(hl, rwh_ref[...]) + rb_ref[...]
    rows = TM // ROUTE_CHUNKS
    lane = lax.broadcasted_iota(jnp.int32, (rows, 128), 1)
    lanef = lane.astype(F32)
    onehots, osums = [], []
    for c in range(ROUTE_CHUNKS):
        l = jnp.where(lane < N_EXPERTS, logits[c * rows:(c + 1) * rows], -jnp.inf)
        vals, idxs = [], []
        for _ in range(TOP_K):
            m = jnp.max(l, axis=-1, keepdims=True)
            ix = jnp.min(jnp.where(l == m, lanef, 128.0), axis=-1, keepdims=True)
            vals.append(m)
            idxs.append(ix)
            l = jnp.where(lanef == ix, -jnp.inf, l)
        es = [jnp.exp(v - vals[0]) for v in vals]
        inv = 1.0 / (es[0] + es[1] + es[2] + es[3])
        idx_out = jnp.zeros((rows, 128), F32)
        w_out = jnp.zeros((rows, 128), F32)
        for kk in range(TOP_K):
            idx_out = jnp.where(lane == kk, idxs[kk], idx_out)
            w_out = jnp.where(lane == kk, es[kk] * inv, w_out)
        idx_ref[c * rows:(c + 1) * rows, :] = idx_out.astype(jnp.int32)
        w_ref[c * rows:(c + 1) * rows, :] = w_out
        oh = [jnp.where(lanef == ix, 1.0, 0.0) for ix in idxs]
        onehots.append(oh)
        osums.append((oh[0] + oh[1]) + (oh[2] + oh[3]))

    osum = jnp.concatenate(osums, axis=0)
    ri = lax.broadcasted_iota(jnp.int32, (TM, TM), 0)
    ci = lax.broadcasted_iota(jnp.int32, (TM, TM), 1)
    before = jnp.where(ri > ci, 1.0, 0.0).astype(BF16)
    prior = _dot(before, osum.astype(BF16)) + cnt_ref[0:1, :]
    for c in range(ROUTE_CHUNKS):
        pc = prior[c * rows:(c + 1) * rows]
        rank_out = jnp.zeros((rows, 128), F32)
        for kk in range(TOP_K):
            rank_out = jnp.where(lane == kk, jnp.sum(onehots[c][kk] * pc, axis=-1, keepdims=True), rank_out)
        rank_ref[c * rows:(c + 1) * rows, :] = rank_out.astype(jnp.int32)
    cnt_ref[...] = cnt_ref[...] + jnp.sum(osum, axis=0, keepdims=True)


def _merge(branches, z, xc, xl, modr, w, layer):
    mrow = _mod_row(TM)
    base = layer * N_MOD

    def mod_spec(part):
        return pl.BlockSpec((1, 1, D_MODEL), lambda i: ((base + mrow(i)) * 6 + part, 0, 0))

    const = lambda shape: pl.BlockSpec(shape, lambda i: (0,) * len(shape))
    row = lambda width: pl.BlockSpec((TM, width), lambda i: (i, 0))
    gate = lambda col: pl.BlockSpec((TM, 1024), lambda i: (i, col))
    return pl.pallas_call(
        _merge_kernel,
        out_shape=[jax.ShapeDtypeStruct((T_ALL, D_MODEL), F32), jax.ShapeDtypeStruct((T_ALL, D_MODEL), F32),
                   jax.ShapeDtypeStruct((T_ALL, 128), jnp.int32), jax.ShapeDtypeStruct((T_ALL, 128), F32),
                   jax.ShapeDtypeStruct((T_ALL, 128), jnp.int32), jax.ShapeDtypeStruct((8, 128), F32)],
        grid=(N_TILES,),
        in_specs=[_ctx_spec(TM, 512), _lat_spec(TM, 512)] * 3 + [gate(3), gate(4), gate(5),
                  _ctx_spec(TM, D_MODEL), _lat_spec(TM, D_MODEL),
                  const((3, 512, 1024)), const((1024, 1024)),
                  mod_spec(2), const((1, 1024)), mod_spec(3), mod_spec(4),
                  const((1024, 128)), const((1024, 128)), const((1, 128))],
        out_specs=[row(1024), row(1024), row(128), row(128), row(128), const((8, 128))],
        compiler_params=_params(1),
        name="merge",
    )(*branches, z, z, z, xc, xl, w["wb"], w["wo"], modr, w["n2_g"], modr, modr,
      w["rw_hi"], w["rw_lo"], w["rb"])


def _route(top_idx, rank, counts):
    flat_e = top_idx.reshape(N_SLOTS)
    onehot = (flat_e[:, None] == jnp.arange(N_EXPERTS, dtype=jnp.int32)[None, :]).astype(jnp.int32)
    padded = (counts + MOE_ROWS - 1) // MOE_ROWS * MOE_ROWS
    pad_end = jnp.cumsum(padded)
    pad_start = pad_end - padded
    dest = (rank.reshape(N_SLOTS) + jnp.sum(onehot * pad_start[None, :], axis=1)).astype(jnp.int32)
    blk_start = jnp.arange(N_MOE_BLOCKS, dtype=jnp.int32) * MOE_ROWS
    block_e = jnp.minimum(jnp.sum((pad_end[None, :] <= blk_start[:, None]).astype(jnp.int32), axis=1),
                          N_EXPERTS - 1).astype(jnp.int32)
    n_used = (pad_end[-1] // MOE_ROWS).astype(jnp.int32)
    nxt_blk = pad_end[block_e] // MOE_ROWS
    next_e = jnp.where(nxt_blk < n_used, block_e[jnp.minimum(nxt_blk, N_MOE_BLOCKS - 1)], block_e).astype(jnp.int32)
    tail_start = (pad_start + counts).astype(jnp.int32)
    return dest, tail_start, block_e, n_used.reshape(1), next_e


def _dispatch_kernel(dest_ref, tail_ref, nb_ref, h_ref, xs_ref, zero_buf, stage, sem, ssem):
    i = pl.program_id(0)

    @pl.when(i == 0)
    def _():
        zero_buf[...] = jnp.zeros_like(zero_buf)
        fills = [pltpu.make_async_copy(
            zero_buf, xs_ref.at[pl.ds(pl.multiple_of((tail_ref[e] // 8) * 8, 8), TAIL_FILL), :], sem)
            for e in range(N_EXPERTS)]
        for f in fills:
            f.start()
        for f in fills:
            f.wait()

        def fill_block(b, c):
            f = pltpu.make_async_copy(zero_buf.at[pl.ds(0, MOE_ROWS), :],
                                      xs_ref.at[pl.ds(pl.multiple_of(b * MOE_ROWS, MOE_ROWS), MOE_ROWS), :], sem)
            f.start()
            f.wait()
            return c
        lax.fori_loop(nb_ref[0], (N_MOE_ROWS + XS_EXTRA) // MOE_ROWS, fill_block, 0)

    base = i * TM * TOP_K
    cur = i % 2

    def wait_tile(slot):
        for kk in range(TOP_K):
            pltpu.make_async_copy(stage.at[slot], xs_ref.at[pl.ds(0, TM), :], ssem.at[slot]).wait()

    @pl.when(i >= 2)
    def _():
        wait_tile(cur)
    stage[cur] = h_ref[...]

    def issue(t, c):
        for kk in range(TOP_K):
            pltpu.make_async_copy(stage.at[cur, pl.ds(t, 1), :],
                                  xs_ref.at[pl.ds(dest_ref[base + t * TOP_K + kk], 1), :], ssem.at[cur]).start()
        return c
    lax.fori_loop(0, TM, issue, 0, unroll=8)

    @pl.when(i == N_TILES - 1)
    def _():
        wait_tile(1 - cur)
        wait_tile(cur)


def _moe_dispatch(h, dest, tail_start, n_used):
    return pl.pallas_call(
        _dispatch_kernel,
        out_shape=jax.ShapeDtypeStruct((N_MOE_ROWS + XS_EXTRA, D_MODEL), F32),
        grid_spec=pltpu.PrefetchScalarGridSpec(
            num_scalar_prefetch=3, grid=(N_TILES,),
            in_specs=[pl.BlockSpec((TM, D_MODEL), lambda i, d, t, nb: (i, 0))],
            out_specs=pl.BlockSpec(memory_space=pl.ANY),
            scratch_shapes=[pltpu.VMEM((TAIL_FILL, D_MODEL), F32), pltpu.VMEM((2, TM, D_MODEL), F32),
                            pltpu.SemaphoreType.DMA, pltpu.SemaphoreType.DMA((2,))]),
        compiler_params=_params(1),
        name="moe_dispatch",
    )(dest, tail_start, n_used, h)


def _expert_kernel(be_ref, nb_ref, nxt_ref, x_ref, wgu_hbm, bgu_ref, wd_hbm, bd_ref, o_ref,
                   wgu_st, wd_st, wgu_bf, wd_bf, sem, *, layer):
    i = pl.program_id(0)
    e = be_ref[i]
    prev = be_ref[jnp.maximum(i - 1, 0)]

    def weight_copies(expert):
        idx = layer * N_EXPERTS + expert
        return (pltpu.make_async_copy(wgu_hbm.at[idx], wgu_st, sem.at[0]),
                pltpu.make_async_copy(wd_hbm.at[idx], wd_st, sem.at[1]))

    @pl.when(i == 0)
    def _():
        for cp in weight_copies(e):
            cp.start()

    @pl.when(((i == 0) | (e != prev)) & (i < nb_ref[0]))
    def _():
        for cp in weight_copies(e):
            cp.wait()
        wgu_bf[...] = wgu_st[...].astype(BF16)
        wd_bf[...] = wd_st[...].astype(BF16)
        nxt = nxt_ref[i]

        @pl.when(nxt != e)
        def _():
            for cp in weight_copies(nxt):
                cp.start()

    @pl.when(i < nb_ref[0])
    def _():
        gu = _dot(x_ref[...].astype(BF16), wgu_bf[...]) + bgu_ref[0]
        gate = jnp.minimum(gu[:, :D_EXPERT], SWIGLU_LIMIT)
        up = jnp.clip(gu[:, D_EXPERT:], -SWIGLU_LIMIT, SWIGLU_LIMIT)
        act = (up + 1.0) * gate * _sigmoid(SWIGLU_ALPHA * gate)
        out = _dot(act.astype(BF16), wd_bf[...]) + bd_ref[0]
        o_ref[...] = out

    @pl.when(i >= nb_ref[0])
    def _():
        o_ref[...] = jnp.zeros_like(o_ref)


def _moe_experts(xs, block_e, n_used, next_e, w_gu, b_gu, w_down, b_down, layer):
    w_idx = lambda i, be, nb, nx: (layer * N_EXPERTS + be[i], 0, 0)
    return pl.pallas_call(
        functools.partial(_expert_kernel, layer=layer),
        out_shape=jax.ShapeDtypeStruct((N_MOE_ROWS, D_MODEL), F32),
        grid_spec=pltpu.PrefetchScalarGridSpec(
            num_scalar_prefetch=3, grid=(N_MOE_BLOCKS,),
            in_specs=[pl.BlockSpec((MOE_ROWS, D_MODEL), lambda i, be, nb, nx: (jnp.minimum(i, nb[0] - 1), 0)),
                      pl.BlockSpec(memory_space=pl.ANY),
                      pl.BlockSpec((1, 1, 2 * D_EXPERT), w_idx),
                      pl.BlockSpec(memory_space=pl.ANY),
                      pl.BlockSpec((1, 1, D_MODEL), w_idx)],
            out_specs=pl.BlockSpec((MOE_ROWS, D_MODEL), lambda i, be, nb, nx: (i, 0)),
            scratch_shapes=[pltpu.VMEM((D_MODEL, 2 * D_EXPERT), F32), pltpu.VMEM((D_EXPERT, D_MODEL), F32),
                            pltpu.VMEM((D_MODEL, 2 * D_EXPERT), BF16), pltpu.VMEM((D_EXPERT, D_MODEL), BF16),
                            pltpu.SemaphoreType.DMA((2,))]),
        compiler_params=_params(1),
        name="moe_experts",
    )(block_e, n_used, next_e, xs, w_gu, b_gu, w_down, b_down)


def _combine_kernel(dest_ref, x_ref, g2_ref, w_ref, eo_ref, yc_ref, yl_ref, buf, sem):
    i = pl.program_id(0)
    cur = i % 2

    def issue_tile(tile, slot):
        base = tile * TM * TOP_K

        def issue(t, c):
            for kk in range(TOP_K):
                pltpu.make_async_copy(eo_ref.at[pl.ds(dest_ref[base + t * TOP_K + kk], 1), :],
                                      buf.at[slot, kk, pl.ds(t, 1), :], sem.at[slot]).start()
            return c
        lax.fori_loop(0, TM, issue, 0, unroll=8)

    @pl.when(i == 0)
    def _():
        issue_tile(0, 0)

    @pl.when(i + 1 < N_TILES)
    def _():
        issue_tile(i + 1, 1 - cur)

    for kk in range(TOP_K):
        pltpu.make_async_copy(eo_ref.at[pl.ds(0, TM), :], buf.at[cur, kk], sem.at[cur]).wait()
    w = w_ref[...]
    ff = None
    for kk in range(TOP_K):
        term = buf[cur, kk] * w[:, kk:kk + 1]
        ff = term if ff is None else ff + term
    y = x_ref[...] + g2_ref[0] * ff
    is_ctx = i < N_CTX_TILES

    @pl.when(is_ctx)
    def _():
        yc_ref[...] = y

    @pl.when(jnp.logical_not(is_ctx))
    def _():
        yl_ref[...] = y


def _moe_combine(dest, x1, modr, top_w, eo, layer):
    mrow = _mod_row(TM)
    base = layer * N_MOD
    return pl.pallas_call(
        _combine_kernel,
        out_shape=[jax.ShapeDtypeStruct((T_CTX, D_MODEL), F32), jax.ShapeDtypeStruct((T_LAT, D_MODEL), F32)],
        grid_spec=pltpu.PrefetchScalarGridSpec(
            num_scalar_prefetch=1, grid=(N_TILES,),
            in_specs=[pl.BlockSpec((TM, D_MODEL), lambda i, d: (i, 0)),
                      pl.BlockSpec((1, 1, D_MODEL), lambda i, d: ((base + mrow(i)) * 6 + 5, 0, 0)),
                      pl.BlockSpec((TM, 128), lambda i, d: (i, 0)),
                      pl.BlockSpec(memory_space=pl.ANY)],
            out_specs=[pl.BlockSpec((TM, D_MODEL), lambda i, d: (jnp.minimum(i, N_CTX_TILES - 1), 0)),
                       pl.BlockSpec((TM, D_MODEL), lambda i, d: (jnp.maximum(i - N_CTX_TILES, 0), 0))],
            scratch_shapes=[pltpu.VMEM((2, TOP_K, TM, D_MODEL), F32), pltpu.SemaphoreType.DMA((2,))]),
        compiler_params=_params(1),
        name="moe_combine",
    )(dest, x1, modr, top_w, eo)


def _pad_heads(w, n_heads, width):
    lead = w.shape[:-1]
    w = w.reshape(lead + (n_heads, width))
    w = jnp.pad(w, [(0, 0)] * len(lead) + [(0, 0), (0, HEAD_PAD - width)])
    return w.reshape(lead + (n_heads * HEAD_PAD,))


def _layer_weights(l, w_in, gla_wa2, gla_ba, mla_qa_g, mla_wuq, mla_kva_g, mla_wukv, mla_qn_g, mla_kn_g,
                   w_branch, w_out, router_w, router_b, norm2_g):
    wi = w_in[l]
    w_in_p = jnp.concatenate([wi[:, :3072], wi[:, 3520:], wi[:, 3104:3520], wi[:, 3072:3104],
                              jnp.zeros((D_MODEL, 64), F32)], axis=1).astype(BF16)
    wa_p = jnp.zeros((512, 512), F32)
    wa_p = wa_p.at[416:432, 0:256].set(gla_wa2[l, 0]).at[432:448, 256:512].set(gla_wa2[l, 1]).astype(BF16)
    ba_p = gla_ba[l].reshape(1, 512)
    wukv = mla_wukv[l].reshape(128, MLA_HEADS, MLA_NOPE + MLA_V)
    rw = jnp.pad(router_w[l], ((0, 0), (0, 128 - N_EXPERTS)))
    rw_hi = rw.astype(BF16)
    return {
        "w_in": w_in_p, "wa": wa_p, "ba": ba_p,
        "qa_g": mla_qa_g[l].reshape(1, 256),
        "wuq": _pad_heads(mla_wuq[l], MLA_HEADS, MLA_QK).astype(BF16),
        "qn_g": jnp.pad(mla_qn_g[l], (0, HEAD_PAD - MLA_QK)).reshape(1, 128),
        "kn_g": jnp.pad(mla_kn_g[l], (0, HEAD_PAD - MLA_QK)).reshape(1, 128),
        "kva_g": mla_kva_g[l].reshape(1, 128),
        "wk": _pad_heads(wukv[:, :, :MLA_NOPE].reshape(128, MLA_HEADS * MLA_NOPE), MLA_HEADS, MLA_NOPE).astype(BF16),
        "wv": wukv[:, :, MLA_NOPE:].reshape(128, MLA_HEADS * MLA_V).astype(BF16),
        "wb": w_branch[l].astype(BF16), "wo": w_out[l].astype(BF16),
        "rw_hi": rw_hi, "rw_lo": (rw - rw_hi.astype(F32)).astype(BF16),
        "rb": jnp.pad(router_b[l], (0, 128 - N_EXPERTS)).reshape(1, 128),
        "n2_g": norm2_g[l].reshape(1, D_MODEL),
    }


def kernel(x_prompt, x_sample, cache_mla_ckv, cache_mla_krope, state_ret, state_gla, c, c_ctx, w_mod, b_mod, norm1_g, norm2_g, w_in, ret_gn_g, gla_wa2, gla_ba, gla_norm_g, mla_qa_g, mla_wuq, mla_kva_g, mla_wukv, mla_qn_g, mla_kn_g, w_branch, w_out, router_w, router_b, moe_w_gu, moe_b_gu, moe_w_down, moe_b_down):
    xc, xl = x_prompt.reshape(T_CTX, D_MODEL), x_sample.reshape(T_LAT, D_MODEL)
    cc = jnp.concatenate([c_ctx[None, :], c, jnp.zeros((N_MOD - 1 - N_LAT_SEQ, D_MODEL), F32)], axis=0)
    modr = _modulation(cc, w_mod, b_mod).reshape(DEPTH * N_MOD * 6, 1, D_MODEL)
    rope_tabs = _rope_tables()
    w_gu = moe_w_gu.reshape(DEPTH * N_EXPERTS, D_MODEL, 2 * D_EXPERT)
    b_gu = moe_b_gu.reshape(DEPTH * N_EXPERTS, 1, 2 * D_EXPERT)
    w_dn = moe_w_down.reshape(DEPTH * N_EXPERTS, D_EXPERT, D_MODEL)
    b_dn = moe_b_down.reshape(DEPTH * N_EXPERTS, 1, D_MODEL)

    ckv_l, krope_l, ret_l, gla_l = [], [], [], []
    for l in range(DEPTH):
        w = _layer_weights(l, w_in, gla_wa2, gla_ba, mla_qa_g, mla_wuq, mla_kva_g, mla_wukv, mla_qn_g, mla_kn_g,
                           w_branch, w_out, router_w, router_b, norm2_g)
        z = _in_proj(xc, xl, norm1_g[l].reshape(1, D_MODEL), modr, w["w_in"], l)

        gn = ret_gn_g[l].reshape(1, 512)
        ret_c, ret_state = _retention(z, gn, None, ctx=True)
        (ret_s,) = _retention(z, gn, state_ret[:, l], ctx=False)
        gng = gla_norm_g[l].reshape(1, 512)
        gla_c, gla_state = _gla(z, w["wa"], w["ba"], gng, None, ctx=True)
        gla_s, _ = _gla(z, w["wa"], w["ba"], gng, state_gla[:, l], ctx=False)

        q, k, v, ckv, kr = _mla_prep(z, w, rope_tabs)
        kc, vc = _mla_cache(cache_mla_ckv[:, l].reshape(N_LAT_SEQ * CTX_LEN, 128),
                            jnp.pad(cache_mla_krope[:, l].reshape(N_LAT_SEQ * CTX_LEN, MLA_ROPE),
                                    ((0, 0), (0, 128 - MLA_ROPE))), w)
        mla_c = _mla_attn(q, k, v, None, None, ctx=True)
        mla_s = _mla_attn(q, k, v, kc, vc, ctx=False)

        x1, h2, top_idx, top_w, rank, cnt = _merge((ret_c, ret_s, gla_c, gla_s, mla_c, mla_s), z, xc, xl, modr, w, l)
        dest, tail_start, block_e, n_used, next_e = _route(top_idx[:, :TOP_K], rank[:, :TOP_K],
                                                           cnt[0, :N_EXPERTS].astype(jnp.int32))
        xs = _moe_dispatch(h2, dest, tail_start, n_used)
        eo = _moe_experts(xs, block_e, n_used, next_e, w_gu, b_gu, w_dn, b_dn, l)
        xc, xl = _moe_combine(dest, x1, modr, top_w, eo, l)

        ckv_l.append(ckv[:T_CTX].reshape(N_CTX_SEQ, CTX_LEN, 128))
        krope_l.append(kr[:T_CTX, :MLA_ROPE].reshape(N_CTX_SEQ, CTX_LEN, MLA_ROPE))
        ret_l.append(ret_state)
        gla_l.append(gla_state.reshape(N_CTX_SEQ, 2, N_HEADS, DK, DV))

    y_p = xc.reshape(N_CTX_SEQ, CTX_LEN, D_MODEL)
    y_s = xl.reshape(N_LAT_SEQ, LAT_LEN, D_MODEL)
    return (y_p, y_s, jnp.stack(ckv_l, axis=1), jnp.stack(krope_l, axis=1),
            jnp.stack(ret_l, axis=1), jnp.stack(gla_l, axis=1))
```

```python
import functools

import jax
import jax.numpy as jnp
import numpy as np
from jax import lax
from jax.experimental import pallas as pl
from jax.experimental.pallas import tpu as pltpu

F32 = jnp.float32
BF16 = jnp.bfloat16

D_MODEL = 1024
DEPTH = 2
N_CTX_SEQ, CTX_LEN = 32, 256
N_LAT_SEQ, LAT_LEN = 4, 1024
T_CTX = N_CTX_SEQ * CTX_LEN
T_LAT = N_LAT_SEQ * LAT_LEN
T_ALL = T_CTX + T_LAT
TM = 256
N_TILES = T_ALL // TM
N_CTX_TILES = T_CTX // TM
LAT_TILES = LAT_LEN // TM
N_MOD = 8
EPS = 1e-6

N_HEADS = 4
DK, DV = 64, 128
GRID_W = 64
MLA_HEADS, MLA_NOPE, MLA_ROPE, MLA_V = 8, 64, 32, 64
MLA_QK = MLA_NOPE + MLA_ROPE
HEAD_PAD = 128
GLA_TAU = 16.0
N_EXPERTS, TOP_K, D_EXPERT = 32, 4, 1024
SWIGLU_LIMIT, SWIGLU_ALPHA = 7.0, 1.702
MOE_ROWS = 256
N_SLOTS = T_ALL * TOP_K
N_MOE_BLOCKS = N_SLOTS // MOE_ROWS + N_EXPERTS
N_MOE_ROWS = N_MOE_BLOCKS * MOE_ROWS
ROUTE_CHUNKS = 4
TAIL_FILL = MOE_ROWS + 8
XS_EXTRA = 2 * MOE_ROWS

DZ = 6656
IN_TILE = 512
VMEM_LIMIT = 56 * 1024 * 1024

RET_LOG_F = [float(np.log1p(-np.exp2(-(5.0 + h)))) for h in range(N_HEADS)]
RET_LOG_B = [float(np.log1p(-np.exp2(-(5.5 + h)))) for h in range(N_HEADS)]


def _params(n_axes, vmem=VMEM_LIMIT):
    return pltpu.CompilerParams(dimension_semantics=("arbitrary",) * n_axes, vmem_limit_bytes=vmem)


def _sigmoid(x):
    return 1.0 / (1.0 + jnp.exp(-x))


def _dot(a, b):
    return jnp.dot(a, b, preferred_element_type=F32)


def _dot_t(a, b):
    return lax.dot_general(a, b, (((1,), (1,)), ((), ())), preferred_element_type=F32)


def _mod_row(tile_rows):
    def f(i):
        r0 = i * tile_rows
        return jnp.where(r0 < T_CTX, 0, 1 + (r0 - T_CTX) // LAT_LEN)
    return f


def _ctx_spec(rows, width):
    n_ctx = T_CTX // rows
    return pl.BlockSpec((rows, width), lambda i: (jnp.minimum(i, n_ctx - 1), 0))


def _lat_spec(rows, width):
    n_ctx = T_CTX // rows
    return pl.BlockSpec((rows, width), lambda i: (jnp.maximum(i - n_ctx, 0), 0))


def _mod_kernel(c_ref, w_ref, b_ref, o_ref):
    c = c_ref[...]
    s = c * _sigmoid(c)
    o_ref[0] = jnp.dot(s, w_ref[0], preferred_element_type=F32, precision=lax.Precision.HIGHEST) + b_ref[0]


def _modulation(cc, w_mod, b_mod):
    n = 6 * D_MODEL
    blk = 2048
    return pl.pallas_call(
        _mod_kernel,
        out_shape=jax.ShapeDtypeStruct((DEPTH, N_MOD, n), F32),
        grid=(DEPTH, n // blk),
        in_specs=[pl.BlockSpec((N_MOD, D_MODEL), lambda l, j: (0, 0)),
                  pl.BlockSpec((1, D_MODEL, blk), lambda l, j: (l, 0, j)),
                  pl.BlockSpec((1, 1, blk), lambda l, j: (l, 0, j))],
        out_specs=pl.BlockSpec((1, N_MOD, blk), lambda l, j: (l, 0, j)),
        compiler_params=_params(2),
        name="modulation",
    )(cc, w_mod, b_mod.reshape(DEPTH, 1, n))


def _in_kernel(xc_ref, xl_ref, g_ref, sh_ref, sc_ref, w_ref, o_ref):
    x = jnp.where(pl.program_id(0) < T_CTX // IN_TILE, xc_ref[...], xl_ref[...])
    h = x * lax.rsqrt(jnp.mean(x * x, axis=-1, keepdims=True) + EPS) * g_ref[...]
    h = h * (1.0 + sc_ref[0]) + sh_ref[0]
    hb = h.astype(BF16)
    for n0 in range(0, DZ, 512):
        o_ref[:, n0:n0 + 512] = _dot(hb, w_ref[:, n0:n0 + 512]).astype(BF16)


def _in_proj(xc, xl, g, modr, w_in_p, layer):
    mrow = _mod_row(IN_TILE)
    base = layer * N_MOD

    def mod_spec(part):
        return pl.BlockSpec((1, 1, D_MODEL), lambda i: ((base + mrow(i)) * 6 + part, 0, 0))

    return pl.pallas_call(
        _in_kernel,
        out_shape=jax.ShapeDtypeStruct((T_ALL, DZ), BF16),
        grid=(T_ALL // IN_TILE,),
        in_specs=[_ctx_spec(IN_TILE, D_MODEL), _lat_spec(IN_TILE, D_MODEL),
                  pl.BlockSpec((1, D_MODEL), lambda i: (0, 0)),
                  mod_spec(0), mod_spec(1),
                  pl.BlockSpec((D_MODEL, DZ), lambda i: (0, 0))],
        out_specs=pl.BlockSpec((IN_TILE, DZ), lambda i: (i, 0)),
        compiler_params=_params(1),
        name="in_proj",
    )(xc, xl, g, modr, modr, w_in_p)


def _lane_half_mask(hh):
    lane = lax.broadcasted_iota(jnp.int32, (1, 128), 1)
    return (lane < 64) if hh == 0 else (lane >= 64)


def _ret_kernel(*refs, seq, has_state, emit_state):
    q_ref, k_ref, v_ref, g_ref, gn_ref = refs[:5]
    pos = 5
    if has_state:
        s0_ref = refs[pos]
        pos += 1
    o_ref = refs[pos]
    pos += 1
    if emit_state:
        st_ref = refs[pos]

    r0 = pl.multiple_of(pl.program_id(1) * TM, TM)
    qb = q_ref[pl.ds(r0, TM), :]
    rowi = lax.broadcasted_iota(jnp.int32, (TM, seq), 0) + r0
    colj = lax.broadcasted_iota(jnp.int32, (TM, seq), 1)
    diff = (rowi - colj).astype(F32)
    on_diag = rowi == colj
    ri = (lax.broadcasted_iota(jnp.int32, (TM, 1), 0) + r0).astype(F32)

    for h in range(N_HEADS):
        p, hh = h // 2, h % 2
        lanes = slice(128 * p, 128 * p + 128)
        qp = qb[:, lanes]
        qh = jnp.where(_lane_half_mask(hh), qp, jnp.zeros_like(qp))
        sc = _dot_t(qh, k_ref[:, lanes])
        ex = jnp.where(diff > 0, RET_LOG_F[h] * diff, -RET_LOG_B[h] * diff)
        dec = jnp.exp(ex) * jnp.where(on_diag, 2.0 * DK ** -0.5, DK ** -0.5)
        o = _dot((sc * dec).astype(BF16), v_ref[:, 128 * h:128 * h + 128])
        if has_state:
            qf = qh.astype(F32)
            o += _dot((qf * jnp.exp(RET_LOG_F[h] * (ri + 1.0))).astype(BF16), s0_ref[0, 0, p].astype(BF16))
            o += _dot((qf * jnp.exp(RET_LOG_B[h] * (seq - ri))).astype(BF16), s0_ref[0, 1, p].astype(BF16))
        mu = jnp.mean(o, axis=-1, keepdims=True)
        d = o - mu
        var = jnp.mean(d * d, axis=-1, keepdims=True)
        on = d * lax.rsqrt(var + EPS)
        g = g_ref[:, 128 * h:128 * h + 128].astype(F32)
        out = on * gn_ref[:, 128 * h:128 * h + 128] * (g * _sigmoid(g))
        o_ref[:, 128 * h:128 * h + 128] = out.astype(BF16)

    if emit_state:
        jc = lax.broadcasted_iota(jnp.int32, (seq, 1), 0).astype(F32)
        lane = lax.broadcasted_iota(jnp.int32, (1, 128), 1)
        for p in range(2):
            kp = k_ref[:, 128 * p:128 * p + 128].astype(F32) * DK ** -0.5
            lgf = jnp.where(lane < 64, RET_LOG_F[2 * p], RET_LOG_F[2 * p + 1])
            lgb = jnp.where(lane < 64, RET_LOG_B[2 * p], RET_LOG_B[2 * p + 1])
            kdf = (kp * jnp.exp(lgf * (seq - 1.0 - jc))).T.astype(BF16)
            kdb = (kp * jnp.exp(lgb * jc)).T.astype(BF16)
            for hh in range(2):
                h = 2 * p + hh
                vh = v_ref[:, 128 * h:128 * h + 128]
                st_ref[0, 0, h] = _dot(kdf, vh)[64 * hh:64 * hh + 64, :]
                st_ref[0, 1, h] = _dot(kdb, vh)[64 * hh:64 * hh + 64, :]


def _retention(z, gn_g, s0, *, ctx):
    if ctx:
        nb, seq, row_blk, tile0 = N_CTX_SEQ, CTX_LEN, 0, 0
    else:
        nb, seq, row_blk, tile0 = N_LAT_SEQ, LAT_LEN, T_CTX // LAT_LEN, N_CTX_TILES
    nq = seq // TM
    in_specs = [pl.BlockSpec((seq, 256), lambda b, i: (row_blk + b, 0)),
                pl.BlockSpec((seq, 256), lambda b, i: (row_blk + b, 1)),
                pl.BlockSpec((seq, 512), lambda b, i: (row_blk + b, 1)),
                pl.BlockSpec((TM, 512), lambda b, i: (tile0 + b * nq + i, 2)),
                pl.BlockSpec((1, 512), lambda b, i: (0, 0))]
    args = [z, z, z, z, gn_g]
    out_shape = [jax.ShapeDtypeStruct((nb * seq, 512), BF16)]
    out_specs = [pl.BlockSpec((TM, 512), lambda b, i: (b * nq + i, 0))]
    if not ctx:
        in_specs.append(pl.BlockSpec((1, 2, 2, 128, 128), lambda b, i: (b, 0, 0, 0, 0)))
        args.append(s0.reshape(N_LAT_SEQ, 2, 2, 128, 128))
    else:
        out_shape.append(jax.ShapeDtypeStruct((nb, 2, N_HEADS, DK, DV), F32))
        out_specs.append(pl.BlockSpec((1, 2, N_HEADS, DK, DV), lambda b, i: (b, 0, 0, 0, 0)))
    return pl.pallas_call(
        functools.partial(_ret_kernel, seq=seq, has_state=not ctx, emit_state=ctx),
        out_shape=out_shape, grid=(nb, nq), in_specs=in_specs, out_specs=out_specs,
        compiler_params=_params(2),
        name="retention_ctx" if ctx else "retention_lat",
    )(*args)


def _gla_decay(small_ref, wa_ref, ba_ref):
    x = _dot(small_ref[...], wa_ref[...]) + ba_ref[...]
    la = -(jnp.maximum(-x, 0.0) + jnp.log(1.0 + jnp.exp(-jnp.abs(x)))) * (1.0 / GLA_TAU)
    ri = lax.broadcasted_iota(jnp.int32, (TM, TM), 0)
    ci = lax.broadcasted_iota(jnp.int32, (TM, TM), 1)
    ltri = jnp.where(ri >= ci, 1.0, 0.0).astype(BF16)
    hi = la.astype(BF16)
    r1 = la - hi.astype(F32)
    mid = r1.astype(BF16)
    lo = (r1 - mid.astype(F32)).astype(BF16)
    cum = _dot(ltri, hi) + _dot(ltri, mid) + _dot(ltri, lo)
    return la, cum


def _gla_state_kernel(k_ref, v_ref, small_ref, wa_ref, ba_ref, kv_ref, tot_ref):
    la, cum = _gla_decay(small_ref, wa_ref, ba_ref)
    bf, bb = cum[:, :256], cum[:, 256:]
    xb = bb - la[:, 256:]
    k = k_ref[...].astype(F32)
    kdf = k * jnp.exp(bf[TM - 1:TM, :] - bf)
    kdb = k * jnp.exp(xb)
    for p in range(2):
        kf_t = kdf[:, 128 * p:128 * p + 128].T.astype(BF16)
        kb_t = kdb[:, 128 * p:128 * p + 128].T.astype(BF16)
        for hh in range(2):
            h = 2 * p + hh
            vh = v_ref[:, 128 * h:128 * h + 128]
            kv_ref[0, 0, 0, h] = _dot(kf_t, vh)[64 * hh:64 * hh + 64, :]
            kv_ref[0, 0, 1, h] = _dot(kb_t, vh)[64 * hh:64 * hh + 64, :]
    tot_ref[0, 0] = jnp.sum(la.T, axis=-1, keepdims=True)


def _mid_bcast(x, s, r):
    w = 2 * s
    if w >= 8:
        n = TM // w
        x3 = x.reshape(n, w, 256)
        return jnp.broadcast_to(x3[:, r:r + 1, :], (n, w, 256)).reshape(TM, 256)
    x3 = x.reshape(TM // 8, 8, 256)
    sub = lax.broadcasted_iota(jnp.int32, (1, 8, 1), 1)
    out = None
    for blk in range(8 // w):
        rowv = jnp.broadcast_to(x3[:, blk * w + r:blk * w + r + 1, :], (TM // 8, 8, 256))
        out = rowv if out is None else jnp.where(sub >= blk * w, rowv, out)
    return out.reshape(TM, 256)


def _gla_kernel(*refs, n_blk, has_state, emit_state):
    q_ref, k_ref, v_ref, g_ref, small_ref, wa_ref, ba_ref, gn_ref = refs[:8]
    pos = 8
    if has_state:
        kv_ref, tot_ref, s0_ref = refs[pos:pos + 3]
        pos += 3
    o_ref = refs[pos]
    pos += 1
    if emit_state:
        kvo_ref = refs[pos]

    la, cum = _gla_decay(small_ref, wa_ref, ba_ref)
    bf, bb = cum[:, :256], cum[:, 256:]
    xb = bb - la[:, 256:]
    q = q_ref[...].astype(F32) * DK ** -0.5
    k = k_ref[...].astype(F32)
    row = lax.broadcasted_iota(jnp.int32, (TM, 1), 0)
    rowi = lax.broadcasted_iota(jnp.int32, (TM, TM), 0)
    colj = lax.broadcasted_iota(jnp.int32, (TM, TM), 1)
    low_half = _lane_half_mask(0)

    def join(fwd, bwd):
        ops = []
        for p in range(N_HEADS // 2):
            f = fwd[:, 128 * p:128 * p + 128]
            br = pltpu.roll(bwd[:, 128 * p:128 * p + 128], 64, 1)
            ops.append(jnp.where(low_half, f, br).astype(BF16))
            ops.append(jnp.where(low_half, br, f).astype(BF16))
        return ops

    acc = []
    for qo, ko in zip(join(q, q), join(k, k)):
        acc.append(jnp.where(rowi == colj, _dot_t(qo, ko), 0.0))

    s = 1
    while s < TM:
        upper = ((row // s) % 2) == 1
        mf = _mid_bcast(bf, s, s - 1)
        mb = _mid_bcast(xb, s, s)
        ef = jnp.exp(-jnp.abs(bf - mf))
        eb = jnp.exp(-jnp.abs(xb - mb))
        qs = join(jnp.where(upper, q * ef, 0.0), jnp.where(upper, 0.0, q * eb))
        ks = join(jnp.where(upper, 0.0, k * ef), jnp.where(upper, k * eb, 0.0))
        same = (rowi // (2 * s)) == (colj // (2 * s))
        for h in range(N_HEADS):
            sl = _dot_t(qs[h], ks[h])
            acc[h] = acc[h] + (jnp.where(same, sl, 0.0) if 2 * s < TM else sl)
        s *= 2

    if has_state:
        n = pl.program_id(1)
        q_state = join(q * jnp.exp(bf), q * jnp.exp(bb[TM - 1:TM, :] - xb))

    if emit_state:
        kdf = k * jnp.exp(bf[TM - 1:TM, :] - bf)
        kdb = k * jnp.exp(xb)
        for p in range(N_HEADS // 2):
            kf_t = kdf[:, 128 * p:128 * p + 128].T.astype(BF16)
            kb_t = kdb[:, 128 * p:128 * p + 128].T.astype(BF16)
            for hh in range(2):
                h = 2 * p + hh
                vh = v_ref[:, 128 * h:128 * h + 128]
                kvo_ref[0, 0, 0, h] = _dot(kf_t, vh)[64 * hh:64 * hh + 64, :]
                kvo_ref[0, 0, 1, h] = _dot(kb_t, vh)[64 * hh:64 * hh + 64, :]

    for h in range(N_HEADS):
        o = _dot(acc[h].astype(BF16), v_ref[:, 128 * h:128 * h + 128])
        if has_state:
            sf = s0_ref[0, 0, h]
            for m in range(n_blk - 1):
                dec = jnp.exp(tot_ref[0, m, 64 * h:64 * h + 64, :])
                sf = jnp.where(m < n, dec * sf + kv_ref[0, m, 0, h], sf)
            sb = s0_ref[0, 1, h]
            for m in range(n_blk - 1, 0, -1):
                dec = jnp.exp(tot_ref[0, m, 256 + 64 * h:256 + 64 * h + 64, :])
                sb = jnp.where(m > n, dec * sb + kv_ref[0, m, 1, h], sb)
            state = jnp.concatenate([sf, sb] if h % 2 == 0 else [sb, sf], axis=0).astype(BF16)
            o += _dot(q_state[h], state)
        on = o * lax.rsqrt(jnp.mean(o * o, axis=-1, keepdims=True) + EPS)
        g = g_ref[:, 128 * h:128 * h + 128].astype(F32)
        out = on * gn_ref[:, 128 * h:128 * h + 128] * (g * _sigmoid(g))
        o_ref[:, 128 * h:128 * h + 128] = out.astype(BF16)


def _gla(z, wa_p, ba_p, gn_g, s0, *, ctx):
    if ctx:
        nb, n_blk, tile0 = N_CTX_SEQ, 1, 0
    else:
        nb, n_blk, tile0 = N_LAT_SEQ, LAT_TILES, N_CTX_TILES

    def zspec(width, col):
        return pl.BlockSpec((TM, width), lambda b, n: (tile0 + b * n_blk + n, col))

    w_specs = [pl.BlockSpec((512, 512), lambda b, n: (0, 0)), pl.BlockSpec((1, 512), lambda b, n: (0, 0))]
    kv_shape = jax.ShapeDtypeStruct((nb, n_blk, 2, N_HEADS, DK, DV), F32)
    kv_spec = pl.BlockSpec((1, 1, 2, N_HEADS, DK, DV), lambda b, n: (b, n, 0, 0, 0, 0))
    in_specs = [zspec(256, 6), zspec(256, 7), zspec(512, 4), zspec(512, 5), zspec(512, 12)] + w_specs
    in_specs.append(pl.BlockSpec((1, 512), lambda b, n: (0, 0)))
    args = [z, z, z, z, z, wa_p, ba_p, gn_g]
    out_shape = [jax.ShapeDtypeStruct((nb * n_blk * TM, 512), BF16)]
    out_specs = [pl.BlockSpec((TM, 512), lambda b, n: (b * n_blk + n, 0))]
    if ctx:
        out_shape.append(kv_shape)
        out_specs.append(kv_spec)
    else:
        kv, tot = pl.pallas_call(
            _gla_state_kernel,
            out_shape=[kv_shape, jax.ShapeDtypeStruct((nb, n_blk, 512, 1), F32)],
            grid=(nb, n_blk),
            in_specs=[zspec(256, 7), zspec(512, 4), zspec(512, 12)] + w_specs,
            out_specs=[kv_spec, pl.BlockSpec((1, 1, 512, 1), lambda b, n: (b, n, 0, 0))],
            compiler_params=_params(2),
            name="gla_state_lat",
        )(z, z, z, wa_p, ba_p)
        in_specs += [pl.BlockSpec((1, n_blk, 2, N_HEADS, DK, DV), lambda b, n: (b, 0, 0, 0, 0, 0)),
                     pl.BlockSpec((1, n_blk, 512, 1), lambda b, n: (b, 0, 0, 0)),
                     pl.BlockSpec((1, 2, N_HEADS, DK, DV), lambda b, n: (b, 0, 0, 0, 0))]
        args += [kv, tot, s0]
    res = pl.pallas_call(
        functools.partial(_gla_kernel, n_blk=n_blk, has_state=not ctx, emit_state=ctx),
        out_shape=out_shape, grid=(nb, n_blk), in_specs=in_specs, out_specs=out_specs,
        compiler_params=_params(2),
        name="gla_ctx" if ctx else "gla_lat",
    )(*args)
    return (res[0], res[1]) if ctx else (res[0], None)


def _rope_tables():
    nf = MLA_ROPE // 4
    pos = np.arange(LAT_LEN)
    freqs = (10000.0 ** (-np.arange(nf, dtype=np.float32) / nf)).astype(np.float32)
    ang_r = ((pos // GRID_W).astype(np.float32)[:, None] * freqs).astype(np.float32)
    ang_c = ((pos % GRID_W).astype(np.float32)[:, None] * freqs).astype(np.float32)
    cos = np.ones((TM + LAT_LEN, HEAD_PAD), np.float32)
    sa = np.zeros((TM + LAT_LEN, HEAD_PAD), np.float32)
    sb = np.zeros((TM + LAT_LEN, HEAD_PAD), np.float32)
    o = MLA_NOPE
    for base, ang in ((o, ang_r), (o + 2 * nf, ang_c)):
        cos[TM:, base:base + nf] = np.cos(ang)
        cos[TM:, base + nf:base + 2 * nf] = np.cos(ang)
        sa[TM:, base:base + nf] = -np.sin(ang)
        sb[TM:, base + nf:base + 2 * nf] = np.sin(ang)
    return jnp.asarray(cos), jnp.asarray(sa), jnp.asarray(sb)


def _rope(x, cos, sa, sb):
    return x * cos + pltpu.roll(x, 128 - 8, 1) * sa + pltpu.roll(x, 8, 1) * sb


def _head_norm(x, g):
    return x * lax.rsqrt(jnp.sum(x * x, axis=-1, keepdims=True) * (1.0 / MLA_QK) + EPS) * g


def _mla_keys(ckv, kr_tile, wk_ref, wv_ref, kn_ref, k_ref, v_ref, rope=None):
    cb = ckv.astype(BF16)
    kpre = _dot(cb, wk_ref[...])
    v_ref[...] = _dot(cb, wv_ref[...]).astype(BF16)
    for h in range(MLA_HEADS):
        kh = _head_norm(kpre[:, 128 * h:128 * h + 128] + kr_tile, kn_ref[...])
        if rope is not None:
            kh = _rope(kh, *rope)
        k_ref[:, 128 * h:128 * h + 128] = kh.astype(BF16)


def _mla_prep_kernel(small_ref, qa_ref, wuq_ref, qn_ref, kva_ref, wk_ref, wv_ref, kn_ref,
                     cos_ref, sa_ref, sb_ref, q_ref, k_ref, v_ref, ckv_ref, kr_ref):
    def body(rope):
        sm = small_ref[...].astype(F32)
        cq, ckv_raw, g3 = sm[:, 0:256], sm[:, 256:384], sm[:, 384:512]
        cqn = cq * lax.rsqrt(jnp.mean(cq * cq, axis=-1, keepdims=True) + EPS) * qa_ref[...]
        q = _dot(cqn.astype(BF16), wuq_ref[...])
        scale = MLA_QK ** -0.5
        for h in range(MLA_HEADS):
            qh = _head_norm(q[:, 128 * h:128 * h + 128], qn_ref[...])
            if rope is not None:
                qh = _rope(qh, *rope)
            q_ref[:, 128 * h:128 * h + 128] = (qh * scale).astype(BF16)
        ckv = ckv_raw * lax.rsqrt(jnp.mean(ckv_raw * ckv_raw, axis=-1, keepdims=True) + EPS) * kva_ref[...]
        ckv_ref[...] = ckv
        lane = lax.broadcasted_iota(jnp.int32, (1, 128), 1)
        kr = jnp.where(lane < MLA_ROPE, g3, 0.0)
        kr_ref[...] = kr
        _mla_keys(ckv, pltpu.roll(kr, MLA_NOPE, 1), wk_ref, wv_ref, kn_ref, k_ref, v_ref, rope)

    is_ctx = pl.program_id(0) < N_CTX_TILES

    @pl.when(is_ctx)
    def _():
        body(None)

    @pl.when(jnp.logical_not(is_ctx))
    def _():
        body((cos_ref[...], sa_ref[...], sb_ref[...]))


def _mla_cache_kernel(ckv_ref, kr_ref, wk_ref, wv_ref, kn_ref, k_ref, v_ref):
    _mla_keys(ckv_ref[...], pltpu.roll(kr_ref[...], MLA_NOPE, 1), wk_ref, wv_ref, kn_ref, k_ref, v_ref)


def _mla_prep(z, w, rope_tabs):
    def rope_blk(i):
        return jnp.where(i < N_CTX_TILES, 0, 1 + (i - N_CTX_TILES) % LAT_TILES)

    const = lambda shape: pl.BlockSpec(shape, lambda i: (0,) * len(shape))
    rope_spec = pl.BlockSpec((TM, HEAD_PAD), lambda i: (rope_blk(i), 0))
    row = lambda width: pl.BlockSpec((TM, width), lambda i: (i, 0))
    return pl.pallas_call(
        _mla_prep_kernel,
        out_shape=[jax.ShapeDtypeStruct((T_ALL, 1024), BF16), jax.ShapeDtypeStruct((T_ALL, 1024), BF16),
                   jax.ShapeDtypeStruct((T_ALL, 512), BF16), jax.ShapeDtypeStruct((T_ALL, 128), F32),
                   jax.ShapeDtypeStruct((T_ALL, 128), F32)],
        grid=(N_TILES,),
        in_specs=[pl.BlockSpec((TM, 512), lambda i: (i, 12)),
                  const((1, 256)), const((256, 1024)), const((1, 128)), const((1, 128)),
                  const((128, 1024)), const((128, 512)), const((1, 128)),
                  rope_spec, rope_spec, rope_spec],
        out_specs=[row(1024), row(1024), row(512), row(128), row(128)],
        compiler_params=_params(1),
        name="mla_prep",
    )(z, w["qa_g"], w["wuq"], w["qn_g"], w["kva_g"], w["wk"], w["wv"], w["kn_g"], *rope_tabs)


def _mla_cache(ckv, kr_pad, w):
    const = lambda shape: pl.BlockSpec(shape, lambda i: (0,) * len(shape))
    row = lambda width: pl.BlockSpec((TM, width), lambda i: (i, 0))
    n = ckv.shape[0]
    return pl.pallas_call(
        _mla_cache_kernel,
        out_shape=[jax.ShapeDtypeStruct((n, 1024), BF16), jax.ShapeDtypeStruct((n, 512), BF16)],
        grid=(n // TM,),
        in_specs=[row(128), row(128), const((128, 1024)), const((128, 512)), const((1, 128))],
        out_specs=[row(1024), row(512)],
        compiler_params=_params(1),
        name="mla_cache",
    )(ckv, kr_pad, w["wk"], w["wv"], w["kn_g"])


def _mla_attn_kernel(*refs, has_cache):
    q_ref, k_ref, v_ref = refs[:3]
    pos = 3
    if has_cache:
        kc_ref, vc_ref = refs[3:5]
        pos = 5
    o_ref = refs[pos]
    for p in range(MLA_HEADS // 2):
        acc = jnp.zeros((TM, 128), F32)
        for hh in range(2):
            h = 2 * p + hh
            lanes = slice(128 * h, 128 * h + 128)
            qh = q_ref[:, lanes]
            l1 = _dot_t(qh, k_ref[:, lanes])
            m = jnp.max(l1, axis=-1, keepdims=True)
            if has_cache:
                l0 = _dot_t(qh, kc_ref[:, lanes])
                m = jnp.maximum(m, jnp.max(l0, axis=-1, keepdims=True))
                p0 = jnp.exp(l0 - m)
            p1 = jnp.exp(l1 - m)
            den = jnp.sum(p1, axis=-1, keepdims=True)
            if has_cache:
                den = den + jnp.sum(p0, axis=-1, keepdims=True)
            mask = _lane_half_mask(hh)
            vp = v_ref[:, 128 * p:128 * p + 128]
            o = _dot(p1.astype(BF16), jnp.where(mask, vp, jnp.zeros_like(vp)))
            if has_cache:
                vcp = vc_ref[:, 128 * p:128 * p + 128]
                o += _dot(p0.astype(BF16), jnp.where(mask, vcp, jnp.zeros_like(vcp)))
            acc += o * (1.0 / den)
        o_ref[:, 128 * p:128 * p + 128] = acc.astype(BF16)


def _mla_attn(q, k, v, kc, vc, *, ctx):
    if ctx:
        nb, seq, row_blk, tile0 = N_CTX_SEQ, CTX_LEN, 0, 0
    else:
        nb, seq, row_blk, tile0 = N_LAT_SEQ, LAT_LEN, T_CTX // LAT_LEN, N_CTX_TILES
    nq = seq // TM
    in_specs = [pl.BlockSpec((TM, 1024), lambda b, i: (tile0 + b * nq + i, 0)),
                pl.BlockSpec((seq, 1024), lambda b, i: (row_blk + b, 0)),
                pl.BlockSpec((seq, 512), lambda b, i: (row_blk + b, 0))]
    args = [q, k, v]
    if not ctx:
        in_specs += [pl.BlockSpec((TM, 1024), lambda b, i: (b, 0)), pl.BlockSpec((TM, 512), lambda b, i: (b, 0))]
        args += [kc, vc]
    return pl.pallas_call(
        functools.partial(_mla_attn_kernel, has_cache=not ctx),
        out_shape=jax.ShapeDtypeStruct((nb * seq, 512), BF16),
        grid=(nb, nq), in_specs=in_specs,
        out_specs=pl.BlockSpec((TM, 512), lambda b, i: (b * nq + i, 0)),
        compiler_params=_params(2),
        name="mla_attn_ctx" if ctx else "mla_attn_lat",
    )(*args)


def _merge_kernel(retc_ref, retl_ref, glac_ref, glal_ref, mlac_ref, mlal_ref, m0_ref, m1_ref, m2_ref,
                  xc_ref, xl_ref, wb_ref, wo_ref,
                  g1_ref, n2_ref, sh2_ref, sc2_ref, rwh_ref, rwl_ref, rb_ref,
                  x1_ref, h_ref, idx_ref, w_ref, rank_ref, cnt_ref):
    @pl.when(pl.program_id(0) == 0)
    def _():
        cnt_ref[...] = jnp.zeros_like(cnt_ref)

    is_ctx = pl.program_id(0) < N_CTX_TILES
    mix = None
    for c_ref, l_ref, m_ref, n in ((retc_ref, retl_ref, m0_ref, 0), (glac_ref, glal_ref, m1_ref, 1),
                                   (mlac_ref, mlal_ref, m2_ref, 2)):
        branch = jnp.where(is_ctx, c_ref[...], l_ref[...])
        term = _sigmoid(m_ref[...]).astype(F32) * _dot(branch, wb_ref[n])
        mix = term if mix is None else mix + term
    out = _dot(mix.astype(BF16), wo_ref[...])
    x1 = jnp.where(is_ctx, xc_ref[...], xl_ref[...]) + g1_ref[0] * out
    x1_ref[...] = x1
    h = x1 * lax.rsqrt(jnp.mean(x1 * x1, axis=-1, keepdims=True) + EPS) * n2_ref[...]
    h = h * (1.0 + sc2_ref[0]) + sh2_ref[0]
    h_ref[...] = h
    hh = h.astype(BF16)
    hl = (h - hh.astype(F32)).astype(BF16)
    logits = _dot(hh, rwh_ref[...]) + _dot(hh, rwl_ref[...]) + _dot(hl, rwh_ref[...]) + rb_ref[...]
    rows = TM // ROUTE_CHUNKS
    lane = lax.broadcasted_iota(jnp.int32, (rows, 128), 1)
    lanef = lane.astype(F32)
    onehots, osums = [], []
    for c in range(ROUTE_CHUNKS):
        l = jnp.where(lane < N_EXPERTS, logits[c * rows:(c + 1) * rows], -jnp.inf)
        vals, idxs = [], []
        for _ in range(TOP_K):
            m = jnp.max(l, axis=-1, keepdims=True)
            ix = jnp.min(jnp.where(l == m, lanef, 128.0), axis=-1, keepdims=True)
            vals.append(m)
            idxs.append(ix)
            l = jnp.where(lanef == ix, -jnp.inf, l)
        es = [jnp.exp(v - vals[0]) for v in vals]
        inv = 1.0 / (es[0] + es[1] + es[2] + es[3])
        idx_out = jnp.zeros((rows, 128), F32)
        w_out = jnp.zeros((rows, 128), F32)
        for kk in range(TOP_K):
            idx_out = jnp.where(lane == kk, idxs[kk], idx_out)
            w_out = jnp.where(lane == kk, es[kk] * inv, w_out)
        idx_ref[c * rows:(c + 1) * rows, :] = idx_out.astype(jnp.int32)
        w_ref[c * rows:(c + 1) * rows, :] = w_out
        oh = [jnp.where(lanef == ix, 1.0, 0.0) for ix in idxs]
        onehots.append(oh)
        osums.append((oh[0] + oh[1]) + (oh[2] + oh[3]))

    osum = jnp.concatenate(osums, axis=0)
    ri = lax.broadcasted_iota(jnp.int32, (TM, TM), 0)
    ci = lax.broadcasted_iota(jnp.int32, (TM, TM), 1)
    before = jnp.where(ri > ci, 1.0, 0.0).astype(BF16)
    prior = _dot(before, osum.astype(BF16)) + cnt_ref[0:1, :]
    for c in range(ROUTE_CHUNKS):
        pc = prior[c * rows:(c + 1) * rows]
        rank_out = jnp.zeros((rows, 128), F32)
        for kk in range(TOP_K):
            rank_out = jnp.where(lane == kk, jnp.sum(onehots[c][kk] * pc, axis=-1, keepdims=True), rank_out)
        rank_ref[c * rows:(c + 1) * rows, :] = rank_out.astype(jnp.int32)
    cnt_ref[...] = cnt_ref[...] + jnp.sum(osum, axis=0, keepdims=True)


def _merge(branches, z, xc, xl, modr, w, layer):
    mrow = _mod_row(TM)
    base = layer * N_MOD

    def mod_spec(part):
        return pl.BlockSpec((1, 1, D_MODEL), lambda i: ((base + mrow(i)) * 6 + part, 0, 0))

    const = lambda shape: pl.BlockSpec(shape, lambda i: (0,) * len(shape))
    row = lambda width: pl.BlockSpec((TM, width), lambda i: (i, 0))
    gate = lambda col: pl.BlockSpec((TM, 1024), lambda i: (i, col))
    return pl.pallas_call(
        _merge_kernel,
        out_shape=[jax.ShapeDtypeStruct((T_ALL, D_MODEL), F32), jax.ShapeDtypeStruct((T_ALL, D_MODEL), F32),
                   jax.ShapeDtypeStruct((T_ALL, 128), jnp.int32), jax.ShapeDtypeStruct((T_ALL, 128), F32),
                   jax.ShapeDtypeStruct((T_ALL, 128), jnp.int32), jax.ShapeDtypeStruct((8, 128), F32)],
        grid=(N_TILES,),
        in_specs=[_ctx_spec(TM, 512), _lat_spec(TM, 512)] * 3 + [gate(3), gate(4), gate(5),
                  _ctx_spec(TM, D_MODEL), _lat_spec(TM, D_MODEL),
                  const((3, 512, 1024)), const((1024, 1024)),
                  mod_spec(2), const((1, 1024)), mod_spec(3), mod_spec(4),
                  const((1024, 128)), const((1024, 128)), const((1, 128))],
        out_specs=[row(1024), row(1024), row(128), row(128), row(128), const((8, 128))],
        compiler_params=_params(1),
        name="merge",
    )(*branches, z, z, z, xc, xl, w["wb"], w["wo"], modr, w["n2_g"], modr, modr,
      w["rw_hi"], w["rw_lo"], w["rb"])


def _route(top_idx, rank, counts):
    flat_e = top_idx.reshape(N_SLOTS)
    onehot = (flat_e[:, None] == jnp.arange(N_EXPERTS, dtype=jnp.int32)[None, :]).astype(jnp.int32)
    padded = (counts + MOE_ROWS - 1) // MOE_ROWS * MOE_ROWS
    pad_end = jnp.cumsum(padded)
    pad_start = pad_end - padded
    dest = (rank.reshape(N_SLOTS) + jnp.sum(onehot * pad_start[None, :], axis=1)).astype(jnp.int32)
    blk_start = jnp.arange(N_MOE_BLOCKS, dtype=jnp.int32) * MOE_ROWS
    block_e = jnp.minimum(jnp.sum((pad_end[None, :] <= blk_start[:, None]).astype(jnp.int32), axis=1),
                          N_EXPERTS - 1).astype(jnp.int32)
    n_used = (pad_end[-1] // MOE_ROWS).astype(jnp.int32)
    nxt_blk = pad_end[block_e] // MOE_ROWS
    next_e = jnp.where(nxt_blk < n_used, block_e[jnp.minimum(nxt_blk, N_MOE_BLOCKS - 1)], block_e).astype(jnp.int32)
    tail_start = (pad_start + counts).astype(jnp.int32)
    return dest, tail_start, block_e, n_used.reshape(1), next_e


def _dispatch_kernel(dest_ref, tail_ref, nb_ref, h_ref, xs_ref, zero_buf, stage, sem, ssem):
    i = pl.program_id(0)

    @pl.when(i == 0)
    def _():
        zero_buf[...] = jnp.zeros_like(zero_buf)
        fills = [pltpu.make_async_copy(
            zero_buf, xs_ref.at[pl.ds(pl.multiple_of((tail_ref[e] // 8) * 8, 8), TAIL_FILL), :], sem)
            for e in range(N_EXPERTS)]
        for f in fills:
            f.start()
        for f in fills:
            f.wait()

        def fill_block(b, c):
            f = pltpu.make_async_copy(zero_buf.at[pl.ds(0, MOE_ROWS), :],
                                      xs_ref.at[pl.ds(pl.multiple_of(b * MOE_ROWS, MOE_ROWS), MOE_ROWS), :], sem)
            f.start()
            f.wait()
            return c
        lax.fori_loop(nb_ref[0], (N_MOE_ROWS + XS_EXTRA) // MOE_ROWS, fill_block, 0)

    base = i * TM * TOP_K
    cur = i % 2

    def wait_tile(slot):
        for kk in range(TOP_K):
            pltpu.make_async_copy(stage.at[slot], xs_ref.at[pl.ds(0, TM), :], ssem.at[slot]).wait()

    @pl.when(i >= 2)
    def _():
        wait_tile(cur)
    stage[cur] = h_ref[...]

    def issue(t, c):
        for kk in range(TOP_K):
            pltpu.make_async_copy(stage.at[cur, pl.ds(t, 1), :],
                                  xs_ref.at[pl.ds(dest_ref[base + t * TOP_K + kk], 1), :], ssem.at[cur]).start()
        return c
    lax.fori_loop(0, TM, issue, 0, unroll=8)

    @pl.when(i == N_TILES - 1)
    def _():
        wait_tile(1 - cur)
        wait_tile(cur)


def _moe_dispatch(h, dest, tail_start, n_used):
    return pl.pallas_call(
        _dispatch_kernel,
        out_shape=jax.ShapeDtypeStruct((N_MOE_ROWS + XS_EXTRA, D_MODEL), F32),
        grid_spec=pltpu.PrefetchScalarGridSpec(
            num_scalar_prefetch=3, grid=(N_TILES,),
            in_specs=[pl.BlockSpec((TM, D_MODEL), lambda i, d, t, nb: (i, 0))],
            out_specs=pl.BlockSpec(memory_space=pl.ANY),
            scratch_shapes=[pltpu.VMEM((TAIL_FILL, D_MODEL), F32), pltpu.VMEM((2, TM, D_MODEL), F32),
                            pltpu.SemaphoreType.DMA, pltpu.SemaphoreType.DMA((2,))]),
        compiler_params=_params(1),
        name="moe_dispatch",
    )(dest, tail_start, n_used, h)


def _expert_kernel(be_ref, nb_ref, nxt_ref, x_ref, wgu_hbm, bgu_ref, wd_hbm, bd_ref, o_ref,
                   wgu_st, wd_st, wgu_bf, wd_bf, sem, *, layer):
    i = pl.program_id(0)
    e = be_ref[i]
    prev = be_ref[jnp.maximum(i - 1, 0)]

    def weight_copies(expert):
        idx = layer * N_EXPERTS + expert
        return (pltpu.make_async_copy(wgu_hbm.at[idx], wgu_st, sem.at[0]),
                pltpu.make_async_copy(wd_hbm.at[idx], wd_st, sem.at[1]))

    @pl.when(i == 0)
    def _():
        for cp in weight_copies(e):
            cp.start()

    @pl.when(((i == 0) | (e != prev)) & (i < nb_ref[0]))
    def _():
        for cp in weight_copies(e):
            cp.wait()
        wgu_bf[...] = wgu_st[...].astype(BF16)
        wd_bf[...] = wd_st[...].astype(BF16)
        nxt = nxt_ref[i]

        @pl.when(nxt != e)
        def _():
            for cp in weight_copies(nxt):
                cp.start()

    @pl.when(i < nb_ref[0])
    def _():
        gu = _dot(x_ref[...].astype(BF16), wgu_bf[...]) + bgu_ref[0]
        gate = jnp.minimum(gu[:, :D_EXPERT], SWIGLU_LIMIT)
        up = jnp.clip(gu[:, D_EXPERT:], -SWIGLU_LIMIT, SWIGLU_LIMIT)
        act = (up + 1.0) * gate * _sigmoid(SWIGLU_ALPHA * gate)
        out = _dot(act.astype(BF16), wd_bf[...]) + bd_ref[0]
        o_ref[...] = out

    @pl.when(i >= nb_ref[0])
    def _():
        o_ref[...] = jnp.zeros_like(o_ref)


def _moe_experts(xs, block_e, n_used, next_e, w_gu, b_gu, w_down, b_down, layer):
    w_idx = lambda i, be, nb, nx: (layer * N_EXPERTS + be[i], 0, 0)
    return pl.pallas_call(
        functools.partial(_expert_kernel, layer=layer),
        out_shape=jax.ShapeDtypeStruct((N_MOE_ROWS, D_MODEL), F32),
        grid_spec=pltpu.PrefetchScalarGridSpec(
            num_scalar_prefetch=3, grid=(N_MOE_BLOCKS,),
            in_specs=[pl.BlockSpec((MOE_ROWS, D_MODEL), lambda i, be, nb, nx: (jnp.minimum(i, nb[0] - 1), 0)),
                      pl.BlockSpec(memory_space=pl.ANY),
                      pl.BlockSpec((1, 1, 2 * D_EXPERT), w_idx),
                      pl.BlockSpec(memory_space=pl.ANY),
                      pl.BlockSpec((1, 1, D_MODEL), w_idx)],
            out_specs=pl.BlockSpec((MOE_ROWS, D_MODEL), lambda i, be, nb, nx: (i, 0)),
            scratch_shapes=[pltpu.VMEM((D_MODEL, 2 * D_EXPERT), F32), pltpu.VMEM((D_EXPERT, D_MODEL), F32),
                            pltpu.VMEM((D_MODEL, 2 * D_EXPERT), BF16), pltpu.VMEM((D_EXPERT, D_MODEL), BF16),
                            pltpu.SemaphoreType.DMA((2,))]),
        compiler_params=_params(1),
        name="moe_experts",
    )(block_e, n_used, next_e, xs, w_gu, b_gu, w_down, b_down)


def _combine_kernel(dest_ref, x_ref, g2_ref, w_ref, eo_ref, yc_ref, yl_ref, buf, sem):
    i = pl.program_id(0)
    cur = i % 2

    def issue_tile(tile, slot):
        base = tile * TM * TOP_K

        def issue(t, c):
            for kk in range(TOP_K):
                pltpu.make_async_copy(eo_ref.at[pl.ds(dest_ref[base + t * TOP_K + kk], 1), :],
                                      buf.at[slot, kk, pl.ds(t, 1), :], sem.at[slot]).start()
            return c
        lax.fori_loop(0, TM, issue, 0, unroll=8)

    @pl.when(i == 0)
    def _():
        issue_tile(0, 0)

    @pl.when(i + 1 < N_TILES)
    def _():
        issue_tile(i + 1, 1 - cur)

    for kk in range(TOP_K):
        pltpu.make_async_copy(eo_ref.at[pl.ds(0, TM), :], buf.at[cur, kk], sem.at[cur]).wait()
    w = w_ref[...]
    ff = None
    for kk in range(TOP_K):
        term = buf[cur, kk] * w[:, kk:kk + 1]
        ff = term if ff is None else ff + term
    y = x_ref[...] + g2_ref[0] * ff
    is_ctx = i < N_CTX_TILES

    @pl.when(is_ctx)
    def _():
        yc_ref[...] = y

    @pl.when(jnp.logical_not(is_ctx))
    def _():
        yl_ref[...] = y


def _moe_combine(dest, x1, modr, top_w, eo, layer):
    mrow = _mod_row(TM)
    base = layer * N_MOD
    return pl.pallas_call(
        _combine_kernel,
        out_shape=[jax.ShapeDtypeStruct((T_CTX, D_MODEL), F32), jax.ShapeDtypeStruct((T_LAT, D_MODEL), F32)],
        grid_spec=pltpu.PrefetchScalarGridSpec(
            num_scalar_prefetch=1, grid=(N_TILES,),
            in_specs=[pl.BlockSpec((TM, D_MODEL), lambda i, d: (i, 0)),
                      pl.BlockSpec((1, 1, D_MODEL), lambda i, d: ((base + mrow(i)) * 6 + 5, 0, 0)),
                      pl.BlockSpec((TM, 128), lambda i, d: (i, 0)),
                      pl.BlockSpec(memory_space=pl.ANY)],
            out_specs=[pl.BlockSpec((TM, D_MODEL), lambda i, d: (jnp.minimum(i, N_CTX_TILES - 1), 0)),
                       pl.BlockSpec((TM, D_MODEL), lambda i, d: (jnp.maximum(i - N_CTX_TILES, 0), 0))],
            scratch_shapes=[pltpu.VMEM((2, TOP_K, TM, D_MODEL), F32), pltpu.SemaphoreType.DMA((2,))]),
        compiler_params=_params(1),
        name="moe_combine",
    )(dest, x1, modr, top_w, eo)


def _pad_heads(w, n_heads, width):
    lead = w.shape[:-1]
    w = w.reshape(lead + (n_heads, width))
    w = jnp.pad(w, [(0, 0)] * len(lead) + [(0, 0), (0, HEAD_PAD - width)])
    return w.reshape(lead + (n_heads * HEAD_PAD,))


def _layer_weights(l, w_in, gla_wa2, gla_ba, mla_qa_g, mla_wuq, mla_kva_g, mla_wukv, mla_qn_g, mla_kn_g,
                   w_branch, w_out, router_w, router_b, norm2_g):
    wi = w_in[l]
    w_in_p = jnp.concatenate([wi[:, :3072], wi[:, 3520:], wi[:, 3104:3520], wi[:, 3072:3104],
                              jnp.zeros((D_MODEL, 64), F32)], axis=1).astype(BF16)
    wa_p = jnp.zeros((512, 512), F32)
    wa_p = wa_p.at[416:432, 0:256].set(gla_wa2[l, 0]).at[432:448, 256:512].set(gla_wa2[l, 1]).astype(BF16)
    ba_p = gla_ba[l].reshape(1, 512)
    wukv = mla_wukv[l].reshape(128, MLA_HEADS, MLA_NOPE + MLA_V)
    rw = jnp.pad(router_w[l], ((0, 0), (0, 128 - N_EXPERTS)))
    rw_hi = rw.astype(BF16)
    return {
        "w_in": w_in_p, "wa": wa_p, "ba": ba_p,
        "qa_g": mla_qa_g[l].reshape(1, 256),
        "wuq": _pad_heads(mla_wuq[l], MLA_HEADS, MLA_QK).astype(BF16),
        "qn_g": jnp.pad(mla_qn_g[l], (0, HEAD_PAD - MLA_QK)).reshape(1, 128),
        "kn_g": jnp.pad(mla_kn_g[l], (0, HEAD_PAD - MLA_QK)).reshape(1, 128),
        "kva_g": mla_kva_g[l].reshape(1, 128),
        "wk": _pad_heads(wukv[:, :, :MLA_NOPE].reshape(128, MLA_HEADS * MLA_NOPE), MLA_HEADS, MLA_NOPE).astype(BF16),
        "wv": wukv[:, :, MLA_NOPE:].reshape(128, MLA_HEADS * MLA_V).astype(BF16),
        "wb": w_branch[l].astype(BF16), "wo": w_out[l].astype(BF16),
        "rw_hi": rw_hi, "rw_lo": (rw - rw_hi.astype(F32)).astype(BF16),
        "rb": jnp.pad(router_b[l], (0, 128 - N_EXPERTS)).reshape(1, 128),
        "n2_g": norm2_g[l].reshape(1, D_MODEL),
    }


def kernel(x_prompt, x_sample, cache_mla_ckv, cache_mla_krope, state_ret, state_gla, c, c_ctx, w_mod, b_mod, norm1_g, norm2_g, w_in, ret_gn_g, gla_wa2, gla_ba, gla_norm_g, mla_qa_g, mla_wuq, mla_kva_g, mla_wukv, mla_qn_g, mla_kn_g, w_branch, w_out, router_w, router_b, moe_w_gu, moe_b_gu, moe_w_down, moe_b_down):
    xc, xl = x_prompt.reshape(T_CTX, D_MODEL), x_sample.reshape(T_LAT, D_MODEL)
    cc = jnp.concatenate([c_ctx[None, :], c, jnp.zeros((N_MOD - 1 - N_LAT_SEQ, D_MODEL), F32)], axis=0)
    modr = _modulation(cc, w_mod, b_mod).reshape(DEPTH * N_MOD * 6, 1, D_MODEL)
    rope_tabs = _rope_tables()
    w_gu = moe_w_gu.reshape(DEPTH * N_EXPERTS, D_MODEL, 2 * D_EXPERT)
    b_gu = moe_b_gu.reshape(DEPTH * N_EXPERTS, 1, 2 * D_EXPERT)
    w_dn = moe_w_down.reshape(DEPTH * N_EXPERTS, D_EXPERT, D_MODEL)
    b_dn = moe_b_down.reshape(DEPTH * N_EXPERTS, 1, D_MODEL)

    ckv_l, krope_l, ret_l, gla_l = [], [], [], []
    for l in range(DEPTH):
        w = _layer_weights(l, w_in, gla_wa2, gla_ba, mla_qa_g, mla_wuq, mla_kva_g, mla_wukv, mla_qn_g, mla_kn_g,
                           w_branch, w_out, router_w, router_b, norm2_g)
        z = _in_proj(xc, xl, norm1_g[l].reshape(1, D_MODEL), modr, w["w_in"], l)

        gn = ret_gn_g[l].reshape(1, 512)
        ret_c, ret_state = _retention(z, gn, None, ctx=True)
        (ret_s,) = _retention(z, gn, state_ret[:, l], ctx=False)
        gng = gla_norm_g[l].reshape(1, 512)
        gla_c, gla_state = _gla(z, w["wa"], w["ba"], gng, None, ctx=True)
        gla_s, _ = _gla(z, w["wa"], w["ba"], gng, state_gla[:, l], ctx=False)

        q, k, v, ckv, kr = _mla_prep(z, w, rope_tabs)
        kc, vc = _mla_cache(cache_mla_ckv[:, l].reshape(N_LAT_SEQ * CTX_LEN, 128),
                            jnp.pad(cache_mla_krope[:, l].reshape(N_LAT_SEQ * CTX_LEN, MLA_ROPE),
                                    ((0, 0), (0, 128 - MLA_ROPE))), w)
        mla_c = _mla_attn(q, k, v, None, None, ctx=True)
        mla_s = _mla_attn(q, k, v, kc, vc, ctx=False)

        x1, h2, top_idx, top_w, rank, cnt = _merge((ret_c, ret_s, gla_c, gla_s, mla_c, mla_s), z, xc, xl, modr, w, l)
        dest, tail_start, block_e, n_used, next_e = _route(top_idx[:, :TOP_K], rank[:, :TOP_K],
                                                           cnt[0, :N_EXPERTS].astype(jnp.int32))
        xs = _moe_dispatch(h2, dest, tail_start, n_used)
        eo = _moe_experts(xs, block_e, n_used, next_e, w_gu, b_gu, w_dn, b_dn, l)
        xc, xl = _moe_combine(dest, x1, modr, top_w, eo, l)

        ckv_l.append(ckv[:T_CTX].reshape(N_CTX_SEQ, CTX_LEN, 128))
        krope_l.append(kr[:T_CTX, :MLA_ROPE].reshape(N_CTX_SEQ, CTX_LEN, MLA_ROPE))
        ret_l.append(ret_state)
        gla_l.append(gla_state.reshape(N_CTX_SEQ, 2, N_HEADS, DK, DV))

    y_p = xc.reshape(N_CTX_SEQ, CTX_LEN, D_MODEL)
    y_s = xl.reshape(N_LAT_SEQ, LAT_LEN, D_MODEL)
    return (y_p, y_s, jnp.stack(ckv_l, axis=1), jnp.stack(krope_l, axis=1),
            jnp.stack(ret_l, axis=1), jnp.stack(gla_l, axis=1))
```

```python
import functools

import jax
import jax.numpy as jnp
import numpy as np
from jax import lax
from jax.experimental import pallas as pl
from jax.experimental.pallas import tpu as pltpu

F32 = jnp.float32
BF16 = jnp.bfloat16

D_MODEL = 1024
DEPTH = 2
N_CTX_SEQ, CTX_LEN = 32, 256
N_LAT_SEQ, LAT_LEN = 4, 1024
T_CTX = N_CTX_SEQ * CTX_LEN
T_LAT = N_LAT_SEQ * LAT_LEN
T_ALL = T_CTX + T_LAT
TM = 256
N_TILES = T_ALL // TM
N_CTX_TILES = T_CTX // TM
LAT_TILES = LAT_LEN // TM
N_MOD = 8
EPS = 1e-6

N_HEADS = 4
DK, DV = 64, 128
GRID_W = 64
MLA_HEADS, MLA_NOPE, MLA_ROPE, MLA_V = 8, 64, 32, 64
MLA_QK = MLA_NOPE + MLA_ROPE
HEAD_PAD = 128
GLA_TAU = 16.0
N_EXPERTS, TOP_K, D_EXPERT = 32, 4, 1024
SWIGLU_LIMIT, SWIGLU_ALPHA = 7.0, 1.702
MOE_ROWS = 256
N_SLOTS = T_ALL * TOP_K
N_MOE_BLOCKS = N_SLOTS // MOE_ROWS + N_EXPERTS
N_MOE_ROWS = N_MOE_BLOCKS * MOE_ROWS
ROUTE_CHUNKS = 4
TAIL_FILL = MOE_ROWS + 8
XS_EXTRA = 2 * MOE_ROWS

DZ = 6656
IN_TILE = 512
VMEM_LIMIT = 56 * 1024 * 1024

RET_LOG_F = [float(np.log1p(-np.exp2(-(5.0 + h)))) for h in range(N_HEADS)]
RET_LOG_B = [float(np.log1p(-np.exp2(-(5.5 + h)))) for h in range(N_HEADS)]


def _params(n_axes, vmem=VMEM_LIMIT):
    return pltpu.CompilerParams(dimension_semantics=("arbitrary",) * n_axes, vmem_limit_bytes=vmem)


def _sigmoid(x):
    return 1.0 / (1.0 + jnp.exp(-x))


def _dot(a, b):
    return jnp.dot(a, b, preferred_element_type=F32)


def _dot_t(a, b):
    return lax.dot_general(a, b, (((1,), (1,)), ((), ())), preferred_element_type=F32)


def _mod_row(tile_rows):
    def f(i):
        r0 = i * tile_rows
        return jnp.where(r0 < T_CTX, 0, 1 + (r0 - T_CTX) // LAT_LEN)
    return f


def _ctx_spec(rows, width):
    n_ctx = T_CTX // rows
    return pl.BlockSpec((rows, width), lambda i: (jnp.minimum(i, n_ctx - 1), 0))


def _lat_spec(rows, width):
    n_ctx = T_CTX // rows
    return pl.BlockSpec((rows, width), lambda i: (jnp.maximum(i - n_ctx, 0), 0))


def _mod_kernel(c_ref, w_ref, b_ref, o_ref):
    c = c_ref[...]
    s = c * _sigmoid(c)
    o_ref[0] = jnp.dot(s, w_ref[0], preferred_element_type=F32, precision=lax.Precision.HIGHEST) + b_ref[0]


def _modulation(cc, w_mod, b_mod):
    n = 6 * D_MODEL
    blk = 2048
    return pl.pallas_call(
        _mod_kernel,
        out_shape=jax.ShapeDtypeStruct((DEPTH, N_MOD, n), F32),
        grid=(DEPTH, n // blk),
        in_specs=[pl.BlockSpec((N_MOD, D_MODEL), lambda l, j: (0, 0)),
                  pl.BlockSpec((1, D_MODEL, blk), lambda l, j: (l, 0, j)),
                  pl.BlockSpec((1, 1, blk), lambda l, j: (l, 0, j))],
        out_specs=pl.BlockSpec((1, N_MOD, blk), lambda l, j: (l, 0, j)),
        compiler_params=_params(2),
        name="modulation",
    )(cc, w_mod, b_mod.reshape(DEPTH, 1, n))


def _in_kernel(xc_ref, xl_ref, g_ref, sh_ref, sc_ref, w_ref, o_ref):
    x = jnp.where(pl.program_id(0) < T_CTX // IN_TILE, xc_ref[...], xl_ref[...])
    h = x * lax.rsqrt(jnp.mean(x * x, axis=-1, keepdims=True) + EPS) * g_ref[...]
    h = h * (1.0 + sc_ref[0]) + sh_ref[0]
    hb = h.astype(BF16)
    for n0 in range(0, DZ, 512):
        o_ref[:, n0:n0 + 512] = _dot(hb, w_ref[:, n0:n0 + 512]).astype(BF16)


def _in_proj(xc, xl, g, modr, w_in_p, layer):
    mrow = _mod_row(IN_TILE)
    base = layer * N_MOD

    def mod_spec(part):
        return pl.BlockSpec((1, 1, D_MODEL), lambda i: ((base + mrow(i)) * 6 + part, 0, 0))

    return pl.pallas_call(
        _in_kernel,
        out_shape=jax.ShapeDtypeStruct((T_ALL, DZ), BF16),
        grid=(T_ALL // IN_TILE,),
        in_specs=[_ctx_spec(IN_TILE, D_MODEL), _lat_spec(IN_TILE, D_MODEL),
                  pl.BlockSpec((1, D_MODEL), lambda i: (0, 0)),
                  mod_spec(0), mod_spec(1),
                  pl.BlockSpec((D_MODEL, DZ), lambda i: (0, 0))],
        out_specs=pl.BlockSpec((IN_TILE, DZ), lambda i: (i, 0)),
        compiler_params=_params(1),
        name="in_proj",
    )(xc, xl, g, modr, modr, w_in_p)


def _lane_half_mask(hh):
    lane = lax.broadcasted_iota(jnp.int32, (1, 128), 1)
    return (lane < 64) if hh == 0 else (lane >= 64)


def _ret_kernel(*refs, seq, has_state, emit_state):
    q_ref, k_ref, v_ref, g_ref, gn_ref = refs[:5]
    pos = 5
    if has_state:
        s0_ref = refs[pos]
        pos += 1
    o_ref = refs[pos]
    pos += 1
    if emit_state:
        st_ref = refs[pos]

    r0 = pl.multiple_of(pl.program_id(1) * TM, TM)
    qb = q_ref[pl.ds(r0, TM), :]
    rowi = lax.broadcasted_iota(jnp.int32, (TM, seq), 0) + r0
    colj = lax.broadcasted_iota(jnp.int32, (TM, seq), 1)
    diff = (rowi - colj).astype(F32)
    on_diag = rowi == colj
    ri = (lax.broadcasted_iota(jnp.int32, (TM, 1), 0) + r0).astype(F32)

    for h in range(N_HEADS):
        p, hh = h // 2, h % 2
        lanes = slice(128 * p, 128 * p + 128)
        qp = qb[:, lanes]
        qh = jnp.where(_lane_half_mask(hh), qp, jnp.zeros_like(qp))
        sc = _dot_t(qh, k_ref[:, lanes])
        ex = jnp.where(diff > 0, RET_LOG_F[h] * diff, -RET_LOG_B[h] * diff)
        dec = jnp.exp(ex) * jnp.where(on_diag, 2.0 * DK ** -0.5, DK ** -0.5)
        o = _dot((sc * dec).astype(BF16), v_ref[:, 128 * h:128 * h + 128])
        if has_state:
            qf = qh.astype(F32)
            o += _dot((qf * jnp.exp(RET_LOG_F[h] * (ri + 1.0))).astype(BF16), s0_ref[0, 0, p].astype(BF16))
            o += _dot((qf * jnp.exp(RET_LOG_B[h] * (seq - ri))).astype(BF16), s0_ref[0, 1, p].astype(BF16))
        mu = jnp.mean(o, axis=-1, keepdims=True)
        d = o - mu
        var = jnp.mean(d * d, axis=-1, keepdims=True)
        on = d * lax.rsqrt(var + EPS)
        g = g_ref[:, 128 * h:128 * h + 128].astype(F32)
        out = on * gn_ref[:, 128 * h:128 * h + 128] * (g * _sigmoid(g))
        o_ref[:, 128 * h:128 * h + 128] = out.astype(BF16)

    if emit_state:
        jc = lax.broadcasted_iota(jnp.int32, (seq, 1), 0).astype(F32)
        lane = lax.broadcasted_iota(jnp.int32, (1, 128), 1)
        for p in range(2):
            kp = k_ref[:, 128 * p:128 * p + 128].astype(F32) * DK ** -0.5
            lgf = jnp.where(lane < 64, RET_LOG_F[2 * p], RET_LOG_F[2 * p + 1])
            lgb = jnp.where(lane < 64, RET_LOG_B[2 * p], RET_LOG_B[2 * p + 1])
            kdf = (kp * jnp.exp(lgf * (seq - 1.0 - jc))).T.astype(BF16)
            kdb = (kp * jnp.exp(lgb * jc)).T.astype(BF16)
            for hh in range(2):
                h = 2 * p + hh
                vh = v_ref[:, 128 * h:128 * h + 128]
                st_ref[0, 0, h] = _dot(kdf, vh)[64 * hh:64 * hh + 64, :]
                st_ref[0, 1, h] = _dot(kdb, vh)[64 * hh:64 * hh + 64, :]


def _retention(z, gn_g, s0, *, ctx):
    if ctx:
        nb, seq, row_blk, tile0 = N_CTX_SEQ, CTX_LEN, 0, 0
    else:
        nb, seq, row_blk, tile0 = N_LAT_SEQ, LAT_LEN, T_CTX // LAT_LEN, N_CTX_TILES
    nq = seq // TM
    in_specs = [pl.BlockSpec((seq, 256), lambda b, i: (row_blk + b, 0)),
                pl.BlockSpec((seq, 256), lambda b, i: (row_blk + b, 1)),
                pl.BlockSpec((seq, 512), lambda b, i: (row_blk + b, 1)),
                pl.BlockSpec((TM, 512), lambda b, i: (tile0 + b * nq + i, 2)),
                pl.BlockSpec((1, 512), lambda b, i: (0, 0))]
    args = [z, z, z, z, gn_g]
    out_shape = [jax.ShapeDtypeStruct((nb * seq, 512), BF16)]
    out_specs = [pl.BlockSpec((TM, 512), lambda b, i: (b * nq + i, 0))]
    if not ctx:
        in_specs.append(pl.BlockSpec((1, 2, 2, 128, 128), lambda b, i: (b, 0, 0, 0, 0)))
        args.append(s0.reshape(N_LAT_SEQ, 2, 2, 128, 128))
    else:
        out_shape.append(jax.ShapeDtypeStruct((nb, 2, N_HEADS, DK, DV), F32))
        out_specs.append(pl.BlockSpec((1, 2, N_HEADS, DK, DV), lambda b, i: (b, 0, 0, 0, 0)))
    return pl.pallas_call(
        functools.partial(_ret_kernel, seq=seq, has_state=not ctx, emit_state=ctx),
        out_shape=out_shape, grid=(nb, nq), in_specs=in_specs, out_specs=out_specs,
        compiler_params=_params(2),
        name="retention_ctx" if ctx else "retention_lat",
    )(*args)


def _gla_decay(small_ref, wa_ref, ba_ref):
    x = _dot(small_ref[...], wa_ref[...]) + ba_ref[...]
    la = -(jnp.maximum(-x, 0.0) + jnp.log(1.0 + jnp.exp(-jnp.abs(x)))) * (1.0 / GLA_TAU)
    ri = lax.broadcasted_iota(jnp.int32, (TM, TM), 0)
    ci = lax.broadcasted_iota(jnp.int32, (TM, TM), 1)
    ltri = jnp.where(ri >= ci, 1.0, 0.0).astype(BF16)
    hi = la.astype(BF16)
    r1 = la - hi.astype(F32)
    mid = r1.astype(BF16)
    lo = (r1 - mid.astype(F32)).astype(BF16)
    cum = _dot(ltri, hi) + _dot(ltri, mid) + _dot(ltri, lo)
    return la, cum


def _gla_state_kernel(k_ref, v_ref, small_ref, wa_ref, ba_ref, kv_ref, tot_ref):
    la, cum = _gla_decay(small_ref, wa_ref, ba_ref)
    bf, bb = cum[:, :256], cum[:, 256:]
    xb = bb - la[:, 256:]
    k = k_ref[...].astype(F32)
    kdf = k * jnp.exp(bf[TM - 1:TM, :] - bf)
    kdb = k * jnp.exp(xb)
    for p in range(2):
        kf_t = kdf[:, 128 * p:128 * p + 128].T.astype(BF16)
        kb_t = kdb[:, 128 * p:128 * p + 128].T.astype(BF16)
        for hh in range(2):
            h = 2 * p + hh
            vh = v_ref[:, 128 * h:128 * h + 128]
            kv_ref[0, 0, 0, h] = _dot(kf_t, vh)[64 * hh:64 * hh + 64, :]
            kv_ref[0, 0, 1, h] = _dot(kb_t, vh)[64 * hh:64 * hh + 64, :]
    tot_ref[0, 0] = jnp.sum(la.T, axis=-1, keepdims=True)


def _mid_bcast(x, s, r):
    w = 2 * s
    if w >= 8:
        n = TM // w
        x3 = x.reshape(n, w, 256)
        return jnp.broadcast_to(x3[:, r:r + 1, :], (n, w, 256)).reshape(TM, 256)
    x3 = x.reshape(TM // 8, 8, 256)
    sub = lax.broadcasted_iota(jnp.int32, (1, 8, 1), 1)
    out = None
    for blk in range(8 // w):
        rowv = jnp.broadcast_to(x3[:, blk * w + r:blk * w + r + 1, :], (TM // 8, 8, 256))
        out = rowv if out is None else jnp.where(sub >= blk * w, rowv, out)
    return out.reshape(TM, 256)


def _gla_kernel(*refs, n_blk, has_state, emit_state):
    q_ref, k_ref, v_ref, g_ref, small_ref, wa_ref, ba_ref, gn_ref = refs[:8]
    pos = 8
    if has_state:
        kv_ref, tot_ref, s0_ref = refs[pos:pos + 3]
        pos += 3
    o_ref = refs[pos]
    pos += 1
    if emit_state:
        kvo_ref = refs[pos]

    la, cum = _gla_decay(small_ref, wa_ref, ba_ref)
    bf, bb = cum[:, :256], cum[:, 256:]
    xb = bb - la[:, 256:]
    q = q_ref[...].astype(F32) * DK ** -0.5
    k = k_ref[...].astype(F32)
    row = lax.broadcasted_iota(jnp.int32, (TM, 1), 0)
    rowi = lax.broadcasted_iota(jnp.int32, (TM, TM), 0)
    colj = lax.broadcasted_iota(jnp.int32, (TM, TM), 1)
    low_half = _lane_half_mask(0)

    def join(fwd, bwd):
        ops = []
        for p in range(N_HEADS // 2):
            f = fwd[:, 128 * p:128 * p + 128]
            br = pltpu.roll(bwd[:, 128 * p:128 * p + 128], 64, 1)
            ops.append(jnp.where(low_half, f, br).astype(BF16))
            ops.append(jnp.where(low_half, br, f).astype(BF16))
        return ops

    acc = []
    for qo, ko in zip(join(q, q), join(k, k)):
        acc.append(jnp.where(rowi == colj, _dot_t(qo, ko), 0.0))

    s = 1
    while s < TM:
        upper = ((row // s) % 2) == 1
        mf = _mid_bcast(bf, s, s - 1)
        mb = _mid_bcast(xb, s, s)
        ef = jnp.exp(-jnp.abs(bf - mf))
        eb = jnp.exp(-jnp.abs(xb - mb))
        qs = join(jnp.where(upper, q * ef, 0.0), jnp.where(upper, 0.0, q * eb))
        ks = join(jnp.where(upper, 0.0, k * ef), jnp.where(upper, k * eb, 0.0))
        same = (rowi // (2 * s)) == (colj // (2 * s))
        for h in range(N_HEADS):
            sl = _dot_t(qs[h], ks[h])
            acc[h] = acc[h] + (jnp.where(same, sl, 0.0) if 2 * s < TM else sl)
        s *= 2

    if has_state:
        n = pl.program_id(1)
        q_state = join(q * jnp.exp(bf), q * jnp.exp(bb[TM - 1:TM, :] - xb))

    if emit_state:
        kdf = k * jnp.exp(bf[TM - 1:TM, :] - bf)
        kdb = k * jnp.exp(xb)
        for p in range(N_HEADS // 2):
            kf_t = kdf[:, 128 * p:128 * p + 128].T.astype(BF16)
            kb_t = kdb[:, 128 * p:128 * p + 128].T.astype(BF16)
            for hh in range(2):
                h = 2 * p + hh
                vh = v_ref[:, 128 * h:128 * h + 128]
                kvo_ref[0, 0, 0, h] = _dot(kf_t, vh)[64 * hh:64 * hh + 64, :]
                kvo_ref[0, 0, 1, h] = _dot(kb_t, vh)[64 * hh:64 * hh + 64, :]

    for h in range(N_HEADS):
        o = _dot(acc[h].astype(BF16), v_ref[:, 128 * h:128 * h + 128])
        if has_state:
            sf = s0_ref[0, 0, h]
            for m in range(n_blk - 1):
                dec = jnp.exp(tot_ref[0, m, 64 * h:64 * h + 64, :])
                sf = jnp.where(m < n, dec * sf + kv_ref[0, m, 0, h], sf)
            sb = s0_ref[0, 1, h]
            for m in range(n_blk - 1, 0, -1):
                dec = jnp.exp(tot_ref[0, m, 256 + 64 * h:256 + 64 * h + 64, :])
                sb = jnp.where(m > n, dec * sb + kv_ref[0, m, 1, h], sb)
            state = jnp.concatenate([sf, sb] if h % 2 == 0 else [sb, sf], axis=0).astype(BF16)
            o += _dot(q_state[h], state)
        on = o * lax.rsqrt(jnp.mean(o * o, axis=-1, keepdims=True) + EPS)
        g = g_ref[:, 128 * h:128 * h + 128].astype(F32)
        out = on * gn_ref[:, 128 * h:128 * h + 128] * (g * _sigmoid(g))
        o_ref[:, 128 * h:128 * h + 128] = out.astype(BF16)


def _gla(z, wa_p, ba_p, gn_g, s0, *, ctx):
    if ctx:
        nb, n_blk, tile0 = N_CTX_SEQ, 1, 0
    else:
        nb, n_blk, tile0 = N_LAT_SEQ, LAT_TILES, N_CTX_TILES

    def zspec(width, col):
        return pl.BlockSpec((TM, width), lambda b, n: (tile0 + b * n_blk + n, col))

    w_specs = [pl.BlockSpec((512, 512), lambda b, n: (0, 0)), pl.BlockSpec((1, 512), lambda b, n: (0, 0))]
    kv_shape = jax.ShapeDtypeStruct((nb, n_blk, 2, N_HEADS, DK, DV), F32)
    kv_spec = pl.BlockSpec((1, 1, 2, N_HEADS, DK, DV), lambda b, n: (b, n, 0, 0, 0, 0))
    in_specs = [zspec(256, 6), zspec(256, 7), zspec(512, 4), zspec(512, 5), zspec(512, 12)] + w_specs
    in_specs.append(pl.BlockSpec((1, 512), lambda b, n: (0, 0)))
    args = [z, z, z, z, z, wa_p, ba_p, gn_g]
    out_shape = [jax.ShapeDtypeStruct((nb * n_blk * TM, 512), BF16)]
    out_specs = [pl.BlockSpec((TM, 512), lambda b, n: (b * n_blk + n, 0))]
    if ctx:
        out_shape.append(kv_shape)
        out_specs.append(kv_spec)
    else:
        kv, tot = pl.pallas_call(
            _gla_state_kernel,
            out_shape=[kv_shape, jax.ShapeDtypeStruct((nb, n_blk, 512, 1), F32)],
            grid=(nb, n_blk),
            in_specs=[zspec(256, 7), zspec(512, 4), zspec(512, 12)] + w_specs,
            out_specs=[kv_spec, pl.BlockSpec((1, 1, 512, 1), lambda b, n: (b, n, 0, 0))],
            compiler_params=_params(2),
            name="gla_state_lat",
        )(z, z, z, wa_p, ba_p)
        in_specs += [pl.BlockSpec((1, n_blk, 2, N_HEADS, DK, DV), lambda b, n: (b, 0, 0, 0, 0, 0)),
                     pl.BlockSpec((1, n_blk, 512, 1), lambda b, n: (b, 0, 0, 0)),
                     pl.BlockSpec((1, 2, N_HEADS, DK, DV), lambda b, n: (b, 0, 0, 0, 0))]
        args += [kv, tot, s0]
    res = pl.pallas_call(
        functools.partial(_gla_kernel, n_blk=n_blk, has_state=not ctx, emit_state=ctx),
        out_shape=out_shape, grid=(nb, n_blk), in_specs=in_specs, out_specs=out_specs,
        compiler_params=_params(2),
        name="gla_ctx" if ctx else "gla_lat",
    )(*args)
    return (res[0], res[1]) if ctx else (res[0], None)


def _rope_tables():
    nf = MLA_ROPE // 4
    pos = np.arange(LAT_LEN)
    freqs = (10000.0 ** (-np.arange(nf, dtype=np.float32) / nf)).astype(np.float32)
    ang_r = ((pos // GRID_W).astype(np.float32)[:, None] * freqs).astype(np.float32)
    ang_c = ((pos % GRID_W).astype(np.float32)[:, None] * freqs).astype(np.float32)
    cos = np.ones((TM + LAT_LEN, HEAD_PAD), np.float32)
    sa = np.zeros((TM + LAT_LEN, HEAD_PAD), np.float32)
    sb = np.zeros((TM + LAT_LEN, HEAD_PAD), np.float32)
    o = MLA_NOPE
    for base, ang in ((o, ang_r), (o + 2 * nf, ang_c)):
        cos[TM:, base:base + nf] = np.cos(ang)
        cos[TM:, base + nf:base + 2 * nf] = np.cos(ang)
        sa[TM:, base:base + nf] = -np.sin(ang)
        sb[TM:, base + nf:base + 2 * nf] = np.sin(ang)
    return jnp.asarray(cos), jnp.asarray(sa), jnp.asarray(sb)


def _rope(x, cos, sa, sb):
    return x * cos + pltpu.roll(x, 128 - 8, 1) * sa + pltpu.roll(x, 8, 1) * sb


def _head_segments():
    seg = np.zeros((MLA_HEADS * HEAD_PAD, 128), np.float32)
    for h in range(MLA_HEADS):
        seg[h * HEAD_PAD:(h + 1) * HEAD_PAD, h] = 1.0
    return jnp.asarray(seg, BF16), jnp.asarray(seg.T.copy(), BF16)


def _head_norm(x, gain, seg_ref, segt_ref):
    ss = _dot((x * x).astype(BF16), seg_ref[...])
    rs = lax.rsqrt(ss * (1.0 / MLA_QK) + EPS)
    hi = rs.astype(BF16)
    lo = (rs - hi.astype(F32)).astype(BF16)
    return x * (_dot(hi, segt_ref[...]) + _dot(lo, segt_ref[...])) * gain


def _store_heads(x, o_ref, rope):
    if rope is None:
        o_ref[...] = x.astype(BF16)
    else:
        for h in range(MLA_HEADS):
            o_ref[:, 128 * h:128 * h + 128] = _rope(x[:, 128 * h:128 * h + 128], *rope).astype(BF16)


def _mla_keys(ckv, kr_tile, wk_ref, wv_ref, kn_ref, seg_ref, segt_ref, k_ref, v_ref, rope=None):
    cb = ckv.astype(BF16)
    kpre = _dot(cb, wk_ref[...]) + jnp.concatenate([kr_tile] * MLA_HEADS, axis=1)
    v_ref[...] = _dot(cb, wv_ref[...]).astype(BF16)
    _store_heads(_head_norm(kpre, kn_ref[...], seg_ref, segt_ref), k_ref, rope)


def _mla_prep_kernel(small_ref, qa_ref, wuq_ref, qn_ref, kva_ref, wk_ref, wv_ref, kn_ref, seg_ref, segt_ref,
                     cos_ref, sa_ref, sb_ref, q_ref, k_ref, v_ref, ckv_ref, kr_ref):
    def body(rope):
        sm = small_ref[...].astype(F32)
        cq, ckv_raw, g3 = sm[:, 0:256], sm[:, 256:384], sm[:, 384:512]
        cqn = cq * lax.rsqrt(jnp.mean(cq * cq, axis=-1, keepdims=True) + EPS) * qa_ref[...]
        q = _dot(cqn.astype(BF16), wuq_ref[...])
        _store_heads(_head_norm(q, qn_ref[...], seg_ref, segt_ref), q_ref, rope)
        ckv = ckv_raw * lax.rsqrt(jnp.mean(ckv_raw * ckv_raw, axis=-1, keepdims=True) + EPS) * kva_ref[...]
        ckv_ref[...] = ckv
        lane = lax.broadcasted_iota(jnp.int32, (1, 128), 1)
        kr = jnp.where(lane < MLA_ROPE, g3, 0.0)
        kr_ref[...] = kr
        _mla_keys(ckv, pltpu.roll(kr, MLA_NOPE, 1), wk_ref, wv_ref, kn_ref, seg_ref, segt_ref, k_ref, v_ref, rope)

    is_ctx = pl.program_id(0) < N_CTX_TILES

    @pl.when(is_ctx)
    def _():
        body(None)

    @pl.when(jnp.logical_not(is_ctx))
    def _():
        body((cos_ref[...], sa_ref[...], sb_ref[...]))


def _mla_cache_kernel(ckv_ref, kr_ref, wk_ref, wv_ref, kn_ref, seg_ref, segt_ref, k_ref, v_ref):
    _mla_keys(ckv_ref[...], pltpu.roll(kr_ref[...], MLA_NOPE, 1), wk_ref, wv_ref, kn_ref, seg_ref, segt_ref,
              k_ref, v_ref)


def _mla_prep(z, w, rope_tabs):
    def rope_blk(i):
        return jnp.where(i < N_CTX_TILES, 0, 1 + (i - N_CTX_TILES) % LAT_TILES)

    const = lambda shape: pl.BlockSpec(shape, lambda i: (0,) * len(shape))
    rope_spec = pl.BlockSpec((TM, HEAD_PAD), lambda i: (rope_blk(i), 0))
    row = lambda width: pl.BlockSpec((TM, width), lambda i: (i, 0))
    return pl.pallas_call(
        _mla_prep_kernel,
        out_shape=[jax.ShapeDtypeStruct((T_ALL, 1024), BF16), jax.ShapeDtypeStruct((T_ALL, 1024), BF16),
                   jax.ShapeDtypeStruct((T_ALL, 512), BF16), jax.ShapeDtypeStruct((T_ALL, 128), F32),
                   jax.ShapeDtypeStruct((T_ALL, 128), F32)],
        grid=(N_TILES,),
        in_specs=[pl.BlockSpec((TM, 512), lambda i: (i, 12)),
                  const((1, 256)), const((256, 1024)), const((1, 1024)), const((1, 128)),
                  const((128, 1024)), const((128, 512)), const((1, 1024)),
                  const((1024, 128)), const((128, 1024)),
                  rope_spec, rope_spec, rope_spec],
        out_specs=[row(1024), row(1024), row(512), row(128), row(128)],
        compiler_params=_params(1),
        name="mla_prep",
    )(z, w["qa_g"], w["wuq"], w["qn_g"], w["kva_g"], w["wk"], w["wv"], w["kn_g"], *_head_segments(), *rope_tabs)


def _mla_cache(ckv, kr_pad, w):
    const = lambda shape: pl.BlockSpec(shape, lambda i: (0,) * len(shape))
    row = lambda width: pl.BlockSpec((TM, width), lambda i: (i, 0))
    n = ckv.shape[0]
    return pl.pallas_call(
        _mla_cache_kernel,
        out_shape=[jax.ShapeDtypeStruct((n, 1024), BF16), jax.ShapeDtypeStruct((n, 512), BF16)],
        grid=(n // TM,),
        in_specs=[row(128), row(128), const((128, 1024)), const((128, 512)), const((1, 1024)),
                  const((1024, 128)), const((128, 1024))],
        out_specs=[row(1024), row(512)],
        compiler_params=_params(1),
        name="mla_cache",
    )(ckv, kr_pad, w["wk"], w["wv"], w["kn_g"], *_head_segments())


def _mla_attn_kernel(*refs, has_cache):
    q_ref, k_ref, v_ref = refs[:3]
    pos = 3
    if has_cache:
        kc_ref, vc_ref = refs[3:5]
        pos = 5
    o_ref = refs[pos]
    for p in range(MLA_HEADS // 2):
        acc = jnp.zeros((TM, 128), F32)
        for hh in range(2):
            h = 2 * p + hh
            lanes = slice(128 * h, 128 * h + 128)
            qh = q_ref[:, lanes]
            l1 = _dot_t(qh, k_ref[:, lanes])
            m = jnp.max(l1, axis=-1, keepdims=True)
            if has_cache:
                l0 = _dot_t(qh, kc_ref[:, lanes])
                m = jnp.maximum(m, jnp.max(l0, axis=-1, keepdims=True))
                p0 = jnp.exp(l0 - m)
            p1 = jnp.exp(l1 - m)
            den = jnp.sum(p1, axis=-1, keepdims=True)
            if has_cache:
                den = den + jnp.sum(p0, axis=-1, keepdims=True)
            mask = _lane_half_mask(hh)
            vp = v_ref[:, 128 * p:128 * p + 128]
            o = _dot(p1.astype(BF16), jnp.where(mask, vp, jnp.zeros_like(vp)))
            if has_cache:
                vcp = vc_ref[:, 128 * p:128 * p + 128]
                o += _dot(p0.astype(BF16), jnp.where(mask, vcp, jnp.zeros_like(vcp)))
            acc += o * (1.0 / den)
        o_ref[:, 128 * p:128 * p + 128] = acc.astype(BF16)


def _mla_attn(q, k, v, kc, vc, *, ctx):
    if ctx:
        nb, seq, row_blk, tile0 = N_CTX_SEQ, CTX_LEN, 0, 0
    else:
        nb, seq, row_blk, tile0 = N_LAT_SEQ, LAT_LEN, T_CTX // LAT_LEN, N_CTX_TILES
    nq = seq // TM
    in_specs = [pl.BlockSpec((TM, 1024), lambda b, i: (tile0 + b * nq + i, 0)),
                pl.BlockSpec((seq, 1024), lambda b, i: (row_blk + b, 0)),
                pl.BlockSpec((seq, 512), lambda b, i: (row_blk + b, 0))]
    args = [q, k, v]
    if not ctx:
        in_specs += [pl.BlockSpec((TM, 1024), lambda b, i: (b, 0)), pl.BlockSpec((TM, 512), lambda b, i: (b, 0))]
        args += [kc, vc]
    return pl.pallas_call(
        functools.partial(_mla_attn_kernel, has_cache=not ctx),
        out_shape=jax.ShapeDtypeStruct((nb * seq, 512), BF16),
        grid=(nb, nq), in_specs=in_specs,
        out_specs=pl.BlockSpec((TM, 512), lambda b, i: (b * nq + i, 0)),
        compiler_params=_params(2),
        name="mla_attn_ctx" if ctx else "mla_attn_lat",
    )(*args)


def _merge_kernel(retc_ref, retl_ref, glac_ref, glal_ref, mlac_ref, mlal_ref, m0_ref, m1_ref, m2_ref,
                  xc_ref, xl_ref, wb_ref, wo_ref,
                  g1_ref, n2_ref, sh2_ref, sc2_ref, rwh_ref, rwl_ref, rb_ref,
                  x1_ref, h_ref, idx_ref, w_ref, rank_ref, cnt_ref):
    @pl.when(pl.program_id(0) == 0)
    def _():
        cnt_ref[...] = jnp.zeros_like(cnt_ref)

    is_ctx = pl.program_id(0) < N_CTX_TILES
    mix = None
    for c_ref, l_ref, m_ref, n in ((retc_ref, retl_ref, m0_ref, 0), (glac_ref, glal_ref, m1_ref, 1),
                                   (mlac_ref, mlal_ref, m2_ref, 2)):
        branch = jnp.where(is_ctx, c_ref[...], l_ref[...])
        term = _sigmoid(m_ref[...]) * _dot(branch, wb_ref[n]).astype(BF16)
        mix = term if mix is None else mix + term
    out = _dot(mix, wo_ref[...])
    x1 = jnp.where(is_ctx, xc_ref[...], xl_ref[...]) + g1_ref[0] * out
    x1_ref[...] = x1
    h = x1 * lax.rsqrt(jnp.mean(x1 * x1, axis=-1, keepdims=True) + EPS) * n2_ref[...]
    h = h * (1.0 + sc2_ref[0]) + sh2_ref[0]
    h_ref[...] = h
    hh = h.astype(BF16)
    hl = (h - hh.astype(F32)).astype(BF16)
    logits = _dot(hh, rwh_ref[...]) + _dot(hh, rwl_ref[...]) + _dot(hl, rwh_ref[...]) + rb_ref[...]
    rows = TM // ROUTE_CHUNKS
    lane = lax.broadcasted_iota(jnp.int32, (rows, 128), 1)
    lanef = lane.astype(F32)
    onehots, osums = [], []
    for c in range(ROUTE_CHUNKS):
        l = jnp.where(lane < N_EXPERTS, logits[c * rows:(c + 1) * rows], -jnp.inf)
        vals, idxs = [], []
        for _ in range(TOP_K):
            m = jnp.max(l, axis=-1, keepdims=True)
            ix = jnp.min(jnp.where(l == m, lanef, 128.0), axis=-1, keepdims=True)
            vals.append(m)
            idxs.append(ix)
            l = jnp.where(lanef == ix, -jnp.inf, l)
        es = [jnp.exp(v - vals[0]) for v in vals]
        inv = 1.0 / (es[0] + es[1] + es[2] + es[3])
        idx_out = jnp.zeros((rows, 128), F32)
        w_out = jnp.zeros((rows, 128), F32)
        for kk in range(TOP_K):
            idx_out = jnp.where(lane == kk, idxs[kk], idx_out)
            w_out = jnp.where(lane == kk, es[kk] * inv, w_out)
        idx_ref[c * rows:(c + 1) * rows, :] = idx_out.astype(jnp.int32)
        w_ref[c * rows:(c + 1) * rows, :] = w_out
        oh = [jnp.where(lanef == ix, 1.0, 0.0) for ix in idxs]
        onehots.append(oh)
        osums.append((oh[0] + oh[1]) + (oh[2] + oh[3]))

    osum = jnp.concatenate(osums, axis=0)
    ri = lax.broadcasted_iota(jnp.int32, (TM, TM), 0)
    ci = lax.broadcasted_iota(jnp.int32, (TM, TM), 1)
    before = jnp.where(ri > ci, 1.0, 0.0).astype(BF16)
    prior = _dot(before, osum.astype(BF16)) + cnt_ref[0:1, :]
    for c in range(ROUTE_CHUNKS):
        pc = prior[c * rows:(c + 1) * rows]
        rank_out = jnp.zeros((rows, 128), F32)
        for kk in range(TOP_K):
            rank_out = jnp.where(lane == kk, jnp.sum(onehots[c][kk] * pc, axis=-1, keepdims=True), rank_out)
        rank_ref[c * rows:(c + 1) * rows, :] = rank_out.astype(jnp.int32)
    cnt_ref[...] = cnt_ref[...] + jnp.sum(osum, axis=0, keepdims=True)


def _merge(branches, z, xc, xl, modr, w, layer):
    mrow = _mod_row(TM)
    base = layer * N_MOD

    def mod_spec(part):
        return pl.BlockSpec((1, 1, D_MODEL), lambda i: ((base + mrow(i)) * 6 + part, 0, 0))

    const = lambda shape: pl.BlockSpec(shape, lambda i: (0,) * len(shape))
    row = lambda width: pl.BlockSpec((TM, width), lambda i: (i, 0))
    gate = lambda col: pl.BlockSpec((TM, 1024), lambda i: (i, col))
    return pl.pallas_call(
        _merge_kernel,
        out_shape=[jax.ShapeDtypeStruct((T_ALL, D_MODEL), F32), jax.ShapeDtypeStruct((T_ALL, D_MODEL), F32),
                   jax.ShapeDtypeStruct((T_ALL, 128), jnp.int32), jax.ShapeDtypeStruct((T_ALL, 128), F32),
                   jax.ShapeDtypeStruct((T_ALL, 128), jnp.int32), jax.ShapeDtypeStruct((8, 128), F32)],
        grid=(N_TILES,),
        in_specs=[_ctx_spec(TM, 512), _lat_spec(TM, 512)] * 3 + [gate(3), gate(4), gate(5),
                  _ctx_spec(TM, D_MODEL), _lat_spec(TM, D_MODEL),
                  const((3, 512, 1024)), const((1024, 1024)),
                  mod_spec(2), const((1, 1024)), mod_spec(3), mod_spec(4),
                  const((1024, 128)), const((1024, 128)), const((1, 128))],
        out_specs=[row(1024), row(1024), row(128), row(128), row(128), const((8, 128))],
        compiler_params=_params(1),
        name="merge",
    )(*branches, z, z, z, xc, xl, w["wb"], w["wo"], modr, w["n2_g"], modr, modr,
      w["rw_hi"], w["rw_lo"], w["rb"])


def _route(top_idx, rank, counts):
    flat_e = top_idx.reshape(N_SLOTS)
    onehot = (flat_e[:, None] == jnp.arange(N_EXPERTS, dtype=jnp.int32)[None, :]).astype(jnp.int32)
    padded = (counts + MOE_ROWS - 1) // MOE_ROWS * MOE_ROWS
    pad_end = jnp.cumsum(padded)
    pad_start = pad_end - padded
    dest = (rank.reshape(N_SLOTS) + jnp.sum(onehot * pad_start[None, :], axis=1)).astype(jnp.int32)
    blk_start = jnp.arange(N_MOE_BLOCKS, dtype=jnp.int32) * MOE_ROWS
    block_e = jnp.minimum(jnp.sum((pad_end[None, :] <= blk_start[:, None]).astype(jnp.int32), axis=1),
                          N_EXPERTS - 1).astype(jnp.int32)
    n_used = (pad_end[-1] // MOE_ROWS).astype(jnp.int32)
    nxt_blk = pad_end[block_e] // MOE_ROWS
    next_e = jnp.where(nxt_blk < n_used, block_e[jnp.minimum(nxt_blk, N_MOE_BLOCKS - 1)], block_e).astype(jnp.int32)
    tail_start = (pad_start + counts).astype(jnp.int32)
    return dest, tail_start, block_e, n_used.reshape(1), next_e


def _dispatch_kernel(dest_ref, tail_ref, nb_ref, h_ref, xs_ref, zero_buf, stage, sem, ssem):
    i = pl.program_id(0)

    @pl.when(i == 0)
    def _():
        zero_buf[...] = jnp.zeros_like(zero_buf)
        fills = [pltpu.make_async_copy(
            zero_buf, xs_ref.at[pl.ds(pl.multiple_of((tail_ref[e] // 8) * 8, 8), TAIL_FILL), :], sem)
            for e in range(N_EXPERTS)]
        for f in fills:
            f.start()
        for f in fills:
            f.wait()

        def fill_block(b, c):
            f = pltpu.make_async_copy(zero_buf.at[pl.ds(0, MOE_ROWS), :],
                                      xs_ref.at[pl.ds(pl.multiple_of(b * MOE_ROWS, MOE_ROWS), MOE_ROWS), :], sem)
            f.start()
            f.wait()
            return c
        lax.fori_loop(nb_ref[0], (N_MOE_ROWS + XS_EXTRA) // MOE_ROWS, fill_block, 0)

    base = i * TM * TOP_K
    cur = i % 2

    def wait_tile(slot):
        for kk in range(TOP_K):
            pltpu.make_async_copy(stage.at[slot], xs_ref.at[pl.ds(0, TM), :], ssem.at[slot]).wait()

    @pl.when(i >= 2)
    def _():
        wait_tile(cur)
    stage[cur] = h_ref[...]

    def issue(t, c):
        for kk in range(TOP_K):
            pltpu.make_async_copy(stage.at[cur, pl.ds(t, 1), :],
                                  xs_ref.at[pl.ds(dest_ref[base + t * TOP_K + kk], 1), :], ssem.at[cur]).start()
        return c
    lax.fori_loop(0, TM, issue, 0, unroll=8)

    @pl.when(i == N_TILES - 1)
    def _():
        wait_tile(1 - cur)
        wait_tile(cur)


def _moe_dispatch(h, dest, tail_start, n_used):
    return pl.pallas_call(
        _dispatch_kernel,
        out_shape=jax.ShapeDtypeStruct((N_MOE_ROWS + XS_EXTRA, D_MODEL), F32),
        grid_spec=pltpu.PrefetchScalarGridSpec(
            num_scalar_prefetch=3, grid=(N_TILES,),
            in_specs=[pl.BlockSpec((TM, D_MODEL), lambda i, d, t, nb: (i, 0))],
            out_specs=pl.BlockSpec(memory_space=pl.ANY),
            scratch_shapes=[pltpu.VMEM((TAIL_FILL, D_MODEL), F32), pltpu.VMEM((2, TM, D_MODEL), F32),
                            pltpu.SemaphoreType.DMA, pltpu.SemaphoreType.DMA((2,))]),
        compiler_params=_params(1),
        name="moe_dispatch",
    )(dest, tail_start, n_used, h)


def _expert_kernel(be_ref, nb_ref, nxt_ref, x_ref, wgu_hbm, bgu_ref, wd_hbm, bd_ref, o_ref,
                   wgu_st, wd_st, wgu_bf, wd_bf, sem, *, layer):
    i = pl.program_id(0)
    e = be_ref[i]
    prev = be_ref[jnp.maximum(i - 1, 0)]

    def weight_copies(expert):
        idx = layer * N_EXPERTS + expert
        return (pltpu.make_async_copy(wgu_hbm.at[idx], wgu_st, sem.at[0]),
                pltpu.make_async_copy(wd_hbm.at[idx], wd_st, sem.at[1]))

    @pl.when(i == 0)
    def _():
        for cp in weight_copies(e):
            cp.start()

    @pl.when(((i == 0) | (e != prev)) & (i < nb_ref[0]))
    def _():
        for cp in weight_copies(e):
            cp.wait()
        wgu_bf[...] = wgu_st[...].astype(BF16)
        wd_bf[...] = wd_st[...].astype(BF16)
        nxt = nxt_ref[i]

        @pl.when(nxt != e)
        def _():
            for cp in weight_copies(nxt):
                cp.start()

    @pl.when(i < nb_ref[0])
    def _():
        gu = _dot(x_ref[...].astype(BF16), wgu_bf[...]) + bgu_ref[0]
        gate = jnp.minimum(gu[:, :D_EXPERT], SWIGLU_LIMIT)
        up = jnp.clip(gu[:, D_EXPERT:], -SWIGLU_LIMIT, SWIGLU_LIMIT)
        act = (up + 1.0) * gate * _sigmoid(SWIGLU_ALPHA * gate)
        out = _dot(act.astype(BF16), wd_bf[...]) + bd_ref[0]
        o_ref[...] = out

    @pl.when(i >= nb_ref[0])
    def _():
        o_ref[...] = jnp.zeros_like(o_ref)


def _moe_experts(xs, block_e, n_used, next_e, w_gu, b_gu, w_down, b_down, layer):
    w_idx = lambda i, be, nb, nx: (layer * N_EXPERTS + be[i], 0, 0)
    return pl.pallas_call(
        functools.partial(_expert_kernel, layer=layer),
        out_shape=jax.ShapeDtypeStruct((N_MOE_ROWS, D_MODEL), F32),
        grid_spec=pltpu.PrefetchScalarGridSpec(
            num_scalar_prefetch=3, grid=(N_MOE_BLOCKS,),
            in_specs=[pl.BlockSpec((MOE_ROWS, D_MODEL), lambda i, be, nb, nx: (jnp.minimum(i, nb[0] - 1), 0)),
                      pl.BlockSpec(memory_space=pl.ANY),
                      pl.BlockSpec((1, 1, 2 * D_EXPERT), w_idx),
                      pl.BlockSpec(memory_space=pl.ANY),
                      pl.BlockSpec((1, 1, D_MODEL), w_idx)],
            out_specs=pl.BlockSpec((MOE_ROWS, D_MODEL), lambda i, be, nb, nx: (i, 0)),
            scratch_shapes=[pltpu.VMEM((D_MODEL, 2 * D_EXPERT), F32), pltpu.VMEM((D_EXPERT, D_MODEL), F32),
                            pltpu.VMEM((D_MODEL, 2 * D_EXPERT), BF16), pltpu.VMEM((D_EXPERT, D_MODEL), BF16),
                            pltpu.SemaphoreType.DMA((2,))]),
        compiler_params=_params(1),
        name="moe_experts",
    )(block_e, n_used, next_e, xs, w_gu, b_gu, w_down, b_down)


def _combine_kernel(dest_ref, x_ref, g2_ref, w_ref, eo_ref, yc_ref, yl_ref, buf, sem):
    i = pl.program_id(0)
    cur = i % 2

    def issue_tile(tile, slot):
        base = tile * TM * TOP_K

        def issue(t, c):
            for kk in range(TOP_K):
                pltpu.make_async_copy(eo_ref.at[pl.ds(dest_ref[base + t * TOP_K + kk], 1), :],
                                      buf.at[slot, kk, pl.ds(t, 1), :], sem.at[slot]).start()
            return c
        lax.fori_loop(0, TM, issue, 0, unroll=8)

    @pl.when(i == 0)
    def _():
        issue_tile(0, 0)

    @pl.when(i + 1 < N_TILES)
    def _():
        issue_tile(i + 1, 1 - cur)

    for kk in range(TOP_K):
        pltpu.make_async_copy(eo_ref.at[pl.ds(0, TM), :], buf.at[cur, kk], sem.at[cur]).wait()
    w = w_ref[...]
    ff = None
    for kk in range(TOP_K):
        term = buf[cur, kk] * w[:, kk:kk + 1]
        ff = term if ff is None else ff + term
    y = x_ref[...] + g2_ref[0] * ff
    is_ctx = i < N_CTX_TILES

    @pl.when(is_ctx)
    def _():
        yc_ref[...] = y

    @pl.when(jnp.logical_not(is_ctx))
    def _():
        yl_ref[...] = y


def _moe_combine(dest, x1, modr, top_w, eo, layer):
    mrow = _mod_row(TM)
    base = layer * N_MOD
    return pl.pallas_call(
        _combine_kernel,
        out_shape=[jax.ShapeDtypeStruct((T_CTX, D_MODEL), F32), jax.ShapeDtypeStruct((T_LAT, D_MODEL), F32)],
        grid_spec=pltpu.PrefetchScalarGridSpec(
            num_scalar_prefetch=1, grid=(N_TILES,),
            in_specs=[pl.BlockSpec((TM, D_MODEL), lambda i, d: (i, 0)),
                      pl.BlockSpec((1, 1, D_MODEL), lambda i, d: ((base + mrow(i)) * 6 + 5, 0, 0)),
                      pl.BlockSpec((TM, 128), lambda i, d: (i, 0)),
                      pl.BlockSpec(memory_space=pl.ANY)],
            out_specs=[pl.BlockSpec((TM, D_MODEL), lambda i, d: (jnp.minimum(i, N_CTX_TILES - 1), 0)),
                       pl.BlockSpec((TM, D_MODEL), lambda i, d: (jnp.maximum(i - N_CTX_TILES, 0), 0))],
            scratch_shapes=[pltpu.VMEM((2, TOP_K, TM, D_MODEL), F32), pltpu.SemaphoreType.DMA((2,))]),
        compiler_params=_params(1),
        name="moe_combine",
    )(dest, x1, modr, top_w, eo)


def _pad_heads(w, n_heads, width):
    lead = w.shape[:-1]
    w = w.reshape(lead + (n_heads, width))
    w = jnp.pad(w, [(0, 0)] * len(lead) + [(0, 0), (0, HEAD_PAD - width)])
    return w.reshape(lead + (n_heads * HEAD_PAD,))


def _layer_weights(l, w_in, gla_wa2, gla_ba, mla_qa_g, mla_wuq, mla_kva_g, mla_wukv, mla_qn_g, mla_kn_g,
                   w_branch, w_out, router_w, router_b, norm2_g):
    wi = w_in[l]
    w_in_p = jnp.concatenate([wi[:, :3072], wi[:, 3520:], wi[:, 3104:3520], wi[:, 3072:3104],
                              jnp.zeros((D_MODEL, 64), F32)], axis=1).astype(BF16)
    wa_p = jnp.zeros((512, 512), F32)
    wa_p = wa_p.at[416:432, 0:256].set(gla_wa2[l, 0]).at[432:448, 256:512].set(gla_wa2[l, 1]).astype(BF16)
    ba_p = gla_ba[l].reshape(1, 512)
    wukv = mla_wukv[l].reshape(128, MLA_HEADS, MLA_NOPE + MLA_V)
    rw = jnp.pad(router_w[l], ((0, 0), (0, 128 - N_EXPERTS)))
    rw_hi = rw.astype(BF16)
    return {
        "w_in": w_in_p, "wa": wa_p, "ba": ba_p,
        "qa_g": mla_qa_g[l].reshape(1, 256),
        "wuq": _pad_heads(mla_wuq[l], MLA_HEADS, MLA_QK).astype(BF16),
        "qn_g": jnp.tile(jnp.pad(mla_qn_g[l], (0, HEAD_PAD - MLA_QK)), MLA_HEADS).reshape(1, 1024) * MLA_QK ** -0.5,
        "kn_g": jnp.tile(jnp.pad(mla_kn_g[l], (0, HEAD_PAD - MLA_QK)), MLA_HEADS).reshape(1, 1024),
        "kva_g": mla_kva_g[l].reshape(1, 128),
        "wk": _pad_heads(wukv[:, :, :MLA_NOPE].reshape(128, MLA_HEADS * MLA_NOPE), MLA_HEADS, MLA_NOPE).astype(BF16),
        "wv": wukv[:, :, MLA_NOPE:].reshape(128, MLA_HEADS * MLA_V).astype(BF16),
        "wb": w_branch[l].astype(BF16), "wo": w_out[l].astype(BF16),
        "rw_hi": rw_hi, "rw_lo": (rw - rw_hi.astype(F32)).astype(BF16),
        "rb": jnp.pad(router_b[l], (0, 128 - N_EXPERTS)).reshape(1, 128),
        "n2_g": norm2_g[l].reshape(1, D_MODEL),
    }


def kernel(x_prompt, x_sample, cache_mla_ckv, cache_mla_krope, state_ret, state_gla, c, c_ctx, w_mod, b_mod, norm1_g, norm2_g, w_in, ret_gn_g, gla_wa2, gla_ba, gla_norm_g, mla_qa_g, mla_wuq, mla_kva_g, mla_wukv, mla_qn_g, mla_kn_g, w_branch, w_out, router_w, router_b, moe_w_gu, moe_b_gu, moe_w_down, moe_b_down):
    xc, xl = x_prompt.reshape(T_CTX, D_MODEL), x_sample.reshape(T_LAT, D_MODEL)
    cc = jnp.concatenate([c_ctx[None, :], c, jnp.zeros((N_MOD - 1 - N_LAT_SEQ, D_MODEL), F32)], axis=0)
    modr = _modulation(cc, w_mod, b_mod).reshape(DEPTH * N_MOD * 6, 1, D_MODEL)
    rope_tabs = _rope_tables()
    w_gu = moe_w_gu.reshape(DEPTH * N_EXPERTS, D_MODEL, 2 * D_EXPERT)
    b_gu = moe_b_gu.reshape(DEPTH * N_EXPERTS, 1, 2 * D_EXPERT)
    w_dn = moe_w_down.reshape(DEPTH * N_EXPERTS, D_EXPERT, D_MODEL)
    b_dn = moe_b_down.reshape(DEPTH * N_EXPERTS, 1, D_MODEL)

    ckv_l, krope_l, ret_l, gla_l = [], [], [], []
    for l in range(DEPTH):
        w = _layer_weights(l, w_in, gla_wa2, gla_ba, mla_qa_g, mla_wuq, mla_kva_g, mla_wukv, mla_qn_g, mla_kn_g,
                           w_branch, w_out, router_w, router_b, norm2_g)
        z = _in_proj(xc, xl, norm1_g[l].reshape(1, D_MODEL), modr, w["w_in"], l)

        gn = ret_gn_g[l].reshape(1, 512)
        ret_c, ret_state = _retention(z, gn, None, ctx=True)
        (ret_s,) = _retention(z, gn, state_ret[:, l], ctx=False)
        gng = gla_norm_g[l].reshape(1, 512)
        gla_c, gla_state = _gla(z, w["wa"], w["ba"], gng, None, ctx=True)
        gla_s, _ = _gla(z, w["wa"], w["ba"], gng, state_gla[:, l], ctx=False)

        q, k, v, ckv, kr = _mla_prep(z, w, rope_tabs)
        kc, vc = _mla_cache(cache_mla_ckv[:, l].reshape(N_LAT_SEQ * CTX_LEN, 128),
                            jnp.pad(cache_mla_krope[:, l].reshape(N_LAT_SEQ * CTX_LEN, MLA_ROPE),
                                    ((0, 0), (0, 128 - MLA_ROPE))), w)
        mla_c = _mla_attn(q, k, v, None, None, ctx=True)
        mla_s = _mla_attn(q, k, v, kc, vc, ctx=False)

        x1, h2, top_idx, top_w, rank, cnt = _merge((ret_c, ret_s, gla_c, gla_s, mla_c, mla_s), z, xc, xl, modr, w, l)
        dest, tail_start, block_e, n_used, next_e = _route(top_idx[:, :TOP_K], rank[:, :TOP_K],
                                                           cnt[0, :N_EXPERTS].astype(jnp.int32))
        xs = _moe_dispatch(h2, dest, tail_start, n_used)
        eo = _moe_experts(xs, block_e, n_used, next_e, w_gu, b_gu, w_dn, b_dn, l)
        xc, xl = _moe_combine(dest, x1, modr, top_w, eo, l)

        ckv_l.append(ckv[:T_CTX].reshape(N_CTX_SEQ, CTX_LEN, 128))
        krope_l.append(kr[:T_CTX, :MLA_ROPE].reshape(N_CTX_SEQ, CTX_LEN, MLA_ROPE))
        ret_l.append(ret_state)
        gla_l.append(gla_state.reshape(N_CTX_SEQ, 2, N_HEADS, DK, DV))

    y_p = xc.reshape(N_CTX_SEQ, CTX_LEN, D_MODEL)
    y_s = xl.reshape(N_LAT_SEQ, LAT_LEN, D_MODEL)
    return (y_p, y_s, jnp.stack(ckv_l, axis=1), jnp.stack(krope_l, axis=1),
            jnp.stack(ret_l, axis=1), jnp.stack(gla_l, axis=1))
```

```python
import functools

import jax
import jax.numpy as jnp
import numpy as np
from jax import lax
from jax.experimental import pallas as pl
from jax.experimental.pallas import tpu as pltpu

F32 = jnp.float32
BF16 = jnp.bfloat16

D_MODEL = 1024
DEPTH = 2
N_CTX_SEQ, CTX_LEN = 32, 256
N_LAT_SEQ, LAT_LEN = 4, 1024
T_CTX = N_CTX_SEQ * CTX_LEN
T_LAT = N_LAT_SEQ * LAT_LEN
T_ALL = T_CTX + T_LAT
TM = 256
N_TILES = T_ALL // TM
N_CTX_TILES = T_CTX // TM
LAT_TILES = LAT_LEN // TM
N_MOD = 8
EPS = 1e-6

N_HEADS = 4
DK, DV = 64, 128
GRID_W = 64
MLA_HEADS, MLA_NOPE, MLA_ROPE, MLA_V = 8, 64, 32, 64
MLA_QK = MLA_NOPE + MLA_ROPE
HEAD_PAD = 128
GLA_TAU = 16.0
N_EXPERTS, TOP_K, D_EXPERT = 32, 4, 1024
SWIGLU_LIMIT, SWIGLU_ALPHA = 7.0, 1.702
MOE_ROWS = 512
N_SLOTS = T_ALL * TOP_K
N_MOE_BLOCKS = N_SLOTS // MOE_ROWS + N_EXPERTS
N_MOE_ROWS = N_MOE_BLOCKS * MOE_ROWS
ROUTE_CHUNKS = 4
TAIL_FILL = MOE_ROWS + 8
XS_EXTRA = 2 * MOE_ROWS

DZ = 6656
IN_TILE = 512
VMEM_LIMIT = 56 * 1024 * 1024

RET_LOG_F = [float(np.log1p(-np.exp2(-(5.0 + h)))) for h in range(N_HEADS)]
RET_LOG_B = [float(np.log1p(-np.exp2(-(5.5 + h)))) for h in range(N_HEADS)]


def _params(n_axes, vmem=VMEM_LIMIT):
    return pltpu.CompilerParams(dimension_semantics=("arbitrary",) * n_axes, vmem_limit_bytes=vmem)


def _sigmoid(x):
    return 1.0 / (1.0 + jnp.exp(-x))


def _dot(a, b):
    return jnp.dot(a, b, preferred_element_type=F32)


def _dot_t(a, b):
    return lax.dot_general(a, b, (((1,), (1,)), ((), ())), preferred_element_type=F32)


def _mod_row(tile_rows):
    def f(i):
        r0 = i * tile_rows
        return jnp.where(r0 < T_CTX, 0, 1 + (r0 - T_CTX) // LAT_LEN)
    return f


def _ctx_spec(rows, width):
    n_ctx = T_CTX // rows
    return pl.BlockSpec((rows, width), lambda i: (jnp.minimum(i, n_ctx - 1), 0))


def _lat_spec(rows, width):
    n_ctx = T_CTX // rows
    return pl.BlockSpec((rows, width), lambda i: (jnp.maximum(i - n_ctx, 0), 0))


def _mod_kernel(c_ref, w_ref, b_ref, o_ref):
    c = c_ref[...]
    s = c * _sigmoid(c)
    o_ref[0] = jnp.dot(s, w_ref[0], preferred_element_type=F32, precision=lax.Precision.HIGHEST) + b_ref[0]


def _modulation(cc, w_mod, b_mod):
    n = 6 * D_MODEL
    blk = 2048
    return pl.pallas_call(
        _mod_kernel,
        out_shape=jax.ShapeDtypeStruct((DEPTH, N_MOD, n), F32),
        grid=(DEPTH, n // blk),
        in_specs=[pl.BlockSpec((N_MOD, D_MODEL), lambda l, j: (0, 0)),
                  pl.BlockSpec((1, D_MODEL, blk), lambda l, j: (l, 0, j)),
                  pl.BlockSpec((1, 1, blk), lambda l, j: (l, 0, j))],
        out_specs=pl.BlockSpec((1, N_MOD, blk), lambda l, j: (l, 0, j)),
        compiler_params=_params(2),
        name="modulation",
    )(cc, w_mod, b_mod.reshape(DEPTH, 1, n))


def _in_kernel(xc_ref, xl_ref, g_ref, sh_ref, sc_ref, w_ref, o_ref):
    x = jnp.where(pl.program_id(0) < T_CTX // IN_TILE, xc_ref[...], xl_ref[...])
    h = x * lax.rsqrt(jnp.mean(x * x, axis=-1, keepdims=True) + EPS) * g_ref[...]
    h = h * (1.0 + sc_ref[0]) + sh_ref[0]
    hb = h.astype(BF16)
    for n0 in range(0, DZ, 512):
        o_ref[:, n0:n0 + 512] = _dot(hb, w_ref[:, n0:n0 + 512]).astype(BF16)


def _in_proj(xc, xl, g, modr, w_in_p, layer):
    mrow = _mod_row(IN_TILE)
    base = layer * N_MOD

    def mod_spec(part):
        return pl.BlockSpec((1, 1, D_MODEL), lambda i: ((base + mrow(i)) * 6 + part, 0, 0))

    return pl.pallas_call(
        _in_kernel,
        out_shape=jax.ShapeDtypeStruct((T_ALL, DZ), BF16),
        grid=(T_ALL // IN_TILE,),
        in_specs=[_ctx_spec(IN_TILE, D_MODEL), _lat_spec(IN_TILE, D_MODEL),
                  pl.BlockSpec((1, D_MODEL), lambda i: (0, 0)),
                  mod_spec(0), mod_spec(1),
                  pl.BlockSpec((D_MODEL, DZ), lambda i: (0, 0))],
        out_specs=pl.BlockSpec((IN_TILE, DZ), lambda i: (i, 0)),
        compiler_params=_params(1),
        name="in_proj",
    )(xc, xl, g, modr, modr, w_in_p)


def _lane_half_mask(hh):
    lane = lax.broadcasted_iota(jnp.int32, (1, 128), 1)
    return (lane < 64) if hh == 0 else (lane >= 64)


@functools.lru_cache(maxsize=None)
def _ret_decay_table(seq):
    d = np.arange(seq)[:, None] - np.arange(seq)[None, :]
    tab = np.stack([np.exp(np.where(d > 0, RET_LOG_F[h] * d, -RET_LOG_B[h] * d)) for h in range(N_HEADS)])
    return (tab * np.where(d == 0, 2.0, 1.0) * DK ** -0.5).astype(np.float32)


def _ret_kernel(*refs, seq, has_state, emit_state):
    q_ref, k_ref, v_ref, g_ref, gn_ref, dec_ref = refs[:6]
    pos = 6
    if has_state:
        s0_ref = refs[pos]
        pos += 1
    o_ref = refs[pos]
    pos += 1
    if emit_state:
        st_ref = refs[pos]

    r0 = pl.multiple_of(pl.program_id(1) * TM, TM)
    qb = q_ref[pl.ds(r0, TM), :]
    ri = (lax.broadcasted_iota(jnp.int32, (TM, 1), 0) + r0).astype(F32)

    for h in range(N_HEADS):
        p, hh = h // 2, h % 2
        lanes = slice(128 * p, 128 * p + 128)
        qp = qb[:, lanes]
        qh = jnp.where(_lane_half_mask(hh), qp, jnp.zeros_like(qp))
        sc = _dot_t(qh, k_ref[:, lanes])
        o = _dot((sc * dec_ref[h]).astype(BF16), v_ref[:, 128 * h:128 * h + 128])
        if has_state:
            qf = qh.astype(F32)
            o += _dot((qf * jnp.exp(RET_LOG_F[h] * (ri + 1.0))).astype(BF16), s0_ref[0, 0, p].astype(BF16))
            o += _dot((qf * jnp.exp(RET_LOG_B[h] * (seq - ri))).astype(BF16), s0_ref[0, 1, p].astype(BF16))
        mu = jnp.mean(o, axis=-1, keepdims=True)
        d = o - mu
        var = jnp.mean(d * d, axis=-1, keepdims=True)
        on = d * lax.rsqrt(var + EPS)
        g = g_ref[:, 128 * h:128 * h + 128].astype(F32)
        out = on * gn_ref[:, 128 * h:128 * h + 128] * (g * _sigmoid(g))
        o_ref[:, 128 * h:128 * h + 128] = out.astype(BF16)

    if emit_state:
        jc = lax.broadcasted_iota(jnp.int32, (seq, 1), 0).astype(F32)
        lane = lax.broadcasted_iota(jnp.int32, (1, 128), 1)
        for p in range(2):
            kp = k_ref[:, 128 * p:128 * p + 128].astype(F32) * DK ** -0.5
            lgf = jnp.where(lane < 64, RET_LOG_F[2 * p], RET_LOG_F[2 * p + 1])
            lgb = jnp.where(lane < 64, RET_LOG_B[2 * p], RET_LOG_B[2 * p + 1])
            kdf = (kp * jnp.exp(lgf * (seq - 1.0 - jc))).T.astype(BF16)
            kdb = (kp * jnp.exp(lgb * jc)).T.astype(BF16)
            for hh in range(2):
                h = 2 * p + hh
                vh = v_ref[:, 128 * h:128 * h + 128]
                st_ref[0, 0, h] = _dot(kdf, vh)[64 * hh:64 * hh + 64, :]
                st_ref[0, 1, h] = _dot(kdb, vh)[64 * hh:64 * hh + 64, :]


def _retention(z, gn_g, s0, *, ctx):
    if ctx:
        nb, seq, row_blk, tile0 = N_CTX_SEQ, CTX_LEN, 0, 0
    else:
        nb, seq, row_blk, tile0 = N_LAT_SEQ, LAT_LEN, T_CTX // LAT_LEN, N_CTX_TILES
    nq = seq // TM
    in_specs = [pl.BlockSpec((seq, 256), lambda b, i: (row_blk + b, 0)),
                pl.BlockSpec((seq, 256), lambda b, i: (row_blk + b, 1)),
                pl.BlockSpec((seq, 512), lambda b, i: (row_blk + b, 1)),
                pl.BlockSpec((TM, 512), lambda b, i: (tile0 + b * nq + i, 2)),
                pl.BlockSpec((1, 512), lambda b, i: (0, 0)),
                pl.BlockSpec((N_HEADS, TM, seq), lambda b, i: (0, i, 0))]
    args = [z, z, z, z, gn_g, _ret_decay_table(seq)]
    out_shape = [jax.ShapeDtypeStruct((nb * seq, 512), BF16)]
    out_specs = [pl.BlockSpec((TM, 512), lambda b, i: (b * nq + i, 0))]
    if not ctx:
        in_specs.append(pl.BlockSpec((1, 2, 2, 128, 128), lambda b, i: (b, 0, 0, 0, 0)))
        args.append(s0.reshape(N_LAT_SEQ, 2, 2, 128, 128))
    else:
        out_shape.append(jax.ShapeDtypeStruct((nb, 2, N_HEADS, DK, DV), F32))
        out_specs.append(pl.BlockSpec((1, 2, N_HEADS, DK, DV), lambda b, i: (b, 0, 0, 0, 0)))
    return pl.pallas_call(
        functools.partial(_ret_kernel, seq=seq, has_state=not ctx, emit_state=ctx),
        out_shape=out_shape, grid=(nb, nq), in_specs=in_specs, out_specs=out_specs,
        compiler_params=_params(2),
        name="retention_ctx" if ctx else "retention_lat",
    )(*args)


def _gla_decay(small_ref, wa_ref, ba_ref):
    x = _dot(small_ref[...], wa_ref[...]) + ba_ref[...]
    la = -(jnp.maximum(-x, 0.0) + jnp.log(1.0 + jnp.exp(-jnp.abs(x)))) * (1.0 / GLA_TAU)
    ri = lax.broadcasted_iota(jnp.int32, (TM, TM), 0)
    ci = lax.broadcasted_iota(jnp.int32, (TM, TM), 1)
    ltri = jnp.where(ri >= ci, 1.0, 0.0).astype(BF16)
    hi = la.astype(BF16)
    r1 = la - hi.astype(F32)
    mid = r1.astype(BF16)
    lo = (r1 - mid.astype(F32)).astype(BF16)
    cum = _dot(ltri, hi) + _dot(ltri, mid) + _dot(ltri, lo)
    return la, cum


def _gla_state_kernel(k_ref, v_ref, small_ref, wa_ref, ba_ref, kv_ref, tot_ref):
    la, cum = _gla_decay(small_ref, wa_ref, ba_ref)
    bf, bb = cum[:, :256], cum[:, 256:]
    xb = bb - la[:, 256:]
    k = k_ref[...].astype(F32)
    kdf = k * jnp.exp(bf[TM - 1:TM, :] - bf)
    kdb = k * jnp.exp(xb)
    for p in range(2):
        kf_t = kdf[:, 128 * p:128 * p + 128].T.astype(BF16)
        kb_t = kdb[:, 128 * p:128 * p + 128].T.astype(BF16)
        for hh in range(2):
            h = 2 * p + hh
            vh = v_ref[:, 128 * h:128 * h + 128]
            kv_ref[0, 0, 0, h] = _dot(kf_t, vh)[64 * hh:64 * hh + 64, :]
            kv_ref[0, 0, 1, h] = _dot(kb_t, vh)[64 * hh:64 * hh + 64, :]
    tot_ref[0, 0] = jnp.sum(la.T, axis=-1, keepdims=True)


def _mid_bcast(x, s, r):
    w = 2 * s
    if w >= 8:
        n = TM // w
        x3 = x.reshape(n, w, 256)
        return jnp.broadcast_to(x3[:, r:r + 1, :], (n, w, 256)).reshape(TM, 256)
    x3 = x.reshape(TM // 8, 8, 256)
    sub = lax.broadcasted_iota(jnp.int32, (1, 8, 1), 1)
    out = None
    for blk in range(8 // w):
        rowv = jnp.broadcast_to(x3[:, blk * w + r:blk * w + r + 1, :], (TM // 8, 8, 256))
        out = rowv if out is None else jnp.where(sub >= blk * w, rowv, out)
    return out.reshape(TM, 256)


def _gla_kernel(*refs, n_blk, has_state, emit_state):
    q_ref, k_ref, v_ref, g_ref, small_ref, wa_ref, ba_ref, gn_ref = refs[:8]
    pos = 8
    if has_state:
        kv_ref, tot_ref, s0_ref = refs[pos:pos + 3]
        pos += 3
    o_ref = refs[pos]
    pos += 1
    if emit_state:
        kvo_ref = refs[pos]

    la, cum = _gla_decay(small_ref, wa_ref, ba_ref)
    bf, bb = cum[:, :256], cum[:, 256:]
    xb = bb - la[:, 256:]
    q = q_ref[...].astype(F32) * DK ** -0.5
    k = k_ref[...].astype(F32)
    row = lax.broadcasted_iota(jnp.int32, (TM, 1), 0)
    rowi = lax.broadcasted_iota(jnp.int32, (TM, TM), 0)
    colj = lax.broadcasted_iota(jnp.int32, (TM, TM), 1)
    low_half = _lane_half_mask(0)

    def join(fwd, bwd):
        ops = []
        for p in range(N_HEADS // 2):
            f = fwd[:, 128 * p:128 * p + 128]
            br = pltpu.roll(bwd[:, 128 * p:128 * p + 128], 64, 1)
            ops.append(jnp.where(low_half, f, br).astype(BF16))
            ops.append(jnp.where(low_half, br, f).astype(BF16))
        return ops

    acc = []
    for qo, ko in zip(join(q, q), join(k, k)):
        acc.append(jnp.where(rowi == colj, _dot_t(qo, ko), 0.0))

    s = 1
    while s < TM:
        upper = ((row // s) % 2) == 1
        mf = _mid_bcast(bf, s, s - 1)
        mb = _mid_bcast(xb, s, s)
        ef = jnp.exp(-jnp.abs(bf - mf))
        eb = jnp.exp(-jnp.abs(xb - mb))
        qs = join(jnp.where(upper, q * ef, 0.0), jnp.where(upper, 0.0, q * eb))
        ks = join(jnp.where(upper, 0.0, k * ef), jnp.where(upper, k * eb, 0.0))
        same = (rowi // (2 * s)) == (colj // (2 * s))
        for h in range(N_HEADS):
            sl = _dot_t(qs[h], ks[h])
            acc[h] = acc[h] + (jnp.where(same, sl, 0.0) if 2 * s < TM else sl)
        s *= 2

    if has_state:
        n = pl.program_id(1)
        q_state = join(q * jnp.exp(bf), q * jnp.exp(bb[TM - 1:TM, :] - xb))

    if emit_state:
        kdf = k * jnp.exp(bf[TM - 1:TM, :] - bf)
        kdb = k * jnp.exp(xb)
        for p in range(N_HEADS // 2):
            kf_t = kdf[:, 128 * p:128 * p + 128].T.astype(BF16)
            kb_t = kdb[:, 128 * p:128 * p + 128].T.astype(BF16)
            for hh in range(2):
                h = 2 * p + hh
                vh = v_ref[:, 128 * h:128 * h + 128]
                kvo_ref[0, 0, 0, h] = _dot(kf_t, vh)[64 * hh:64 * hh + 64, :]
                kvo_ref[0, 0, 1, h] = _dot(kb_t, vh)[64 * hh:64 * hh + 64, :]

    for h in range(N_HEADS):
        o = _dot(acc[h].astype(BF16), v_ref[:, 128 * h:128 * h + 128])
        if has_state:
            sf = s0_ref[0, 0, h]
            for m in range(n_blk - 1):
                dec = jnp.exp(tot_ref[0, m, 64 * h:64 * h + 64, :])
                sf = jnp.where(m < n, dec * sf + kv_ref[0, m, 0, h], sf)
            sb = s0_ref[0, 1, h]
            for m in range(n_blk - 1, 0, -1):
                dec = jnp.exp(tot_ref[0, m, 256 + 64 * h:256 + 64 * h + 64, :])
                sb = jnp.where(m > n, dec * sb + kv_ref[0, m, 1, h], sb)
            state = jnp.concatenate([sf, sb] if h % 2 == 0 else [sb, sf], axis=0).astype(BF16)
            o += _dot(q_state[h], state)
        on = o * lax.rsqrt(jnp.mean(o * o, axis=-1, keepdims=True) + EPS)
        g = g_ref[:, 128 * h:128 * h + 128].astype(F32)
        out = on * gn_ref[:, 128 * h:128 * h + 128] * (g * _sigmoid(g))
        o_ref[:, 128 * h:128 * h + 128] = out.astype(BF16)


def _gla(z, wa_p, ba_p, gn_g, s0, *, ctx):
    if ctx:
        nb, n_blk, tile0 = N_CTX_SEQ, 1, 0
    else:
        nb, n_blk, tile0 = N_LAT_SEQ, LAT_TILES, N_CTX_TILES

    def zspec(width, col):
        return pl.BlockSpec((TM, width), lambda b, n: (tile0 + b * n_blk + n, col))

    w_specs = [pl.BlockSpec((512, 512), lambda b, n: (0, 0)), pl.BlockSpec((1, 512), lambda b, n: (0, 0))]
    kv_shape = jax.ShapeDtypeStruct((nb, n_blk, 2, N_HEADS, DK, DV), F32)
    kv_spec = pl.BlockSpec((1, 1, 2, N_HEADS, DK, DV), lambda b, n: (b, n, 0, 0, 0, 0))
    in_specs = [zspec(256, 6), zspec(256, 7), zspec(512, 4), zspec(512, 5), zspec(512, 12)] + w_specs
    in_specs.append(pl.BlockSpec((1, 512), lambda b, n: (0, 0)))
    args = [z, z, z, z, z, wa_p, ba_p, gn_g]
    out_shape = [jax.ShapeDtypeStruct((nb * n_blk * TM, 512), BF16)]
    out_specs = [pl.BlockSpec((TM, 512), lambda b, n: (b * n_blk + n, 0))]
    if ctx:
        out_shape.append(kv_shape)
        out_specs.append(kv_spec)
    else:
        kv, tot = pl.pallas_call(
            _gla_state_kernel,
            out_shape=[kv_shape, jax.ShapeDtypeStruct((nb, n_blk, 512, 1), F32)],
            grid=(nb, n_blk),
            in_specs=[zspec(256, 7), zspec(512, 4), zspec(512, 12)] + w_specs,
            out_specs=[kv_spec, pl.BlockSpec((1, 1, 512, 1), lambda b, n: (b, n, 0, 0))],
            compiler_params=_params(2),
            name="gla_state_lat",
        )(z, z, z, wa_p, ba_p)
        in_specs += [pl.BlockSpec((1, n_blk, 2, N_HEADS, DK, DV), lambda b, n: (b, 0, 0, 0, 0, 0)),
                     pl.BlockSpec((1, n_blk, 512, 1), lambda b, n: (b, 0, 0, 0)),
                     pl.BlockSpec((1, 2, N_HEADS, DK, DV), lambda b, n: (b, 0, 0, 0, 0))]
        args += [kv, tot, s0]
    res = pl.pallas_call(
        functools.partial(_gla_kernel, n_blk=n_blk, has_state=not ctx, emit_state=ctx),
        out_shape=out_shape, grid=(nb, n_blk), in_specs=in_specs, out_specs=out_specs,
        compiler_params=_params(2),
        name="gla_ctx" if ctx else "gla_lat",
    )(*args)
    return (res[0], res[1]) if ctx else (res[0], None)


def _rope_tables():
    nf = MLA_ROPE // 4
    pos = np.arange(LAT_LEN)
    freqs = (10000.0 ** (-np.arange(nf, dtype=np.float32) / nf)).astype(np.float32)
    ang_r = ((pos // GRID_W).astype(np.float32)[:, None] * freqs).astype(np.float32)
    ang_c = ((pos % GRID_W).astype(np.float32)[:, None] * freqs).astype(np.float32)
    cos = np.ones((TM + LAT_LEN, HEAD_PAD), np.float32)
    sa = np.zeros((TM + LAT_LEN, HEAD_PAD), np.float32)
    sb = np.zeros((TM + LAT_LEN, HEAD_PAD), np.float32)
    o = MLA_NOPE
    for base, ang in ((o, ang_r), (o + 2 * nf, ang_c)):
        cos[TM:, base:base + nf] = np.cos(ang)
        cos[TM:, base + nf:base + 2 * nf] = np.cos(ang)
        sa[TM:, base:base + nf] = -np.sin(ang)
        sb[TM:, base + nf:base + 2 * nf] = np.sin(ang)
    return jnp.asarray(cos), jnp.asarray(sa), jnp.asarray(sb)


def _rope(x, cos, sa, sb):
    return x * cos + pltpu.roll(x, 128 - 8, 1) * sa + pltpu.roll(x, 8, 1) * sb


def _head_segments():
    seg = np.zeros((MLA_HEADS * HEAD_PAD, 128), np.float32)
    for h in range(MLA_HEADS):
        seg[h * HEAD_PAD:(h + 1) * HEAD_PAD, h] = 1.0
    return jnp.asarray(seg, BF16), jnp.asarray(seg.T.copy(), BF16)


def _head_norm(x, gain, seg_ref, segt_ref):
    ss = _dot((x * x).astype(BF16), seg_ref[...])
    rs = lax.rsqrt(ss * (1.0 / MLA_QK) + EPS)
    hi = rs.astype(BF16)
    lo = (rs - hi.astype(F32)).astype(BF16)
    return x * (_dot(hi, segt_ref[...]) + _dot(lo, segt_ref[...])) * gain


def _store_heads(x, o_ref, rope):
    if rope is None:
        o_ref[...] = x.astype(BF16)
    else:
        for h in range(MLA_HEADS):
            o_ref[:, 128 * h:128 * h + 128] = _rope(x[:, 128 * h:128 * h + 128], *rope).astype(BF16)


def _mla_keys(ckv, kr_tile, wk_ref, wv_ref, kn_ref, seg_ref, segt_ref, k_ref, v_ref, rope=None):
    cb = ckv.astype(BF16)
    kpre = _dot(cb, wk_ref[...]) + jnp.concatenate([kr_tile] * MLA_HEADS, axis=1)
    v_ref[...] = _dot(cb, wv_ref[...]).astype(BF16)
    _store_heads(_head_norm(kpre, kn_ref[...], seg_ref, segt_ref), k_ref, rope)


def _mla_prep_kernel(small_ref, qa_ref, wuq_ref, qn_ref, kva_ref, wk_ref, wv_ref, kn_ref, seg_ref, segt_ref,
                     cos_ref, sa_ref, sb_ref, q_ref, k_ref, v_ref, ckv_ref, kr_ref):
    def body(rope):
        sm = small_ref[...].astype(F32)
        cq, ckv_raw, g3 = sm[:, 0:256], sm[:, 256:384], sm[:, 384:512]
        cqn = cq * lax.rsqrt(jnp.mean(cq * cq, axis=-1, keepdims=True) + EPS) * qa_ref[...]
        q = _dot(cqn.astype(BF16), wuq_ref[...])
        _store_heads(_head_norm(q, qn_ref[...], seg_ref, segt_ref), q_ref, rope)
        ckv = ckv_raw * lax.rsqrt(jnp.mean(ckv_raw * ckv_raw, axis=-1, keepdims=True) + EPS) * kva_ref[...]
        ckv_ref[...] = ckv
        lane = lax.broadcasted_iota(jnp.int32, (1, 128), 1)
        kr = jnp.where(lane < MLA_ROPE, g3, 0.0)
        kr_ref[...] = kr
        _mla_keys(ckv, pltpu.roll(kr, MLA_NOPE, 1), wk_ref, wv_ref, kn_ref, seg_ref, segt_ref, k_ref, v_ref, rope)

    is_ctx = pl.program_id(0) < N_CTX_TILES

    @pl.when(is_ctx)
    def _():
        body(None)

    @pl.when(jnp.logical_not(is_ctx))
    def _():
        body((cos_ref[...], sa_ref[...], sb_ref[...]))


def _mla_cache_kernel(ckv_ref, kr_ref, wk_ref, wv_ref, kn_ref, seg_ref, segt_ref, k_ref, v_ref):
    _mla_keys(ckv_ref[...], pltpu.roll(kr_ref[...], MLA_NOPE, 1), wk_ref, wv_ref, kn_ref, seg_ref, segt_ref,
              k_ref, v_ref)


def _mla_prep(z, w, rope_tabs):
    def rope_blk(i):
        return jnp.where(i < N_CTX_TILES, 0, 1 + (i - N_CTX_TILES) % LAT_TILES)

    const = lambda shape: pl.BlockSpec(shape, lambda i: (0,) * len(shape))
    rope_spec = pl.BlockSpec((TM, HEAD_PAD), lambda i: (rope_blk(i), 0))
    row = lambda width: pl.BlockSpec((TM, width), lambda i: (i, 0))
    return pl.pallas_call(
        _mla_prep_kernel,
        out_shape=[jax.ShapeDtypeStruct((T_ALL, 1024), BF16), jax.ShapeDtypeStruct((T_ALL, 1024), BF16),
                   jax.ShapeDtypeStruct((T_ALL, 512), BF16), jax.ShapeDtypeStruct((T_ALL, 128), F32),
                   jax.ShapeDtypeStruct((T_ALL, 128), F32)],
        grid=(N_TILES,),
        in_specs=[pl.BlockSpec((TM, 512), lambda i: (i, 12)),
                  const((1, 256)), const((256, 1024)), const((1, 1024)), const((1, 128)),
                  const((128, 1024)), const((128, 512)), const((1, 1024)),
                  const((1024, 128)), const((128, 1024)),
                  rope_spec, rope_spec, rope_spec],
        out_specs=[row(1024), row(1024), row(512), row(128), row(128)],
        compiler_params=_params(1),
        name="mla_prep",
    )(z, w["qa_g"], w["wuq"], w["qn_g"], w["kva_g"], w["wk"], w["wv"], w["kn_g"], *_head_segments(), *rope_tabs)


def _mla_cache(ckv, kr_pad, w):
    const = lambda shape: pl.BlockSpec(shape, lambda i: (0,) * len(shape))
    row = lambda width: pl.BlockSpec((TM, width), lambda i: (i, 0))
    n = ckv.shape[0]
    return pl.pallas_call(
        _mla_cache_kernel,
        out_shape=[jax.ShapeDtypeStruct((n, 1024), BF16), jax.ShapeDtypeStruct((n, 512), BF16)],
        grid=(n // TM,),
        in_specs=[row(128), row(128), const((128, 1024)), const((128, 512)), const((1, 1024)),
                  const((1024, 128)), const((128, 1024))],
        out_specs=[row(1024), row(512)],
        compiler_params=_params(1),
        name="mla_cache",
    )(ckv, kr_pad, w["wk"], w["wv"], w["kn_g"], *_head_segments())


def _mla_attn_kernel(*refs, has_cache):
    q_ref, k_ref, v_ref = refs[:3]
    pos = 3
    if has_cache:
        kc_ref, vc_ref = refs[3:5]
        pos = 5
    o_ref = refs[pos]
    for p in range(MLA_HEADS // 2):
        acc = jnp.zeros((TM, 128), F32)
        for hh in range(2):
            h = 2 * p + hh
            lanes = slice(128 * h, 128 * h + 128)
            qh = q_ref[:, lanes]
            l1 = _dot_t(qh, k_ref[:, lanes])
            m = jnp.max(l1, axis=-1, keepdims=True)
            if has_cache:
                l0 = _dot_t(qh, kc_ref[:, lanes])
                m = jnp.maximum(m, jnp.max(l0, axis=-1, keepdims=True))
                p0 = jnp.exp(l0 - m)
            p1 = jnp.exp(l1 - m)
            den = jnp.sum(p1, axis=-1, keepdims=True)
            if has_cache:
                den = den + jnp.sum(p0, axis=-1, keepdims=True)
            mask = _lane_half_mask(hh)
            vp = v_ref[:, 128 * p:128 * p + 128]
            o = _dot(p1.astype(BF16), jnp.where(mask, vp, jnp.zeros_like(vp)))
            if has_cache:
                vcp = vc_ref[:, 128 * p:128 * p + 128]
                o += _dot(p0.astype(BF16), jnp.where(mask, vcp, jnp.zeros_like(vcp)))
            acc += o * (1.0 / den)
        o_ref[:, 128 * p:128 * p + 128] = acc.astype(BF16)


def _mla_attn(q, k, v, kc, vc, *, ctx):
    if ctx:
        nb, seq, row_blk, tile0 = N_CTX_SEQ, CTX_LEN, 0, 0
    else:
        nb, seq, row_blk, tile0 = N_LAT_SEQ, LAT_LEN, T_CTX // LAT_LEN, N_CTX_TILES
    nq = seq // TM
    in_specs = [pl.BlockSpec((TM, 1024), lambda b, i: (tile0 + b * nq + i, 0)),
                pl.BlockSpec((seq, 1024), lambda b, i: (row_blk + b, 0)),
                pl.BlockSpec((seq, 512), lambda b, i: (row_blk + b, 0))]
    args = [q, k, v]
    if not ctx:
        in_specs += [pl.BlockSpec((TM, 1024), lambda b, i: (b, 0)), pl.BlockSpec((TM, 512), lambda b, i: (b, 0))]
        args += [kc, vc]
    return pl.pallas_call(
        functools.partial(_mla_attn_kernel, has_cache=not ctx),
        out_shape=jax.ShapeDtypeStruct((nb * seq, 512), BF16),
        grid=(nb, nq), in_specs=in_specs,
        out_specs=pl.BlockSpec((TM, 512), lambda b, i: (b * nq + i, 0)),
        compiler_params=_params(2),
        name="mla_attn_ctx" if ctx else "mla_attn_lat",
    )(*args)


def _merge_kernel(retc_ref, retl_ref, glac_ref, glal_ref, mlac_ref, mlal_ref, m0_ref, m1_ref, m2_ref,
                  xc_ref, xl_ref, wb_ref, wo_ref,
                  g1_ref, n2_ref, sh2_ref, sc2_ref, rwh_ref, rwl_ref, rb_ref,
                  x1_ref, h_ref, idx_ref, w_ref, rank_ref, cnt_ref):
    @pl.when(pl.program_id(0) == 0)
    def _():
        cnt_ref[...] = jnp.zeros_like(cnt_ref)

    is_ctx = pl.program_id(0) < N_CTX_TILES
    mix = None
    for c_ref, l_ref, m_ref, n in ((retc_ref, retl_ref, m0_ref, 0), (glac_ref, glal_ref, m1_ref, 1),
                                   (mlac_ref, mlal_ref, m2_ref, 2)):
        branch = jnp.where(is_ctx, c_ref[...], l_ref[...])
        term = _sigmoid(m_ref[...]).astype(F32) * _dot(branch, wb_ref[n])
        mix = term if mix is None else mix + term
    out = _dot(mix.astype(BF16), wo_ref[...])
    x1 = jnp.where(is_ctx, xc_ref[...], xl_ref[...]) + g1_ref[0] * out
    x1_ref[...] = x1
    h = x1 * lax.rsqrt(jnp.mean(x1 * x1, axis=-1, keepdims=True) + EPS) * n2_ref[...]
    h = h * (1.0 + sc2_ref[0]) + sh2_ref[0]
    h_ref[...] = h
    hh = h.astype(BF16)
    hl = (h - hh.astype(F32)).astype(BF16)
    logits = _dot(hh, rwh_ref[...]) + _dot(hh, rwl_ref[...]) + _dot(hl, rwh_ref[...]) + rb_ref[...]
    rows = TM // ROUTE_CHUNKS
    lane = lax.broadcasted_iota(jnp.int32, (rows, 128), 1)
    lanef = lane.astype(F32)
    onehots, osums = [], []
    for c in range(ROUTE_CHUNKS):
        l = jnp.where(lane < N_EXPERTS, logits[c * rows:(c + 1) * rows], -jnp.inf)
        vals, idxs = [], []
        for _ in range(TOP_K):
            m = jnp.max(l, axis=-1, keepdims=True)
            ix = jnp.min(jnp.where(l == m, lanef, 128.0), axis=-1, keepdims=True)
            vals.append(m)
            idxs.append(ix)
            l = jnp.where(lanef == ix, -jnp.inf, l)
        es = [jnp.exp(v - vals[0]) for v in vals]
        inv = 1.0 / (es[0] + es[1] + es[2] + es[3])
        idx_out = jnp.zeros((rows, 128), F32)
        w_out = jnp.zeros((rows, 128), F32)
        for kk in range(TOP_K):
            idx_out = jnp.where(lane == kk, idxs[kk], idx_out)
            w_out = jnp.where(lane == kk, es[kk] * inv, w_out)
        idx_ref[c * rows:(c + 1) * rows, :] = idx_out.astype(jnp.int32)
        w_ref[c * rows:(c + 1) * rows, :] = w_out
        oh = [jnp.where(lanef == ix, 1.0, 0.0) for ix in idxs]
        onehots.append(oh)
        osums.append((oh[0] + oh[1]) + (oh[2] + oh[3]))

    osum = jnp.concatenate(osums, axis=0)
    ri = lax.broadcasted_iota(jnp.int32, (TM, TM), 0)
    ci = lax.broadcasted_iota(jnp.int32, (TM, TM), 1)
    before = jnp.where(ri > ci, 1.0, 0.0).astype(BF16)
    prior = _dot(before, osum.astype(BF16)) + cnt_ref[0:1, :]
    for c in range(ROUTE_CHUNKS):
        pc = prior[c * rows:(c + 1) * rows]
        rank_out = jnp.zeros((rows, 128), F32)
        for kk in range(TOP_K):
            rank_out = jnp.where(lane == kk, jnp.sum(onehots[c][kk] * pc, axis=-1, keepdims=True), rank_out)
        rank_ref[c * rows:(c + 1) * rows, :] = rank_out.astype(jnp.int32)
    cnt_ref[...] = cnt_ref[...] + jnp.sum(osum, axis=0, keepdims=True)


def _merge(branches, z, xc, xl, modr, w, layer):
    mrow = _mod_row(TM)
    base = layer * N_MOD

    def mod_spec(part):
        return pl.BlockSpec((1, 1, D_MODEL), lambda i: ((base + mrow(i)) * 6 + part, 0, 0))

    const = lambda shape: pl.BlockSpec(shape, lambda i: (0,) * len(shape))
    row = lambda width: pl.BlockSpec((TM, width), lambda i: (i, 0))
    gate = lambda col: pl.BlockSpec((TM, 1024), lambda i: (i, col))
    return pl.pallas_call(
        _merge_kernel,
        out_shape=[jax.ShapeDtypeStruct((T_ALL, D_MODEL), F32), jax.ShapeDtypeStruct((T_ALL, D_MODEL), F32),
                   jax.ShapeDtypeStruct((T_ALL, 128), jnp.int32), jax.ShapeDtypeStruct((T_ALL, 128), F32),
                   jax.ShapeDtypeStruct((T_ALL, 128), jnp.int32), jax.ShapeDtypeStruct((8, 128), F32)],
        grid=(N_TILES,),
        in_specs=[_ctx_spec(TM, 512), _lat_spec(TM, 512)] * 3 + [gate(3), gate(4), gate(5),
                  _ctx_spec(TM, D_MODEL), _lat_spec(TM, D_MODEL),
                  const((3, 512, 1024)), const((1024, 1024)),
                  mod_spec(2), const((1, 1024)), mod_spec(3), mod_spec(4),
                  const((1024, 128)), const((1024, 128)), const((1, 128))],
        out_specs=[row(1024), row(1024), row(128), row(128), row(128), const((8, 128))],
        compiler_params=_params(1),
        name="merge",
    )(*branches, z, z, z, xc, xl, w["wb"], w["wo"], modr, w["n2_g"], modr, modr,
      w["rw_hi"], w["rw_lo"], w["rb"])


def _route(top_idx, rank, counts):
    flat_e = top_idx.reshape(N_SLOTS)
    onehot = (flat_e[:, None] == jnp.arange(N_EXPERTS, dtype=jnp.int32)[None, :]).astype(jnp.int32)
    padded = (counts + MOE_ROWS - 1) // MOE_ROWS * MOE_ROWS
    pad_end = jnp.cumsum(padded)
    pad_start = pad_end - padded
    dest = (rank.reshape(N_SLOTS) + jnp.sum(onehot * pad_start[None, :], axis=1)).astype(jnp.int32)
    blk_start = jnp.arange(N_MOE_BLOCKS, dtype=jnp.int32) * MOE_ROWS
    block_e = jnp.minimum(jnp.sum((pad_end[None, :] <= blk_start[:, None]).astype(jnp.int32), axis=1),
                          N_EXPERTS - 1).astype(jnp.int32)
    n_used = (pad_end[-1] // MOE_ROWS).astype(jnp.int32)
    e_hot = (block_e[:, None] == jnp.arange(N_EXPERTS, dtype=jnp.int32)[None, :]).astype(jnp.int32)
    nxt_blk = jnp.sum(e_hot * pad_end[None, :], axis=1) // MOE_ROWS
    b_hot = (nxt_blk[:, None] == jnp.arange(N_MOE_BLOCKS, dtype=jnp.int32)[None, :]).astype(jnp.int32)
    next_e = jnp.where(nxt_blk < n_used, jnp.sum(b_hot * block_e[None, :], axis=1), block_e).astype(jnp.int32)
    tail_start = (pad_start + counts).astype(jnp.int32)
    return dest, tail_start, block_e, n_used.reshape(1), next_e


def _dispatch_kernel(dest_ref, tail_ref, nb_ref, h_ref, xs_ref, zero_buf, stage, sem, ssem):
    i = pl.program_id(0)

    @pl.when(i == 0)
    def _():
        zero_buf[...] = jnp.zeros_like(zero_buf)
        fills = [pltpu.make_async_copy(
            zero_buf, xs_ref.at[pl.ds(pl.multiple_of((tail_ref[e] // 8) * 8, 8), TAIL_FILL), :], sem)
            for e in range(N_EXPERTS)]
        for f in fills:
            f.start()
        for f in fills:
            f.wait()

        def fill_block(b, c):
            f = pltpu.make_async_copy(zero_buf.at[pl.ds(0, MOE_ROWS), :],
                                      xs_ref.at[pl.ds(pl.multiple_of(b * MOE_ROWS, MOE_ROWS), MOE_ROWS), :], sem)
            f.start()
            f.wait()
            return c
        lax.fori_loop(nb_ref[0], (N_MOE_ROWS + XS_EXTRA) // MOE_ROWS, fill_block, 0)

    base = i * TM * TOP_K
    cur = i % 2

    def wait_tile(slot):
        for kk in range(TOP_K):
            pltpu.make_async_copy(stage.at[slot], xs_ref.at[pl.ds(0, TM), :], ssem.at[slot]).wait()

    @pl.when(i >= 2)
    def _():
        wait_tile(cur)
    stage[cur] = h_ref[...]

    def issue(t, c):
        for kk in range(TOP_K):
            pltpu.make_async_copy(stage.at[cur, pl.ds(t, 1), :],
                                  xs_ref.at[pl.ds(dest_ref[base + t * TOP_K + kk], 1), :], ssem.at[cur]).start()
        return c
    lax.fori_loop(0, TM, issue, 0, unroll=8)

    @pl.when(i == N_TILES - 1)
    def _():
        wait_tile(1 - cur)
        wait_tile(cur)


def _moe_dispatch(h, dest, tail_start, n_used):
    return pl.pallas_call(
        _dispatch_kernel,
        out_shape=jax.ShapeDtypeStruct((N_MOE_ROWS + XS_EXTRA, D_MODEL), F32),
        grid_spec=pltpu.PrefetchScalarGridSpec(
            num_scalar_prefetch=3, grid=(N_TILES,),
            in_specs=[pl.BlockSpec((TM, D_MODEL), lambda i, d, t, nb: (i, 0))],
            out_specs=pl.BlockSpec(memory_space=pl.ANY),
            scratch_shapes=[pltpu.VMEM((TAIL_FILL, D_MODEL), F32), pltpu.VMEM((2, TM, D_MODEL), F32),
                            pltpu.SemaphoreType.DMA, pltpu.SemaphoreType.DMA((2,))]),
        compiler_params=_params(1),
        name="moe_dispatch",
    )(dest, tail_start, n_used, h)


def _expert_kernel(be_ref, nb_ref, nxt_ref, x_ref, wgu_hbm, bgu_ref, wd_hbm, bd_ref, o_ref,
                   wgu_st, wd_st, wgu_bf, wd_bf, sem, *, layer):
    i = pl.program_id(0)
    e = be_ref[i]
    prev = be_ref[jnp.maximum(i - 1, 0)]

    def weight_copies(expert):
        idx = layer * N_EXPERTS + expert
        return (pltpu.make_async_copy(wgu_hbm.at[idx], wgu_st, sem.at[0]),
                pltpu.make_async_copy(wd_hbm.at[idx], wd_st, sem.at[1]))

    @pl.when(i == 0)
    def _():
        for cp in weight_copies(e):
            cp.start()

    @pl.when(((i == 0) | (e != prev)) & (i < nb_ref[0]))
    def _():
        for cp in weight_copies(e):
            cp.wait()
        wgu_bf[...] = wgu_st[...].astype(BF16)
        wd_bf[...] = wd_st[...].astype(BF16)
        nxt = nxt_ref[i]

        @pl.when(nxt != e)
        def _():
            for cp in weight_copies(nxt):
                cp.start()

    @pl.when(i < nb_ref[0])
    def _():
        gu = _dot(x_ref[...].astype(BF16), wgu_bf[...]) + bgu_ref[0]
        gate = jnp.minimum(gu[:, :D_EXPERT], SWIGLU_LIMIT)
        up = jnp.clip(gu[:, D_EXPERT:], -SWIGLU_LIMIT, SWIGLU_LIMIT)
        act = (up + 1.0) * gate * _sigmoid(SWIGLU_ALPHA * gate)
        out = _dot(act.astype(BF16), wd_bf[...]) + bd_ref[0]
        o_ref[...] = out

    @pl.when(i >= nb_ref[0])
    def _():
        o_ref[...] = jnp.zeros_like(o_ref)


def _moe_experts(xs, block_e, n_used, next_e, w_gu, b_gu, w_down, b_down, layer):
    w_idx = lambda i, be, nb, nx: (layer * N_EXPERTS + be[i], 0, 0)
    return pl.pallas_call(
        functools.partial(_expert_kernel, layer=layer),
        out_shape=jax.ShapeDtypeStruct((N_MOE_ROWS, D_MODEL), F32),
        grid_spec=pltpu.PrefetchScalarGridSpec(
            num_scalar_prefetch=3, grid=(N_MOE_BLOCKS,),
            in_specs=[pl.BlockSpec((MOE_ROWS, D_MODEL), lambda i, be, nb, nx: (jnp.minimum(i, nb[0] - 1), 0)),
                      pl.BlockSpec(memory_space=pl.ANY),
                      pl.BlockSpec((1, 1, 2 * D_EXPERT), w_idx),
                      pl.BlockSpec(memory_space=pl.ANY),
                      pl.BlockSpec((1, 1, D_MODEL), w_idx)],
            out_specs=pl.BlockSpec((MOE_ROWS, D_MODEL), lambda i, be, nb, nx: (i, 0)),
            scratch_shapes=[pltpu.VMEM((D_MODEL, 2 * D_EXPERT), F32), pltpu.VMEM((D_EXPERT, D_MODEL), F32),
                            pltpu.VMEM((D_MODEL, 2 * D_EXPERT), BF16), pltpu.VMEM((D_EXPERT, D_MODEL), BF16),
                            pltpu.SemaphoreType.DMA((2,))]),
        compiler_params=_params(1),
        name="moe_experts",
    )(block_e, n_used, next_e, xs, w_gu, b_gu, w_down, b_down)


def _combine_kernel(dest_ref, x_ref, g2_ref, w_ref, eo_ref, yc_ref, yl_ref, buf, sem):
    i = pl.program_id(0)
    cur = i % 2

    def issue_tile(tile, slot):
        base = tile * TM * TOP_K

        def issue(t, c):
            for kk in range(TOP_K):
                pltpu.make_async_copy(eo_ref.at[pl.ds(dest_ref[base + t * TOP_K + kk], 1), :],
                                      buf.at[slot, kk, pl.ds(t, 1), :], sem.at[slot]).start()
            return c
        lax.fori_loop(0, TM, issue, 0, unroll=8)

    @pl.when(i == 0)
    def _():
        issue_tile(0, 0)

    @pl.when(i + 1 < N_TILES)
    def _():
        issue_tile(i + 1, 1 - cur)

    for kk in range(TOP_K):
        pltpu.make_async_copy(eo_ref.at[pl.ds(0, TM), :], buf.at[cur, kk], sem.at[cur]).wait()
    w = w_ref[...]
    ff = None
    for kk in range(TOP_K):
        term = buf[cur, kk] * w[:, kk:kk + 1]
        ff = term if ff is None else ff + term
    y = x_ref[...] + g2_ref[0] * ff
    is_ctx = i < N_CTX_TILES

    @pl.when(is_ctx)
    def _():
        yc_ref[...] = y

    @pl.when(jnp.logical_not(is_ctx))
    def _():
        yl_ref[...] = y


def _moe_combine(dest, x1, modr, top_w, eo, layer):
    mrow = _mod_row(TM)
    base = layer * N_MOD
    return pl.pallas_call(
        _combine_kernel,
        out_shape=[jax.ShapeDtypeStruct((T_CTX, D_MODEL), F32), jax.ShapeDtypeStruct((T_LAT, D_MODEL), F32)],
        grid_spec=pltpu.PrefetchScalarGridSpec(
            num_scalar_prefetch=1, grid=(N_TILES,),
            in_specs=[pl.BlockSpec((TM, D_MODEL), lambda i, d: (i, 0)),
                      pl.BlockSpec((1, 1, D_MODEL), lambda i, d: ((base + mrow(i)) * 6 + 5, 0, 0)),
                      pl.BlockSpec((TM, 128), lambda i, d: (i, 0)),
                      pl.BlockSpec(memory_space=pl.ANY)],
            out_specs=[pl.BlockSpec((TM, D_MODEL), lambda i, d: (jnp.minimum(i, N_CTX_TILES - 1), 0)),
                       pl.BlockSpec((TM, D_MODEL), lambda i, d: (jnp.maximum(i - N_CTX_TILES, 0), 0))],
            scratch_shapes=[pltpu.VMEM((2, TOP_K, TM, D_MODEL), F32), pltpu.SemaphoreType.DMA((2,))]),
        compiler_params=_params(1),
        name="moe_combine",
    )(dest, x1, modr, top_w, eo)


def _pad_heads(w, n_heads, width):
    lead = w.shape[:-1]
    w = w.reshape(lead + (n_heads, width))
    w = jnp.pad(w, [(0, 0)] * len(lead) + [(0, 0), (0, HEAD_PAD - width)])
    return w.reshape(lead + (n_heads * HEAD_PAD,))


def _layer_weights(l, w_in, gla_wa2, gla_ba, mla_qa_g, mla_wuq, mla_kva_g, mla_wukv, mla_qn_g, mla_kn_g,
                   w_branch, w_out, router_w, router_b, norm2_g):
    wi = w_in[l]
    w_in_p = jnp.concatenate([wi[:, :3072], wi[:, 3520:], wi[:, 3104:3520], wi[:, 3072:3104],
                              jnp.zeros((D_MODEL, 64), F32)], axis=1).astype(BF16)
    wa_p = jnp.zeros((512, 512), F32)
    wa_p = wa_p.at[416:432, 0:256].set(gla_wa2[l, 0]).at[432:448, 256:512].set(gla_wa2[l, 1]).astype(BF16)
    ba_p = gla_ba[l].reshape(1, 512)
    wukv = mla_wukv[l].reshape(128, MLA_HEADS, MLA_NOPE + MLA_V)
    rw = jnp.pad(router_w[l], ((0, 0), (0, 128 - N_EXPERTS)))
    rw_hi = rw.astype(BF16)
    return {
        "w_in": w_in_p, "wa": wa_p, "ba": ba_p,
        "qa_g": mla_qa_g[l].reshape(1, 256),
        "wuq": _pad_heads(mla_wuq[l], MLA_HEADS, MLA_QK).astype(BF16),
        "qn_g": jnp.tile(jnp.pad(mla_qn_g[l], (0, HEAD_PAD - MLA_QK)), MLA_HEADS).reshape(1, 1024) * MLA_QK ** -0.5,
        "kn_g": jnp.tile(jnp.pad(mla_kn_g[l], (0, HEAD_PAD - MLA_QK)), MLA_HEADS).reshape(1, 1024),
        "kva_g": mla_kva_g[l].reshape(1, 128),
        "wk": _pad_heads(wukv[:, :, :MLA_NOPE].reshape(128, MLA_HEADS * MLA_NOPE), MLA_HEADS, MLA_NOPE).astype(BF16),
        "wv": wukv[:, :, MLA_NOPE:].reshape(128, MLA_HEADS * MLA_V).astype(BF16),
        "wb": w_branch[l].astype(BF16), "wo": w_out[l].astype(BF16),
        "rw_hi": rw_hi, "rw_lo": (rw - rw_hi.astype(F32)).astype(BF16),
        "rb": jnp.pad(router_b[l], (0, 128 - N_EXPERTS)).reshape(1, 128),
        "n2_g": norm2_g[l].reshape(1, D_MODEL),
    }


def kernel(x_prompt, x_sample, cache_mla_ckv, cache_mla_krope, state_ret, state_gla, c, c_ctx, w_mod, b_mod, norm1_g, norm2_g, w_in, ret_gn_g, gla_wa2, gla_ba, gla_norm_g, mla_qa_g, mla_wuq, mla_kva_g, mla_wukv, mla_qn_g, mla_kn_g, w_branch, w_out, router_w, router_b, moe_w_gu, moe_b_gu, moe_w_down, moe_b_down):
    xc, xl = x_prompt.reshape(T_CTX, D_MODEL), x_sample.reshape(T_LAT, D_MODEL)
    cc = jnp.concatenate([c_ctx[None, :], c, jnp.zeros((N_MOD - 1 - N_LAT_SEQ, D_MODEL), F32)], axis=0)
    modr = _modulation(cc, w_mod, b_mod).reshape(DEPTH * N_MOD * 6, 1, D_MODEL)
    rope_tabs = _rope_tables()
    w_gu = moe_w_gu.reshape(DEPTH * N_EXPERTS, D_MODEL, 2 * D_EXPERT)
    b_gu = moe_b_gu.reshape(DEPTH * N_EXPERTS, 1, 2 * D_EXPERT)
    w_dn = moe_w_down.reshape(DEPTH * N_EXPERTS, D_EXPERT, D_MODEL)
    b_dn = moe_b_down.reshape(DEPTH * N_EXPERTS, 1, D_MODEL)

    ckv_l, krope_l, ret_l, gla_l = [], [], [], []
    for l in range(DEPTH):
        w = _layer_weights(l, w_in, gla_wa2, gla_ba, mla_qa_g, mla_wuq, mla_kva_g, mla_wukv, mla_qn_g, mla_kn_g,
                           w_branch, w_out, router_w, router_b, norm2_g)
        z = _in_proj(xc, xl, norm1_g[l].reshape(1, D_MODEL), modr, w["w_in"], l)

        gn = ret_gn_g[l].reshape(1, 512)
        ret_c, ret_state = _retention(z, gn, None, ctx=True)
        (ret_s,) = _retention(z, gn, state_ret[:, l], ctx=False)
        gng = gla_norm_g[l].reshape(1, 512)
        gla_c, gla_state = _gla(z, w["wa"], w["ba"], gng, None, ctx=True)
        gla_s, _ = _gla(z, w["wa"], w["ba"], gng, state_gla[:, l], ctx=False)

        q, k, v, ckv, kr = _mla_prep(z, w, rope_tabs)
        kc, vc = _mla_cache(cache_mla_ckv[:, l].reshape(N_LAT_SEQ * CTX_LEN, 128),
                            jnp.pad(cache_mla_krope[:, l].reshape(N_LAT_SEQ * CTX_LEN, MLA_ROPE),
                                    ((0, 0), (0, 128 - MLA_ROPE))), w)
        mla_c = _mla_attn(q, k, v, None, None, ctx=True)
        mla_s = _mla_attn(q, k, v, kc, vc, ctx=False)

        x1, h2, top_idx, top_w, rank, cnt = _merge((ret_c, ret_s, gla_c, gla_s, mla_c, mla_s), z, xc, xl, modr, w, l)
        dest, tail_start, block_e, n_used, next_e = _route(top_idx[:, :TOP_K], rank[:, :TOP_K],
                                                           cnt[0, :N_EXPERTS].astype(jnp.int32))
        xs = _moe_dispatch(h2, dest, tail_start, n_used)
        eo = _moe_experts(xs, block_e, n_used, next_e, w_gu, b_gu, w_dn, b_dn, l)
        xc, xl = _moe_combine(dest, x1, modr, top_w, eo, l)

        ckv_l.append(ckv[:T_CTX].reshape(N_CTX_SEQ, CTX_LEN, 128))
        krope_l.append(kr[:T_CTX, :MLA_ROPE].reshape(N_CTX_SEQ, CTX_LEN, MLA_ROPE))
        ret_l.append(ret_state)
        gla_l.append(gla_state.reshape(N_CTX_SEQ, 2, N_HEADS, DK, DV))

    y_p = xc.reshape(N_CTX_SEQ, CTX_LEN, D_MODEL)
    y_s = xl.reshape(N_LAT_SEQ, LAT_LEN, D_MODEL)
    return (y_p, y_s, jnp.stack(ckv_l, axis=1), jnp.stack(krope_l, axis=1),
            jnp.stack(ret_l, axis=1), jnp.stack(gla_l, axis=1))
```

```python
import functools

import jax
import jax.numpy as jnp
import numpy as np
from jax import lax
from jax.experimental import pallas as pl
from jax.experimental.pallas import tpu as pltpu

F32 = jnp.float32
BF16 = jnp.bfloat16

D_MODEL = 1024
DEPTH = 2
N_CTX_SEQ, CTX_LEN = 32, 256
N_LAT_SEQ, LAT_LEN = 4, 1024
T_CTX = N_CTX_SEQ * CTX_LEN
T_LAT = N_LAT_SEQ * LAT_LEN
T_ALL = T_CTX + T_LAT
TM = 256
N_TILES = T_ALL // TM
N_CTX_TILES = T_CTX // TM
LAT_TILES = LAT_LEN // TM
N_MOD = 8
EPS = 1e-6

N_HEADS = 4
DK, DV = 64, 128
GRID_W = 64
MLA_HEADS, MLA_NOPE, MLA_ROPE, MLA_V = 8, 64, 32, 64
MLA_QK = MLA_NOPE + MLA_ROPE
HEAD_PAD = 128
GLA_TAU = 16.0
N_EXPERTS, TOP_K, D_EXPERT = 32, 4, 1024
SWIGLU_LIMIT, SWIGLU_ALPHA = 7.0, 1.702
MOE_ROWS = 512
N_SLOTS = T_ALL * TOP_K
N_MOE_BLOCKS = N_SLOTS // MOE_ROWS + N_EXPERTS
N_MOE_ROWS = N_MOE_BLOCKS * MOE_ROWS
ROUTE_CHUNKS = 4
TAIL_FILL = MOE_ROWS + 8
XS_EXTRA = 2 * MOE_ROWS

DZ = 6656
IN_TILE = 512
VMEM_LIMIT = 56 * 1024 * 1024

RET_LOG_F = [float(np.log1p(-np.exp2(-(5.0 + h)))) for h in range(N_HEADS)]
RET_LOG_B = [float(np.log1p(-np.exp2(-(5.5 + h)))) for h in range(N_HEADS)]


def _params(n_axes, vmem=VMEM_LIMIT):
    return pltpu.CompilerParams(dimension_semantics=("arbitrary",) * n_axes, vmem_limit_bytes=vmem)


def _sigmoid(x):
    return 1.0 / (1.0 + jnp.exp(-x))


def _dot(a, b):
    return jnp.dot(a, b, preferred_element_type=F32)


def _dot_t(a, b):
    return lax.dot_general(a, b, (((1,), (1,)), ((), ())), preferred_element_type=F32)


def _mod_row(tile_rows):
    def f(i):
        r0 = i * tile_rows
        return jnp.where(r0 < T_CTX, 0, 1 + (r0 - T_CTX) // LAT_LEN)
    return f


def _ctx_spec(rows, width):
    n_ctx = T_CTX // rows
    return pl.BlockSpec((rows, width), lambda i: (jnp.minimum(i, n_ctx - 1), 0))


def _lat_spec(rows, width):
    n_ctx = T_CTX // rows
    return pl.BlockSpec((rows, width), lambda i: (jnp.maximum(i - n_ctx, 0), 0))


def _mod_kernel(c_ref, w_ref, b_ref, o_ref):
    c = c_ref[...]
    s = c * _sigmoid(c)
    o_ref[0] = jnp.dot(s, w_ref[0], preferred_element_type=F32, precision=lax.Precision.HIGHEST) + b_ref[0]


def _modulation(cc, w_mod, b_mod):
    n = 6 * D_MODEL
    blk = 2048
    return pl.pallas_call(
        _mod_kernel,
        out_shape=jax.ShapeDtypeStruct((DEPTH, N_MOD, n), F32),
        grid=(DEPTH, n // blk),
        in_specs=[pl.BlockSpec((N_MOD, D_MODEL), lambda l, j: (0, 0)),
                  pl.BlockSpec((1, D_MODEL, blk), lambda l, j: (l, 0, j)),
                  pl.BlockSpec((1, 1, blk), lambda l, j: (l, 0, j))],
        out_specs=pl.BlockSpec((1, N_MOD, blk), lambda l, j: (l, 0, j)),
        compiler_params=_params(2),
        name="modulation",
    )(cc, w_mod, b_mod.reshape(DEPTH, 1, n))


def _in_kernel(xc_ref, xl_ref, g_ref, sh_ref, sc_ref, w_ref, o_ref):
    x = jnp.where(pl.program_id(0) < T_CTX // IN_TILE, xc_ref[...], xl_ref[...])
    h = x * lax.rsqrt(jnp.mean(x * x, axis=-1, keepdims=True) + EPS) * g_ref[...]
    h = h * (1.0 + sc_ref[0]) + sh_ref[0]
    hb = h.astype(BF16)
    for n0 in range(0, DZ, 512):
        o_ref[:, n0:n0 + 512] = _dot(hb, w_ref[:, n0:n0 + 512]).astype(BF16)


def _in_proj(xc, xl, g, modr, w_in_p, layer):
    mrow = _mod_row(IN_TILE)
    base = layer * N_MOD

    def mod_spec(part):
        return pl.BlockSpec((1, 1, D_MODEL), lambda i: ((base + mrow(i)) * 6 + part, 0, 0))

    return pl.pallas_call(
        _in_kernel,
        out_shape=jax.ShapeDtypeStruct((T_ALL, DZ), BF16),
        grid=(T_ALL // IN_TILE,),
        in_specs=[_ctx_spec(IN_TILE, D_MODEL), _lat_spec(IN_TILE, D_MODEL),
                  pl.BlockSpec((1, D_MODEL), lambda i: (0, 0)),
                  mod_spec(0), mod_spec(1),
                  pl.BlockSpec((D_MODEL, DZ), lambda i: (0, 0))],
        out_specs=pl.BlockSpec((IN_TILE, DZ), lambda i: (i, 0)),
        compiler_params=_params(1),
        name="in_proj",
    )(xc, xl, g, modr, modr, w_in_p)


def _lane_half_mask(hh):
    lane = lax.broadcasted_iota(jnp.int32, (1, 128), 1)
    return (lane < 64) if hh == 0 else (lane >= 64)


@functools.lru_cache(maxsize=None)
def _ret_decay_table(seq):
    d = np.arange(seq)[:, None] - np.arange(seq)[None, :]
    tab = np.stack([np.exp(np.where(d > 0, RET_LOG_F[h] * d, -RET_LOG_B[h] * d)) for h in range(N_HEADS)])
    return (tab * np.where(d == 0, 2.0, 1.0) * DK ** -0.5).astype(np.float32)


def _ret_kernel(*refs, seq, has_state, emit_state):
    q_ref, k_ref, v_ref, g_ref, gn_ref, dec_ref = refs[:6]
    pos = 6
    if has_state:
        s0_ref = refs[pos]
        pos += 1
    o_ref = refs[pos]
    pos += 1
    if emit_state:
        st_ref = refs[pos]

    r0 = pl.multiple_of(pl.program_id(1) * TM, TM)
    qb = q_ref[pl.ds(r0, TM), :]
    ri = (lax.broadcasted_iota(jnp.int32, (TM, 1), 0) + r0).astype(F32)

    for h in range(N_HEADS):
        p, hh = h // 2, h % 2
        lanes = slice(128 * p, 128 * p + 128)
        qp = qb[:, lanes]
        qh = jnp.where(_lane_half_mask(hh), qp, jnp.zeros_like(qp))
        sc = _dot_t(qh, k_ref[:, lanes])
        o = _dot((sc * dec_ref[h]).astype(BF16), v_ref[:, 128 * h:128 * h + 128])
        if has_state:
            qf = qh.astype(F32)
            o += _dot((qf * jnp.exp(RET_LOG_F[h] * (ri + 1.0))).astype(BF16), s0_ref[0, 0, p].astype(BF16))
            o += _dot((qf * jnp.exp(RET_LOG_B[h] * (seq - ri))).astype(BF16), s0_ref[0, 1, p].astype(BF16))
        mu = jnp.mean(o, axis=-1, keepdims=True)
        d = o - mu
        var = jnp.mean(d * d, axis=-1, keepdims=True)
        on = d * lax.rsqrt(var + EPS)
        g = g_ref[:, 128 * h:128 * h + 128].astype(F32)
        out = on * gn_ref[:, 128 * h:128 * h + 128] * (g * _sigmoid(g))
        o_ref[:, 128 * h:128 * h + 128] = out.astype(BF16)

    if emit_state:
        jc = lax.broadcasted_iota(jnp.int32, (seq, 1), 0).astype(F32)
        lane = lax.broadcasted_iota(jnp.int32, (1, 128), 1)
        for p in range(2):
            kp = k_ref[:, 128 * p:128 * p + 128].astype(F32) * DK ** -0.5
            lgf = jnp.where(lane < 64, RET_LOG_F[2 * p], RET_LOG_F[2 * p + 1])
            lgb = jnp.where(lane < 64, RET_LOG_B[2 * p], RET_LOG_B[2 * p + 1])
            kdf = (kp * jnp.exp(lgf * (seq - 1.0 - jc))).T.astype(BF16)
            kdb = (kp * jnp.exp(lgb * jc)).T.astype(BF16)
            for hh in range(2):
                h = 2 * p + hh
                vh = v_ref[:, 128 * h:128 * h + 128]
                st_ref[0, 0, h] = _dot(kdf, vh)[64 * hh:64 * hh + 64, :]
                st_ref[0, 1, h] = _dot(kdb, vh)[64 * hh:64 * hh + 64, :]


def _retention(z, gn_g, s0, *, ctx):
    if ctx:
        nb, seq, row_blk, tile0 = N_CTX_SEQ, CTX_LEN, 0, 0
    else:
        nb, seq, row_blk, tile0 = N_LAT_SEQ, LAT_LEN, T_CTX // LAT_LEN, N_CTX_TILES
    nq = seq // TM
    in_specs = [pl.BlockSpec((seq, 256), lambda b, i: (row_blk + b, 0)),
                pl.BlockSpec((seq, 256), lambda b, i: (row_blk + b, 1)),
                pl.BlockSpec((seq, 512), lambda b, i: (row_blk + b, 1)),
                pl.BlockSpec((TM, 512), lambda b, i: (tile0 + b * nq + i, 2)),
                pl.BlockSpec((1, 512), lambda b, i: (0, 0)),
                pl.BlockSpec((N_HEADS, TM, seq), lambda b, i: (0, i, 0))]
    args = [z, z, z, z, gn_g, _ret_decay_table(seq)]
    out_shape = [jax.ShapeDtypeStruct((nb * seq, 512), BF16)]
    out_specs = [pl.BlockSpec((TM, 512), lambda b, i: (b * nq + i, 0))]
    if not ctx:
        in_specs.append(pl.BlockSpec((1, 2, 2, 128, 128), lambda b, i: (b, 0, 0, 0, 0)))
        args.append(s0.reshape(N_LAT_SEQ, 2, 2, 128, 128))
    else:
        out_shape.append(jax.ShapeDtypeStruct((nb, 2, N_HEADS, DK, DV), F32))
        out_specs.append(pl.BlockSpec((1, 2, N_HEADS, DK, DV), lambda b, i: (b, 0, 0, 0, 0)))
    return pl.pallas_call(
        functools.partial(_ret_kernel, seq=seq, has_state=not ctx, emit_state=ctx),
        out_shape=out_shape, grid=(nb, nq), in_specs=in_specs, out_specs=out_specs,
        compiler_params=_params(2),
        name="retention_ctx" if ctx else "retention_lat",
    )(*args)


def _gla_decay(small_ref, wa_ref, ba_ref):
    x = _dot(small_ref[...], wa_ref[...]) + ba_ref[...]
    la = -(jnp.maximum(-x, 0.0) + jnp.log(1.0 + jnp.exp(-jnp.abs(x)))) * (1.0 / GLA_TAU)
    ri = lax.broadcasted_iota(jnp.int32, (TM, TM), 0)
    ci = lax.broadcasted_iota(jnp.int32, (TM, TM), 1)
    ltri = jnp.where(ri >= ci, 1.0, 0.0).astype(BF16)
    hi = la.astype(BF16)
    r1 = la - hi.astype(F32)
    mid = r1.astype(BF16)
    lo = (r1 - mid.astype(F32)).astype(BF16)
    cum = _dot(ltri, hi) + _dot(ltri, mid) + _dot(ltri, lo)
    return la, cum


def _gla_state_kernel(k_ref, v_ref, small_ref, wa_ref, ba_ref, kv_ref, tot_ref):
    la, cum = _gla_decay(small_ref, wa_ref, ba_ref)
    bf, bb = cum[:, :256], cum[:, 256:]
    xb = bb - la[:, 256:]
    k = k_ref[...].astype(F32)
    kdf = k * jnp.exp(bf[TM - 1:TM, :] - bf)
    kdb = k * jnp.exp(xb)
    for p in range(2):
        kf_t = kdf[:, 128 * p:128 * p + 128].T.astype(BF16)
        kb_t = kdb[:, 128 * p:128 * p + 128].T.astype(BF16)
        for hh in range(2):
            h = 2 * p + hh
            vh = v_ref[:, 128 * h:128 * h + 128]
            kv_ref[0, 0, 0, h] = _dot(kf_t, vh)[64 * hh:64 * hh + 64, :]
            kv_ref[0, 0, 1, h] = _dot(kb_t, vh)[64 * hh:64 * hh + 64, :]
    tot_ref[0, 0] = jnp.sum(la.T, axis=-1, keepdims=True)


def _mid_bcast(x, s, r):
    w = 2 * s
    if w >= 8:
        n = TM // w
        x3 = x.reshape(n, w, 256)
        return jnp.broadcast_to(x3[:, r:r + 1, :], (n, w, 256)).reshape(TM, 256)
    x3 = x.reshape(TM // 8, 8, 256)
    sub = lax.broadcasted_iota(jnp.int32, (1, 8, 1), 1)
    out = None
    for blk in range(8 // w):
        rowv = jnp.broadcast_to(x3[:, blk * w + r:blk * w + r + 1, :], (TM // 8, 8, 256))
        out = rowv if out is None else jnp.where(sub >= blk * w, rowv, out)
    return out.reshape(TM, 256)


def _gla_kernel(*refs, n_blk, has_state, emit_state):
    q_ref, k_ref, v_ref, g_ref, small_ref, wa_ref, ba_ref, gn_ref = refs[:8]
    pos = 8
    if has_state:
        kv_ref, tot_ref, s0_ref = refs[pos:pos + 3]
        pos += 3
    o_ref = refs[pos]
    pos += 1
    if emit_state:
        kvo_ref = refs[pos]

    la, cum = _gla_decay(small_ref, wa_ref, ba_ref)
    bf, bb = cum[:, :256], cum[:, 256:]
    xb = bb - la[:, 256:]
    q = q_ref[...].astype(F32) * DK ** -0.5
    k = k_ref[...].astype(F32)
    row = lax.broadcasted_iota(jnp.int32, (TM, 1), 0)
    rowi = lax.broadcasted_iota(jnp.int32, (TM, TM), 0)
    colj = lax.broadcasted_iota(jnp.int32, (TM, TM), 1)
    low_half = _lane_half_mask(0)

    def join(fwd, bwd):
        ops = []
        for p in range(N_HEADS // 2):
            f = fwd[:, 128 * p:128 * p + 128]
            br = pltpu.roll(bwd[:, 128 * p:128 * p + 128], 64, 1)
            ops.append(jnp.where(low_half, f, br).astype(BF16))
            ops.append(jnp.where(low_half, br, f).astype(BF16))
        return ops

    qd, kd = join(q, q), join(k, k)
    acc = [jnp.where(rowi == colj, _dot_t(qo, ko), 0.0) for qo, ko in zip(qd, kd)]

    low_f = jnp.where(low_half, 1.0, 0.0)
    s = 1
    while s < TM:
        up_f = jnp.where(((row // s) % 2) == 1, 1.0, 0.0)
        live_even = jnp.where(up_f == low_f, 1.0, 0.0).astype(BF16)
        live = [live_even, 1.0 - live_even]
        dead = [live[1], live[0]]
        af = -jnp.abs(bf - _mid_bcast(bf, s, s - 1))
        ab = -jnp.abs(xb - _mid_bcast(xb, s, s))
        same = (rowi // (2 * s)) == (colj // (2 * s))
        for p in range(N_HEADS // 2):
            a_f = af[:, 128 * p:128 * p + 128]
            a_b = pltpu.roll(ab[:, 128 * p:128 * p + 128], 64, 1)
            for hh in range(2):
                h = 2 * p + hh
                arg = jnp.where(low_half, a_f, a_b) if hh == 0 else jnp.where(low_half, a_b, a_f)
                e = jnp.exp(arg).astype(BF16)
                sl = _dot_t(e * (qd[h] * live[hh]), e * (kd[h] * dead[hh]))
                acc[h] = acc[h] + (jnp.where(same, sl, 0.0) if 2 * s < TM else sl)
        s *= 2

    if has_state:
        n = pl.program_id(1)
        q_state = join(q * jnp.exp(bf), q * jnp.exp(bb[TM - 1:TM, :] - xb))

    if emit_state:
        kdf = k * jnp.exp(bf[TM - 1:TM, :] - bf)
        kdb = k * jnp.exp(xb)
        for p in range(N_HEADS // 2):
            kf_t = kdf[:, 128 * p:128 * p + 128].T.astype(BF16)
            kb_t = kdb[:, 128 * p:128 * p + 128].T.astype(BF16)
            for hh in range(2):
                h = 2 * p + hh
                vh = v_ref[:, 128 * h:128 * h + 128]
                kvo_ref[0, 0, 0, h] = _dot(kf_t, vh)[64 * hh:64 * hh + 64, :]
                kvo_ref[0, 0, 1, h] = _dot(kb_t, vh)[64 * hh:64 * hh + 64, :]

    for h in range(N_HEADS):
        o = _dot(acc[h].astype(BF16), v_ref[:, 128 * h:128 * h + 128])
        if has_state:
            sf = s0_ref[0, 0, h]
            for m in range(n_blk - 1):
                dec = jnp.exp(tot_ref[0, m, 64 * h:64 * h + 64, :])
                sf = jnp.where(m < n, dec * sf + kv_ref[0, m, 0, h], sf)
            sb = s0_ref[0, 1, h]
            for m in range(n_blk - 1, 0, -1):
                dec = jnp.exp(tot_ref[0, m, 256 + 64 * h:256 + 64 * h + 64, :])
                sb = jnp.where(m > n, dec * sb + kv_ref[0, m, 1, h], sb)
            state = jnp.concatenate([sf, sb] if h % 2 == 0 else [sb, sf], axis=0).astype(BF16)
            o += _dot(q_state[h], state)
        on = o * lax.rsqrt(jnp.mean(o * o, axis=-1, keepdims=True) + EPS)
        g = g_ref[:, 128 * h:128 * h + 128].astype(F32)
        out = on * gn_ref[:, 128 * h:128 * h + 128] * (g * _sigmoid(g))
        o_ref[:, 128 * h:128 * h + 128] = out.astype(BF16)


def _gla(z, wa_p, ba_p, gn_g, s0, *, ctx):
    if ctx:
        nb, n_blk, tile0 = N_CTX_SEQ, 1, 0
    else:
        nb, n_blk, tile0 = N_LAT_SEQ, LAT_TILES, N_CTX_TILES

    def zspec(width, col):
        return pl.BlockSpec((TM, width), lambda b, n: (tile0 + b * n_blk + n, col))

    w_specs = [pl.BlockSpec((512, 512), lambda b, n: (0, 0)), pl.BlockSpec((1, 512), lambda b, n: (0, 0))]
    kv_shape = jax.ShapeDtypeStruct((nb, n_blk, 2, N_HEADS, DK, DV), F32)
    kv_spec = pl.BlockSpec((1, 1, 2, N_HEADS, DK, DV), lambda b, n: (b, n, 0, 0, 0, 0))
    in_specs = [zspec(256, 6), zspec(256, 7), zspec(512, 4), zspec(512, 5), zspec(512, 12)] + w_specs
    in_specs.append(pl.BlockSpec((1, 512), lambda b, n: (0, 0)))
    args = [z, z, z, z, z, wa_p, ba_p, gn_g]
    out_shape = [jax.ShapeDtypeStruct((nb * n_blk * TM, 512), BF16)]
    out_specs = [pl.BlockSpec((TM, 512), lambda b, n: (b * n_blk + n, 0))]
    if ctx:
        out_shape.append(kv_shape)
        out_specs.append(kv_spec)
    else:
        kv, tot = pl.pallas_call(
            _gla_state_kernel,
            out_shape=[kv_shape, jax.ShapeDtypeStruct((nb, n_blk, 512, 1), F32)],
            grid=(nb, n_blk),
            in_specs=[zspec(256, 7), zspec(512, 4), zspec(512, 12)] + w_specs,
            out_specs=[kv_spec, pl.BlockSpec((1, 1, 512, 1), lambda b, n: (b, n, 0, 0))],
            compiler_params=_params(2),
            name="gla_state_lat",
        )(z, z, z, wa_p, ba_p)
        in_specs += [pl.BlockSpec((1, n_blk, 2, N_HEADS, DK, DV), lambda b, n: (b, 0, 0, 0, 0, 0)),
                     pl.BlockSpec((1, n_blk, 512, 1), lambda b, n: (b, 0, 0, 0)),
                     pl.BlockSpec((1, 2, N_HEADS, DK, DV), lambda b, n: (b, 0, 0, 0, 0))]
        args += [kv, tot, s0]
    res = pl.pallas_call(
        functools.partial(_gla_kernel, n_blk=n_blk, has_state=not ctx, emit_state=ctx),
        out_shape=out_shape, grid=(nb, n_blk), in_specs=in_specs, out_specs=out_specs,
        compiler_params=_params(2),
        name="gla_ctx" if ctx else "gla_lat",
    )(*args)
    return (res[0], res[1]) if ctx else (res[0], None)


def _rope_tables():
    nf = MLA_ROPE // 4
    pos = np.arange(LAT_LEN)
    freqs = (10000.0 ** (-np.arange(nf, dtype=np.float32) / nf)).astype(np.float32)
    ang_r = ((pos // GRID_W).astype(np.float32)[:, None] * freqs).astype(np.float32)
    ang_c = ((pos % GRID_W).astype(np.float32)[:, None] * freqs).astype(np.float32)
    cos = np.ones((TM + LAT_LEN, HEAD_PAD), np.float32)
    sa = np.zeros((TM + LAT_LEN, HEAD_PAD), np.float32)
    sb = np.zeros((TM + LAT_LEN, HEAD_PAD), np.float32)
    o = MLA_NOPE
    for base, ang in ((o, ang_r), (o + 2 * nf, ang_c)):
        cos[TM:, base:base + nf] = np.cos(ang)
        cos[TM:, base + nf:base + 2 * nf] = np.cos(ang)
        sa[TM:, base:base + nf] = -np.sin(ang)
        sb[TM:, base + nf:base + 2 * nf] = np.sin(ang)
    return jnp.asarray(cos), jnp.asarray(sa), jnp.asarray(sb)


def _rope(x, cos, sa, sb):
    return x * cos + pltpu.roll(x, 128 - 8, 1) * sa + pltpu.roll(x, 8, 1) * sb


def _head_segments():
    seg = np.zeros((MLA_HEADS * HEAD_PAD, 128), np.float32)
    for h in range(MLA_HEADS):
        seg[h * HEAD_PAD:(h + 1) * HEAD_PAD, h] = 1.0
    return jnp.asarray(seg, BF16), jnp.asarray(seg.T.copy(), BF16)


def _head_norm(x, gain, seg_ref, segt_ref):
    ss = _dot((x * x).astype(BF16), seg_ref[...])
    rs = lax.rsqrt(ss * (1.0 / MLA_QK) + EPS)
    hi = rs.astype(BF16)
    lo = (rs - hi.astype(F32)).astype(BF16)
    return x * (_dot(hi, segt_ref[...]) + _dot(lo, segt_ref[...])) * gain


def _store_heads(x, o_ref, rope):
    if rope is None:
        o_ref[...] = x.astype(BF16)
    else:
        for h in range(MLA_HEADS):
            o_ref[:, 128 * h:128 * h + 128] = _rope(x[:, 128 * h:128 * h + 128], *rope).astype(BF16)


def _mla_keys(ckv, kr_tile, wk_ref, wv_ref, kn_ref, seg_ref, segt_ref, k_ref, v_ref, rope=None):
    cb = ckv.astype(BF16)
    kpre = _dot(cb, wk_ref[...]) + jnp.concatenate([kr_tile] * MLA_HEADS, axis=1)
    v_ref[...] = _dot(cb, wv_ref[...]).astype(BF16)
    _store_heads(_head_norm(kpre, kn_ref[...], seg_ref, segt_ref), k_ref, rope)


def _mla_prep_kernel(small_ref, qa_ref, wuq_ref, qn_ref, kva_ref, wk_ref, wv_ref, kn_ref, seg_ref, segt_ref,
                     cos_ref, sa_ref, sb_ref, q_ref, k_ref, v_ref, ckv_ref, kr_ref):
    def body(rope):
        sm = small_ref[...].astype(F32)
        cq, ckv_raw, g3 = sm[:, 0:256], sm[:, 256:384], sm[:, 384:512]
        cqn = cq * lax.rsqrt(jnp.mean(cq * cq, axis=-1, keepdims=True) + EPS) * qa_ref[...]
        q = _dot(cqn.astype(BF16), wuq_ref[...])
        _store_heads(_head_norm(q, qn_ref[...], seg_ref, segt_ref), q_ref, rope)
        ckv = ckv_raw * lax.rsqrt(jnp.mean(ckv_raw * ckv_raw, axis=-1, keepdims=True) + EPS) * kva_ref[...]
        ckv_ref[...] = ckv
        lane = lax.broadcasted_iota(jnp.int32, (1, 128), 1)
        kr = jnp.where(lane < MLA_ROPE, g3, 0.0)
        kr_ref[...] = kr
        _mla_keys(ckv, pltpu.roll(kr, MLA_NOPE, 1), wk_ref, wv_ref, kn_ref, seg_ref, segt_ref, k_ref, v_ref, rope)

    is_ctx = pl.program_id(0) < N_CTX_TILES

    @pl.when(is_ctx)
    def _():
        body(None)

    @pl.when(jnp.logical_not(is_ctx))
    def _():
        body((cos_ref[...], sa_ref[...], sb_ref[...]))


def _mla_cache_kernel(ckv_ref, kr_ref, wk_ref, wv_ref, kn_ref, seg_ref, segt_ref, k_ref, v_ref):
    _mla_keys(ckv_ref[...], pltpu.roll(kr_ref[...], MLA_NOPE, 1), wk_ref, wv_ref, kn_ref, seg_ref, segt_ref,
              k_ref, v_ref)


def _mla_prep(z, w, rope_tabs):
    def rope_blk(i):
        return jnp.where(i < N_CTX_TILES, 0, 1 + (i - N_CTX_TILES) % LAT_TILES)

    const = lambda shape: pl.BlockSpec(shape, lambda i: (0,) * len(shape))
    rope_spec = pl.BlockSpec((TM, HEAD_PAD), lambda i: (rope_blk(i), 0))
    row = lambda width: pl.BlockSpec((TM, width), lambda i: (i, 0))
    return pl.pallas_call(
        _mla_prep_kernel,
        out_shape=[jax.ShapeDtypeStruct((T_ALL, 1024), BF16), jax.ShapeDtypeStruct((T_ALL, 1024), BF16),
                   jax.ShapeDtypeStruct((T_ALL, 512), BF16), jax.ShapeDtypeStruct((T_ALL, 128), F32),
                   jax.ShapeDtypeStruct((T_ALL, 128), F32)],
        grid=(N_TILES,),
        in_specs=[pl.BlockSpec((TM, 512), lambda i: (i, 12)),
                  const((1, 256)), const((256, 1024)), const((1, 1024)), const((1, 128)),
                  const((128, 1024)), const((128, 512)), const((1, 1024)),
                  const((1024, 128)), const((128, 1024)),
                  rope_spec, rope_spec, rope_spec],
        out_specs=[row(1024), row(1024), row(512), row(128), row(128)],
        compiler_params=_params(1),
        name="mla_prep",
    )(z, w["qa_g"], w["wuq"], w["qn_g"], w["kva_g"], w["wk"], w["wv"], w["kn_g"], *_head_segments(), *rope_tabs)


def _mla_cache(ckv, kr_pad, w):
    const = lambda shape: pl.BlockSpec(shape, lambda i: (0,) * len(shape))
    row = lambda width: pl.BlockSpec((TM, width), lambda i: (i, 0))
    n = ckv.shape[0]
    return pl.pallas_call(
        _mla_cache_kernel,
        out_shape=[jax.ShapeDtypeStruct((n, 1024), BF16), jax.ShapeDtypeStruct((n, 512), BF16)],
        grid=(n // TM,),
        in_specs=[row(128), row(128), const((128, 1024)), const((128, 512)), const((1, 1024)),
                  const((1024, 128)), const((128, 1024))],
        out_specs=[row(1024), row(512)],
        compiler_params=_params(1),
        name="mla_cache",
    )(ckv, kr_pad, w["wk"], w["wv"], w["kn_g"], *_head_segments())


def _mla_attn_kernel(*refs, has_cache):
    q_ref, k_ref, v_ref = refs[:3]
    pos = 3
    if has_cache:
        kc_ref, vc_ref = refs[3:5]
        pos = 5
    o_ref = refs[pos]
    for p in range(MLA_HEADS // 2):
        acc = jnp.zeros((TM, 128), F32)
        for hh in range(2):
            h = 2 * p + hh
            lanes = slice(128 * h, 128 * h + 128)
            qh = q_ref[:, lanes]
            l1 = _dot_t(qh, k_ref[:, lanes])
            m = jnp.max(l1, axis=-1, keepdims=True)
            if has_cache:
                l0 = _dot_t(qh, kc_ref[:, lanes])
                m = jnp.maximum(m, jnp.max(l0, axis=-1, keepdims=True))
                p0 = jnp.exp(l0 - m)
            p1 = jnp.exp(l1 - m)
            den = jnp.sum(p1, axis=-1, keepdims=True)
            if has_cache:
                den = den + jnp.sum(p0, axis=-1, keepdims=True)
            mask = _lane_half_mask(hh)
            vp = v_ref[:, 128 * p:128 * p + 128]
            o = _dot(p1.astype(BF16), jnp.where(mask, vp, jnp.zeros_like(vp)))
            if has_cache:
                vcp = vc_ref[:, 128 * p:128 * p + 128]
                o += _dot(p0.astype(BF16), jnp.where(mask, vcp, jnp.zeros_like(vcp)))
            acc += o * (1.0 / den)
        o_ref[:, 128 * p:128 * p + 128] = acc.astype(BF16)


def _mla_attn(q, k, v, kc, vc, *, ctx):
    if ctx:
        nb, seq, row_blk, tile0 = N_CTX_SEQ, CTX_LEN, 0, 0
    else:
        nb, seq, row_blk, tile0 = N_LAT_SEQ, LAT_LEN, T_CTX // LAT_LEN, N_CTX_TILES
    nq = seq // TM
    in_specs = [pl.BlockSpec((TM, 1024), lambda b, i: (tile0 + b * nq + i, 0)),
                pl.BlockSpec((seq, 1024), lambda b, i: (row_blk + b, 0)),
                pl.BlockSpec((seq, 512), lambda b, i: (row_blk + b, 0))]
    args = [q, k, v]
    if not ctx:
        in_specs += [pl.BlockSpec((TM, 1024), lambda b, i: (b, 0)), pl.BlockSpec((TM, 512), lambda b, i: (b, 0))]
        args += [kc, vc]
    return pl.pallas_call(
        functools.partial(_mla_attn_kernel, has_cache=not ctx),
        out_shape=jax.ShapeDtypeStruct((nb * seq, 512), BF16),
        grid=(nb, nq), in_specs=in_specs,
        out_specs=pl.BlockSpec((TM, 512), lambda b, i: (b * nq + i, 0)),
        compiler_params=_params(2),
        name="mla_attn_ctx" if ctx else "mla_attn_lat",
    )(*args)


def _merge_kernel(retc_ref, retl_ref, glac_ref, glal_ref, mlac_ref, mlal_ref, m0_ref, m1_ref, m2_ref,
                  xc_ref, xl_ref, wb_ref, wo_ref,
                  g1_ref, n2_ref, sh2_ref, sc2_ref, rwh_ref, rwl_ref, rb_ref,
                  x1_ref, h_ref, idx_ref, w_ref, rank_ref, cnt_ref):
    @pl.when(pl.program_id(0) == 0)
    def _():
        cnt_ref[...] = jnp.zeros_like(cnt_ref)

    is_ctx = pl.program_id(0) < N_CTX_TILES
    mix = None
    for c_ref, l_ref, m_ref, n in ((retc_ref, retl_ref, m0_ref, 0), (glac_ref, glal_ref, m1_ref, 1),
                                   (mlac_ref, mlal_ref, m2_ref, 2)):
        branch = jnp.where(is_ctx, c_ref[...], l_ref[...])
        term = _sigmoid(m_ref[...]).astype(F32) * _dot(branch, wb_ref[n])
        mix = term if mix is None else mix + term
    out = _dot(mix.astype(BF16), wo_ref[...])
    x1 = jnp.where(is_ctx, xc_ref[...], xl_ref[...]) + g1_ref[0] * out
    x1_ref[...] = x1
    h = x1 * lax.rsqrt(jnp.mean(x1 * x1, axis=-1, keepdims=True) + EPS) * n2_ref[...]
    h = h * (1.0 + sc2_ref[0]) + sh2_ref[0]
    h_ref[...] = h
    hh = h.astype(BF16)
    hl = (h - hh.astype(F32)).astype(BF16)
    logits = _dot(hh, rwh_ref[...]) + _dot(hh, rwl_ref[...]) + _dot(hl, rwh_ref[...]) + rb_ref[...]
    rows = TM // ROUTE_CHUNKS
    lane = lax.broadcasted_iota(jnp.int32, (rows, 128), 1)
    lanef = lane.astype(F32)
    onehots, osums = [], []
    for c in range(ROUTE_CHUNKS):
        l = jnp.where(lane < N_EXPERTS, logits[c * rows:(c + 1) * rows], -jnp.inf)
        vals, idxs = [], []
        for _ in range(TOP_K):
            m = jnp.max(l, axis=-1, keepdims=True)
            ix = jnp.min(jnp.where(l == m, lanef, 128.0), axis=-1, keepdims=True)
            vals.append(m)
            idxs.append(ix)
            l = jnp.where(lanef == ix, -jnp.inf, l)
        es = [jnp.exp(v - vals[0]) for v in vals]
        inv = 1.0 / (es[0] + es[1] + es[2] + es[3])
        idx_out = jnp.zeros((rows, 128), F32)
        w_out = jnp.zeros((rows, 128), F32)
        for kk in range(TOP_K):
            idx_out = jnp.where(lane == kk, idxs[kk], idx_out)
            w_out = jnp.where(lane == kk, es[kk] * inv, w_out)
        idx_ref[c * rows:(c + 1) * rows, :] = idx_out.astype(jnp.int32)
        w_ref[c * rows:(c + 1) * rows, :] = w_out
        oh = [jnp.where(lanef == ix, 1.0, 0.0) for ix in idxs]
        onehots.append(oh)
        osums.append((oh[0] + oh[1]) + (oh[2] + oh[3]))

    osum = jnp.concatenate(osums, axis=0)
    ri = lax.broadcasted_iota(jnp.int32, (TM, TM), 0)
    ci = lax.broadcasted_iota(jnp.int32, (TM, TM), 1)
    before = jnp.where(ri > ci, 1.0, 0.0).astype(BF16)
    prior = _dot(before, osum.astype(BF16)) + cnt_ref[0:1, :]
    for c in range(ROUTE_CHUNKS):
        pc = prior[c * rows:(c + 1) * rows]
        rank_out = jnp.zeros((rows, 128), F32)
        for kk in range(TOP_K):
            rank_out = jnp.where(lane == kk, jnp.sum(onehots[c][kk] * pc, axis=-1, keepdims=True), rank_out)
        rank_ref[c * rows:(c + 1) * rows, :] = rank_out.astype(jnp.int32)
    cnt_ref[...] = cnt_ref[...] + jnp.sum(osum, axis=0, keepdims=True)


def _merge(branches, z, xc, xl, modr, w, layer):
    mrow = _mod_row(TM)
    base = layer * N_MOD

    def mod_spec(part):
        return pl.BlockSpec((1, 1, D_MODEL), lambda i: ((base + mrow(i)) * 6 + part, 0, 0))

    const = lambda shape: pl.BlockSpec(shape, lambda i: (0,) * len(shape))
    row = lambda width: pl.BlockSpec((TM, width), lambda i: (i, 0))
    gate = lambda col: pl.BlockSpec((TM, 1024), lambda i: (i, col))
    return pl.pallas_call(
        _merge_kernel,
        out_shape=[jax.ShapeDtypeStruct((T_ALL, D_MODEL), F32), jax.ShapeDtypeStruct((T_ALL, D_MODEL), F32),
                   jax.ShapeDtypeStruct((T_ALL, 128), jnp.int32), jax.ShapeDtypeStruct((T_ALL, 128), F32),
                   jax.ShapeDtypeStruct((T_ALL, 128), jnp.int32), jax.ShapeDtypeStruct((8, 128), F32)],
        grid=(N_TILES,),
        in_specs=[_ctx_spec(TM, 512), _lat_spec(TM, 512)] * 3 + [gate(3), gate(4), gate(5),
                  _ctx_spec(TM, D_MODEL), _lat_spec(TM, D_MODEL),
                  const((3, 512, 1024)), const((1024, 1024)),
                  mod_spec(2), const((1, 1024)), mod_spec(3), mod_spec(4),
                  const((1024, 128)), const((1024, 128)), const((1, 128))],
        out_specs=[row(1024), row(1024), row(128), row(128), row(128), const((8, 128))],
        compiler_params=_params(1),
        name="merge",
    )(*branches, z, z, z, xc, xl, w["wb"], w["wo"], modr, w["n2_g"], modr, modr,
      w["rw_hi"], w["rw_lo"], w["rb"])


def _route(top_idx, rank, counts):
    flat_e = top_idx.reshape(N_SLOTS)
    onehot = (flat_e[:, None] == jnp.arange(N_EXPERTS, dtype=jnp.int32)[None, :]).astype(jnp.int32)
    padded = (counts + MOE_ROWS - 1) // MOE_ROWS * MOE_ROWS
    pad_end = jnp.cumsum(padded)
    pad_start = pad_end - padded
    dest = (rank.reshape(N_SLOTS) + jnp.sum(onehot * pad_start[None, :], axis=1)).astype(jnp.int32)
    blk_start = jnp.arange(N_MOE_BLOCKS, dtype=jnp.int32) * MOE_ROWS
    block_e = jnp.minimum(jnp.sum((pad_end[None, :] <= blk_start[:, None]).astype(jnp.int32), axis=1),
                          N_EXPERTS - 1).astype(jnp.int32)
    n_used = (pad_end[-1] // MOE_ROWS).astype(jnp.int32)
    e_hot = (block_e[:, None] == jnp.arange(N_EXPERTS, dtype=jnp.int32)[None, :]).astype(jnp.int32)
    nxt_blk = jnp.sum(e_hot * pad_end[None, :], axis=1) // MOE_ROWS
    b_hot = (nxt_blk[:, None] == jnp.arange(N_MOE_BLOCKS, dtype=jnp.int32)[None, :]).astype(jnp.int32)
    next_e = jnp.where(nxt_blk < n_used, jnp.sum(b_hot * block_e[None, :], axis=1), block_e).astype(jnp.int32)
    tail_start = (pad_start + counts).astype(jnp.int32)
    return dest, tail_start, block_e, n_used.reshape(1), next_e


def _dispatch_kernel(dest_ref, tail_ref, nb_ref, h_ref, xs_ref, zero_buf, stage, sem, ssem):
    i = pl.program_id(0)

    @pl.when(i == 0)
    def _():
        zero_buf[...] = jnp.zeros_like(zero_buf)
        fills = [pltpu.make_async_copy(
            zero_buf, xs_ref.at[pl.ds(pl.multiple_of((tail_ref[e] // 8) * 8, 8), TAIL_FILL), :], sem)
            for e in range(N_EXPERTS)]
        for f in fills:
            f.start()
        for f in fills:
            f.wait()

        def fill_block(b, c):
            f = pltpu.make_async_copy(zero_buf.at[pl.ds(0, MOE_ROWS), :],
                                      xs_ref.at[pl.ds(pl.multiple_of(b * MOE_ROWS, MOE_ROWS), MOE_ROWS), :], sem)
            f.start()
            f.wait()
            return c
        lax.fori_loop(nb_ref[0], (N_MOE_ROWS + XS_EXTRA) // MOE_ROWS, fill_block, 0)

    base = i * TM * TOP_K
    cur = i % 2

    def wait_tile(slot):
        for kk in range(TOP_K):
            pltpu.make_async_copy(stage.at[slot], xs_ref.at[pl.ds(0, TM), :], ssem.at[slot]).wait()

    @pl.when(i >= 2)
    def _():
        wait_tile(cur)
    stage[cur] = h_ref[...]

    def issue(t, c):
        for kk in range(TOP_K):
            pltpu.make_async_copy(stage.at[cur, pl.ds(t, 1), :],
                                  xs_ref.at[pl.ds(dest_ref[base + t * TOP_K + kk], 1), :], ssem.at[cur]).start()
        return c
    lax.fori_loop(0, TM, issue, 0, unroll=8)

    @pl.when(i == N_TILES - 1)
    def _():
        wait_tile(1 - cur)
        wait_tile(cur)


def _moe_dispatch(h, dest, tail_start, n_used):
    return pl.pallas_call(
        _dispatch_kernel,
        out_shape=jax.ShapeDtypeStruct((N_MOE_ROWS + XS_EXTRA, D_MODEL), F32),
        grid_spec=pltpu.PrefetchScalarGridSpec(
            num_scalar_prefetch=3, grid=(N_TILES,),
            in_specs=[pl.BlockSpec((TM, D_MODEL), lambda i, d, t, nb: (i, 0))],
            out_specs=pl.BlockSpec(memory_space=pl.ANY),
            scratch_shapes=[pltpu.VMEM((TAIL_FILL, D_MODEL), F32), pltpu.VMEM((2, TM, D_MODEL), F32),
                            pltpu.SemaphoreType.DMA, pltpu.SemaphoreType.DMA((2,))]),
        compiler_params=_params(1),
        name="moe_dispatch",
    )(dest, tail_start, n_used, h)


def _expert_kernel(be_ref, nb_ref, nxt_ref, x_ref, wgu_hbm, bgu_ref, wd_hbm, bd_ref, o_ref,
                   wgu_st, wd_st, wgu_bf, wd_bf, sem, *, layer):
    i = pl.program_id(0)
    e = be_ref[i]
    prev = be_ref[jnp.maximum(i - 1, 0)]

    def weight_copies(expert):
        idx = layer * N_EXPERTS + expert
        return (pltpu.make_async_copy(wgu_hbm.at[idx], wgu_st, sem.at[0]),
                pltpu.make_async_copy(wd_hbm.at[idx], wd_st, sem.at[1]))

    @pl.when(i == 0)
    def _():
        for cp in weight_copies(e):
            cp.start()

    @pl.when(((i == 0) | (e != prev)) & (i < nb_ref[0]))
    def _():
        for cp in weight_copies(e):
            cp.wait()
        wgu_bf[...] = wgu_st[...].astype(BF16)
        wd_bf[...] = wd_st[...].astype(BF16)
        nxt = nxt_ref[i]

        @pl.when(nxt != e)
        def _():
            for cp in weight_copies(nxt):
                cp.start()

    @pl.when(i < nb_ref[0])
    def _():
        gu = _dot(x_ref[...].astype(BF16), wgu_bf[...]) + bgu_ref[0]
        gate = jnp.minimum(gu[:, :D_EXPERT], SWIGLU_LIMIT)
        up = jnp.clip(gu[:, D_EXPERT:], -SWIGLU_LIMIT, SWIGLU_LIMIT)
        act = (up + 1.0) * gate * _sigmoid(SWIGLU_ALPHA * gate)
        out = _dot(act.astype(BF16), wd_bf[...]) + bd_ref[0]
        o_ref[...] = out

    @pl.when(i >= nb_ref[0])
    def _():
        o_ref[...] = jnp.zeros_like(o_ref)


def _moe_experts(xs, block_e, n_used, next_e, w_gu, b_gu, w_down, b_down, layer):
    w_idx = lambda i, be, nb, nx: (layer * N_EXPERTS + be[i], 0, 0)
    return pl.pallas_call(
        functools.partial(_expert_kernel, layer=layer),
        out_shape=jax.ShapeDtypeStruct((N_MOE_ROWS, D_MODEL), F32),
        grid_spec=pltpu.PrefetchScalarGridSpec(
            num_scalar_prefetch=3, grid=(N_MOE_BLOCKS,),
            in_specs=[pl.BlockSpec((MOE_ROWS, D_MODEL), lambda i, be, nb, nx: (jnp.minimum(i, nb[0] - 1), 0)),
                      pl.BlockSpec(memory_space=pl.ANY),
                      pl.BlockSpec((1, 1, 2 * D_EXPERT), w_idx),
                      pl.BlockSpec(memory_space=pl.ANY),
                      pl.BlockSpec((1, 1, D_MODEL), w_idx)],
            out_specs=pl.BlockSpec((MOE_ROWS, D_MODEL), lambda i, be, nb, nx: (i, 0)),
            scratch_shapes=[pltpu.VMEM((D_MODEL, 2 * D_EXPERT), F32), pltpu.VMEM((D_EXPERT, D_MODEL), F32),
                            pltpu.VMEM((D_MODEL, 2 * D_EXPERT), BF16), pltpu.VMEM((D_EXPERT, D_MODEL), BF16),
                            pltpu.SemaphoreType.DMA((2,))]),
        compiler_params=_params(1),
        name="moe_experts",
    )(block_e, n_used, next_e, xs, w_gu, b_gu, w_down, b_down)


def _combine_kernel(dest_ref, x_ref, g2_ref, w_ref, eo_ref, yc_ref, yl_ref, buf, sem):
    i = pl.program_id(0)
    cur = i % 2

    def issue_tile(tile, slot):
        base = tile * TM * TOP_K

        def issue(t, c):
            for kk in range(TOP_K):
                pltpu.make_async_copy(eo_ref.at[pl.ds(dest_ref[base + t * TOP_K + kk], 1), :],
                                      buf.at[slot, kk, pl.ds(t, 1), :], sem.at[slot]).start()
            return c
        lax.fori_loop(0, TM, issue, 0, unroll=8)

    @pl.when(i == 0)
    def _():
        issue_tile(0, 0)

    @pl.when(i + 1 < N_TILES)
    def _():
        issue_tile(i + 1, 1 - cur)

    for kk in range(TOP_K):
        pltpu.make_async_copy(eo_ref.at[pl.ds(0, TM), :], buf.at[cur, kk], sem.at[cur]).wait()
    w = w_ref[...]
    ff = None
    for kk in range(TOP_K):
        term = buf[cur, kk] * w[:, kk:kk + 1]
        ff = term if ff is None else ff + term
    y = x_ref[...] + g2_ref[0] * ff
    is_ctx = i < N_CTX_TILES

    @pl.when(is_ctx)
    def _():
        yc_ref[...] = y

    @pl.when(jnp.logical_not(is_ctx))
    def _():
        yl_ref[...] = y


def _moe_combine(dest, x1, modr, top_w, eo, layer):
    mrow = _mod_row(TM)
    base = layer * N_MOD
    return pl.pallas_call(
        _combine_kernel,
        out_shape=[jax.ShapeDtypeStruct((T_CTX, D_MODEL), F32), jax.ShapeDtypeStruct((T_LAT, D_MODEL), F32)],
        grid_spec=pltpu.PrefetchScalarGridSpec(
            num_scalar_prefetch=1, grid=(N_TILES,),
            in_specs=[pl.BlockSpec((TM, D_MODEL), lambda i, d: (i, 0)),
                      pl.BlockSpec((1, 1, D_MODEL), lambda i, d: ((base + mrow(i)) * 6 + 5, 0, 0)),
                      pl.BlockSpec((TM, 128), lambda i, d: (i, 0)),
                      pl.BlockSpec(memory_space=pl.ANY)],
            out_specs=[pl.BlockSpec((TM, D_MODEL), lambda i, d: (jnp.minimum(i, N_CTX_TILES - 1), 0)),
                       pl.BlockSpec((TM, D_MODEL), lambda i, d: (jnp.maximum(i - N_CTX_TILES, 0), 0))],
            scratch_shapes=[pltpu.VMEM((2, TOP_K, TM, D_MODEL), F32), pltpu.SemaphoreType.DMA((2,))]),
        compiler_params=_params(1),
        name="moe_combine",
    )(dest, x1, modr, top_w, eo)


def _pad_heads(w, n_heads, width):
    lead = w.shape[:-1]
    w = w.reshape(lead + (n_heads, width))
    w = jnp.pad(w, [(0, 0)] * len(lead) + [(0, 0), (0, HEAD_PAD - width)])
    return w.reshape(lead + (n_heads * HEAD_PAD,))


def _layer_weights(l, w_in, gla_wa2, gla_ba, mla_qa_g, mla_wuq, mla_kva_g, mla_wukv, mla_qn_g, mla_kn_g,
                   w_branch, w_out, router_w, router_b, norm2_g):
    wi = w_in[l]
    w_in_p = jnp.concatenate([wi[:, :3072], wi[:, 3520:], wi[:, 3104:3520], wi[:, 3072:3104],
                              jnp.zeros((D_MODEL, 64), F32)], axis=1).astype(BF16)
    wa_p = jnp.zeros((512, 512), F32)
    wa_p = wa_p.at[416:432, 0:256].set(gla_wa2[l, 0]).at[432:448, 256:512].set(gla_wa2[l, 1]).astype(BF16)
    ba_p = gla_ba[l].reshape(1, 512)
    wukv = mla_wukv[l].reshape(128, MLA_HEADS, MLA_NOPE + MLA_V)
    rw = jnp.pad(router_w[l], ((0, 0), (0, 128 - N_EXPERTS)))
    rw_hi = rw.astype(BF16)
    return {
        "w_in": w_in_p, "wa": wa_p, "ba": ba_p,
        "qa_g": mla_qa_g[l].reshape(1, 256),
        "wuq": _pad_heads(mla_wuq[l], MLA_HEADS, MLA_QK).astype(BF16),
        "qn_g": jnp.tile(jnp.pad(mla_qn_g[l], (0, HEAD_PAD - MLA_QK)), MLA_HEADS).reshape(1, 1024) * MLA_QK ** -0.5,
        "kn_g": jnp.tile(jnp.pad(mla_kn_g[l], (0, HEAD_PAD - MLA_QK)), MLA_HEADS).reshape(1, 1024),
        "kva_g": mla_kva_g[l].reshape(1, 128),
        "wk": _pad_heads(wukv[:, :, :MLA_NOPE].reshape(128, MLA_HEADS * MLA_NOPE), MLA_HEADS, MLA_NOPE).astype(BF16),
        "wv": wukv[:, :, MLA_NOPE:].reshape(128, MLA_HEADS * MLA_V).astype(BF16),
        "wb": w_branch[l].astype(BF16), "wo": w_out[l].astype(BF16),
        "rw_hi": rw_hi, "rw_lo": (rw - rw_hi.astype(F32)).astype(BF16),
        "rb": jnp.pad(router_b[l], (0, 128 - N_EXPERTS)).reshape(1, 128),
        "n2_g": norm2_g[l].reshape(1, D_MODEL),
    }


def kernel(x_prompt, x_sample, cache_mla_ckv, cache_mla_krope, state_ret, state_gla, c, c_ctx, w_mod, b_mod, norm1_g, norm2_g, w_in, ret_gn_g, gla_wa2, gla_ba, gla_norm_g, mla_qa_g, mla_wuq, mla_kva_g, mla_wukv, mla_qn_g, mla_kn_g, w_branch, w_out, router_w, router_b, moe_w_gu, moe_b_gu, moe_w_down, moe_b_down):
    xc, xl = x_prompt.reshape(T_CTX, D_MODEL), x_sample.reshape(T_LAT, D_MODEL)
    cc = jnp.concatenate([c_ctx[None, :], c, jnp.zeros((N_MOD - 1 - N_LAT_SEQ, D_MODEL), F32)], axis=0)
    modr = _modulation(cc, w_mod, b_mod).reshape(DEPTH * N_MOD * 6, 1, D_MODEL)
    rope_tabs = _rope_tables()
    w_gu = moe_w_gu.reshape(DEPTH * N_EXPERTS, D_MODEL, 2 * D_EXPERT)
    b_gu = moe_b_gu.reshape(DEPTH * N_EXPERTS, 1, 2 * D_EXPERT)
    w_dn = moe_w_down.reshape(DEPTH * N_EXPERTS, D_EXPERT, D_MODEL)
    b_dn = moe_b_down.reshape(DEPTH * N_EXPERTS, 1, D_MODEL)

    ckv_l, krope_l, ret_l, gla_l = [], [], [], []
    for l in range(DEPTH):
        w = _layer_weights(l, w_in, gla_wa2, gla_ba, mla_qa_g, mla_wuq, mla_kva_g, mla_wukv, mla_qn_g, mla_kn_g,
                           w_branch, w_out, router_w, router_b, norm2_g)
        z = _in_proj(xc, xl, norm1_g[l].reshape(1, D_MODEL), modr, w["w_in"], l)

        gn = ret_gn_g[l].reshape(1, 512)
        ret_c, ret_state = _retention(z, gn, None, ctx=True)
        (ret_s,) = _retention(z, gn, state_ret[:, l], ctx=False)
        gng = gla_norm_g[l].reshape(1, 512)
        gla_c, gla_state = _gla(z, w["wa"], w["ba"], gng, None, ctx=True)
        gla_s, _ = _gla(z, w["wa"], w["ba"], gng, state_gla[:, l], ctx=False)

        q, k, v, ckv, kr = _mla_prep(z, w, rope_tabs)
        kc, vc = _mla_cache(cache_mla_ckv[:, l].reshape(N_LAT_SEQ * CTX_LEN, 128),
                            jnp.pad(cache_mla_krope[:, l].reshape(N_LAT_SEQ * CTX_LEN, MLA_ROPE),
                                    ((0, 0), (0, 128 - MLA_ROPE))), w)
        mla_c = _mla_attn(q, k, v, None, None, ctx=True)
        mla_s = _mla_attn(q, k, v, kc, vc, ctx=False)

        x1, h2, top_idx, top_w, rank, cnt = _merge((ret_c, ret_s, gla_c, gla_s, mla_c, mla_s), z, xc, xl, modr, w, l)
        dest, tail_start, block_e, n_used, next_e = _route(top_idx[:, :TOP_K], rank[:, :TOP_K],
                                                           cnt[0, :N_EXPERTS].astype(jnp.int32))
        xs = _moe_dispatch(h2, dest, tail_start, n_used)
        eo = _moe_experts(xs, block_e, n_used, next_e, w_gu, b_gu, w_dn, b_dn, l)
        xc, xl = _moe_combine(dest, x1, modr, top_w, eo, l)

        ckv_l.append(ckv[:T_CTX].reshape(N_CTX_SEQ, CTX_LEN, 128))
        krope_l.append(kr[:T_CTX, :MLA_ROPE].reshape(N_CTX_SEQ, CTX_LEN, MLA_ROPE))
        ret_l.append(ret_state)
        gla_l.append(gla_state.reshape(N_CTX_SEQ, 2, N_HEADS, DK, DV))

    y_p = xc.reshape(N_CTX_SEQ, CTX_LEN, D_MODEL)
    y_s = xl.reshape(N_LAT_SEQ, LAT_LEN, D_MODEL)
    return (y_p, y_s, jnp.stack(ckv_l, axis=1), jnp.stack(krope_l, axis=1),
            jnp.stack(ret_l, axis=1), jnp.stack(gla_l, axis=1))
```

```python
import functools

import jax
import jax.numpy as jnp
import numpy as np
from jax import lax
from jax.experimental import pallas as pl
from jax.experimental.pallas import tpu as pltpu

F32 = jnp.float32
BF16 = jnp.bfloat16

D_MODEL = 1024
DEPTH = 2
N_CTX_SEQ, CTX_LEN = 32, 256
N_LAT_SEQ, LAT_LEN = 4, 1024
T_CTX = N_CTX_SEQ * CTX_LEN
T_LAT = N_LAT_SEQ * LAT_LEN
T_ALL = T_CTX + T_LAT
TM = 256
N_TILES = T_ALL // TM
N_CTX_TILES = T_CTX // TM
LAT_TILES = LAT_LEN // TM
N_MOD = 8
EPS = 1e-6

N_HEADS = 4
DK, DV = 64, 128
GRID_W = 64
MLA_HEADS, MLA_NOPE, MLA_ROPE, MLA_V = 8, 64, 32, 64
MLA_QK = MLA_NOPE + MLA_ROPE
HEAD_PAD = 128
GLA_TAU = 16.0
N_EXPERTS, TOP_K, D_EXPERT = 32, 4, 1024
SWIGLU_LIMIT, SWIGLU_ALPHA = 7.0, 1.702
MOE_ROWS = 512
N_SLOTS = T_ALL * TOP_K
N_MOE_BLOCKS = N_SLOTS // MOE_ROWS + N_EXPERTS
N_MOE_ROWS = N_MOE_BLOCKS * MOE_ROWS
ROUTE_CHUNKS = 4
XS_EXTRA = MOE_ROWS
ROW_BLOCK = (D_MODEL // 128, 8, 128)

DZ = 6656
IN_TILE = 512
VMEM_LIMIT = 56 * 1024 * 1024

RET_LOG_F = [float(np.log1p(-np.exp2(-(5.0 + h)))) for h in range(N_HEADS)]
RET_LOG_B = [float(np.log1p(-np.exp2(-(5.5 + h)))) for h in range(N_HEADS)]


def _params(n_axes, vmem=VMEM_LIMIT):
    return pltpu.CompilerParams(dimension_semantics=("arbitrary",) * n_axes, vmem_limit_bytes=vmem)


def _sigmoid(x):
    return 1.0 / (1.0 + jnp.exp(-x))


def _dot(a, b):
    return jnp.dot(a, b, preferred_element_type=F32)


def _dot_t(a, b):
    return lax.dot_general(a, b, (((1,), (1,)), ((), ())), preferred_element_type=F32)


def _mod_row(tile_rows):
    def f(i):
        r0 = i * tile_rows
        return jnp.where(r0 < T_CTX, 0, 1 + (r0 - T_CTX) // LAT_LEN)
    return f


def _ctx_spec(rows, width):
    n_ctx = T_CTX // rows
    return pl.BlockSpec((rows, width), lambda i: (jnp.minimum(i, n_ctx - 1), 0))


def _lat_spec(rows, width):
    n_ctx = T_CTX // rows
    return pl.BlockSpec((rows, width), lambda i: (jnp.maximum(i - n_ctx, 0), 0))


def _mod_kernel(c_ref, w_ref, b_ref, o_ref):
    c = c_ref[...]
    s = c * _sigmoid(c)
    o_ref[0] = jnp.dot(s, w_ref[0], preferred_element_type=F32, precision=lax.Precision.HIGHEST) + b_ref[0]


def _modulation(cc, w_mod, b_mod):
    n = 6 * D_MODEL
    blk = 2048
    return pl.pallas_call(
        _mod_kernel,
        out_shape=jax.ShapeDtypeStruct((DEPTH, N_MOD, n), F32),
        grid=(DEPTH, n // blk),
        in_specs=[pl.BlockSpec((N_MOD, D_MODEL), lambda l, j: (0, 0)),
                  pl.BlockSpec((1, D_MODEL, blk), lambda l, j: (l, 0, j)),
                  pl.BlockSpec((1, 1, blk), lambda l, j: (l, 0, j))],
        out_specs=pl.BlockSpec((1, N_MOD, blk), lambda l, j: (l, 0, j)),
        compiler_params=_params(2),
        name="modulation",
    )(cc, w_mod, b_mod.reshape(DEPTH, 1, n))


def _in_kernel(xc_ref, xl_ref, g_ref, sh_ref, sc_ref, w_ref, o_ref):
    x = jnp.where(pl.program_id(0) < T_CTX // IN_TILE, xc_ref[...], xl_ref[...])
    h = x * lax.rsqrt(jnp.mean(x * x, axis=-1, keepdims=True) + EPS) * g_ref[...]
    h = h * (1.0 + sc_ref[0]) + sh_ref[0]
    hb = h.astype(BF16)
    for n0 in range(0, DZ, 512):
        o_ref[:, n0:n0 + 512] = _dot(hb, w_ref[:, n0:n0 + 512]).astype(BF16)


def _in_proj(xc, xl, g, modr, w_in_p, layer):
    mrow = _mod_row(IN_TILE)
    base = layer * N_MOD

    def mod_spec(part):
        return pl.BlockSpec((1, 1, D_MODEL), lambda i: ((base + mrow(i)) * 6 + part, 0, 0))

    return pl.pallas_call(
        _in_kernel,
        out_shape=jax.ShapeDtypeStruct((T_ALL, DZ), BF16),
        grid=(T_ALL // IN_TILE,),
        in_specs=[_ctx_spec(IN_TILE, D_MODEL), _lat_spec(IN_TILE, D_MODEL),
                  pl.BlockSpec((1, D_MODEL), lambda i: (0, 0)),
                  mod_spec(0), mod_spec(1),
                  pl.BlockSpec((D_MODEL, DZ), lambda i: (0, 0))],
        out_specs=pl.BlockSpec((IN_TILE, DZ), lambda i: (i, 0)),
        compiler_params=_params(1),
        name="in_proj",
    )(xc, xl, g, modr, modr, w_in_p)


def _lane_half_mask(hh):
    lane = lax.broadcasted_iota(jnp.int32, (1, 128), 1)
    return (lane < 64) if hh == 0 else (lane >= 64)


@functools.lru_cache(maxsize=None)
def _ret_decay_table(seq):
    d = np.arange(seq)[:, None] - np.arange(seq)[None, :]
    tab = np.stack([np.exp(np.where(d > 0, RET_LOG_F[h] * d, -RET_LOG_B[h] * d)) for h in range(N_HEADS)])
    return (tab * np.where(d == 0, 2.0, 1.0) * DK ** -0.5).astype(np.float32)


def _ret_kernel(*refs, seq, has_state, emit_state):
    q_ref, k_ref, v_ref, g_ref, gn_ref, dec_ref = refs[:6]
    pos = 6
    if has_state:
        s0_ref = refs[pos]
        pos += 1
    o_ref = refs[pos]
    pos += 1
    if emit_state:
        st_ref = refs[pos]

    r0 = pl.multiple_of(pl.program_id(1) * TM, TM)
    qb = q_ref[pl.ds(r0, TM), :]
    ri = (lax.broadcasted_iota(jnp.int32, (TM, 1), 0) + r0).astype(F32)

    for h in range(N_HEADS):
        p, hh = h // 2, h % 2
        lanes = slice(128 * p, 128 * p + 128)
        qp = qb[:, lanes]
        qh = jnp.where(_lane_half_mask(hh), qp, jnp.zeros_like(qp))
        sc = _dot_t(qh, k_ref[:, lanes])
        o = _dot((sc * dec_ref[h]).astype(BF16), v_ref[:, 128 * h:128 * h + 128])
        if has_state:
            qf = qh.astype(F32)
            o += _dot((qf * jnp.exp(RET_LOG_F[h] * (ri + 1.0))).astype(BF16), s0_ref[0, 0, p].astype(BF16))
            o += _dot((qf * jnp.exp(RET_LOG_B[h] * (seq - ri))).astype(BF16), s0_ref[0, 1, p].astype(BF16))
        mu = jnp.mean(o, axis=-1, keepdims=True)
        d = o - mu
        var = jnp.mean(d * d, axis=-1, keepdims=True)
        on = d * lax.rsqrt(var + EPS)
        g = g_ref[:, 128 * h:128 * h + 128].astype(F32)
        out = on * gn_ref[:, 128 * h:128 * h + 128] * (g * _sigmoid(g))
        o_ref[:, 128 * h:128 * h + 128] = out.astype(BF16)

    if emit_state:
        jc = lax.broadcasted_iota(jnp.int32, (seq, 1), 0).astype(F32)
        lane = lax.broadcasted_iota(jnp.int32, (1, 128), 1)
        for p in range(2):
            kp = k_ref[:, 128 * p:128 * p + 128].astype(F32) * DK ** -0.5
            lgf = jnp.where(lane < 64, RET_LOG_F[2 * p], RET_LOG_F[2 * p + 1])
            lgb = jnp.where(lane < 64, RET_LOG_B[2 * p], RET_LOG_B[2 * p + 1])
            kdf = (kp * jnp.exp(lgf * (seq - 1.0 - jc))).T.astype(BF16)
            kdb = (kp * jnp.exp(lgb * jc)).T.astype(BF16)
            for hh in range(2):
                h = 2 * p + hh
                vh = v_ref[:, 128 * h:128 * h + 128]
                st_ref[0, 0, h] = _dot(kdf, vh)[64 * hh:64 * hh + 64, :]
                st_ref[0, 1, h] = _dot(kdb, vh)[64 * hh:64 * hh + 64, :]


def _retention(z, gn_g, s0, *, ctx):
    if ctx:
        nb, seq, row_blk, tile0 = N_CTX_SEQ, CTX_LEN, 0, 0
    else:
        nb, seq, row_blk, tile0 = N_LAT_SEQ, LAT_LEN, T_CTX // LAT_LEN, N_CTX_TILES
    nq = seq // TM
    in_specs = [pl.BlockSpec((seq, 256), lambda b, i: (row_blk + b, 0)),
                pl.BlockSpec((seq, 256), lambda b, i: (row_blk + b, 1)),
                pl.BlockSpec((seq, 512), lambda b, i: (row_blk + b, 1)),
                pl.BlockSpec((TM, 512), lambda b, i: (tile0 + b * nq + i, 2)),
                pl.BlockSpec((1, 512), lambda b, i: (0, 0)),
                pl.BlockSpec((N_HEADS, TM, seq), lambda b, i: (0, i, 0))]
    args = [z, z, z, z, gn_g, _ret_decay_table(seq)]
    out_shape = [jax.ShapeDtypeStruct((nb * seq, 512), BF16)]
    out_specs = [pl.BlockSpec((TM, 512), lambda b, i: (b * nq + i, 0))]
    if not ctx:
        in_specs.append(pl.BlockSpec((1, 2, 2, 128, 128), lambda b, i: (b, 0, 0, 0, 0)))
        args.append(s0.reshape(N_LAT_SEQ, 2, 2, 128, 128))
    else:
        out_shape.append(jax.ShapeDtypeStruct((nb, 2, N_HEADS, DK, DV), F32))
        out_specs.append(pl.BlockSpec((1, 2, N_HEADS, DK, DV), lambda b, i: (b, 0, 0, 0, 0)))
    return pl.pallas_call(
        functools.partial(_ret_kernel, seq=seq, has_state=not ctx, emit_state=ctx),
        out_shape=out_shape, grid=(nb, nq), in_specs=in_specs, out_specs=out_specs,
        compiler_params=_params(2),
        name="retention_ctx" if ctx else "retention_lat",
    )(*args)


def _gla_decay(small_ref, wa_ref, ba_ref):
    x = _dot(small_ref[...], wa_ref[...]) + ba_ref[...]
    la = -(jnp.maximum(-x, 0.0) + jnp.log(1.0 + jnp.exp(-jnp.abs(x)))) * (1.0 / GLA_TAU)
    ri = lax.broadcasted_iota(jnp.int32, (TM, TM), 0)
    ci = lax.broadcasted_iota(jnp.int32, (TM, TM), 1)
    ltri = jnp.where(ri >= ci, 1.0, 0.0).astype(BF16)
    hi = la.astype(BF16)
    r1 = la - hi.astype(F32)
    mid = r1.astype(BF16)
    lo = (r1 - mid.astype(F32)).astype(BF16)
    cum = _dot(ltri, hi) + _dot(ltri, mid) + _dot(ltri, lo)
    return la, cum


def _gla_state_kernel(k_ref, v_ref, small_ref, wa_ref, ba_ref, kv_ref, tot_ref):
    la, cum = _gla_decay(small_ref, wa_ref, ba_ref)
    bf, bb = cum[:, :256], cum[:, 256:]
    xb = bb - la[:, 256:]
    k = k_ref[...].astype(F32)
    kdf = k * jnp.exp(bf[TM - 1:TM, :] - bf)
    kdb = k * jnp.exp(xb)
    for p in range(2):
        kf_t = kdf[:, 128 * p:128 * p + 128].T.astype(BF16)
        kb_t = kdb[:, 128 * p:128 * p + 128].T.astype(BF16)
        for hh in range(2):
            h = 2 * p + hh
            vh = v_ref[:, 128 * h:128 * h + 128]
            kv_ref[0, 0, 0, h] = _dot(kf_t, vh)[64 * hh:64 * hh + 64, :]
            kv_ref[0, 0, 1, h] = _dot(kb_t, vh)[64 * hh:64 * hh + 64, :]
    tot_ref[0, 0] = jnp.sum(la.T, axis=-1, keepdims=True)


def _mid_bcast(x, s, r):
    w = 2 * s
    if w >= 8:
        n = TM // w
        x3 = x.reshape(n, w, 256)
        return jnp.broadcast_to(x3[:, r:r + 1, :], (n, w, 256)).reshape(TM, 256)
    x3 = x.reshape(TM // 8, 8, 256)
    sub = lax.broadcasted_iota(jnp.int32, (1, 8, 1), 1)
    out = None
    for blk in range(8 // w):
        rowv = jnp.broadcast_to(x3[:, blk * w + r:blk * w + r + 1, :], (TM // 8, 8, 256))
        out = rowv if out is None else jnp.where(sub >= blk * w, rowv, out)
    return out.reshape(TM, 256)


def _gla_kernel(*refs, n_blk, has_state, emit_state):
    q_ref, k_ref, v_ref, g_ref, small_ref, wa_ref, ba_ref, gn_ref = refs[:8]
    pos = 8
    if has_state:
        kv_ref, tot_ref, s0_ref = refs[pos:pos + 3]
        pos += 3
    o_ref = refs[pos]
    pos += 1
    if emit_state:
        kvo_ref = refs[pos]

    la, cum = _gla_decay(small_ref, wa_ref, ba_ref)
    bf, bb = cum[:, :256], cum[:, 256:]
    xb = bb - la[:, 256:]
    q = q_ref[...].astype(F32) * DK ** -0.5
    k = k_ref[...].astype(F32)
    row = lax.broadcasted_iota(jnp.int32, (TM, 1), 0)
    rowi = lax.broadcasted_iota(jnp.int32, (TM, TM), 0)
    colj = lax.broadcasted_iota(jnp.int32, (TM, TM), 1)
    low_half = _lane_half_mask(0)

    def join(fwd, bwd):
        ops = []
        for p in range(N_HEADS // 2):
            f = fwd[:, 128 * p:128 * p + 128]
            br = pltpu.roll(bwd[:, 128 * p:128 * p + 128], 64, 1)
            ops.append(jnp.where(low_half, f, br).astype(BF16))
            ops.append(jnp.where(low_half, br, f).astype(BF16))
        return ops

    qd, kd = join(q, q), join(k, k)
    acc = [jnp.where(rowi == colj, _dot_t(qo, ko), 0.0) for qo, ko in zip(qd, kd)]

    low_f = jnp.where(low_half, 1.0, 0.0)
    s = 1
    while s < TM:
        up_f = jnp.where(((row // s) % 2) == 1, 1.0, 0.0)
        live_even = jnp.where(up_f == low_f, 1.0, 0.0).astype(BF16)
        live = [live_even, 1.0 - live_even]
        dead = [live[1], live[0]]
        af = -jnp.abs(bf - _mid_bcast(bf, s, s - 1))
        ab = -jnp.abs(xb - _mid_bcast(xb, s, s))
        same = (rowi // (2 * s)) == (colj // (2 * s))
        for p in range(N_HEADS // 2):
            a_f = af[:, 128 * p:128 * p + 128]
            a_b = pltpu.roll(ab[:, 128 * p:128 * p + 128], 64, 1)
            for hh in range(2):
                h = 2 * p + hh
                arg = jnp.where(low_half, a_f, a_b) if hh == 0 else jnp.where(low_half, a_b, a_f)
                e = jnp.exp(arg).astype(BF16)
                sl = _dot_t(e * (qd[h] * live[hh]), e * (kd[h] * dead[hh]))
                acc[h] = acc[h] + (jnp.where(same, sl, 0.0) if 2 * s < TM else sl)
        s *= 2

    if has_state:
        n = pl.program_id(1)
        q_state = join(q * jnp.exp(bf), q * jnp.exp(bb[TM - 1:TM, :] - xb))

    if emit_state:
        kdf = k * jnp.exp(bf[TM - 1:TM, :] - bf)
        kdb = k * jnp.exp(xb)
        for p in range(N_HEADS // 2):
            kf_t = kdf[:, 128 * p:128 * p + 128].T.astype(BF16)
            kb_t = kdb[:, 128 * p:128 * p + 128].T.astype(BF16)
            for hh in range(2):
                h = 2 * p + hh
                vh = v_ref[:, 128 * h:128 * h + 128]
                kvo_ref[0, 0, 0, h] = _dot(kf_t, vh)[64 * hh:64 * hh + 64, :]
                kvo_ref[0, 0, 1, h] = _dot(kb_t, vh)[64 * hh:64 * hh + 64, :]

    for h in range(N_HEADS):
        o = _dot(acc[h].astype(BF16), v_ref[:, 128 * h:128 * h + 128])
        if has_state:
            sf = s0_ref[0, 0, h]
            for m in range(n_blk - 1):
                dec = jnp.exp(tot_ref[0, m, 64 * h:64 * h + 64, :])
                sf = jnp.where(m < n, dec * sf + kv_ref[0, m, 0, h], sf)
            sb = s0_ref[0, 1, h]
            for m in range(n_blk - 1, 0, -1):
                dec = jnp.exp(tot_ref[0, m, 256 + 64 * h:256 + 64 * h + 64, :])
                sb = jnp.where(m > n, dec * sb + kv_ref[0, m, 1, h], sb)
            state = jnp.concatenate([sf, sb] if h % 2 == 0 else [sb, sf], axis=0).astype(BF16)
            o += _dot(q_state[h], state)
        on = o * lax.rsqrt(jnp.mean(o * o, axis=-1, keepdims=True) + EPS)
        g = g_ref[:, 128 * h:128 * h + 128].astype(F32)
        out = on * gn_ref[:, 128 * h:128 * h + 128] * (g * _sigmoid(g))
        o_ref[:, 128 * h:128 * h + 128] = out.astype(BF16)


def _gla(z, wa_p, ba_p, gn_g, s0, *, ctx):
    if ctx:
        nb, n_blk, tile0 = N_CTX_SEQ, 1, 0
    else:
        nb, n_blk, tile0 = N_LAT_SEQ, LAT_TILES, N_CTX_TILES

    def zspec(width, col):
        return pl.BlockSpec((TM, width), lambda b, n: (tile0 + b * n_blk + n, col))

    w_specs = [pl.BlockSpec((512, 512), lambda b, n: (0, 0)), pl.BlockSpec((1, 512), lambda b, n: (0, 0))]
    kv_shape = jax.ShapeDtypeStruct((nb, n_blk, 2, N_HEADS, DK, DV), F32)
    kv_spec = pl.BlockSpec((1, 1, 2, N_HEADS, DK, DV), lambda b, n: (b, n, 0, 0, 0, 0))
    in_specs = [zspec(256, 6), zspec(256, 7), zspec(512, 4), zspec(512, 5), zspec(512, 12)] + w_specs
    in_specs.append(pl.BlockSpec((1, 512), lambda b, n: (0, 0)))
    args = [z, z, z, z, z, wa_p, ba_p, gn_g]
    out_shape = [jax.ShapeDtypeStruct((nb * n_blk * TM, 512), BF16)]
    out_specs = [pl.BlockSpec((TM, 512), lambda b, n: (b * n_blk + n, 0))]
    if ctx:
        out_shape.append(kv_shape)
        out_specs.append(kv_spec)
    else:
        kv, tot = pl.pallas_call(
            _gla_state_kernel,
            out_shape=[kv_shape, jax.ShapeDtypeStruct((nb, n_blk, 512, 1), F32)],
            grid=(nb, n_blk),
            in_specs=[zspec(256, 7), zspec(512, 4), zspec(512, 12)] + w_specs,
            out_specs=[kv_spec, pl.BlockSpec((1, 1, 512, 1), lambda b, n: (b, n, 0, 0))],
            compiler_params=_params(2),
            name="gla_state_lat",
        )(z, z, z, wa_p, ba_p)
        in_specs += [pl.BlockSpec((1, n_blk, 2, N_HEADS, DK, DV), lambda b, n: (b, 0, 0, 0, 0, 0)),
                     pl.BlockSpec((1, n_blk, 512, 1), lambda b, n: (b, 0, 0, 0)),
                     pl.BlockSpec((1, 2, N_HEADS, DK, DV), lambda b, n: (b, 0, 0, 0, 0))]
        args += [kv, tot, s0]
    res = pl.pallas_call(
        functools.partial(_gla_kernel, n_blk=n_blk, has_state=not ctx, emit_state=ctx),
        out_shape=out_shape, grid=(nb, n_blk), in_specs=in_specs, out_specs=out_specs,
        compiler_params=_params(2),
        name="gla_ctx" if ctx else "gla_lat",
    )(*args)
    return (res[0], res[1]) if ctx else (res[0], None)


def _rope_tables():
    nf = MLA_ROPE // 4
    pos = np.arange(LAT_LEN)
    freqs = (10000.0 ** (-np.arange(nf, dtype=np.float32) / nf)).astype(np.float32)
    ang_r = ((pos // GRID_W).astype(np.float32)[:, None] * freqs).astype(np.float32)
    ang_c = ((pos % GRID_W).astype(np.float32)[:, None] * freqs).astype(np.float32)
    cos = np.ones((TM + LAT_LEN, HEAD_PAD), np.float32)
    sa = np.zeros((TM + LAT_LEN, HEAD_PAD), np.float32)
    sb = np.zeros((TM + LAT_LEN, HEAD_PAD), np.float32)
    o = MLA_NOPE
    for base, ang in ((o, ang_r), (o + 2 * nf, ang_c)):
        cos[TM:, base:base + nf] = np.cos(ang)
        cos[TM:, base + nf:base + 2 * nf] = np.cos(ang)
        sa[TM:, base:base + nf] = -np.sin(ang)
        sb[TM:, base + nf:base + 2 * nf] = np.sin(ang)
    return jnp.asarray(cos), jnp.asarray(sa), jnp.asarray(sb)


def _rope(x, cos, sa, sb):
    return x * cos + pltpu.roll(x, 128 - 8, 1) * sa + pltpu.roll(x, 8, 1) * sb


def _head_segments():
    seg = np.zeros((MLA_HEADS * HEAD_PAD, 128), np.float32)
    for h in range(MLA_HEADS):
        seg[h * HEAD_PAD:(h + 1) * HEAD_PAD, h] = 1.0
    return jnp.asarray(seg, BF16), jnp.asarray(seg.T.copy(), BF16)


def _head_norm(x, gain, seg_ref, segt_ref):
    ss = _dot((x * x).astype(BF16), seg_ref[...])
    rs = lax.rsqrt(ss * (1.0 / MLA_QK) + EPS)
    hi = rs.astype(BF16)
    lo = (rs - hi.astype(F32)).astype(BF16)
    return x * (_dot(hi, segt_ref[...]) + _dot(lo, segt_ref[...])) * gain


def _store_heads(x, o_ref, rope):
    if rope is None:
        o_ref[...] = x.astype(BF16)
    else:
        for h in range(MLA_HEADS):
            o_ref[:, 128 * h:128 * h + 128] = _rope(x[:, 128 * h:128 * h + 128], *rope).astype(BF16)


def _mla_keys(ckv, kr_tile, wk_ref, wv_ref, kn_ref, seg_ref, segt_ref, k_ref, v_ref, rope=None):
    cb = ckv.astype(BF16)
    kpre = _dot(cb, wk_ref[...]) + jnp.concatenate([kr_tile] * MLA_HEADS, axis=1)
    v_ref[...] = _dot(cb, wv_ref[...]).astype(BF16)
    _store_heads(_head_norm(kpre, kn_ref[...], seg_ref, segt_ref), k_ref, rope)


def _mla_prep_kernel(small_ref, qa_ref, wuq_ref, qn_ref, kva_ref, wk_ref, wv_ref, kn_ref, seg_ref, segt_ref,
                     cos_ref, sa_ref, sb_ref, q_ref, k_ref, v_ref, ckv_ref, kr_ref):
    def body(rope):
        sm = small_ref[...].astype(F32)
        cq, ckv_raw, g3 = sm[:, 0:256], sm[:, 256:384], sm[:, 384:512]
        cqn = cq * lax.rsqrt(jnp.mean(cq * cq, axis=-1, keepdims=True) + EPS) * qa_ref[...]
        q = _dot(cqn.astype(BF16), wuq_ref[...])
        _store_heads(_head_norm(q, qn_ref[...], seg_ref, segt_ref), q_ref, rope)
        ckv = ckv_raw * lax.rsqrt(jnp.mean(ckv_raw * ckv_raw, axis=-1, keepdims=True) + EPS) * kva_ref[...]
        ckv_ref[...] = ckv
        lane = lax.broadcasted_iota(jnp.int32, (1, 128), 1)
        kr = jnp.where(lane < MLA_ROPE, g3, 0.0)
        kr_ref[...] = kr
        _mla_keys(ckv, pltpu.roll(kr, MLA_NOPE, 1), wk_ref, wv_ref, kn_ref, seg_ref, segt_ref, k_ref, v_ref, rope)

    is_ctx = pl.program_id(0) < N_CTX_TILES

    @pl.when(is_ctx)
    def _():
        body(None)

    @pl.when(jnp.logical_not(is_ctx))
    def _():
        body((cos_ref[...], sa_ref[...], sb_ref[...]))


def _mla_cache_kernel(ckv_ref, kr_ref, wk_ref, wv_ref, kn_ref, seg_ref, segt_ref, k_ref, v_ref):
    _mla_keys(ckv_ref[...], pltpu.roll(kr_ref[...], MLA_NOPE, 1), wk_ref, wv_ref, kn_ref, seg_ref, segt_ref,
              k_ref, v_ref)


def _mla_prep(z, w, rope_tabs):
    def rope_blk(i):
        return jnp.where(i < N_CTX_TILES, 0, 1 + (i - N_CTX_TILES) % LAT_TILES)

    const = lambda shape: pl.BlockSpec(shape, lambda i: (0,) * len(shape))
    rope_spec = pl.BlockSpec((TM, HEAD_PAD), lambda i: (rope_blk(i), 0))
    row = lambda width: pl.BlockSpec((TM, width), lambda i: (i, 0))
    return pl.pallas_call(
        _mla_prep_kernel,
        out_shape=[jax.ShapeDtypeStruct((T_ALL, 1024), BF16), jax.ShapeDtypeStruct((T_ALL, 1024), BF16),
                   jax.ShapeDtypeStruct((T_ALL, 512), BF16), jax.ShapeDtypeStruct((T_ALL, 128), F32),
                   jax.ShapeDtypeStruct((T_ALL, 128), F32)],
        grid=(N_TILES,),
        in_specs=[pl.BlockSpec((TM, 512), lambda i: (i, 12)),
                  const((1, 256)), const((256, 1024)), const((1, 1024)), const((1, 128)),
                  const((128, 1024)), const((128, 512)), const((1, 1024)),
                  const((1024, 128)), const((128, 1024)),
                  rope_spec, rope_spec, rope_spec],
        out_specs=[row(1024), row(1024), row(512), row(128), row(128)],
        compiler_params=_params(1),
        name="mla_prep",
    )(z, w["qa_g"], w["wuq"], w["qn_g"], w["kva_g"], w["wk"], w["wv"], w["kn_g"], *_head_segments(), *rope_tabs)


def _mla_cache(ckv, kr_pad, w):
    const = lambda shape: pl.BlockSpec(shape, lambda i: (0,) * len(shape))
    row = lambda width: pl.BlockSpec((TM, width), lambda i: (i, 0))
    n = ckv.shape[0]
    return pl.pallas_call(
        _mla_cache_kernel,
        out_shape=[jax.ShapeDtypeStruct((n, 1024), BF16), jax.ShapeDtypeStruct((n, 512), BF16)],
        grid=(n // TM,),
        in_specs=[row(128), row(128), const((128, 1024)), const((128, 512)), const((1, 1024)),
                  const((1024, 128)), const((128, 1024))],
        out_specs=[row(1024), row(512)],
        compiler_params=_params(1),
        name="mla_cache",
    )(ckv, kr_pad, w["wk"], w["wv"], w["kn_g"], *_head_segments())


def _mla_attn_kernel(*refs, has_cache):
    q_ref, k_ref, v_ref = refs[:3]
    pos = 3
    if has_cache:
        kc_ref, vc_ref = refs[3:5]
        pos = 5
    o_ref = refs[pos]
    for p in range(MLA_HEADS // 2):
        acc = jnp.zeros((TM, 128), F32)
        for hh in range(2):
            h = 2 * p + hh
            lanes = slice(128 * h, 128 * h + 128)
            qh = q_ref[:, lanes]
            l1 = _dot_t(qh, k_ref[:, lanes])
            m = jnp.max(l1, axis=-1, keepdims=True)
            if has_cache:
                l0 = _dot_t(qh, kc_ref[:, lanes])
                m = jnp.maximum(m, jnp.max(l0, axis=-1, keepdims=True))
                p0 = jnp.exp(l0 - m)
            p1 = jnp.exp(l1 - m)
            den = jnp.sum(p1, axis=-1, keepdims=True)
            if has_cache:
                den = den + jnp.sum(p0, axis=-1, keepdims=True)
            mask = _lane_half_mask(hh)
            vp = v_ref[:, 128 * p:128 * p + 128]
            o = _dot(p1.astype(BF16), jnp.where(mask, vp, jnp.zeros_like(vp)))
            if has_cache:
                vcp = vc_ref[:, 128 * p:128 * p + 128]
                o += _dot(p0.astype(BF16), jnp.where(mask, vcp, jnp.zeros_like(vcp)))
            acc += o * (1.0 / den)
        o_ref[:, 128 * p:128 * p + 128] = acc.astype(BF16)


def _mla_attn(q, k, v, kc, vc, *, ctx):
    if ctx:
        nb, seq, row_blk, tile0 = N_CTX_SEQ, CTX_LEN, 0, 0
    else:
        nb, seq, row_blk, tile0 = N_LAT_SEQ, LAT_LEN, T_CTX // LAT_LEN, N_CTX_TILES
    nq = seq // TM
    in_specs = [pl.BlockSpec((TM, 1024), lambda b, i: (tile0 + b * nq + i, 0)),
                pl.BlockSpec((seq, 1024), lambda b, i: (row_blk + b, 0)),
                pl.BlockSpec((seq, 512), lambda b, i: (row_blk + b, 0))]
    args = [q, k, v]
    if not ctx:
        in_specs += [pl.BlockSpec((TM, 1024), lambda b, i: (b, 0)), pl.BlockSpec((TM, 512), lambda b, i: (b, 0))]
        args += [kc, vc]
    return pl.pallas_call(
        functools.partial(_mla_attn_kernel, has_cache=not ctx),
        out_shape=jax.ShapeDtypeStruct((nb * seq, 512), BF16),
        grid=(nb, nq), in_specs=in_specs,
        out_specs=pl.BlockSpec((TM, 512), lambda b, i: (b * nq + i, 0)),
        compiler_params=_params(2),
        name="mla_attn_ctx" if ctx else "mla_attn_lat",
    )(*args)


def _merge_kernel(retc_ref, retl_ref, glac_ref, glal_ref, mlac_ref, mlal_ref, m0_ref, m1_ref, m2_ref,
                  xc_ref, xl_ref, wb_ref, wo_ref,
                  g1_ref, n2_ref, sh2_ref, sc2_ref, rwh_ref, rwl_ref, rb_ref,
                  x1_ref, h_ref, idx_ref, w_ref, rank_ref, cnt_ref):
    @pl.when(pl.program_id(0) == 0)
    def _():
        cnt_ref[...] = jnp.zeros_like(cnt_ref)

    is_ctx = pl.program_id(0) < N_CTX_TILES
    mix = None
    for c_ref, l_ref, m_ref, n in ((retc_ref, retl_ref, m0_ref, 0), (glac_ref, glal_ref, m1_ref, 1),
                                   (mlac_ref, mlal_ref, m2_ref, 2)):
        branch = jnp.where(is_ctx, c_ref[...], l_ref[...])
        term = _sigmoid(m_ref[...]).astype(F32) * _dot(branch, wb_ref[n])
        mix = term if mix is None else mix + term
    out = _dot(mix.astype(BF16), wo_ref[...])
    x1 = jnp.where(is_ctx, xc_ref[...], xl_ref[...]) + g1_ref[0] * out
    x1_ref[...] = x1
    h = x1 * lax.rsqrt(jnp.mean(x1 * x1, axis=-1, keepdims=True) + EPS) * n2_ref[...]
    h = h * (1.0 + sc2_ref[0]) + sh2_ref[0]
    h_ref[...] = h
    hh = h.astype(BF16)
    hl = (h - hh.astype(F32)).astype(BF16)
    logits = _dot(hh, rwh_ref[...]) + _dot(hh, rwl_ref[...]) + _dot(hl, rwh_ref[...]) + rb_ref[...]
    rows = TM // ROUTE_CHUNKS
    lane = lax.broadcasted_iota(jnp.int32, (rows, 128), 1)
    lanef = lane.astype(F32)
    onehots, osums = [], []
    for c in range(ROUTE_CHUNKS):
        l = jnp.where(lane < N_EXPERTS, logits[c * rows:(c + 1) * rows], -jnp.inf)
        vals, idxs = [], []
        for _ in range(TOP_K):
            m = jnp.max(l, axis=-1, keepdims=True)
            ix = jnp.min(jnp.where(l == m, lanef, 128.0), axis=-1, keepdims=True)
            vals.append(m)
            idxs.append(ix)
            l = jnp.where(lanef == ix, -jnp.inf, l)
        es = [jnp.exp(v - vals[0]) for v in vals]
        inv = 1.0 / (es[0] + es[1] + es[2] + es[3])
        idx_out = jnp.zeros((rows, 128), F32)
        w_out = jnp.zeros((rows, 128), F32)
        for kk in range(TOP_K):
            idx_out = jnp.where(lane == kk, idxs[kk], idx_out)
            w_out = jnp.where(lane == kk, es[kk] * inv, w_out)
        idx_ref[c * rows:(c + 1) * rows, :] = idx_out.astype(jnp.int32)
        w_ref[c * rows:(c + 1) * rows, :] = w_out
        oh = [jnp.where(lanef == ix, 1.0, 0.0) for ix in idxs]
        onehots.append(oh)
        osums.append((oh[0] + oh[1]) + (oh[2] + oh[3]))

    osum = jnp.concatenate(osums, axis=0)
    ri = lax.broadcasted_iota(jnp.int32, (TM, TM), 0)
    ci = lax.broadcasted_iota(jnp.int32, (TM, TM), 1)
    before = jnp.where(ri > ci, 1.0, 0.0).astype(BF16)
    prior = _dot(before, osum.astype(BF16)) + cnt_ref[0:1, :]
    for c in range(ROUTE_CHUNKS):
        pc = prior[c * rows:(c + 1) * rows]
        rank_out = jnp.zeros((rows, 128), F32)
        for kk in range(TOP_K):
            rank_out = jnp.where(lane == kk, jnp.sum(onehots[c][kk] * pc, axis=-1, keepdims=True), rank_out)
        rank_ref[c * rows:(c + 1) * rows, :] = rank_out.astype(jnp.int32)
    cnt_ref[...] = cnt_ref[...] + jnp.sum(osum, axis=0, keepdims=True)


def _merge(branches, z, xc, xl, modr, w, layer):
    mrow = _mod_row(TM)
    base = layer * N_MOD

    def mod_spec(part):
        return pl.BlockSpec((1, 1, D_MODEL), lambda i: ((base + mrow(i)) * 6 + part, 0, 0))

    const = lambda shape: pl.BlockSpec(shape, lambda i: (0,) * len(shape))
    row = lambda width: pl.BlockSpec((TM, width), lambda i: (i, 0))
    gate = lambda col: pl.BlockSpec((TM, 1024), lambda i: (i, col))
    return pl.pallas_call(
        _merge_kernel,
        out_shape=[jax.ShapeDtypeStruct((T_ALL, D_MODEL), F32), jax.ShapeDtypeStruct((T_ALL, D_MODEL), F32),
                   jax.ShapeDtypeStruct((T_ALL, 128), jnp.int32), jax.ShapeDtypeStruct((T_ALL, 128), F32),
                   jax.ShapeDtypeStruct((T_ALL, 128), jnp.int32), jax.ShapeDtypeStruct((8, 128), F32)],
        grid=(N_TILES,),
        in_specs=[_ctx_spec(TM, 512), _lat_spec(TM, 512)] * 3 + [gate(3), gate(4), gate(5),
                  _ctx_spec(TM, D_MODEL), _lat_spec(TM, D_MODEL),
                  const((3, 512, 1024)), const((1024, 1024)),
                  mod_spec(2), const((1, 1024)), mod_spec(3), mod_spec(4),
                  const((1024, 128)), const((1024, 128)), const((1, 128))],
        out_specs=[row(1024), row(1024), row(128), row(128), row(128), const((8, 128))],
        compiler_params=_params(1),
        name="merge",
    )(*branches, z, z, z, xc, xl, w["wb"], w["wo"], modr, w["n2_g"], modr, modr,
      w["rw_hi"], w["rw_lo"], w["rb"])


def _route(top_idx, rank, counts):
    flat_e = top_idx.reshape(N_SLOTS)
    onehot = (flat_e[:, None] == jnp.arange(N_EXPERTS, dtype=jnp.int32)[None, :]).astype(jnp.int32)
    padded = (counts + MOE_ROWS - 1) // MOE_ROWS * MOE_ROWS
    pad_end = jnp.cumsum(padded)
    pad_start = pad_end - padded
    dest = (rank.reshape(N_SLOTS) + jnp.sum(onehot * pad_start[None, :], axis=1)).astype(jnp.int32)
    blk_start = jnp.arange(N_MOE_BLOCKS, dtype=jnp.int32) * MOE_ROWS
    block_e = jnp.minimum(jnp.sum((pad_end[None, :] <= blk_start[:, None]).astype(jnp.int32), axis=1),
                          N_EXPERTS - 1).astype(jnp.int32)
    n_used = (pad_end[-1] // MOE_ROWS).astype(jnp.int32)
    e_hot = (block_e[:, None] == jnp.arange(N_EXPERTS, dtype=jnp.int32)[None, :]).astype(jnp.int32)
    nxt_blk = jnp.sum(e_hot * pad_end[None, :], axis=1) // MOE_ROWS
    b_hot = (nxt_blk[:, None] == jnp.arange(N_MOE_BLOCKS, dtype=jnp.int32)[None, :]).astype(jnp.int32)
    next_e = jnp.where(nxt_blk < n_used, jnp.sum(b_hot * block_e[None, :], axis=1), block_e).astype(jnp.int32)
    tail_start = (pad_start + counts).astype(jnp.int32)
    return dest, tail_start, block_e, n_used.reshape(1), next_e


def _store_rows(x, ref, blk0, rows):
    for g in range(D_MODEL // 128):
        ref[pl.ds(blk0, rows // 8), g] = x[:, 128 * g:128 * (g + 1)].reshape(rows // 8, 8, 128)


def _load_rows(ref, blk0, rows):
    return jnp.concatenate([ref[pl.ds(blk0, rows // 8), g].reshape(rows, 128) for g in range(D_MODEL // 128)],
                           axis=1)


def _row_at(ref, row):
    return ref.at[row >> 3, :, pl.ds(row & 7, 1), :]


def _dispatch_kernel(dest_ref, tail_ref, nb_ref, h_ref, xs_ref, zero_buf, stage, sem, ssem):
    i = pl.program_id(0)

    @pl.when(i == 0)
    def _():
        zero_buf[...] = jnp.zeros_like(zero_buf)
        fills = [pltpu.make_async_copy(zero_buf, xs_ref.at[pl.ds(tail_ref[e] >> 3, MOE_ROWS // 8 + 1)], sem)
                 for e in range(N_EXPERTS)]
        for f in fills:
            f.start()
        for f in fills:
            f.wait()

        def fill_block(b, c):
            f = pltpu.make_async_copy(zero_buf.at[pl.ds(0, MOE_ROWS // 8)],
                                      xs_ref.at[pl.ds(b * (MOE_ROWS // 8), MOE_ROWS // 8)], sem)
            f.start()
            f.wait()
            return c
        lax.fori_loop(nb_ref[0], (N_MOE_ROWS + XS_EXTRA) // MOE_ROWS, fill_block, 0)

    base = i * TM * TOP_K
    cur = i % 2
    blk0 = cur * (TM // 8)

    def wait_tile(slot):
        for kk in range(TOP_K):
            pltpu.make_async_copy(stage.at[pl.ds(slot * (TM // 8), TM // 8)], xs_ref.at[pl.ds(0, TM // 8)],
                                  ssem.at[slot]).wait()

    @pl.when(i >= 2)
    def _():
        wait_tile(cur)
    _store_rows(h_ref[...], stage, blk0, TM)

    def issue(blk, c):
        for u in range(8):
            src = stage.at[blk0 + blk, :, pl.ds(u, 1), :]
            for kk in range(TOP_K):
                dst = _row_at(xs_ref, dest_ref[base + (blk * 8 + u) * TOP_K + kk])
                pltpu.make_async_copy(src, dst, ssem.at[cur]).start(priority=kk % 2)
        return c
    lax.fori_loop(0, TM // 8, issue, 0)

    @pl.when(i == N_TILES - 1)
    def _():
        wait_tile(1 - cur)
        wait_tile(cur)


def _moe_dispatch(h, dest, tail_start, n_used):
    return pl.pallas_call(
        _dispatch_kernel,
        out_shape=jax.ShapeDtypeStruct(((N_MOE_ROWS + XS_EXTRA) // 8,) + ROW_BLOCK, F32),
        grid_spec=pltpu.PrefetchScalarGridSpec(
            num_scalar_prefetch=3, grid=(N_TILES,),
            in_specs=[pl.BlockSpec((TM, D_MODEL), lambda i, d, t, nb: (i, 0))],
            out_specs=pl.BlockSpec(memory_space=pl.ANY),
            scratch_shapes=[pltpu.VMEM((MOE_ROWS // 8 + 1,) + ROW_BLOCK, F32),
                            pltpu.VMEM((2 * TM // 8,) + ROW_BLOCK, F32),
                            pltpu.SemaphoreType.DMA, pltpu.SemaphoreType.DMA((2,))]),
        compiler_params=_params(1),
        name="moe_dispatch",
    )(dest, tail_start, n_used, h)


def _expert_kernel(be_ref, nb_ref, nxt_ref, x_ref, wgu_hbm, bgu_ref, wd_hbm, bd_ref, o_ref,
                   wgu_st, wd_st, wgu_bf, wd_bf, sem, *, layer):
    i = pl.program_id(0)
    e = be_ref[i]
    prev = be_ref[jnp.maximum(i - 1, 0)]

    def weight_copies(expert):
        idx = layer * N_EXPERTS + expert
        return (pltpu.make_async_copy(wgu_hbm.at[idx], wgu_st, sem.at[0]),
                pltpu.make_async_copy(wd_hbm.at[idx], wd_st, sem.at[1]))

    @pl.when(i == 0)
    def _():
        for cp in weight_copies(e):
            cp.start()

    @pl.when(((i == 0) | (e != prev)) & (i < nb_ref[0]))
    def _():
        for cp in weight_copies(e):
            cp.wait()
        wgu_bf[...] = wgu_st[...].astype(BF16)
        wd_bf[...] = wd_st[...].astype(BF16)
        nxt = nxt_ref[i]

        @pl.when(nxt != e)
        def _():
            for cp in weight_copies(nxt):
                cp.start()

    @pl.when(i < nb_ref[0])
    def _():
        gu = _dot(_load_rows(x_ref, 0, MOE_ROWS).astype(BF16), wgu_bf[...]) + bgu_ref[0]
        gate = jnp.minimum(gu[:, :D_EXPERT], SWIGLU_LIMIT)
        up = jnp.clip(gu[:, D_EXPERT:], -SWIGLU_LIMIT, SWIGLU_LIMIT)
        act = (up + 1.0) * gate * _sigmoid(SWIGLU_ALPHA * gate)
        _store_rows(_dot(act.astype(BF16), wd_bf[...]) + bd_ref[0], o_ref, 0, MOE_ROWS)

    @pl.when(i >= nb_ref[0])
    def _():
        o_ref[...] = jnp.zeros_like(o_ref)


def _moe_experts(xs, block_e, n_used, next_e, w_gu, b_gu, w_down, b_down, layer):
    w_idx = lambda i, be, nb, nx: (layer * N_EXPERTS + be[i], 0, 0)
    return pl.pallas_call(
        functools.partial(_expert_kernel, layer=layer),
        out_shape=jax.ShapeDtypeStruct((N_MOE_ROWS // 8,) + ROW_BLOCK, F32),
        grid_spec=pltpu.PrefetchScalarGridSpec(
            num_scalar_prefetch=3, grid=(N_MOE_BLOCKS,),
            in_specs=[pl.BlockSpec((MOE_ROWS // 8,) + ROW_BLOCK,
                                   lambda i, be, nb, nx: (jnp.minimum(i, nb[0] - 1), 0, 0, 0)),
                      pl.BlockSpec(memory_space=pl.ANY),
                      pl.BlockSpec((1, 1, 2 * D_EXPERT), w_idx),
                      pl.BlockSpec(memory_space=pl.ANY),
                      pl.BlockSpec((1, 1, D_MODEL), w_idx)],
            out_specs=pl.BlockSpec((MOE_ROWS // 8,) + ROW_BLOCK, lambda i, be, nb, nx: (i, 0, 0, 0)),
            scratch_shapes=[pltpu.VMEM((D_MODEL, 2 * D_EXPERT), F32), pltpu.VMEM((D_EXPERT, D_MODEL), F32),
                            pltpu.VMEM((D_MODEL, 2 * D_EXPERT), BF16), pltpu.VMEM((D_EXPERT, D_MODEL), BF16),
                            pltpu.SemaphoreType.DMA((2,))]),
        compiler_params=_params(1),
        name="moe_experts",
    )(block_e, n_used, next_e, xs, w_gu, b_gu, w_down, b_down)


def _combine_kernel(dest_ref, x_ref, g2_ref, w_ref, eo_ref, yc_ref, yl_ref, buf, sem):
    i = pl.program_id(0)
    cur = i % 2

    def issue_tile(tile, slot):
        base = tile * TM * TOP_K
        blk0 = slot * (TOP_K * TM // 8)

        def issue(blk, c):
            for u in range(8):
                for kk in range(TOP_K):
                    src = _row_at(eo_ref, dest_ref[base + (blk * 8 + u) * TOP_K + kk])
                    dst = buf.at[blk0 + kk * (TM // 8) + blk, :, pl.ds(u, 1), :]
                    pltpu.make_async_copy(src, dst, sem.at[slot]).start(priority=kk % 2)
            return c
        lax.fori_loop(0, TM // 8, issue, 0)

    @pl.when(i == 0)
    def _():
        issue_tile(0, 0)

    @pl.when(i + 1 < N_TILES)
    def _():
        issue_tile(i + 1, 1 - cur)

    for kk in range(TOP_K):
        pltpu.make_async_copy(eo_ref.at[pl.ds(0, TM // 8)], buf.at[pl.ds(0, TM // 8)], sem.at[cur]).wait()
    w = w_ref[...]
    ff = None
    for kk in range(TOP_K):
        term = _load_rows(buf, cur * (TOP_K * TM // 8) + kk * (TM // 8), TM) * w[:, kk:kk + 1]
        ff = term if ff is None else ff + term
    y = x_ref[...] + g2_ref[0] * ff
    is_ctx = i < N_CTX_TILES

    @pl.when(is_ctx)
    def _():
        yc_ref[...] = y

    @pl.when(jnp.logical_not(is_ctx))
    def _():
        yl_ref[...] = y


def _moe_combine(dest, x1, modr, top_w, eo, layer):
    mrow = _mod_row(TM)
    base = layer * N_MOD
    return pl.pallas_call(
        _combine_kernel,
        out_shape=[jax.ShapeDtypeStruct((T_CTX, D_MODEL), F32), jax.ShapeDtypeStruct((T_LAT, D_MODEL), F32)],
        grid_spec=pltpu.PrefetchScalarGridSpec(
            num_scalar_prefetch=1, grid=(N_TILES,),
            in_specs=[pl.BlockSpec((TM, D_MODEL), lambda i, d: (i, 0)),
                      pl.BlockSpec((1, 1, D_MODEL), lambda i, d: ((base + mrow(i)) * 6 + 5, 0, 0)),
                      pl.BlockSpec((TM, 128), lambda i, d: (i, 0)),
                      pl.BlockSpec(memory_space=pl.ANY)],
            out_specs=[pl.BlockSpec((TM, D_MODEL), lambda i, d: (jnp.minimum(i, N_CTX_TILES - 1), 0)),
                       pl.BlockSpec((TM, D_MODEL), lambda i, d: (jnp.maximum(i - N_CTX_TILES, 0), 0))],
            scratch_shapes=[pltpu.VMEM((2 * TOP_K * TM // 8,) + ROW_BLOCK, F32), pltpu.SemaphoreType.DMA((2,))]),
        compiler_params=_params(1),
        name="moe_combine",
    )(dest, x1, modr, top_w, eo)


def _pad_heads(w, n_heads, width):
    lead = w.shape[:-1]
    w = w.reshape(lead + (n_heads, width))
    w = jnp.pad(w, [(0, 0)] * len(lead) + [(0, 0), (0, HEAD_PAD - width)])
    return w.reshape(lead + (n_heads * HEAD_PAD,))


def _layer_weights(l, w_in, gla_wa2, gla_ba, mla_qa_g, mla_wuq, mla_kva_g, mla_wukv, mla_qn_g, mla_kn_g,
                   w_branch, w_out, router_w, router_b, norm2_g):
    wi = w_in[l]
    w_in_p = jnp.concatenate([wi[:, :3072], wi[:, 3520:], wi[:, 3104:3520], wi[:, 3072:3104],
                              jnp.zeros((D_MODEL, 64), F32)], axis=1).astype(BF16)
    wa_p = jnp.zeros((512, 512), F32)
    wa_p = wa_p.at[416:432, 0:256].set(gla_wa2[l, 0]).at[432:448, 256:512].set(gla_wa2[l, 1]).astype(BF16)
    ba_p = gla_ba[l].reshape(1, 512)
    wukv = mla_wukv[l].reshape(128, MLA_HEADS, MLA_NOPE + MLA_V)
    rw = jnp.pad(router_w[l], ((0, 0), (0, 128 - N_EXPERTS)))
    rw_hi = rw.astype(BF16)
    return {
        "w_in": w_in_p, "wa": wa_p, "ba": ba_p,
        "qa_g": mla_qa_g[l].reshape(1, 256),
        "wuq": _pad_heads(mla_wuq[l], MLA_HEADS, MLA_QK).astype(BF16),
        "qn_g": jnp.tile(jnp.pad(mla_qn_g[l], (0, HEAD_PAD - MLA_QK)), MLA_HEADS).reshape(1, 1024) * MLA_QK ** -0.5,
        "kn_g": jnp.tile(jnp.pad(mla_kn_g[l], (0, HEAD_PAD - MLA_QK)), MLA_HEADS).reshape(1, 1024),
        "kva_g": mla_kva_g[l].reshape(1, 128),
        "wk": _pad_heads(wukv[:, :, :MLA_NOPE].reshape(128, MLA_HEADS * MLA_NOPE), MLA_HEADS, MLA_NOPE).astype(BF16),
        "wv": wukv[:, :, MLA_NOPE:].reshape(128, MLA_HEADS * MLA_V).astype(BF16),
        "wb": w_branch[l].astype(BF16), "wo": w_out[l].astype(BF16),
        "rw_hi": rw_hi, "rw_lo": (rw - rw_hi.astype(F32)).astype(BF16),
        "rb": jnp.pad(router_b[l], (0, 128 - N_EXPERTS)).reshape(1, 128),
        "n2_g": norm2_g[l].reshape(1, D_MODEL),
    }


def kernel(x_prompt, x_sample, cache_mla_ckv, cache_mla_krope, state_ret, state_gla, c, c_ctx, w_mod, b_mod, norm1_g, norm2_g, w_in, ret_gn_g, gla_wa2, gla_ba, gla_norm_g, mla_qa_g, mla_wuq, mla_kva_g, mla_wukv, mla_qn_g, mla_kn_g, w_branch, w_out, router_w, router_b, moe_w_gu, moe_b_gu, moe_w_down, moe_b_down):
    xc, xl = x_prompt.reshape(T_CTX, D_MODEL), x_sample.reshape(T_LAT, D_MODEL)
    cc = jnp.concatenate([c_ctx[None, :], c, jnp.zeros((N_MOD - 1 - N_LAT_SEQ, D_MODEL), F32)], axis=0)
    modr = _modulation(cc, w_mod, b_mod).reshape(DEPTH * N_MOD * 6, 1, D_MODEL)
    rope_tabs = _rope_tables()
    w_gu = moe_w_gu.reshape(DEPTH * N_EXPERTS, D_MODEL, 2 * D_EXPERT)
    b_gu = moe_b_gu.reshape(DEPTH * N_EXPERTS, 1, 2 * D_EXPERT)
    w_dn = moe_w_down.reshape(DEPTH * N_EXPERTS, D_EXPERT, D_MODEL)
    b_dn = moe_b_down.reshape(DEPTH * N_EXPERTS, 1, D_MODEL)

    ckv_l, krope_l, ret_l, gla_l = [], [], [], []
    for l in range(DEPTH):
        w = _layer_weights(l, w_in, gla_wa2, gla_ba, mla_qa_g, mla_wuq, mla_kva_g, mla_wukv, mla_qn_g, mla_kn_g,
                           w_branch, w_out, router_w, router_b, norm2_g)
        z = _in_proj(xc, xl, norm1_g[l].reshape(1, D_MODEL), modr, w["w_in"], l)

        gn = ret_gn_g[l].reshape(1, 512)
        ret_c, ret_state = _retention(z, gn, None, ctx=True)
        (ret_s,) = _retention(z, gn, state_ret[:, l], ctx=False)
        gng = gla_norm_g[l].reshape(1, 512)
        gla_c, gla_state = _gla(z, w["wa"], w["ba"], gng, None, ctx=True)
        gla_s, _ = _gla(z, w["wa"], w["ba"], gng, state_gla[:, l], ctx=False)

        q, k, v, ckv, kr = _mla_prep(z, w, rope_tabs)
        kc, vc = _mla_cache(cache_mla_ckv[:, l].reshape(N_LAT_SEQ * CTX_LEN, 128),
                            jnp.pad(cache_mla_krope[:, l].reshape(N_LAT_SEQ * CTX_LEN, MLA_ROPE),
                                    ((0, 0), (0, 128 - MLA_ROPE))), w)
        mla_c = _mla_attn(q, k, v, None, None, ctx=True)
        mla_s = _mla_attn(q, k, v, kc, vc, ctx=False)

        x1, h2, top_idx, top_w, rank, cnt = _merge((ret_c, ret_s, gla_c, gla_s, mla_c, mla_s), z, xc, xl, modr, w, l)
        dest, tail_start, block_e, n_used, next_e = _route(top_idx[:, :TOP_K], rank[:, :TOP_K],
                                                           cnt[0, :N_EXPERTS].astype(jnp.int32))
        xs = _moe_dispatch(h2, dest, tail_start, n_used)
        eo = _moe_experts(xs, block_e, n_used, next_e, w_gu, b_gu, w_dn, b_dn, l)
        xc, xl = _moe_combine(dest, x1, modr, top_w, eo, l)

        ckv_l.append(ckv[:T_CTX].reshape(N_CTX_SEQ, CTX_LEN, 128))
        krope_l.append(kr[:T_CTX, :MLA_ROPE].reshape(N_CTX_SEQ, CTX_LEN, MLA_ROPE))
        ret_l.append(ret_state)
        gla_l.append(gla_state.reshape(N_CTX_SEQ, 2, N_HEADS, DK, DV))

    y_p = xc.reshape(N_CTX_SEQ, CTX_LEN, D_MODEL)
    y_s = xl.reshape(N_LAT_SEQ, LAT_LEN, D_MODEL)
    return (y_p, y_s, jnp.stack(ckv_l, axis=1), jnp.stack(krope_l, axis=1),
            jnp.stack(ret_l, axis=1), jnp.stack(gla_l, axis=1))
```

```python
import functools

import jax
import jax.numpy as jnp
import numpy as np
from jax import lax
from jax.experimental import pallas as pl
from jax.experimental.pallas import tpu as pltpu

F32 = jnp.float32
BF16 = jnp.bfloat16

D_MODEL = 1024
DEPTH = 2
N_CTX_SEQ, CTX_LEN = 32, 256
N_LAT_SEQ, LAT_LEN = 4, 1024
T_CTX = N_CTX_SEQ * CTX_LEN
T_LAT = N_LAT_SEQ * LAT_LEN
T_ALL = T_CTX + T_LAT
TM = 256
N_TILES = T_ALL // TM
N_CTX_TILES = T_CTX // TM
LAT_TILES = LAT_LEN // TM
N_MOD = 8
EPS = 1e-6

N_HEADS = 4
DK, DV = 64, 128
GRID_W = 64
MLA_HEADS, MLA_NOPE, MLA_ROPE, MLA_V = 8, 64, 32, 64
MLA_QK = MLA_NOPE + MLA_ROPE
HEAD_PAD = 128
GLA_TAU = 16.0
N_EXPERTS, TOP_K, D_EXPERT = 32, 4, 1024
SWIGLU_LIMIT, SWIGLU_ALPHA = 7.0, 1.702
MOE_ROWS = 512
N_SLOTS = T_ALL * TOP_K
N_MOE_BLOCKS = N_SLOTS // MOE_ROWS + N_EXPERTS
N_MOE_ROWS = N_MOE_BLOCKS * MOE_ROWS
MERGE_TM = 512
ROUTE_CHUNKS = 8
XS_EXTRA = MOE_ROWS
ROW_BLOCK = (D_MODEL // 128, 8, 128)

DZ = 6656
IN_TILE = 512
VMEM_LIMIT = 56 * 1024 * 1024

RET_LOG_F = [float(np.log1p(-np.exp2(-(5.0 + h)))) for h in range(N_HEADS)]
RET_LOG_B = [float(np.log1p(-np.exp2(-(5.5 + h)))) for h in range(N_HEADS)]


def _params(n_axes, vmem=VMEM_LIMIT):
    return pltpu.CompilerParams(dimension_semantics=("arbitrary",) * n_axes, vmem_limit_bytes=vmem)


def _sigmoid(x):
    return 1.0 / (1.0 + jnp.exp(-x))


def _dot(a, b):
    return jnp.dot(a, b, preferred_element_type=F32)


def _dot_t(a, b):
    return lax.dot_general(a, b, (((1,), (1,)), ((), ())), preferred_element_type=F32)


def _mod_row(tile_rows):
    def f(i):
        r0 = i * tile_rows
        return jnp.where(r0 < T_CTX, 0, 1 + (r0 - T_CTX) // LAT_LEN)
    return f


def _ctx_spec(rows, width):
    n_ctx = T_CTX // rows
    return pl.BlockSpec((rows, width), lambda i: (jnp.minimum(i, n_ctx - 1), 0))


def _lat_spec(rows, width):
    n_ctx = T_CTX // rows
    return pl.BlockSpec((rows, width), lambda i: (jnp.maximum(i - n_ctx, 0), 0))


def _mod_kernel(c_ref, w_ref, b_ref, o_ref):
    c = c_ref[...]
    s = c * _sigmoid(c)
    o_ref[0] = jnp.dot(s, w_ref[0], preferred_element_type=F32, precision=lax.Precision.HIGHEST) + b_ref[0]


def _modulation(cc, w_mod, b_mod):
    n = 6 * D_MODEL
    blk = 2048
    return pl.pallas_call(
        _mod_kernel,
        out_shape=jax.ShapeDtypeStruct((DEPTH, N_MOD, n), F32),
        grid=(DEPTH, n // blk),
        in_specs=[pl.BlockSpec((N_MOD, D_MODEL), lambda l, j: (0, 0)),
                  pl.BlockSpec((1, D_MODEL, blk), lambda l, j: (l, 0, j)),
                  pl.BlockSpec((1, 1, blk), lambda l, j: (l, 0, j))],
        out_specs=pl.BlockSpec((1, N_MOD, blk), lambda l, j: (l, 0, j)),
        compiler_params=_params(2),
        name="modulation",
    )(cc, w_mod, b_mod.reshape(DEPTH, 1, n))


def _in_kernel(xc_ref, xl_ref, g_ref, sh_ref, sc_ref, w_ref, o_ref):
    x = jnp.where(pl.program_id(0) < T_CTX // IN_TILE, xc_ref[...], xl_ref[...])
    h = x * lax.rsqrt(jnp.mean(x * x, axis=-1, keepdims=True) + EPS) * g_ref[...]
    h = h * (1.0 + sc_ref[0]) + sh_ref[0]
    hb = h.astype(BF16)
    for n0 in range(0, DZ, 512):
        o_ref[:, n0:n0 + 512] = _dot(hb, w_ref[:, n0:n0 + 512]).astype(BF16)


def _in_proj(xc, xl, g, modr, w_in_p, layer):
    mrow = _mod_row(IN_TILE)
    base = layer * N_MOD

    def mod_spec(part):
        return pl.BlockSpec((1, 1, D_MODEL), lambda i: ((base + mrow(i)) * 6 + part, 0, 0))

    return pl.pallas_call(
        _in_kernel,
        out_shape=jax.ShapeDtypeStruct((T_ALL, DZ), BF16),
        grid=(T_ALL // IN_TILE,),
        in_specs=[_ctx_spec(IN_TILE, D_MODEL), _lat_spec(IN_TILE, D_MODEL),
                  pl.BlockSpec((1, D_MODEL), lambda i: (0, 0)),
                  mod_spec(0), mod_spec(1),
                  pl.BlockSpec((D_MODEL, DZ), lambda i: (0, 0))],
        out_specs=pl.BlockSpec((IN_TILE, DZ), lambda i: (i, 0)),
        compiler_params=_params(1),
        name="in_proj",
    )(xc, xl, g, modr, modr, w_in_p)


def _lane_half_mask(hh):
    lane = lax.broadcasted_iota(jnp.int32, (1, 128), 1)
    return (lane < 64) if hh == 0 else (lane >= 64)


@functools.lru_cache(maxsize=None)
def _ret_decay_table(seq):
    d = np.arange(seq)[:, None] - np.arange(seq)[None, :]
    tab = np.stack([np.exp(np.where(d > 0, RET_LOG_F[h] * d, -RET_LOG_B[h] * d)) for h in range(N_HEADS)])
    return (tab * np.where(d == 0, 2.0, 1.0) * DK ** -0.5).astype(np.float32)


def _ret_kernel(*refs, seq, has_state, emit_state):
    q_ref, k_ref, v_ref, g_ref, gn_ref, dec_ref = refs[:6]
    pos = 6
    if has_state:
        s0_ref = refs[pos]
        pos += 1
    o_ref = refs[pos]
    pos += 1
    if emit_state:
        st_ref = refs[pos]

    r0 = pl.multiple_of(pl.program_id(1) * TM, TM)
    qb = q_ref[pl.ds(r0, TM), :]
    ri = (lax.broadcasted_iota(jnp.int32, (TM, 1), 0) + r0).astype(F32)

    for h in range(N_HEADS):
        p, hh = h // 2, h % 2
        lanes = slice(128 * p, 128 * p + 128)
        qp = qb[:, lanes]
        qh = jnp.where(_lane_half_mask(hh), qp, jnp.zeros_like(qp))
        sc = _dot_t(qh, k_ref[:, lanes])
        o = _dot((sc * dec_ref[h]).astype(BF16), v_ref[:, 128 * h:128 * h + 128])
        if has_state:
            qf = qh.astype(F32)
            o += _dot((qf * jnp.exp(RET_LOG_F[h] * (ri + 1.0))).astype(BF16), s0_ref[0, 0, p].astype(BF16))
            o += _dot((qf * jnp.exp(RET_LOG_B[h] * (seq - ri))).astype(BF16), s0_ref[0, 1, p].astype(BF16))
        mu = jnp.mean(o, axis=-1, keepdims=True)
        d = o - mu
        var = jnp.mean(d * d, axis=-1, keepdims=True)
        on = d * lax.rsqrt(var + EPS)
        g = g_ref[:, 128 * h:128 * h + 128].astype(F32)
        out = on * gn_ref[:, 128 * h:128 * h + 128] * (g * _sigmoid(g))
        o_ref[:, 128 * h:128 * h + 128] = out.astype(BF16)

    if emit_state:
        jc = lax.broadcasted_iota(jnp.int32, (seq, 1), 0).astype(F32)
        lane = lax.broadcasted_iota(jnp.int32, (1, 128), 1)
        for p in range(2):
            kp = k_ref[:, 128 * p:128 * p + 128].astype(F32) * DK ** -0.5
            lgf = jnp.where(lane < 64, RET_LOG_F[2 * p], RET_LOG_F[2 * p + 1])
            lgb = jnp.where(lane < 64, RET_LOG_B[2 * p], RET_LOG_B[2 * p + 1])
            kdf = (kp * jnp.exp(lgf * (seq - 1.0 - jc))).T.astype(BF16)
            kdb = (kp * jnp.exp(lgb * jc)).T.astype(BF16)
            for hh in range(2):
                h = 2 * p + hh
                vh = v_ref[:, 128 * h:128 * h + 128]
                st_ref[0, 0, h] = _dot(kdf, vh)[64 * hh:64 * hh + 64, :]
                st_ref[0, 1, h] = _dot(kdb, vh)[64 * hh:64 * hh + 64, :]


def _retention(z, gn_g, s0, *, ctx):
    if ctx:
        nb, seq, row_blk, tile0 = N_CTX_SEQ, CTX_LEN, 0, 0
    else:
        nb, seq, row_blk, tile0 = N_LAT_SEQ, LAT_LEN, T_CTX // LAT_LEN, N_CTX_TILES
    nq = seq // TM
    in_specs = [pl.BlockSpec((seq, 256), lambda b, i: (row_blk + b, 0)),
                pl.BlockSpec((seq, 256), lambda b, i: (row_blk + b, 1)),
                pl.BlockSpec((seq, 512), lambda b, i: (row_blk + b, 1)),
                pl.BlockSpec((TM, 512), lambda b, i: (tile0 + b * nq + i, 2)),
                pl.BlockSpec((1, 512), lambda b, i: (0, 0)),
                pl.BlockSpec((N_HEADS, TM, seq), lambda b, i: (0, i, 0))]
    args = [z, z, z, z, gn_g, _ret_decay_table(seq)]
    out_shape = [jax.ShapeDtypeStruct((nb * seq, 512), BF16)]
    out_specs = [pl.BlockSpec((TM, 512), lambda b, i: (b * nq + i, 0))]
    if not ctx:
        in_specs.append(pl.BlockSpec((1, 2, 2, 128, 128), lambda b, i: (b, 0, 0, 0, 0)))
        args.append(s0.reshape(N_LAT_SEQ, 2, 2, 128, 128))
    else:
        out_shape.append(jax.ShapeDtypeStruct((nb, 2, N_HEADS, DK, DV), F32))
        out_specs.append(pl.BlockSpec((1, 2, N_HEADS, DK, DV), lambda b, i: (b, 0, 0, 0, 0)))
    return pl.pallas_call(
        functools.partial(_ret_kernel, seq=seq, has_state=not ctx, emit_state=ctx),
        out_shape=out_shape, grid=(nb, nq), in_specs=in_specs, out_specs=out_specs,
        compiler_params=_params(2),
        name="retention_ctx" if ctx else "retention_lat",
    )(*args)


def _gla_decay(small_ref, wa_ref, ba_ref):
    x = _dot(small_ref[...], wa_ref[...]) + ba_ref[...]
    la = -(jnp.maximum(-x, 0.0) + jnp.log(1.0 + jnp.exp(-jnp.abs(x)))) * (1.0 / GLA_TAU)
    ri = lax.broadcasted_iota(jnp.int32, (TM, TM), 0)
    ci = lax.broadcasted_iota(jnp.int32, (TM, TM), 1)
    ltri = jnp.where(ri >= ci, 1.0, 0.0).astype(BF16)
    hi = la.astype(BF16)
    r1 = la - hi.astype(F32)
    mid = r1.astype(BF16)
    lo = (r1 - mid.astype(F32)).astype(BF16)
    cum = _dot(ltri, hi) + _dot(ltri, mid) + _dot(ltri, lo)
    return la, cum


def _gla_state_kernel(k_ref, v_ref, small_ref, wa_ref, ba_ref, kv_ref, tot_ref):
    la, cum = _gla_decay(small_ref, wa_ref, ba_ref)
    bf, bb = cum[:, :256], cum[:, 256:]
    xb = bb - la[:, 256:]
    k = k_ref[...].astype(F32)
    kdf = k * jnp.exp(bf[TM - 1:TM, :] - bf)
    kdb = k * jnp.exp(xb)
    for p in range(2):
        kf_t = kdf[:, 128 * p:128 * p + 128].T.astype(BF16)
        kb_t = kdb[:, 128 * p:128 * p + 128].T.astype(BF16)
        for hh in range(2):
            h = 2 * p + hh
            vh = v_ref[:, 128 * h:128 * h + 128]
            kv_ref[0, 0, 0, h] = _dot(kf_t, vh)[64 * hh:64 * hh + 64, :]
            kv_ref[0, 0, 1, h] = _dot(kb_t, vh)[64 * hh:64 * hh + 64, :]
    tot_ref[0, 0] = jnp.sum(la.T, axis=-1, keepdims=True)


def _mid_bcast(x, s, r):
    w = 2 * s
    if w >= 8:
        n = TM // w
        x3 = x.reshape(n, w, 256)
        return jnp.broadcast_to(x3[:, r:r + 1, :], (n, w, 256)).reshape(TM, 256)
    x3 = x.reshape(TM // 8, 8, 256)
    sub = lax.broadcasted_iota(jnp.int32, (1, 8, 1), 1)
    out = None
    for blk in range(8 // w):
        rowv = jnp.broadcast_to(x3[:, blk * w + r:blk * w + r + 1, :], (TM // 8, 8, 256))
        out = rowv if out is None else jnp.where(sub >= blk * w, rowv, out)
    return out.reshape(TM, 256)


def _gla_kernel(*refs, n_blk, has_state, emit_state):
    q_ref, k_ref, v_ref, g_ref, small_ref, wa_ref, ba_ref, gn_ref = refs[:8]
    pos = 8
    if has_state:
        kv_ref, tot_ref, s0_ref = refs[pos:pos + 3]
        pos += 3
    o_ref = refs[pos]
    pos += 1
    if emit_state:
        kvo_ref = refs[pos]

    la, cum = _gla_decay(small_ref, wa_ref, ba_ref)
    bf, bb = cum[:, :256], cum[:, 256:]
    xb = bb - la[:, 256:]
    q = q_ref[...].astype(F32) * DK ** -0.5
    k = k_ref[...].astype(F32)
    row = lax.broadcasted_iota(jnp.int32, (TM, 1), 0)
    rowi = lax.broadcasted_iota(jnp.int32, (TM, TM), 0)
    colj = lax.broadcasted_iota(jnp.int32, (TM, TM), 1)
    low_half = _lane_half_mask(0)

    def join(fwd, bwd):
        ops = []
        for p in range(N_HEADS // 2):
            f = fwd[:, 128 * p:128 * p + 128]
            br = pltpu.roll(bwd[:, 128 * p:128 * p + 128], 64, 1)
            ops.append(jnp.where(low_half, f, br).astype(BF16))
            ops.append(jnp.where(low_half, br, f).astype(BF16))
        return ops

    qd, kd = join(q, q), join(k, k)
    acc = [jnp.where(rowi == colj, _dot_t(qo, ko), 0.0) for qo, ko in zip(qd, kd)]

    low_f = jnp.where(low_half, 1.0, 0.0)
    s = 1
    while s < TM:
        up_f = jnp.where(((row // s) % 2) == 1, 1.0, 0.0)
        live_even = jnp.where(up_f == low_f, 1.0, 0.0).astype(BF16)
        live = [live_even, 1.0 - live_even]
        dead = [live[1], live[0]]
        af = -jnp.abs(bf - _mid_bcast(bf, s, s - 1))
        ab = -jnp.abs(xb - _mid_bcast(xb, s, s))
        same = (rowi // (2 * s)) == (colj // (2 * s))
        for p in range(N_HEADS // 2):
            a_f = af[:, 128 * p:128 * p + 128]
            a_b = pltpu.roll(ab[:, 128 * p:128 * p + 128], 64, 1)
            for hh in range(2):
                h = 2 * p + hh
                arg = jnp.where(low_half, a_f, a_b) if hh == 0 else jnp.where(low_half, a_b, a_f)
                e = jnp.exp(arg).astype(BF16)
                sl = _dot_t(e * (qd[h] * live[hh]), e * (kd[h] * dead[hh]))
                acc[h] = acc[h] + (jnp.where(same, sl, 0.0) if 2 * s < TM else sl)
        s *= 2

    if has_state:
        n = pl.program_id(1)
        q_state = join(q * jnp.exp(bf), q * jnp.exp(bb[TM - 1:TM, :] - xb))

    if emit_state:
        kdf = k * jnp.exp(bf[TM - 1:TM, :] - bf)
        kdb = k * jnp.exp(xb)
        for p in range(N_HEADS // 2):
            kf_t = kdf[:, 128 * p:128 * p + 128].T.astype(BF16)
            kb_t = kdb[:, 128 * p:128 * p + 128].T.astype(BF16)
            for hh in range(2):
                h = 2 * p + hh
                vh = v_ref[:, 128 * h:128 * h + 128]
                kvo_ref[0, 0, 0, h] = _dot(kf_t, vh)[64 * hh:64 * hh + 64, :]
                kvo_ref[0, 0, 1, h] = _dot(kb_t, vh)[64 * hh:64 * hh + 64, :]

    for h in range(N_HEADS):
        o = _dot(acc[h].astype(BF16), v_ref[:, 128 * h:128 * h + 128])
        if has_state:
            sf = s0_ref[0, 0, h]
            for m in range(n_blk - 1):
                dec = jnp.exp(tot_ref[0, m, 64 * h:64 * h + 64, :])
                sf = jnp.where(m < n, dec * sf + kv_ref[0, m, 0, h], sf)
            sb = s0_ref[0, 1, h]
            for m in range(n_blk - 1, 0, -1):
                dec = jnp.exp(tot_ref[0, m, 256 + 64 * h:256 + 64 * h + 64, :])
                sb = jnp.where(m > n, dec * sb + kv_ref[0, m, 1, h], sb)
            state = jnp.concatenate([sf, sb] if h % 2 == 0 else [sb, sf], axis=0).astype(BF16)
            o += _dot(q_state[h], state)
        on = o * lax.rsqrt(jnp.mean(o * o, axis=-1, keepdims=True) + EPS)
        g = g_ref[:, 128 * h:128 * h + 128].astype(F32)
        out = on * gn_ref[:, 128 * h:128 * h + 128] * (g * _sigmoid(g))
        o_ref[:, 128 * h:128 * h + 128] = out.astype(BF16)


def _gla(z, wa_p, ba_p, gn_g, s0, *, ctx):
    if ctx:
        nb, n_blk, tile0 = N_CTX_SEQ, 1, 0
    else:
        nb, n_blk, tile0 = N_LAT_SEQ, LAT_TILES, N_CTX_TILES

    def zspec(width, col):
        return pl.BlockSpec((TM, width), lambda b, n: (tile0 + b * n_blk + n, col))

    w_specs = [pl.BlockSpec((512, 512), lambda b, n: (0, 0)), pl.BlockSpec((1, 512), lambda b, n: (0, 0))]
    kv_shape = jax.ShapeDtypeStruct((nb, n_blk, 2, N_HEADS, DK, DV), F32)
    kv_spec = pl.BlockSpec((1, 1, 2, N_HEADS, DK, DV), lambda b, n: (b, n, 0, 0, 0, 0))
    in_specs = [zspec(256, 6), zspec(256, 7), zspec(512, 4), zspec(512, 5), zspec(512, 12)] + w_specs
    in_specs.append(pl.BlockSpec((1, 512), lambda b, n: (0, 0)))
    args = [z, z, z, z, z, wa_p, ba_p, gn_g]
    out_shape = [jax.ShapeDtypeStruct((nb * n_blk * TM, 512), BF16)]
    out_specs = [pl.BlockSpec((TM, 512), lambda b, n: (b * n_blk + n, 0))]
    if ctx:
        out_shape.append(kv_shape)
        out_specs.append(kv_spec)
    else:
        kv, tot = pl.pallas_call(
            _gla_state_kernel,
            out_shape=[kv_shape, jax.ShapeDtypeStruct((nb, n_blk, 512, 1), F32)],
            grid=(nb, n_blk),
            in_specs=[zspec(256, 7), zspec(512, 4), zspec(512, 12)] + w_specs,
            out_specs=[kv_spec, pl.BlockSpec((1, 1, 512, 1), lambda b, n: (b, n, 0, 0))],
            compiler_params=_params(2),
            name="gla_state_lat",
        )(z, z, z, wa_p, ba_p)
        in_specs += [pl.BlockSpec((1, n_blk, 2, N_HEADS, DK, DV), lambda b, n: (b, 0, 0, 0, 0, 0)),
                     pl.BlockSpec((1, n_blk, 512, 1), lambda b, n: (b, 0, 0, 0)),
                     pl.BlockSpec((1, 2, N_HEADS, DK, DV), lambda b, n: (b, 0, 0, 0, 0))]
        args += [kv, tot, s0]
    res = pl.pallas_call(
        functools.partial(_gla_kernel, n_blk=n_blk, has_state=not ctx, emit_state=ctx),
        out_shape=out_shape, grid=(nb, n_blk), in_specs=in_specs, out_specs=out_specs,
        compiler_params=_params(2),
        name="gla_ctx" if ctx else "gla_lat",
    )(*args)
    return (res[0], res[1]) if ctx else (res[0], None)


def _rope_tables():
    nf = MLA_ROPE // 4
    pos = np.arange(LAT_LEN)
    freqs = (10000.0 ** (-np.arange(nf, dtype=np.float32) / nf)).astype(np.float32)
    ang_r = ((pos // GRID_W).astype(np.float32)[:, None] * freqs).astype(np.float32)
    ang_c = ((pos % GRID_W).astype(np.float32)[:, None] * freqs).astype(np.float32)
    cos = np.ones((TM + LAT_LEN, HEAD_PAD), np.float32)
    sa = np.zeros((TM + LAT_LEN, HEAD_PAD), np.float32)
    sb = np.zeros((TM + LAT_LEN, HEAD_PAD), np.float32)
    o = MLA_NOPE
    for base, ang in ((o, ang_r), (o + 2 * nf, ang_c)):
        cos[TM:, base:base + nf] = np.cos(ang)
        cos[TM:, base + nf:base + 2 * nf] = np.cos(ang)
        sa[TM:, base:base + nf] = -np.sin(ang)
        sb[TM:, base + nf:base + 2 * nf] = np.sin(ang)
    return jnp.asarray(cos), jnp.asarray(sa), jnp.asarray(sb)


def _rope(x, cos, sa, sb):
    return x * cos + pltpu.roll(x, 128 - 8, 1) * sa + pltpu.roll(x, 8, 1) * sb


def _head_segments():
    seg = np.zeros((MLA_HEADS * HEAD_PAD, 128), np.float32)
    for h in range(MLA_HEADS):
        seg[h * HEAD_PAD:(h + 1) * HEAD_PAD, h] = 1.0
    return jnp.asarray(seg, BF16), jnp.asarray(seg.T.copy(), BF16)


def _head_norm(x, gain, seg_ref, segt_ref):
    ss = _dot((x * x).astype(BF16), seg_ref[...])
    rs = lax.rsqrt(ss * (1.0 / MLA_QK) + EPS)
    hi = rs.astype(BF16)
    lo = (rs - hi.astype(F32)).astype(BF16)
    return x * (_dot(hi, segt_ref[...]) + _dot(lo, segt_ref[...])) * gain


def _store_heads(x, o_ref, rope):
    if rope is None:
        o_ref[...] = x.astype(BF16)
    else:
        for h in range(MLA_HEADS):
            o_ref[:, 128 * h:128 * h + 128] = _rope(x[:, 128 * h:128 * h + 128], *rope).astype(BF16)


def _mla_keys(ckv, kr_tile, wk_ref, wv_ref, kn_ref, seg_ref, segt_ref, k_ref, v_ref, rope=None):
    cb = ckv.astype(BF16)
    kpre = _dot(cb, wk_ref[...]) + jnp.concatenate([kr_tile] * MLA_HEADS, axis=1)
    v_ref[...] = _dot(cb, wv_ref[...]).astype(BF16)
    _store_heads(_head_norm(kpre, kn_ref[...], seg_ref, segt_ref), k_ref, rope)


def _mla_prep_kernel(small_ref, qa_ref, wuq_ref, qn_ref, kva_ref, wk_ref, wv_ref, kn_ref, seg_ref, segt_ref,
                     cos_ref, sa_ref, sb_ref, q_ref, k_ref, v_ref, ckv_ref, kr_ref):
    def body(rope):
        sm = small_ref[...].astype(F32)
        cq, ckv_raw, g3 = sm[:, 0:256], sm[:, 256:384], sm[:, 384:512]
        cqn = cq * lax.rsqrt(jnp.mean(cq * cq, axis=-1, keepdims=True) + EPS) * qa_ref[...]
        q = _dot(cqn.astype(BF16), wuq_ref[...])
        _store_heads(_head_norm(q, qn_ref[...], seg_ref, segt_ref), q_ref, rope)
        ckv = ckv_raw * lax.rsqrt(jnp.mean(ckv_raw * ckv_raw, axis=-1, keepdims=True) + EPS) * kva_ref[...]
        ckv_ref[...] = ckv
        lane = lax.broadcasted_iota(jnp.int32, (1, 128), 1)
        kr = jnp.where(lane < MLA_ROPE, g3, 0.0)
        kr_ref[...] = kr
        _mla_keys(ckv, pltpu.roll(kr, MLA_NOPE, 1), wk_ref, wv_ref, kn_ref, seg_ref, segt_ref, k_ref, v_ref, rope)

    is_ctx = pl.program_id(0) < N_CTX_TILES

    @pl.when(is_ctx)
    def _():
        body(None)

    @pl.when(jnp.logical_not(is_ctx))
    def _():
        body((cos_ref[...], sa_ref[...], sb_ref[...]))


def _mla_cache_kernel(ckv_ref, kr_ref, wk_ref, wv_ref, kn_ref, seg_ref, segt_ref, k_ref, v_ref):
    _mla_keys(ckv_ref[...], pltpu.roll(kr_ref[...], MLA_NOPE, 1), wk_ref, wv_ref, kn_ref, seg_ref, segt_ref,
              k_ref, v_ref)


def _mla_prep(z, w, rope_tabs):
    def rope_blk(i):
        return jnp.where(i < N_CTX_TILES, 0, 1 + (i - N_CTX_TILES) % LAT_TILES)

    const = lambda shape: pl.BlockSpec(shape, lambda i: (0,) * len(shape))
    rope_spec = pl.BlockSpec((TM, HEAD_PAD), lambda i: (rope_blk(i), 0))
    row = lambda width: pl.BlockSpec((TM, width), lambda i: (i, 0))
    return pl.pallas_call(
        _mla_prep_kernel,
        out_shape=[jax.ShapeDtypeStruct((T_ALL, 1024), BF16), jax.ShapeDtypeStruct((T_ALL, 1024), BF16),
                   jax.ShapeDtypeStruct((T_ALL, 512), BF16), jax.ShapeDtypeStruct((T_ALL, 128), F32),
                   jax.ShapeDtypeStruct((T_ALL, 128), F32)],
        grid=(N_TILES,),
        in_specs=[pl.BlockSpec((TM, 512), lambda i: (i, 12)),
                  const((1, 256)), const((256, 1024)), const((1, 1024)), const((1, 128)),
                  const((128, 1024)), const((128, 512)), const((1, 1024)),
                  const((1024, 128)), const((128, 1024)),
                  rope_spec, rope_spec, rope_spec],
        out_specs=[row(1024), row(1024), row(512), row(128), row(128)],
        compiler_params=_params(1),
        name="mla_prep",
    )(z, w["qa_g"], w["wuq"], w["qn_g"], w["kva_g"], w["wk"], w["wv"], w["kn_g"], *_head_segments(), *rope_tabs)


def _mla_cache(ckv, kr_pad, w):
    const = lambda shape: pl.BlockSpec(shape, lambda i: (0,) * len(shape))
    row = lambda width: pl.BlockSpec((TM, width), lambda i: (i, 0))
    n = ckv.shape[0]
    return pl.pallas_call(
        _mla_cache_kernel,
        out_shape=[jax.ShapeDtypeStruct((n, 1024), BF16), jax.ShapeDtypeStruct((n, 512), BF16)],
        grid=(n // TM,),
        in_specs=[row(128), row(128), const((128, 1024)), const((128, 512)), const((1, 1024)),
                  const((1024, 128)), const((128, 1024))],
        out_specs=[row(1024), row(512)],
        compiler_params=_params(1),
        name="mla_cache",
    )(ckv, kr_pad, w["wk"], w["wv"], w["kn_g"], *_head_segments())


def _mla_attn_kernel(*refs, has_cache):
    q_ref, k_ref, v_ref = refs[:3]
    pos = 3
    if has_cache:
        kc_ref, vc_ref = refs[3:5]
        pos = 5
    o_ref = refs[pos]
    for p in range(MLA_HEADS // 2):
        acc = jnp.zeros((TM, 128), F32)
        for hh in range(2):
            h = 2 * p + hh
            lanes = slice(128 * h, 128 * h + 128)
            qh = q_ref[:, lanes]
            l1 = _dot_t(qh, k_ref[:, lanes])
            m = jnp.max(l1, axis=-1, keepdims=True)
            if has_cache:
                l0 = _dot_t(qh, kc_ref[:, lanes])
                m = jnp.maximum(m, jnp.max(l0, axis=-1, keepdims=True))
                p0 = jnp.exp(l0 - m)
            p1 = jnp.exp(l1 - m)
            den = jnp.sum(p1, axis=-1, keepdims=True)
            if has_cache:
                den = den + jnp.sum(p0, axis=-1, keepdims=True)
            mask = _lane_half_mask(hh)
            vp = v_ref[:, 128 * p:128 * p + 128]
            o = _dot(p1.astype(BF16), jnp.where(mask, vp, jnp.zeros_like(vp)))
            if has_cache:
                vcp = vc_ref[:, 128 * p:128 * p + 128]
                o += _dot(p0.astype(BF16), jnp.where(mask, vcp, jnp.zeros_like(vcp)))
            acc += o * (1.0 / den)
        o_ref[:, 128 * p:128 * p + 128] = acc.astype(BF16)


def _mla_attn(q, k, v, kc, vc, *, ctx):
    if ctx:
        nb, seq, row_blk, tile0 = N_CTX_SEQ, CTX_LEN, 0, 0
    else:
        nb, seq, row_blk, tile0 = N_LAT_SEQ, LAT_LEN, T_CTX // LAT_LEN, N_CTX_TILES
    nq = seq // TM
    in_specs = [pl.BlockSpec((TM, 1024), lambda b, i: (tile0 + b * nq + i, 0)),
                pl.BlockSpec((seq, 1024), lambda b, i: (row_blk + b, 0)),
                pl.BlockSpec((seq, 512), lambda b, i: (row_blk + b, 0))]
    args = [q, k, v]
    if not ctx:
        in_specs += [pl.BlockSpec((TM, 1024), lambda b, i: (b, 0)), pl.BlockSpec((TM, 512), lambda b, i: (b, 0))]
        args += [kc, vc]
    return pl.pallas_call(
        functools.partial(_mla_attn_kernel, has_cache=not ctx),
        out_shape=jax.ShapeDtypeStruct((nb * seq, 512), BF16),
        grid=(nb, nq), in_specs=in_specs,
        out_specs=pl.BlockSpec((TM, 512), lambda b, i: (b * nq + i, 0)),
        compiler_params=_params(2),
        name="mla_attn_ctx" if ctx else "mla_attn_lat",
    )(*args)


def _merge_kernel(retc_ref, retl_ref, glac_ref, glal_ref, mlac_ref, mlal_ref, m0_ref, m1_ref, m2_ref,
                  xc_ref, xl_ref, wb_ref, wo_ref,
                  g1_ref, n2_ref, sh2_ref, sc2_ref, rwh_ref, rwl_ref, rb_ref,
                  x1_ref, h_ref, idx_ref, w_ref, rank_ref, cnt_ref):
    @pl.when(pl.program_id(0) == 0)
    def _():
        cnt_ref[...] = jnp.zeros_like(cnt_ref)

    is_ctx = pl.program_id(0) < T_CTX // MERGE_TM
    mix = None
    for c_ref, l_ref, m_ref, n in ((retc_ref, retl_ref, m0_ref, 0), (glac_ref, glal_ref, m1_ref, 1),
                                   (mlac_ref, mlal_ref, m2_ref, 2)):
        branch = jnp.where(is_ctx, c_ref[...], l_ref[...])
        term = _sigmoid(m_ref[...]).astype(F32) * _dot(branch, wb_ref[n])
        mix = term if mix is None else mix + term
    out = _dot(mix.astype(BF16), wo_ref[...])
    x1 = jnp.where(is_ctx, xc_ref[...], xl_ref[...]) + g1_ref[0] * out
    x1_ref[...] = x1
    h = x1 * lax.rsqrt(jnp.mean(x1 * x1, axis=-1, keepdims=True) + EPS) * n2_ref[...]
    h = h * (1.0 + sc2_ref[0]) + sh2_ref[0]
    h_ref[...] = h
    hh = h.astype(BF16)
    hl = (h - hh.astype(F32)).astype(BF16)
    logits = _dot(hh, rwh_ref[...]) + _dot(hh, rwl_ref[...]) + _dot(hl, rwh_ref[...]) + rb_ref[...]
    rows = MERGE_TM // ROUTE_CHUNKS
    lane = lax.broadcasted_iota(jnp.int32, (rows, 128), 1)
    lanef = lane.astype(F32)
    onehots, osums = [], []
    for c in range(ROUTE_CHUNKS):
        l = jnp.where(lane < N_EXPERTS, logits[c * rows:(c + 1) * rows], -jnp.inf)
        vals, idxs = [], []
        for _ in range(TOP_K):
            m = jnp.max(l, axis=-1, keepdims=True)
            ix = jnp.min(jnp.where(l == m, lanef, 128.0), axis=-1, keepdims=True)
            vals.append(m)
            idxs.append(ix)
            l = jnp.where(lanef == ix, -jnp.inf, l)
        es = [jnp.exp(v - vals[0]) for v in vals]
        inv = 1.0 / (es[0] + es[1] + es[2] + es[3])
        idx_out = jnp.zeros((rows, 128), F32)
        w_out = jnp.zeros((rows, 128), F32)
        for kk in range(TOP_K):
            idx_out = jnp.where(lane == kk, idxs[kk], idx_out)
            w_out = jnp.where(lane == kk, es[kk] * inv, w_out)
        idx_ref[c * rows:(c + 1) * rows, :] = idx_out.astype(jnp.int32)
        w_ref[c * rows:(c + 1) * rows, :] = w_out
        oh = [jnp.where(lanef == ix, 1.0, 0.0) for ix in idxs]
        onehots.append(oh)
        osums.append((oh[0] + oh[1]) + (oh[2] + oh[3]))

    osum = jnp.concatenate(osums, axis=0)
    ri = lax.broadcasted_iota(jnp.int32, (MERGE_TM, MERGE_TM), 0)
    ci = lax.broadcasted_iota(jnp.int32, (MERGE_TM, MERGE_TM), 1)
    before = jnp.where(ri > ci, 1.0, 0.0).astype(BF16)
    prior = _dot(before, osum.astype(BF16)) + cnt_ref[0:1, :]
    for c in range(ROUTE_CHUNKS):
        pc = prior[c * rows:(c + 1) * rows]
        rank_out = jnp.zeros((rows, 128), F32)
        for kk in range(TOP_K):
            rank_out = jnp.where(lane == kk, jnp.sum(onehots[c][kk] * pc, axis=-1, keepdims=True), rank_out)
        rank_ref[c * rows:(c + 1) * rows, :] = rank_out.astype(jnp.int32)
    cnt_ref[...] = cnt_ref[...] + jnp.sum(osum, axis=0, keepdims=True)


def _merge(branches, z, xc, xl, modr, w, layer):
    mrow = _mod_row(MERGE_TM)
    base = layer * N_MOD

    def mod_spec(part):
        return pl.BlockSpec((1, 1, D_MODEL), lambda i: ((base + mrow(i)) * 6 + part, 0, 0))

    const = lambda shape: pl.BlockSpec(shape, lambda i: (0,) * len(shape))
    row = lambda width: pl.BlockSpec((MERGE_TM, width), lambda i: (i, 0))
    gate = lambda col: pl.BlockSpec((MERGE_TM, 1024), lambda i: (i, col))
    return pl.pallas_call(
        _merge_kernel,
        out_shape=[jax.ShapeDtypeStruct((T_ALL, D_MODEL), F32), jax.ShapeDtypeStruct((T_ALL, D_MODEL), F32),
                   jax.ShapeDtypeStruct((T_ALL, 128), jnp.int32), jax.ShapeDtypeStruct((T_ALL, 128), F32),
                   jax.ShapeDtypeStruct((T_ALL, 128), jnp.int32), jax.ShapeDtypeStruct((8, 128), F32)],
        grid=(T_ALL // MERGE_TM,),
        in_specs=[_ctx_spec(MERGE_TM, 512), _lat_spec(MERGE_TM, 512)] * 3 + [gate(3), gate(4), gate(5),
                  _ctx_spec(MERGE_TM, D_MODEL), _lat_spec(MERGE_TM, D_MODEL),
                  const((3, 512, 1024)), const((1024, 1024)),
                  mod_spec(2), const((1, 1024)), mod_spec(3), mod_spec(4),
                  const((1024, 128)), const((1024, 128)), const((1, 128))],
        out_specs=[row(1024), row(1024), row(128), row(128), row(128), const((8, 128))],
        compiler_params=_params(1),
        name="merge",
    )(*branches, z, z, z, xc, xl, w["wb"], w["wo"], modr, w["n2_g"], modr, modr,
      w["rw_hi"], w["rw_lo"], w["rb"])


def _route(top_idx, rank, counts):
    flat_e = top_idx.reshape(N_SLOTS)
    onehot = (flat_e[:, None] == jnp.arange(N_EXPERTS, dtype=jnp.int32)[None, :]).astype(jnp.int32)
    padded = (counts + MOE_ROWS - 1) // MOE_ROWS * MOE_ROWS
    pad_end = jnp.cumsum(padded)
    pad_start = pad_end - padded
    dest = (rank.reshape(N_SLOTS) + jnp.sum(onehot * pad_start[None, :], axis=1)).astype(jnp.int32)
    blk_start = jnp.arange(N_MOE_BLOCKS, dtype=jnp.int32) * MOE_ROWS
    block_e = jnp.minimum(jnp.sum((pad_end[None, :] <= blk_start[:, None]).astype(jnp.int32), axis=1),
                          N_EXPERTS - 1).astype(jnp.int32)
    n_used = (pad_end[-1] // MOE_ROWS).astype(jnp.int32)
    e_hot = (block_e[:, None] == jnp.arange(N_EXPERTS, dtype=jnp.int32)[None, :]).astype(jnp.int32)
    nxt_blk = jnp.sum(e_hot * pad_end[None, :], axis=1) // MOE_ROWS
    b_hot = (nxt_blk[:, None] == jnp.arange(N_MOE_BLOCKS, dtype=jnp.int32)[None, :]).astype(jnp.int32)
    next_e = jnp.where(nxt_blk < n_used, jnp.sum(b_hot * block_e[None, :], axis=1), block_e).astype(jnp.int32)
    tail_start = (pad_start + counts).astype(jnp.int32)
    return dest, tail_start, block_e, n_used.reshape(1), next_e


def _store_rows(x, ref, blk0, rows):
    for g in range(D_MODEL // 128):
        ref[pl.ds(blk0, rows // 8), g] = x[:, 128 * g:128 * (g + 1)].reshape(rows // 8, 8, 128)


def _load_rows(ref, blk0, rows):
    return jnp.concatenate([ref[pl.ds(blk0, rows // 8), g].reshape(rows, 128) for g in range(D_MODEL // 128)],
                           axis=1)


def _row_at(ref, row):
    return ref.at[row >> 3, :, pl.ds(row & 7, 1), :]


def _dispatch_kernel(dest_ref, tail_ref, nb_ref, h_ref, xs_ref, zero_buf, stage, sem, ssem):
    i = pl.program_id(0)

    @pl.when(i == 0)
    def _():
        zero_buf[...] = jnp.zeros_like(zero_buf)
        fills = [pltpu.make_async_copy(zero_buf, xs_ref.at[pl.ds(tail_ref[e] >> 3, MOE_ROWS // 8 + 1)], sem)
                 for e in range(N_EXPERTS)]
        for f in fills:
            f.start()
        for f in fills:
            f.wait()

        def fill_block(b, c):
            f = pltpu.make_async_copy(zero_buf.at[pl.ds(0, MOE_ROWS // 8)],
                                      xs_ref.at[pl.ds(b * (MOE_ROWS // 8), MOE_ROWS // 8)], sem)
            f.start()
            f.wait()
            return c
        lax.fori_loop(nb_ref[0], (N_MOE_ROWS + XS_EXTRA) // MOE_ROWS, fill_block, 0)

    base = i * TM * TOP_K
    cur = i % 2
    blk0 = cur * (TM // 8)

    def wait_tile(slot):
        for kk in range(TOP_K):
            pltpu.make_async_copy(stage.at[pl.ds(slot * (TM // 8), TM // 8)], xs_ref.at[pl.ds(0, TM // 8)],
                                  ssem.at[slot]).wait()

    @pl.when(i >= 2)
    def _():
        wait_tile(cur)
    _store_rows(h_ref[...], stage, blk0, TM)

    def issue(blk, c):
        for u in range(8):
            src = stage.at[blk0 + blk, :, pl.ds(u, 1), :]
            for kk in range(TOP_K):
                dst = _row_at(xs_ref, dest_ref[base + (blk * 8 + u) * TOP_K + kk])
                pltpu.make_async_copy(src, dst, ssem.at[cur]).start(priority=kk % 2)
        return c
    lax.fori_loop(0, TM // 8, issue, 0)

    @pl.when(i == N_TILES - 1)
    def _():
        wait_tile(1 - cur)
        wait_tile(cur)


def _moe_dispatch(h, dest, tail_start, n_used):
    return pl.pallas_call(
        _dispatch_kernel,
        out_shape=jax.ShapeDtypeStruct(((N_MOE_ROWS + XS_EXTRA) // 8,) + ROW_BLOCK, F32),
        grid_spec=pltpu.PrefetchScalarGridSpec(
            num_scalar_prefetch=3, grid=(N_TILES,),
            in_specs=[pl.BlockSpec((TM, D_MODEL), lambda i, d, t, nb: (i, 0))],
            out_specs=pl.BlockSpec(memory_space=pl.ANY),
            scratch_shapes=[pltpu.VMEM((MOE_ROWS // 8 + 1,) + ROW_BLOCK, F32),
                            pltpu.VMEM((2 * TM // 8,) + ROW_BLOCK, F32),
                            pltpu.SemaphoreType.DMA, pltpu.SemaphoreType.DMA((2,))]),
        compiler_params=_params(1),
        name="moe_dispatch",
    )(dest, tail_start, n_used, h)


def _expert_kernel(be_ref, nb_ref, nxt_ref, x_ref, wgu_hbm, bgu_ref, wd_hbm, bd_ref, o_ref,
                   wgu_st, wd_st, wgu_bf, wd_bf, sem, *, layer):
    i = pl.program_id(0)
    e = be_ref[i]
    prev = be_ref[jnp.maximum(i - 1, 0)]

    def weight_copies(expert):
        idx = layer * N_EXPERTS + expert
        return (pltpu.make_async_copy(wgu_hbm.at[idx], wgu_st, sem.at[0]),
                pltpu.make_async_copy(wd_hbm.at[idx], wd_st, sem.at[1]))

    @pl.when(i == 0)
    def _():
        for cp in weight_copies(e):
            cp.start()

    @pl.when(((i == 0) | (e != prev)) & (i < nb_ref[0]))
    def _():
        for cp in weight_copies(e):
            cp.wait()
        wgu_bf[...] = wgu_st[...].astype(BF16)
        wd_bf[...] = wd_st[...].astype(BF16)
        nxt = nxt_ref[i]

        @pl.when(nxt != e)
        def _():
            for cp in weight_copies(nxt):
                cp.start()

    @pl.when(i < nb_ref[0])
    def _():
        gu = _dot(_load_rows(x_ref, 0, MOE_ROWS).astype(BF16), wgu_bf[...]) + bgu_ref[0]
        gate = jnp.minimum(gu[:, :D_EXPERT], SWIGLU_LIMIT)
        up = jnp.clip(gu[:, D_EXPERT:], -SWIGLU_LIMIT, SWIGLU_LIMIT)
        act = (up + 1.0) * gate * _sigmoid(SWIGLU_ALPHA * gate)
        _store_rows(_dot(act.astype(BF16), wd_bf[...]) + bd_ref[0], o_ref, 0, MOE_ROWS)

    @pl.when(i >= nb_ref[0])
    def _():
        o_ref[...] = jnp.zeros_like(o_ref)


def _moe_experts(xs, block_e, n_used, next_e, w_gu, b_gu, w_down, b_down, layer):
    w_idx = lambda i, be, nb, nx: (layer * N_EXPERTS + be[i], 0, 0)
    return pl.pallas_call(
        functools.partial(_expert_kernel, layer=layer),
        out_shape=jax.ShapeDtypeStruct((N_MOE_ROWS // 8,) + ROW_BLOCK, F32),
        grid_spec=pltpu.PrefetchScalarGridSpec(
            num_scalar_prefetch=3, grid=(N_MOE_BLOCKS,),
            in_specs=[pl.BlockSpec((MOE_ROWS // 8,) + ROW_BLOCK,
                                   lambda i, be, nb, nx: (jnp.minimum(i, nb[0] - 1), 0, 0, 0)),
                      pl.BlockSpec(memory_space=pl.ANY),
                      pl.BlockSpec((1, 1, 2 * D_EXPERT), w_idx),
                      pl.BlockSpec(memory_space=pl.ANY),
                      pl.BlockSpec((1, 1, D_MODEL), w_idx)],
            out_specs=pl.BlockSpec((MOE_ROWS // 8,) + ROW_BLOCK, lambda i, be, nb, nx: (i, 0, 0, 0)),
            scratch_shapes=[pltpu.VMEM((D_MODEL, 2 * D_EXPERT), F32), pltpu.VMEM((D_EXPERT, D_MODEL), F32),
                            pltpu.VMEM((D_MODEL, 2 * D_EXPERT), BF16), pltpu.VMEM((D_EXPERT, D_MODEL), BF16),
                            pltpu.SemaphoreType.DMA((2,))]),
        compiler_params=_params(1),
        name="moe_experts",
    )(block_e, n_used, next_e, xs, w_gu, b_gu, w_down, b_down)


def _combine_kernel(dest_ref, x_ref, g2_ref, w_ref, eo_ref, yc_ref, yl_ref, buf, sem):
    i = pl.program_id(0)
    cur = i % 2

    def issue_tile(tile, slot):
        base = tile * TM * TOP_K
        blk0 = slot * (TOP_K * TM // 8)

        def issue(blk, c):
            for u in range(8):
                for kk in range(TOP_K):
                    src = _row_at(eo_ref, dest_ref[base + (blk * 8 + u) * TOP_K + kk])
                    dst = buf.at[blk0 + kk * (TM // 8) + blk, :, pl.ds(u, 1), :]
                    pltpu.make_async_copy(src, dst, sem.at[slot]).start(priority=kk % 2)
            return c
        lax.fori_loop(0, TM // 8, issue, 0)

    @pl.when(i == 0)
    def _():
        issue_tile(0, 0)

    @pl.when(i + 1 < N_TILES)
    def _():
        issue_tile(i + 1, 1 - cur)

    for kk in range(TOP_K):
        pltpu.make_async_copy(eo_ref.at[pl.ds(0, TM // 8)], buf.at[pl.ds(0, TM // 8)], sem.at[cur]).wait()
    w = w_ref[...]
    ff = None
    for kk in range(TOP_K):
        term = _load_rows(buf, cur * (TOP_K * TM // 8) + kk * (TM // 8), TM) * w[:, kk:kk + 1]
        ff = term if ff is None else ff + term
    y = x_ref[...] + g2_ref[0] * ff
    is_ctx = i < N_CTX_TILES

    @pl.when(is_ctx)
    def _():
        yc_ref[...] = y

    @pl.when(jnp.logical_not(is_ctx))
    def _():
        yl_ref[...] = y


def _moe_combine(dest, x1, modr, top_w, eo, layer):
    mrow = _mod_row(TM)
    base = layer * N_MOD
    return pl.pallas_call(
        _combine_kernel,
        out_shape=[jax.ShapeDtypeStruct((T_CTX, D_MODEL), F32), jax.ShapeDtypeStruct((T_LAT, D_MODEL), F32)],
        grid_spec=pltpu.PrefetchScalarGridSpec(
            num_scalar_prefetch=1, grid=(N_TILES,),
            in_specs=[pl.BlockSpec((TM, D_MODEL), lambda i, d: (i, 0)),
                      pl.BlockSpec((1, 1, D_MODEL), lambda i, d: ((base + mrow(i)) * 6 + 5, 0, 0)),
                      pl.BlockSpec((TM, 128), lambda i, d: (i, 0)),
                      pl.BlockSpec(memory_space=pl.ANY)],
            out_specs=[pl.BlockSpec((TM, D_MODEL), lambda i, d: (jnp.minimum(i, N_CTX_TILES - 1), 0)),
                       pl.BlockSpec((TM, D_MODEL), lambda i, d: (jnp.maximum(i - N_CTX_TILES, 0), 0))],
            scratch_shapes=[pltpu.VMEM((2 * TOP_K * TM // 8,) + ROW_BLOCK, F32), pltpu.SemaphoreType.DMA((2,))]),
        compiler_params=_params(1),
        name="moe_combine",
    )(dest, x1, modr, top_w, eo)


def _pad_heads(w, n_heads, width):
    lead = w.shape[:-1]
    w = w.reshape(lead + (n_heads, width))
    w = jnp.pad(w, [(0, 0)] * len(lead) + [(0, 0), (0, HEAD_PAD - width)])
    return w.reshape(lead + (n_heads * HEAD_PAD,))


def _layer_weights(l, w_in, gla_wa2, gla_ba, mla_qa_g, mla_wuq, mla_kva_g, mla_wukv, mla_qn_g, mla_kn_g,
                   w_branch, w_out, router_w, router_b, norm2_g):
    wi = w_in[l]
    w_in_p = jnp.concatenate([wi[:, :3072], wi[:, 3520:], wi[:, 3104:3520], wi[:, 3072:3104],
                              jnp.zeros((D_MODEL, 64), F32)], axis=1).astype(BF16)
    wa_p = jnp.zeros((512, 512), F32)
    wa_p = wa_p.at[416:432, 0:256].set(gla_wa2[l, 0]).at[432:448, 256:512].set(gla_wa2[l, 1]).astype(BF16)
    ba_p = gla_ba[l].reshape(1, 512)
    wukv = mla_wukv[l].reshape(128, MLA_HEADS, MLA_NOPE + MLA_V)
    rw = jnp.pad(router_w[l], ((0, 0), (0, 128 - N_EXPERTS)))
    rw_hi = rw.astype(BF16)
    return {
        "w_in": w_in_p, "wa": wa_p, "ba": ba_p,
        "qa_g": mla_qa_g[l].reshape(1, 256),
        "wuq": _pad_heads(mla_wuq[l], MLA_HEADS, MLA_QK).astype(BF16),
        "qn_g": jnp.tile(jnp.pad(mla_qn_g[l], (0, HEAD_PAD - MLA_QK)), MLA_HEADS).reshape(1, 1024) * MLA_QK ** -0.5,
        "kn_g": jnp.tile(jnp.pad(mla_kn_g[l], (0, HEAD_PAD - MLA_QK)), MLA_HEADS).reshape(1, 1024),
        "kva_g": mla_kva_g[l].reshape(1, 128),
        "wk": _pad_heads(wukv[:, :, :MLA_NOPE].reshape(128, MLA_HEADS * MLA_NOPE), MLA_HEADS, MLA_NOPE).astype(BF16),
        "wv": wukv[:, :, MLA_NOPE:].reshape(128, MLA_HEADS * MLA_V).astype(BF16),
        "wb": w_branch[l].astype(BF16), "wo": w_out[l].astype(BF16),
        "rw_hi": rw_hi, "rw_lo": (rw - rw_hi.astype(F32)).astype(BF16),
        "rb": jnp.pad(router_b[l], (0, 128 - N_EXPERTS)).reshape(1, 128),
        "n2_g": norm2_g[l].reshape(1, D_MODEL),
    }


def kernel(x_prompt, x_sample, cache_mla_ckv, cache_mla_krope, state_ret, state_gla, c, c_ctx, w_mod, b_mod, norm1_g, norm2_g, w_in, ret_gn_g, gla_wa2, gla_ba, gla_norm_g, mla_qa_g, mla_wuq, mla_kva_g, mla_wukv, mla_qn_g, mla_kn_g, w_branch, w_out, router_w, router_b, moe_w_gu, moe_b_gu, moe_w_down, moe_b_down):
    xc, xl = x_prompt.reshape(T_CTX, D_MODEL), x_sample.reshape(T_LAT, D_MODEL)
    cc = jnp.concatenate([c_ctx[None, :], c, jnp.zeros((N_MOD - 1 - N_LAT_SEQ, D_MODEL), F32)], axis=0)
    modr = _modulation(cc, w_mod, b_mod).reshape(DEPTH * N_MOD * 6, 1, D_MODEL)
    rope_tabs = _rope_tables()
    w_gu = moe_w_gu.reshape(DEPTH * N_EXPERTS, D_MODEL, 2 * D_EXPERT)
    b_gu = moe_b_gu.reshape(DEPTH * N_EXPERTS, 1, 2 * D_EXPERT)
    w_dn = moe_w_down.reshape(DEPTH * N_EXPERTS, D_EXPERT, D_MODEL)
    b_dn = moe_b_down.reshape(DEPTH * N_EXPERTS, 1, D_MODEL)

    ckv_l, krope_l, ret_l, gla_l = [], [], [], []
    for l in range(DEPTH):
        w = _layer_weights(l, w_in, gla_wa2, gla_ba, mla_qa_g, mla_wuq, mla_kva_g, mla_wukv, mla_qn_g, mla_kn_g,
                           w_branch, w_out, router_w, router_b, norm2_g)
        z = _in_proj(xc, xl, norm1_g[l].reshape(1, D_MODEL), modr, w["w_in"], l)

        gn = ret_gn_g[l].reshape(1, 512)
        ret_c, ret_state = _retention(z, gn, None, ctx=True)
        (ret_s,) = _retention(z, gn, state_ret[:, l], ctx=False)
        gng = gla_norm_g[l].reshape(1, 512)
        gla_c, gla_state = _gla(z, w["wa"], w["ba"], gng, None, ctx=True)
        gla_s, _ = _gla(z, w["wa"], w["ba"], gng, state_gla[:, l], ctx=False)

        q, k, v, ckv, kr = _mla_prep(z, w, rope_tabs)
        kc, vc = _mla_cache(cache_mla_ckv[:, l].reshape(N_LAT_SEQ * CTX_LEN, 128),
                            jnp.pad(cache_mla_krope[:, l].reshape(N_LAT_SEQ * CTX_LEN, MLA_ROPE),
                                    ((0, 0), (0, 128 - MLA_ROPE))), w)
        mla_c = _mla_attn(q, k, v, None, None, ctx=True)
        mla_s = _mla_attn(q, k, v, kc, vc, ctx=False)

        x1, h2, top_idx, top_w, rank, cnt = _merge((ret_c, ret_s, gla_c, gla_s, mla_c, mla_s), z, xc, xl, modr, w, l)
        dest, tail_start, block_e, n_used, next_e = _route(top_idx[:, :TOP_K], rank[:, :TOP_K],
                                                           cnt[0, :N_EXPERTS].astype(jnp.int32))
        xs = _moe_dispatch(h2, dest, tail_start, n_used)
        eo = _moe_experts(xs, block_e, n_used, next_e, w_gu, b_gu, w_dn, b_dn, l)
        xc, xl = _moe_combine(dest, x1, modr, top_w, eo, l)

        ckv_l.append(ckv[:T_CTX].reshape(N_CTX_SEQ, CTX_LEN, 128))
        krope_l.append(kr[:T_CTX, :MLA_ROPE].reshape(N_CTX_SEQ, CTX_LEN, MLA_ROPE))
        ret_l.append(ret_state)
        gla_l.append(gla_state.reshape(N_CTX_SEQ, 2, N_HEADS, DK, DV))

    y_p = xc.reshape(N_CTX_SEQ, CTX_LEN, D_MODEL)
    y_s = xl.reshape(N_LAT_SEQ, LAT_LEN, D_MODEL)
    return (y_p, y_s, jnp.stack(ckv_l, axis=1), jnp.stack(krope_l, axis=1),
            jnp.stack(ret_l, axis=1), jnp.stack(gla_l, axis=1))
```

```python
import functools

import jax
import jax.numpy as jnp
import numpy as np
from jax import lax
from jax.experimental import pallas as pl
from jax.experimental.pallas import tpu as pltpu

F32 = jnp.float32
BF16 = jnp.bfloat16

D_MODEL = 1024
DEPTH = 2
N_CTX_SEQ, CTX_LEN = 32, 256
N_LAT_SEQ, LAT_LEN = 4, 1024
T_CTX = N_CTX_SEQ * CTX_LEN
T_LAT = N_LAT_SEQ * LAT_LEN
T_ALL = T_CTX + T_LAT
TM = 256
N_TILES = T_ALL // TM
N_CTX_TILES = T_CTX // TM
LAT_TILES = LAT_LEN // TM
N_MOD = 8
EPS = 1e-6

N_HEADS = 4
DK, DV = 64, 128
GRID_W = 64
MLA_HEADS, MLA_NOPE, MLA_ROPE, MLA_V = 8, 64, 32, 64
MLA_QK = MLA_NOPE + MLA_ROPE
HEAD_PAD = 128
GLA_TAU = 16.0
N_EXPERTS, TOP_K, D_EXPERT = 32, 4, 1024
SWIGLU_LIMIT, SWIGLU_ALPHA = 7.0, 1.702
MOE_ROWS = 512
N_SLOTS = T_ALL * TOP_K
N_MOE_BLOCKS = N_SLOTS // MOE_ROWS + N_EXPERTS
N_MOE_ROWS = N_MOE_BLOCKS * MOE_ROWS
MERGE_TM = 512
ROUTE_CHUNKS = 8
XS_EXTRA = MOE_ROWS
ROW_BLOCK = (D_MODEL // 128, 8, 128)

DZ = 6656
IN_TILE = 512
VMEM_LIMIT = 56 * 1024 * 1024

RET_LOG_F = [float(np.log1p(-np.exp2(-(5.0 + h)))) for h in range(N_HEADS)]
RET_LOG_B = [float(np.log1p(-np.exp2(-(5.5 + h)))) for h in range(N_HEADS)]


def _params(n_axes, vmem=VMEM_LIMIT):
    return pltpu.CompilerParams(dimension_semantics=("arbitrary",) * n_axes, vmem_limit_bytes=vmem)


def _sigmoid(x):
    return 1.0 / (1.0 + jnp.exp(-x))


def _dot(a, b):
    return jnp.dot(a, b, preferred_element_type=F32)


def _dot_t(a, b):
    return lax.dot_general(a, b, (((1,), (1,)), ((), ())), preferred_element_type=F32)


def _mod_row(tile_rows):
    def f(i):
        r0 = i * tile_rows
        return jnp.where(r0 < T_CTX, 0, 1 + (r0 - T_CTX) // LAT_LEN)
    return f


def _ctx_spec(rows, width):
    n_ctx = T_CTX // rows
    return pl.BlockSpec((rows, width), lambda i: (jnp.minimum(i, n_ctx - 1), 0))


def _lat_spec(rows, width):
    n_ctx = T_CTX // rows
    return pl.BlockSpec((rows, width), lambda i: (jnp.maximum(i - n_ctx, 0), 0))


def _mod_kernel(c_ref, w_ref, b_ref, o_ref):
    c = c_ref[...]
    s = c * _sigmoid(c)
    o_ref[0] = jnp.dot(s, w_ref[0], preferred_element_type=F32, precision=lax.Precision.HIGHEST) + b_ref[0]


def _modulation(cc, w_mod, b_mod):
    n = 6 * D_MODEL
    blk = 2048
    return pl.pallas_call(
        _mod_kernel,
        out_shape=jax.ShapeDtypeStruct((DEPTH, N_MOD, n), F32),
        grid=(DEPTH, n // blk),
        in_specs=[pl.BlockSpec((N_MOD, D_MODEL), lambda l, j: (0, 0)),
                  pl.BlockSpec((1, D_MODEL, blk), lambda l, j: (l, 0, j)),
                  pl.BlockSpec((1, 1, blk), lambda l, j: (l, 0, j))],
        out_specs=pl.BlockSpec((1, N_MOD, blk), lambda l, j: (l, 0, j)),
        compiler_params=_params(2),
        name="modulation",
    )(cc, w_mod, b_mod.reshape(DEPTH, 1, n))


def _in_kernel(xc_ref, xl_ref, g_ref, sh_ref, sc_ref, w_ref, o_ref):
    x = jnp.where(pl.program_id(0) < T_CTX // IN_TILE, xc_ref[...], xl_ref[...])
    h = x * lax.rsqrt(jnp.mean(x * x, axis=-1, keepdims=True) + EPS) * g_ref[...]
    h = h * (1.0 + sc_ref[0]) + sh_ref[0]
    hb = h.astype(BF16)
    for n0 in range(0, DZ, 512):
        o_ref[:, n0:n0 + 512] = _dot(hb, w_ref[:, n0:n0 + 512]).astype(BF16)


def _in_proj(xc, xl, g, modr, w_in_p, layer):
    mrow = _mod_row(IN_TILE)
    base = layer * N_MOD

    def mod_spec(part):
        return pl.BlockSpec((1, 1, D_MODEL), lambda i: ((base + mrow(i)) * 6 + part, 0, 0))

    return pl.pallas_call(
        _in_kernel,
        out_shape=jax.ShapeDtypeStruct((T_ALL, DZ), BF16),
        grid=(T_ALL // IN_TILE,),
        in_specs=[_ctx_spec(IN_TILE, D_MODEL), _lat_spec(IN_TILE, D_MODEL),
                  pl.BlockSpec((1, D_MODEL), lambda i: (0, 0)),
                  mod_spec(0), mod_spec(1),
                  pl.BlockSpec((D_MODEL, DZ), lambda i: (0, 0))],
        out_specs=pl.BlockSpec((IN_TILE, DZ), lambda i: (i, 0)),
        compiler_params=_params(1),
        name="in_proj",
    )(xc, xl, g, modr, modr, w_in_p)


def _lane_half_mask(hh):
    lane = lax.broadcasted_iota(jnp.int32, (1, 128), 1)
    return (lane < 64) if hh == 0 else (lane >= 64)


@functools.lru_cache(maxsize=None)
def _ret_decay_table(seq):
    d = np.arange(seq)[:, None] - np.arange(seq)[None, :]
    tab = np.stack([np.exp(np.where(d > 0, RET_LOG_F[h] * d, -RET_LOG_B[h] * d)) for h in range(N_HEADS)])
    return (tab * np.where(d == 0, 2.0, 1.0) * DK ** -0.5).astype(np.float32)


def _ret_kernel(*refs, seq, has_state, emit_state):
    q_ref, k_ref, v_ref, g_ref, gn_ref, dec_ref = refs[:6]
    pos = 6
    if has_state:
        s0_ref = refs[pos]
        pos += 1
    o_ref = refs[pos]
    pos += 1
    if emit_state:
        st_ref = refs[pos]

    r0 = pl.multiple_of(pl.program_id(1) * TM, TM)
    qb = q_ref[pl.ds(r0, TM), :]
    ri = (lax.broadcasted_iota(jnp.int32, (TM, 1), 0) + r0).astype(F32)

    for h in range(N_HEADS):
        p, hh = h // 2, h % 2
        lanes = slice(128 * p, 128 * p + 128)
        qp = qb[:, lanes]
        qh = jnp.where(_lane_half_mask(hh), qp, jnp.zeros_like(qp))
        sc = _dot_t(qh, k_ref[:, lanes])
        o = _dot((sc * dec_ref[h]).astype(BF16), v_ref[:, 128 * h:128 * h + 128])
        if has_state:
            qf = qh.astype(F32)
            o += _dot((qf * jnp.exp(RET_LOG_F[h] * (ri + 1.0))).astype(BF16), s0_ref[0, 0, p].astype(BF16))
            o += _dot((qf * jnp.exp(RET_LOG_B[h] * (seq - ri))).astype(BF16), s0_ref[0, 1, p].astype(BF16))
        mu = jnp.mean(o, axis=-1, keepdims=True)
        d = o - mu
        var = jnp.mean(d * d, axis=-1, keepdims=True)
        on = d * lax.rsqrt(var + EPS)
        g = g_ref[:, 128 * h:128 * h + 128].astype(F32)
        out = on * gn_ref[:, 128 * h:128 * h + 128] * (g * _sigmoid(g))
        o_ref[:, 128 * h:128 * h + 128] = out.astype(BF16)

    if emit_state:
        jc = lax.broadcasted_iota(jnp.int32, (seq, 1), 0).astype(F32)
        lane = lax.broadcasted_iota(jnp.int32, (1, 128), 1)
        for p in range(2):
            kp = k_ref[:, 128 * p:128 * p + 128].astype(F32) * DK ** -0.5
            lgf = jnp.where(lane < 64, RET_LOG_F[2 * p], RET_LOG_F[2 * p + 1])
            lgb = jnp.where(lane < 64, RET_LOG_B[2 * p], RET_LOG_B[2 * p + 1])
            kdf = (kp * jnp.exp(lgf * (seq - 1.0 - jc))).T.astype(BF16)
            kdb = (kp * jnp.exp(lgb * jc)).T.astype(BF16)
            for hh in range(2):
                h = 2 * p + hh
                vh = v_ref[:, 128 * h:128 * h + 128]
                st_ref[0, 0, h] = _dot(kdf, vh)[64 * hh:64 * hh + 64, :]
                st_ref[0, 1, h] = _dot(kdb, vh)[64 * hh:64 * hh + 64, :]


def _retention(z, gn_g, s0, *, ctx):
    if ctx:
        nb, seq, row_blk, tile0 = N_CTX_SEQ, CTX_LEN, 0, 0
    else:
        nb, seq, row_blk, tile0 = N_LAT_SEQ, LAT_LEN, T_CTX // LAT_LEN, N_CTX_TILES
    nq = seq // TM
    in_specs = [pl.BlockSpec((seq, 256), lambda b, i: (row_blk + b, 0)),
                pl.BlockSpec((seq, 256), lambda b, i: (row_blk + b, 1)),
                pl.BlockSpec((seq, 512), lambda b, i: (row_blk + b, 1)),
                pl.BlockSpec((TM, 512), lambda b, i: (tile0 + b * nq + i, 2)),
                pl.BlockSpec((1, 512), lambda b, i: (0, 0)),
                pl.BlockSpec((N_HEADS, TM, seq), lambda b, i: (0, i, 0))]
    args = [z, z, z, z, gn_g, _ret_decay_table(seq)]
    out_shape = [jax.ShapeDtypeStruct((nb * seq, 512), BF16)]
    out_specs = [pl.BlockSpec((TM, 512), lambda b, i: (b * nq + i, 0))]
    if not ctx:
        in_specs.append(pl.BlockSpec((1, 2, 2, 128, 128), lambda b, i: (b, 0, 0, 0, 0)))
        args.append(s0.reshape(N_LAT_SEQ, 2, 2, 128, 128))
    else:
        out_shape.append(jax.ShapeDtypeStruct((nb, 2, N_HEADS, DK, DV), F32))
        out_specs.append(pl.BlockSpec((1, 2, N_HEADS, DK, DV), lambda b, i: (b, 0, 0, 0, 0)))
    return pl.pallas_call(
        functools.partial(_ret_kernel, seq=seq, has_state=not ctx, emit_state=ctx),
        out_shape=out_shape, grid=(nb, nq), in_specs=in_specs, out_specs=out_specs,
        compiler_params=_params(2),
        name="retention_ctx" if ctx else "retention_lat",
    )(*args)


def _gla_decay(small_ref, wa_ref, ba_ref):
    x = _dot(small_ref[...], wa_ref[...]) + ba_ref[...]
    la = -(jnp.maximum(-x, 0.0) + jnp.log(1.0 + jnp.exp(-jnp.abs(x)))) * (1.0 / GLA_TAU)
    ri = lax.broadcasted_iota(jnp.int32, (TM, TM), 0)
    ci = lax.broadcasted_iota(jnp.int32, (TM, TM), 1)
    ltri = jnp.where(ri >= ci, 1.0, 0.0).astype(BF16)
    hi = la.astype(BF16)
    r1 = la - hi.astype(F32)
    mid = r1.astype(BF16)
    lo = (r1 - mid.astype(F32)).astype(BF16)
    cum = _dot(ltri, hi) + _dot(ltri, mid) + _dot(ltri, lo)
    return la, cum


def _gla_state_kernel(k_ref, v_ref, small_ref, wa_ref, ba_ref, kv_ref, tot_ref):
    la, cum = _gla_decay(small_ref, wa_ref, ba_ref)
    bf, bb = cum[:, :256], cum[:, 256:]
    xb = bb - la[:, 256:]
    k = k_ref[...].astype(F32)
    kdf = k * jnp.exp(bf[TM - 1:TM, :] - bf)
    kdb = k * jnp.exp(xb)
    for p in range(2):
        kf_t = kdf[:, 128 * p:128 * p + 128].T.astype(BF16)
        kb_t = kdb[:, 128 * p:128 * p + 128].T.astype(BF16)
        for hh in range(2):
            h = 2 * p + hh
            vh = v_ref[:, 128 * h:128 * h + 128]
            kv_ref[0, 0, 0, h] = _dot(kf_t, vh)[64 * hh:64 * hh + 64, :]
            kv_ref[0, 0, 1, h] = _dot(kb_t, vh)[64 * hh:64 * hh + 64, :]
    tot_ref[0, 0] = jnp.sum(la.T, axis=-1, keepdims=True)


def _mid_bcast(x, s, r):
    w = 2 * s
    if w >= 8:
        n = TM // w
        x3 = x.reshape(n, w, 256)
        return jnp.broadcast_to(x3[:, r:r + 1, :], (n, w, 256)).reshape(TM, 256)
    x3 = x.reshape(TM // 8, 8, 256)
    sub = lax.broadcasted_iota(jnp.int32, (1, 8, 1), 1)
    out = None
    for blk in range(8 // w):
        rowv = jnp.broadcast_to(x3[:, blk * w + r:blk * w + r + 1, :], (TM // 8, 8, 256))
        out = rowv if out is None else jnp.where(sub >= blk * w, rowv, out)
    return out.reshape(TM, 256)


def _gla_kernel(*refs, n_blk, has_state, emit_state):
    q_ref, k_ref, v_ref, g_ref, small_ref, wa_ref, ba_ref, gn_ref = refs[:8]
    pos = 8
    if has_state:
        kv_ref, tot_ref, s0_ref = refs[pos:pos + 3]
        pos += 3
    o_ref = refs[pos]
    pos += 1
    if emit_state:
        kvo_ref = refs[pos]

    la, cum = _gla_decay(small_ref, wa_ref, ba_ref)
    bf, bb = cum[:, :256], cum[:, 256:]
    xb = bb - la[:, 256:]
    q = q_ref[...].astype(F32) * DK ** -0.5
    k = k_ref[...].astype(F32)
    row = lax.broadcasted_iota(jnp.int32, (TM, 1), 0)
    rowi = lax.broadcasted_iota(jnp.int32, (TM, TM), 0)
    colj = lax.broadcasted_iota(jnp.int32, (TM, TM), 1)
    low_half = _lane_half_mask(0)

    def join(fwd, bwd):
        ops = []
        for p in range(N_HEADS // 2):
            f = fwd[:, 128 * p:128 * p + 128]
            br = pltpu.roll(bwd[:, 128 * p:128 * p + 128], 64, 1)
            ops.append(jnp.where(low_half, f, br).astype(BF16))
            ops.append(jnp.where(low_half, br, f).astype(BF16))
        return ops

    qd, kd = join(q, q), join(k, k)
    acc = [jnp.where(rowi == colj, _dot_t(qo, ko), 0.0) for qo, ko in zip(qd, kd)]

    low_f = jnp.where(low_half, 1.0, 0.0)
    s = 1
    while s < TM:
        up_f = jnp.where(((row // s) % 2) == 1, 1.0, 0.0)
        live_even = jnp.where(up_f == low_f, 1.0, 0.0).astype(BF16)
        live = [live_even, 1.0 - live_even]
        dead = [live[1], live[0]]
        af = -jnp.abs(bf - _mid_bcast(bf, s, s - 1))
        ab = -jnp.abs(xb - _mid_bcast(xb, s, s))
        same = (rowi // (2 * s)) == (colj // (2 * s))
        for p in range(N_HEADS // 2):
            a_f = af[:, 128 * p:128 * p + 128]
            a_b = pltpu.roll(ab[:, 128 * p:128 * p + 128], 64, 1)
            for hh in range(2):
                h = 2 * p + hh
                arg = jnp.where(low_half, a_f, a_b) if hh == 0 else jnp.where(low_half, a_b, a_f)
                e = jnp.exp(arg).astype(BF16)
                sl = _dot_t(e * (qd[h] * live[hh]), e * (kd[h] * dead[hh]))
                acc[h] = acc[h] + (jnp.where(same, sl, 0.0) if 2 * s < TM else sl)
        s *= 2

    if has_state:
        n = pl.program_id(1)
        q_state = join(q * jnp.exp(bf), q * jnp.exp(bb[TM - 1:TM, :] - xb))

    if emit_state:
        kdf = k * jnp.exp(bf[TM - 1:TM, :] - bf)
        kdb = k * jnp.exp(xb)
        for p in range(N_HEADS // 2):
            kf_t = kdf[:, 128 * p:128 * p + 128].T.astype(BF16)
            kb_t = kdb[:, 128 * p:128 * p + 128].T.astype(BF16)
            for hh in range(2):
                h = 2 * p + hh
                vh = v_ref[:, 128 * h:128 * h + 128]
                kvo_ref[0, 0, 0, h] = _dot(kf_t, vh)[64 * hh:64 * hh + 64, :]
                kvo_ref[0, 0, 1, h] = _dot(kb_t, vh)[64 * hh:64 * hh + 64, :]

    for h in range(N_HEADS):
        o = _dot(acc[h].astype(BF16), v_ref[:, 128 * h:128 * h + 128])
        if has_state:
            sf = s0_ref[0, 0, h]
            for m in range(n_blk - 1):
                dec = jnp.exp(tot_ref[0, m, 64 * h:64 * h + 64, :])
                sf = jnp.where(m < n, dec * sf + kv_ref[0, m, 0, h], sf)
            sb = s0_ref[0, 1, h]
            for m in range(n_blk - 1, 0, -1):
                dec = jnp.exp(tot_ref[0, m, 256 + 64 * h:256 + 64 * h + 64, :])
                sb = jnp.where(m > n, dec * sb + kv_ref[0, m, 1, h], sb)
            state = jnp.concatenate([sf, sb] if h % 2 == 0 else [sb, sf], axis=0).astype(BF16)
            o += _dot(q_state[h], state)
        on = o * lax.rsqrt(jnp.mean(o * o, axis=-1, keepdims=True) + EPS)
        g = g_ref[:, 128 * h:128 * h + 128].astype(F32)
        out = on * gn_ref[:, 128 * h:128 * h + 128] * (g * _sigmoid(g))
        o_ref[:, 128 * h:128 * h + 128] = out.astype(BF16)


def _gla(z, wa_p, ba_p, gn_g, s0, *, ctx):
    if ctx:
        nb, n_blk, tile0 = N_CTX_SEQ, 1, 0
    else:
        nb, n_blk, tile0 = N_LAT_SEQ, LAT_TILES, N_CTX_TILES

    def zspec(width, col):
        return pl.BlockSpec((TM, width), lambda b, n: (tile0 + b * n_blk + n, col))

    w_specs = [pl.BlockSpec((512, 512), lambda b, n: (0, 0)), pl.BlockSpec((1, 512), lambda b, n: (0, 0))]
    kv_shape = jax.ShapeDtypeStruct((nb, n_blk, 2, N_HEADS, DK, DV), F32)
    kv_spec = pl.BlockSpec((1, 1, 2, N_HEADS, DK, DV), lambda b, n: (b, n, 0, 0, 0, 0))
    in_specs = [zspec(256, 6), zspec(256, 7), zspec(512, 4), zspec(512, 5), zspec(512, 12)] + w_specs
    in_specs.append(pl.BlockSpec((1, 512), lambda b, n: (0, 0)))
    args = [z, z, z, z, z, wa_p, ba_p, gn_g]
    out_shape = [jax.ShapeDtypeStruct((nb * n_blk * TM, 512), BF16)]
    out_specs = [pl.BlockSpec((TM, 512), lambda b, n: (b * n_blk + n, 0))]
    if ctx:
        out_shape.append(kv_shape)
        out_specs.append(kv_spec)
    else:
        kv, tot = pl.pallas_call(
            _gla_state_kernel,
            out_shape=[kv_shape, jax.ShapeDtypeStruct((nb, n_blk, 512, 1), F32)],
            grid=(nb, n_blk),
            in_specs=[zspec(256, 7), zspec(512, 4), zspec(512, 12)] + w_specs,
            out_specs=[kv_spec, pl.BlockSpec((1, 1, 512, 1), lambda b, n: (b, n, 0, 0))],
            compiler_params=_params(2),
            name="gla_state_lat",
        )(z, z, z, wa_p, ba_p)
        in_specs += [pl.BlockSpec((1, n_blk, 2, N_HEADS, DK, DV), lambda b, n: (b, 0, 0, 0, 0, 0)),
                     pl.BlockSpec((1, n_blk, 512, 1), lambda b, n: (b, 0, 0, 0)),
                     pl.BlockSpec((1, 2, N_HEADS, DK, DV), lambda b, n: (b, 0, 0, 0, 0))]
        args += [kv, tot, s0]
    res = pl.pallas_call(
        functools.partial(_gla_kernel, n_blk=n_blk, has_state=not ctx, emit_state=ctx),
        out_shape=out_shape, grid=(nb, n_blk), in_specs=in_specs, out_specs=out_specs,
        compiler_params=_params(2),
        name="gla_ctx" if ctx else "gla_lat",
    )(*args)
    return (res[0], res[1]) if ctx else (res[0], None)


def _rope_tables():
    nf = MLA_ROPE // 4
    pos = np.arange(LAT_LEN)
    freqs = (10000.0 ** (-np.arange(nf, dtype=np.float32) / nf)).astype(np.float32)
    ang_r = ((pos // GRID_W).astype(np.float32)[:, None] * freqs).astype(np.float32)
    ang_c = ((pos % GRID_W).astype(np.float32)[:, None] * freqs).astype(np.float32)
    cos = np.ones((TM + LAT_LEN, HEAD_PAD), np.float32)
    sa = np.zeros((TM + LAT_LEN, HEAD_PAD), np.float32)
    sb = np.zeros((TM + LAT_LEN, HEAD_PAD), np.float32)
    o = MLA_NOPE
    for base, ang in ((o, ang_r), (o + 2 * nf, ang_c)):
        cos[TM:, base:base + nf] = np.cos(ang)
        cos[TM:, base + nf:base + 2 * nf] = np.cos(ang)
        sa[TM:, base:base + nf] = -np.sin(ang)
        sb[TM:, base + nf:base + 2 * nf] = np.sin(ang)
    return jnp.asarray(cos), jnp.asarray(sa), jnp.asarray(sb)


def _rope(x, cos, sa, sb):
    return x * cos + pltpu.roll(x, 128 - 8, 1) * sa + pltpu.roll(x, 8, 1) * sb


def _head_segments():
    seg = np.zeros((MLA_HEADS * HEAD_PAD, 128), np.float32)
    for h in range(MLA_HEADS):
        seg[h * HEAD_PAD:(h + 1) * HEAD_PAD, h] = 1.0
    return jnp.asarray(seg, BF16), jnp.asarray(seg.T.copy(), BF16)


def _head_norm(x, gain, seg_ref, segt_ref):
    ss = _dot((x * x).astype(BF16), seg_ref[...])
    rs = lax.rsqrt(ss * (1.0 / MLA_QK) + EPS)
    hi = rs.astype(BF16)
    lo = (rs - hi.astype(F32)).astype(BF16)
    return x * (_dot(hi, segt_ref[...]) + _dot(lo, segt_ref[...])) * gain


def _store_heads(x, o_ref, rope):
    if rope is None:
        o_ref[...] = x.astype(BF16)
    else:
        for h in range(MLA_HEADS):
            o_ref[:, 128 * h:128 * h + 128] = _rope(x[:, 128 * h:128 * h + 128], *rope).astype(BF16)


def _mla_keys(ckv, kr_tile, wk_ref, wv_ref, kn_ref, seg_ref, segt_ref, k_ref, v_ref, rope=None):
    cb = ckv.astype(BF16)
    kpre = _dot(cb, wk_ref[...]) + jnp.concatenate([kr_tile] * MLA_HEADS, axis=1)
    v_ref[...] = _dot(cb, wv_ref[...]).astype(BF16)
    _store_heads(_head_norm(kpre, kn_ref[...], seg_ref, segt_ref), k_ref, rope)


def _mla_prep_kernel(small_ref, qa_ref, wuq_ref, qn_ref, kva_ref, wk_ref, wv_ref, kn_ref, seg_ref, segt_ref,
                     cos_ref, sa_ref, sb_ref, q_ref, k_ref, v_ref, ckv_ref, kr_ref):
    def body(rope):
        sm = small_ref[...].astype(F32)
        cq, ckv_raw, g3 = sm[:, 0:256], sm[:, 256:384], sm[:, 384:512]
        cqn = cq * lax.rsqrt(jnp.mean(cq * cq, axis=-1, keepdims=True) + EPS) * qa_ref[...]
        q = _dot(cqn.astype(BF16), wuq_ref[...])
        _store_heads(_head_norm(q, qn_ref[...], seg_ref, segt_ref), q_ref, rope)
        ckv = ckv_raw * lax.rsqrt(jnp.mean(ckv_raw * ckv_raw, axis=-1, keepdims=True) + EPS) * kva_ref[...]
        ckv_ref[...] = ckv
        lane = lax.broadcasted_iota(jnp.int32, (1, 128), 1)
        kr = jnp.where(lane < MLA_ROPE, g3, 0.0)
        kr_ref[...] = kr
        _mla_keys(ckv, pltpu.roll(kr, MLA_NOPE, 1), wk_ref, wv_ref, kn_ref, seg_ref, segt_ref, k_ref, v_ref, rope)

    is_ctx = pl.program_id(0) < N_CTX_TILES

    @pl.when(is_ctx)
    def _():
        body(None)

    @pl.when(jnp.logical_not(is_ctx))
    def _():
        body((cos_ref[...], sa_ref[...], sb_ref[...]))


def _mla_cache_kernel(ckv_ref, kr_ref, wk_ref, wv_ref, kn_ref, seg_ref, segt_ref, k_ref, v_ref):
    _mla_keys(ckv_ref[...], pltpu.roll(kr_ref[...], MLA_NOPE, 1), wk_ref, wv_ref, kn_ref, seg_ref, segt_ref,
              k_ref, v_ref)


def _mla_prep(z, w, rope_tabs):
    def rope_blk(i):
        return jnp.where(i < N_CTX_TILES, 0, 1 + (i - N_CTX_TILES) % LAT_TILES)

    const = lambda shape: pl.BlockSpec(shape, lambda i: (0,) * len(shape))
    rope_spec = pl.BlockSpec((TM, HEAD_PAD), lambda i: (rope_blk(i), 0))
    row = lambda width: pl.BlockSpec((TM, width), lambda i: (i, 0))
    return pl.pallas_call(
        _mla_prep_kernel,
        out_shape=[jax.ShapeDtypeStruct((T_ALL, 1024), BF16), jax.ShapeDtypeStruct((T_ALL, 1024), BF16),
                   jax.ShapeDtypeStruct((T_ALL, 512), BF16), jax.ShapeDtypeStruct((T_ALL, 128), F32),
                   jax.ShapeDtypeStruct((T_ALL, 128), F32)],
        grid=(N_TILES,),
        in_specs=[pl.BlockSpec((TM, 512), lambda i: (i, 12)),
                  const((1, 256)), const((256, 1024)), const((1, 1024)), const((1, 128)),
                  const((128, 1024)), const((128, 512)), const((1, 1024)),
                  const((1024, 128)), const((128, 1024)),
                  rope_spec, rope_spec, rope_spec],
        out_specs=[row(1024), row(1024), row(512), row(128), row(128)],
        compiler_params=_params(1),
        name="mla_prep",
    )(z, w["qa_g"], w["wuq"], w["qn_g"], w["kva_g"], w["wk"], w["wv"], w["kn_g"], *_head_segments(), *rope_tabs)


def _mla_cache(ckv, kr_pad, w):
    const = lambda shape: pl.BlockSpec(shape, lambda i: (0,) * len(shape))
    row = lambda width: pl.BlockSpec((TM, width), lambda i: (i, 0))
    n = ckv.shape[0]
    return pl.pallas_call(
        _mla_cache_kernel,
        out_shape=[jax.ShapeDtypeStruct((n, 1024), BF16), jax.ShapeDtypeStruct((n, 512), BF16)],
        grid=(n // TM,),
        in_specs=[row(128), row(128), const((128, 1024)), const((128, 512)), const((1, 1024)),
                  const((1024, 128)), const((128, 1024))],
        out_specs=[row(1024), row(512)],
        compiler_params=_params(1),
        name="mla_cache",
    )(ckv, kr_pad, w["wk"], w["wv"], w["kn_g"], *_head_segments())


def _mla_attn_kernel(*refs, has_cache):
    q_ref, k_ref, v_ref = refs[:3]
    pos = 3
    if has_cache:
        kc_ref, vc_ref = refs[3:5]
        pos = 5
    o_ref = refs[pos]
    for p in range(MLA_HEADS // 2):
        acc = jnp.zeros((TM, 128), F32)
        for hh in range(2):
            h = 2 * p + hh
            lanes = slice(128 * h, 128 * h + 128)
            qh = q_ref[:, lanes]
            l1 = _dot_t(qh, k_ref[:, lanes])
            m = jnp.max(l1, axis=-1, keepdims=True)
            if has_cache:
                l0 = _dot_t(qh, kc_ref[:, lanes])
                m = jnp.maximum(m, jnp.max(l0, axis=-1, keepdims=True))
                p0 = jnp.exp(l0 - m)
            p1 = jnp.exp(l1 - m)
            den = jnp.sum(p1, axis=-1, keepdims=True)
            if has_cache:
                den = den + jnp.sum(p0, axis=-1, keepdims=True)
            mask = _lane_half_mask(hh)
            vp = v_ref[:, 128 * p:128 * p + 128]
            o = _dot(p1.astype(BF16), jnp.where(mask, vp, jnp.zeros_like(vp)))
            if has_cache:
                vcp = vc_ref[:, 128 * p:128 * p + 128]
                o += _dot(p0.astype(BF16), jnp.where(mask, vcp, jnp.zeros_like(vcp)))
            acc += o * (1.0 / den)
        o_ref[:, 128 * p:128 * p + 128] = acc.astype(BF16)


def _mla_attn(q, k, v, kc, vc, *, ctx):
    if ctx:
        nb, seq, row_blk, tile0 = N_CTX_SEQ, CTX_LEN, 0, 0
    else:
        nb, seq, row_blk, tile0 = N_LAT_SEQ, LAT_LEN, T_CTX // LAT_LEN, N_CTX_TILES
    nq = seq // TM
    in_specs = [pl.BlockSpec((TM, 1024), lambda b, i: (tile0 + b * nq + i, 0)),
                pl.BlockSpec((seq, 1024), lambda b, i: (row_blk + b, 0)),
                pl.BlockSpec((seq, 512), lambda b, i: (row_blk + b, 0))]
    args = [q, k, v]
    if not ctx:
        in_specs += [pl.BlockSpec((TM, 1024), lambda b, i: (b, 0)), pl.BlockSpec((TM, 512), lambda b, i: (b, 0))]
        args += [kc, vc]
    return pl.pallas_call(
        functools.partial(_mla_attn_kernel, has_cache=not ctx),
        out_shape=jax.ShapeDtypeStruct((nb * seq, 512), BF16),
        grid=(nb, nq), in_specs=in_specs,
        out_specs=pl.BlockSpec((TM, 512), lambda b, i: (b * nq + i, 0)),
        compiler_params=_params(2),
        name="mla_attn_ctx" if ctx else "mla_attn_lat",
    )(*args)


def _merge_kernel(retc_ref, retl_ref, glac_ref, glal_ref, mlac_ref, mlal_ref, m0_ref, m1_ref, m2_ref,
                  xc_ref, xl_ref, wb_ref, wo_ref,
                  g1_ref, n2_ref, sh2_ref, sc2_ref, rwh_ref, rwl_ref, rb_ref,
                  x1_ref, h_ref, idx_ref, w_ref, rank_ref, cnt_ref):
    @pl.when(pl.program_id(0) == 0)
    def _():
        cnt_ref[...] = jnp.zeros_like(cnt_ref)

    is_ctx = pl.program_id(0) < T_CTX // MERGE_TM
    mix = None
    for c_ref, l_ref, m_ref, n in ((retc_ref, retl_ref, m0_ref, 0), (glac_ref, glal_ref, m1_ref, 1),
                                   (mlac_ref, mlal_ref, m2_ref, 2)):
        branch = jnp.where(is_ctx, c_ref[...], l_ref[...])
        term = _sigmoid(m_ref[...]).astype(F32) * _dot(branch, wb_ref[n])
        mix = term if mix is None else mix + term
    out = _dot(mix.astype(BF16), wo_ref[...])
    x1 = jnp.where(is_ctx, xc_ref[...], xl_ref[...]) + g1_ref[0] * out
    x1_ref[...] = x1
    h = x1 * lax.rsqrt(jnp.mean(x1 * x1, axis=-1, keepdims=True) + EPS) * n2_ref[...]
    h = h * (1.0 + sc2_ref[0]) + sh2_ref[0]
    h_ref[...] = h
    hh = h.astype(BF16)
    hl = (h - hh.astype(F32)).astype(BF16)
    logits = _dot(hh, rwh_ref[...]) + _dot(hh, rwl_ref[...]) + _dot(hl, rwh_ref[...]) + rb_ref[...]
    rows = MERGE_TM // ROUTE_CHUNKS
    lane = lax.broadcasted_iota(jnp.int32, (rows, 128), 1)
    lanef = lane.astype(F32)
    onehots, osums = [], []
    for c in range(ROUTE_CHUNKS):
        l = jnp.where(lane < N_EXPERTS, logits[c * rows:(c + 1) * rows], -jnp.inf)
        vals, idxs = [], []
        for _ in range(TOP_K):
            m = jnp.max(l, axis=-1, keepdims=True)
            ix = jnp.min(jnp.where(l == m, lanef, 128.0), axis=-1, keepdims=True)
            vals.append(m)
            idxs.append(ix)
            l = jnp.where(lanef == ix, -jnp.inf, l)
        es = [jnp.exp(v - vals[0]) for v in vals]
        inv = 1.0 / (es[0] + es[1] + es[2] + es[3])
        idx_out = jnp.zeros((rows, 128), F32)
        w_out = jnp.zeros((rows, 128), F32)
        for kk in range(TOP_K):
            idx_out = jnp.where(lane == kk, idxs[kk], idx_out)
            w_out = jnp.where(lane == kk, es[kk] * inv, w_out)
        idx_ref[c * rows:(c + 1) * rows, :] = idx_out.astype(jnp.int32)
        w_ref[c * rows:(c + 1) * rows, :] = w_out
        oh = [jnp.where(lanef == ix, 1.0, 0.0) for ix in idxs]
        onehots.append(oh)
        osums.append((oh[0] + oh[1]) + (oh[2] + oh[3]))

    osum = jnp.concatenate(osums, axis=0)
    ri = lax.broadcasted_iota(jnp.int32, (MERGE_TM, MERGE_TM), 0)
    ci = lax.broadcasted_iota(jnp.int32, (MERGE_TM, MERGE_TM), 1)
    before = jnp.where(ri > ci, 1.0, 0.0).astype(BF16)
    prior = _dot(before, osum.astype(BF16)) + cnt_ref[0:1, :]
    for c in range(ROUTE_CHUNKS):
        pc = prior[c * rows:(c + 1) * rows]
        rank_out = jnp.zeros((rows, 128), F32)
        for kk in range(TOP_K):
            rank_out = jnp.where(lane == kk, jnp.sum(onehots[c][kk] * pc, axis=-1, keepdims=True), rank_out)
        rank_ref[c * rows:(c + 1) * rows, :] = rank_out.astype(jnp.int32)
    cnt_ref[...] = cnt_ref[...] + jnp.sum(osum, axis=0, keepdims=True)


def _merge(branches, z, xc, xl, modr, w, layer):
    mrow = _mod_row(MERGE_TM)
    base = layer * N_MOD

    def mod_spec(part):
        return pl.BlockSpec((1, 1, D_MODEL), lambda i: ((base + mrow(i)) * 6 + part, 0, 0))

    const = lambda shape: pl.BlockSpec(shape, lambda i: (0,) * len(shape))
    row = lambda width: pl.BlockSpec((MERGE_TM, width), lambda i: (i, 0))
    gate = lambda col: pl.BlockSpec((MERGE_TM, 1024), lambda i: (i, col))
    return pl.pallas_call(
        _merge_kernel,
        out_shape=[jax.ShapeDtypeStruct((T_ALL, D_MODEL), F32), jax.ShapeDtypeStruct((T_ALL, D_MODEL), F32),
                   jax.ShapeDtypeStruct((T_ALL, 128), jnp.int32), jax.ShapeDtypeStruct((T_ALL, 128), F32),
                   jax.ShapeDtypeStruct((T_ALL, 128), jnp.int32), jax.ShapeDtypeStruct((8, 128), F32)],
        grid=(T_ALL // MERGE_TM,),
        in_specs=[_ctx_spec(MERGE_TM, 512), _lat_spec(MERGE_TM, 512)] * 3 + [gate(3), gate(4), gate(5),
                  _ctx_spec(MERGE_TM, D_MODEL), _lat_spec(MERGE_TM, D_MODEL),
                  const((3, 512, 1024)), const((1024, 1024)),
                  mod_spec(2), const((1, 1024)), mod_spec(3), mod_spec(4),
                  const((1024, 128)), const((1024, 128)), const((1, 128))],
        out_specs=[row(1024), row(1024), row(128), row(128), row(128), const((8, 128))],
        compiler_params=_params(1),
        name="merge",
    )(*branches, z, z, z, xc, xl, w["wb"], w["wo"], modr, w["n2_g"], modr, modr,
      w["rw_hi"], w["rw_lo"], w["rb"])


def _route(top_idx, rank, counts):
    flat_e = top_idx.reshape(N_SLOTS)
    onehot = (flat_e[:, None] == jnp.arange(N_EXPERTS, dtype=jnp.int32)[None, :]).astype(jnp.int32)
    padded = (counts + MOE_ROWS - 1) // MOE_ROWS * MOE_ROWS
    pad_end = jnp.cumsum(padded)
    pad_start = pad_end - padded
    dest = (rank.reshape(N_SLOTS) + jnp.sum(onehot * pad_start[None, :], axis=1)).astype(jnp.int32)
    blk_start = jnp.arange(N_MOE_BLOCKS, dtype=jnp.int32) * MOE_ROWS
    block_e = jnp.minimum(jnp.sum((pad_end[None, :] <= blk_start[:, None]).astype(jnp.int32), axis=1),
                          N_EXPERTS - 1).astype(jnp.int32)
    n_used = (pad_end[-1] // MOE_ROWS).astype(jnp.int32)
    e_hot = (block_e[:, None] == jnp.arange(N_EXPERTS, dtype=jnp.int32)[None, :]).astype(jnp.int32)
    nxt_blk = jnp.sum(e_hot * pad_end[None, :], axis=1) // MOE_ROWS
    b_hot = (nxt_blk[:, None] == jnp.arange(N_MOE_BLOCKS, dtype=jnp.int32)[None, :]).astype(jnp.int32)
    next_e = jnp.where(nxt_blk < n_used, jnp.sum(b_hot * block_e[None, :], axis=1), block_e).astype(jnp.int32)
    tail_start = (pad_start + counts).astype(jnp.int32)
    return dest, tail_start, block_e, n_used.reshape(1), next_e


def _store_rows(x, ref, blk0, rows):
    for g in range(D_MODEL // 128):
        ref[pl.ds(blk0, rows // 8), g] = x[:, 128 * g:128 * (g + 1)].reshape(rows // 8, 8, 128)


def _load_rows(ref, blk0, rows):
    return jnp.concatenate([ref[pl.ds(blk0, rows // 8), g].reshape(rows, 128) for g in range(D_MODEL // 128)],
                           axis=1)


def _row_at(ref, blk, sub):
    return ref.at[blk, :, pl.ds(sub, 1), :]


def _dispatch_kernel(dblk_ref, dsub_ref, tail_ref, nb_ref, h_ref, xs_ref, zero_buf, stage, sem, ssem):
    i = pl.program_id(0)

    @pl.when(i == 0)
    def _():
        zero_buf[...] = jnp.zeros_like(zero_buf)
        fills = [pltpu.make_async_copy(zero_buf, xs_ref.at[pl.ds(tail_ref[e] >> 3, MOE_ROWS // 8 + 1)], sem)
                 for e in range(N_EXPERTS)]
        for f in fills:
            f.start()
        for f in fills:
            f.wait()

        def fill_block(b, c):
            f = pltpu.make_async_copy(zero_buf.at[pl.ds(0, MOE_ROWS // 8)],
                                      xs_ref.at[pl.ds(b * (MOE_ROWS // 8), MOE_ROWS // 8)], sem)
            f.start()
            f.wait()
            return c
        lax.fori_loop(nb_ref[0], (N_MOE_ROWS + XS_EXTRA) // MOE_ROWS, fill_block, 0)

    base = i * TM * TOP_K
    cur = i % 2
    blk0 = cur * (TM // 8)

    def wait_tile(slot):
        for kk in range(TOP_K):
            pltpu.make_async_copy(stage.at[pl.ds(slot * (TM // 8), TM // 8)], xs_ref.at[pl.ds(0, TM // 8)],
                                  ssem.at[slot]).wait()

    @pl.when(i >= 2)
    def _():
        wait_tile(cur)
    _store_rows(h_ref[...], stage, blk0, TM)

    def issue(blk, c):
        for u in range(8):
            src = stage.at[blk0 + blk, :, pl.ds(u, 1), :]
            for kk in range(TOP_K):
                j = base + (blk * 8 + u) * TOP_K + kk
                dst = _row_at(xs_ref, dblk_ref[j], dsub_ref[j])
                pltpu.make_async_copy(src, dst, ssem.at[cur]).start(priority=kk % 2)
        return c
    lax.fori_loop(0, TM // 8, issue, 0)

    @pl.when(i == N_TILES - 1)
    def _():
        wait_tile(1 - cur)
        wait_tile(cur)


def _moe_dispatch(h, dest, tail_start, n_used):
    return pl.pallas_call(
        _dispatch_kernel,
        out_shape=jax.ShapeDtypeStruct(((N_MOE_ROWS + XS_EXTRA) // 8,) + ROW_BLOCK, F32),
        grid_spec=pltpu.PrefetchScalarGridSpec(
            num_scalar_prefetch=4, grid=(N_TILES,),
            in_specs=[pl.BlockSpec((TM, D_MODEL), lambda i, db, ds, t, nb: (i, 0))],
            out_specs=pl.BlockSpec(memory_space=pl.ANY),
            scratch_shapes=[pltpu.VMEM((MOE_ROWS // 8 + 1,) + ROW_BLOCK, F32),
                            pltpu.VMEM((2 * TM // 8,) + ROW_BLOCK, F32),
                            pltpu.SemaphoreType.DMA, pltpu.SemaphoreType.DMA((2,))]),
        compiler_params=_params(1),
        name="moe_dispatch",
    )(dest >> 3, dest & 7, tail_start, n_used, h)


def _expert_kernel(be_ref, nb_ref, nxt_ref, x_ref, wgu_hbm, bgu_ref, wd_hbm, bd_ref, o_ref,
                   wgu_st, wd_st, wgu_bf, wd_bf, sem, *, layer):
    i = pl.program_id(0)
    e = be_ref[i]
    prev = be_ref[jnp.maximum(i - 1, 0)]

    def weight_copies(expert):
        idx = layer * N_EXPERTS + expert
        return (pltpu.make_async_copy(wgu_hbm.at[idx], wgu_st, sem.at[0]),
                pltpu.make_async_copy(wd_hbm.at[idx], wd_st, sem.at[1]))

    @pl.when(i == 0)
    def _():
        for cp in weight_copies(e):
            cp.start()

    @pl.when(((i == 0) | (e != prev)) & (i < nb_ref[0]))
    def _():
        for cp in weight_copies(e):
            cp.wait()
        wgu_bf[...] = wgu_st[...].astype(BF16)
        wd_bf[...] = wd_st[...].astype(BF16)
        nxt = nxt_ref[i]

        @pl.when(nxt != e)
        def _():
            for cp in weight_copies(nxt):
                cp.start()

    @pl.when(i < nb_ref[0])
    def _():
        gu = _dot(_load_rows(x_ref, 0, MOE_ROWS).astype(BF16), wgu_bf[...]) + bgu_ref[0]
        gate = jnp.minimum(gu[:, :D_EXPERT], SWIGLU_LIMIT)
        up = jnp.clip(gu[:, D_EXPERT:], -SWIGLU_LIMIT, SWIGLU_LIMIT)
        act = (up + 1.0) * gate * _sigmoid(SWIGLU_ALPHA * gate)
        _store_rows(_dot(act.astype(BF16), wd_bf[...]) + bd_ref[0], o_ref, 0, MOE_ROWS)

    @pl.when(i >= nb_ref[0])
    def _():
        o_ref[...] = jnp.zeros_like(o_ref)


def _moe_experts(xs, block_e, n_used, next_e, w_gu, b_gu, w_down, b_down, layer):
    w_idx = lambda i, be, nb, nx: (layer * N_EXPERTS + be[i], 0, 0)
    return pl.pallas_call(
        functools.partial(_expert_kernel, layer=layer),
        out_shape=jax.ShapeDtypeStruct((N_MOE_ROWS // 8,) + ROW_BLOCK, F32),
        grid_spec=pltpu.PrefetchScalarGridSpec(
            num_scalar_prefetch=3, grid=(N_MOE_BLOCKS,),
            in_specs=[pl.BlockSpec((MOE_ROWS // 8,) + ROW_BLOCK,
                                   lambda i, be, nb, nx: (jnp.minimum(i, nb[0] - 1), 0, 0, 0)),
                      pl.BlockSpec(memory_space=pl.ANY),
                      pl.BlockSpec((1, 1, 2 * D_EXPERT), w_idx),
                      pl.BlockSpec(memory_space=pl.ANY),
                      pl.BlockSpec((1, 1, D_MODEL), w_idx)],
            out_specs=pl.BlockSpec((MOE_ROWS // 8,) + ROW_BLOCK, lambda i, be, nb, nx: (i, 0, 0, 0)),
            scratch_shapes=[pltpu.VMEM((D_MODEL, 2 * D_EXPERT), F32), pltpu.VMEM((D_EXPERT, D_MODEL), F32),
                            pltpu.VMEM((D_MODEL, 2 * D_EXPERT), BF16), pltpu.VMEM((D_EXPERT, D_MODEL), BF16),
                            pltpu.SemaphoreType.DMA((2,))]),
        compiler_params=_params(1),
        name="moe_experts",
    )(block_e, n_used, next_e, xs, w_gu, b_gu, w_down, b_down)


def _combine_kernel(dblk_ref, dsub_ref, x_ref, g2_ref, w_ref, eo_ref, yc_ref, yl_ref, buf, sem):
    i = pl.program_id(0)
    cur = i % 2

    def issue_tile(tile, slot):
        base = tile * TM * TOP_K
        blk0 = slot * (TOP_K * TM // 8)

        def issue(blk, c):
            for u in range(8):
                for kk in range(TOP_K):
                    j = base + (blk * 8 + u) * TOP_K + kk
                    src = _row_at(eo_ref, dblk_ref[j], dsub_ref[j])
                    dst = buf.at[blk0 + kk * (TM // 8) + blk, :, pl.ds(u, 1), :]
                    pltpu.make_async_copy(src, dst, sem.at[slot]).start(priority=kk % 2)
            return c
        lax.fori_loop(0, TM // 8, issue, 0)

    @pl.when(i == 0)
    def _():
        issue_tile(0, 0)

    @pl.when(i + 1 < N_TILES)
    def _():
        issue_tile(i + 1, 1 - cur)

    for kk in range(TOP_K):
        pltpu.make_async_copy(eo_ref.at[pl.ds(0, TM // 8)], buf.at[pl.ds(0, TM // 8)], sem.at[cur]).wait()
    w = w_ref[...]
    ff = None
    for kk in range(TOP_K):
        term = _load_rows(buf, cur * (TOP_K * TM // 8) + kk * (TM // 8), TM) * w[:, kk:kk + 1]
        ff = term if ff is None else ff + term
    y = x_ref[...] + g2_ref[0] * ff
    is_ctx = i < N_CTX_TILES

    @pl.when(is_ctx)
    def _():
        yc_ref[...] = y

    @pl.when(jnp.logical_not(is_ctx))
    def _():
        yl_ref[...] = y


def _moe_combine(dest, x1, modr, top_w, eo, layer):
    mrow = _mod_row(TM)
    base = layer * N_MOD
    return pl.pallas_call(
        _combine_kernel,
        out_shape=[jax.ShapeDtypeStruct((T_CTX, D_MODEL), F32), jax.ShapeDtypeStruct((T_LAT, D_MODEL), F32)],
        grid_spec=pltpu.PrefetchScalarGridSpec(
            num_scalar_prefetch=2, grid=(N_TILES,),
            in_specs=[pl.BlockSpec((TM, D_MODEL), lambda i, db, ds: (i, 0)),
                      pl.BlockSpec((1, 1, D_MODEL), lambda i, db, ds: ((base + mrow(i)) * 6 + 5, 0, 0)),
                      pl.BlockSpec((TM, 128), lambda i, db, ds: (i, 0)),
                      pl.BlockSpec(memory_space=pl.ANY)],
            out_specs=[pl.BlockSpec((TM, D_MODEL), lambda i, db, ds: (jnp.minimum(i, N_CTX_TILES - 1), 0)),
                       pl.BlockSpec((TM, D_MODEL), lambda i, db, ds: (jnp.maximum(i - N_CTX_TILES, 0), 0))],
            scratch_shapes=[pltpu.VMEM((2 * TOP_K * TM // 8,) + ROW_BLOCK, F32), pltpu.SemaphoreType.DMA((2,))]),
        compiler_params=_params(1),
        name="moe_combine",
    )(dest >> 3, dest & 7, x1, modr, top_w, eo)


def _pad_heads(w, n_heads, width):
    lead = w.shape[:-1]
    w = w.reshape(lead + (n_heads, width))
    w = jnp.pad(w, [(0, 0)] * len(lead) + [(0, 0), (0, HEAD_PAD - width)])
    return w.reshape(lead + (n_heads * HEAD_PAD,))


def _layer_weights(l, w_in, gla_wa2, gla_ba, mla_qa_g, mla_wuq, mla_kva_g, mla_wukv, mla_qn_g, mla_kn_g,
                   w_branch, w_out, router_w, router_b, norm2_g):
    wi = w_in[l]
    w_in_p = jnp.concatenate([wi[:, :3072], wi[:, 3520:], wi[:, 3104:3520], wi[:, 3072:3104],
                              jnp.zeros((D_MODEL, 64), F32)], axis=1).astype(BF16)
    wa_p = jnp.zeros((512, 512), F32)
    wa_p = wa_p.at[416:432, 0:256].set(gla_wa2[l, 0]).at[432:448, 256:512].set(gla_wa2[l, 1]).astype(BF16)
    ba_p = gla_ba[l].reshape(1, 512)
    wukv = mla_wukv[l].reshape(128, MLA_HEADS, MLA_NOPE + MLA_V)
    rw = jnp.pad(router_w[l], ((0, 0), (0, 128 - N_EXPERTS)))
    rw_hi = rw.astype(BF16)
    return {
        "w_in": w_in_p, "wa": wa_p, "ba": ba_p,
        "qa_g": mla_qa_g[l].reshape(1, 256),
        "wuq": _pad_heads(mla_wuq[l], MLA_HEADS, MLA_QK).astype(BF16),
        "qn_g": jnp.tile(jnp.pad(mla_qn_g[l], (0, HEAD_PAD - MLA_QK)), MLA_HEADS).reshape(1, 1024) * MLA_QK ** -0.5,
        "kn_g": jnp.tile(jnp.pad(mla_kn_g[l], (0, HEAD_PAD - MLA_QK)), MLA_HEADS).reshape(1, 1024),
        "kva_g": mla_kva_g[l].reshape(1, 128),
        "wk": _pad_heads(wukv[:, :, :MLA_NOPE].reshape(128, MLA_HEADS * MLA_NOPE), MLA_HEADS, MLA_NOPE).astype(BF16),
        "wv": wukv[:, :, MLA_NOPE:].reshape(128, MLA_HEADS * MLA_V).astype(BF16),
        "wb": w_branch[l].astype(BF16), "wo": w_out[l].astype(BF16),
        "rw_hi": rw_hi, "rw_lo": (rw - rw_hi.astype(F32)).astype(BF16),
        "rb": jnp.pad(router_b[l], (0, 128 - N_EXPERTS)).reshape(1, 128),
        "n2_g": norm2_g[l].reshape(1, D_MODEL),
    }


def kernel(x_prompt, x_sample, cache_mla_ckv, cache_mla_krope, state_ret, state_gla, c, c_ctx, w_mod, b_mod, norm1_g, norm2_g, w_in, ret_gn_g, gla_wa2, gla_ba, gla_norm_g, mla_qa_g, mla_wuq, mla_kva_g, mla_wukv, mla_qn_g, mla_kn_g, w_branch, w_out, router_w, router_b, moe_w_gu, moe_b_gu, moe_w_down, moe_b_down):
    xc, xl = x_prompt.reshape(T_CTX, D_MODEL), x_sample.reshape(T_LAT, D_MODEL)
    cc = jnp.concatenate([c_ctx[None, :], c, jnp.zeros((N_MOD - 1 - N_LAT_SEQ, D_MODEL), F32)], axis=0)
    modr = _modulation(cc, w_mod, b_mod).reshape(DEPTH * N_MOD * 6, 1, D_MODEL)
    rope_tabs = _rope_tables()
    w_gu = moe_w_gu.reshape(DEPTH * N_EXPERTS, D_MODEL, 2 * D_EXPERT)
    b_gu = moe_b_gu.reshape(DEPTH * N_EXPERTS, 1, 2 * D_EXPERT)
    w_dn = moe_w_down.reshape(DEPTH * N_EXPERTS, D_EXPERT, D_MODEL)
    b_dn = moe_b_down.reshape(DEPTH * N_EXPERTS, 1, D_MODEL)

    ckv_l, krope_l, ret_l, gla_l = [], [], [], []
    for l in range(DEPTH):
        w = _layer_weights(l, w_in, gla_wa2, gla_ba, mla_qa_g, mla_wuq, mla_kva_g, mla_wukv, mla_qn_g, mla_kn_g,
                           w_branch, w_out, router_w, router_b, norm2_g)
        z = _in_proj(xc, xl, norm1_g[l].reshape(1, D_MODEL), modr, w["w_in"], l)

        gn = ret_gn_g[l].reshape(1, 512)
        ret_c, ret_state = _retention(z, gn, None, ctx=True)
        (ret_s,) = _retention(z, gn, state_ret[:, l], ctx=False)
        gng = gla_norm_g[l].reshape(1, 512)
        gla_c, gla_state = _gla(z, w["wa"], w["ba"], gng, None, ctx=True)
        gla_s, _ = _gla(z, w["wa"], w["ba"], gng, state_gla[:, l], ctx=False)

        q, k, v, ckv, kr = _mla_prep(z, w, rope_tabs)
        kc, vc = _mla_cache(cache_mla_ckv[:, l].reshape(N_LAT_SEQ * CTX_LEN, 128),
                            jnp.pad(cache_mla_krope[:, l].reshape(N_LAT_SEQ * CTX_LEN, MLA_ROPE),
                                    ((0, 0), (0, 128 - MLA_ROPE))), w)
        mla_c = _mla_attn(q, k, v, None, None, ctx=True)
        mla_s = _mla_attn(q, k, v, kc, vc, ctx=False)

        x1, h2, top_idx, top_w, rank, cnt = _merge((ret_c, ret_s, gla_c, gla_s, mla_c, mla_s), z, xc, xl, modr, w, l)
        dest, tail_start, block_e, n_used, next_e = _route(top_idx[:, :TOP_K], rank[:, :TOP_K],
                                                           cnt[0, :N_EXPERTS].astype(jnp.int32))
        xs = _moe_dispatch(h2, dest, tail_start, n_used)
        eo = _moe_experts(xs, block_e, n_used, next_e, w_gu, b_gu, w_dn, b_dn, l)
        xc, xl = _moe_combine(dest, x1, modr, top_w, eo, l)

        ckv_l.append(ckv[:T_CTX].reshape(N_CTX_SEQ, CTX_LEN, 128))
        krope_l.append(kr[:T_CTX, :MLA_ROPE].reshape(N_CTX_SEQ, CTX_LEN, MLA_ROPE))
        ret_l.append(ret_state)
        gla_l.append(gla_state.reshape(N_CTX_SEQ, 2, N_HEADS, DK, DV))

    y_p = xc.reshape(N_CTX_SEQ, CTX_LEN, D_MODEL)
    y_s = xl.reshape(N_LAT_SEQ, LAT_LEN, D_MODEL)
    return (y_p, y_s, jnp.stack(ckv_l, axis=1), jnp.stack(krope_l, axis=1),
            jnp.stack(ret_l, axis=1), jnp.stack(gla_l, axis=1))
```

```python
import functools

import jax
import jax.numpy as jnp
import numpy as np
from jax import lax
from jax.experimental import pallas as pl
from jax.experimental.pallas import tpu as pltpu

F32 = jnp.float32
BF16 = jnp.bfloat16

D_MODEL = 1024
DEPTH = 2
N_CTX_SEQ, CTX_LEN = 32, 256
N_LAT_SEQ, LAT_LEN = 4, 1024
T_CTX = N_CTX_SEQ * CTX_LEN
T_LAT = N_LAT_SEQ * LAT_LEN
T_ALL = T_CTX + T_LAT
TM = 256
N_TILES = T_ALL // TM
N_CTX_TILES = T_CTX // TM
LAT_TILES = LAT_LEN // TM
N_MOD = 8
EPS = 1e-6

N_HEADS = 4
DK, DV = 64, 128
GRID_W = 64
MLA_HEADS, MLA_NOPE, MLA_ROPE, MLA_V = 8, 64, 32, 64
MLA_QK = MLA_NOPE + MLA_ROPE
HEAD_PAD = 128
GLA_TAU = 16.0
N_EXPERTS, TOP_K, D_EXPERT = 32, 4, 1024
SWIGLU_LIMIT, SWIGLU_ALPHA = 7.0, 1.702
MOE_ROWS = 512
N_SLOTS = T_ALL * TOP_K
N_MOE_BLOCKS = N_SLOTS // MOE_ROWS + N_EXPERTS
N_MOE_ROWS = N_MOE_BLOCKS * MOE_ROWS
MERGE_TM = 512
ROUTE_CHUNKS = 8
XS_EXTRA = MOE_ROWS
ROW_TILE = (D_MODEL // 128, 128)

DZ = 6656
IN_TILE = 512
VMEM_LIMIT = 56 * 1024 * 1024

RET_LOG_F = [float(np.log1p(-np.exp2(-(5.0 + h)))) for h in range(N_HEADS)]
RET_LOG_B = [float(np.log1p(-np.exp2(-(5.5 + h)))) for h in range(N_HEADS)]


def _params(n_axes, vmem=VMEM_LIMIT):
    return pltpu.CompilerParams(dimension_semantics=("arbitrary",) * n_axes, vmem_limit_bytes=vmem)


def _sigmoid(x):
    return 1.0 / (1.0 + jnp.exp(-x))


def _dot(a, b):
    return jnp.dot(a, b, preferred_element_type=F32)


def _dot_t(a, b):
    return lax.dot_general(a, b, (((1,), (1,)), ((), ())), preferred_element_type=F32)


def _mod_row(tile_rows):
    def f(i):
        r0 = i * tile_rows
        return jnp.where(r0 < T_CTX, 0, 1 + (r0 - T_CTX) // LAT_LEN)
    return f


def _ctx_spec(rows, width):
    n_ctx = T_CTX // rows
    return pl.BlockSpec((rows, width), lambda i: (jnp.minimum(i, n_ctx - 1), 0))


def _lat_spec(rows, width):
    n_ctx = T_CTX // rows
    return pl.BlockSpec((rows, width), lambda i: (jnp.maximum(i - n_ctx, 0), 0))


def _mod_kernel(c_ref, w_ref, b_ref, o_ref):
    c = c_ref[...]
    s = c * _sigmoid(c)
    o_ref[0] = jnp.dot(s, w_ref[0], preferred_element_type=F32, precision=lax.Precision.HIGHEST) + b_ref[0]


def _modulation(cc, w_mod, b_mod):
    n = 6 * D_MODEL
    blk = 2048
    return pl.pallas_call(
        _mod_kernel,
        out_shape=jax.ShapeDtypeStruct((DEPTH, N_MOD, n), F32),
        grid=(DEPTH, n // blk),
        in_specs=[pl.BlockSpec((N_MOD, D_MODEL), lambda l, j: (0, 0)),
                  pl.BlockSpec((1, D_MODEL, blk), lambda l, j: (l, 0, j)),
                  pl.BlockSpec((1, 1, blk), lambda l, j: (l, 0, j))],
        out_specs=pl.BlockSpec((1, N_MOD, blk), lambda l, j: (l, 0, j)),
        compiler_params=_params(2),
        name="modulation",
    )(cc, w_mod, b_mod.reshape(DEPTH, 1, n))


def _in_kernel(xc_ref, xl_ref, g_ref, sh_ref, sc_ref, w_ref, o_ref):
    x = jnp.where(pl.program_id(0) < T_CTX // IN_TILE, xc_ref[...], xl_ref[...])
    h = x * lax.rsqrt(jnp.mean(x * x, axis=-1, keepdims=True) + EPS) * g_ref[...]
    h = h * (1.0 + sc_ref[0]) + sh_ref[0]
    hb = h.astype(BF16)
    for n0 in range(0, DZ, 512):
        o_ref[:, n0:n0 + 512] = _dot(hb, w_ref[:, n0:n0 + 512]).astype(BF16)


def _in_proj(xc, xl, g, modr, w_in_p, layer):
    mrow = _mod_row(IN_TILE)
    base = layer * N_MOD

    def mod_spec(part):
        return pl.BlockSpec((1, 1, D_MODEL), lambda i: ((base + mrow(i)) * 6 + part, 0, 0))

    return pl.pallas_call(
        _in_kernel,
        out_shape=jax.ShapeDtypeStruct((T_ALL, DZ), BF16),
        grid=(T_ALL // IN_TILE,),
        in_specs=[_ctx_spec(IN_TILE, D_MODEL), _lat_spec(IN_TILE, D_MODEL),
                  pl.BlockSpec((1, D_MODEL), lambda i: (0, 0)),
                  mod_spec(0), mod_spec(1),
                  pl.BlockSpec((D_MODEL, DZ), lambda i: (0, 0))],
        out_specs=pl.BlockSpec((IN_TILE, DZ), lambda i: (i, 0)),
        compiler_params=_params(1),
        name="in_proj",
    )(xc, xl, g, modr, modr, w_in_p)


def _lane_half_mask(hh):
    lane = lax.broadcasted_iota(jnp.int32, (1, 128), 1)
    return (lane < 64) if hh == 0 else (lane >= 64)


@functools.lru_cache(maxsize=None)
def _ret_decay_table(seq):
    d = np.arange(seq)[:, None] - np.arange(seq)[None, :]
    tab = np.stack([np.exp(np.where(d > 0, RET_LOG_F[h] * d, -RET_LOG_B[h] * d)) for h in range(N_HEADS)])
    return (tab * np.where(d == 0, 2.0, 1.0) * DK ** -0.5).astype(np.float32)


def _ret_kernel(*refs, seq, has_state, emit_state):
    q_ref, k_ref, v_ref, g_ref, gn_ref, dec_ref = refs[:6]
    pos = 6
    if has_state:
        s0_ref = refs[pos]
        pos += 1
    o_ref = refs[pos]
    pos += 1
    if emit_state:
        st_ref = refs[pos]

    r0 = pl.multiple_of(pl.program_id(1) * TM, TM)
    qb = q_ref[pl.ds(r0, TM), :]
    ri = (lax.broadcasted_iota(jnp.int32, (TM, 1), 0) + r0).astype(F32)

    for h in range(N_HEADS):
        p, hh = h // 2, h % 2
        lanes = slice(128 * p, 128 * p + 128)
        qp = qb[:, lanes]
        qh = jnp.where(_lane_half_mask(hh), qp, jnp.zeros_like(qp))
        sc = _dot_t(qh, k_ref[:, lanes])
        o = _dot((sc * dec_ref[h]).astype(BF16), v_ref[:, 128 * h:128 * h + 128])
        if has_state:
            qf = qh.astype(F32)
            o += _dot((qf * jnp.exp(RET_LOG_F[h] * (ri + 1.0))).astype(BF16), s0_ref[0, 0, p].astype(BF16))
            o += _dot((qf * jnp.exp(RET_LOG_B[h] * (seq - ri))).astype(BF16), s0_ref[0, 1, p].astype(BF16))
        mu = jnp.mean(o, axis=-1, keepdims=True)
        d = o - mu
        var = jnp.mean(d * d, axis=-1, keepdims=True)
        on = d * lax.rsqrt(var + EPS)
        g = g_ref[:, 128 * h:128 * h + 128].astype(F32)
        out = on * gn_ref[:, 128 * h:128 * h + 128] * (g * _sigmoid(g))
        o_ref[:, 128 * h:128 * h + 128] = out.astype(BF16)

    if emit_state:
        jc = lax.broadcasted_iota(jnp.int32, (seq, 1), 0).astype(F32)
        lane = lax.broadcasted_iota(jnp.int32, (1, 128), 1)
        for p in range(2):
            kp = k_ref[:, 128 * p:128 * p + 128].astype(F32) * DK ** -0.5
            lgf = jnp.where(lane < 64, RET_LOG_F[2 * p], RET_LOG_F[2 * p + 1])
            lgb = jnp.where(lane < 64, RET_LOG_B[2 * p], RET_LOG_B[2 * p + 1])
            kdf = (kp * jnp.exp(lgf * (seq - 1.0 - jc))).T.astype(BF16)
            kdb = (kp * jnp.exp(lgb * jc)).T.astype(BF16)
            for hh in range(2):
                h = 2 * p + hh
                vh = v_ref[:, 128 * h:128 * h + 128]
                st_ref[0, 0, h] = _dot(kdf, vh)[64 * hh:64 * hh + 64, :]
                st_ref[0, 1, h] = _dot(kdb, vh)[64 * hh:64 * hh + 64, :]


def _retention(z, gn_g, s0, *, ctx):
    if ctx:
        nb, seq, row_blk, tile0 = N_CTX_SEQ, CTX_LEN, 0, 0
    else:
        nb, seq, row_blk, tile0 = N_LAT_SEQ, LAT_LEN, T_CTX // LAT_LEN, N_CTX_TILES
    nq = seq // TM
    in_specs = [pl.BlockSpec((seq, 256), lambda b, i: (row_blk + b, 0)),
                pl.BlockSpec((seq, 256), lambda b, i: (row_blk + b, 1)),
                pl.BlockSpec((seq, 512), lambda b, i: (row_blk + b, 1)),
                pl.BlockSpec((TM, 512), lambda b, i: (tile0 + b * nq + i, 2)),
                pl.BlockSpec((1, 512), lambda b, i: (0, 0)),
                pl.BlockSpec((N_HEADS, TM, seq), lambda b, i: (0, i, 0))]
    args = [z, z, z, z, gn_g, _ret_decay_table(seq)]
    out_shape = [jax.ShapeDtypeStruct((nb * seq, 512), BF16)]
    out_specs = [pl.BlockSpec((TM, 512), lambda b, i: (b * nq + i, 0))]
    if not ctx:
        in_specs.append(pl.BlockSpec((1, 2, 2, 128, 128), lambda b, i: (b, 0, 0, 0, 0)))
        args.append(s0.reshape(N_LAT_SEQ, 2, 2, 128, 128))
    else:
        out_shape.append(jax.ShapeDtypeStruct((nb, 2, N_HEADS, DK, DV), F32))
        out_specs.append(pl.BlockSpec((1, 2, N_HEADS, DK, DV), lambda b, i: (b, 0, 0, 0, 0)))
    return pl.pallas_call(
        functools.partial(_ret_kernel, seq=seq, has_state=not ctx, emit_state=ctx),
        out_shape=out_shape, grid=(nb, nq), in_specs=in_specs, out_specs=out_specs,
        compiler_params=_params(2),
        name="retention_ctx" if ctx else "retention_lat",
    )(*args)


def _gla_decay(small_ref, wa_ref, ba_ref):
    x = _dot(small_ref[...], wa_ref[...]) + ba_ref[...]
    la = -(jnp.maximum(-x, 0.0) + jnp.log(1.0 + jnp.exp(-jnp.abs(x)))) * (1.0 / GLA_TAU)
    ri = lax.broadcasted_iota(jnp.int32, (TM, TM), 0)
    ci = lax.broadcasted_iota(jnp.int32, (TM, TM), 1)
    ltri = jnp.where(ri >= ci, 1.0, 0.0).astype(BF16)
    hi = la.astype(BF16)
    r1 = la - hi.astype(F32)
    mid = r1.astype(BF16)
    lo = (r1 - mid.astype(F32)).astype(BF16)
    cum = _dot(ltri, hi) + _dot(ltri, mid) + _dot(ltri, lo)
    return la, cum


def _gla_state_kernel(k_ref, v_ref, small_ref, wa_ref, ba_ref, kv_ref, tot_ref):
    la, cum = _gla_decay(small_ref, wa_ref, ba_ref)
    bf, bb = cum[:, :256], cum[:, 256:]
    xb = bb - la[:, 256:]
    k = k_ref[...].astype(F32)
    kdf = k * jnp.exp(bf[TM - 1:TM, :] - bf)
    kdb = k * jnp.exp(xb)
    for p in range(2):
        kf_t = kdf[:, 128 * p:128 * p + 128].T.astype(BF16)
        kb_t = kdb[:, 128 * p:128 * p + 128].T.astype(BF16)
        for hh in range(2):
            h = 2 * p + hh
            vh = v_ref[:, 128 * h:128 * h + 128]
            kv_ref[0, 0, 0, h] = _dot(kf_t, vh)[64 * hh:64 * hh + 64, :]
            kv_ref[0, 0, 1, h] = _dot(kb_t, vh)[64 * hh:64 * hh + 64, :]
    tot_ref[0, 0] = jnp.sum(la.T, axis=-1, keepdims=True)


def _mid_bcast(x, s, r):
    w = 2 * s
    if w >= 8:
        n = TM // w
        x3 = x.reshape(n, w, 256)
        return jnp.broadcast_to(x3[:, r:r + 1, :], (n, w, 256)).reshape(TM, 256)
    x3 = x.reshape(TM // 8, 8, 256)
    sub = lax.broadcasted_iota(jnp.int32, (1, 8, 1), 1)
    out = None
    for blk in range(8 // w):
        rowv = jnp.broadcast_to(x3[:, blk * w + r:blk * w + r + 1, :], (TM // 8, 8, 256))
        out = rowv if out is None else jnp.where(sub >= blk * w, rowv, out)
    return out.reshape(TM, 256)


def _gla_kernel(*refs, n_blk, has_state, emit_state):
    q_ref, k_ref, v_ref, g_ref, small_ref, wa_ref, ba_ref, gn_ref = refs[:8]
    pos = 8
    if has_state:
        kv_ref, tot_ref, s0_ref = refs[pos:pos + 3]
        pos += 3
    o_ref = refs[pos]
    pos += 1
    if emit_state:
        kvo_ref = refs[pos]

    la, cum = _gla_decay(small_ref, wa_ref, ba_ref)
    bf, bb = cum[:, :256], cum[:, 256:]
    xb = bb - la[:, 256:]
    q = q_ref[...].astype(F32) * DK ** -0.5
    k = k_ref[...].astype(F32)
    row = lax.broadcasted_iota(jnp.int32, (TM, 1), 0)
    rowi = lax.broadcasted_iota(jnp.int32, (TM, TM), 0)
    colj = lax.broadcasted_iota(jnp.int32, (TM, TM), 1)
    low_half = _lane_half_mask(0)

    def join(fwd, bwd):
        ops = []
        for p in range(N_HEADS // 2):
            f = fwd[:, 128 * p:128 * p + 128]
            br = pltpu.roll(bwd[:, 128 * p:128 * p + 128], 64, 1)
            ops.append(jnp.where(low_half, f, br).astype(BF16))
            ops.append(jnp.where(low_half, br, f).astype(BF16))
        return ops

    qd, kd = join(q, q), join(k, k)
    acc = [jnp.where(rowi == colj, _dot_t(qo, ko), 0.0) for qo, ko in zip(qd, kd)]

    low_f = jnp.where(low_half, 1.0, 0.0)
    s = 1
    while s < TM:
        up_f = jnp.where(((row // s) % 2) == 1, 1.0, 0.0)
        live_even = jnp.where(up_f == low_f, 1.0, 0.0).astype(BF16)
        live = [live_even, 1.0 - live_even]
        dead = [live[1], live[0]]
        af = -jnp.abs(bf - _mid_bcast(bf, s, s - 1))
        ab = -jnp.abs(xb - _mid_bcast(xb, s, s))
        same = (rowi // (2 * s)) == (colj // (2 * s))
        for p in range(N_HEADS // 2):
            a_f = af[:, 128 * p:128 * p + 128]
            a_b = pltpu.roll(ab[:, 128 * p:128 * p + 128], 64, 1)
            for hh in range(2):
                h = 2 * p + hh
                arg = jnp.where(low_half, a_f, a_b) if hh == 0 else jnp.where(low_half, a_b, a_f)
                e = jnp.exp(arg).astype(BF16)
                sl = _dot_t(e * (qd[h] * live[hh]), e * (kd[h] * dead[hh]))
                acc[h] = acc[h] + (jnp.where(same, sl, 0.0) if 2 * s < TM else sl)
        s *= 2

    if has_state:
        n = pl.program_id(1)
        q_state = join(q * jnp.exp(bf), q * jnp.exp(bb[TM - 1:TM, :] - xb))

    if emit_state:
        kdf = k * jnp.exp(bf[TM - 1:TM, :] - bf)
        kdb = k * jnp.exp(xb)
        for p in range(N_HEADS // 2):
            kf_t = kdf[:, 128 * p:128 * p + 128].T.astype(BF16)
            kb_t = kdb[:, 128 * p:128 * p + 128].T.astype(BF16)
            for hh in range(2):
                h = 2 * p + hh
                vh = v_ref[:, 128 * h:128 * h + 128]
                kvo_ref[0, 0, 0, h] = _dot(kf_t, vh)[64 * hh:64 * hh + 64, :]
                kvo_ref[0, 0, 1, h] = _dot(kb_t, vh)[64 * hh:64 * hh + 64, :]

    for h in range(N_HEADS):
        o = _dot(acc[h].astype(BF16), v_ref[:, 128 * h:128 * h + 128])
        if has_state:
            sf = s0_ref[0, 0, h]
            for m in range(n_blk - 1):
                dec = jnp.exp(tot_ref[0, m, 64 * h:64 * h + 64, :])
                sf = jnp.where(m < n, dec * sf + kv_ref[0, m, 0, h], sf)
            sb = s0_ref[0, 1, h]
            for m in range(n_blk - 1, 0, -1):
                dec = jnp.exp(tot_ref[0, m, 256 + 64 * h:256 + 64 * h + 64, :])
                sb = jnp.where(m > n, dec * sb + kv_ref[0, m, 1, h], sb)
            state = jnp.concatenate([sf, sb] if h % 2 == 0 else [sb, sf], axis=0).astype(BF16)
            o += _dot(q_state[h], state)
        on = o * lax.rsqrt(jnp.mean(o * o, axis=-1, keepdims=True) + EPS)
        g = g_ref[:, 128 * h:128 * h + 128].astype(F32)
        out = on * gn_ref[:, 128 * h:128 * h + 128] * (g * _sigmoid(g))
        o_ref[:, 128 * h:128 * h + 128] = out.astype(BF16)


def _gla(z, wa_p, ba_p, gn_g, s0, *, ctx):
    if ctx:
        nb, n_blk, tile0 = N_CTX_SEQ, 1, 0
    else:
        nb, n_blk, tile0 = N_LAT_SEQ, LAT_TILES, N_CTX_TILES

    def zspec(width, col):
        return pl.BlockSpec((TM, width), lambda b, n: (tile0 + b * n_blk + n, col))

    w_specs = [pl.BlockSpec((512, 512), lambda b, n: (0, 0)), pl.BlockSpec((1, 512), lambda b, n: (0, 0))]
    kv_shape = jax.ShapeDtypeStruct((nb, n_blk, 2, N_HEADS, DK, DV), F32)
    kv_spec = pl.BlockSpec((1, 1, 2, N_HEADS, DK, DV), lambda b, n: (b, n, 0, 0, 0, 0))
    in_specs = [zspec(256, 6), zspec(256, 7), zspec(512, 4), zspec(512, 5), zspec(512, 12)] + w_specs
    in_specs.append(pl.BlockSpec((1, 512), lambda b, n: (0, 0)))
    args = [z, z, z, z, z, wa_p, ba_p, gn_g]
    out_shape = [jax.ShapeDtypeStruct((nb * n_blk * TM, 512), BF16)]
    out_specs = [pl.BlockSpec((TM, 512), lambda b, n: (b * n_blk + n, 0))]
    if ctx:
        out_shape.append(kv_shape)
        out_specs.append(kv_spec)
    else:
        kv, tot = pl.pallas_call(
            _gla_state_kernel,
            out_shape=[kv_shape, jax.ShapeDtypeStruct((nb, n_blk, 512, 1), F32)],
            grid=(nb, n_blk),
            in_specs=[zspec(256, 7), zspec(512, 4), zspec(512, 12)] + w_specs,
            out_specs=[kv_spec, pl.BlockSpec((1, 1, 512, 1), lambda b, n: (b, n, 0, 0))],
            compiler_params=_params(2),
            name="gla_state_lat",
        )(z, z, z, wa_p, ba_p)
        in_specs += [pl.BlockSpec((1, n_blk, 2, N_HEADS, DK, DV), lambda b, n: (b, 0, 0, 0, 0, 0)),
                     pl.BlockSpec((1, n_blk, 512, 1), lambda b, n: (b, 0, 0, 0)),
                     pl.BlockSpec((1, 2, N_HEADS, DK, DV), lambda b, n: (b, 0, 0, 0, 0))]
        args += [kv, tot, s0]
    res = pl.pallas_call(
        functools.partial(_gla_kernel, n_blk=n_blk, has_state=not ctx, emit_state=ctx),
        out_shape=out_shape, grid=(nb, n_blk), in_specs=in_specs, out_specs=out_specs,
        compiler_params=_params(2),
        name="gla_ctx" if ctx else "gla_lat",
    )(*args)
    return (res[0], res[1]) if ctx else (res[0], None)


def _rope_tables():
    nf = MLA_ROPE // 4
    pos = np.arange(LAT_LEN)
    freqs = (10000.0 ** (-np.arange(nf, dtype=np.float32) / nf)).astype(np.float32)
    ang_r = ((pos // GRID_W).astype(np.float32)[:, None] * freqs).astype(np.float32)
    ang_c = ((pos % GRID_W).astype(np.float32)[:, None] * freqs).astype(np.float32)
    cos = np.ones((TM + LAT_LEN, HEAD_PAD), np.float32)
    sa = np.zeros((TM + LAT_LEN, HEAD_PAD), np.float32)
    sb = np.zeros((TM + LAT_LEN, HEAD_PAD), np.float32)
    o = MLA_NOPE
    for base, ang in ((o, ang_r), (o + 2 * nf, ang_c)):
        cos[TM:, base:base + nf] = np.cos(ang)
        cos[TM:, base + nf:base + 2 * nf] = np.cos(ang)
        sa[TM:, base:base + nf] = -np.sin(ang)
        sb[TM:, base + nf:base + 2 * nf] = np.sin(ang)
    return jnp.asarray(cos), jnp.asarray(sa), jnp.asarray(sb)


def _rope(x, cos, sa, sb):
    return x * cos + pltpu.roll(x, 128 - 8, 1) * sa + pltpu.roll(x, 8, 1) * sb


def _head_segments():
    seg = np.zeros((MLA_HEADS * HEAD_PAD, 128), np.float32)
    for h in range(MLA_HEADS):
        seg[h * HEAD_PAD:(h + 1) * HEAD_PAD, h] = 1.0
    return jnp.asarray(seg, BF16), jnp.asarray(seg.T.copy(), BF16)


def _head_norm(x, gain, seg_ref, segt_ref):
    ss = _dot((x * x).astype(BF16), seg_ref[...])
    rs = lax.rsqrt(ss * (1.0 / MLA_QK) + EPS)
    hi = rs.astype(BF16)
    lo = (rs - hi.astype(F32)).astype(BF16)
    return x * (_dot(hi, segt_ref[...]) + _dot(lo, segt_ref[...])) * gain


def _store_heads(x, o_ref, rope):
    if rope is None:
        o_ref[...] = x.astype(BF16)
    else:
        for h in range(MLA_HEADS):
            o_ref[:, 128 * h:128 * h + 128] = _rope(x[:, 128 * h:128 * h + 128], *rope).astype(BF16)


def _mla_keys(ckv, kr_tile, wk_ref, wv_ref, kn_ref, seg_ref, segt_ref, k_ref, v_ref, rope=None):
    cb = ckv.astype(BF16)
    kpre = _dot(cb, wk_ref[...]) + jnp.concatenate([kr_tile] * MLA_HEADS, axis=1)
    v_ref[...] = _dot(cb, wv_ref[...]).astype(BF16)
    _store_heads(_head_norm(kpre, kn_ref[...], seg_ref, segt_ref), k_ref, rope)


def _mla_prep_kernel(small_ref, qa_ref, wuq_ref, qn_ref, kva_ref, wk_ref, wv_ref, kn_ref, seg_ref, segt_ref,
                     cos_ref, sa_ref, sb_ref, q_ref, k_ref, v_ref, ckv_ref, kr_ref):
    def body(rope):
        sm = small_ref[...].astype(F32)
        cq, ckv_raw, g3 = sm[:, 0:256], sm[:, 256:384], sm[:, 384:512]
        cqn = cq * lax.rsqrt(jnp.mean(cq * cq, axis=-1, keepdims=True) + EPS) * qa_ref[...]
        q = _dot(cqn.astype(BF16), wuq_ref[...])
        _store_heads(_head_norm(q, qn_ref[...], seg_ref, segt_ref), q_ref, rope)
        ckv = ckv_raw * lax.rsqrt(jnp.mean(ckv_raw * ckv_raw, axis=-1, keepdims=True) + EPS) * kva_ref[...]
        ckv_ref[...] = ckv
        lane = lax.broadcasted_iota(jnp.int32, (1, 128), 1)
        kr = jnp.where(lane < MLA_ROPE, g3, 0.0)
        kr_ref[...] = kr
        _mla_keys(ckv, pltpu.roll(kr, MLA_NOPE, 1), wk_ref, wv_ref, kn_ref, seg_ref, segt_ref, k_ref, v_ref, rope)

    is_ctx = pl.program_id(0) < N_CTX_TILES

    @pl.when(is_ctx)
    def _():
        body(None)

    @pl.when(jnp.logical_not(is_ctx))
    def _():
        body((cos_ref[...], sa_ref[...], sb_ref[...]))


def _mla_cache_kernel(ckv_ref, kr_ref, wk_ref, wv_ref, kn_ref, seg_ref, segt_ref, k_ref, v_ref):
    _mla_keys(ckv_ref[...], pltpu.roll(kr_ref[...], MLA_NOPE, 1), wk_ref, wv_ref, kn_ref, seg_ref, segt_ref,
              k_ref, v_ref)


def _mla_prep(z, w, rope_tabs):
    def rope_blk(i):
        return jnp.where(i < N_CTX_TILES, 0, 1 + (i - N_CTX_TILES) % LAT_TILES)

    const = lambda shape: pl.BlockSpec(shape, lambda i: (0,) * len(shape))
    rope_spec = pl.BlockSpec((TM, HEAD_PAD), lambda i: (rope_blk(i), 0))
    row = lambda width: pl.BlockSpec((TM, width), lambda i: (i, 0))
    return pl.pallas_call(
        _mla_prep_kernel,
        out_shape=[jax.ShapeDtypeStruct((T_ALL, 1024), BF16), jax.ShapeDtypeStruct((T_ALL, 1024), BF16),
                   jax.ShapeDtypeStruct((T_ALL, 512), BF16), jax.ShapeDtypeStruct((T_ALL, 128), F32),
                   jax.ShapeDtypeStruct((T_ALL, 128), F32)],
        grid=(N_TILES,),
        in_specs=[pl.BlockSpec((TM, 512), lambda i: (i, 12)),
                  const((1, 256)), const((256, 1024)), const((1, 1024)), const((1, 128)),
                  const((128, 1024)), const((128, 512)), const((1, 1024)),
                  const((1024, 128)), const((128, 1024)),
                  rope_spec, rope_spec, rope_spec],
        out_specs=[row(1024), row(1024), row(512), row(128), row(128)],
        compiler_params=_params(1),
        name="mla_prep",
    )(z, w["qa_g"], w["wuq"], w["qn_g"], w["kva_g"], w["wk"], w["wv"], w["kn_g"], *_head_segments(), *rope_tabs)


def _mla_cache(ckv, kr_pad, w):
    const = lambda shape: pl.BlockSpec(shape, lambda i: (0,) * len(shape))
    row = lambda width: pl.BlockSpec((TM, width), lambda i: (i, 0))
    n = ckv.shape[0]
    return pl.pallas_call(
        _mla_cache_kernel,
        out_shape=[jax.ShapeDtypeStruct((n, 1024), BF16), jax.ShapeDtypeStruct((n, 512), BF16)],
        grid=(n // TM,),
        in_specs=[row(128), row(128), const((128, 1024)), const((128, 512)), const((1, 1024)),
                  const((1024, 128)), const((128, 1024))],
        out_specs=[row(1024), row(512)],
        compiler_params=_params(1),
        name="mla_cache",
    )(ckv, kr_pad, w["wk"], w["wv"], w["kn_g"], *_head_segments())


def _mla_attn_kernel(*refs, has_cache):
    q_ref, k_ref, v_ref = refs[:3]
    pos = 3
    if has_cache:
        kc_ref, vc_ref = refs[3:5]
        pos = 5
    o_ref = refs[pos]
    for p in range(MLA_HEADS // 2):
        acc = jnp.zeros((TM, 128), F32)
        for hh in range(2):
            h = 2 * p + hh
            lanes = slice(128 * h, 128 * h + 128)
            qh = q_ref[:, lanes]
            l1 = _dot_t(qh, k_ref[:, lanes])
            m = jnp.max(l1, axis=-1, keepdims=True)
            if has_cache:
                l0 = _dot_t(qh, kc_ref[:, lanes])
                m = jnp.maximum(m, jnp.max(l0, axis=-1, keepdims=True))
                p0 = jnp.exp(l0 - m)
            p1 = jnp.exp(l1 - m)
            den = jnp.sum(p1, axis=-1, keepdims=True)
            if has_cache:
                den = den + jnp.sum(p0, axis=-1, keepdims=True)
            mask = _lane_half_mask(hh)
            vp = v_ref[:, 128 * p:128 * p + 128]
            o = _dot(p1.astype(BF16), jnp.where(mask, vp, jnp.zeros_like(vp)))
            if has_cache:
                vcp = vc_ref[:, 128 * p:128 * p + 128]
                o += _dot(p0.astype(BF16), jnp.where(mask, vcp, jnp.zeros_like(vcp)))
            acc += o * (1.0 / den)
        o_ref[:, 128 * p:128 * p + 128] = acc.astype(BF16)


def _mla_attn(q, k, v, kc, vc, *, ctx):
    if ctx:
        nb, seq, row_blk, tile0 = N_CTX_SEQ, CTX_LEN, 0, 0
    else:
        nb, seq, row_blk, tile0 = N_LAT_SEQ, LAT_LEN, T_CTX // LAT_LEN, N_CTX_TILES
    nq = seq // TM
    in_specs = [pl.BlockSpec((TM, 1024), lambda b, i: (tile0 + b * nq + i, 0)),
                pl.BlockSpec((seq, 1024), lambda b, i: (row_blk + b, 0)),
                pl.BlockSpec((seq, 512), lambda b, i: (row_blk + b, 0))]
    args = [q, k, v]
    if not ctx:
        in_specs += [pl.BlockSpec((TM, 1024), lambda b, i: (b, 0)), pl.BlockSpec((TM, 512), lambda b, i: (b, 0))]
        args += [kc, vc]
    return pl.pallas_call(
        functools.partial(_mla_attn_kernel, has_cache=not ctx),
        out_shape=jax.ShapeDtypeStruct((nb * seq, 512), BF16),
        grid=(nb, nq), in_specs=in_specs,
        out_specs=pl.BlockSpec((TM, 512), lambda b, i: (b * nq + i, 0)),
        compiler_params=_params(2),
        name="mla_attn_ctx" if ctx else "mla_attn_lat",
    )(*args)


def _merge_kernel(retc_ref, retl_ref, glac_ref, glal_ref, mlac_ref, mlal_ref, m0_ref, m1_ref, m2_ref,
                  xc_ref, xl_ref, wb_ref, wo_ref,
                  g1_ref, n2_ref, sh2_ref, sc2_ref, rwh_ref, rwl_ref, rb_ref,
                  x1_ref, h_ref, idx_ref, w_ref, rank_ref, cnt_ref):
    @pl.when(pl.program_id(0) == 0)
    def _():
        cnt_ref[...] = jnp.zeros_like(cnt_ref)

    is_ctx = pl.program_id(0) < T_CTX // MERGE_TM
    mix = None
    for c_ref, l_ref, m_ref, n in ((retc_ref, retl_ref, m0_ref, 0), (glac_ref, glal_ref, m1_ref, 1),
                                   (mlac_ref, mlal_ref, m2_ref, 2)):
        branch = jnp.where(is_ctx, c_ref[...], l_ref[...])
        term = _sigmoid(m_ref[...]).astype(F32) * _dot(branch, wb_ref[n])
        mix = term if mix is None else mix + term
    out = _dot(mix.astype(BF16), wo_ref[...])
    x1 = jnp.where(is_ctx, xc_ref[...], xl_ref[...]) + g1_ref[0] * out
    x1_ref[...] = x1
    h = x1 * lax.rsqrt(jnp.mean(x1 * x1, axis=-1, keepdims=True) + EPS) * n2_ref[...]
    h = h * (1.0 + sc2_ref[0]) + sh2_ref[0]
    h_ref[...] = h
    hh = h.astype(BF16)
    hl = (h - hh.astype(F32)).astype(BF16)
    logits = _dot(hh, rwh_ref[...]) + _dot(hh, rwl_ref[...]) + _dot(hl, rwh_ref[...]) + rb_ref[...]
    rows = MERGE_TM // ROUTE_CHUNKS
    lane = lax.broadcasted_iota(jnp.int32, (rows, 128), 1)
    lanef = lane.astype(F32)
    onehots, osums = [], []
    for c in range(ROUTE_CHUNKS):
        l = jnp.where(lane < N_EXPERTS, logits[c * rows:(c + 1) * rows], -jnp.inf)
        vals, idxs = [], []
        for _ in range(TOP_K):
            m = jnp.max(l, axis=-1, keepdims=True)
            ix = jnp.min(jnp.where(l == m, lanef, 128.0), axis=-1, keepdims=True)
            vals.append(m)
            idxs.append(ix)
            l = jnp.where(lanef == ix, -jnp.inf, l)
        es = [jnp.exp(v - vals[0]) for v in vals]
        inv = 1.0 / (es[0] + es[1] + es[2] + es[3])
        idx_out = jnp.zeros((rows, 128), F32)
        w_out = jnp.zeros((rows, 128), F32)
        for kk in range(TOP_K):
            idx_out = jnp.where(lane == kk, idxs[kk], idx_out)
            w_out = jnp.where(lane == kk, es[kk] * inv, w_out)
        idx_ref[c * rows:(c + 1) * rows, :] = idx_out.astype(jnp.int32)
        w_ref[c * rows:(c + 1) * rows, :] = w_out
        oh = [jnp.where(lanef == ix, 1.0, 0.0) for ix in idxs]
        onehots.append(oh)
        osums.append((oh[0] + oh[1]) + (oh[2] + oh[3]))

    osum = jnp.concatenate(osums, axis=0)
    ri = lax.broadcasted_iota(jnp.int32, (MERGE_TM, MERGE_TM), 0)
    ci = lax.broadcasted_iota(jnp.int32, (MERGE_TM, MERGE_TM), 1)
    before = jnp.where(ri > ci, 1.0, 0.0).astype(BF16)
    prior = _dot(before, osum.astype(BF16)) + cnt_ref[0:1, :]
    for c in range(ROUTE_CHUNKS):
        pc = prior[c * rows:(c + 1) * rows]
        rank_out = jnp.zeros((rows, 128), F32)
        for kk in range(TOP_K):
            rank_out = jnp.where(lane == kk, jnp.sum(onehots[c][kk] * pc, axis=-1, keepdims=True), rank_out)
        rank_ref[c * rows:(c + 1) * rows, :] = rank_out.astype(jnp.int32)
    cnt_ref[...] = cnt_ref[...] + jnp.sum(osum, axis=0, keepdims=True)


def _merge(branches, z, xc, xl, modr, w, layer):
    mrow = _mod_row(MERGE_TM)
    base = layer * N_MOD

    def mod_spec(part):
        return pl.BlockSpec((1, 1, D_MODEL), lambda i: ((base + mrow(i)) * 6 + part, 0, 0))

    const = lambda shape: pl.BlockSpec(shape, lambda i: (0,) * len(shape))
    row = lambda width: pl.BlockSpec((MERGE_TM, width), lambda i: (i, 0))
    gate = lambda col: pl.BlockSpec((MERGE_TM, 1024), lambda i: (i, col))
    return pl.pallas_call(
        _merge_kernel,
        out_shape=[jax.ShapeDtypeStruct((T_ALL, D_MODEL), F32), jax.ShapeDtypeStruct((T_ALL, D_MODEL), F32),
                   jax.ShapeDtypeStruct((T_ALL, 128), jnp.int32), jax.ShapeDtypeStruct((T_ALL, 128), F32),
                   jax.ShapeDtypeStruct((T_ALL, 128), jnp.int32), jax.ShapeDtypeStruct((8, 128), F32)],
        grid=(T_ALL // MERGE_TM,),
        in_specs=[_ctx_spec(MERGE_TM, 512), _lat_spec(MERGE_TM, 512)] * 3 + [gate(3), gate(4), gate(5),
                  _ctx_spec(MERGE_TM, D_MODEL), _lat_spec(MERGE_TM, D_MODEL),
                  const((3, 512, 1024)), const((1024, 1024)),
                  mod_spec(2), const((1, 1024)), mod_spec(3), mod_spec(4),
                  const((1024, 128)), const((1024, 128)), const((1, 128))],
        out_specs=[row(1024), row(1024), row(128), row(128), row(128), const((8, 128))],
        compiler_params=_params(1),
        name="merge",
    )(*branches, z, z, z, xc, xl, w["wb"], w["wo"], modr, w["n2_g"], modr, modr,
      w["rw_hi"], w["rw_lo"], w["rb"])


def _route(top_idx, rank, counts):
    flat_e = top_idx.reshape(N_SLOTS)
    onehot = (flat_e[:, None] == jnp.arange(N_EXPERTS, dtype=jnp.int32)[None, :]).astype(jnp.int32)
    padded = (counts + MOE_ROWS - 1) // MOE_ROWS * MOE_ROWS
    pad_end = jnp.cumsum(padded)
    pad_start = pad_end - padded
    dest = (rank.reshape(N_SLOTS) + jnp.sum(onehot * pad_start[None, :], axis=1)).astype(jnp.int32)
    blk_start = jnp.arange(N_MOE_BLOCKS, dtype=jnp.int32) * MOE_ROWS
    block_e = jnp.minimum(jnp.sum((pad_end[None, :] <= blk_start[:, None]).astype(jnp.int32), axis=1),
                          N_EXPERTS - 1).astype(jnp.int32)
    n_used = (pad_end[-1] // MOE_ROWS).astype(jnp.int32)
    e_hot = (block_e[:, None] == jnp.arange(N_EXPERTS, dtype=jnp.int32)[None, :]).astype(jnp.int32)
    nxt_blk = jnp.sum(e_hot * pad_end[None, :], axis=1) // MOE_ROWS
    b_hot = (nxt_blk[:, None] == jnp.arange(N_MOE_BLOCKS, dtype=jnp.int32)[None, :]).astype(jnp.int32)
    next_e = jnp.where(nxt_blk < n_used, jnp.sum(b_hot * block_e[None, :], axis=1), block_e).astype(jnp.int32)
    tail_start = (pad_start + counts).astype(jnp.int32)
    return dest, tail_start, block_e, n_used.reshape(1), next_e


def _rows_to_tiles(x):
    r = x.shape[0]
    blocks = jnp.stack([x[:, 128 * g:128 * (g + 1)].reshape(r // 8, 8, 128) for g in range(D_MODEL // 128)], axis=1)
    return jnp.swapaxes(blocks, 1, 2).reshape(r, D_MODEL // 128, 128)


def _tiles_to_rows(v):
    r = v.shape[0]
    blocks = jnp.swapaxes(v.reshape(r // 8, 8, D_MODEL // 128, 128), 1, 2)
    return jnp.concatenate([blocks[:, g].reshape(r, 128) for g in range(D_MODEL // 128)], axis=1)


def _dispatch_kernel(dest_ref, tail_ref, nb_ref, h_ref, xs_ref, zero_buf, stage, sem, ssem):
    i = pl.program_id(0)

    @pl.when(i == 0)
    def _():
        zero_buf[...] = jnp.zeros_like(zero_buf)
        fills = [pltpu.make_async_copy(zero_buf, xs_ref.at[pl.ds(tail_ref[e], MOE_ROWS)], sem)
                 for e in range(N_EXPERTS)]
        for f in fills:
            f.start()
        for f in fills:
            f.wait()

        def fill_block(b, c):
            f = pltpu.make_async_copy(zero_buf, xs_ref.at[pl.ds(b * MOE_ROWS, MOE_ROWS)], sem)
            f.start()
            f.wait()
            return c
        lax.fori_loop(nb_ref[0], (N_MOE_ROWS + XS_EXTRA) // MOE_ROWS, fill_block, 0)

    base = i * TM * TOP_K
    cur = i % 2
    row0 = pl.multiple_of(cur * TM, TM)

    def wait_tile(slot):
        for kk in range(TOP_K):
            pltpu.make_async_copy(stage.at[pl.ds(0, TM)], xs_ref.at[pl.ds(0, TM)], ssem.at[slot]).wait()

    @pl.when(i >= 2)
    def _():
        wait_tile(cur)
    stage[pl.ds(row0, TM)] = _rows_to_tiles(h_ref[...])

    def issue(t, c):
        for kk in range(TOP_K):
            pltpu.make_async_copy(stage.at[row0 + t], xs_ref.at[dest_ref[base + t * TOP_K + kk]], ssem.at[cur]
                                  ).start(priority=kk % 2)
        return c
    lax.fori_loop(0, TM, issue, 0, unroll=8)

    @pl.when(i == N_TILES - 1)
    def _():
        wait_tile(1 - cur)
        wait_tile(cur)


def _moe_dispatch(h, dest, tail_start, n_used):
    return pl.pallas_call(
        _dispatch_kernel,
        out_shape=jax.ShapeDtypeStruct((N_MOE_ROWS + XS_EXTRA,) + ROW_TILE, F32),
        grid_spec=pltpu.PrefetchScalarGridSpec(
            num_scalar_prefetch=3, grid=(N_TILES,),
            in_specs=[pl.BlockSpec((TM, D_MODEL), lambda i, d, t, nb: (i, 0))],
            out_specs=pl.BlockSpec(memory_space=pl.ANY),
            scratch_shapes=[pltpu.VMEM((MOE_ROWS,) + ROW_TILE, F32), pltpu.VMEM((2 * TM,) + ROW_TILE, F32),
                            pltpu.SemaphoreType.DMA, pltpu.SemaphoreType.DMA((2,))]),
        compiler_params=_params(1),
        name="moe_dispatch",
    )(dest, tail_start, n_used, h)


def _expert_kernel(be_ref, nb_ref, nxt_ref, x_ref, wgu_hbm, bgu_ref, wd_hbm, bd_ref, o_ref,
                   wgu_st, wd_st, wgu_bf, wd_bf, sem, *, layer):
    i = pl.program_id(0)
    e = be_ref[i]
    prev = be_ref[jnp.maximum(i - 1, 0)]

    def weight_copies(expert):
        idx = layer * N_EXPERTS + expert
        return (pltpu.make_async_copy(wgu_hbm.at[idx], wgu_st, sem.at[0]),
                pltpu.make_async_copy(wd_hbm.at[idx], wd_st, sem.at[1]))

    @pl.when(i == 0)
    def _():
        for cp in weight_copies(e):
            cp.start()

    @pl.when(((i == 0) | (e != prev)) & (i < nb_ref[0]))
    def _():
        for cp in weight_copies(e):
            cp.wait()
        wgu_bf[...] = wgu_st[...].astype(BF16)
        wd_bf[...] = wd_st[...].astype(BF16)
        nxt = nxt_ref[i]

        @pl.when(nxt != e)
        def _():
            for cp in weight_copies(nxt):
                cp.start()

    @pl.when(i < nb_ref[0])
    def _():
        gu = _dot(_tiles_to_rows(x_ref[...]).astype(BF16), wgu_bf[...]) + bgu_ref[0]
        gate = jnp.minimum(gu[:, :D_EXPERT], SWIGLU_LIMIT)
        up = jnp.clip(gu[:, D_EXPERT:], -SWIGLU_LIMIT, SWIGLU_LIMIT)
        act = (up + 1.0) * gate * _sigmoid(SWIGLU_ALPHA * gate)
        o_ref[...] = _rows_to_tiles(_dot(act.astype(BF16), wd_bf[...]) + bd_ref[0])

    @pl.when(i >= nb_ref[0])
    def _():
        o_ref[...] = jnp.zeros_like(o_ref)


def _moe_experts(xs, block_e, n_used, next_e, w_gu, b_gu, w_down, b_down, layer):
    w_idx = lambda i, be, nb, nx: (layer * N_EXPERTS + be[i], 0, 0)
    return pl.pallas_call(
        functools.partial(_expert_kernel, layer=layer),
        out_shape=jax.ShapeDtypeStruct((N_MOE_ROWS,) + ROW_TILE, F32),
        grid_spec=pltpu.PrefetchScalarGridSpec(
            num_scalar_prefetch=3, grid=(N_MOE_BLOCKS,),
            in_specs=[pl.BlockSpec((MOE_ROWS,) + ROW_TILE, lambda i, be, nb, nx: (jnp.minimum(i, nb[0] - 1), 0, 0)),
                      pl.BlockSpec(memory_space=pl.ANY),
                      pl.BlockSpec((1, 1, 2 * D_EXPERT), w_idx),
                      pl.BlockSpec(memory_space=pl.ANY),
                      pl.BlockSpec((1, 1, D_MODEL), w_idx)],
            out_specs=pl.BlockSpec((MOE_ROWS,) + ROW_TILE, lambda i, be, nb, nx: (i, 0, 0)),
            scratch_shapes=[pltpu.VMEM((D_MODEL, 2 * D_EXPERT), F32), pltpu.VMEM((D_EXPERT, D_MODEL), F32),
                            pltpu.VMEM((D_MODEL, 2 * D_EXPERT), BF16), pltpu.VMEM((D_EXPERT, D_MODEL), BF16),
                            pltpu.SemaphoreType.DMA((2,))]),
        compiler_params=_params(1),
        name="moe_experts",
    )(block_e, n_used, next_e, xs, w_gu, b_gu, w_down, b_down)


def _combine_kernel(dest_ref, x_ref, g2_ref, w_ref, eo_ref, yc_ref, yl_ref, buf, sem):
    i = pl.program_id(0)
    cur = i % 2

    def issue_tile(tile, slot):
        base = tile * TM * TOP_K
        row0 = slot * (TOP_K * TM)

        def issue(t, c):
            for kk in range(TOP_K):
                pltpu.make_async_copy(eo_ref.at[dest_ref[base + t * TOP_K + kk]], buf.at[row0 + kk * TM + t],
                                      sem.at[slot]).start(priority=kk % 2)
            return c
        lax.fori_loop(0, TM, issue, 0, unroll=8)

    @pl.when(i == 0)
    def _():
        issue_tile(0, 0)

    @pl.when(i + 1 < N_TILES)
    def _():
        issue_tile(i + 1, 1 - cur)

    for kk in range(TOP_K):
        pltpu.make_async_copy(eo_ref.at[pl.ds(0, TM)], buf.at[pl.ds(0, TM)], sem.at[cur]).wait()
    w = w_ref[...]
    ff = None
    for kk in range(TOP_K):
        rows = _tiles_to_rows(buf[pl.ds(pl.multiple_of(cur * (TOP_K * TM) + kk * TM, TM), TM)])
        term = rows * w[:, kk:kk + 1]
        ff = term if ff is None else ff + term
    y = x_ref[...] + g2_ref[0] * ff
    is_ctx = i < N_CTX_TILES

    @pl.when(is_ctx)
    def _():
        yc_ref[...] = y

    @pl.when(jnp.logical_not(is_ctx))
    def _():
        yl_ref[...] = y


def _moe_combine(dest, x1, modr, top_w, eo, layer):
    mrow = _mod_row(TM)
    base = layer * N_MOD
    return pl.pallas_call(
        _combine_kernel,
        out_shape=[jax.ShapeDtypeStruct((T_CTX, D_MODEL), F32), jax.ShapeDtypeStruct((T_LAT, D_MODEL), F32)],
        grid_spec=pltpu.PrefetchScalarGridSpec(
            num_scalar_prefetch=1, grid=(N_TILES,),
            in_specs=[pl.BlockSpec((TM, D_MODEL), lambda i, d: (i, 0)),
                      pl.BlockSpec((1, 1, D_MODEL), lambda i, d: ((base + mrow(i)) * 6 + 5, 0, 0)),
                      pl.BlockSpec((TM, 128), lambda i, d: (i, 0)),
                      pl.BlockSpec(memory_space=pl.ANY)],
            out_specs=[pl.BlockSpec((TM, D_MODEL), lambda i, d: (jnp.minimum(i, N_CTX_TILES - 1), 0)),
                       pl.BlockSpec((TM, D_MODEL), lambda i, d: (jnp.maximum(i - N_CTX_TILES, 0), 0))],
            scratch_shapes=[pltpu.VMEM((2 * TOP_K * TM,) + ROW_TILE, F32), pltpu.SemaphoreType.DMA((2,))]),
        compiler_params=_params(1),
        name="moe_combine",
    )(dest, x1, modr, top_w, eo)


def _pad_heads(w, n_heads, width):
    lead = w.shape[:-1]
    w = w.reshape(lead + (n_heads, width))
    w = jnp.pad(w, [(0, 0)] * len(lead) + [(0, 0), (0, HEAD_PAD - width)])
    return w.reshape(lead + (n_heads * HEAD_PAD,))


def _layer_weights(l, w_in, gla_wa2, gla_ba, mla_qa_g, mla_wuq, mla_kva_g, mla_wukv, mla_qn_g, mla_kn_g,
                   w_branch, w_out, router_w, router_b, norm2_g):
    wi = w_in[l]
    w_in_p = jnp.concatenate([wi[:, :3072], wi[:, 3520:], wi[:, 3104:3520], wi[:, 3072:3104],
                              jnp.zeros((D_MODEL, 64), F32)], axis=1).astype(BF16)
    wa_p = jnp.zeros((512, 512), F32)
    wa_p = wa_p.at[416:432, 0:256].set(gla_wa2[l, 0]).at[432:448, 256:512].set(gla_wa2[l, 1]).astype(BF16)
    ba_p = gla_ba[l].reshape(1, 512)
    wukv = mla_wukv[l].reshape(128, MLA_HEADS, MLA_NOPE + MLA_V)
    rw = jnp.pad(router_w[l], ((0, 0), (0, 128 - N_EXPERTS)))
    rw_hi = rw.astype(BF16)
    return {
        "w_in": w_in_p, "wa": wa_p, "ba": ba_p,
        "qa_g": mla_qa_g[l].reshape(1, 256),
        "wuq": _pad_heads(mla_wuq[l], MLA_HEADS, MLA_QK).astype(BF16),
        "qn_g": jnp.tile(jnp.pad(mla_qn_g[l], (0, HEAD_PAD - MLA_QK)), MLA_HEADS).reshape(1, 1024) * MLA_QK ** -0.5,
        "kn_g": jnp.tile(jnp.pad(mla_kn_g[l], (0, HEAD_PAD - MLA_QK)), MLA_HEADS).reshape(1, 1024),
        "kva_g": mla_kva_g[l].reshape(1, 128),
        "wk": _pad_heads(wukv[:, :, :MLA_NOPE].reshape(128, MLA_HEADS * MLA_NOPE), MLA_HEADS, MLA_NOPE).astype(BF16),
        "wv": wukv[:, :, MLA_NOPE:].reshape(128, MLA_HEADS * MLA_V).astype(BF16),
        "wb": w_branch[l].astype(BF16), "wo": w_out[l].astype(BF16),
        "rw_hi": rw_hi, "rw_lo": (rw - rw_hi.astype(F32)).astype(BF16),
        "rb": jnp.pad(router_b[l], (0, 128 - N_EXPERTS)).reshape(1, 128),
        "n2_g": norm2_g[l].reshape(1, D_MODEL),
    }


def kernel(x_prompt, x_sample, cache_mla_ckv, cache_mla_krope, state_ret, state_gla, c, c_ctx, w_mod, b_mod, norm1_g, norm2_g, w_in, ret_gn_g, gla_wa2, gla_ba, gla_norm_g, mla_qa_g, mla_wuq, mla_kva_g, mla_wukv, mla_qn_g, mla_kn_g, w_branch, w_out, router_w, router_b, moe_w_gu, moe_b_gu, moe_w_down, moe_b_down):
    xc, xl = x_prompt.reshape(T_CTX, D_MODEL), x_sample.reshape(T_LAT, D_MODEL)
    cc = jnp.concatenate([c_ctx[None, :], c, jnp.zeros((N_MOD - 1 - N_LAT_SEQ, D_MODEL), F32)], axis=0)
    modr = _modulation(cc, w_mod, b_mod).reshape(DEPTH * N_MOD * 6, 1, D_MODEL)
    rope_tabs = _rope_tables()
    w_gu = moe_w_gu.reshape(DEPTH * N_EXPERTS, D_MODEL, 2 * D_EXPERT)
    b_gu = moe_b_gu.reshape(DEPTH * N_EXPERTS, 1, 2 * D_EXPERT)
    w_dn = moe_w_down.reshape(DEPTH * N_EXPERTS, D_EXPERT, D_MODEL)
    b_dn = moe_b_down.reshape(DEPTH * N_EXPERTS, 1, D_MODEL)

    ckv_l, krope_l, ret_l, gla_l = [], [], [], []
    for l in range(DEPTH):
        w = _layer_weights(l, w_in, gla_wa2, gla_ba, mla_qa_g, mla_wuq, mla_kva_g, mla_wukv, mla_qn_g, mla_kn_g,
                           w_branch, w_out, router_w, router_b, norm2_g)
        z = _in_proj(xc, xl, norm1_g[l].reshape(1, D_MODEL), modr, w["w_in"], l)

        gn = ret_gn_g[l].reshape(1, 512)
        ret_c, ret_state = _retention(z, gn, None, ctx=True)
        (ret_s,) = _retention(z, gn, state_ret[:, l], ctx=False)
        gng = gla_norm_g[l].reshape(1, 512)
        gla_c, gla_state = _gla(z, w["wa"], w["ba"], gng, None, ctx=True)
        gla_s, _ = _gla(z, w["wa"], w["ba"], gng, state_gla[:, l], ctx=False)

        q, k, v, ckv, kr = _mla_prep(z, w, rope_tabs)
        kc, vc = _mla_cache(cache_mla_ckv[:, l].reshape(N_LAT_SEQ * CTX_LEN, 128),
                            jnp.pad(cache_mla_krope[:, l].reshape(N_LAT_SEQ * CTX_LEN, MLA_ROPE),
                                    ((0, 0), (0, 128 - MLA_ROPE))), w)
        mla_c = _mla_attn(q, k, v, None, None, ctx=True)
        mla_s = _mla_attn(q, k, v, kc, vc, ctx=False)

        x1, h2, top_idx, top_w, rank, cnt = _merge((ret_c, ret_s, gla_c, gla_s, mla_c, mla_s), z, xc, xl, modr, w, l)
        dest, tail_start, block_e, n_used, next_e = _route(top_idx[:, :TOP_K], rank[:, :TOP_K],
                                                           cnt[0, :N_EXPERTS].astype(jnp.int32))
        xs = _moe_dispatch(h2, dest, tail_start, n_used)
        eo = _moe_experts(xs, block_e, n_used, next_e, w_gu, b_gu, w_dn, b_dn, l)
        xc, xl = _moe_combine(dest, x1, modr, top_w, eo, l)

        ckv_l.append(ckv[:T_CTX].reshape(N_CTX_SEQ, CTX_LEN, 128))
        krope_l.append(kr[:T_CTX, :MLA_ROPE].reshape(N_CTX_SEQ, CTX_LEN, MLA_ROPE))
        ret_l.append(ret_state)
        gla_l.append(gla_state.reshape(N_CTX_SEQ, 2, N_HEADS, DK, DV))

    y_p = xc.reshape(N_CTX_SEQ, CTX_LEN, D_MODEL)
    y_s = xl.reshape(N_LAT_SEQ, LAT_LEN, D_MODEL)
    return (y_p, y_s, jnp.stack(ckv_l, axis=1), jnp.stack(krope_l, axis=1),
            jnp.stack(ret_l, axis=1), jnp.stack(gla_l, axis=1))
```

```python
import functools

import jax
import jax.numpy as jnp
import numpy as np
from jax import lax
from jax.experimental import pallas as pl
from jax.experimental.pallas import tpu as pltpu

F32 = jnp.float32
BF16 = jnp.bfloat16

D_MODEL = 1024
DEPTH = 2
N_CTX_SEQ, CTX_LEN = 32, 256
N_LAT_SEQ, LAT_LEN = 4, 1024
T_CTX = N_CTX_SEQ * CTX_LEN
T_LAT = N_LAT_SEQ * LAT_LEN
T_ALL = T_CTX + T_LAT
TM = 256
N_TILES = T_ALL // TM
N_CTX_TILES = T_CTX // TM
LAT_TILES = LAT_LEN // TM
N_MOD = 8
EPS = 1e-6

N_HEADS = 4
DK, DV = 64, 128
GRID_W = 64
MLA_HEADS, MLA_NOPE, MLA_ROPE, MLA_V = 8, 64, 32, 64
MLA_QK = MLA_NOPE + MLA_ROPE
HEAD_PAD = 128
GLA_TAU = 16.0
N_EXPERTS, TOP_K, D_EXPERT = 32, 4, 1024
SWIGLU_LIMIT, SWIGLU_ALPHA = 7.0, 1.702
MOE_ROWS = 512
N_SLOTS = T_ALL * TOP_K
N_MOE_BLOCKS = N_SLOTS // MOE_ROWS + N_EXPERTS
N_MOE_ROWS = N_MOE_BLOCKS * MOE_ROWS
MERGE_TM = 512
ROUTE_CHUNKS = 8
XS_EXTRA = MOE_ROWS
ROW_TILE = (D_MODEL // 128, 128)

DZ = 6656
IN_TILE = 512
VMEM_LIMIT = 56 * 1024 * 1024

RET_LOG_F = [float(np.log1p(-np.exp2(-(5.0 + h)))) for h in range(N_HEADS)]
RET_LOG_B = [float(np.log1p(-np.exp2(-(5.5 + h)))) for h in range(N_HEADS)]


def _params(n_axes, vmem=VMEM_LIMIT):
    return pltpu.CompilerParams(dimension_semantics=("arbitrary",) * n_axes, vmem_limit_bytes=vmem)


def _sigmoid(x):
    return 1.0 / (1.0 + jnp.exp(-x))


def _dot(a, b):
    return jnp.dot(a, b, preferred_element_type=F32)


def _dot_t(a, b):
    return lax.dot_general(a, b, (((1,), (1,)), ((), ())), preferred_element_type=F32)


def _mod_row(tile_rows):
    def f(i):
        r0 = i * tile_rows
        return jnp.where(r0 < T_CTX, 0, 1 + (r0 - T_CTX) // LAT_LEN)
    return f


def _ctx_spec(rows, width):
    n_ctx = T_CTX // rows
    return pl.BlockSpec((rows, width), lambda i: (jnp.minimum(i, n_ctx - 1), 0))


def _lat_spec(rows, width):
    n_ctx = T_CTX // rows
    return pl.BlockSpec((rows, width), lambda i: (jnp.maximum(i - n_ctx, 0), 0))


def _mod_kernel(c_ref, w_ref, b_ref, o_ref):
    c = c_ref[...]
    s = c * _sigmoid(c)
    sh = s.astype(BF16)
    sl = (s - sh.astype(F32)).astype(BF16)
    w = w_ref[0]
    wh = w.astype(BF16)
    wl = (w - wh.astype(F32)).astype(BF16)
    o_ref[0] = _dot(sh, wh) + _dot(sh, wl) + _dot(sl, wh) + b_ref[0]


def _modulation(cc, w_mod, b_mod):
    n = 6 * D_MODEL
    blk = 2048
    return pl.pallas_call(
        _mod_kernel,
        out_shape=jax.ShapeDtypeStruct((DEPTH, N_MOD, n), F32),
        grid=(DEPTH, n // blk),
        in_specs=[pl.BlockSpec((N_MOD, D_MODEL), lambda l, j: (0, 0)),
                  pl.BlockSpec((1, D_MODEL, blk), lambda l, j: (l, 0, j)),
                  pl.BlockSpec((1, 1, blk), lambda l, j: (l, 0, j))],
        out_specs=pl.BlockSpec((1, N_MOD, blk), lambda l, j: (l, 0, j)),
        compiler_params=_params(2),
        name="modulation",
    )(cc, w_mod, b_mod.reshape(DEPTH, 1, n))


def _in_kernel(xc_ref, xl_ref, g_ref, sh_ref, sc_ref, w_ref, o_ref):
    x = jnp.where(pl.program_id(0) < T_CTX // IN_TILE, xc_ref[...], xl_ref[...])
    h = x * lax.rsqrt(jnp.mean(x * x, axis=-1, keepdims=True) + EPS) * g_ref[...]
    h = h * (1.0 + sc_ref[0]) + sh_ref[0]
    hb = h.astype(BF16)
    for n0 in range(0, DZ, 512):
        o_ref[:, n0:n0 + 512] = _dot(hb, w_ref[0, :, n0:n0 + 512]).astype(BF16)


def _in_proj(xc, xl, g, modr, w_in_p, layer):
    mrow = _mod_row(IN_TILE)
    base = layer * N_MOD

    def mod_spec(part):
        return pl.BlockSpec((1, 1, D_MODEL), lambda i: ((base + mrow(i)) * 6 + part, 0, 0))

    return pl.pallas_call(
        _in_kernel,
        out_shape=jax.ShapeDtypeStruct((T_ALL, DZ), BF16),
        grid=(T_ALL // IN_TILE,),
        in_specs=[_ctx_spec(IN_TILE, D_MODEL), _lat_spec(IN_TILE, D_MODEL),
                  pl.BlockSpec((1, D_MODEL), lambda i: (0, 0)),
                  mod_spec(0), mod_spec(1),
                  pl.BlockSpec((1, D_MODEL, DZ), lambda i: (layer, 0, 0))],
        out_specs=pl.BlockSpec((IN_TILE, DZ), lambda i: (i, 0)),
        compiler_params=_params(1),
        name="in_proj",
    )(xc, xl, g, modr, modr, w_in_p)


def _lane_half_mask(hh):
    lane = lax.broadcasted_iota(jnp.int32, (1, 128), 1)
    return (lane < 64) if hh == 0 else (lane >= 64)


@functools.lru_cache(maxsize=None)
def _ret_decay_table(seq):
    d = np.arange(seq)[:, None] - np.arange(seq)[None, :]
    tab = np.stack([np.exp(np.where(d > 0, RET_LOG_F[h] * d, -RET_LOG_B[h] * d)) for h in range(N_HEADS)])
    return (tab * np.where(d == 0, 2.0, 1.0) * DK ** -0.5).astype(np.float32)


def _ret_kernel(*refs, seq, has_state, emit_state):
    q_ref, k_ref, v_ref, g_ref, gn_ref, dec_ref = refs[:6]
    pos = 6
    if has_state:
        s0_ref = refs[pos]
        pos += 1
    o_ref = refs[pos]
    pos += 1
    if emit_state:
        st_ref = refs[pos]

    r0 = pl.multiple_of(pl.program_id(1) * TM, TM)
    qb = q_ref[pl.ds(r0, TM), :]
    ri = (lax.broadcasted_iota(jnp.int32, (TM, 1), 0) + r0).astype(F32)

    for h in range(N_HEADS):
        p, hh = h // 2, h % 2
        lanes = slice(128 * p, 128 * p + 128)
        qp = qb[:, lanes]
        qh = jnp.where(_lane_half_mask(hh), qp, jnp.zeros_like(qp))
        sc = _dot_t(qh, k_ref[:, lanes])
        o = _dot((sc * dec_ref[h]).astype(BF16), v_ref[:, 128 * h:128 * h + 128])
        if has_state:
            qf = qh.astype(F32)
            o += _dot((qf * jnp.exp(RET_LOG_F[h] * (ri + 1.0))).astype(BF16), s0_ref[0, 0, p].astype(BF16))
            o += _dot((qf * jnp.exp(RET_LOG_B[h] * (seq - ri))).astype(BF16), s0_ref[0, 1, p].astype(BF16))
        mu = jnp.mean(o, axis=-1, keepdims=True)
        d = o - mu
        var = jnp.mean(d * d, axis=-1, keepdims=True)
        on = d * lax.rsqrt(var + EPS)
        g = g_ref[:, 128 * h:128 * h + 128].astype(F32)
        out = on * gn_ref[:, 128 * h:128 * h + 128] * (g * _sigmoid(g))
        o_ref[:, 128 * h:128 * h + 128] = out.astype(BF16)

    if emit_state:
        jc = lax.broadcasted_iota(jnp.int32, (seq, 1), 0).astype(F32)
        lane = lax.broadcasted_iota(jnp.int32, (1, 128), 1)
        for p in range(2):
            kp = k_ref[:, 128 * p:128 * p + 128].astype(F32) * DK ** -0.5
            lgf = jnp.where(lane < 64, RET_LOG_F[2 * p], RET_LOG_F[2 * p + 1])
            lgb = jnp.where(lane < 64, RET_LOG_B[2 * p], RET_LOG_B[2 * p + 1])
            kdf = (kp * jnp.exp(lgf * (seq - 1.0 - jc))).T.astype(BF16)
            kdb = (kp * jnp.exp(lgb * jc)).T.astype(BF16)
            for hh in range(2):
                h = 2 * p + hh
                vh = v_ref[:, 128 * h:128 * h + 128]
                st_ref[0, 0, h] = _dot(kdf, vh)[64 * hh:64 * hh + 64, :]
                st_ref[0, 1, h] = _dot(kdb, vh)[64 * hh:64 * hh + 64, :]


def _retention(z, gn_g, s0, *, ctx):
    if ctx:
        nb, seq, row_blk, tile0 = N_CTX_SEQ, CTX_LEN, 0, 0
    else:
        nb, seq, row_blk, tile0 = N_LAT_SEQ, LAT_LEN, T_CTX // LAT_LEN, N_CTX_TILES
    nq = seq // TM
    in_specs = [pl.BlockSpec((seq, 256), lambda b, i: (row_blk + b, 0)),
                pl.BlockSpec((seq, 256), lambda b, i: (row_blk + b, 1)),
                pl.BlockSpec((seq, 512), lambda b, i: (row_blk + b, 1)),
                pl.BlockSpec((TM, 512), lambda b, i: (tile0 + b * nq + i, 2)),
                pl.BlockSpec((1, 512), lambda b, i: (0, 0)),
                pl.BlockSpec((N_HEADS, TM, seq), lambda b, i: (0, i, 0))]
    args = [z, z, z, z, gn_g, _ret_decay_table(seq)]
    out_shape = [jax.ShapeDtypeStruct((nb * seq, 512), BF16)]
    out_specs = [pl.BlockSpec((TM, 512), lambda b, i: (b * nq + i, 0))]
    if not ctx:
        in_specs.append(pl.BlockSpec((1, 2, 2, 128, 128), lambda b, i: (b, 0, 0, 0, 0)))
        args.append(s0.reshape(N_LAT_SEQ, 2, 2, 128, 128))
    else:
        out_shape.append(jax.ShapeDtypeStruct((nb, 2, N_HEADS, DK, DV), F32))
        out_specs.append(pl.BlockSpec((1, 2, N_HEADS, DK, DV), lambda b, i: (b, 0, 0, 0, 0)))
    return pl.pallas_call(
        functools.partial(_ret_kernel, seq=seq, has_state=not ctx, emit_state=ctx),
        out_shape=out_shape, grid=(nb, nq), in_specs=in_specs, out_specs=out_specs,
        compiler_params=_params(2),
        name="retention_ctx" if ctx else "retention_lat",
    )(*args)


def _gla_decay(small_ref, wa_ref, ba_ref):
    x = _dot(small_ref[...], wa_ref[...]) + ba_ref[...]
    la = -(jnp.maximum(-x, 0.0) + jnp.log(1.0 + jnp.exp(-jnp.abs(x)))) * (1.0 / GLA_TAU)
    ri = lax.broadcasted_iota(jnp.int32, (TM, TM), 0)
    ci = lax.broadcasted_iota(jnp.int32, (TM, TM), 1)
    ltri = jnp.where(ri >= ci, 1.0, 0.0).astype(BF16)
    hi = la.astype(BF16)
    r1 = la - hi.astype(F32)
    mid = r1.astype(BF16)
    lo = (r1 - mid.astype(F32)).astype(BF16)
    cum = _dot(ltri, hi) + _dot(ltri, mid) + _dot(ltri, lo)
    return la, cum


def _gla_state_kernel(k_ref, v_ref, small_ref, wa_ref, ba_ref, kv_ref, tot_ref):
    la, cum = _gla_decay(small_ref, wa_ref, ba_ref)
    bf, bb = cum[:, :256], cum[:, 256:]
    xb = bb - la[:, 256:]
    k = k_ref[...].astype(F32)
    kdf = k * jnp.exp(bf[TM - 1:TM, :] - bf)
    kdb = k * jnp.exp(xb)
    for p in range(2):
        kf_t = kdf[:, 128 * p:128 * p + 128].T.astype(BF16)
        kb_t = kdb[:, 128 * p:128 * p + 128].T.astype(BF16)
        for hh in range(2):
            h = 2 * p + hh
            vh = v_ref[:, 128 * h:128 * h + 128]
            kv_ref[0, 0, 0, h] = _dot(kf_t, vh)[64 * hh:64 * hh + 64, :]
            kv_ref[0, 0, 1, h] = _dot(kb_t, vh)[64 * hh:64 * hh + 64, :]
    tot_ref[0, 0] = jnp.sum(la.T, axis=-1, keepdims=True)


def _mid_bcast(x, s, r):
    w = 2 * s
    if w >= 8:
        n = TM // w
        x3 = x.reshape(n, w, 256)
        return jnp.broadcast_to(x3[:, r:r + 1, :], (n, w, 256)).reshape(TM, 256)
    x3 = x.reshape(TM // 8, 8, 256)
    sub = lax.broadcasted_iota(jnp.int32, (1, 8, 1), 1)
    out = None
    for blk in range(8 // w):
        rowv = jnp.broadcast_to(x3[:, blk * w + r:blk * w + r + 1, :], (TM // 8, 8, 256))
        out = rowv if out is None else jnp.where(sub >= blk * w, rowv, out)
    return out.reshape(TM, 256)


def _gla_kernel(*refs, n_blk, has_state, emit_state):
    q_ref, k_ref, v_ref, g_ref, small_ref, wa_ref, ba_ref, gn_ref = refs[:8]
    pos = 8
    if has_state:
        kv_ref, tot_ref, s0_ref = refs[pos:pos + 3]
        pos += 3
    o_ref = refs[pos]
    pos += 1
    if emit_state:
        kvo_ref = refs[pos]

    la, cum = _gla_decay(small_ref, wa_ref, ba_ref)
    bf, bb = cum[:, :256], cum[:, 256:]
    xb = bb - la[:, 256:]
    q = q_ref[...].astype(F32) * DK ** -0.5
    k = k_ref[...].astype(F32)
    row = lax.broadcasted_iota(jnp.int32, (TM, 1), 0)
    rowi = lax.broadcasted_iota(jnp.int32, (TM, TM), 0)
    colj = lax.broadcasted_iota(jnp.int32, (TM, TM), 1)
    low_half = _lane_half_mask(0)

    def join(fwd, bwd):
        ops = []
        for p in range(N_HEADS // 2):
            f = fwd[:, 128 * p:128 * p + 128]
            br = pltpu.roll(bwd[:, 128 * p:128 * p + 128], 64, 1)
            ops.append(jnp.where(low_half, f, br).astype(BF16))
            ops.append(jnp.where(low_half, br, f).astype(BF16))
        return ops

    qd, kd = join(q, q), join(k, k)
    acc = [jnp.where(rowi == colj, _dot_t(qo, ko), 0.0) for qo, ko in zip(qd, kd)]

    low_f = jnp.where(low_half, 1.0, 0.0)
    s = 1
    while s < TM:
        up_f = jnp.where(((row // s) % 2) == 1, 1.0, 0.0)
        live_even = jnp.where(up_f == low_f, 1.0, 0.0).astype(BF16)
        live = [live_even, 1.0 - live_even]
        dead = [live[1], live[0]]
        af = -jnp.abs(bf - _mid_bcast(bf, s, s - 1))
        ab = -jnp.abs(xb - _mid_bcast(xb, s, s))
        same = (rowi // (2 * s)) == (colj // (2 * s))
        for p in range(N_HEADS // 2):
            a_f = af[:, 128 * p:128 * p + 128]
            a_b = pltpu.roll(ab[:, 128 * p:128 * p + 128], 64, 1)
            for hh in range(2):
                h = 2 * p + hh
                arg = jnp.where(low_half, a_f, a_b) if hh == 0 else jnp.where(low_half, a_b, a_f)
                e = jnp.exp(arg).astype(BF16)
                sl = _dot_t(e * (qd[h] * live[hh]), e * (kd[h] * dead[hh]))
                acc[h] = acc[h] + (jnp.where(same, sl, 0.0) if 2 * s < TM else sl)
        s *= 2

    if has_state:
        n = pl.program_id(1)
        q_state = join(q * jnp.exp(bf), q * jnp.exp(bb[TM - 1:TM, :] - xb))

    if emit_state:
        kdf = k * jnp.exp(bf[TM - 1:TM, :] - bf)
        kdb = k * jnp.exp(xb)
        for p in range(N_HEADS // 2):
            kf_t = kdf[:, 128 * p:128 * p + 128].T.astype(BF16)
            kb_t = kdb[:, 128 * p:128 * p + 128].T.astype(BF16)
            for hh in range(2):
                h = 2 * p + hh
                vh = v_ref[:, 128 * h:128 * h + 128]
                kvo_ref[0, 0, 0, h] = _dot(kf_t, vh)[64 * hh:64 * hh + 64, :]
                kvo_ref[0, 0, 1, h] = _dot(kb_t, vh)[64 * hh:64 * hh + 64, :]

    for h in range(N_HEADS):
        o = _dot(acc[h].astype(BF16), v_ref[:, 128 * h:128 * h + 128])
        if has_state:
            sf = s0_ref[0, 0, h]
            for m in range(n_blk - 1):
                dec = jnp.exp(tot_ref[0, m, 64 * h:64 * h + 64, :])
                sf = jnp.where(m < n, dec * sf + kv_ref[0, m, 0, h], sf)
            sb = s0_ref[0, 1, h]
            for m in range(n_blk - 1, 0, -1):
                dec = jnp.exp(tot_ref[0, m, 256 + 64 * h:256 + 64 * h + 64, :])
                sb = jnp.where(m > n, dec * sb + kv_ref[0, m, 1, h], sb)
            state = jnp.concatenate([sf, sb] if h % 2 == 0 else [sb, sf], axis=0).astype(BF16)
            o += _dot(q_state[h], state)
        on = o * lax.rsqrt(jnp.mean(o * o, axis=-1, keepdims=True) + EPS)
        g = g_ref[:, 128 * h:128 * h + 128].astype(F32)
        out = on * gn_ref[:, 128 * h:128 * h + 128] * (g * _sigmoid(g))
        o_ref[:, 128 * h:128 * h + 128] = out.astype(BF16)


def _gla(z, wa_p, ba_p, gn_g, s0, *, ctx):
    if ctx:
        nb, n_blk, tile0 = N_CTX_SEQ, 1, 0
    else:
        nb, n_blk, tile0 = N_LAT_SEQ, LAT_TILES, N_CTX_TILES

    def zspec(width, col):
        return pl.BlockSpec((TM, width), lambda b, n: (tile0 + b * n_blk + n, col))

    w_specs = [pl.BlockSpec((512, 512), lambda b, n: (0, 0)), pl.BlockSpec((1, 512), lambda b, n: (0, 0))]
    kv_shape = jax.ShapeDtypeStruct((nb, n_blk, 2, N_HEADS, DK, DV), F32)
    kv_spec = pl.BlockSpec((1, 1, 2, N_HEADS, DK, DV), lambda b, n: (b, n, 0, 0, 0, 0))
    in_specs = [zspec(256, 6), zspec(256, 7), zspec(512, 4), zspec(512, 5), zspec(512, 12)] + w_specs
    in_specs.append(pl.BlockSpec((1, 512), lambda b, n: (0, 0)))
    args = [z, z, z, z, z, wa_p, ba_p, gn_g]
    out_shape = [jax.ShapeDtypeStruct((nb * n_blk * TM, 512), BF16)]
    out_specs = [pl.BlockSpec((TM, 512), lambda b, n: (b * n_blk + n, 0))]
    if ctx:
        out_shape.append(kv_shape)
        out_specs.append(kv_spec)
    else:
        kv, tot = pl.pallas_call(
            _gla_state_kernel,
            out_shape=[kv_shape, jax.ShapeDtypeStruct((nb, n_blk, 512, 1), F32)],
            grid=(nb, n_blk),
            in_specs=[zspec(256, 7), zspec(512, 4), zspec(512, 12)] + w_specs,
            out_specs=[kv_spec, pl.BlockSpec((1, 1, 512, 1), lambda b, n: (b, n, 0, 0))],
            compiler_params=_params(2),
            name="gla_state_lat",
        )(z, z, z, wa_p, ba_p)
        in_specs += [pl.BlockSpec((1, n_blk, 2, N_HEADS, DK, DV), lambda b, n: (b, 0, 0, 0, 0, 0)),
                     pl.BlockSpec((1, n_blk, 512, 1), lambda b, n: (b, 0, 0, 0)),
                     pl.BlockSpec((1, 2, N_HEADS, DK, DV), lambda b, n: (b, 0, 0, 0, 0))]
        args += [kv, tot, s0]
    res = pl.pallas_call(
        functools.partial(_gla_kernel, n_blk=n_blk, has_state=not ctx, emit_state=ctx),
        out_shape=out_shape, grid=(nb, n_blk), in_specs=in_specs, out_specs=out_specs,
        compiler_params=_params(2),
        name="gla_ctx" if ctx else "gla_lat",
    )(*args)
    return (res[0], res[1]) if ctx else (res[0], None)


def _rope_tables():
    nf = MLA_ROPE // 4
    pos = np.arange(LAT_LEN)
    freqs = (10000.0 ** (-np.arange(nf, dtype=np.float32) / nf)).astype(np.float32)
    ang_r = ((pos // GRID_W).astype(np.float32)[:, None] * freqs).astype(np.float32)
    ang_c = ((pos % GRID_W).astype(np.float32)[:, None] * freqs).astype(np.float32)
    cos = np.ones((TM + LAT_LEN, HEAD_PAD), np.float32)
    sa = np.zeros((TM + LAT_LEN, HEAD_PAD), np.float32)
    sb = np.zeros((TM + LAT_LEN, HEAD_PAD), np.float32)
    o = MLA_NOPE
    for base, ang in ((o, ang_r), (o + 2 * nf, ang_c)):
        cos[TM:, base:base + nf] = np.cos(ang)
        cos[TM:, base + nf:base + 2 * nf] = np.cos(ang)
        sa[TM:, base:base + nf] = -np.sin(ang)
        sb[TM:, base + nf:base + 2 * nf] = np.sin(ang)
    return jnp.asarray(cos), jnp.asarray(sa), jnp.asarray(sb)


def _rope(x, cos, sa, sb):
    return x * cos + pltpu.roll(x, 128 - 8, 1) * sa + pltpu.roll(x, 8, 1) * sb


def _head_segments():
    seg = np.zeros((MLA_HEADS * HEAD_PAD, 128), np.float32)
    for h in range(MLA_HEADS):
        seg[h * HEAD_PAD:(h + 1) * HEAD_PAD, h] = 1.0
    return jnp.asarray(seg, BF16), jnp.asarray(seg.T.copy(), BF16)


def _head_norm(x, gain, seg_ref, segt_ref):
    ss = _dot((x * x).astype(BF16), seg_ref[...])
    rs = lax.rsqrt(ss * (1.0 / MLA_QK) + EPS)
    hi = rs.astype(BF16)
    lo = (rs - hi.astype(F32)).astype(BF16)
    return x * (_dot(hi, segt_ref[...]) + _dot(lo, segt_ref[...])) * gain


def _store_heads(x, o_ref, rope):
    if rope is None:
        o_ref[...] = x.astype(BF16)
    else:
        for h in range(MLA_HEADS):
            o_ref[:, 128 * h:128 * h + 128] = _rope(x[:, 128 * h:128 * h + 128], *rope).astype(BF16)


def _mla_keys(ckv, kr_tile, wk_ref, wv_ref, kn_ref, seg_ref, segt_ref, k_ref, v_ref, rope=None):
    cb = ckv.astype(BF16)
    kpre = _dot(cb, wk_ref[...]) + jnp.concatenate([kr_tile] * MLA_HEADS, axis=1)
    v_ref[...] = _dot(cb, wv_ref[...]).astype(BF16)
    _store_heads(_head_norm(kpre, kn_ref[...], seg_ref, segt_ref), k_ref, rope)


def _mla_prep_kernel(small_ref, qa_ref, wuq_ref, qn_ref, kva_ref, wk_ref, wv_ref, kn_ref, seg_ref, segt_ref,
                     cos_ref, sa_ref, sb_ref, q_ref, k_ref, v_ref, ckv_ref, kr_ref):
    def body(rope):
        sm = small_ref[...].astype(F32)
        cq, ckv_raw, g3 = sm[:, 0:256], sm[:, 256:384], sm[:, 384:512]
        cqn = cq * lax.rsqrt(jnp.mean(cq * cq, axis=-1, keepdims=True) + EPS) * qa_ref[...]
        q = _dot(cqn.astype(BF16), wuq_ref[...])
        _store_heads(_head_norm(q, qn_ref[...], seg_ref, segt_ref), q_ref, rope)
        ckv = ckv_raw * lax.rsqrt(jnp.mean(ckv_raw * ckv_raw, axis=-1, keepdims=True) + EPS) * kva_ref[...]
        ckv_ref[...] = ckv
        lane = lax.broadcasted_iota(jnp.int32, (1, 128), 1)
        kr = jnp.where(lane < MLA_ROPE, g3, 0.0)
        kr_ref[...] = kr
        _mla_keys(ckv, pltpu.roll(kr, MLA_NOPE, 1), wk_ref, wv_ref, kn_ref, seg_ref, segt_ref, k_ref, v_ref, rope)

    is_ctx = pl.program_id(0) < N_CTX_TILES

    @pl.when(is_ctx)
    def _():
        body(None)

    @pl.when(jnp.logical_not(is_ctx))
    def _():
        body((cos_ref[...], sa_ref[...], sb_ref[...]))


def _mla_cache_kernel(ckv_ref, kr_ref, wk_ref, wv_ref, kn_ref, seg_ref, segt_ref, k_ref, v_ref):
    _mla_keys(ckv_ref[...], pltpu.roll(kr_ref[...], MLA_NOPE, 1), wk_ref, wv_ref, kn_ref, seg_ref, segt_ref,
              k_ref, v_ref)


def _mla_prep(z, w, rope_tabs):
    def rope_blk(i):
        return jnp.where(i < N_CTX_TILES, 0, 1 + (i - N_CTX_TILES) % LAT_TILES)

    const = lambda shape: pl.BlockSpec(shape, lambda i: (0,) * len(shape))
    rope_spec = pl.BlockSpec((TM, HEAD_PAD), lambda i: (rope_blk(i), 0))
    row = lambda width: pl.BlockSpec((TM, width), lambda i: (i, 0))
    return pl.pallas_call(
        _mla_prep_kernel,
        out_shape=[jax.ShapeDtypeStruct((T_ALL, 1024), BF16), jax.ShapeDtypeStruct((T_ALL, 1024), BF16),
                   jax.ShapeDtypeStruct((T_ALL, 512), BF16), jax.ShapeDtypeStruct((T_ALL, 128), F32),
                   jax.ShapeDtypeStruct((T_ALL, 128), F32)],
        grid=(N_TILES,),
        in_specs=[pl.BlockSpec((TM, 512), lambda i: (i, 12)),
                  const((1, 256)), const((256, 1024)), const((1, 1024)), const((1, 128)),
                  const((128, 1024)), const((128, 512)), const((1, 1024)),
                  const((1024, 128)), const((128, 1024)),
                  rope_spec, rope_spec, rope_spec],
        out_specs=[row(1024), row(1024), row(512), row(128), row(128)],
        compiler_params=_params(1),
        name="mla_prep",
    )(z, w["qa_g"], w["wuq"], w["qn_g"], w["kva_g"], w["wk"], w["wv"], w["kn_g"], *_head_segments(), *rope_tabs)


def _mla_cache(ckv, kr_pad, w):
    const = lambda shape: pl.BlockSpec(shape, lambda i: (0,) * len(shape))
    row = lambda width: pl.BlockSpec((TM, width), lambda i: (i, 0))
    n = ckv.shape[0]
    return pl.pallas_call(
        _mla_cache_kernel,
        out_shape=[jax.ShapeDtypeStruct((n, 1024), BF16), jax.ShapeDtypeStruct((n, 512), BF16)],
        grid=(n // TM,),
        in_specs=[row(128), row(128), const((128, 1024)), const((128, 512)), const((1, 1024)),
                  const((1024, 128)), const((128, 1024))],
        out_specs=[row(1024), row(512)],
        compiler_params=_params(1),
        name="mla_cache",
    )(ckv, kr_pad, w["wk"], w["wv"], w["kn_g"], *_head_segments())


def _mla_attn_kernel(*refs, has_cache):
    q_ref, k_ref, v_ref = refs[:3]
    pos = 3
    if has_cache:
        kc_ref, vc_ref = refs[3:5]
        pos = 5
    o_ref = refs[pos]
    for p in range(MLA_HEADS // 2):
        acc = jnp.zeros((TM, 128), F32)
        for hh in range(2):
            h = 2 * p + hh
            lanes = slice(128 * h, 128 * h + 128)
            qh = q_ref[:, lanes]
            l1 = _dot_t(qh, k_ref[:, lanes])
            m = jnp.max(l1, axis=-1, keepdims=True)
            if has_cache:
                l0 = _dot_t(qh, kc_ref[:, lanes])
                m = jnp.maximum(m, jnp.max(l0, axis=-1, keepdims=True))
                p0 = jnp.exp(l0 - m)
            p1 = jnp.exp(l1 - m)
            den = jnp.sum(p1, axis=-1, keepdims=True)
            if has_cache:
                den = den + jnp.sum(p0, axis=-1, keepdims=True)
            mask = _lane_half_mask(hh)
            vp = v_ref[:, 128 * p:128 * p + 128]
            o = _dot(p1.astype(BF16), jnp.where(mask, vp, jnp.zeros_like(vp)))
            if has_cache:
                vcp = vc_ref[:, 128 * p:128 * p + 128]
                o += _dot(p0.astype(BF16), jnp.where(mask, vcp, jnp.zeros_like(vcp)))
            acc += o * (1.0 / den)
        o_ref[:, 128 * p:128 * p + 128] = acc.astype(BF16)


def _mla_attn(q, k, v, kc, vc, *, ctx):
    if ctx:
        nb, seq, row_blk, tile0 = N_CTX_SEQ, CTX_LEN, 0, 0
    else:
        nb, seq, row_blk, tile0 = N_LAT_SEQ, LAT_LEN, T_CTX // LAT_LEN, N_CTX_TILES
    nq = seq // TM
    in_specs = [pl.BlockSpec((TM, 1024), lambda b, i: (tile0 + b * nq + i, 0)),
                pl.BlockSpec((seq, 1024), lambda b, i: (row_blk + b, 0)),
                pl.BlockSpec((seq, 512), lambda b, i: (row_blk + b, 0))]
    args = [q, k, v]
    if not ctx:
        in_specs += [pl.BlockSpec((TM, 1024), lambda b, i: (b, 0)), pl.BlockSpec((TM, 512), lambda b, i: (b, 0))]
        args += [kc, vc]
    return pl.pallas_call(
        functools.partial(_mla_attn_kernel, has_cache=not ctx),
        out_shape=jax.ShapeDtypeStruct((nb * seq, 512), BF16),
        grid=(nb, nq), in_specs=in_specs,
        out_specs=pl.BlockSpec((TM, 512), lambda b, i: (b * nq + i, 0)),
        compiler_params=_params(2),
        name="mla_attn_ctx" if ctx else "mla_attn_lat",
    )(*args)


def _merge_kernel(retc_ref, retl_ref, glac_ref, glal_ref, mlac_ref, mlal_ref, m0_ref, m1_ref, m2_ref,
                  xc_ref, xl_ref, wb_ref, wo_ref,
                  g1_ref, n2_ref, sh2_ref, sc2_ref, rwh_ref, rwl_ref, rb_ref,
                  x1_ref, h_ref, idx_ref, w_ref, rank_ref, cnt_ref):
    @pl.when(pl.program_id(0) == 0)
    def _():
        cnt_ref[...] = jnp.zeros_like(cnt_ref)

    is_ctx = pl.program_id(0) < T_CTX // MERGE_TM
    mix = None
    for c_ref, l_ref, m_ref, n in ((retc_ref, retl_ref, m0_ref, 0), (glac_ref, glal_ref, m1_ref, 1),
                                   (mlac_ref, mlal_ref, m2_ref, 2)):
        branch = jnp.where(is_ctx, c_ref[...], l_ref[...])
        term = _sigmoid(m_ref[...]).astype(F32) * _dot(branch, wb_ref[n])
        mix = term if mix is None else mix + term
    out = _dot(mix.astype(BF16), wo_ref[...])
    x1 = jnp.where(is_ctx, xc_ref[...], xl_ref[...]) + g1_ref[0] * out
    x1_ref[...] = x1
    h = x1 * lax.rsqrt(jnp.mean(x1 * x1, axis=-1, keepdims=True) + EPS) * n2_ref[...]
    h = h * (1.0 + sc2_ref[0]) + sh2_ref[0]
    h_ref[...] = h
    hh = h.astype(BF16)
    hl = (h - hh.astype(F32)).astype(BF16)
    logits = _dot(hh, rwh_ref[...]) + _dot(hh, rwl_ref[...]) + _dot(hl, rwh_ref[...]) + rb_ref[...]
    rows = MERGE_TM // ROUTE_CHUNKS
    lane = lax.broadcasted_iota(jnp.int32, (rows, 128), 1)
    lanef = lane.astype(F32)
    onehots, osums = [], []
    for c in range(ROUTE_CHUNKS):
        l = jnp.where(lane < N_EXPERTS, logits[c * rows:(c + 1) * rows], -jnp.inf)
        vals, idxs = [], []
        for _ in range(TOP_K):
            m = jnp.max(l, axis=-1, keepdims=True)
            ix = jnp.min(jnp.where(l == m, lanef, 128.0), axis=-1, keepdims=True)
            vals.append(m)
            idxs.append(ix)
            l = jnp.where(lanef == ix, -jnp.inf, l)
        es = [jnp.exp(v - vals[0]) for v in vals]
        inv = 1.0 / (es[0] + es[1] + es[2] + es[3])
        idx_out = jnp.zeros((rows, 128), F32)
        w_out = jnp.zeros((rows, 128), F32)
        for kk in range(TOP_K):
            idx_out = jnp.where(lane == kk, idxs[kk], idx_out)
            w_out = jnp.where(lane == kk, es[kk] * inv, w_out)
        idx_ref[c * rows:(c + 1) * rows, :] = idx_out.astype(jnp.int32)
        w_ref[c * rows:(c + 1) * rows, :] = w_out
        oh = [jnp.where(lanef == ix, 1.0, 0.0) for ix in idxs]
        onehots.append(oh)
        osums.append((oh[0] + oh[1]) + (oh[2] + oh[3]))

    osum = jnp.concatenate(osums, axis=0)
    ri = lax.broadcasted_iota(jnp.int32, (MERGE_TM, MERGE_TM), 0)
    ci = lax.broadcasted_iota(jnp.int32, (MERGE_TM, MERGE_TM), 1)
    before = jnp.where(ri > ci, 1.0, 0.0).astype(BF16)
    prior = _dot(before, osum.astype(BF16)) + cnt_ref[0:1, :]
    for c in range(ROUTE_CHUNKS):
        pc = prior[c * rows:(c + 1) * rows]
        rank_out = jnp.zeros((rows, 128), F32)
        for kk in range(TOP_K):
            rank_out = jnp.where(lane == kk, jnp.sum(onehots[c][kk] * pc, axis=-1, keepdims=True), rank_out)
        rank_ref[c * rows:(c + 1) * rows, :] = rank_out.astype(jnp.int32)
    cnt_ref[...] = cnt_ref[...] + jnp.sum(osum, axis=0, keepdims=True)


def _merge(branches, z, xc, xl, modr, w, layer):
    mrow = _mod_row(MERGE_TM)
    base = layer * N_MOD

    def mod_spec(part):
        return pl.BlockSpec((1, 1, D_MODEL), lambda i: ((base + mrow(i)) * 6 + part, 0, 0))

    const = lambda shape: pl.BlockSpec(shape, lambda i: (0,) * len(shape))
    row = lambda width: pl.BlockSpec((MERGE_TM, width), lambda i: (i, 0))
    gate = lambda col: pl.BlockSpec((MERGE_TM, 1024), lambda i: (i, col))
    return pl.pallas_call(
        _merge_kernel,
        out_shape=[jax.ShapeDtypeStruct((T_ALL, D_MODEL), F32), jax.ShapeDtypeStruct((T_ALL, D_MODEL), F32),
                   jax.ShapeDtypeStruct((T_ALL, 128), jnp.int32), jax.ShapeDtypeStruct((T_ALL, 128), F32),
                   jax.ShapeDtypeStruct((T_ALL, 128), jnp.int32), jax.ShapeDtypeStruct((8, 128), F32)],
        grid=(T_ALL // MERGE_TM,),
        in_specs=[_ctx_spec(MERGE_TM, 512), _lat_spec(MERGE_TM, 512)] * 3 + [gate(3), gate(4), gate(5),
                  _ctx_spec(MERGE_TM, D_MODEL), _lat_spec(MERGE_TM, D_MODEL),
                  const((3, 512, 1024)), const((1024, 1024)),
                  mod_spec(2), const((1, 1024)), mod_spec(3), mod_spec(4),
                  const((1024, 128)), const((1024, 128)), const((1, 128))],
        out_specs=[row(1024), row(1024), row(128), row(128), row(128), const((8, 128))],
        compiler_params=_params(1),
        name="merge",
    )(*branches, z, z, z, xc, xl, w["wb"], w["wo"], modr, w["n2_g"], modr, modr,
      w["rw_hi"], w["rw_lo"], w["rb"])


def _route(top_idx, rank, counts):
    flat_e = top_idx.reshape(N_SLOTS)
    onehot = (flat_e[:, None] == jnp.arange(N_EXPERTS, dtype=jnp.int32)[None, :]).astype(jnp.int32)
    padded = (counts + MOE_ROWS - 1) // MOE_ROWS * MOE_ROWS
    pad_end = jnp.cumsum(padded)
    pad_start = pad_end - padded
    dest = (rank.reshape(N_SLOTS) + jnp.sum(onehot * pad_start[None, :], axis=1)).astype(jnp.int32)
    blk_start = jnp.arange(N_MOE_BLOCKS, dtype=jnp.int32) * MOE_ROWS
    block_e = jnp.minimum(jnp.sum((pad_end[None, :] <= blk_start[:, None]).astype(jnp.int32), axis=1),
                          N_EXPERTS - 1).astype(jnp.int32)
    n_used = (pad_end[-1] // MOE_ROWS).astype(jnp.int32)
    e_hot = (block_e[:, None] == jnp.arange(N_EXPERTS, dtype=jnp.int32)[None, :]).astype(jnp.int32)
    nxt_blk = jnp.sum(e_hot * pad_end[None, :], axis=1) // MOE_ROWS
    b_hot = (nxt_blk[:, None] == jnp.arange(N_MOE_BLOCKS, dtype=jnp.int32)[None, :]).astype(jnp.int32)
    next_e = jnp.where(nxt_blk < n_used, jnp.sum(b_hot * block_e[None, :], axis=1), block_e).astype(jnp.int32)
    tail_start = (pad_start + counts).astype(jnp.int32)
    return dest, tail_start, block_e, n_used.reshape(1), next_e


def _rows_to_tiles(x):
    r = x.shape[0]
    blocks = jnp.stack([x[:, 128 * g:128 * (g + 1)].reshape(r // 8, 8, 128) for g in range(D_MODEL // 128)], axis=1)
    return jnp.swapaxes(blocks, 1, 2).reshape(r, D_MODEL // 128, 128)


def _tiles_to_rows(v):
    r = v.shape[0]
    blocks = jnp.swapaxes(v.reshape(r // 8, 8, D_MODEL // 128, 128), 1, 2)
    return jnp.concatenate([blocks[:, g].reshape(r, 128) for g in range(D_MODEL // 128)], axis=1)


def _dispatch_kernel(dest_ref, tail_ref, nb_ref, h_ref, xs_ref, zero_buf, stage, sem, ssem):
    i = pl.program_id(0)

    @pl.when(i == 0)
    def _():
        zero_buf[...] = jnp.zeros_like(zero_buf)
        fills = [pltpu.make_async_copy(zero_buf, xs_ref.at[pl.ds(tail_ref[e], MOE_ROWS)], sem)
                 for e in range(N_EXPERTS)]
        for f in fills:
            f.start()
        for f in fills:
            f.wait()

        def fill_block(b, c):
            f = pltpu.make_async_copy(zero_buf, xs_ref.at[pl.ds(b * MOE_ROWS, MOE_ROWS)], sem)
            f.start()
            f.wait()
            return c
        lax.fori_loop(nb_ref[0], (N_MOE_ROWS + XS_EXTRA) // MOE_ROWS, fill_block, 0)

    base = i * TM * TOP_K
    cur = i % 2
    row0 = pl.multiple_of(cur * TM, TM)

    def wait_tile(slot):
        for kk in range(TOP_K):
            pltpu.make_async_copy(stage.at[pl.ds(0, TM)], xs_ref.at[pl.ds(0, TM)], ssem.at[slot]).wait()

    @pl.when(i >= 2)
    def _():
        wait_tile(cur)
    stage[pl.ds(row0, TM)] = _rows_to_tiles(h_ref[...])

    def issue(t, c):
        for kk in range(TOP_K):
            pltpu.make_async_copy(stage.at[row0 + t], xs_ref.at[dest_ref[base + t * TOP_K + kk]], ssem.at[cur]
                                  ).start(priority=kk % 2)
        return c
    lax.fori_loop(0, TM, issue, 0, unroll=8)

    @pl.when(i == N_TILES - 1)
    def _():
        wait_tile(1 - cur)
        wait_tile(cur)


def _moe_dispatch(h, dest, tail_start, n_used):
    return pl.pallas_call(
        _dispatch_kernel,
        out_shape=jax.ShapeDtypeStruct((N_MOE_ROWS + XS_EXTRA,) + ROW_TILE, F32),
        grid_spec=pltpu.PrefetchScalarGridSpec(
            num_scalar_prefetch=3, grid=(N_TILES,),
            in_specs=[pl.BlockSpec((TM, D_MODEL), lambda i, d, t, nb: (i, 0))],
            out_specs=pl.BlockSpec(memory_space=pl.ANY),
            scratch_shapes=[pltpu.VMEM((MOE_ROWS,) + ROW_TILE, F32), pltpu.VMEM((2 * TM,) + ROW_TILE, F32),
                            pltpu.SemaphoreType.DMA, pltpu.SemaphoreType.DMA((2,))]),
        compiler_params=_params(1),
        name="moe_dispatch",
    )(dest, tail_start, n_used, h)


def _expert_kernel(be_ref, nb_ref, nxt_ref, x_ref, wgu_hbm, bgu_ref, wd_hbm, bd_ref, o_ref,
                   wgu_st, wd_st, wgu_bf, wd_bf, sem, *, layer):
    i = pl.program_id(0)
    e = be_ref[i]
    prev = be_ref[jnp.maximum(i - 1, 0)]

    def weight_copies(expert):
        idx = layer * N_EXPERTS + expert
        return (pltpu.make_async_copy(wgu_hbm.at[idx], wgu_st, sem.at[0]),
                pltpu.make_async_copy(wd_hbm.at[idx], wd_st, sem.at[1]))

    @pl.when(i == 0)
    def _():
        for cp in weight_copies(e):
            cp.start()

    @pl.when(((i == 0) | (e != prev)) & (i < nb_ref[0]))
    def _():
        for cp in weight_copies(e):
            cp.wait()
        wgu_bf[...] = wgu_st[...].astype(BF16)
        wd_bf[...] = wd_st[...].astype(BF16)
        nxt = nxt_ref[i]

        @pl.when(nxt != e)
        def _():
            for cp in weight_copies(nxt):
                cp.start()

    @pl.when(i < nb_ref[0])
    def _():
        gu = _dot(_tiles_to_rows(x_ref[...]).astype(BF16), wgu_bf[...]) + bgu_ref[0]
        gate = jnp.minimum(gu[:, :D_EXPERT], SWIGLU_LIMIT)
        up = jnp.clip(gu[:, D_EXPERT:], -SWIGLU_LIMIT, SWIGLU_LIMIT)
        act = (up + 1.0) * gate * _sigmoid(SWIGLU_ALPHA * gate)
        o_ref[...] = _rows_to_tiles(_dot(act.astype(BF16), wd_bf[...]) + bd_ref[0])

    @pl.when(i >= nb_ref[0])
    def _():
        o_ref[...] = jnp.zeros_like(o_ref)


def _moe_experts(xs, block_e, n_used, next_e, w_gu, b_gu, w_down, b_down, layer):
    w_idx = lambda i, be, nb, nx: (layer * N_EXPERTS + be[i], 0, 0)
    return pl.pallas_call(
        functools.partial(_expert_kernel, layer=layer),
        out_shape=jax.ShapeDtypeStruct((N_MOE_ROWS,) + ROW_TILE, F32),
        grid_spec=pltpu.PrefetchScalarGridSpec(
            num_scalar_prefetch=3, grid=(N_MOE_BLOCKS,),
            in_specs=[pl.BlockSpec((MOE_ROWS,) + ROW_TILE, lambda i, be, nb, nx: (jnp.minimum(i, nb[0] - 1), 0, 0)),
                      pl.BlockSpec(memory_space=pl.ANY),
                      pl.BlockSpec((1, 1, 2 * D_EXPERT), w_idx),
                      pl.BlockSpec(memory_space=pl.ANY),
                      pl.BlockSpec((1, 1, D_MODEL), w_idx)],
            out_specs=pl.BlockSpec((MOE_ROWS,) + ROW_TILE, lambda i, be, nb, nx: (i, 0, 0)),
            scratch_shapes=[pltpu.VMEM((D_MODEL, 2 * D_EXPERT), F32), pltpu.VMEM((D_EXPERT, D_MODEL), F32),
                            pltpu.VMEM((D_MODEL, 2 * D_EXPERT), BF16), pltpu.VMEM((D_EXPERT, D_MODEL), BF16),
                            pltpu.SemaphoreType.DMA((2,))]),
        compiler_params=_params(1),
        name="moe_experts",
    )(block_e, n_used, next_e, xs, w_gu, b_gu, w_down, b_down)


def _combine_kernel(dest_ref, x_ref, g2_ref, w_ref, eo_ref, yc_ref, yl_ref, buf, sem):
    i = pl.program_id(0)
    cur = i % 2

    def issue_tile(tile, slot):
        base = tile * TM * TOP_K
        row0 = slot * (TOP_K * TM)

        def issue(t, c):
            for kk in range(TOP_K):
                pltpu.make_async_copy(eo_ref.at[dest_ref[base + t * TOP_K + kk]], buf.at[row0 + kk * TM + t],
                                      sem.at[slot]).start(priority=kk % 2)
            return c
        lax.fori_loop(0, TM, issue, 0, unroll=8)

    @pl.when(i == 0)
    def _():
        issue_tile(0, 0)

    @pl.when(i + 1 < N_TILES)
    def _():
        issue_tile(i + 1, 1 - cur)

    for kk in range(TOP_K):
        pltpu.make_async_copy(eo_ref.at[pl.ds(0, TM)], buf.at[pl.ds(0, TM)], sem.at[cur]).wait()
    w = w_ref[...]
    ff = None
    for kk in range(TOP_K):
        rows = _tiles_to_rows(buf[pl.ds(pl.multiple_of(cur * (TOP_K * TM) + kk * TM, TM), TM)])
        term = rows * w[:, kk:kk + 1]
        ff = term if ff is None else ff + term
    y = x_ref[...] + g2_ref[0] * ff
    is_ctx = i < N_CTX_TILES

    @pl.when(is_ctx)
    def _():
        yc_ref[...] = y

    @pl.when(jnp.logical_not(is_ctx))
    def _():
        yl_ref[...] = y


def _moe_combine(dest, x1, modr, top_w, eo, layer):
    mrow = _mod_row(TM)
    base = layer * N_MOD
    return pl.pallas_call(
        _combine_kernel,
        out_shape=[jax.ShapeDtypeStruct((T_CTX, D_MODEL), F32), jax.ShapeDtypeStruct((T_LAT, D_MODEL), F32)],
        grid_spec=pltpu.PrefetchScalarGridSpec(
            num_scalar_prefetch=1, grid=(N_TILES,),
            in_specs=[pl.BlockSpec((TM, D_MODEL), lambda i, d: (i, 0)),
                      pl.BlockSpec((1, 1, D_MODEL), lambda i, d: ((base + mrow(i)) * 6 + 5, 0, 0)),
                      pl.BlockSpec((TM, 128), lambda i, d: (i, 0)),
                      pl.BlockSpec(memory_space=pl.ANY)],
            out_specs=[pl.BlockSpec((TM, D_MODEL), lambda i, d: (jnp.minimum(i, N_CTX_TILES - 1), 0)),
                       pl.BlockSpec((TM, D_MODEL), lambda i, d: (jnp.maximum(i - N_CTX_TILES, 0), 0))],
            scratch_shapes=[pltpu.VMEM((2 * TOP_K * TM,) + ROW_TILE, F32), pltpu.SemaphoreType.DMA((2,))]),
        compiler_params=_params(1),
        name="moe_combine",
    )(dest, x1, modr, top_w, eo)


def _pad_heads(w, n_heads, width):
    lead = w.shape[:-1]
    w = w.reshape(lead + (n_heads, width))
    w = jnp.pad(w, [(0, 0)] * len(lead) + [(0, 0), (0, HEAD_PAD - width)])
    return w.reshape(lead + (n_heads * HEAD_PAD,))


def _permute_w_in_kernel(w_ref, o_ref):
    w = w_ref[0]
    o_ref[0] = jnp.concatenate([w[:, :3072], w[:, 3520:], w[:, 3104:3520], w[:, 3072:3104],
                                jnp.zeros((w.shape[0], DZ - 6592), F32)], axis=1).astype(BF16)


def _permute_w_in(w_in):
    rows = 128
    return pl.pallas_call(
        _permute_w_in_kernel,
        out_shape=jax.ShapeDtypeStruct((DEPTH, D_MODEL, DZ), BF16),
        grid=(DEPTH, D_MODEL // rows),
        in_specs=[pl.BlockSpec((1, rows, w_in.shape[2]), lambda l, r: (l, r, 0))],
        out_specs=pl.BlockSpec((1, rows, DZ), lambda l, r: (l, r, 0)),
        compiler_params=_params(2),
        name="permute_w_in",
    )(w_in)


def _layer_weights(l, gla_wa2, gla_ba, mla_qa_g, mla_wuq, mla_kva_g, mla_wukv, mla_qn_g, mla_kn_g,
                   w_branch, w_out, router_w, router_b, norm2_g):
    wa_p = jnp.zeros((512, 512), F32)
    wa_p = wa_p.at[416:432, 0:256].set(gla_wa2[l, 0]).at[432:448, 256:512].set(gla_wa2[l, 1]).astype(BF16)
    ba_p = gla_ba[l].reshape(1, 512)
    wukv = mla_wukv[l].reshape(128, MLA_HEADS, MLA_NOPE + MLA_V)
    rw = jnp.pad(router_w[l], ((0, 0), (0, 128 - N_EXPERTS)))
    rw_hi = rw.astype(BF16)
    return {
        "wa": wa_p, "ba": ba_p,
        "qa_g": mla_qa_g[l].reshape(1, 256),
        "wuq": _pad_heads(mla_wuq[l], MLA_HEADS, MLA_QK).astype(BF16),
        "qn_g": jnp.tile(jnp.pad(mla_qn_g[l], (0, HEAD_PAD - MLA_QK)), MLA_HEADS).reshape(1, 1024) * MLA_QK ** -0.5,
        "kn_g": jnp.tile(jnp.pad(mla_kn_g[l], (0, HEAD_PAD - MLA_QK)), MLA_HEADS).reshape(1, 1024),
        "kva_g": mla_kva_g[l].reshape(1, 128),
        "wk": _pad_heads(wukv[:, :, :MLA_NOPE].reshape(128, MLA_HEADS * MLA_NOPE), MLA_HEADS, MLA_NOPE).astype(BF16),
        "wv": wukv[:, :, MLA_NOPE:].reshape(128, MLA_HEADS * MLA_V).astype(BF16),
        "wb": w_branch[l].astype(BF16), "wo": w_out[l].astype(BF16),
        "rw_hi": rw_hi, "rw_lo": (rw - rw_hi.astype(F32)).astype(BF16),
        "rb": jnp.pad(router_b[l], (0, 128 - N_EXPERTS)).reshape(1, 128),
        "n2_g": norm2_g[l].reshape(1, D_MODEL),
    }


def kernel(x_prompt, x_sample, cache_mla_ckv, cache_mla_krope, state_ret, state_gla, c, c_ctx, w_mod, b_mod, norm1_g, norm2_g, w_in, ret_gn_g, gla_wa2, gla_ba, gla_norm_g, mla_qa_g, mla_wuq, mla_kva_g, mla_wukv, mla_qn_g, mla_kn_g, w_branch, w_out, router_w, router_b, moe_w_gu, moe_b_gu, moe_w_down, moe_b_down):
    xc, xl = x_prompt.reshape(T_CTX, D_MODEL), x_sample.reshape(T_LAT, D_MODEL)
    cc = jnp.concatenate([c_ctx[None, :], c, jnp.zeros((N_MOD - 1 - N_LAT_SEQ, D_MODEL), F32)], axis=0)
    modr = _modulation(cc, w_mod, b_mod).reshape(DEPTH * N_MOD * 6, 1, D_MODEL)
    rope_tabs = _rope_tables()
    w_in_p = _permute_w_in(w_in)
    w_gu = moe_w_gu.reshape(DEPTH * N_EXPERTS, D_MODEL, 2 * D_EXPERT)
    b_gu = moe_b_gu.reshape(DEPTH * N_EXPERTS, 1, 2 * D_EXPERT)
    w_dn = moe_w_down.reshape(DEPTH * N_EXPERTS, D_EXPERT, D_MODEL)
    b_dn = moe_b_down.reshape(DEPTH * N_EXPERTS, 1, D_MODEL)

    ckv_l, krope_l, ret_l, gla_l = [], [], [], []
    for l in range(DEPTH):
        w = _layer_weights(l, gla_wa2, gla_ba, mla_qa_g, mla_wuq, mla_kva_g, mla_wukv, mla_qn_g, mla_kn_g,
                           w_branch, w_out, router_w, router_b, norm2_g)
        z = _in_proj(xc, xl, norm1_g[l].reshape(1, D_MODEL), modr, w_in_p, l)

        gn = ret_gn_g[l].reshape(1, 512)
        ret_c, ret_state = _retention(z, gn, None, ctx=True)
        (ret_s,) = _retention(z, gn, state_ret[:, l], ctx=False)
        gng = gla_norm_g[l].reshape(1, 512)
        gla_c, gla_state = _gla(z, w["wa"], w["ba"], gng, None, ctx=True)
        gla_s, _ = _gla(z, w["wa"], w["ba"], gng, state_gla[:, l], ctx=False)

        q, k, v, ckv, kr = _mla_prep(z, w, rope_tabs)
        kc, vc = _mla_cache(cache_mla_ckv[:, l].reshape(N_LAT_SEQ * CTX_LEN, 128),
                            jnp.pad(cache_mla_krope[:, l].reshape(N_LAT_SEQ * CTX_LEN, MLA_ROPE),
                                    ((0, 0), (0, 128 - MLA_ROPE))), w)
        mla_c = _mla_attn(q, k, v, None, None, ctx=True)
        mla_s = _mla_attn(q, k, v, kc, vc, ctx=False)

        x1, h2, top_idx, top_w, rank, cnt = _merge((ret_c, ret_s, gla_c, gla_s, mla_c, mla_s), z, xc, xl, modr, w, l)
        dest, tail_start, block_e, n_used, next_e = _route(top_idx[:, :TOP_K], rank[:, :TOP_K],
                                                           cnt[0, :N_EXPERTS].astype(jnp.int32))
        xs = _moe_dispatch(h2, dest, tail_start, n_used)
        eo = _moe_experts(xs, block_e, n_used, next_e, w_gu, b_gu, w_dn, b_dn, l)
        xc, xl = _moe_combine(dest, x1, modr, top_w, eo, l)

        ckv_l.append(ckv[:T_CTX].reshape(N_CTX_SEQ, CTX_LEN, 128))
        krope_l.append(kr[:T_CTX, :MLA_ROPE].reshape(N_CTX_SEQ, CTX_LEN, MLA_ROPE))
        ret_l.append(ret_state)
        gla_l.append(gla_state.reshape(N_CTX_SEQ, 2, N_HEADS, DK, DV))

    y_p = xc.reshape(N_CTX_SEQ, CTX_LEN, D_MODEL)
    y_s = xl.reshape(N_LAT_SEQ, LAT_LEN, D_MODEL)
    return (y_p, y_s, jnp.stack(ckv_l, axis=1), jnp.stack(krope_l, axis=1),
            jnp.stack(ret_l, axis=1), jnp.stack(gla_l, axis=1))
```

```python
import functools

import jax
import jax.numpy as jnp
import numpy as np
from jax import lax
from jax.experimental import pallas as pl
from jax.experimental.pallas import tpu as pltpu

F32 = jnp.float32
BF16 = jnp.bfloat16

D_MODEL = 1024
DEPTH = 2
N_CTX_SEQ, CTX_LEN = 32, 256
N_LAT_SEQ, LAT_LEN = 4, 1024
T_CTX = N_CTX_SEQ * CTX_LEN
T_LAT = N_LAT_SEQ * LAT_LEN
T_ALL = T_CTX + T_LAT
TM = 256
N_TILES = T_ALL // TM
N_CTX_TILES = T_CTX // TM
LAT_TILES = LAT_LEN // TM
N_MOD = 8
EPS = 1e-6

N_HEADS = 4
DK, DV = 64, 128
GRID_W = 64
MLA_HEADS, MLA_NOPE, MLA_ROPE, MLA_V = 8, 64, 32, 64
MLA_QK = MLA_NOPE + MLA_ROPE
HEAD_PAD = 128
GLA_TAU = 16.0
N_EXPERTS, TOP_K, D_EXPERT = 32, 4, 1024
SWIGLU_LIMIT, SWIGLU_ALPHA = 7.0, 1.702
MOE_ROWS = 512
N_SLOTS = T_ALL * TOP_K
N_MOE_BLOCKS = N_SLOTS // MOE_ROWS + N_EXPERTS
N_MOE_ROWS = N_MOE_BLOCKS * MOE_ROWS
MERGE_TM = 512
ROUTE_CHUNKS = 8
XS_EXTRA = MOE_ROWS
ROW_TILE = (D_MODEL // 128, 128)

DZ = 6656
IN_TILE = 512
VMEM_LIMIT = 56 * 1024 * 1024

RET_LOG_F = [float(np.log1p(-np.exp2(-(5.0 + h)))) for h in range(N_HEADS)]
RET_LOG_B = [float(np.log1p(-np.exp2(-(5.5 + h)))) for h in range(N_HEADS)]


def _params(n_axes, vmem=VMEM_LIMIT):
    return pltpu.CompilerParams(dimension_semantics=("arbitrary",) * n_axes, vmem_limit_bytes=vmem)


def _sigmoid(x):
    return 1.0 / (1.0 + jnp.exp(-x))


def _dot(a, b):
    return jnp.dot(a, b, preferred_element_type=F32)


def _dot_t(a, b):
    return lax.dot_general(a, b, (((1,), (1,)), ((), ())), preferred_element_type=F32)


def _mod_row(tile_rows):
    def f(i):
        r0 = i * tile_rows
        return jnp.where(r0 < T_CTX, 0, 1 + (r0 - T_CTX) // LAT_LEN)
    return f


def _ctx_spec(rows, width):
    n_ctx = T_CTX // rows
    return pl.BlockSpec((rows, width), lambda i: (jnp.minimum(i, n_ctx - 1), 0))


def _lat_spec(rows, width):
    n_ctx = T_CTX // rows
    return pl.BlockSpec((rows, width), lambda i: (jnp.maximum(i - n_ctx, 0), 0))


def _mod_kernel(c_ref, w_ref, b_ref, o_ref):
    c = c_ref[...]
    s = c * _sigmoid(c)
    sh = s.astype(BF16)
    sl = (s - sh.astype(F32)).astype(BF16)
    w = w_ref[0]
    wh = w.astype(BF16)
    wl = (w - wh.astype(F32)).astype(BF16)
    o_ref[0] = _dot(sh, wh) + _dot(sh, wl) + _dot(sl, wh) + b_ref[0]


def _modulation(cc, w_mod, b_mod):
    n = 6 * D_MODEL
    blk = 2048
    return pl.pallas_call(
        _mod_kernel,
        out_shape=jax.ShapeDtypeStruct((DEPTH, N_MOD, n), F32),
        grid=(DEPTH, n // blk),
        in_specs=[pl.BlockSpec((N_MOD, D_MODEL), lambda l, j: (0, 0)),
                  pl.BlockSpec((1, D_MODEL, blk), lambda l, j: (l, 0, j)),
                  pl.BlockSpec((1, 1, blk), lambda l, j: (l, 0, j))],
        out_specs=pl.BlockSpec((1, N_MOD, blk), lambda l, j: (l, 0, j)),
        compiler_params=_params(2),
        name="modulation",
    )(cc, w_mod, b_mod.reshape(DEPTH, 1, n))


def _in_kernel(xc_ref, xl_ref, g_ref, sh_ref, sc_ref, w_ref, o_ref):
    x = jnp.where(pl.program_id(0) < T_CTX // IN_TILE, xc_ref[...], xl_ref[...])
    h = x * lax.rsqrt(jnp.mean(x * x, axis=-1, keepdims=True) + EPS) * g_ref[...]
    h = h * (1.0 + sc_ref[0]) + sh_ref[0]
    hb = h.astype(BF16)
    for n0 in range(0, DZ, 512):
        o_ref[:, n0:n0 + 512] = _dot(hb, w_ref[0, :, n0:n0 + 512]).astype(BF16)


def _in_proj(xc, xl, g, modr, w_in_p, layer):
    mrow = _mod_row(IN_TILE)
    base = layer * N_MOD

    def mod_spec(part):
        return pl.BlockSpec((1, 1, D_MODEL), lambda i: ((base + mrow(i)) * 6 + part, 0, 0))

    return pl.pallas_call(
        _in_kernel,
        out_shape=jax.ShapeDtypeStruct((T_ALL, DZ), BF16),
        grid=(T_ALL // IN_TILE,),
        in_specs=[_ctx_spec(IN_TILE, D_MODEL), _lat_spec(IN_TILE, D_MODEL),
                  pl.BlockSpec((1, D_MODEL), lambda i: (0, 0)),
                  mod_spec(0), mod_spec(1),
                  pl.BlockSpec((1, D_MODEL, DZ), lambda i: (layer, 0, 0))],
        out_specs=pl.BlockSpec((IN_TILE, DZ), lambda i: (i, 0)),
        compiler_params=_params(1),
        name="in_proj",
    )(xc, xl, g, modr, modr, w_in_p)


def _lane_half_mask(hh):
    lane = lax.broadcasted_iota(jnp.int32, (1, 128), 1)
    return (lane < 64) if hh == 0 else (lane >= 64)


@functools.lru_cache(maxsize=None)
def _ret_decay_table(seq):
    d = np.arange(seq)[:, None] - np.arange(seq)[None, :]
    tab = np.stack([np.exp(np.where(d > 0, RET_LOG_F[h] * d, -RET_LOG_B[h] * d)) for h in range(N_HEADS)])
    return (tab * np.where(d == 0, 2.0, 1.0) * DK ** -0.5).astype(np.float32)


def _ret_kernel(*refs, seq, has_state, emit_state):
    q_ref, k_ref, v_ref, g_ref, gn_ref, dec_ref = refs[:6]
    pos = 6
    if has_state:
        s0_ref = refs[pos]
        pos += 1
    o_ref = refs[pos]
    pos += 1
    if emit_state:
        st_ref = refs[pos]

    r0 = pl.multiple_of(pl.program_id(1) * TM, TM)
    qb = q_ref[pl.ds(r0, TM), :]
    ri = (lax.broadcasted_iota(jnp.int32, (TM, 1), 0) + r0).astype(F32)

    for h in range(N_HEADS):
        p, hh = h // 2, h % 2
        lanes = slice(128 * p, 128 * p + 128)
        qp = qb[:, lanes]
        qh = jnp.where(_lane_half_mask(hh), qp, jnp.zeros_like(qp))
        sc = _dot_t(qh, k_ref[:, lanes])
        o = _dot((sc * dec_ref[h]).astype(BF16), v_ref[:, 128 * h:128 * h + 128])
        if has_state:
            qf = qh.astype(F32)
            o += _dot((qf * jnp.exp(RET_LOG_F[h] * (ri + 1.0))).astype(BF16), s0_ref[0, 0, p].astype(BF16))
            o += _dot((qf * jnp.exp(RET_LOG_B[h] * (seq - ri))).astype(BF16), s0_ref[0, 1, p].astype(BF16))
        mu = jnp.mean(o, axis=-1, keepdims=True)
        d = o - mu
        var = jnp.mean(d * d, axis=-1, keepdims=True)
        on = d * lax.rsqrt(var + EPS)
        g = g_ref[:, 128 * h:128 * h + 128].astype(F32)
        out = on * gn_ref[:, 128 * h:128 * h + 128] * (g * _sigmoid(g))
        o_ref[:, 128 * h:128 * h + 128] = out.astype(BF16)

    if emit_state:
        jc = lax.broadcasted_iota(jnp.int32, (seq, 1), 0).astype(F32)
        lane = lax.broadcasted_iota(jnp.int32, (1, 128), 1)
        for p in range(2):
            kp = k_ref[:, 128 * p:128 * p + 128].astype(F32) * DK ** -0.5
            lgf = jnp.where(lane < 64, RET_LOG_F[2 * p], RET_LOG_F[2 * p + 1])
            lgb = jnp.where(lane < 64, RET_LOG_B[2 * p], RET_LOG_B[2 * p + 1])
            kdf = (kp * jnp.exp(lgf * (seq - 1.0 - jc))).T.astype(BF16)
            kdb = (kp * jnp.exp(lgb * jc)).T.astype(BF16)
            for hh in range(2):
                h = 2 * p + hh
                vh = v_ref[:, 128 * h:128 * h + 128]
                st_ref[0, 0, h] = _dot(kdf, vh)[64 * hh:64 * hh + 64, :]
                st_ref[0, 1, h] = _dot(kdb, vh)[64 * hh:64 * hh + 64, :]


def _retention(z, gn_g, s0, *, ctx):
    if ctx:
        nb, seq, row_blk, tile0 = N_CTX_SEQ, CTX_LEN, 0, 0
    else:
        nb, seq, row_blk, tile0 = N_LAT_SEQ, LAT_LEN, T_CTX // LAT_LEN, N_CTX_TILES
    nq = seq // TM
    in_specs = [pl.BlockSpec((seq, 256), lambda b, i: (row_blk + b, 0)),
                pl.BlockSpec((seq, 256), lambda b, i: (row_blk + b, 1)),
                pl.BlockSpec((seq, 512), lambda b, i: (row_blk + b, 1)),
                pl.BlockSpec((TM, 512), lambda b, i: (tile0 + b * nq + i, 2)),
                pl.BlockSpec((1, 512), lambda b, i: (0, 0)),
                pl.BlockSpec((N_HEADS, TM, seq), lambda b, i: (0, i, 0))]
    args = [z, z, z, z, gn_g, _ret_decay_table(seq)]
    out_shape = [jax.ShapeDtypeStruct((nb * seq, 512), BF16)]
    out_specs = [pl.BlockSpec((TM, 512), lambda b, i: (b * nq + i, 0))]
    if not ctx:
        in_specs.append(pl.BlockSpec((1, 2, 2, 128, 128), lambda b, i: (b, 0, 0, 0, 0)))
        args.append(s0.reshape(N_LAT_SEQ, 2, 2, 128, 128))
    else:
        out_shape.append(jax.ShapeDtypeStruct((nb, 2, N_HEADS, DK, DV), F32))
        out_specs.append(pl.BlockSpec((1, 2, N_HEADS, DK, DV), lambda b, i: (b, 0, 0, 0, 0)))
    return pl.pallas_call(
        functools.partial(_ret_kernel, seq=seq, has_state=not ctx, emit_state=ctx),
        out_shape=out_shape, grid=(nb, nq), in_specs=in_specs, out_specs=out_specs,
        compiler_params=_params(2),
        name="retention_ctx" if ctx else "retention_lat",
    )(*args)


def _gla_decay(small_ref, wa_ref, ba_ref):
    x = _dot(small_ref[...], wa_ref[...]) + ba_ref[...]
    la = -(jnp.maximum(-x, 0.0) + jnp.log(1.0 + jnp.exp(-jnp.abs(x)))) * (1.0 / GLA_TAU)
    ri = lax.broadcasted_iota(jnp.int32, (TM, TM), 0)
    ci = lax.broadcasted_iota(jnp.int32, (TM, TM), 1)
    ltri = jnp.where(ri >= ci, 1.0, 0.0).astype(BF16)
    hi = la.astype(BF16)
    r1 = la - hi.astype(F32)
    mid = r1.astype(BF16)
    lo = (r1 - mid.astype(F32)).astype(BF16)
    cum = _dot(ltri, hi) + _dot(ltri, mid) + _dot(ltri, lo)
    return la, cum


def _gla_state_kernel(k_ref, v_ref, small_ref, wa_ref, ba_ref, kv_ref, tot_ref):
    la, cum = _gla_decay(small_ref, wa_ref, ba_ref)
    bf, bb = cum[:, :256], cum[:, 256:]
    xb = bb - la[:, 256:]
    k = k_ref[...].astype(F32)
    kdf = k * jnp.exp(bf[TM - 1:TM, :] - bf)
    kdb = k * jnp.exp(xb)
    for p in range(2):
        kf_t = kdf[:, 128 * p:128 * p + 128].T.astype(BF16)
        kb_t = kdb[:, 128 * p:128 * p + 128].T.astype(BF16)
        for hh in range(2):
            h = 2 * p + hh
            vh = v_ref[:, 128 * h:128 * h + 128]
            kv_ref[0, 0, 0, h] = _dot(kf_t, vh)[64 * hh:64 * hh + 64, :]
            kv_ref[0, 0, 1, h] = _dot(kb_t, vh)[64 * hh:64 * hh + 64, :]
    tot_ref[0, 0] = jnp.sum(la.T, axis=-1, keepdims=True)


def _mid_bcast(x, s, r):
    w = 2 * s
    if w >= 8:
        n = TM // w
        x3 = x.reshape(n, w, 256)
        return jnp.broadcast_to(x3[:, r:r + 1, :], (n, w, 256)).reshape(TM, 256)
    x3 = x.reshape(TM // 8, 8, 256)
    sub = lax.broadcasted_iota(jnp.int32, (1, 8, 1), 1)
    out = None
    for blk in range(8 // w):
        rowv = jnp.broadcast_to(x3[:, blk * w + r:blk * w + r + 1, :], (TM // 8, 8, 256))
        out = rowv if out is None else jnp.where(sub >= blk * w, rowv, out)
    return out.reshape(TM, 256)


def _gla_kernel(*refs, n_blk, has_state, emit_state):
    q_ref, k_ref, v_ref, g_ref, small_ref, wa_ref, ba_ref, gn_ref = refs[:8]
    pos = 8
    if has_state:
        kv_ref, tot_ref, s0_ref = refs[pos:pos + 3]
        pos += 3
    o_ref = refs[pos]
    pos += 1
    if emit_state:
        kvo_ref = refs[pos]

    la, cum = _gla_decay(small_ref, wa_ref, ba_ref)
    bf, bb = cum[:, :256], cum[:, 256:]
    xb = bb - la[:, 256:]
    q = q_ref[...].astype(F32) * DK ** -0.5
    k = k_ref[...].astype(F32)
    row = lax.broadcasted_iota(jnp.int32, (TM, 1), 0)
    rowi = lax.broadcasted_iota(jnp.int32, (TM, TM), 0)
    colj = lax.broadcasted_iota(jnp.int32, (TM, TM), 1)
    low_half = _lane_half_mask(0)

    def join(fwd, bwd):
        ops = []
        for p in range(N_HEADS // 2):
            f = fwd[:, 128 * p:128 * p + 128]
            br = pltpu.roll(bwd[:, 128 * p:128 * p + 128], 64, 1)
            ops.append(jnp.where(low_half, f, br).astype(BF16))
            ops.append(jnp.where(low_half, br, f).astype(BF16))
        return ops

    qd, kd = join(q, q), join(k, k)
    acc = [jnp.where(rowi == colj, _dot_t(qo, ko), 0.0) for qo, ko in zip(qd, kd)]

    low_f = jnp.where(low_half, 1.0, 0.0)
    s = 1
    while s < TM:
        up_f = jnp.where(((row // s) % 2) == 1, 1.0, 0.0)
        live_even = jnp.where(up_f == low_f, 1.0, 0.0).astype(BF16)
        live = [live_even, 1.0 - live_even]
        dead = [live[1], live[0]]
        af = -jnp.abs(bf - _mid_bcast(bf, s, s - 1))
        ab = -jnp.abs(xb - _mid_bcast(xb, s, s))
        same = (rowi // (2 * s)) == (colj // (2 * s))
        for p in range(N_HEADS // 2):
            a_f = af[:, 128 * p:128 * p + 128]
            a_b = pltpu.roll(ab[:, 128 * p:128 * p + 128], 64, 1)
            for hh in range(2):
                h = 2 * p + hh
                arg = jnp.where(low_half, a_f, a_b) if hh == 0 else jnp.where(low_half, a_b, a_f)
                e = jnp.exp(arg).astype(BF16)
                sl = _dot_t(e * (qd[h] * live[hh]), e * (kd[h] * dead[hh]))
                acc[h] = acc[h] + (jnp.where(same, sl, 0.0) if 2 * s < TM else sl)
        s *= 2

    if has_state:
        n = pl.program_id(1)
        q_state = join(q * jnp.exp(bf), q * jnp.exp(bb[TM - 1:TM, :] - xb))

    if emit_state:
        kdf = k * jnp.exp(bf[TM - 1:TM, :] - bf)
        kdb = k * jnp.exp(xb)
        for p in range(N_HEADS // 2):
            kf_t = kdf[:, 128 * p:128 * p + 128].T.astype(BF16)
            kb_t = kdb[:, 128 * p:128 * p + 128].T.astype(BF16)
            for hh in range(2):
                h = 2 * p + hh
                vh = v_ref[:, 128 * h:128 * h + 128]
                kvo_ref[0, 0, 0, h] = _dot(kf_t, vh)[64 * hh:64 * hh + 64, :]
                kvo_ref[0, 0, 1, h] = _dot(kb_t, vh)[64 * hh:64 * hh + 64, :]

    for h in range(N_HEADS):
        o = _dot(acc[h].astype(BF16), v_ref[:, 128 * h:128 * h + 128])
        if has_state:
            sf = s0_ref[0, 0, h]
            for m in range(n_blk - 1):
                dec = jnp.exp(tot_ref[0, m, 64 * h:64 * h + 64, :])
                sf = jnp.where(m < n, dec * sf + kv_ref[0, m, 0, h], sf)
            sb = s0_ref[0, 1, h]
            for m in range(n_blk - 1, 0, -1):
                dec = jnp.exp(tot_ref[0, m, 256 + 64 * h:256 + 64 * h + 64, :])
                sb = jnp.where(m > n, dec * sb + kv_ref[0, m, 1, h], sb)
            state = jnp.concatenate([sf, sb] if h % 2 == 0 else [sb, sf], axis=0).astype(BF16)
            o += _dot(q_state[h], state)
        on = o * lax.rsqrt(jnp.mean(o * o, axis=-1, keepdims=True) + EPS)
        g = g_ref[:, 128 * h:128 * h + 128].astype(F32)
        out = on * gn_ref[:, 128 * h:128 * h + 128] * (g * _sigmoid(g))
        o_ref[:, 128 * h:128 * h + 128] = out.astype(BF16)


def _gla(z, wa_p, ba_p, gn_g, s0, *, ctx):
    if ctx:
        nb, n_blk, tile0 = N_CTX_SEQ, 1, 0
    else:
        nb, n_blk, tile0 = N_LAT_SEQ, LAT_TILES, N_CTX_TILES

    def zspec(width, col):
        return pl.BlockSpec((TM, width), lambda b, n: (tile0 + b * n_blk + n, col))

    w_specs = [pl.BlockSpec((512, 512), lambda b, n: (0, 0)), pl.BlockSpec((1, 512), lambda b, n: (0, 0))]
    kv_shape = jax.ShapeDtypeStruct((nb, n_blk, 2, N_HEADS, DK, DV), F32)
    kv_spec = pl.BlockSpec((1, 1, 2, N_HEADS, DK, DV), lambda b, n: (b, n, 0, 0, 0, 0))
    in_specs = [zspec(256, 6), zspec(256, 7), zspec(512, 4), zspec(512, 5), zspec(512, 12)] + w_specs
    in_specs.append(pl.BlockSpec((1, 512), lambda b, n: (0, 0)))
    args = [z, z, z, z, z, wa_p, ba_p, gn_g]
    out_shape = [jax.ShapeDtypeStruct((nb * n_blk * TM, 512), BF16)]
    out_specs = [pl.BlockSpec((TM, 512), lambda b, n: (b * n_blk + n, 0))]
    if ctx:
        out_shape.append(kv_shape)
        out_specs.append(kv_spec)
    else:
        kv, tot = pl.pallas_call(
            _gla_state_kernel,
            out_shape=[kv_shape, jax.ShapeDtypeStruct((nb, n_blk, 512, 1), F32)],
            grid=(nb, n_blk),
            in_specs=[zspec(256, 7), zspec(512, 4), zspec(512, 12)] + w_specs,
            out_specs=[kv_spec, pl.BlockSpec((1, 1, 512, 1), lambda b, n: (b, n, 0, 0))],
            compiler_params=_params(2),
            name="gla_state_lat",
        )(z, z, z, wa_p, ba_p)
        in_specs += [pl.BlockSpec((1, n_blk, 2, N_HEADS, DK, DV), lambda b, n: (b, 0, 0, 0, 0, 0)),
                     pl.BlockSpec((1, n_blk, 512, 1), lambda b, n: (b, 0, 0, 0)),
                     pl.BlockSpec((1, 2, N_HEADS, DK, DV), lambda b, n: (b, 0, 0, 0, 0))]
        args += [kv, tot, s0]
    res = pl.pallas_call(
        functools.partial(_gla_kernel, n_blk=n_blk, has_state=not ctx, emit_state=ctx),
        out_shape=out_shape, grid=(nb, n_blk), in_specs=in_specs, out_specs=out_specs,
        compiler_params=_params(2),
        name="gla_ctx" if ctx else "gla_lat",
    )(*args)
    return (res[0], res[1]) if ctx else (res[0], None)


def _rope_tables():
    nf = MLA_ROPE // 4
    pos = np.arange(LAT_LEN)
    freqs = (10000.0 ** (-np.arange(nf, dtype=np.float32) / nf)).astype(np.float32)
    ang_r = ((pos // GRID_W).astype(np.float32)[:, None] * freqs).astype(np.float32)
    ang_c = ((pos % GRID_W).astype(np.float32)[:, None] * freqs).astype(np.float32)
    cos = np.ones((TM + LAT_LEN, HEAD_PAD), np.float32)
    sa = np.zeros((TM + LAT_LEN, HEAD_PAD), np.float32)
    sb = np.zeros((TM + LAT_LEN, HEAD_PAD), np.float32)
    o = MLA_NOPE
    for base, ang in ((o, ang_r), (o + 2 * nf, ang_c)):
        cos[TM:, base:base + nf] = np.cos(ang)
        cos[TM:, base + nf:base + 2 * nf] = np.cos(ang)
        sa[TM:, base:base + nf] = -np.sin(ang)
        sb[TM:, base + nf:base + 2 * nf] = np.sin(ang)
    return jnp.asarray(cos), jnp.asarray(sa), jnp.asarray(sb)


def _rope(x, cos, sa, sb):
    return x * cos + pltpu.roll(x, 128 - 8, 1) * sa + pltpu.roll(x, 8, 1) * sb


def _head_segments():
    seg = np.zeros((MLA_HEADS * HEAD_PAD, 128), np.float32)
    for h in range(MLA_HEADS):
        seg[h * HEAD_PAD:(h + 1) * HEAD_PAD, h] = 1.0
    return jnp.asarray(seg, BF16), jnp.asarray(seg.T.copy(), BF16)


def _head_norm(x, gain, seg_ref, segt_ref):
    ss = _dot((x * x).astype(BF16), seg_ref[...])
    rs = lax.rsqrt(ss * (1.0 / MLA_QK) + EPS)
    hi = rs.astype(BF16)
    lo = (rs - hi.astype(F32)).astype(BF16)
    return x * (_dot(hi, segt_ref[...]) + _dot(lo, segt_ref[...])) * gain


def _store_heads(x, o_ref, rope):
    if rope is None:
        o_ref[...] = x.astype(BF16)
    else:
        for h in range(MLA_HEADS):
            o_ref[:, 128 * h:128 * h + 128] = _rope(x[:, 128 * h:128 * h + 128], *rope).astype(BF16)


def _mla_keys(ckv, kr_tile, wk_ref, wv_ref, kn_ref, seg_ref, segt_ref, k_ref, v_ref, rope=None):
    cb = ckv.astype(BF16)
    kpre = _dot(cb, wk_ref[...]) + jnp.concatenate([kr_tile] * MLA_HEADS, axis=1)
    v_ref[...] = _dot(cb, wv_ref[...]).astype(BF16)
    _store_heads(_head_norm(kpre, kn_ref[...], seg_ref, segt_ref), k_ref, rope)


def _mla_prep_kernel(small_ref, qa_ref, wuq_ref, qn_ref, kva_ref, wk_ref, wv_ref, kn_ref, seg_ref, segt_ref,
                     cos_ref, sa_ref, sb_ref, q_ref, k_ref, v_ref, ckv_ref, kr_ref):
    def body(rope):
        sm = small_ref[...].astype(F32)
        cq, ckv_raw, g3 = sm[:, 0:256], sm[:, 256:384], sm[:, 384:512]
        cqn = cq * lax.rsqrt(jnp.mean(cq * cq, axis=-1, keepdims=True) + EPS) * qa_ref[...]
        q = _dot(cqn.astype(BF16), wuq_ref[...])
        _store_heads(_head_norm(q, qn_ref[...], seg_ref, segt_ref), q_ref, rope)
        ckv = ckv_raw * lax.rsqrt(jnp.mean(ckv_raw * ckv_raw, axis=-1, keepdims=True) + EPS) * kva_ref[...]
        ckv_ref[...] = ckv
        lane = lax.broadcasted_iota(jnp.int32, (1, 128), 1)
        kr = jnp.where(lane < MLA_ROPE, g3, 0.0)
        kr_ref[...] = kr
        _mla_keys(ckv, pltpu.roll(kr, MLA_NOPE, 1), wk_ref, wv_ref, kn_ref, seg_ref, segt_ref, k_ref, v_ref, rope)

    is_ctx = pl.program_id(0) < N_CTX_TILES

    @pl.when(is_ctx)
    def _():
        body(None)

    @pl.when(jnp.logical_not(is_ctx))
    def _():
        body((cos_ref[...], sa_ref[...], sb_ref[...]))


def _mla_cache_kernel(ckv_ref, kr_ref, wk_ref, wv_ref, kn_ref, seg_ref, segt_ref, k_ref, v_ref):
    _mla_keys(ckv_ref[...], pltpu.roll(kr_ref[...], MLA_NOPE, 1), wk_ref, wv_ref, kn_ref, seg_ref, segt_ref,
              k_ref, v_ref)


def _mla_prep(z, w, rope_tabs):
    def rope_blk(i):
        return jnp.where(i < N_CTX_TILES, 0, 1 + (i - N_CTX_TILES) % LAT_TILES)

    const = lambda shape: pl.BlockSpec(shape, lambda i: (0,) * len(shape))
    rope_spec = pl.BlockSpec((TM, HEAD_PAD), lambda i: (rope_blk(i), 0))
    row = lambda width: pl.BlockSpec((TM, width), lambda i: (i, 0))
    return pl.pallas_call(
        _mla_prep_kernel,
        out_shape=[jax.ShapeDtypeStruct((T_ALL, 1024), BF16), jax.ShapeDtypeStruct((T_ALL, 1024), BF16),
                   jax.ShapeDtypeStruct((T_ALL, 512), BF16), jax.ShapeDtypeStruct((T_ALL, 128), F32),
                   jax.ShapeDtypeStruct((T_ALL, 128), F32)],
        grid=(N_TILES,),
        in_specs=[pl.BlockSpec((TM, 512), lambda i: (i, 12)),
                  const((1, 256)), const((256, 1024)), const((1, 1024)), const((1, 128)),
                  const((128, 1024)), const((128, 512)), const((1, 1024)),
                  const((1024, 128)), const((128, 1024)),
                  rope_spec, rope_spec, rope_spec],
        out_specs=[row(1024), row(1024), row(512), row(128), row(128)],
        compiler_params=_params(1),
        name="mla_prep",
    )(z, w["qa_g"], w["wuq"], w["qn_g"], w["kva_g"], w["wk"], w["wv"], w["kn_g"], *_head_segments(), *rope_tabs)


def _mla_cache(ckv, kr_pad, w):
    const = lambda shape: pl.BlockSpec(shape, lambda i: (0,) * len(shape))
    row = lambda width: pl.BlockSpec((TM, width), lambda i: (i, 0))
    n = ckv.shape[0]
    return pl.pallas_call(
        _mla_cache_kernel,
        out_shape=[jax.ShapeDtypeStruct((n, 1024), BF16), jax.ShapeDtypeStruct((n, 512), BF16)],
        grid=(n // TM,),
        in_specs=[row(128), row(128), const((128, 1024)), const((128, 512)), const((1, 1024)),
                  const((1024, 128)), const((128, 1024))],
        out_specs=[row(1024), row(512)],
        compiler_params=_params(1),
        name="mla_cache",
    )(ckv, kr_pad, w["wk"], w["wv"], w["kn_g"], *_head_segments())


def _mla_attn_kernel(*refs, has_cache):
    q_ref, k_ref, v_ref = refs[:3]
    pos = 3
    if has_cache:
        kc_ref, vc_ref = refs[3:5]
        pos = 5
    o_ref = refs[pos]
    for p in range(MLA_HEADS // 2):
        acc = jnp.zeros((TM, 128), F32)
        for hh in range(2):
            h = 2 * p + hh
            lanes = slice(128 * h, 128 * h + 128)
            qh = q_ref[:, lanes]
            l1 = _dot_t(qh, k_ref[:, lanes])
            m = jnp.max(l1, axis=-1, keepdims=True)
            if has_cache:
                l0 = _dot_t(qh, kc_ref[:, lanes])
                m = jnp.maximum(m, jnp.max(l0, axis=-1, keepdims=True))
                p0 = jnp.exp(l0 - m)
            p1 = jnp.exp(l1 - m)
            den = jnp.sum(p1, axis=-1, keepdims=True)
            if has_cache:
                den = den + jnp.sum(p0, axis=-1, keepdims=True)
            mask = _lane_half_mask(hh)
            vp = v_ref[:, 128 * p:128 * p + 128]
            o = _dot(p1.astype(BF16), jnp.where(mask, vp, jnp.zeros_like(vp)))
            if has_cache:
                vcp = vc_ref[:, 128 * p:128 * p + 128]
                o += _dot(p0.astype(BF16), jnp.where(mask, vcp, jnp.zeros_like(vcp)))
            acc += o * (1.0 / den)
        o_ref[:, 128 * p:128 * p + 128] = acc.astype(BF16)


def _mla_attn(q, k, v, kc, vc, *, ctx):
    if ctx:
        nb, seq, row_blk, tile0 = N_CTX_SEQ, CTX_LEN, 0, 0
    else:
        nb, seq, row_blk, tile0 = N_LAT_SEQ, LAT_LEN, T_CTX // LAT_LEN, N_CTX_TILES
    nq = seq // TM
    in_specs = [pl.BlockSpec((TM, 1024), lambda b, i: (tile0 + b * nq + i, 0)),
                pl.BlockSpec((seq, 1024), lambda b, i: (row_blk + b, 0)),
                pl.BlockSpec((seq, 512), lambda b, i: (row_blk + b, 0))]
    args = [q, k, v]
    if not ctx:
        in_specs += [pl.BlockSpec((TM, 1024), lambda b, i: (b, 0)), pl.BlockSpec((TM, 512), lambda b, i: (b, 0))]
        args += [kc, vc]
    return pl.pallas_call(
        functools.partial(_mla_attn_kernel, has_cache=not ctx),
        out_shape=jax.ShapeDtypeStruct((nb * seq, 512), BF16),
        grid=(nb, nq), in_specs=in_specs,
        out_specs=pl.BlockSpec((TM, 512), lambda b, i: (b * nq + i, 0)),
        compiler_params=_params(2),
        name="mla_attn_ctx" if ctx else "mla_attn_lat",
    )(*args)


def _merge_kernel(retc_ref, retl_ref, glac_ref, glal_ref, mlac_ref, mlal_ref, m0_ref, m1_ref, m2_ref,
                  xc_ref, xl_ref, wb_ref, wo_ref,
                  g1_ref, n2_ref, sh2_ref, sc2_ref, rwh_ref, rwl_ref, rb_ref,
                  x1_ref, h_ref, idx_ref, w_ref, rank_ref, cnt_ref):
    @pl.when(pl.program_id(0) == 0)
    def _():
        cnt_ref[...] = jnp.zeros_like(cnt_ref)

    is_ctx = pl.program_id(0) < T_CTX // MERGE_TM
    mix = None
    for c_ref, l_ref, m_ref, n in ((retc_ref, retl_ref, m0_ref, 0), (glac_ref, glal_ref, m1_ref, 1),
                                   (mlac_ref, mlal_ref, m2_ref, 2)):
        branch = jnp.where(is_ctx, c_ref[...], l_ref[...])
        term = _sigmoid(m_ref[...]).astype(F32) * _dot(branch, wb_ref[n])
        mix = term if mix is None else mix + term
    out = _dot(mix.astype(BF16), wo_ref[...])
    x1 = jnp.where(is_ctx, xc_ref[...], xl_ref[...]) + g1_ref[0] * out
    x1_ref[...] = x1
    h = x1 * lax.rsqrt(jnp.mean(x1 * x1, axis=-1, keepdims=True) + EPS) * n2_ref[...]
    h = h * (1.0 + sc2_ref[0]) + sh2_ref[0]
    h_ref[...] = h
    hh = h.astype(BF16)
    hl = (h - hh.astype(F32)).astype(BF16)
    logits = _dot(hh, rwh_ref[...]) + _dot(hh, rwl_ref[...]) + _dot(hl, rwh_ref[...]) + rb_ref[...]
    rows = MERGE_TM // ROUTE_CHUNKS
    lane = lax.broadcasted_iota(jnp.int32, (rows, 128), 1)
    lanef = lane.astype(F32)
    onehots, osums = [], []
    for c in range(ROUTE_CHUNKS):
        l = jnp.where(lane < N_EXPERTS, logits[c * rows:(c + 1) * rows], -jnp.inf)
        vals, idxs = [], []
        for _ in range(TOP_K):
            m = jnp.max(l, axis=-1, keepdims=True)
            ix = jnp.min(jnp.where(l == m, lanef, 128.0), axis=-1, keepdims=True)
            vals.append(m)
            idxs.append(ix)
            l = jnp.where(lanef == ix, -jnp.inf, l)
        es = [jnp.exp(v - vals[0]) for v in vals]
        inv = 1.0 / (es[0] + es[1] + es[2] + es[3])
        idx_out = jnp.zeros((rows, 128), F32)
        w_out = jnp.zeros((rows, 128), F32)
        for kk in range(TOP_K):
            idx_out = jnp.where(lane == kk, idxs[kk], idx_out)
            w_out = jnp.where(lane == kk, es[kk] * inv, w_out)
        idx_ref[c * rows:(c + 1) * rows, :] = idx_out.astype(jnp.int32)
        w_ref[c * rows:(c + 1) * rows, :] = w_out
        oh = [jnp.where(lanef == ix, 1.0, 0.0) for ix in idxs]
        onehots.append(oh)
        osums.append((oh[0] + oh[1]) + (oh[2] + oh[3]))

    osum = jnp.concatenate(osums, axis=0)
    ri = lax.broadcasted_iota(jnp.int32, (MERGE_TM, MERGE_TM), 0)
    ci = lax.broadcasted_iota(jnp.int32, (MERGE_TM, MERGE_TM), 1)
    before = jnp.where(ri > ci, 1.0, 0.0).astype(BF16)
    prior = _dot(before, osum.astype(BF16)) + cnt_ref[0:1, :]
    for c in range(ROUTE_CHUNKS):
        pc = prior[c * rows:(c + 1) * rows]
        rank_out = jnp.zeros((rows, 128), F32)
        for kk in range(TOP_K):
            rank_out = jnp.where(lane == kk, jnp.sum(onehots[c][kk] * pc, axis=-1, keepdims=True), rank_out)
        rank_ref[c * rows:(c + 1) * rows, :] = rank_out.astype(jnp.int32)
    cnt_ref[...] = cnt_ref[...] + jnp.sum(osum, axis=0, keepdims=True)


def _merge(branches, z, xc, xl, modr, w, layer):
    mrow = _mod_row(MERGE_TM)
    base = layer * N_MOD

    def mod_spec(part):
        return pl.BlockSpec((1, 1, D_MODEL), lambda i: ((base + mrow(i)) * 6 + part, 0, 0))

    const = lambda shape: pl.BlockSpec(shape, lambda i: (0,) * len(shape))
    row = lambda width: pl.BlockSpec((MERGE_TM, width), lambda i: (i, 0))
    gate = lambda col: pl.BlockSpec((MERGE_TM, 1024), lambda i: (i, col))
    return pl.pallas_call(
        _merge_kernel,
        out_shape=[jax.ShapeDtypeStruct((T_ALL, D_MODEL), F32), jax.ShapeDtypeStruct((T_ALL, D_MODEL), F32),
                   jax.ShapeDtypeStruct((T_ALL, 128), jnp.int32), jax.ShapeDtypeStruct((T_ALL, 128), F32),
                   jax.ShapeDtypeStruct((T_ALL, 128), jnp.int32), jax.ShapeDtypeStruct((8, 128), F32)],
        grid=(T_ALL // MERGE_TM,),
        in_specs=[_ctx_spec(MERGE_TM, 512), _lat_spec(MERGE_TM, 512)] * 3 + [gate(3), gate(4), gate(5),
                  _ctx_spec(MERGE_TM, D_MODEL), _lat_spec(MERGE_TM, D_MODEL),
                  const((3, 512, 1024)), const((1024, 1024)),
                  mod_spec(2), const((1, 1024)), mod_spec(3), mod_spec(4),
                  const((1024, 128)), const((1024, 128)), const((1, 128))],
        out_specs=[row(1024), row(1024), row(128), row(128), row(128), const((8, 128))],
        compiler_params=_params(1),
        name="merge",
    )(*branches, z, z, z, xc, xl, w["wb"], w["wo"], modr, w["n2_g"], modr, modr,
      w["rw_hi"], w["rw_lo"], w["rb"])


def _route(top_idx, rank, counts):
    flat_e = top_idx.reshape(N_SLOTS)
    onehot = (flat_e[:, None] == jnp.arange(N_EXPERTS, dtype=jnp.int32)[None, :]).astype(jnp.int32)
    padded = (counts + MOE_ROWS - 1) // MOE_ROWS * MOE_ROWS
    pad_end = jnp.cumsum(padded)
    pad_start = pad_end - padded
    dest = (rank.reshape(N_SLOTS) + jnp.sum(onehot * pad_start[None, :], axis=1)).astype(jnp.int32)
    blk_start = jnp.arange(N_MOE_BLOCKS, dtype=jnp.int32) * MOE_ROWS
    block_e = jnp.minimum(jnp.sum((pad_end[None, :] <= blk_start[:, None]).astype(jnp.int32), axis=1),
                          N_EXPERTS - 1).astype(jnp.int32)
    n_used = (pad_end[-1] // MOE_ROWS).astype(jnp.int32)
    e_hot = (block_e[:, None] == jnp.arange(N_EXPERTS, dtype=jnp.int32)[None, :]).astype(jnp.int32)
    nxt_blk = jnp.sum(e_hot * pad_end[None, :], axis=1) // MOE_ROWS
    b_hot = (nxt_blk[:, None] == jnp.arange(N_MOE_BLOCKS, dtype=jnp.int32)[None, :]).astype(jnp.int32)
    next_e = jnp.where(nxt_blk < n_used, jnp.sum(b_hot * block_e[None, :], axis=1), block_e).astype(jnp.int32)
    tail_start = (pad_start + counts).astype(jnp.int32)
    return dest, tail_start, block_e, n_used.reshape(1), next_e


def _rows_to_tiles(x):
    r = x.shape[0]
    blocks = jnp.stack([x[:, 128 * g:128 * (g + 1)].reshape(r // 8, 8, 128) for g in range(D_MODEL // 128)], axis=1)
    return jnp.swapaxes(blocks, 1, 2).reshape(r, D_MODEL // 128, 128)


def _tiles_to_rows(v):
    r = v.shape[0]
    blocks = jnp.swapaxes(v.reshape(r // 8, 8, D_MODEL // 128, 128), 1, 2)
    return jnp.concatenate([blocks[:, g].reshape(r, 128) for g in range(D_MODEL // 128)], axis=1)


def _dispatch_kernel(dest_ref, tail_ref, nb_ref, h_ref, xs_ref, zero_buf, stage, sem, ssem):
    i = pl.program_id(0)

    @pl.when(i == 0)
    def _():
        zero_buf[...] = jnp.zeros_like(zero_buf)
        fills = [pltpu.make_async_copy(zero_buf, xs_ref.at[pl.ds(tail_ref[e], MOE_ROWS)], sem)
                 for e in range(N_EXPERTS)]
        for f in fills:
            f.start()
        for f in fills:
            f.wait()

        def fill_block(b, c):
            f = pltpu.make_async_copy(zero_buf, xs_ref.at[pl.ds(b * MOE_ROWS, MOE_ROWS)], sem)
            f.start()
            f.wait()
            return c
        lax.fori_loop(nb_ref[0], (N_MOE_ROWS + XS_EXTRA) // MOE_ROWS, fill_block, 0)

    base = i * TM * TOP_K
    cur = i % 2
    row0 = pl.multiple_of(cur * TM, TM)

    def wait_tile(slot):
        for kk in range(TOP_K):
            pltpu.make_async_copy(stage.at[pl.ds(0, TM)], xs_ref.at[pl.ds(0, TM)], ssem.at[slot]).wait()

    @pl.when(i >= 2)
    def _():
        wait_tile(cur)
    stage[pl.ds(row0, TM)] = _rows_to_tiles(h_ref[...])

    def issue(t, c):
        for kk in range(TOP_K):
            pltpu.make_async_copy(stage.at[row0 + t], xs_ref.at[dest_ref[base + t * TOP_K + kk]], ssem.at[cur]
                                  ).start(priority=kk % 2)
        return c
    lax.fori_loop(0, TM, issue, 0, unroll=8)

    @pl.when(i == N_TILES - 1)
    def _():
        wait_tile(1 - cur)
        wait_tile(cur)


def _moe_dispatch(h, dest, tail_start, n_used):
    return pl.pallas_call(
        _dispatch_kernel,
        out_shape=jax.ShapeDtypeStruct((N_MOE_ROWS + XS_EXTRA,) + ROW_TILE, F32),
        grid_spec=pltpu.PrefetchScalarGridSpec(
            num_scalar_prefetch=3, grid=(N_TILES,),
            in_specs=[pl.BlockSpec((TM, D_MODEL), lambda i, d, t, nb: (i, 0))],
            out_specs=pl.BlockSpec(memory_space=pl.ANY),
            scratch_shapes=[pltpu.VMEM((MOE_ROWS,) + ROW_TILE, F32), pltpu.VMEM((2 * TM,) + ROW_TILE, F32),
                            pltpu.SemaphoreType.DMA, pltpu.SemaphoreType.DMA((2,))]),
        compiler_params=_params(1),
        name="moe_dispatch",
    )(dest, tail_start, n_used, h)


def _expert_kernel(be_ref, nb_ref, nxt_ref, x_ref, wgu_hbm, bgu_ref, wd_hbm, bd_ref, o_ref,
                   wgu_st, wd_st, wgu_bf, wd_bf, sem, *, layer):
    i = pl.program_id(0)
    e = be_ref[i]
    prev = be_ref[jnp.maximum(i - 1, 0)]

    def weight_copies(expert):
        idx = layer * N_EXPERTS + expert
        return (pltpu.make_async_copy(wgu_hbm.at[idx], wgu_st, sem.at[0]),
                pltpu.make_async_copy(wd_hbm.at[idx], wd_st, sem.at[1]))

    @pl.when(i == 0)
    def _():
        for cp in weight_copies(e):
            cp.start()

    @pl.when(((i == 0) | (e != prev)) & (i < nb_ref[0]))
    def _():
        for cp in weight_copies(e):
            cp.wait()
        wgu_bf[...] = wgu_st[...].astype(BF16)
        wd_bf[...] = wd_st[...].astype(BF16)
        nxt = nxt_ref[i]

        @pl.when(nxt != e)
        def _():
            for cp in weight_copies(nxt):
                cp.start()

    @pl.when(i < nb_ref[0])
    def _():
        gu = _dot(_tiles_to_rows(x_ref[...]).astype(BF16), wgu_bf[...]) + bgu_ref[0]
        gate = jnp.minimum(gu[:, :D_EXPERT], SWIGLU_LIMIT)
        up = jnp.clip(gu[:, D_EXPERT:], -SWIGLU_LIMIT, SWIGLU_LIMIT)
        act = (up + 1.0) * gate * _sigmoid(SWIGLU_ALPHA * gate)
        o_ref[...] = _rows_to_tiles(_dot(act.astype(BF16), wd_bf[...]) + bd_ref[0])

    @pl.when(i >= nb_ref[0])
    def _():
        o_ref[...] = jnp.zeros_like(o_ref)


def _moe_experts(xs, block_e, n_used, next_e, w_gu, b_gu, w_down, b_down, layer):
    w_idx = lambda i, be, nb, nx: (layer * N_EXPERTS + be[i], 0, 0)
    return pl.pallas_call(
        functools.partial(_expert_kernel, layer=layer),
        out_shape=jax.ShapeDtypeStruct((N_MOE_ROWS,) + ROW_TILE, F32),
        grid_spec=pltpu.PrefetchScalarGridSpec(
            num_scalar_prefetch=3, grid=(N_MOE_BLOCKS,),
            in_specs=[pl.BlockSpec((MOE_ROWS,) + ROW_TILE, lambda i, be, nb, nx: (jnp.minimum(i, nb[0] - 1), 0, 0)),
                      pl.BlockSpec(memory_space=pl.ANY),
                      pl.BlockSpec((1, 1, 2 * D_EXPERT), w_idx),
                      pl.BlockSpec(memory_space=pl.ANY),
                      pl.BlockSpec((1, 1, D_MODEL), w_idx)],
            out_specs=pl.BlockSpec((MOE_ROWS,) + ROW_TILE, lambda i, be, nb, nx: (i, 0, 0)),
            scratch_shapes=[pltpu.VMEM((D_MODEL, 2 * D_EXPERT), F32), pltpu.VMEM((D_EXPERT, D_MODEL), F32),
                            pltpu.VMEM((D_MODEL, 2 * D_EXPERT), BF16), pltpu.VMEM((D_EXPERT, D_MODEL), BF16),
                            pltpu.SemaphoreType.DMA((2,))]),
        compiler_params=_params(1),
        name="moe_experts",
    )(block_e, n_used, next_e, xs, w_gu, b_gu, w_down, b_down)


def _combine_kernel(dest_ref, x_ref, g2_ref, w_ref, eo_ref, yc_ref, yl_ref, buf, sem):
    i = pl.program_id(0)
    cur = i % 2

    def issue_tile(tile, slot):
        base = tile * TM * TOP_K
        row0 = slot * (TOP_K * TM)

        def issue(t, c):
            for kk in range(TOP_K):
                pltpu.make_async_copy(eo_ref.at[dest_ref[base + t * TOP_K + kk]], buf.at[row0 + kk * TM + t],
                                      sem.at[slot]).start(priority=kk % 2)
            return c
        lax.fori_loop(0, TM, issue, 0, unroll=8)

    @pl.when(i == 0)
    def _():
        issue_tile(0, 0)

    @pl.when(i + 1 < N_TILES)
    def _():
        issue_tile(i + 1, 1 - cur)

    for kk in range(TOP_K):
        pltpu.make_async_copy(eo_ref.at[pl.ds(0, TM)], buf.at[pl.ds(0, TM)], sem.at[cur]).wait()
    w = w_ref[...]
    ff = None
    for kk in range(TOP_K):
        rows = _tiles_to_rows(buf[pl.ds(pl.multiple_of(cur * (TOP_K * TM) + kk * TM, TM), TM)])
        term = rows * w[:, kk:kk + 1]
        ff = term if ff is None else ff + term
    y = x_ref[...] + g2_ref[0] * ff
    is_ctx = i < N_CTX_TILES

    @pl.when(is_ctx)
    def _():
        yc_ref[...] = y

    @pl.when(jnp.logical_not(is_ctx))
    def _():
        yl_ref[...] = y


def _moe_combine(dest, x1, modr, top_w, eo, layer):
    mrow = _mod_row(TM)
    base = layer * N_MOD
    return pl.pallas_call(
        _combine_kernel,
        out_shape=[jax.ShapeDtypeStruct((T_CTX, D_MODEL), F32), jax.ShapeDtypeStruct((T_LAT, D_MODEL), F32)],
        grid_spec=pltpu.PrefetchScalarGridSpec(
            num_scalar_prefetch=1, grid=(N_TILES,),
            in_specs=[pl.BlockSpec((TM, D_MODEL), lambda i, d: (i, 0)),
                      pl.BlockSpec((1, 1, D_MODEL), lambda i, d: ((base + mrow(i)) * 6 + 5, 0, 0)),
                      pl.BlockSpec((TM, 128), lambda i, d: (i, 0)),
                      pl.BlockSpec(memory_space=pl.ANY)],
            out_specs=[pl.BlockSpec((TM, D_MODEL), lambda i, d: (jnp.minimum(i, N_CTX_TILES - 1), 0)),
                       pl.BlockSpec((TM, D_MODEL), lambda i, d: (jnp.maximum(i - N_CTX_TILES, 0), 0))],
            scratch_shapes=[pltpu.VMEM((2 * TOP_K * TM,) + ROW_TILE, F32), pltpu.SemaphoreType.DMA((2,))]),
        compiler_params=_params(1),
        name="moe_combine",
    )(dest, x1, modr, top_w, eo)


def _pad_heads(w, n_heads, width):
    lead = w.shape[:-1]
    w = w.reshape(lead + (n_heads, width))
    w = jnp.pad(w, [(0, 0)] * len(lead) + [(0, 0), (0, HEAD_PAD - width)])
    return w.reshape(lead + (n_heads * HEAD_PAD,))


def _permute_w_in_kernel(w_ref, o_ref):
    w = w_ref[0]
    o_ref[0] = jnp.concatenate([w[:, :3072], w[:, 3520:6592], w[:, 3104:3520], w[:, 3072:3104], w[:, 6592:]],
                               axis=1)


def _permute_w_in(w_in):
    rows = 128
    w_pad = jnp.pad(w_in, ((0, 0), (0, 0), (0, DZ - w_in.shape[2]))).astype(BF16)
    return pl.pallas_call(
        _permute_w_in_kernel,
        out_shape=jax.ShapeDtypeStruct((DEPTH, D_MODEL, DZ), BF16),
        grid=(DEPTH, D_MODEL // rows),
        in_specs=[pl.BlockSpec((1, rows, DZ), lambda l, r: (l, r, 0))],
        out_specs=pl.BlockSpec((1, rows, DZ), lambda l, r: (l, r, 0)),
        compiler_params=_params(2),
        name="permute_w_in",
    )(w_pad)


def _layer_weights(l, gla_wa2, gla_ba, mla_qa_g, mla_wuq, mla_kva_g, mla_wukv, mla_qn_g, mla_kn_g,
                   w_branch, w_out, router_w, router_b, norm2_g):
    wa_p = jnp.zeros((512, 512), F32)
    wa_p = wa_p.at[416:432, 0:256].set(gla_wa2[l, 0]).at[432:448, 256:512].set(gla_wa2[l, 1]).astype(BF16)
    ba_p = gla_ba[l].reshape(1, 512)
    wukv = mla_wukv[l].reshape(128, MLA_HEADS, MLA_NOPE + MLA_V)
    rw = jnp.pad(router_w[l], ((0, 0), (0, 128 - N_EXPERTS)))
    rw_hi = rw.astype(BF16)
    return {
        "wa": wa_p, "ba": ba_p,
        "qa_g": mla_qa_g[l].reshape(1, 256),
        "wuq": _pad_heads(mla_wuq[l], MLA_HEADS, MLA_QK).astype(BF16),
        "qn_g": jnp.tile(jnp.pad(mla_qn_g[l], (0, HEAD_PAD - MLA_QK)), MLA_HEADS).reshape(1, 1024) * MLA_QK ** -0.5,
        "kn_g": jnp.tile(jnp.pad(mla_kn_g[l], (0, HEAD_PAD - MLA_QK)), MLA_HEADS).reshape(1, 1024),
        "kva_g": mla_kva_g[l].reshape(1, 128),
        "wk": _pad_heads(wukv[:, :, :MLA_NOPE].reshape(128, MLA_HEADS * MLA_NOPE), MLA_HEADS, MLA_NOPE).astype(BF16),
        "wv": wukv[:, :, MLA_NOPE:].reshape(128, MLA_HEADS * MLA_V).astype(BF16),
        "wb": w_branch[l].astype(BF16), "wo": w_out[l].astype(BF16),
        "rw_hi": rw_hi, "rw_lo": (rw - rw_hi.astype(F32)).astype(BF16),
        "rb": jnp.pad(router_b[l], (0, 128 - N_EXPERTS)).reshape(1, 128),
        "n2_g": norm2_g[l].reshape(1, D_MODEL),
    }


def kernel(x_prompt, x_sample, cache_mla_ckv, cache_mla_krope, state_ret, state_gla, c, c_ctx, w_mod, b_mod, norm1_g, norm2_g, w_in, ret_gn_g, gla_wa2, gla_ba, gla_norm_g, mla_qa_g, mla_wuq, mla_kva_g, mla_wukv, mla_qn_g, mla_kn_g, w_branch, w_out, router_w, router_b, moe_w_gu, moe_b_gu, moe_w_down, moe_b_down):
    xc, xl = x_prompt.reshape(T_CTX, D_MODEL), x_sample.reshape(T_LAT, D_MODEL)
    cc = jnp.concatenate([c_ctx[None, :], c, jnp.zeros((N_MOD - 1 - N_LAT_SEQ, D_MODEL), F32)], axis=0)
    modr = _modulation(cc, w_mod, b_mod).reshape(DEPTH * N_MOD * 6, 1, D_MODEL)
    rope_tabs = _rope_tables()
    w_in_p = _permute_w_in(w_in)
    w_gu = moe_w_gu.reshape(DEPTH * N_EXPERTS, D_MODEL, 2 * D_EXPERT)
    b_gu = moe_b_gu.reshape(DEPTH * N_EXPERTS, 1, 2 * D_EXPERT)
    w_dn = moe_w_down.reshape(DEPTH * N_EXPERTS, D_EXPERT, D_MODEL)
    b_dn = moe_b_down.reshape(DEPTH * N_EXPERTS, 1, D_MODEL)

    ckv_l, krope_l, ret_l, gla_l = [], [], [], []
    for l in range(DEPTH):
        w = _layer_weights(l, gla_wa2, gla_ba, mla_qa_g, mla_wuq, mla_kva_g, mla_wukv, mla_qn_g, mla_kn_g,
                           w_branch, w_out, router_w, router_b, norm2_g)
        z = _in_proj(xc, xl, norm1_g[l].reshape(1, D_MODEL), modr, w_in_p, l)

        gn = ret_gn_g[l].reshape(1, 512)
        ret_c, ret_state = _retention(z, gn, None, ctx=True)
        (ret_s,) = _retention(z, gn, state_ret[:, l], ctx=False)
        gng = gla_norm_g[l].reshape(1, 512)
        gla_c, gla_state = _gla(z, w["wa"], w["ba"], gng, None, ctx=True)
        gla_s, _ = _gla(z, w["wa"], w["ba"], gng, state_gla[:, l], ctx=False)

        q, k, v, ckv, kr = _mla_prep(z, w, rope_tabs)
        kc, vc = _mla_cache(cache_mla_ckv[:, l].reshape(N_LAT_SEQ * CTX_LEN, 128),
                            jnp.pad(cache_mla_krope[:, l].reshape(N_LAT_SEQ * CTX_LEN, MLA_ROPE),
                                    ((0, 0), (0, 128 - MLA_ROPE))), w)
        mla_c = _mla_attn(q, k, v, None, None, ctx=True)
        mla_s = _mla_attn(q, k, v, kc, vc, ctx=False)

        x1, h2, top_idx, top_w, rank, cnt = _merge((ret_c, ret_s, gla_c, gla_s, mla_c, mla_s), z, xc, xl, modr, w, l)
        dest, tail_start, block_e, n_used, next_e = _route(top_idx[:, :TOP_K], rank[:, :TOP_K],
                                                           cnt[0, :N_EXPERTS].astype(jnp.int32))
        xs = _moe_dispatch(h2, dest, tail_start, n_used)
        eo = _moe_experts(xs, block_e, n_used, next_e, w_gu, b_gu, w_dn, b_dn, l)
        xc, xl = _moe_combine(dest, x1, modr, top_w, eo, l)

        ckv_l.append(ckv[:T_CTX].reshape(N_CTX_SEQ, CTX_LEN, 128))
        krope_l.append(kr[:T_CTX, :MLA_ROPE].reshape(N_CTX_SEQ, CTX_LEN, MLA_ROPE))
        ret_l.append(ret_state)
        gla_l.append(gla_state.reshape(N_CTX_SEQ, 2, N_HEADS, DK, DV))

    y_p = xc.reshape(N_CTX_SEQ, CTX_LEN, D_MODEL)
    y_s = xl.reshape(N_LAT_SEQ, LAT_LEN, D_MODEL)
    return (y_p, y_s, jnp.stack(ckv_l, axis=1), jnp.stack(krope_l, axis=1),
            jnp.stack(ret_l, axis=1), jnp.stack(gla_l, axis=1))
```

```python
import functools

import jax
import jax.numpy as jnp
import numpy as np
from jax import lax
from jax.experimental import pallas as pl
from jax.experimental.pallas import tpu as pltpu

F32 = jnp.float32
BF16 = jnp.bfloat16

D_MODEL = 1024
DEPTH = 2
N_CTX_SEQ, CTX_LEN = 32, 256
N_LAT_SEQ, LAT_LEN = 4, 1024
T_CTX = N_CTX_SEQ * CTX_LEN
T_LAT = N_LAT_SEQ * LAT_LEN
T_ALL = T_CTX + T_LAT
TM = 256
N_TILES = T_ALL // TM
N_CTX_TILES = T_CTX // TM
LAT_TILES = LAT_LEN // TM
N_MOD = 8
EPS = 1e-6

N_HEADS = 4
DK, DV = 64, 128
GRID_W = 64
MLA_HEADS, MLA_NOPE, MLA_ROPE, MLA_V = 8, 64, 32, 64
MLA_QK = MLA_NOPE + MLA_ROPE
HEAD_PAD = 128
GLA_TAU = 16.0
N_EXPERTS, TOP_K, D_EXPERT = 32, 4, 1024
SWIGLU_LIMIT, SWIGLU_ALPHA = 7.0, 1.702
MOE_ROWS = 512
N_SLOTS = T_ALL * TOP_K
N_MOE_BLOCKS = N_SLOTS // MOE_ROWS + N_EXPERTS
N_MOE_ROWS = N_MOE_BLOCKS * MOE_ROWS
MERGE_TM = 512
ROUTE_CHUNKS = 8
XS_EXTRA = MOE_ROWS
ROW_TILE = (D_MODEL // 128, 128)

DZ = 6656
IN_TILE = 512
VMEM_LIMIT = 56 * 1024 * 1024

RET_LOG_F = [float(np.log1p(-np.exp2(-(5.0 + h)))) for h in range(N_HEADS)]
RET_LOG_B = [float(np.log1p(-np.exp2(-(5.5 + h)))) for h in range(N_HEADS)]


def _params(n_axes, vmem=VMEM_LIMIT):
    return pltpu.CompilerParams(dimension_semantics=("arbitrary",) * n_axes, vmem_limit_bytes=vmem)


def _sigmoid(x):
    return 1.0 / (1.0 + jnp.exp(-x))


def _dot(a, b):
    return jnp.dot(a, b, preferred_element_type=F32)


def _dot_t(a, b):
    return lax.dot_general(a, b, (((1,), (1,)), ((), ())), preferred_element_type=F32)


def _mod_row(tile_rows):
    def f(i):
        r0 = i * tile_rows
        return jnp.where(r0 < T_CTX, 0, 1 + (r0 - T_CTX) // LAT_LEN)
    return f


def _ctx_spec(rows, width):
    n_ctx = T_CTX // rows
    return pl.BlockSpec((rows, width), lambda i: (jnp.minimum(i, n_ctx - 1), 0))


def _lat_spec(rows, width):
    n_ctx = T_CTX // rows
    return pl.BlockSpec((rows, width), lambda i: (jnp.maximum(i - n_ctx, 0), 0))


def _mod_kernel(c_ref, w_ref, b_ref, o_ref):
    c = c_ref[...]
    s = c * _sigmoid(c)
    sh = s.astype(BF16)
    sl = (s - sh.astype(F32)).astype(BF16)
    w = w_ref[0]
    wh = w.astype(BF16)
    wl = (w - wh.astype(F32)).astype(BF16)
    o_ref[0] = _dot(sh, wh) + _dot(sh, wl) + _dot(sl, wh) + b_ref[0]


def _modulation(cc, w_mod, b_mod):
    n = 6 * D_MODEL
    blk = 2048
    return pl.pallas_call(
        _mod_kernel,
        out_shape=jax.ShapeDtypeStruct((DEPTH, N_MOD, n), F32),
        grid=(DEPTH, n // blk),
        in_specs=[pl.BlockSpec((N_MOD, D_MODEL), lambda l, j: (0, 0)),
                  pl.BlockSpec((1, D_MODEL, blk), lambda l, j: (l, 0, j)),
                  pl.BlockSpec((1, 1, blk), lambda l, j: (l, 0, j))],
        out_specs=pl.BlockSpec((1, N_MOD, blk), lambda l, j: (l, 0, j)),
        compiler_params=_params(2),
        name="modulation",
    )(cc, w_mod, b_mod.reshape(DEPTH, 1, n))


def _in_kernel(xc_ref, xl_ref, g_ref, sh_ref, sc_ref, w_ref, o_ref):
    x = jnp.where(pl.program_id(0) < T_CTX // IN_TILE, xc_ref[...], xl_ref[...])
    h = x * lax.rsqrt(jnp.mean(x * x, axis=-1, keepdims=True) + EPS) * g_ref[...]
    h = h * (1.0 + sc_ref[0]) + sh_ref[0]
    hb = h.astype(BF16)
    for n0 in range(0, DZ, 512):
        o_ref[:, n0:n0 + 512] = _dot(hb, w_ref[0, :, n0:n0 + 512]).astype(BF16)


def _in_proj(xc, xl, g, modr, w_in_p, layer):
    mrow = _mod_row(IN_TILE)
    base = layer * N_MOD

    def mod_spec(part):
        return pl.BlockSpec((1, 1, D_MODEL), lambda i: ((base + mrow(i)) * 6 + part, 0, 0))

    return pl.pallas_call(
        _in_kernel,
        out_shape=jax.ShapeDtypeStruct((T_ALL, DZ), BF16),
        grid=(T_ALL // IN_TILE,),
        in_specs=[_ctx_spec(IN_TILE, D_MODEL), _lat_spec(IN_TILE, D_MODEL),
                  pl.BlockSpec((1, D_MODEL), lambda i: (0, 0)),
                  mod_spec(0), mod_spec(1),
                  pl.BlockSpec((1, D_MODEL, DZ), lambda i: (layer, 0, 0))],
        out_specs=pl.BlockSpec((IN_TILE, DZ), lambda i: (i, 0)),
        compiler_params=_params(1),
        name="in_proj",
    )(xc, xl, g, modr, modr, w_in_p)


def _lane_half_mask(hh):
    lane = lax.broadcasted_iota(jnp.int32, (1, 128), 1)
    return (lane < 64) if hh == 0 else (lane >= 64)


@functools.lru_cache(maxsize=None)
def _ret_decay_table(seq):
    d = np.arange(seq)[:, None] - np.arange(seq)[None, :]
    tab = np.stack([np.exp(np.where(d > 0, RET_LOG_F[h] * d, -RET_LOG_B[h] * d)) for h in range(N_HEADS)])
    return (tab * np.where(d == 0, 2.0, 1.0) * DK ** -0.5).astype(np.float32)


def _ret_kernel(*refs, seq, has_state, emit_state):
    q_ref, k_ref, v_ref, g_ref, gn_ref, dec_ref = refs[:6]
    pos = 6
    if has_state:
        s0_ref = refs[pos]
        pos += 1
    o_ref = refs[pos]
    pos += 1
    if emit_state:
        st_ref = refs[pos]

    r0 = pl.multiple_of(pl.program_id(1) * TM, TM)
    qb = q_ref[pl.ds(r0, TM), :]
    ri = (lax.broadcasted_iota(jnp.int32, (TM, 1), 0) + r0).astype(F32)

    for h in range(N_HEADS):
        p, hh = h // 2, h % 2
        lanes = slice(128 * p, 128 * p + 128)
        qp = qb[:, lanes]
        qh = jnp.where(_lane_half_mask(hh), qp, jnp.zeros_like(qp))
        sc = _dot_t(qh, k_ref[:, lanes])
        o = _dot((sc * dec_ref[h]).astype(BF16), v_ref[:, 128 * h:128 * h + 128])
        if has_state:
            qf = qh.astype(F32)
            o += _dot((qf * jnp.exp(RET_LOG_F[h] * (ri + 1.0))).astype(BF16), s0_ref[0, 0, p].astype(BF16))
            o += _dot((qf * jnp.exp(RET_LOG_B[h] * (seq - ri))).astype(BF16), s0_ref[0, 1, p].astype(BF16))
        mu = jnp.mean(o, axis=-1, keepdims=True)
        d = o - mu
        var = jnp.mean(d * d, axis=-1, keepdims=True)
        on = d * lax.rsqrt(var + EPS)
        g = g_ref[:, 128 * h:128 * h + 128].astype(F32)
        out = on * gn_ref[:, 128 * h:128 * h + 128] * (g * _sigmoid(g))
        o_ref[:, 128 * h:128 * h + 128] = out.astype(BF16)

    if emit_state:
        jc = lax.broadcasted_iota(jnp.int32, (seq, 1), 0).astype(F32)
        lane = lax.broadcasted_iota(jnp.int32, (1, 128), 1)
        for p in range(2):
            kp = k_ref[:, 128 * p:128 * p + 128].astype(F32) * DK ** -0.5
            lgf = jnp.where(lane < 64, RET_LOG_F[2 * p], RET_LOG_F[2 * p + 1])
            lgb = jnp.where(lane < 64, RET_LOG_B[2 * p], RET_LOG_B[2 * p + 1])
            kdf = (kp * jnp.exp(lgf * (seq - 1.0 - jc))).T.astype(BF16)
            kdb = (kp * jnp.exp(lgb * jc)).T.astype(BF16)
            for hh in range(2):
                h = 2 * p + hh
                vh = v_ref[:, 128 * h:128 * h + 128]
                st_ref[0, 0, h] = _dot(kdf, vh)[64 * hh:64 * hh + 64, :]
                st_ref[0, 1, h] = _dot(kdb, vh)[64 * hh:64 * hh + 64, :]


def _retention(z, gn_g, s0, *, ctx):
    if ctx:
        nb, seq, row_blk, tile0 = N_CTX_SEQ, CTX_LEN, 0, 0
    else:
        nb, seq, row_blk, tile0 = N_LAT_SEQ, LAT_LEN, T_CTX // LAT_LEN, N_CTX_TILES
    nq = seq // TM
    in_specs = [pl.BlockSpec((seq, 256), lambda b, i: (row_blk + b, 0)),
                pl.BlockSpec((seq, 256), lambda b, i: (row_blk + b, 1)),
                pl.BlockSpec((seq, 512), lambda b, i: (row_blk + b, 1)),
                pl.BlockSpec((TM, 512), lambda b, i: (tile0 + b * nq + i, 2)),
                pl.BlockSpec((1, 512), lambda b, i: (0, 0)),
                pl.BlockSpec((N_HEADS, TM, seq), lambda b, i: (0, i, 0))]
    args = [z, z, z, z, gn_g, _ret_decay_table(seq)]
    out_shape = [jax.ShapeDtypeStruct((nb * seq, 512), BF16)]
    out_specs = [pl.BlockSpec((TM, 512), lambda b, i: (b * nq + i, 0))]
    if not ctx:
        in_specs.append(pl.BlockSpec((1, 2, 2, 128, 128), lambda b, i: (b, 0, 0, 0, 0)))
        args.append(s0.reshape(N_LAT_SEQ, 2, 2, 128, 128))
    else:
        out_shape.append(jax.ShapeDtypeStruct((nb, 2, N_HEADS, DK, DV), F32))
        out_specs.append(pl.BlockSpec((1, 2, N_HEADS, DK, DV), lambda b, i: (b, 0, 0, 0, 0)))
    return pl.pallas_call(
        functools.partial(_ret_kernel, seq=seq, has_state=not ctx, emit_state=ctx),
        out_shape=out_shape, grid=(nb, nq), in_specs=in_specs, out_specs=out_specs,
        compiler_params=_params(2),
        name="retention_ctx" if ctx else "retention_lat",
    )(*args)


def _gla_decay(small_ref, wa_ref, ba_ref):
    x = _dot(small_ref[...], wa_ref[...]) + ba_ref[...]
    la = -(jnp.maximum(-x, 0.0) + jnp.log(1.0 + jnp.exp(-jnp.abs(x)))) * (1.0 / GLA_TAU)
    ri = lax.broadcasted_iota(jnp.int32, (TM, TM), 0)
    ci = lax.broadcasted_iota(jnp.int32, (TM, TM), 1)
    ltri = jnp.where(ri >= ci, 1.0, 0.0).astype(BF16)
    hi = la.astype(BF16)
    r1 = la - hi.astype(F32)
    mid = r1.astype(BF16)
    lo = (r1 - mid.astype(F32)).astype(BF16)
    cum = _dot(ltri, hi) + _dot(ltri, mid) + _dot(ltri, lo)
    return la, cum


def _gla_state_kernel(k_ref, v_ref, small_ref, wa_ref, ba_ref, kv_ref, tot_ref):
    la, cum = _gla_decay(small_ref, wa_ref, ba_ref)
    bf, bb = cum[:, :256], cum[:, 256:]
    xb = bb - la[:, 256:]
    k = k_ref[...].astype(F32)
    kdf = k * jnp.exp(bf[TM - 1:TM, :] - bf)
    kdb = k * jnp.exp(xb)
    for p in range(2):
        kf_t = kdf[:, 128 * p:128 * p + 128].T.astype(BF16)
        kb_t = kdb[:, 128 * p:128 * p + 128].T.astype(BF16)
        for hh in range(2):
            h = 2 * p + hh
            vh = v_ref[:, 128 * h:128 * h + 128]
            kv_ref[0, 0, 0, h] = _dot(kf_t, vh)[64 * hh:64 * hh + 64, :]
            kv_ref[0, 0, 1, h] = _dot(kb_t, vh)[64 * hh:64 * hh + 64, :]
    tot_ref[0, 0] = jnp.sum(la.T, axis=-1, keepdims=True)


def _mid_bcast(x, s, r):
    w = 2 * s
    if w >= 8:
        n = TM // w
        x3 = x.reshape(n, w, 256)
        return jnp.broadcast_to(x3[:, r:r + 1, :], (n, w, 256)).reshape(TM, 256)
    x3 = x.reshape(TM // 8, 8, 256)
    sub = lax.broadcasted_iota(jnp.int32, (1, 8, 1), 1)
    out = None
    for blk in range(8 // w):
        rowv = jnp.broadcast_to(x3[:, blk * w + r:blk * w + r + 1, :], (TM // 8, 8, 256))
        out = rowv if out is None else jnp.where(sub >= blk * w, rowv, out)
    return out.reshape(TM, 256)


def _gla_kernel(*refs, n_blk, has_state, emit_state):
    q_ref, k_ref, v_ref, g_ref, small_ref, wa_ref, ba_ref, gn_ref = refs[:8]
    pos = 8
    if has_state:
        kv_ref, tot_ref, s0_ref = refs[pos:pos + 3]
        pos += 3
    o_ref = refs[pos]
    pos += 1
    if emit_state:
        kvo_ref = refs[pos]

    la, cum = _gla_decay(small_ref, wa_ref, ba_ref)
    bf, bb = cum[:, :256], cum[:, 256:]
    xb = bb - la[:, 256:]
    q = q_ref[...].astype(F32) * DK ** -0.5
    k = k_ref[...].astype(F32)
    row = lax.broadcasted_iota(jnp.int32, (TM, 1), 0)
    rowi = lax.broadcasted_iota(jnp.int32, (TM, TM), 0)
    colj = lax.broadcasted_iota(jnp.int32, (TM, TM), 1)
    low_half = _lane_half_mask(0)

    def join(fwd, bwd):
        ops = []
        for p in range(N_HEADS // 2):
            f = fwd[:, 128 * p:128 * p + 128]
            br = pltpu.roll(bwd[:, 128 * p:128 * p + 128], 64, 1)
            ops.append(jnp.where(low_half, f, br).astype(BF16))
            ops.append(jnp.where(low_half, br, f).astype(BF16))
        return ops

    qd, kd = join(q, q), join(k, k)
    acc = [jnp.where(rowi == colj, _dot_t(qo, ko), 0.0) for qo, ko in zip(qd, kd)]

    low_f = jnp.where(low_half, 1.0, 0.0)
    s = 1
    while s < TM:
        up_f = jnp.where(((row // s) % 2) == 1, 1.0, 0.0)
        live_even = jnp.where(up_f == low_f, 1.0, 0.0).astype(BF16)
        live = [live_even, 1.0 - live_even]
        dead = [live[1], live[0]]
        af = -jnp.abs(bf - _mid_bcast(bf, s, s - 1))
        ab = -jnp.abs(xb - _mid_bcast(xb, s, s))
        same = (rowi // (2 * s)) == (colj // (2 * s))
        for p in range(N_HEADS // 2):
            a_f = af[:, 128 * p:128 * p + 128]
            a_b = pltpu.roll(ab[:, 128 * p:128 * p + 128], 64, 1)
            for hh in range(2):
                h = 2 * p + hh
                arg = jnp.where(low_half, a_f, a_b) if hh == 0 else jnp.where(low_half, a_b, a_f)
                e = jnp.exp(arg).astype(BF16)
                sl = _dot_t(e * (qd[h] * live[hh]), e * (kd[h] * dead[hh]))
                acc[h] = acc[h] + (jnp.where(same, sl, 0.0) if 2 * s < TM else sl)
        s *= 2

    if has_state:
        n = pl.program_id(1)
        q_state = join(q * jnp.exp(bf), q * jnp.exp(bb[TM - 1:TM, :] - xb))

    if emit_state:
        kdf = k * jnp.exp(bf[TM - 1:TM, :] - bf)
        kdb = k * jnp.exp(xb)
        for p in range(N_HEADS // 2):
            kf_t = kdf[:, 128 * p:128 * p + 128].T.astype(BF16)
            kb_t = kdb[:, 128 * p:128 * p + 128].T.astype(BF16)
            for hh in range(2):
                h = 2 * p + hh
                vh = v_ref[:, 128 * h:128 * h + 128]
                kvo_ref[0, 0, 0, h] = _dot(kf_t, vh)[64 * hh:64 * hh + 64, :]
                kvo_ref[0, 0, 1, h] = _dot(kb_t, vh)[64 * hh:64 * hh + 64, :]

    for h in range(N_HEADS):
        o = _dot(acc[h].astype(BF16), v_ref[:, 128 * h:128 * h + 128])
        if has_state:
            sf = s0_ref[0, 0, h]
            for m in range(n_blk - 1):
                dec = jnp.exp(tot_ref[0, m, 64 * h:64 * h + 64, :])
                sf = jnp.where(m < n, dec * sf + kv_ref[0, m, 0, h], sf)
            sb = s0_ref[0, 1, h]
            for m in range(n_blk - 1, 0, -1):
                dec = jnp.exp(tot_ref[0, m, 256 + 64 * h:256 + 64 * h + 64, :])
                sb = jnp.where(m > n, dec * sb + kv_ref[0, m, 1, h], sb)
            state = jnp.concatenate([sf, sb] if h % 2 == 0 else [sb, sf], axis=0).astype(BF16)
            o += _dot(q_state[h], state)
        on = o * lax.rsqrt(jnp.mean(o * o, axis=-1, keepdims=True) + EPS)
        g = g_ref[:, 128 * h:128 * h + 128].astype(F32)
        out = on * gn_ref[:, 128 * h:128 * h + 128] * (g * _sigmoid(g))
        o_ref[:, 128 * h:128 * h + 128] = out.astype(BF16)


def _gla(z, wa_p, ba_p, gn_g, s0, *, ctx):
    if ctx:
        nb, n_blk, tile0 = N_CTX_SEQ, 1, 0
    else:
        nb, n_blk, tile0 = N_LAT_SEQ, LAT_TILES, N_CTX_TILES

    def zspec(width, col):
        return pl.BlockSpec((TM, width), lambda b, n: (tile0 + b * n_blk + n, col))

    w_specs = [pl.BlockSpec((512, 512), lambda b, n: (0, 0)), pl.BlockSpec((1, 512), lambda b, n: (0, 0))]
    kv_shape = jax.ShapeDtypeStruct((nb, n_blk, 2, N_HEADS, DK, DV), F32)
    kv_spec = pl.BlockSpec((1, 1, 2, N_HEADS, DK, DV), lambda b, n: (b, n, 0, 0, 0, 0))
    in_specs = [zspec(256, 6), zspec(256, 7), zspec(512, 4), zspec(512, 5), zspec(512, 12)] + w_specs
    in_specs.append(pl.BlockSpec((1, 512), lambda b, n: (0, 0)))
    args = [z, z, z, z, z, wa_p, ba_p, gn_g]
    out_shape = [jax.ShapeDtypeStruct((nb * n_blk * TM, 512), BF16)]
    out_specs = [pl.BlockSpec((TM, 512), lambda b, n: (b * n_blk + n, 0))]
    if ctx:
        out_shape.append(kv_shape)
        out_specs.append(kv_spec)
    else:
        kv, tot = pl.pallas_call(
            _gla_state_kernel,
            out_shape=[kv_shape, jax.ShapeDtypeStruct((nb, n_blk, 512, 1), F32)],
            grid=(nb, n_blk),
            in_specs=[zspec(256, 7), zspec(512, 4), zspec(512, 12)] + w_specs,
            out_specs=[kv_spec, pl.BlockSpec((1, 1, 512, 1), lambda b, n: (b, n, 0, 0))],
            compiler_params=_params(2),
            name="gla_state_lat",
        )(z, z, z, wa_p, ba_p)
        in_specs += [pl.BlockSpec((1, n_blk, 2, N_HEADS, DK, DV), lambda b, n: (b, 0, 0, 0, 0, 0)),
                     pl.BlockSpec((1, n_blk, 512, 1), lambda b, n: (b, 0, 0, 0)),
                     pl.BlockSpec((1, 2, N_HEADS, DK, DV), lambda b, n: (b, 0, 0, 0, 0))]
        args += [kv, tot, s0]
    res = pl.pallas_call(
        functools.partial(_gla_kernel, n_blk=n_blk, has_state=not ctx, emit_state=ctx),
        out_shape=out_shape, grid=(nb, n_blk), in_specs=in_specs, out_specs=out_specs,
        compiler_params=_params(2),
        name="gla_ctx" if ctx else "gla_lat",
    )(*args)
    return (res[0], res[1]) if ctx else (res[0], None)


def _rope_tables():
    nf = MLA_ROPE // 4
    pos = np.arange(LAT_LEN)
    freqs = (10000.0 ** (-np.arange(nf, dtype=np.float32) / nf)).astype(np.float32)
    ang_r = ((pos // GRID_W).astype(np.float32)[:, None] * freqs).astype(np.float32)
    ang_c = ((pos % GRID_W).astype(np.float32)[:, None] * freqs).astype(np.float32)
    cos = np.ones((TM + LAT_LEN, HEAD_PAD), np.float32)
    sa = np.zeros((TM + LAT_LEN, HEAD_PAD), np.float32)
    sb = np.zeros((TM + LAT_LEN, HEAD_PAD), np.float32)
    o = MLA_NOPE
    for base, ang in ((o, ang_r), (o + 2 * nf, ang_c)):
        cos[TM:, base:base + nf] = np.cos(ang)
        cos[TM:, base + nf:base + 2 * nf] = np.cos(ang)
        sa[TM:, base:base + nf] = -np.sin(ang)
        sb[TM:, base + nf:base + 2 * nf] = np.sin(ang)
    return jnp.asarray(cos), jnp.asarray(sa), jnp.asarray(sb)


def _rope(x, cos, sa, sb):
    return x * cos + pltpu.roll(x, 128 - 8, 1) * sa + pltpu.roll(x, 8, 1) * sb


def _head_segments():
    seg = np.zeros((MLA_HEADS * HEAD_PAD, 128), np.float32)
    for h in range(MLA_HEADS):
        seg[h * HEAD_PAD:(h + 1) * HEAD_PAD, h] = 1.0
    return jnp.asarray(seg, BF16), jnp.asarray(seg.T.copy(), BF16)


def _head_norm(x, gain, seg_ref, segt_ref, lane_broadcast):
    ss = _dot((x * x).astype(BF16), seg_ref[...])
    rs = lax.rsqrt(ss * (1.0 / MLA_QK) + EPS)
    if lane_broadcast:
        return jnp.concatenate([x[:, 128 * h:128 * h + 128] * rs[:, h:h + 1] for h in range(MLA_HEADS)],
                               axis=1) * gain
    hi = rs.astype(BF16)
    lo = (rs - hi.astype(F32)).astype(BF16)
    return x * (_dot(hi, segt_ref[...]) + _dot(lo, segt_ref[...])) * gain


def _store_heads(x, o_ref, rope):
    if rope is None:
        o_ref[...] = x.astype(BF16)
    else:
        for h in range(MLA_HEADS):
            o_ref[:, 128 * h:128 * h + 128] = _rope(x[:, 128 * h:128 * h + 128], *rope).astype(BF16)


def _mla_keys(ckv, kr_tile, wk_ref, wv_ref, kn_ref, seg_ref, segt_ref, k_ref, v_ref, rope=None):
    cb = ckv.astype(BF16)
    kpre = _dot(cb, wk_ref[...]) + jnp.concatenate([kr_tile] * MLA_HEADS, axis=1)
    v_ref[...] = _dot(cb, wv_ref[...]).astype(BF16)
    _store_heads(_head_norm(kpre, kn_ref[...], seg_ref, segt_ref, rope is None), k_ref, rope)


def _mla_prep_kernel(small_ref, qa_ref, wuq_ref, qn_ref, kva_ref, wk_ref, wv_ref, kn_ref, seg_ref, segt_ref,
                     cos_ref, sa_ref, sb_ref, q_ref, k_ref, v_ref, ckv_ref, kr_ref):
    def body(rope):
        sm = small_ref[...].astype(F32)
        cq, ckv_raw, g3 = sm[:, 0:256], sm[:, 256:384], sm[:, 384:512]
        cqn = cq * lax.rsqrt(jnp.mean(cq * cq, axis=-1, keepdims=True) + EPS) * qa_ref[...]
        q = _dot(cqn.astype(BF16), wuq_ref[...])
        _store_heads(_head_norm(q, qn_ref[...], seg_ref, segt_ref, rope is None), q_ref, rope)
        ckv = ckv_raw * lax.rsqrt(jnp.mean(ckv_raw * ckv_raw, axis=-1, keepdims=True) + EPS) * kva_ref[...]
        ckv_ref[...] = ckv
        lane = lax.broadcasted_iota(jnp.int32, (1, 128), 1)
        kr = jnp.where(lane < MLA_ROPE, g3, 0.0)
        kr_ref[...] = kr
        _mla_keys(ckv, pltpu.roll(kr, MLA_NOPE, 1), wk_ref, wv_ref, kn_ref, seg_ref, segt_ref, k_ref, v_ref, rope)

    is_ctx = pl.program_id(0) < N_CTX_TILES

    @pl.when(is_ctx)
    def _():
        body(None)

    @pl.when(jnp.logical_not(is_ctx))
    def _():
        body((cos_ref[...], sa_ref[...], sb_ref[...]))


def _mla_cache_kernel(ckv_ref, kr_ref, wk_ref, wv_ref, kn_ref, seg_ref, segt_ref, k_ref, v_ref):
    _mla_keys(ckv_ref[...], pltpu.roll(kr_ref[...], MLA_NOPE, 1), wk_ref, wv_ref, kn_ref, seg_ref, segt_ref,
              k_ref, v_ref)


def _mla_prep(z, w, rope_tabs):
    def rope_blk(i):
        return jnp.where(i < N_CTX_TILES, 0, 1 + (i - N_CTX_TILES) % LAT_TILES)

    const = lambda shape: pl.BlockSpec(shape, lambda i: (0,) * len(shape))
    rope_spec = pl.BlockSpec((TM, HEAD_PAD), lambda i: (rope_blk(i), 0))
    row = lambda width: pl.BlockSpec((TM, width), lambda i: (i, 0))
    return pl.pallas_call(
        _mla_prep_kernel,
        out_shape=[jax.ShapeDtypeStruct((T_ALL, 1024), BF16), jax.ShapeDtypeStruct((T_ALL, 1024), BF16),
                   jax.ShapeDtypeStruct((T_ALL, 512), BF16), jax.ShapeDtypeStruct((T_ALL, 128), F32),
                   jax.ShapeDtypeStruct((T_ALL, 128), F32)],
        grid=(N_TILES,),
        in_specs=[pl.BlockSpec((TM, 512), lambda i: (i, 12)),
                  const((1, 256)), const((256, 1024)), const((1, 1024)), const((1, 128)),
                  const((128, 1024)), const((128, 512)), const((1, 1024)),
                  const((1024, 128)), const((128, 1024)),
                  rope_spec, rope_spec, rope_spec],
        out_specs=[row(1024), row(1024), row(512), row(128), row(128)],
        compiler_params=_params(1),
        name="mla_prep",
    )(z, w["qa_g"], w["wuq"], w["qn_g"], w["kva_g"], w["wk"], w["wv"], w["kn_g"], *_head_segments(), *rope_tabs)


def _mla_cache(ckv, kr_pad, w):
    const = lambda shape: pl.BlockSpec(shape, lambda i: (0,) * len(shape))
    row = lambda width: pl.BlockSpec((TM, width), lambda i: (i, 0))
    n = ckv.shape[0]
    return pl.pallas_call(
        _mla_cache_kernel,
        out_shape=[jax.ShapeDtypeStruct((n, 1024), BF16), jax.ShapeDtypeStruct((n, 512), BF16)],
        grid=(n // TM,),
        in_specs=[row(128), row(128), const((128, 1024)), const((128, 512)), const((1, 1024)),
                  const((1024, 128)), const((128, 1024))],
        out_specs=[row(1024), row(512)],
        compiler_params=_params(1),
        name="mla_cache",
    )(ckv, kr_pad, w["wk"], w["wv"], w["kn_g"], *_head_segments())


def _mla_attn_kernel(*refs, has_cache):
    q_ref, k_ref, v_ref = refs[:3]
    pos = 3
    if has_cache:
        kc_ref, vc_ref = refs[3:5]
        pos = 5
    o_ref = refs[pos]
    for p in range(MLA_HEADS // 2):
        acc = jnp.zeros((TM, 128), F32)
        for hh in range(2):
            h = 2 * p + hh
            lanes = slice(128 * h, 128 * h + 128)
            qh = q_ref[:, lanes]
            l1 = _dot_t(qh, k_ref[:, lanes])
            m = jnp.max(l1, axis=-1, keepdims=True)
            if has_cache:
                l0 = _dot_t(qh, kc_ref[:, lanes])
                m = jnp.maximum(m, jnp.max(l0, axis=-1, keepdims=True))
                p0 = jnp.exp(l0 - m)
            p1 = jnp.exp(l1 - m)
            den = jnp.sum(p1, axis=-1, keepdims=True)
            if has_cache:
                den = den + jnp.sum(p0, axis=-1, keepdims=True)
            mask = _lane_half_mask(hh)
            vp = v_ref[:, 128 * p:128 * p + 128]
            o = _dot(p1.astype(BF16), jnp.where(mask, vp, jnp.zeros_like(vp)))
            if has_cache:
                vcp = vc_ref[:, 128 * p:128 * p + 128]
                o += _dot(p0.astype(BF16), jnp.where(mask, vcp, jnp.zeros_like(vcp)))
            acc += o * (1.0 / den)
        o_ref[:, 128 * p:128 * p + 128] = acc.astype(BF16)


def _mla_attn(q, k, v, kc, vc, *, ctx):
    if ctx:
        nb, seq, row_blk, tile0 = N_CTX_SEQ, CTX_LEN, 0, 0
    else:
        nb, seq, row_blk, tile0 = N_LAT_SEQ, LAT_LEN, T_CTX // LAT_LEN, N_CTX_TILES
    nq = seq // TM
    in_specs = [pl.BlockSpec((TM, 1024), lambda b, i: (tile0 + b * nq + i, 0)),
                pl.BlockSpec((seq, 1024), lambda b, i: (row_blk + b, 0)),
                pl.BlockSpec((seq, 512), lambda b, i: (row_blk + b, 0))]
    args = [q, k, v]
    if not ctx:
        in_specs += [pl.BlockSpec((TM, 1024), lambda b, i: (b, 0)), pl.BlockSpec((TM, 512), lambda b, i: (b, 0))]
        args += [kc, vc]
    return pl.pallas_call(
        functools.partial(_mla_attn_kernel, has_cache=not ctx),
        out_shape=jax.ShapeDtypeStruct((nb * seq, 512), BF16),
        grid=(nb, nq), in_specs=in_specs,
        out_specs=pl.BlockSpec((TM, 512), lambda b, i: (b * nq + i, 0)),
        compiler_params=_params(2),
        name="mla_attn_ctx" if ctx else "mla_attn_lat",
    )(*args)


def _merge_kernel(retc_ref, retl_ref, glac_ref, glal_ref, mlac_ref, mlal_ref, m0_ref, m1_ref, m2_ref,
                  xc_ref, xl_ref, wb_ref, wo_ref,
                  g1_ref, n2_ref, sh2_ref, sc2_ref, rwh_ref, rwl_ref, rb_ref,
                  x1_ref, h_ref, idx_ref, w_ref, rank_ref, cnt_ref):
    @pl.when(pl.program_id(0) == 0)
    def _():
        cnt_ref[...] = jnp.zeros_like(cnt_ref)

    is_ctx = pl.program_id(0) < T_CTX // MERGE_TM
    mix = None
    for c_ref, l_ref, m_ref, n in ((retc_ref, retl_ref, m0_ref, 0), (glac_ref, glal_ref, m1_ref, 1),
                                   (mlac_ref, mlal_ref, m2_ref, 2)):
        branch = jnp.where(is_ctx, c_ref[...], l_ref[...])
        term = _sigmoid(m_ref[...]).astype(F32) * _dot(branch, wb_ref[n])
        mix = term if mix is None else mix + term
    out = _dot(mix.astype(BF16), wo_ref[...])
    x1 = jnp.where(is_ctx, xc_ref[...], xl_ref[...]) + g1_ref[0] * out
    x1_ref[...] = x1
    h = x1 * lax.rsqrt(jnp.mean(x1 * x1, axis=-1, keepdims=True) + EPS) * n2_ref[...]
    h = h * (1.0 + sc2_ref[0]) + sh2_ref[0]
    h_ref[...] = h
    hh = h.astype(BF16)
    hl = (h - hh.astype(F32)).astype(BF16)
    logits = _dot(hh, rwh_ref[...]) + _dot(hh, rwl_ref[...]) + _dot(hl, rwh_ref[...]) + rb_ref[...]
    rows = MERGE_TM // ROUTE_CHUNKS
    lane = lax.broadcasted_iota(jnp.int32, (rows, 128), 1)
    lanef = lane.astype(F32)
    onehots, osums = [], []
    for c in range(ROUTE_CHUNKS):
        l = jnp.where(lane < N_EXPERTS, logits[c * rows:(c + 1) * rows], -jnp.inf)
        vals, idxs = [], []
        for _ in range(TOP_K):
            m = jnp.max(l, axis=-1, keepdims=True)
            ix = jnp.min(jnp.where(l == m, lanef, 128.0), axis=-1, keepdims=True)
            vals.append(m)
            idxs.append(ix)
            l = jnp.where(lanef == ix, -jnp.inf, l)
        es = [jnp.exp(v - vals[0]) for v in vals]
        inv = 1.0 / (es[0] + es[1] + es[2] + es[3])
        idx_out = jnp.zeros((rows, 128), F32)
        w_out = jnp.zeros((rows, 128), F32)
        for kk in range(TOP_K):
            idx_out = jnp.where(lane == kk, idxs[kk], idx_out)
            w_out = jnp.where(lane == kk, es[kk] * inv, w_out)
        idx_ref[c * rows:(c + 1) * rows, :] = idx_out.astype(jnp.int32)
        w_ref[c * rows:(c + 1) * rows, :] = w_out
        oh = [jnp.where(lanef == ix, 1.0, 0.0) for ix in idxs]
        onehots.append(oh)
        osums.append((oh[0] + oh[1]) + (oh[2] + oh[3]))

    osum = jnp.concatenate(osums, axis=0)
    ri = lax.broadcasted_iota(jnp.int32, (MERGE_TM, MERGE_TM), 0)
    ci = lax.broadcasted_iota(jnp.int32, (MERGE_TM, MERGE_TM), 1)
    before = jnp.where(ri > ci, 1.0, 0.0).astype(BF16)
    prior = _dot(before, osum.astype(BF16)) + cnt_ref[0:1, :]
    for c in range(ROUTE_CHUNKS):
        pc = prior[c * rows:(c + 1) * rows]
        rank_out = jnp.zeros((rows, 128), F32)
        for kk in range(TOP_K):
            rank_out = jnp.where(lane == kk, jnp.sum(onehots[c][kk] * pc, axis=-1, keepdims=True), rank_out)
        rank_ref[c * rows:(c + 1) * rows, :] = rank_out.astype(jnp.int32)
    cnt_ref[...] = cnt_ref[...] + jnp.sum(osum, axis=0, keepdims=True)


def _merge(branches, z, xc, xl, modr, w, layer):
    mrow = _mod_row(MERGE_TM)
    base = layer * N_MOD

    def mod_spec(part):
        return pl.BlockSpec((1, 1, D_MODEL), lambda i: ((base + mrow(i)) * 6 + part, 0, 0))

    const = lambda shape: pl.BlockSpec(shape, lambda i: (0,) * len(shape))
    row = lambda width: pl.BlockSpec((MERGE_TM, width), lambda i: (i, 0))
    gate = lambda col: pl.BlockSpec((MERGE_TM, 1024), lambda i: (i, col))
    return pl.pallas_call(
        _merge_kernel,
        out_shape=[jax.ShapeDtypeStruct((T_ALL, D_MODEL), F32), jax.ShapeDtypeStruct((T_ALL, D_MODEL), F32),
                   jax.ShapeDtypeStruct((T_ALL, 128), jnp.int32), jax.ShapeDtypeStruct((T_ALL, 128), F32),
                   jax.ShapeDtypeStruct((T_ALL, 128), jnp.int32), jax.ShapeDtypeStruct((8, 128), F32)],
        grid=(T_ALL // MERGE_TM,),
        in_specs=[_ctx_spec(MERGE_TM, 512), _lat_spec(MERGE_TM, 512)] * 3 + [gate(3), gate(4), gate(5),
                  _ctx_spec(MERGE_TM, D_MODEL), _lat_spec(MERGE_TM, D_MODEL),
                  const((3, 512, 1024)), const((1024, 1024)),
                  mod_spec(2), const((1, 1024)), mod_spec(3), mod_spec(4),
                  const((1024, 128)), const((1024, 128)), const((1, 128))],
        out_specs=[row(1024), row(1024), row(128), row(128), row(128), const((8, 128))],
        compiler_params=_params(1),
        name="merge",
    )(*branches, z, z, z, xc, xl, w["wb"], w["wo"], modr, w["n2_g"], modr, modr,
      w["rw_hi"], w["rw_lo"], w["rb"])


def _route(top_idx, rank, counts):
    flat_e = top_idx.reshape(N_SLOTS)
    onehot = (flat_e[:, None] == jnp.arange(N_EXPERTS, dtype=jnp.int32)[None, :]).astype(jnp.int32)
    padded = (counts + MOE_ROWS - 1) // MOE_ROWS * MOE_ROWS
    pad_end = jnp.cumsum(padded)
    pad_start = pad_end - padded
    dest = (rank.reshape(N_SLOTS) + jnp.sum(onehot * pad_start[None, :], axis=1)).astype(jnp.int32)
    blk_start = jnp.arange(N_MOE_BLOCKS, dtype=jnp.int32) * MOE_ROWS
    block_e = jnp.minimum(jnp.sum((pad_end[None, :] <= blk_start[:, None]).astype(jnp.int32), axis=1),
                          N_EXPERTS - 1).astype(jnp.int32)
    n_used = (pad_end[-1] // MOE_ROWS).astype(jnp.int32)
    e_hot = (block_e[:, None] == jnp.arange(N_EXPERTS, dtype=jnp.int32)[None, :]).astype(jnp.int32)
    nxt_blk = jnp.sum(e_hot * pad_end[None, :], axis=1) // MOE_ROWS
    b_hot = (nxt_blk[:, None] == jnp.arange(N_MOE_BLOCKS, dtype=jnp.int32)[None, :]).astype(jnp.int32)
    next_e = jnp.where(nxt_blk < n_used, jnp.sum(b_hot * block_e[None, :], axis=1), block_e).astype(jnp.int32)
    tail_start = (pad_start + counts).astype(jnp.int32)
    return dest, tail_start, block_e, n_used.reshape(1), next_e


def _rows_to_tiles(x):
    r = x.shape[0]
    blocks = jnp.stack([x[:, 128 * g:128 * (g + 1)].reshape(r // 8, 8, 128) for g in range(D_MODEL // 128)], axis=1)
    return jnp.swapaxes(blocks, 1, 2).reshape(r, D_MODEL // 128, 128)


def _tiles_to_rows(v):
    r = v.shape[0]
    blocks = jnp.swapaxes(v.reshape(r // 8, 8, D_MODEL // 128, 128), 1, 2)
    return jnp.concatenate([blocks[:, g].reshape(r, 128) for g in range(D_MODEL // 128)], axis=1)


def _dispatch_kernel(dest_ref, tail_ref, nb_ref, h_ref, xs_ref, zero_buf, stage, sem, ssem):
    i = pl.program_id(0)

    @pl.when(i == 0)
    def _():
        zero_buf[...] = jnp.zeros_like(zero_buf)
        fills = [pltpu.make_async_copy(zero_buf, xs_ref.at[pl.ds(tail_ref[e], MOE_ROWS)], sem)
                 for e in range(N_EXPERTS)]
        for f in fills:
            f.start()
        for f in fills:
            f.wait()

        def fill_block(b, c):
            f = pltpu.make_async_copy(zero_buf, xs_ref.at[pl.ds(b * MOE_ROWS, MOE_ROWS)], sem)
            f.start()
            f.wait()
            return c
        lax.fori_loop(nb_ref[0], (N_MOE_ROWS + XS_EXTRA) // MOE_ROWS, fill_block, 0)

    base = i * TM * TOP_K
    cur = i % 2
    row0 = pl.multiple_of(cur * TM, TM)

    def wait_tile(slot):
        for kk in range(TOP_K):
            pltpu.make_async_copy(stage.at[pl.ds(0, TM)], xs_ref.at[pl.ds(0, TM)], ssem.at[slot]).wait()

    @pl.when(i >= 2)
    def _():
        wait_tile(cur)
    stage[pl.ds(row0, TM)] = _rows_to_tiles(h_ref[...])

    def issue(t, c):
        for kk in range(TOP_K):
            pltpu.make_async_copy(stage.at[row0 + t], xs_ref.at[dest_ref[base + t * TOP_K + kk]], ssem.at[cur]
                                  ).start(priority=kk % 2)
        return c
    lax.fori_loop(0, TM, issue, 0, unroll=8)

    @pl.when(i == N_TILES - 1)
    def _():
        wait_tile(1 - cur)
        wait_tile(cur)


def _moe_dispatch(h, dest, tail_start, n_used):
    return pl.pallas_call(
        _dispatch_kernel,
        out_shape=jax.ShapeDtypeStruct((N_MOE_ROWS + XS_EXTRA,) + ROW_TILE, F32),
        grid_spec=pltpu.PrefetchScalarGridSpec(
            num_scalar_prefetch=3, grid=(N_TILES,),
            in_specs=[pl.BlockSpec((TM, D_MODEL), lambda i, d, t, nb: (i, 0))],
            out_specs=pl.BlockSpec(memory_space=pl.ANY),
            scratch_shapes=[pltpu.VMEM((MOE_ROWS,) + ROW_TILE, F32), pltpu.VMEM((2 * TM,) + ROW_TILE, F32),
                            pltpu.SemaphoreType.DMA, pltpu.SemaphoreType.DMA((2,))]),
        compiler_params=_params(1),
        name="moe_dispatch",
    )(dest, tail_start, n_used, h)


def _expert_kernel(be_ref, nb_ref, nxt_ref, x_ref, wgu_hbm, bgu_ref, wd_hbm, bd_ref, o_ref,
                   wgu_st, wd_st, wgu_bf, wd_bf, sem, *, layer):
    i = pl.program_id(0)
    e = be_ref[i]
    prev = be_ref[jnp.maximum(i - 1, 0)]

    def weight_copies(expert):
        idx = layer * N_EXPERTS + expert
        return (pltpu.make_async_copy(wgu_hbm.at[idx], wgu_st, sem.at[0]),
                pltpu.make_async_copy(wd_hbm.at[idx], wd_st, sem.at[1]))

    @pl.when(i == 0)
    def _():
        for cp in weight_copies(e):
            cp.start()

    @pl.when(((i == 0) | (e != prev)) & (i < nb_ref[0]))
    def _():
        for cp in weight_copies(e):
            cp.wait()
        wgu_bf[...] = wgu_st[...].astype(BF16)
        wd_bf[...] = wd_st[...].astype(BF16)
        nxt = nxt_ref[i]

        @pl.when(nxt != e)
        def _():
            for cp in weight_copies(nxt):
                cp.start()

    @pl.when(i < nb_ref[0])
    def _():
        gu = _dot(_tiles_to_rows(x_ref[...]).astype(BF16), wgu_bf[...]) + bgu_ref[0]
        gate = jnp.minimum(gu[:, :D_EXPERT], SWIGLU_LIMIT)
        up = jnp.clip(gu[:, D_EXPERT:], -SWIGLU_LIMIT, SWIGLU_LIMIT)
        act = (up + 1.0) * gate * _sigmoid(SWIGLU_ALPHA * gate)
        o_ref[...] = _rows_to_tiles(_dot(act.astype(BF16), wd_bf[...]) + bd_ref[0])

    @pl.when(i >= nb_ref[0])
    def _():
        o_ref[...] = jnp.zeros_like(o_ref)


def _moe_experts(xs, block_e, n_used, next_e, w_gu, b_gu, w_down, b_down, layer):
    w_idx = lambda i, be, nb, nx: (layer * N_EXPERTS + be[i], 0, 0)
    return pl.pallas_call(
        functools.partial(_expert_kernel, layer=layer),
        out_shape=jax.ShapeDtypeStruct((N_MOE_ROWS,) + ROW_TILE, F32),
        grid_spec=pltpu.PrefetchScalarGridSpec(
            num_scalar_prefetch=3, grid=(N_MOE_BLOCKS,),
            in_specs=[pl.BlockSpec((MOE_ROWS,) + ROW_TILE, lambda i, be, nb, nx: (jnp.minimum(i, nb[0] - 1), 0, 0)),
                      pl.BlockSpec(memory_space=pl.ANY),
                      pl.BlockSpec((1, 1, 2 * D_EXPERT), w_idx),
                      pl.BlockSpec(memory_space=pl.ANY),
                      pl.BlockSpec((1, 1, D_MODEL), w_idx)],
            out_specs=pl.BlockSpec((MOE_ROWS,) + ROW_TILE, lambda i, be, nb, nx: (i, 0, 0)),
            scratch_shapes=[pltpu.VMEM((D_MODEL, 2 * D_EXPERT), F32), pltpu.VMEM((D_EXPERT, D_MODEL), F32),
                            pltpu.VMEM((D_MODEL, 2 * D_EXPERT), BF16), pltpu.VMEM((D_EXPERT, D_MODEL), BF16),
                            pltpu.SemaphoreType.DMA((2,))]),
        compiler_params=_params(1),
        name="moe_experts",
    )(block_e, n_used, next_e, xs, w_gu, b_gu, w_down, b_down)


def _combine_kernel(dest_ref, x_ref, g2_ref, w_ref, eo_ref, yc_ref, yl_ref, buf, sem):
    i = pl.program_id(0)
    cur = i % 2

    def issue_tile(tile, slot):
        base = tile * TM * TOP_K
        row0 = slot * (TOP_K * TM)

        def issue(t, c):
            for kk in range(TOP_K):
                pltpu.make_async_copy(eo_ref.at[dest_ref[base + t * TOP_K + kk]], buf.at[row0 + kk * TM + t],
                                      sem.at[slot]).start(priority=kk % 2)
            return c
        lax.fori_loop(0, TM, issue, 0, unroll=8)

    @pl.when(i == 0)
    def _():
        issue_tile(0, 0)

    @pl.when(i + 1 < N_TILES)
    def _():
        issue_tile(i + 1, 1 - cur)

    for kk in range(TOP_K):
        pltpu.make_async_copy(eo_ref.at[pl.ds(0, TM)], buf.at[pl.ds(0, TM)], sem.at[cur]).wait()
    w = w_ref[...]
    ff = None
    for kk in range(TOP_K):
        rows = _tiles_to_rows(buf[pl.ds(pl.multiple_of(cur * (TOP_K * TM) + kk * TM, TM), TM)])
        term = rows * w[:, kk:kk + 1]
        ff = term if ff is None else ff + term
    y = x_ref[...] + g2_ref[0] * ff
    is_ctx = i < N_CTX_TILES

    @pl.when(is_ctx)
    def _():
        yc_ref[...] = y

    @pl.when(jnp.logical_not(is_ctx))
    def _():
        yl_ref[...] = y


def _moe_combine(dest, x1, modr, top_w, eo, layer):
    mrow = _mod_row(TM)
    base = layer * N_MOD
    return pl.pallas_call(
        _combine_kernel,
        out_shape=[jax.ShapeDtypeStruct((T_CTX, D_MODEL), F32), jax.ShapeDtypeStruct((T_LAT, D_MODEL), F32)],
        grid_spec=pltpu.PrefetchScalarGridSpec(
            num_scalar_prefetch=1, grid=(N_TILES,),
            in_specs=[pl.BlockSpec((TM, D_MODEL), lambda i, d: (i, 0)),
                      pl.BlockSpec((1, 1, D_MODEL), lambda i, d: ((base + mrow(i)) * 6 + 5, 0, 0)),
                      pl.BlockSpec((TM, 128), lambda i, d: (i, 0)),
                      pl.BlockSpec(memory_space=pl.ANY)],
            out_specs=[pl.BlockSpec((TM, D_MODEL), lambda i, d: (jnp.minimum(i, N_CTX_TILES - 1), 0)),
                       pl.BlockSpec((TM, D_MODEL), lambda i, d: (jnp.maximum(i - N_CTX_TILES, 0), 0))],
            scratch_shapes=[pltpu.VMEM((2 * TOP_K * TM,) + ROW_TILE, F32), pltpu.SemaphoreType.DMA((2,))]),
        compiler_params=_params(1),
        name="moe_combine",
    )(dest, x1, modr, top_w, eo)


def _pad_heads(w, n_heads, width):
    lead = w.shape[:-1]
    w = w.reshape(lead + (n_heads, width))
    w = jnp.pad(w, [(0, 0)] * len(lead) + [(0, 0), (0, HEAD_PAD - width)])
    return w.reshape(lead + (n_heads * HEAD_PAD,))


def _permute_w_in_kernel(w_ref, o_ref):
    w = w_ref[0]
    o_ref[0] = jnp.concatenate([w[:, :3072], w[:, 3520:6592], w[:, 3104:3520], w[:, 3072:3104], w[:, 6592:]],
                               axis=1)


def _permute_w_in(w_in):
    rows = 128
    w_pad = jnp.pad(w_in, ((0, 0), (0, 0), (0, DZ - w_in.shape[2]))).astype(BF16)
    return pl.pallas_call(
        _permute_w_in_kernel,
        out_shape=jax.ShapeDtypeStruct((DEPTH, D_MODEL, DZ), BF16),
        grid=(DEPTH, D_MODEL // rows),
        in_specs=[pl.BlockSpec((1, rows, DZ), lambda l, r: (l, r, 0))],
        out_specs=pl.BlockSpec((1, rows, DZ), lambda l, r: (l, r, 0)),
        compiler_params=_params(2),
        name="permute_w_in",
    )(w_pad)


def _layer_weights(l, gla_wa2, gla_ba, mla_qa_g, mla_wuq, mla_kva_g, mla_wukv, mla_qn_g, mla_kn_g,
                   w_branch, w_out, router_w, router_b, norm2_g):
    wa_p = jnp.zeros((512, 512), F32)
    wa_p = wa_p.at[416:432, 0:256].set(gla_wa2[l, 0]).at[432:448, 256:512].set(gla_wa2[l, 1]).astype(BF16)
    ba_p = gla_ba[l].reshape(1, 512)
    wukv = mla_wukv[l].reshape(128, MLA_HEADS, MLA_NOPE + MLA_V)
    rw = jnp.pad(router_w[l], ((0, 0), (0, 128 - N_EXPERTS)))
    rw_hi = rw.astype(BF16)
    return {
        "wa": wa_p, "ba": ba_p,
        "qa_g": mla_qa_g[l].reshape(1, 256),
        "wuq": _pad_heads(mla_wuq[l], MLA_HEADS, MLA_QK).astype(BF16),
        "qn_g": jnp.tile(jnp.pad(mla_qn_g[l], (0, HEAD_PAD - MLA_QK)), MLA_HEADS).reshape(1, 1024) * MLA_QK ** -0.5,
        "kn_g": jnp.tile(jnp.pad(mla_kn_g[l], (0, HEAD_PAD - MLA_QK)), MLA_HEADS).reshape(1, 1024),
        "kva_g": mla_kva_g[l].reshape(1, 128),
        "wk": _pad_heads(wukv[:, :, :MLA_NOPE].reshape(128, MLA_HEADS * MLA_NOPE), MLA_HEADS, MLA_NOPE).astype(BF16),
        "wv": wukv[:, :, MLA_NOPE:].reshape(128, MLA_HEADS * MLA_V).astype(BF16),
        "wb": w_branch[l].astype(BF16), "wo": w_out[l].astype(BF16),
        "rw_hi": rw_hi, "rw_lo": (rw - rw_hi.astype(F32)).astype(BF16),
        "rb": jnp.pad(router_b[l], (0, 128 - N_EXPERTS)).reshape(1, 128),
        "n2_g": norm2_g[l].reshape(1, D_MODEL),
    }


def kernel(x_prompt, x_sample, cache_mla_ckv, cache_mla_krope, state_ret, state_gla, c, c_ctx, w_mod, b_mod, norm1_g, norm2_g, w_in, ret_gn_g, gla_wa2, gla_ba, gla_norm_g, mla_qa_g, mla_wuq, mla_kva_g, mla_wukv, mla_qn_g, mla_kn_g, w_branch, w_out, router_w, router_b, moe_w_gu, moe_b_gu, moe_w_down, moe_b_down):
    xc, xl = x_prompt.reshape(T_CTX, D_MODEL), x_sample.reshape(T_LAT, D_MODEL)
    cc = jnp.concatenate([c_ctx[None, :], c, jnp.zeros((N_MOD - 1 - N_LAT_SEQ, D_MODEL), F32)], axis=0)
    modr = _modulation(cc, w_mod, b_mod).reshape(DEPTH * N_MOD * 6, 1, D_MODEL)
    rope_tabs = _rope_tables()
    w_in_p = _permute_w_in(w_in)
    w_gu = moe_w_gu.reshape(DEPTH * N_EXPERTS, D_MODEL, 2 * D_EXPERT)
    b_gu = moe_b_gu.reshape(DEPTH * N_EXPERTS, 1, 2 * D_EXPERT)
    w_dn = moe_w_down.reshape(DEPTH * N_EXPERTS, D_EXPERT, D_MODEL)
    b_dn = moe_b_down.reshape(DEPTH * N_EXPERTS, 1, D_MODEL)

    ckv_l, krope_l, ret_l, gla_l = [], [], [], []
    for l in range(DEPTH):
        w = _layer_weights(l, gla_wa2, gla_ba, mla_qa_g, mla_wuq, mla_kva_g, mla_wukv, mla_qn_g, mla_kn_g,
                           w_branch, w_out, router_w, router_b, norm2_g)
        z = _in_proj(xc, xl, norm1_g[l].reshape(1, D_MODEL), modr, w_in_p, l)

        gn = ret_gn_g[l].reshape(1, 512)
        ret_c, ret_state = _retention(z, gn, None, ctx=True)
        (ret_s,) = _retention(z, gn, state_ret[:, l], ctx=False)
        gng = gla_norm_g[l].reshape(1, 512)
        gla_c, gla_state = _gla(z, w["wa"], w["ba"], gng, None, ctx=True)
        gla_s, _ = _gla(z, w["wa"], w["ba"], gng, state_gla[:, l], ctx=False)

        q, k, v, ckv, kr = _mla_prep(z, w, rope_tabs)
        kc, vc = _mla_cache(cache_mla_ckv[:, l].reshape(N_LAT_SEQ * CTX_LEN, 128),
                            jnp.pad(cache_mla_krope[:, l].reshape(N_LAT_SEQ * CTX_LEN, MLA_ROPE),
                                    ((0, 0), (0, 128 - MLA_ROPE))), w)
        mla_c = _mla_attn(q, k, v, None, None, ctx=True)
        mla_s = _mla_attn(q, k, v, kc, vc, ctx=False)

        x1, h2, top_idx, top_w, rank, cnt = _merge((ret_c, ret_s, gla_c, gla_s, mla_c, mla_s), z, xc, xl, modr, w, l)
        dest, tail_start, block_e, n_used, next_e = _route(top_idx[:, :TOP_K], rank[:, :TOP_K],
                                                           cnt[0, :N_EXPERTS].astype(jnp.int32))
        xs = _moe_dispatch(h2, dest, tail_start, n_used)
        eo = _moe_experts(xs, block_e, n_used, next_e, w_gu, b_gu, w_dn, b_dn, l)
        xc, xl = _moe_combine(dest, x1, modr, top_w, eo, l)

        ckv_l.append(ckv[:T_CTX].reshape(N_CTX_SEQ, CTX_LEN, 128))
        krope_l.append(kr[:T_CTX, :MLA_ROPE].reshape(N_CTX_SEQ, CTX_LEN, MLA_ROPE))
        ret_l.append(ret_state)
        gla_l.append(gla_state.reshape(N_CTX_SEQ, 2, N_HEADS, DK, DV))

    y_p = xc.reshape(N_CTX_SEQ, CTX_LEN, D_MODEL)
    y_s = xl.reshape(N_LAT_SEQ, LAT_LEN, D_MODEL)
    return (y_p, y_s, jnp.stack(ckv_l, axis=1), jnp.stack(krope_l, axis=1),
            jnp.stack(ret_l, axis=1), jnp.stack(gla_l, axis=1))
```

```python
import functools

import jax
import jax.numpy as jnp
import numpy as np
from jax import lax
from jax.experimental import pallas as pl
from jax.experimental.pallas import tpu as pltpu

F32 = jnp.float32
BF16 = jnp.bfloat16

D_MODEL = 1024
DEPTH = 2
N_CTX_SEQ, CTX_LEN = 32, 256
N_LAT_SEQ, LAT_LEN = 4, 1024
T_CTX = N_CTX_SEQ * CTX_LEN
T_LAT = N_LAT_SEQ * LAT_LEN
T_ALL = T_CTX + T_LAT
TM = 256
N_TILES = T_ALL // TM
N_CTX_TILES = T_CTX // TM
LAT_TILES = LAT_LEN // TM
N_MOD = 8
EPS = 1e-6

N_HEADS = 4
DK, DV = 64, 128
GRID_W = 64
MLA_HEADS, MLA_NOPE, MLA_ROPE, MLA_V = 8, 64, 32, 64
MLA_QK = MLA_NOPE + MLA_ROPE
HEAD_PAD = 128
GLA_TAU = 16.0
N_EXPERTS, TOP_K, D_EXPERT = 32, 4, 1024
SWIGLU_LIMIT, SWIGLU_ALPHA = 7.0, 1.702
MOE_ROWS = 512
N_SLOTS = T_ALL * TOP_K
N_MOE_BLOCKS = N_SLOTS // MOE_ROWS + N_EXPERTS
N_MOE_ROWS = N_MOE_BLOCKS * MOE_ROWS
MERGE_TM = 512
ROUTE_CHUNKS = 8
XS_EXTRA = MOE_ROWS
ROW_TILE = (D_MODEL // 128, 128)

DZ = 6656
IN_TILE = 512
VMEM_LIMIT = 56 * 1024 * 1024

RET_LOG_F = [float(np.log1p(-np.exp2(-(5.0 + h)))) for h in range(N_HEADS)]
RET_LOG_B = [float(np.log1p(-np.exp2(-(5.5 + h)))) for h in range(N_HEADS)]


def _params(n_axes, vmem=VMEM_LIMIT):
    return pltpu.CompilerParams(dimension_semantics=("arbitrary",) * n_axes, vmem_limit_bytes=vmem)


def _sigmoid(x):
    return 1.0 / (1.0 + jnp.exp(-x))


def _dot(a, b):
    return jnp.dot(a, b, preferred_element_type=F32)


def _dot_t(a, b):
    return lax.dot_general(a, b, (((1,), (1,)), ((), ())), preferred_element_type=F32)


def _mod_row(tile_rows):
    def f(i):
        r0 = i * tile_rows
        return jnp.where(r0 < T_CTX, 0, 1 + (r0 - T_CTX) // LAT_LEN)
    return f


def _ctx_spec(rows, width):
    n_ctx = T_CTX // rows
    return pl.BlockSpec((rows, width), lambda i: (jnp.minimum(i, n_ctx - 1), 0))


def _lat_spec(rows, width):
    n_ctx = T_CTX // rows
    return pl.BlockSpec((rows, width), lambda i: (jnp.maximum(i - n_ctx, 0), 0))


def _mod_kernel(c_ref, w_ref, b_ref, o_ref):
    c = c_ref[...]
    s = c * _sigmoid(c)
    sh = s.astype(BF16)
    sl = (s - sh.astype(F32)).astype(BF16)
    w = w_ref[0]
    wh = w.astype(BF16)
    wl = (w - wh.astype(F32)).astype(BF16)
    o_ref[0] = _dot(sh, wh) + _dot(sh, wl) + _dot(sl, wh) + b_ref[0]


def _modulation(cc, w_mod, b_mod):
    n = 6 * D_MODEL
    blk = 2048
    return pl.pallas_call(
        _mod_kernel,
        out_shape=jax.ShapeDtypeStruct((DEPTH, N_MOD, n), F32),
        grid=(DEPTH, n // blk),
        in_specs=[pl.BlockSpec((N_MOD, D_MODEL), lambda l, j: (0, 0)),
                  pl.BlockSpec((1, D_MODEL, blk), lambda l, j: (l, 0, j)),
                  pl.BlockSpec((1, 1, blk), lambda l, j: (l, 0, j))],
        out_specs=pl.BlockSpec((1, N_MOD, blk), lambda l, j: (l, 0, j)),
        compiler_params=_params(2),
        name="modulation",
    )(cc, w_mod, b_mod.reshape(DEPTH, 1, n))


def _in_kernel(xc_ref, xl_ref, g_ref, sh_ref, sc_ref, w_ref, o_ref):
    x = jnp.where(pl.program_id(0) < T_CTX // IN_TILE, xc_ref[...], xl_ref[...])
    h = x * lax.rsqrt(jnp.mean(x * x, axis=-1, keepdims=True) + EPS) * g_ref[...]
    h = h * (1.0 + sc_ref[0]) + sh_ref[0]
    hb = h.astype(BF16)
    for n0 in range(0, DZ, 512):
        o_ref[:, n0:n0 + 512] = _dot(hb, w_ref[0, :, n0:n0 + 512]).astype(BF16)


def _in_proj(xc, xl, g, modr, w_in_p, layer):
    mrow = _mod_row(IN_TILE)
    base = layer * N_MOD

    def mod_spec(part):
        return pl.BlockSpec((1, 1, D_MODEL), lambda i: ((base + mrow(i)) * 6 + part, 0, 0))

    return pl.pallas_call(
        _in_kernel,
        out_shape=jax.ShapeDtypeStruct((T_ALL, DZ), BF16),
        grid=(T_ALL // IN_TILE,),
        in_specs=[_ctx_spec(IN_TILE, D_MODEL), _lat_spec(IN_TILE, D_MODEL),
                  pl.BlockSpec((1, D_MODEL), lambda i: (0, 0)),
                  mod_spec(0), mod_spec(1),
                  pl.BlockSpec((1, D_MODEL, DZ), lambda i: (layer, 0, 0))],
        out_specs=pl.BlockSpec((IN_TILE, DZ), lambda i: (i, 0)),
        compiler_params=_params(1),
        name="in_proj",
    )(xc, xl, g, modr, modr, w_in_p)


def _lane_half_mask(hh):
    lane = lax.broadcasted_iota(jnp.int32, (1, 128), 1)
    return (lane < 64) if hh == 0 else (lane >= 64)


@functools.lru_cache(maxsize=None)
def _ret_decay_table(seq):
    d = np.arange(seq)[:, None] - np.arange(seq)[None, :]
    tab = np.stack([np.exp(np.where(d > 0, RET_LOG_F[h] * d, -RET_LOG_B[h] * d)) for h in range(N_HEADS)])
    return (tab * np.where(d == 0, 2.0, 1.0) * DK ** -0.5).astype(np.float32)


def _ret_kernel(*refs, seq, has_state, emit_state):
    q_ref, k_ref, v_ref, g_ref, gn_ref, dec_ref = refs[:6]
    pos = 6
    if has_state:
        s0_ref = refs[pos]
        pos += 1
    o_ref = refs[pos]
    pos += 1
    if emit_state:
        st_ref = refs[pos]

    r0 = pl.multiple_of(pl.program_id(1) * TM, TM)
    qb = q_ref[pl.ds(r0, TM), :]
    ri = (lax.broadcasted_iota(jnp.int32, (TM, 1), 0) + r0).astype(F32)

    for h in range(N_HEADS):
        p, hh = h // 2, h % 2
        lanes = slice(128 * p, 128 * p + 128)
        qp = qb[:, lanes]
        qh = jnp.where(_lane_half_mask(hh), qp, jnp.zeros_like(qp))
        sc = _dot_t(qh, k_ref[:, lanes])
        o = _dot((sc * dec_ref[h]).astype(BF16), v_ref[:, 128 * h:128 * h + 128])
        if has_state:
            qf = qh.astype(F32)
            o += _dot((qf * jnp.exp(RET_LOG_F[h] * (ri + 1.0))).astype(BF16), s0_ref[0, 0, p].astype(BF16))
            o += _dot((qf * jnp.exp(RET_LOG_B[h] * (seq - ri))).astype(BF16), s0_ref[0, 1, p].astype(BF16))
        mu = jnp.mean(o, axis=-1, keepdims=True)
        d = o - mu
        var = jnp.mean(d * d, axis=-1, keepdims=True)
        on = d * lax.rsqrt(var + EPS)
        g = g_ref[:, 128 * h:128 * h + 128].astype(F32)
        out = on * gn_ref[:, 128 * h:128 * h + 128] * (g * _sigmoid(g))
        o_ref[:, 128 * h:128 * h + 128] = out.astype(BF16)

    if emit_state:
        jc = lax.broadcasted_iota(jnp.int32, (seq, 1), 0).astype(F32)
        lane = lax.broadcasted_iota(jnp.int32, (1, 128), 1)
        for p in range(2):
            kp = k_ref[:, 128 * p:128 * p + 128].astype(F32) * DK ** -0.5
            lgf = jnp.where(lane < 64, RET_LOG_F[2 * p], RET_LOG_F[2 * p + 1])
            lgb = jnp.where(lane < 64, RET_LOG_B[2 * p], RET_LOG_B[2 * p + 1])
            kdf = (kp * jnp.exp(lgf * (seq - 1.0 - jc))).T.astype(BF16)
            kdb = (kp * jnp.exp(lgb * jc)).T.astype(BF16)
            for hh in range(2):
                h = 2 * p + hh
                vh = v_ref[:, 128 * h:128 * h + 128]
                st_ref[0, 0, h] = _dot(kdf, vh)[64 * hh:64 * hh + 64, :]
                st_ref[0, 1, h] = _dot(kdb, vh)[64 * hh:64 * hh + 64, :]


def _retention(z, gn_g, s0, *, ctx):
    if ctx:
        nb, seq, row_blk, tile0 = N_CTX_SEQ, CTX_LEN, 0, 0
    else:
        nb, seq, row_blk, tile0 = N_LAT_SEQ, LAT_LEN, T_CTX // LAT_LEN, N_CTX_TILES
    nq = seq // TM
    in_specs = [pl.BlockSpec((seq, 256), lambda b, i: (row_blk + b, 0)),
                pl.BlockSpec((seq, 256), lambda b, i: (row_blk + b, 1)),
                pl.BlockSpec((seq, 512), lambda b, i: (row_blk + b, 1)),
                pl.BlockSpec((TM, 512), lambda b, i: (tile0 + b * nq + i, 2)),
                pl.BlockSpec((1, 512), lambda b, i: (0, 0)),
                pl.BlockSpec((N_HEADS, TM, seq), lambda b, i: (0, i, 0))]
    args = [z, z, z, z, gn_g, _ret_decay_table(seq)]
    out_shape = [jax.ShapeDtypeStruct((nb * seq, 512), BF16)]
    out_specs = [pl.BlockSpec((TM, 512), lambda b, i: (b * nq + i, 0))]
    if not ctx:
        in_specs.append(pl.BlockSpec((1, 2, 2, 128, 128), lambda b, i: (b, 0, 0, 0, 0)))
        args.append(s0.reshape(N_LAT_SEQ, 2, 2, 128, 128))
    else:
        out_shape.append(jax.ShapeDtypeStruct((nb, 2, N_HEADS, DK, DV), F32))
        out_specs.append(pl.BlockSpec((1, 2, N_HEADS, DK, DV), lambda b, i: (b, 0, 0, 0, 0)))
    return pl.pallas_call(
        functools.partial(_ret_kernel, seq=seq, has_state=not ctx, emit_state=ctx),
        out_shape=out_shape, grid=(nb, nq), in_specs=in_specs, out_specs=out_specs,
        compiler_params=_params(2),
        name="retention_ctx" if ctx else "retention_lat",
    )(*args)


def _gla_decay(small_ref, wa_ref, ba_ref):
    x = _dot(small_ref[...], wa_ref[...]) + ba_ref[...]
    la = -(jnp.maximum(-x, 0.0) + jnp.log(1.0 + jnp.exp(-jnp.abs(x)))) * (1.0 / GLA_TAU)
    ri = lax.broadcasted_iota(jnp.int32, (TM, TM), 0)
    ci = lax.broadcasted_iota(jnp.int32, (TM, TM), 1)
    ltri = jnp.where(ri >= ci, 1.0, 0.0).astype(BF16)
    hi = la.astype(BF16)
    r1 = la - hi.astype(F32)
    mid = r1.astype(BF16)
    lo = (r1 - mid.astype(F32)).astype(BF16)
    cum = _dot(ltri, hi) + _dot(ltri, mid) + _dot(ltri, lo)
    return la, cum


def _gla_state_kernel(k_ref, v_ref, small_ref, wa_ref, ba_ref, kv_ref, tot_ref):
    la, cum = _gla_decay(small_ref, wa_ref, ba_ref)
    bf, bb = cum[:, :256], cum[:, 256:]
    xb = bb - la[:, 256:]
    k = k_ref[...].astype(F32)
    kdf = k * jnp.exp(bf[TM - 1:TM, :] - bf)
    kdb = k * jnp.exp(xb)
    for p in range(2):
        kf_t = kdf[:, 128 * p:128 * p + 128].T.astype(BF16)
        kb_t = kdb[:, 128 * p:128 * p + 128].T.astype(BF16)
        for hh in range(2):
            h = 2 * p + hh
            vh = v_ref[:, 128 * h:128 * h + 128]
            kv_ref[0, 0, 0, h] = _dot(kf_t, vh)[64 * hh:64 * hh + 64, :]
            kv_ref[0, 0, 1, h] = _dot(kb_t, vh)[64 * hh:64 * hh + 64, :]
    tot_ref[0, 0] = jnp.sum(la.T, axis=-1, keepdims=True)


def _mid_bcast(x, s, r):
    w = 2 * s
    if w >= 8:
        n = TM // w
        x3 = x.reshape(n, w, 256)
        return jnp.broadcast_to(x3[:, r:r + 1, :], (n, w, 256)).reshape(TM, 256)
    x3 = x.reshape(TM // 8, 8, 256)
    sub = lax.broadcasted_iota(jnp.int32, (1, 8, 1), 1)
    out = None
    for blk in range(8 // w):
        rowv = jnp.broadcast_to(x3[:, blk * w + r:blk * w + r + 1, :], (TM // 8, 8, 256))
        out = rowv if out is None else jnp.where(sub >= blk * w, rowv, out)
    return out.reshape(TM, 256)


def _gla_kernel(*refs, n_blk, has_state, emit_state):
    q_ref, k_ref, v_ref, g_ref, small_ref, wa_ref, ba_ref, gn_ref = refs[:8]
    pos = 8
    if has_state:
        kv_ref, tot_ref, s0_ref = refs[pos:pos + 3]
        pos += 3
    o_ref = refs[pos]
    pos += 1
    if emit_state:
        kvo_ref = refs[pos]

    la, cum = _gla_decay(small_ref, wa_ref, ba_ref)
    bf, bb = cum[:, :256], cum[:, 256:]
    xb = bb - la[:, 256:]
    q = q_ref[...].astype(F32) * DK ** -0.5
    k = k_ref[...].astype(F32)
    row = lax.broadcasted_iota(jnp.int32, (TM, 1), 0)
    rowi = lax.broadcasted_iota(jnp.int32, (TM, TM), 0)
    colj = lax.broadcasted_iota(jnp.int32, (TM, TM), 1)
    low_half = _lane_half_mask(0)

    def join(fwd, bwd):
        ops = []
        for p in range(N_HEADS // 2):
            f = fwd[:, 128 * p:128 * p + 128]
            br = pltpu.roll(bwd[:, 128 * p:128 * p + 128], 64, 1)
            ops.append(jnp.where(low_half, f, br).astype(BF16))
            ops.append(jnp.where(low_half, br, f).astype(BF16))
        return ops

    qd, kd = join(q, q), join(k, k)
    acc = [jnp.where(rowi == colj, _dot_t(qo, ko), 0.0) for qo, ko in zip(qd, kd)]

    low_f = jnp.where(low_half, 1.0, 0.0)
    s = 1
    while s < TM:
        up_f = jnp.where(((row // s) % 2) == 1, 1.0, 0.0)
        live_even = jnp.where(up_f == low_f, 1.0, 0.0).astype(BF16)
        live = [live_even, 1.0 - live_even]
        dead = [live[1], live[0]]
        af = -jnp.abs(bf - _mid_bcast(bf, s, s - 1))
        ab = -jnp.abs(xb - _mid_bcast(xb, s, s))
        same = (rowi // (2 * s)) == (colj // (2 * s))
        for p in range(N_HEADS // 2):
            a_f = af[:, 128 * p:128 * p + 128]
            a_b = pltpu.roll(ab[:, 128 * p:128 * p + 128], 64, 1)
            for hh in range(2):
                h = 2 * p + hh
                arg = jnp.where(low_half, a_f, a_b) if hh == 0 else jnp.where(low_half, a_b, a_f)
                e = jnp.exp(arg).astype(BF16)
                sl = _dot_t(e * (qd[h] * live[hh]), e * (kd[h] * dead[hh]))
                acc[h] = acc[h] + (jnp.where(same, sl, 0.0) if 2 * s < TM else sl)
        s *= 2

    if has_state:
        n = pl.program_id(1)
        q_state = join(q * jnp.exp(bf), q * jnp.exp(bb[TM - 1:TM, :] - xb))

    if emit_state:
        kdf = k * jnp.exp(bf[TM - 1:TM, :] - bf)
        kdb = k * jnp.exp(xb)
        for p in range(N_HEADS // 2):
            kf_t = kdf[:, 128 * p:128 * p + 128].T.astype(BF16)
            kb_t = kdb[:, 128 * p:128 * p + 128].T.astype(BF16)
            for hh in range(2):
                h = 2 * p + hh
                vh = v_ref[:, 128 * h:128 * h + 128]
                kvo_ref[0, 0, 0, h] = _dot(kf_t, vh)[64 * hh:64 * hh + 64, :]
                kvo_ref[0, 0, 1, h] = _dot(kb_t, vh)[64 * hh:64 * hh + 64, :]

    for h in range(N_HEADS):
        o = _dot(acc[h].astype(BF16), v_ref[:, 128 * h:128 * h + 128])
        if has_state:
            sf = s0_ref[0, 0, h]
            for m in range(n_blk - 1):
                dec = jnp.exp(tot_ref[0, m, 64 * h:64 * h + 64, :])
                sf = jnp.where(m < n, dec * sf + kv_ref[0, m, 0, h], sf)
            sb = s0_ref[0, 1, h]
            for m in range(n_blk - 1, 0, -1):
                dec = jnp.exp(tot_ref[0, m, 256 + 64 * h:256 + 64 * h + 64, :])
                sb = jnp.where(m > n, dec * sb + kv_ref[0, m, 1, h], sb)
            state = jnp.concatenate([sf, sb] if h % 2 == 0 else [sb, sf], axis=0).astype(BF16)
            o += _dot(q_state[h], state)
        on = o * lax.rsqrt(jnp.mean(o * o, axis=-1, keepdims=True) + EPS)
        g = g_ref[:, 128 * h:128 * h + 128].astype(F32)
        out = on * gn_ref[:, 128 * h:128 * h + 128] * (g * _sigmoid(g))
        o_ref[:, 128 * h:128 * h + 128] = out.astype(BF16)


def _gla(z, wa_p, ba_p, gn_g, s0, *, ctx):
    if ctx:
        nb, n_blk, tile0 = N_CTX_SEQ, 1, 0
    else:
        nb, n_blk, tile0 = N_LAT_SEQ, LAT_TILES, N_CTX_TILES

    def zspec(width, col):
        return pl.BlockSpec((TM, width), lambda b, n: (tile0 + b * n_blk + n, col))

    w_specs = [pl.BlockSpec((512, 512), lambda b, n: (0, 0)), pl.BlockSpec((1, 512), lambda b, n: (0, 0))]
    kv_shape = jax.ShapeDtypeStruct((nb, n_blk, 2, N_HEADS, DK, DV), F32)
    kv_spec = pl.BlockSpec((1, 1, 2, N_HEADS, DK, DV), lambda b, n: (b, n, 0, 0, 0, 0))
    in_specs = [zspec(256, 6), zspec(256, 7), zspec(512, 4), zspec(512, 5), zspec(512, 12)] + w_specs
    in_specs.append(pl.BlockSpec((1, 512), lambda b, n: (0, 0)))
    args = [z, z, z, z, z, wa_p, ba_p, gn_g]
    out_shape = [jax.ShapeDtypeStruct((nb * n_blk * TM, 512), BF16)]
    out_specs = [pl.BlockSpec((TM, 512), lambda b, n: (b * n_blk + n, 0))]
    if ctx:
        out_shape.append(kv_shape)
        out_specs.append(kv_spec)
    else:
        kv, tot = pl.pallas_call(
            _gla_state_kernel,
            out_shape=[kv_shape, jax.ShapeDtypeStruct((nb, n_blk, 512, 1), F32)],
            grid=(nb, n_blk),
            in_specs=[zspec(256, 7), zspec(512, 4), zspec(512, 12)] + w_specs,
            out_specs=[kv_spec, pl.BlockSpec((1, 1, 512, 1), lambda b, n: (b, n, 0, 0))],
            compiler_params=_params(2),
            name="gla_state_lat",
        )(z, z, z, wa_p, ba_p)
        in_specs += [pl.BlockSpec((1, n_blk, 2, N_HEADS, DK, DV), lambda b, n: (b, 0, 0, 0, 0, 0)),
                     pl.BlockSpec((1, n_blk, 512, 1), lambda b, n: (b, 0, 0, 0)),
                     pl.BlockSpec((1, 2, N_HEADS, DK, DV), lambda b, n: (b, 0, 0, 0, 0))]
        args += [kv, tot, s0]
    res = pl.pallas_call(
        functools.partial(_gla_kernel, n_blk=n_blk, has_state=not ctx, emit_state=ctx),
        out_shape=out_shape, grid=(nb, n_blk), in_specs=in_specs, out_specs=out_specs,
        compiler_params=_params(2),
        name="gla_ctx" if ctx else "gla_lat",
    )(*args)
    return (res[0], res[1]) if ctx else (res[0], None)


def _rope_tables():
    nf = MLA_ROPE // 4
    pos = np.arange(LAT_LEN)
    freqs = (10000.0 ** (-np.arange(nf, dtype=np.float32) / nf)).astype(np.float32)
    ang_r = ((pos // GRID_W).astype(np.float32)[:, None] * freqs).astype(np.float32)
    ang_c = ((pos % GRID_W).astype(np.float32)[:, None] * freqs).astype(np.float32)
    cos = np.ones((TM + LAT_LEN, HEAD_PAD), np.float32)
    sa = np.zeros((TM + LAT_LEN, HEAD_PAD), np.float32)
    sb = np.zeros((TM + LAT_LEN, HEAD_PAD), np.float32)
    o = MLA_NOPE
    for base, ang in ((o, ang_r), (o + 2 * nf, ang_c)):
        cos[TM:, base:base + nf] = np.cos(ang)
        cos[TM:, base + nf:base + 2 * nf] = np.cos(ang)
        sa[TM:, base:base + nf] = -np.sin(ang)
        sb[TM:, base + nf:base + 2 * nf] = np.sin(ang)
    return jnp.asarray(cos), jnp.asarray(sa), jnp.asarray(sb)


def _rope(x, cos, sa, sb):
    return x * cos + pltpu.roll(x, 128 - 8, 1) * sa + pltpu.roll(x, 8, 1) * sb


def _head_segments():
    seg = np.zeros((MLA_HEADS * HEAD_PAD, 128), np.float32)
    for h in range(MLA_HEADS):
        seg[h * HEAD_PAD:(h + 1) * HEAD_PAD, h] = 1.0
    return jnp.asarray(seg, BF16), jnp.asarray(seg.T.copy(), BF16)


def _head_norm(x, gain, seg_ref, segt_ref, lane_broadcast):
    ss = _dot((x * x).astype(BF16), seg_ref[...])
    rs = lax.rsqrt(ss * (1.0 / MLA_QK) + EPS)
    if lane_broadcast:
        return jnp.concatenate([x[:, 128 * h:128 * h + 128] * rs[:, h:h + 1] for h in range(MLA_HEADS)],
                               axis=1) * gain
    hi = rs.astype(BF16)
    lo = (rs - hi.astype(F32)).astype(BF16)
    return x * (_dot(hi, segt_ref[...]) + _dot(lo, segt_ref[...])) * gain


def _store_heads(x, o_ref, rope):
    if rope is None:
        o_ref[...] = x.astype(BF16)
    else:
        for h in range(MLA_HEADS):
            o_ref[:, 128 * h:128 * h + 128] = _rope(x[:, 128 * h:128 * h + 128], *rope).astype(BF16)


def _mla_keys(ckv, kr_tile, wk_ref, wv_ref, kn_ref, seg_ref, segt_ref, k_ref, v_ref, rope=None):
    cb = ckv.astype(BF16)
    kpre = _dot(cb, wk_ref[...]) + jnp.concatenate([kr_tile] * MLA_HEADS, axis=1)
    v_ref[...] = _dot(cb, wv_ref[...]).astype(BF16)
    _store_heads(_head_norm(kpre, kn_ref[...], seg_ref, segt_ref, rope is None), k_ref, rope)


def _mla_prep_kernel(small_ref, qa_ref, wuq_ref, qn_ref, kva_ref, wk_ref, wv_ref, kn_ref, seg_ref, segt_ref,
                     cos_ref, sa_ref, sb_ref, q_ref, k_ref, v_ref, ckv_ref, kr_ref):
    def body(rope):
        sm = small_ref[...].astype(F32)
        cq, ckv_raw, g3 = sm[:, 0:256], sm[:, 256:384], sm[:, 384:512]
        cqn = cq * lax.rsqrt(jnp.mean(cq * cq, axis=-1, keepdims=True) + EPS) * qa_ref[...]
        q = _dot(cqn.astype(BF16), wuq_ref[...])
        _store_heads(_head_norm(q, qn_ref[...], seg_ref, segt_ref, rope is None), q_ref, rope)
        ckv = ckv_raw * lax.rsqrt(jnp.mean(ckv_raw * ckv_raw, axis=-1, keepdims=True) + EPS) * kva_ref[...]
        ckv_ref[...] = ckv
        lane = lax.broadcasted_iota(jnp.int32, (1, 128), 1)
        kr = jnp.where(lane < MLA_ROPE, g3, 0.0)
        kr_ref[...] = kr
        _mla_keys(ckv, pltpu.roll(kr, MLA_NOPE, 1), wk_ref, wv_ref, kn_ref, seg_ref, segt_ref, k_ref, v_ref, rope)

    is_ctx = pl.program_id(0) < N_CTX_TILES

    @pl.when(is_ctx)
    def _():
        body(None)

    @pl.when(jnp.logical_not(is_ctx))
    def _():
        body((cos_ref[...], sa_ref[...], sb_ref[...]))


def _mla_cache_kernel(ckv_ref, kr_ref, wk_ref, wv_ref, kn_ref, seg_ref, segt_ref, k_ref, v_ref):
    _mla_keys(ckv_ref[...], pltpu.roll(kr_ref[...], MLA_NOPE, 1), wk_ref, wv_ref, kn_ref, seg_ref, segt_ref,
              k_ref, v_ref)


def _mla_prep(z, w, rope_tabs):
    def rope_blk(i):
        return jnp.where(i < N_CTX_TILES, 0, 1 + (i - N_CTX_TILES) % LAT_TILES)

    const = lambda shape: pl.BlockSpec(shape, lambda i: (0,) * len(shape))
    rope_spec = pl.BlockSpec((TM, HEAD_PAD), lambda i: (rope_blk(i), 0))
    row = lambda width: pl.BlockSpec((TM, width), lambda i: (i, 0))
    return pl.pallas_call(
        _mla_prep_kernel,
        out_shape=[jax.ShapeDtypeStruct((T_ALL, 1024), BF16), jax.ShapeDtypeStruct((T_ALL, 1024), BF16),
                   jax.ShapeDtypeStruct((T_ALL, 512), BF16), jax.ShapeDtypeStruct((T_ALL, 128), F32),
                   jax.ShapeDtypeStruct((T_ALL, 128), F32)],
        grid=(N_TILES,),
        in_specs=[pl.BlockSpec((TM, 512), lambda i: (i, 12)),
                  const((1, 256)), const((256, 1024)), const((1, 1024)), const((1, 128)),
                  const((128, 1024)), const((128, 512)), const((1, 1024)),
                  const((1024, 128)), const((128, 1024)),
                  rope_spec, rope_spec, rope_spec],
        out_specs=[row(1024), row(1024), row(512), row(128), row(128)],
        compiler_params=_params(1),
        name="mla_prep",
    )(z, w["qa_g"], w["wuq"], w["qn_g"], w["kva_g"], w["wk"], w["wv"], w["kn_g"], *_head_segments(), *rope_tabs)


def _mla_cache(ckv, kr_pad, w):
    const = lambda shape: pl.BlockSpec(shape, lambda i: (0,) * len(shape))
    row = lambda width: pl.BlockSpec((TM, width), lambda i: (i, 0))
    n = ckv.shape[0]
    return pl.pallas_call(
        _mla_cache_kernel,
        out_shape=[jax.ShapeDtypeStruct((n, 1024), BF16), jax.ShapeDtypeStruct((n, 512), BF16)],
        grid=(n // TM,),
        in_specs=[row(128), row(128), const((128, 1024)), const((128, 512)), const((1, 1024)),
                  const((1024, 128)), const((128, 1024))],
        out_specs=[row(1024), row(512)],
        compiler_params=_params(1),
        name="mla_cache",
    )(ckv, kr_pad, w["wk"], w["wv"], w["kn_g"], *_head_segments())


def _mla_attn_kernel(*refs, has_cache):
    q_ref, k_ref, v_ref = refs[:3]
    pos = 3
    if has_cache:
        kc_ref, vc_ref = refs[3:5]
        pos = 5
    o_ref = refs[pos]
    for p in range(MLA_HEADS // 2):
        acc = jnp.zeros((TM, 128), F32)
        for hh in range(2):
            h = 2 * p + hh
            lanes = slice(128 * h, 128 * h + 128)
            qh = q_ref[:, lanes]
            l1 = _dot_t(qh, k_ref[:, lanes])
            m = jnp.max(l1, axis=-1, keepdims=True)
            if has_cache:
                l0 = _dot_t(qh, kc_ref[:, lanes])
                m = jnp.maximum(m, jnp.max(l0, axis=-1, keepdims=True))
                p0 = jnp.exp(l0 - m)
            p1 = jnp.exp(l1 - m)
            den = jnp.sum(p1, axis=-1, keepdims=True)
            if has_cache:
                den = den + jnp.sum(p0, axis=-1, keepdims=True)
            mask = _lane_half_mask(hh)
            vp = v_ref[:, 128 * p:128 * p + 128]
            o = _dot(p1.astype(BF16), jnp.where(mask, vp, jnp.zeros_like(vp)))
            if has_cache:
                vcp = vc_ref[:, 128 * p:128 * p + 128]
                o += _dot(p0.astype(BF16), jnp.where(mask, vcp, jnp.zeros_like(vcp)))
            acc += o * (1.0 / den)
        o_ref[:, 128 * p:128 * p + 128] = acc.astype(BF16)


def _mla_attn(q, k, v, kc, vc, *, ctx):
    if ctx:
        nb, seq, row_blk, tile0 = N_CTX_SEQ, CTX_LEN, 0, 0
    else:
        nb, seq, row_blk, tile0 = N_LAT_SEQ, LAT_LEN, T_CTX // LAT_LEN, N_CTX_TILES
    nq = seq // TM
    in_specs = [pl.BlockSpec((TM, 1024), lambda b, i: (tile0 + b * nq + i, 0)),
                pl.BlockSpec((seq, 1024), lambda b, i: (row_blk + b, 0)),
                pl.BlockSpec((seq, 512), lambda b, i: (row_blk + b, 0))]
    args = [q, k, v]
    if not ctx:
        in_specs += [pl.BlockSpec((TM, 1024), lambda b, i: (b, 0)), pl.BlockSpec((TM, 512), lambda b, i: (b, 0))]
        args += [kc, vc]
    return pl.pallas_call(
        functools.partial(_mla_attn_kernel, has_cache=not ctx),
        out_shape=jax.ShapeDtypeStruct((nb * seq, 512), BF16),
        grid=(nb, nq), in_specs=in_specs,
        out_specs=pl.BlockSpec((TM, 512), lambda b, i: (b * nq + i, 0)),
        compiler_params=_params(2),
        name="mla_attn_ctx" if ctx else "mla_attn_lat",
    )(*args)


def _merge_kernel(retc_ref, retl_ref, glac_ref, glal_ref, mlac_ref, mlal_ref, m0_ref, m1_ref, m2_ref,
                  xc_ref, xl_ref, wb_ref, wo_ref,
                  g1_ref, n2_ref, sh2_ref, sc2_ref, rwh_ref, rwl_ref, rb_ref,
                  x1_ref, h_ref, idx_ref, w_ref, rank_ref, cnt_ref):
    @pl.when(pl.program_id(0) == 0)
    def _():
        cnt_ref[...] = jnp.zeros_like(cnt_ref)

    is_ctx = pl.program_id(0) < T_CTX // MERGE_TM
    mix = None
    for c_ref, l_ref, m_ref, n in ((retc_ref, retl_ref, m0_ref, 0), (glac_ref, glal_ref, m1_ref, 1),
                                   (mlac_ref, mlal_ref, m2_ref, 2)):
        branch = jnp.where(is_ctx, c_ref[...], l_ref[...])
        term = _sigmoid(m_ref[...]).astype(F32) * _dot(branch, wb_ref[n])
        mix = term if mix is None else mix + term
    out = _dot(mix.astype(BF16), wo_ref[...])
    x1 = jnp.where(is_ctx, xc_ref[...], xl_ref[...]) + g1_ref[0] * out
    x1_ref[...] = x1
    h = x1 * lax.rsqrt(jnp.mean(x1 * x1, axis=-1, keepdims=True) + EPS) * n2_ref[...]
    h = h * (1.0 + sc2_ref[0]) + sh2_ref[0]
    h_ref[...] = h
    hh = h.astype(BF16)
    hl = (h - hh.astype(F32)).astype(BF16)
    logits = _dot(hh, rwh_ref[...]) + _dot(hh, rwl_ref[...]) + _dot(hl, rwh_ref[...]) + rb_ref[...]
    rows = MERGE_TM // ROUTE_CHUNKS
    lane = lax.broadcasted_iota(jnp.int32, (rows, 128), 1)
    lanef = lane.astype(F32)
    onehots, osums = [], []
    for c in range(ROUTE_CHUNKS):
        l = jnp.where(lane < N_EXPERTS, logits[c * rows:(c + 1) * rows], -jnp.inf)
        vals, idxs = [], []
        for _ in range(TOP_K):
            m = jnp.max(l, axis=-1, keepdims=True)
            ix = jnp.min(jnp.where(l == m, lanef, 128.0), axis=-1, keepdims=True)
            vals.append(m)
            idxs.append(ix)
            l = jnp.where(lanef == ix, -jnp.inf, l)
        es = [jnp.exp(v - vals[0]) for v in vals]
        inv = 1.0 / (es[0] + es[1] + es[2] + es[3])
        idx_out = jnp.zeros((rows, 128), F32)
        w_out = jnp.zeros((rows, 128), F32)
        for kk in range(TOP_K):
            idx_out = jnp.where(lane == kk, idxs[kk], idx_out)
            w_out = jnp.where(lane == kk, es[kk] * inv, w_out)
        idx_ref[c * rows:(c + 1) * rows, :] = idx_out.astype(jnp.int32)
        w_ref[c * rows:(c + 1) * rows, :] = w_out
        oh = [jnp.where(lanef == ix, 1.0, 0.0) for ix in idxs]
        onehots.append(oh)
        osums.append((oh[0] + oh[1]) + (oh[2] + oh[3]))

    osum = jnp.concatenate(osums, axis=0)
    ri = lax.broadcasted_iota(jnp.int32, (MERGE_TM, MERGE_TM), 0)
    ci = lax.broadcasted_iota(jnp.int32, (MERGE_TM, MERGE_TM), 1)
    before = jnp.where(ri > ci, 1.0, 0.0).astype(BF16)
    prior = _dot(before, osum.astype(BF16)) + cnt_ref[0:1, :]
    for c in range(ROUTE_CHUNKS):
        pc = prior[c * rows:(c + 1) * rows]
        rank_out = jnp.zeros((rows, 128), F32)
        for kk in range(TOP_K):
            rank_out = jnp.where(lane == kk, jnp.sum(onehots[c][kk] * pc, axis=-1, keepdims=True), rank_out)
        rank_ref[c * rows:(c + 1) * rows, :] = rank_out.astype(jnp.int32)
    cnt_ref[...] = cnt_ref[...] + jnp.sum(osum, axis=0, keepdims=True)


def _merge(branches, z, xc, xl, modr, w, layer):
    mrow = _mod_row(MERGE_TM)
    base = layer * N_MOD

    def mod_spec(part):
        return pl.BlockSpec((1, 1, D_MODEL), lambda i: ((base + mrow(i)) * 6 + part, 0, 0))

    const = lambda shape: pl.BlockSpec(shape, lambda i: (0,) * len(shape))
    row = lambda width: pl.BlockSpec((MERGE_TM, width), lambda i: (i, 0))
    gate = lambda col: pl.BlockSpec((MERGE_TM, 1024), lambda i: (i, col))
    return pl.pallas_call(
        _merge_kernel,
        out_shape=[jax.ShapeDtypeStruct((T_ALL, D_MODEL), F32), jax.ShapeDtypeStruct((T_ALL, D_MODEL), F32),
                   jax.ShapeDtypeStruct((T_ALL, 128), jnp.int32), jax.ShapeDtypeStruct((T_ALL, 128), F32),
                   jax.ShapeDtypeStruct((T_ALL, 128), jnp.int32), jax.ShapeDtypeStruct((8, 128), F32)],
        grid=(T_ALL // MERGE_TM,),
        in_specs=[_ctx_spec(MERGE_TM, 512), _lat_spec(MERGE_TM, 512)] * 3 + [gate(3), gate(4), gate(5),
                  _ctx_spec(MERGE_TM, D_MODEL), _lat_spec(MERGE_TM, D_MODEL),
                  const((3, 512, 1024)), const((1024, 1024)),
                  mod_spec(2), const((1, 1024)), mod_spec(3), mod_spec(4),
                  const((1024, 128)), const((1024, 128)), const((1, 128))],
        out_specs=[row(1024), row(1024), row(128), row(128), row(128), const((8, 128))],
        compiler_params=_params(1),
        name="merge",
    )(*branches, z, z, z, xc, xl, w["wb"], w["wo"], modr, w["n2_g"], modr, modr,
      w["rw_hi"], w["rw_lo"], w["rb"])


def _route(top_idx, rank, counts):
    flat_e = top_idx.reshape(N_SLOTS)
    onehot = (flat_e[:, None] == jnp.arange(N_EXPERTS, dtype=jnp.int32)[None, :]).astype(jnp.int32)
    padded = (counts + MOE_ROWS - 1) // MOE_ROWS * MOE_ROWS
    pad_end = jnp.cumsum(padded)
    pad_start = pad_end - padded
    dest = (rank.reshape(N_SLOTS) + jnp.sum(onehot * pad_start[None, :], axis=1)).astype(jnp.int32)
    blk_start = jnp.arange(N_MOE_BLOCKS, dtype=jnp.int32) * MOE_ROWS
    block_e = jnp.minimum(jnp.sum((pad_end[None, :] <= blk_start[:, None]).astype(jnp.int32), axis=1),
                          N_EXPERTS - 1).astype(jnp.int32)
    n_used = (pad_end[-1] // MOE_ROWS).astype(jnp.int32)
    e_hot = (block_e[:, None] == jnp.arange(N_EXPERTS, dtype=jnp.int32)[None, :]).astype(jnp.int32)
    nxt_blk = jnp.sum(e_hot * pad_end[None, :], axis=1) // MOE_ROWS
    b_hot = (nxt_blk[:, None] == jnp.arange(N_MOE_BLOCKS, dtype=jnp.int32)[None, :]).astype(jnp.int32)
    next_e = jnp.where(nxt_blk < n_used, jnp.sum(b_hot * block_e[None, :], axis=1), block_e).astype(jnp.int32)
    tail_start = (pad_start + counts).astype(jnp.int32)
    return dest, tail_start, block_e, n_used.reshape(1), next_e


def _rows_to_tiles(x):
    r = x.shape[0]
    blocks = jnp.stack([x[:, 128 * g:128 * (g + 1)].reshape(r // 8, 8, 128) for g in range(D_MODEL // 128)], axis=1)
    return jnp.swapaxes(blocks, 1, 2).reshape(r, D_MODEL // 128, 128)


def _tiles_to_rows(v):
    r = v.shape[0]
    blocks = jnp.swapaxes(v.reshape(r // 8, 8, D_MODEL // 128, 128), 1, 2)
    return jnp.concatenate([blocks[:, g].reshape(r, 128) for g in range(D_MODEL // 128)], axis=1)


def _dispatch_kernel(dest_ref, tail_ref, nb_ref, h_ref, xs_ref, zero_buf, stage, sem, ssem):
    i = pl.program_id(0)

    @pl.when(i == 0)
    def _():
        zero_buf[...] = jnp.zeros_like(zero_buf)
        fills = [pltpu.make_async_copy(zero_buf, xs_ref.at[pl.ds(tail_ref[e], MOE_ROWS)], sem)
                 for e in range(N_EXPERTS)]
        for f in fills:
            f.start()
        for f in fills:
            f.wait()

        def fill_block(b, c):
            f = pltpu.make_async_copy(zero_buf, xs_ref.at[pl.ds(b * MOE_ROWS, MOE_ROWS)], sem)
            f.start()
            f.wait()
            return c
        lax.fori_loop(nb_ref[0], (N_MOE_ROWS + XS_EXTRA) // MOE_ROWS, fill_block, 0)

    base = i * TM * TOP_K
    cur = i % 2
    row0 = pl.multiple_of(cur * TM, TM)

    def wait_tile(slot):
        for kk in range(TOP_K):
            pltpu.make_async_copy(stage.at[pl.ds(0, TM)], xs_ref.at[pl.ds(0, TM)], ssem.at[slot]).wait()

    @pl.when(i >= 2)
    def _():
        wait_tile(cur)
    stage[pl.ds(row0, TM)] = _rows_to_tiles(h_ref[...])

    def issue(t, c):
        for kk in range(TOP_K):
            pltpu.make_async_copy(stage.at[row0 + t], xs_ref.at[dest_ref[base + t * TOP_K + kk]], ssem.at[cur]
                                  ).start(priority=kk % 2)
        return c
    lax.fori_loop(0, TM, issue, 0, unroll=8)

    @pl.when(i == N_TILES - 1)
    def _():
        wait_tile(1 - cur)
        wait_tile(cur)


def _moe_dispatch(h, dest, tail_start, n_used):
    return pl.pallas_call(
        _dispatch_kernel,
        out_shape=jax.ShapeDtypeStruct((N_MOE_ROWS + XS_EXTRA,) + ROW_TILE, F32),
        grid_spec=pltpu.PrefetchScalarGridSpec(
            num_scalar_prefetch=3, grid=(N_TILES,),
            in_specs=[pl.BlockSpec((TM, D_MODEL), lambda i, d, t, nb: (i, 0))],
            out_specs=pl.BlockSpec(memory_space=pl.ANY),
            scratch_shapes=[pltpu.VMEM((MOE_ROWS,) + ROW_TILE, F32), pltpu.VMEM((2 * TM,) + ROW_TILE, F32),
                            pltpu.SemaphoreType.DMA, pltpu.SemaphoreType.DMA((2,))]),
        compiler_params=_params(1),
        name="moe_dispatch",
    )(dest, tail_start, n_used, h)


def _expert_kernel(be_ref, nb_ref, nxt_ref, x_ref, wgu_hbm, bgu_ref, wd_hbm, bd_ref, o_ref,
                   wgu_st, wd_st, wgu_bf, wd_bf, sem, *, layer):
    i = pl.program_id(0)
    e = be_ref[i]
    prev = be_ref[jnp.maximum(i - 1, 0)]

    def weight_copies(expert):
        idx = layer * N_EXPERTS + expert
        return (pltpu.make_async_copy(wgu_hbm.at[idx], wgu_st, sem.at[0]),
                pltpu.make_async_copy(wd_hbm.at[idx], wd_st, sem.at[1]))

    @pl.when(i == 0)
    def _():
        for cp in weight_copies(e):
            cp.start()

    @pl.when(((i == 0) | (e != prev)) & (i < nb_ref[0]))
    def _():
        for cp in weight_copies(e):
            cp.wait()
        wgu_bf[...] = wgu_st[...].astype(BF16)
        wd_bf[...] = wd_st[...].astype(BF16)
        nxt = nxt_ref[i]

        @pl.when(nxt != e)
        def _():
            for cp in weight_copies(nxt):
                cp.start()

    @pl.when(i < nb_ref[0])
    def _():
        gu = _dot(_tiles_to_rows(x_ref[...]).astype(BF16), wgu_bf[...]) + bgu_ref[0]
        gate = jnp.minimum(gu[:, :D_EXPERT], SWIGLU_LIMIT)
        up = jnp.clip(gu[:, D_EXPERT:], -SWIGLU_LIMIT, SWIGLU_LIMIT)
        act = (up + 1.0) * gate * _sigmoid(SWIGLU_ALPHA * gate)
        o_ref[...] = _rows_to_tiles(_dot(act.astype(BF16), wd_bf[...]) + bd_ref[0])

    @pl.when(i >= nb_ref[0])
    def _():
        o_ref[...] = jnp.zeros_like(o_ref)


def _moe_experts(xs, block_e, n_used, next_e, w_gu, b_gu, w_down, b_down, layer):
    w_idx = lambda i, be, nb, nx: (layer * N_EXPERTS + be[i], 0, 0)
    return pl.pallas_call(
        functools.partial(_expert_kernel, layer=layer),
        out_shape=jax.ShapeDtypeStruct((N_MOE_ROWS,) + ROW_TILE, F32),
        grid_spec=pltpu.PrefetchScalarGridSpec(
            num_scalar_prefetch=3, grid=(N_MOE_BLOCKS,),
            in_specs=[pl.BlockSpec((MOE_ROWS,) + ROW_TILE, lambda i, be, nb, nx: (jnp.minimum(i, nb[0] - 1), 0, 0)),
                      pl.BlockSpec(memory_space=pl.ANY),
                      pl.BlockSpec((1, 1, 2 * D_EXPERT), w_idx),
                      pl.BlockSpec(memory_space=pl.ANY),
                      pl.BlockSpec((1, 1, D_MODEL), w_idx)],
            out_specs=pl.BlockSpec((MOE_ROWS,) + ROW_TILE, lambda i, be, nb, nx: (i, 0, 0)),
            scratch_shapes=[pltpu.VMEM((D_MODEL, 2 * D_EXPERT), F32), pltpu.VMEM((D_EXPERT, D_MODEL), F32),
                            pltpu.VMEM((D_MODEL, 2 * D_EXPERT), BF16), pltpu.VMEM((D_EXPERT, D_MODEL), BF16),
                            pltpu.SemaphoreType.DMA((2,))]),
        compiler_params=_params(1),
        name="moe_experts",
    )(block_e, n_used, next_e, xs, w_gu, b_gu, w_down, b_down)


def _combine_kernel(dest_ref, x_ref, g2_ref, w_ref, eo_ref, yc_ref, yl_ref, buf, sem):
    i = pl.program_id(0)
    cur = i % 2

    def issue_tile(tile, slot):
        base = tile * TM * TOP_K
        row0 = slot * (TOP_K * TM)

        def issue(t, c):
            for kk in range(TOP_K):
                pltpu.make_async_copy(eo_ref.at[dest_ref[base + t * TOP_K + kk]], buf.at[row0 + kk * TM + t],
                                      sem.at[slot]).start(priority=kk % 2)
            return c
        lax.fori_loop(0, TM, issue, 0, unroll=8)

    @pl.when(i == 0)
    def _():
        issue_tile(0, 0)

    @pl.when(i + 1 < N_TILES)
    def _():
        issue_tile(i + 1, 1 - cur)

    for kk in range(TOP_K):
        pltpu.make_async_copy(eo_ref.at[pl.ds(0, TM)], buf.at[pl.ds(0, TM)], sem.at[cur]).wait()
    w = w_ref[...]
    acc = None
    for kk in range(TOP_K):
        wk = jnp.broadcast_to(w[:, kk:kk + 1], (TM, 128)).reshape(TM // 8, 8, 128)
        wt = jnp.stack([jnp.broadcast_to(wk[:, r:r + 1, :], (TM // 8, 8, 128)) for r in range(8)], axis=1)
        term = buf[pl.ds(pl.multiple_of(cur * (TOP_K * TM) + kk * TM, TM), TM)] * wt.reshape(TM, 8, 128)
        acc = term if acc is None else acc + term
    ff = _tiles_to_rows(acc)
    y = x_ref[...] + g2_ref[0] * ff
    is_ctx = i < N_CTX_TILES

    @pl.when(is_ctx)
    def _():
        yc_ref[...] = y

    @pl.when(jnp.logical_not(is_ctx))
    def _():
        yl_ref[...] = y


def _moe_combine(dest, x1, modr, top_w, eo, layer):
    mrow = _mod_row(TM)
    base = layer * N_MOD
    return pl.pallas_call(
        _combine_kernel,
        out_shape=[jax.ShapeDtypeStruct((T_CTX, D_MODEL), F32), jax.ShapeDtypeStruct((T_LAT, D_MODEL), F32)],
        grid_spec=pltpu.PrefetchScalarGridSpec(
            num_scalar_prefetch=1, grid=(N_TILES,),
            in_specs=[pl.BlockSpec((TM, D_MODEL), lambda i, d: (i, 0)),
                      pl.BlockSpec((1, 1, D_MODEL), lambda i, d: ((base + mrow(i)) * 6 + 5, 0, 0)),
                      pl.BlockSpec((TM, 128), lambda i, d: (i, 0)),
                      pl.BlockSpec(memory_space=pl.ANY)],
            out_specs=[pl.BlockSpec((TM, D_MODEL), lambda i, d: (jnp.minimum(i, N_CTX_TILES - 1), 0)),
                       pl.BlockSpec((TM, D_MODEL), lambda i, d: (jnp.maximum(i - N_CTX_TILES, 0), 0))],
            scratch_shapes=[pltpu.VMEM((2 * TOP_K * TM,) + ROW_TILE, F32), pltpu.SemaphoreType.DMA((2,))]),
        compiler_params=_params(1),
        name="moe_combine",
    )(dest, x1, modr, top_w, eo)


def _pad_heads(w, n_heads, width):
    lead = w.shape[:-1]
    w = w.reshape(lead + (n_heads, width))
    w = jnp.pad(w, [(0, 0)] * len(lead) + [(0, 0), (0, HEAD_PAD - width)])
    return w.reshape(lead + (n_heads * HEAD_PAD,))


def _permute_w_in_kernel(w_ref, o_ref):
    w = w_ref[0]
    o_ref[0] = jnp.concatenate([w[:, :3072], w[:, 3520:6592], w[:, 3104:3520], w[:, 3072:3104], w[:, 6592:]],
                               axis=1)


def _permute_w_in(w_in):
    rows = 128
    w_pad = jnp.pad(w_in, ((0, 0), (0, 0), (0, DZ - w_in.shape[2]))).astype(BF16)
    return pl.pallas_call(
        _permute_w_in_kernel,
        out_shape=jax.ShapeDtypeStruct((DEPTH, D_MODEL, DZ), BF16),
        grid=(DEPTH, D_MODEL // rows),
        in_specs=[pl.BlockSpec((1, rows, DZ), lambda l, r: (l, r, 0))],
        out_specs=pl.BlockSpec((1, rows, DZ), lambda l, r: (l, r, 0)),
        compiler_params=_params(2),
        name="permute_w_in",
    )(w_pad)


def _layer_weights(l, gla_wa2, gla_ba, mla_qa_g, mla_wuq, mla_kva_g, mla_wukv, mla_qn_g, mla_kn_g,
                   w_branch, w_out, router_w, router_b, norm2_g):
    wa_p = jnp.zeros((512, 512), F32)
    wa_p = wa_p.at[416:432, 0:256].set(gla_wa2[l, 0]).at[432:448, 256:512].set(gla_wa2[l, 1]).astype(BF16)
    ba_p = gla_ba[l].reshape(1, 512)
    wukv = mla_wukv[l].reshape(128, MLA_HEADS, MLA_NOPE + MLA_V)
    rw = jnp.pad(router_w[l], ((0, 0), (0, 128 - N_EXPERTS)))
    rw_hi = rw.astype(BF16)
    return {
        "wa": wa_p, "ba": ba_p,
        "qa_g": mla_qa_g[l].reshape(1, 256),
        "wuq": _pad_heads(mla_wuq[l], MLA_HEADS, MLA_QK).astype(BF16),
        "qn_g": jnp.tile(jnp.pad(mla_qn_g[l], (0, HEAD_PAD - MLA_QK)), MLA_HEADS).reshape(1, 1024) * MLA_QK ** -0.5,
        "kn_g": jnp.tile(jnp.pad(mla_kn_g[l], (0, HEAD_PAD - MLA_QK)), MLA_HEADS).reshape(1, 1024),
        "kva_g": mla_kva_g[l].reshape(1, 128),
        "wk": _pad_heads(wukv[:, :, :MLA_NOPE].reshape(128, MLA_HEADS * MLA_NOPE), MLA_HEADS, MLA_NOPE).astype(BF16),
        "wv": wukv[:, :, MLA_NOPE:].reshape(128, MLA_HEADS * MLA_V).astype(BF16),
        "wb": w_branch[l].astype(BF16), "wo": w_out[l].astype(BF16),
        "rw_hi": rw_hi, "rw_lo": (rw - rw_hi.astype(F32)).astype(BF16),
        "rb": jnp.pad(router_b[l], (0, 128 - N_EXPERTS)).reshape(1, 128),
        "n2_g": norm2_g[l].reshape(1, D_MODEL),
    }


def kernel(x_prompt, x_sample, cache_mla_ckv, cache_mla_krope, state_ret, state_gla, c, c_ctx, w_mod, b_mod, norm1_g, norm2_g, w_in, ret_gn_g, gla_wa2, gla_ba, gla_norm_g, mla_qa_g, mla_wuq, mla_kva_g, mla_wukv, mla_qn_g, mla_kn_g, w_branch, w_out, router_w, router_b, moe_w_gu, moe_b_gu, moe_w_down, moe_b_down):
    xc, xl = x_prompt.reshape(T_CTX, D_MODEL), x_sample.reshape(T_LAT, D_MODEL)
    cc = jnp.concatenate([c_ctx[None, :], c, jnp.zeros((N_MOD - 1 - N_LAT_SEQ, D_MODEL), F32)], axis=0)
    modr = _modulation(cc, w_mod, b_mod).reshape(DEPTH * N_MOD * 6, 1, D_MODEL)
    rope_tabs = _rope_tables()
    w_in_p = _permute_w_in(w_in)
    w_gu = moe_w_gu.reshape(DEPTH * N_EXPERTS, D_MODEL, 2 * D_EXPERT)
    b_gu = moe_b_gu.reshape(DEPTH * N_EXPERTS, 1, 2 * D_EXPERT)
    w_dn = moe_w_down.reshape(DEPTH * N_EXPERTS, D_EXPERT, D_MODEL)
    b_dn = moe_b_down.reshape(DEPTH * N_EXPERTS, 1, D_MODEL)

    ckv_l, krope_l, ret_l, gla_l = [], [], [], []
    for l in range(DEPTH):
        w = _layer_weights(l, gla_wa2, gla_ba, mla_qa_g, mla_wuq, mla_kva_g, mla_wukv, mla_qn_g, mla_kn_g,
                           w_branch, w_out, router_w, router_b, norm2_g)
        z = _in_proj(xc, xl, norm1_g[l].reshape(1, D_MODEL), modr, w_in_p, l)

        gn = ret_gn_g[l].reshape(1, 512)
        ret_c, ret_state = _retention(z, gn, None, ctx=True)
        (ret_s,) = _retention(z, gn, state_ret[:, l], ctx=False)
        gng = gla_norm_g[l].reshape(1, 512)
        gla_c, gla_state = _gla(z, w["wa"], w["ba"], gng, None, ctx=True)
        gla_s, _ = _gla(z, w["wa"], w["ba"], gng, state_gla[:, l], ctx=False)

        q, k, v, ckv, kr = _mla_prep(z, w, rope_tabs)
        kc, vc = _mla_cache(cache_mla_ckv[:, l].reshape(N_LAT_SEQ * CTX_LEN, 128),
                            jnp.pad(cache_mla_krope[:, l].reshape(N_LAT_SEQ * CTX_LEN, MLA_ROPE),
                                    ((0, 0), (0, 128 - MLA_ROPE))), w)
        mla_c = _mla_attn(q, k, v, None, None, ctx=True)
        mla_s = _mla_attn(q, k, v, kc, vc, ctx=False)

        x1, h2, top_idx, top_w, rank, cnt = _merge((ret_c, ret_s, gla_c, gla_s, mla_c, mla_s), z, xc, xl, modr, w, l)
        dest, tail_start, block_e, n_used, next_e = _route(top_idx[:, :TOP_K], rank[:, :TOP_K],
                                                           cnt[0, :N_EXPERTS].astype(jnp.int32))
        xs = _moe_dispatch(h2, dest, tail_start, n_used)
        eo = _moe_experts(xs, block_e, n_used, next_e, w_gu, b_gu, w_dn, b_dn, l)
        xc, xl = _moe_combine(dest, x1, modr, top_w, eo, l)

        ckv_l.append(ckv[:T_CTX].reshape(N_CTX_SEQ, CTX_LEN, 128))
        krope_l.append(kr[:T_CTX, :MLA_ROPE].reshape(N_CTX_SEQ, CTX_LEN, MLA_ROPE))
        ret_l.append(ret_state)
        gla_l.append(gla_state.reshape(N_CTX_SEQ, 2, N_HEADS, DK, DV))

    y_p = xc.reshape(N_CTX_SEQ, CTX_LEN, D_MODEL)
    y_s = xl.reshape(N_LAT_SEQ, LAT_LEN, D_MODEL)
    return (y_p, y_s, jnp.stack(ckv_l, axis=1), jnp.stack(krope_l, axis=1),
            jnp.stack(ret_l, axis=1), jnp.stack(gla_l, axis=1))
```

```python
import functools

import jax
import jax.numpy as jnp
import numpy as np
from jax import lax
from jax.experimental import pallas as pl
from jax.experimental.pallas import tpu as pltpu

F32 = jnp.float32
BF16 = jnp.bfloat16

D_MODEL = 1024
DEPTH = 2
N_CTX_SEQ, CTX_LEN = 32, 256
N_LAT_SEQ, LAT_LEN = 4, 1024
T_CTX = N_CTX_SEQ * CTX_LEN
T_LAT = N_LAT_SEQ * LAT_LEN
T_ALL = T_CTX + T_LAT
TM = 256
N_TILES = T_ALL // TM
N_CTX_TILES = T_CTX // TM
LAT_TILES = LAT_LEN // TM
N_MOD = 8
EPS = 1e-6

N_HEADS = 4
DK, DV = 64, 128
GRID_W = 64
MLA_HEADS, MLA_NOPE, MLA_ROPE, MLA_V = 8, 64, 32, 64
MLA_QK = MLA_NOPE + MLA_ROPE
HEAD_PAD = 128
GLA_TAU = 16.0
N_EXPERTS, TOP_K, D_EXPERT = 32, 4, 1024
SWIGLU_LIMIT, SWIGLU_ALPHA = 7.0, 1.702
MOE_ROWS = 512
N_SLOTS = T_ALL * TOP_K
N_MOE_BLOCKS = N_SLOTS // MOE_ROWS + N_EXPERTS
N_MOE_ROWS = N_MOE_BLOCKS * MOE_ROWS
MERGE_TM = 512
CTX_GROUP = 4
ROUTE_CHUNKS = 8
XS_EXTRA = MOE_ROWS
ROW_TILE = (D_MODEL // 128, 128)

DZ = 6656
IN_TILE = 512
VMEM_LIMIT = 56 * 1024 * 1024

RET_LOG_F = [float(np.log1p(-np.exp2(-(5.0 + h)))) for h in range(N_HEADS)]
RET_LOG_B = [float(np.log1p(-np.exp2(-(5.5 + h)))) for h in range(N_HEADS)]


def _params(n_axes, vmem=VMEM_LIMIT):
    return pltpu.CompilerParams(dimension_semantics=("arbitrary",) * n_axes, vmem_limit_bytes=vmem)


def _sigmoid(x):
    return 1.0 / (1.0 + jnp.exp(-x))


def _dot(a, b):
    return jnp.dot(a, b, preferred_element_type=F32)


def _dot_t(a, b):
    return lax.dot_general(a, b, (((1,), (1,)), ((), ())), preferred_element_type=F32)


def _mod_row(tile_rows):
    def f(i):
        r0 = i * tile_rows
        return jnp.where(r0 < T_CTX, 0, 1 + (r0 - T_CTX) // LAT_LEN)
    return f


def _ctx_spec(rows, width):
    n_ctx = T_CTX // rows
    return pl.BlockSpec((rows, width), lambda i: (jnp.minimum(i, n_ctx - 1), 0))


def _lat_spec(rows, width):
    n_ctx = T_CTX // rows
    return pl.BlockSpec((rows, width), lambda i: (jnp.maximum(i - n_ctx, 0), 0))


def _mod_kernel(c_ref, w_ref, b_ref, o_ref):
    c = c_ref[...]
    s = c * _sigmoid(c)
    sh = s.astype(BF16)
    sl = (s - sh.astype(F32)).astype(BF16)
    w = w_ref[0]
    wh = w.astype(BF16)
    wl = (w - wh.astype(F32)).astype(BF16)
    o_ref[0] = _dot(sh, wh) + _dot(sh, wl) + _dot(sl, wh) + b_ref[0]


def _modulation(cc, w_mod, b_mod):
    n = 6 * D_MODEL
    blk = 2048
    return pl.pallas_call(
        _mod_kernel,
        out_shape=jax.ShapeDtypeStruct((DEPTH, N_MOD, n), F32),
        grid=(DEPTH, n // blk),
        in_specs=[pl.BlockSpec((N_MOD, D_MODEL), lambda l, j: (0, 0)),
                  pl.BlockSpec((1, D_MODEL, blk), lambda l, j: (l, 0, j)),
                  pl.BlockSpec((1, 1, blk), lambda l, j: (l, 0, j))],
        out_specs=pl.BlockSpec((1, N_MOD, blk), lambda l, j: (l, 0, j)),
        compiler_params=_params(2),
        name="modulation",
    )(cc, w_mod, b_mod.reshape(DEPTH, 1, n))


def _in_kernel(xc_ref, xl_ref, g_ref, sh_ref, sc_ref, w_ref, o_ref):
    x = jnp.where(pl.program_id(0) < T_CTX // IN_TILE, xc_ref[...], xl_ref[...])
    h = x * lax.rsqrt(jnp.mean(x * x, axis=-1, keepdims=True) + EPS) * g_ref[...]
    h = h * (1.0 + sc_ref[0]) + sh_ref[0]
    hb = h.astype(BF16)
    for n0 in range(0, DZ, 512):
        o_ref[:, n0:n0 + 512] = _dot(hb, w_ref[0, :, n0:n0 + 512]).astype(BF16)


def _in_proj(xc, xl, g, modr, w_in_p, layer):
    mrow = _mod_row(IN_TILE)
    base = layer * N_MOD

    def mod_spec(part):
        return pl.BlockSpec((1, 1, D_MODEL), lambda i: ((base + mrow(i)) * 6 + part, 0, 0))

    return pl.pallas_call(
        _in_kernel,
        out_shape=jax.ShapeDtypeStruct((T_ALL, DZ), BF16),
        grid=(T_ALL // IN_TILE,),
        in_specs=[_ctx_spec(IN_TILE, D_MODEL), _lat_spec(IN_TILE, D_MODEL),
                  pl.BlockSpec((1, D_MODEL), lambda i: (0, 0)),
                  mod_spec(0), mod_spec(1),
                  pl.BlockSpec((1, D_MODEL, DZ), lambda i: (layer, 0, 0))],
        out_specs=pl.BlockSpec((IN_TILE, DZ), lambda i: (i, 0)),
        compiler_params=_params(1),
        name="in_proj",
    )(xc, xl, g, modr, modr, w_in_p)


def _lane_half_mask(hh):
    lane = lax.broadcasted_iota(jnp.int32, (1, 128), 1)
    return (lane < 64) if hh == 0 else (lane >= 64)


@functools.lru_cache(maxsize=None)
def _ret_decay_table(seq):
    d = np.arange(seq)[:, None] - np.arange(seq)[None, :]
    tab = np.stack([np.exp(np.where(d > 0, RET_LOG_F[h] * d, -RET_LOG_B[h] * d)) for h in range(N_HEADS)])
    return (tab * np.where(d == 0, 2.0, 1.0) * DK ** -0.5).astype(np.float32)


def _ret_kernel(*refs, seq, has_state, emit_state):
    q_ref, k_ref, v_ref, g_ref, gn_ref, dec_ref = refs[:6]
    pos = 6
    if has_state:
        s0_ref = refs[pos]
        pos += 1
    o_ref = refs[pos]
    pos += 1
    if emit_state:
        st_ref = refs[pos]

    r0 = pl.multiple_of(pl.program_id(1) * TM, TM)
    qb = q_ref[pl.ds(r0, TM), :]
    ri = (lax.broadcasted_iota(jnp.int32, (TM, 1), 0) + r0).astype(F32)

    for h in range(N_HEADS):
        p, hh = h // 2, h % 2
        lanes = slice(128 * p, 128 * p + 128)
        qp = qb[:, lanes]
        qh = jnp.where(_lane_half_mask(hh), qp, jnp.zeros_like(qp))
        sc = _dot_t(qh, k_ref[:, lanes])
        o = _dot((sc * dec_ref[h]).astype(BF16), v_ref[:, 128 * h:128 * h + 128])
        if has_state:
            qf = qh.astype(F32)
            o += _dot((qf * jnp.exp(RET_LOG_F[h] * (ri + 1.0))).astype(BF16), s0_ref[0, 0, p].astype(BF16))
            o += _dot((qf * jnp.exp(RET_LOG_B[h] * (seq - ri))).astype(BF16), s0_ref[0, 1, p].astype(BF16))
        mu = jnp.mean(o, axis=-1, keepdims=True)
        d = o - mu
        var = jnp.mean(d * d, axis=-1, keepdims=True)
        on = d * lax.rsqrt(var + EPS)
        g = g_ref[:, 128 * h:128 * h + 128].astype(F32)
        out = on * gn_ref[:, 128 * h:128 * h + 128] * (g * _sigmoid(g))
        o_ref[:, 128 * h:128 * h + 128] = out.astype(BF16)

    if emit_state:
        jc = lax.broadcasted_iota(jnp.int32, (seq, 1), 0).astype(F32)
        lane = lax.broadcasted_iota(jnp.int32, (1, 128), 1)
        for p in range(2):
            kp = k_ref[:, 128 * p:128 * p + 128].astype(F32) * DK ** -0.5
            lgf = jnp.where(lane < 64, RET_LOG_F[2 * p], RET_LOG_F[2 * p + 1])
            lgb = jnp.where(lane < 64, RET_LOG_B[2 * p], RET_LOG_B[2 * p + 1])
            kdf = (kp * jnp.exp(lgf * (seq - 1.0 - jc))).T.astype(BF16)
            kdb = (kp * jnp.exp(lgb * jc)).T.astype(BF16)
            for hh in range(2):
                h = 2 * p + hh
                vh = v_ref[:, 128 * h:128 * h + 128]
                st_ref[0, 0, h] = _dot(kdf, vh)[64 * hh:64 * hh + 64, :]
                st_ref[0, 1, h] = _dot(kdb, vh)[64 * hh:64 * hh + 64, :]


def _retention(z, gn_g, s0, *, ctx):
    if ctx:
        nb, seq, row_blk, tile0 = N_CTX_SEQ, CTX_LEN, 0, 0
    else:
        nb, seq, row_blk, tile0 = N_LAT_SEQ, LAT_LEN, T_CTX // LAT_LEN, N_CTX_TILES
    nq = seq // TM
    in_specs = [pl.BlockSpec((seq, 256), lambda b, i: (row_blk + b, 0)),
                pl.BlockSpec((seq, 256), lambda b, i: (row_blk + b, 1)),
                pl.BlockSpec((seq, 512), lambda b, i: (row_blk + b, 1)),
                pl.BlockSpec((TM, 512), lambda b, i: (tile0 + b * nq + i, 2)),
                pl.BlockSpec((1, 512), lambda b, i: (0, 0)),
                pl.BlockSpec((N_HEADS, TM, seq), lambda b, i: (0, i, 0))]
    args = [z, z, z, z, gn_g, _ret_decay_table(seq)]
    out_shape = [jax.ShapeDtypeStruct((nb * seq, 512), BF16)]
    out_specs = [pl.BlockSpec((TM, 512), lambda b, i: (b * nq + i, 0))]
    if not ctx:
        in_specs.append(pl.BlockSpec((1, 2, 2, 128, 128), lambda b, i: (b, 0, 0, 0, 0)))
        args.append(s0.reshape(N_LAT_SEQ, 2, 2, 128, 128))
    else:
        out_shape.append(jax.ShapeDtypeStruct((nb, 2, N_HEADS, DK, DV), F32))
        out_specs.append(pl.BlockSpec((1, 2, N_HEADS, DK, DV), lambda b, i: (b, 0, 0, 0, 0)))
    return pl.pallas_call(
        functools.partial(_ret_kernel, seq=seq, has_state=not ctx, emit_state=ctx),
        out_shape=out_shape, grid=(nb, nq), in_specs=in_specs, out_specs=out_specs,
        compiler_params=_params(2),
        name="retention_ctx" if ctx else "retention_lat",
    )(*args)


def _gla_decay(small_ref, wa_ref, ba_ref):
    x = _dot(small_ref[...], wa_ref[...]) + ba_ref[...]
    la = -(jnp.maximum(-x, 0.0) + jnp.log(1.0 + jnp.exp(-jnp.abs(x)))) * (1.0 / GLA_TAU)
    ri = lax.broadcasted_iota(jnp.int32, (TM, TM), 0)
    ci = lax.broadcasted_iota(jnp.int32, (TM, TM), 1)
    ltri = jnp.where(ri >= ci, 1.0, 0.0).astype(BF16)
    hi = la.astype(BF16)
    r1 = la - hi.astype(F32)
    mid = r1.astype(BF16)
    lo = (r1 - mid.astype(F32)).astype(BF16)
    cum = _dot(ltri, hi) + _dot(ltri, mid) + _dot(ltri, lo)
    return la, cum


def _gla_state_kernel(k_ref, v_ref, small_ref, wa_ref, ba_ref, kv_ref, tot_ref):
    la, cum = _gla_decay(small_ref, wa_ref, ba_ref)
    bf, bb = cum[:, :256], cum[:, 256:]
    xb = bb - la[:, 256:]
    k = k_ref[...].astype(F32)
    kdf = k * jnp.exp(bf[TM - 1:TM, :] - bf)
    kdb = k * jnp.exp(xb)
    for p in range(2):
        kf_t = kdf[:, 128 * p:128 * p + 128].T.astype(BF16)
        kb_t = kdb[:, 128 * p:128 * p + 128].T.astype(BF16)
        for hh in range(2):
            h = 2 * p + hh
            vh = v_ref[:, 128 * h:128 * h + 128]
            kv_ref[0, 0, 0, h] = _dot(kf_t, vh)[64 * hh:64 * hh + 64, :]
            kv_ref[0, 0, 1, h] = _dot(kb_t, vh)[64 * hh:64 * hh + 64, :]
    tot_ref[0, 0] = jnp.sum(la.T, axis=-1, keepdims=True)


def _mid_bcast(x, s, r):
    w = 2 * s
    if w >= 8:
        n = TM // w
        x3 = x.reshape(n, w, 256)
        return jnp.broadcast_to(x3[:, r:r + 1, :], (n, w, 256)).reshape(TM, 256)
    x3 = x.reshape(TM // 8, 8, 256)
    sub = lax.broadcasted_iota(jnp.int32, (1, 8, 1), 1)
    out = None
    for blk in range(8 // w):
        rowv = jnp.broadcast_to(x3[:, blk * w + r:blk * w + r + 1, :], (TM // 8, 8, 256))
        out = rowv if out is None else jnp.where(sub >= blk * w, rowv, out)
    return out.reshape(TM, 256)


def _gla_kernel(*refs, n_blk, has_state, emit_state):
    q_ref, k_ref, v_ref, g_ref, small_ref, wa_ref, ba_ref, gn_ref = refs[:8]
    pos = 8
    if has_state:
        kv_ref, tot_ref, s0_ref = refs[pos:pos + 3]
        pos += 3
    o_ref = refs[pos]
    pos += 1
    if emit_state:
        kvo_ref = refs[pos]

    la, cum = _gla_decay(small_ref, wa_ref, ba_ref)
    bf, bb = cum[:, :256], cum[:, 256:]
    xb = bb - la[:, 256:]
    q = q_ref[...].astype(F32) * DK ** -0.5
    k = k_ref[...].astype(F32)
    row = lax.broadcasted_iota(jnp.int32, (TM, 1), 0)
    rowi = lax.broadcasted_iota(jnp.int32, (TM, TM), 0)
    colj = lax.broadcasted_iota(jnp.int32, (TM, TM), 1)
    low_half = _lane_half_mask(0)

    def join(fwd, bwd):
        ops = []
        for p in range(N_HEADS // 2):
            f = fwd[:, 128 * p:128 * p + 128]
            br = pltpu.roll(bwd[:, 128 * p:128 * p + 128], 64, 1)
            ops.append(jnp.where(low_half, f, br).astype(BF16))
            ops.append(jnp.where(low_half, br, f).astype(BF16))
        return ops

    qd, kd = join(q, q), join(k, k)
    acc = [jnp.where(rowi == colj, _dot_t(qo, ko), 0.0) for qo, ko in zip(qd, kd)]

    low_f = jnp.where(low_half, 1.0, 0.0)
    s = 1
    while s < TM:
        up_f = jnp.where(((row // s) % 2) == 1, 1.0, 0.0)
        live_even = jnp.where(up_f == low_f, 1.0, 0.0).astype(BF16)
        live = [live_even, 1.0 - live_even]
        dead = [live[1], live[0]]
        af = -jnp.abs(bf - _mid_bcast(bf, s, s - 1))
        ab = -jnp.abs(xb - _mid_bcast(xb, s, s))
        same = (rowi // (2 * s)) == (colj // (2 * s))
        for p in range(N_HEADS // 2):
            a_f = af[:, 128 * p:128 * p + 128]
            a_b = pltpu.roll(ab[:, 128 * p:128 * p + 128], 64, 1)
            for hh in range(2):
                h = 2 * p + hh
                arg = jnp.where(low_half, a_f, a_b) if hh == 0 else jnp.where(low_half, a_b, a_f)
                e = jnp.exp(arg).astype(BF16)
                sl = _dot_t(e * (qd[h] * live[hh]), e * (kd[h] * dead[hh]))
                acc[h] = acc[h] + (jnp.where(same, sl, 0.0) if 2 * s < TM else sl)
        s *= 2

    if has_state:
        n = pl.program_id(1)
        q_state = join(q * jnp.exp(bf), q * jnp.exp(bb[TM - 1:TM, :] - xb))

    if emit_state:
        kdf = k * jnp.exp(bf[TM - 1:TM, :] - bf)
        kdb = k * jnp.exp(xb)
        for p in range(N_HEADS // 2):
            kf_t = kdf[:, 128 * p:128 * p + 128].T.astype(BF16)
            kb_t = kdb[:, 128 * p:128 * p + 128].T.astype(BF16)
            for hh in range(2):
                h = 2 * p + hh
                vh = v_ref[:, 128 * h:128 * h + 128]
                kvo_ref[0, 0, 0, h] = _dot(kf_t, vh)[64 * hh:64 * hh + 64, :]
                kvo_ref[0, 0, 1, h] = _dot(kb_t, vh)[64 * hh:64 * hh + 64, :]

    for h in range(N_HEADS):
        o = _dot(acc[h].astype(BF16), v_ref[:, 128 * h:128 * h + 128])
        if has_state:
            sf = s0_ref[0, 0, h]
            for m in range(n_blk - 1):
                dec = jnp.exp(tot_ref[0, m, 64 * h:64 * h + 64, :])
                sf = jnp.where(m < n, dec * sf + kv_ref[0, m, 0, h], sf)
            sb = s0_ref[0, 1, h]
            for m in range(n_blk - 1, 0, -1):
                dec = jnp.exp(tot_ref[0, m, 256 + 64 * h:256 + 64 * h + 64, :])
                sb = jnp.where(m > n, dec * sb + kv_ref[0, m, 1, h], sb)
            state = jnp.concatenate([sf, sb] if h % 2 == 0 else [sb, sf], axis=0).astype(BF16)
            o += _dot(q_state[h], state)
        on = o * lax.rsqrt(jnp.mean(o * o, axis=-1, keepdims=True) + EPS)
        g = g_ref[:, 128 * h:128 * h + 128].astype(F32)
        out = on * gn_ref[:, 128 * h:128 * h + 128] * (g * _sigmoid(g))
        o_ref[:, 128 * h:128 * h + 128] = out.astype(BF16)


def _gla(z, wa_p, ba_p, gn_g, s0, *, ctx):
    if ctx:
        nb, n_blk, tile0 = N_CTX_SEQ, 1, 0
    else:
        nb, n_blk, tile0 = N_LAT_SEQ, LAT_TILES, N_CTX_TILES

    def zspec(width, col):
        return pl.BlockSpec((TM, width), lambda b, n: (tile0 + b * n_blk + n, col))

    w_specs = [pl.BlockSpec((512, 512), lambda b, n: (0, 0)), pl.BlockSpec((1, 512), lambda b, n: (0, 0))]
    kv_shape = jax.ShapeDtypeStruct((nb, n_blk, 2, N_HEADS, DK, DV), F32)
    kv_spec = pl.BlockSpec((1, 1, 2, N_HEADS, DK, DV), lambda b, n: (b, n, 0, 0, 0, 0))
    in_specs = [zspec(256, 6), zspec(256, 7), zspec(512, 4), zspec(512, 5), zspec(512, 12)] + w_specs
    in_specs.append(pl.BlockSpec((1, 512), lambda b, n: (0, 0)))
    args = [z, z, z, z, z, wa_p, ba_p, gn_g]
    out_shape = [jax.ShapeDtypeStruct((nb * n_blk * TM, 512), BF16)]
    out_specs = [pl.BlockSpec((TM, 512), lambda b, n: (b * n_blk + n, 0))]
    if ctx:
        out_shape.append(kv_shape)
        out_specs.append(kv_spec)
    else:
        kv, tot = pl.pallas_call(
            _gla_state_kernel,
            out_shape=[kv_shape, jax.ShapeDtypeStruct((nb, n_blk, 512, 1), F32)],
            grid=(nb, n_blk),
            in_specs=[zspec(256, 7), zspec(512, 4), zspec(512, 12)] + w_specs,
            out_specs=[kv_spec, pl.BlockSpec((1, 1, 512, 1), lambda b, n: (b, n, 0, 0))],
            compiler_params=_params(2),
            name="gla_state_lat",
        )(z, z, z, wa_p, ba_p)
        in_specs += [pl.BlockSpec((1, n_blk, 2, N_HEADS, DK, DV), lambda b, n: (b, 0, 0, 0, 0, 0)),
                     pl.BlockSpec((1, n_blk, 512, 1), lambda b, n: (b, 0, 0, 0)),
                     pl.BlockSpec((1, 2, N_HEADS, DK, DV), lambda b, n: (b, 0, 0, 0, 0))]
        args += [kv, tot, s0]
    res = pl.pallas_call(
        functools.partial(_gla_kernel, n_blk=n_blk, has_state=not ctx, emit_state=ctx),
        out_shape=out_shape, grid=(nb, n_blk), in_specs=in_specs, out_specs=out_specs,
        compiler_params=_params(2),
        name="gla_ctx" if ctx else "gla_lat",
    )(*args)
    return (res[0], res[1]) if ctx else (res[0], None)


def _rope_tables():
    nf = MLA_ROPE // 4
    pos = np.arange(LAT_LEN)
    freqs = (10000.0 ** (-np.arange(nf, dtype=np.float32) / nf)).astype(np.float32)
    ang_r = ((pos // GRID_W).astype(np.float32)[:, None] * freqs).astype(np.float32)
    ang_c = ((pos % GRID_W).astype(np.float32)[:, None] * freqs).astype(np.float32)
    cos = np.ones((TM + LAT_LEN, HEAD_PAD), np.float32)
    sa = np.zeros((TM + LAT_LEN, HEAD_PAD), np.float32)
    sb = np.zeros((TM + LAT_LEN, HEAD_PAD), np.float32)
    o = MLA_NOPE
    for base, ang in ((o, ang_r), (o + 2 * nf, ang_c)):
        cos[TM:, base:base + nf] = np.cos(ang)
        cos[TM:, base + nf:base + 2 * nf] = np.cos(ang)
        sa[TM:, base:base + nf] = -np.sin(ang)
        sb[TM:, base + nf:base + 2 * nf] = np.sin(ang)
    return jnp.asarray(cos), jnp.asarray(sa), jnp.asarray(sb)


def _rope(x, cos, sa, sb):
    return x * cos + pltpu.roll(x, 128 - 8, 1) * sa + pltpu.roll(x, 8, 1) * sb


def _head_segments():
    seg = np.zeros((MLA_HEADS * HEAD_PAD, 128), np.float32)
    for h in range(MLA_HEADS):
        seg[h * HEAD_PAD:(h + 1) * HEAD_PAD, h] = 1.0
    return jnp.asarray(seg, BF16), jnp.asarray(seg.T.copy(), BF16)


def _head_norm(x, gain, seg_ref, segt_ref, lane_broadcast):
    ss = _dot((x * x).astype(BF16), seg_ref[...])
    rs = lax.rsqrt(ss * (1.0 / MLA_QK) + EPS)
    if lane_broadcast:
        return jnp.concatenate([x[:, 128 * h:128 * h + 128] * rs[:, h:h + 1] for h in range(MLA_HEADS)],
                               axis=1) * gain
    hi = rs.astype(BF16)
    lo = (rs - hi.astype(F32)).astype(BF16)
    return x * (_dot(hi, segt_ref[...]) + _dot(lo, segt_ref[...])) * gain


def _store_heads(x, o_ref, rope):
    if rope is None:
        o_ref[...] = x.astype(BF16)
    else:
        for h in range(MLA_HEADS):
            o_ref[:, 128 * h:128 * h + 128] = _rope(x[:, 128 * h:128 * h + 128], *rope).astype(BF16)


def _mla_keys(ckv, kr_tile, wk_ref, wv_ref, kn_ref, seg_ref, segt_ref, k_ref, v_ref, rope=None):
    cb = ckv.astype(BF16)
    kpre = _dot(cb, wk_ref[...]) + jnp.concatenate([kr_tile] * MLA_HEADS, axis=1)
    v_ref[...] = _dot(cb, wv_ref[...]).astype(BF16)
    _store_heads(_head_norm(kpre, kn_ref[...], seg_ref, segt_ref, rope is None), k_ref, rope)


def _mla_prep_kernel(small_ref, qa_ref, wuq_ref, qn_ref, kva_ref, wk_ref, wv_ref, kn_ref, seg_ref, segt_ref,
                     cos_ref, sa_ref, sb_ref, q_ref, k_ref, v_ref, ckv_ref, kr_ref):
    def body(rope):
        sm = small_ref[...].astype(F32)
        cq, ckv_raw, g3 = sm[:, 0:256], sm[:, 256:384], sm[:, 384:512]
        cqn = cq * lax.rsqrt(jnp.mean(cq * cq, axis=-1, keepdims=True) + EPS) * qa_ref[...]
        q = _dot(cqn.astype(BF16), wuq_ref[...])
        _store_heads(_head_norm(q, qn_ref[...], seg_ref, segt_ref, rope is None), q_ref, rope)
        ckv = ckv_raw * lax.rsqrt(jnp.mean(ckv_raw * ckv_raw, axis=-1, keepdims=True) + EPS) * kva_ref[...]
        ckv_ref[...] = ckv
        lane = lax.broadcasted_iota(jnp.int32, (1, 128), 1)
        kr = jnp.where(lane < MLA_ROPE, g3, 0.0)
        kr_ref[...] = kr
        _mla_keys(ckv, pltpu.roll(kr, MLA_NOPE, 1), wk_ref, wv_ref, kn_ref, seg_ref, segt_ref, k_ref, v_ref, rope)

    is_ctx = pl.program_id(0) < N_CTX_TILES

    @pl.when(is_ctx)
    def _():
        body(None)

    @pl.when(jnp.logical_not(is_ctx))
    def _():
        body((cos_ref[...], sa_ref[...], sb_ref[...]))


def _mla_cache_kernel(ckv_ref, kr_ref, wk_ref, wv_ref, kn_ref, seg_ref, segt_ref, k_ref, v_ref):
    _mla_keys(ckv_ref[...], pltpu.roll(kr_ref[...], MLA_NOPE, 1), wk_ref, wv_ref, kn_ref, seg_ref, segt_ref,
              k_ref, v_ref)


def _mla_prep(z, w, rope_tabs):
    def rope_blk(i):
        return jnp.where(i < N_CTX_TILES, 0, 1 + (i - N_CTX_TILES) % LAT_TILES)

    const = lambda shape: pl.BlockSpec(shape, lambda i: (0,) * len(shape))
    rope_spec = pl.BlockSpec((TM, HEAD_PAD), lambda i: (rope_blk(i), 0))
    row = lambda width: pl.BlockSpec((TM, width), lambda i: (i, 0))
    return pl.pallas_call(
        _mla_prep_kernel,
        out_shape=[jax.ShapeDtypeStruct((T_ALL, 1024), BF16), jax.ShapeDtypeStruct((T_ALL, 1024), BF16),
                   jax.ShapeDtypeStruct((T_ALL, 512), BF16), jax.ShapeDtypeStruct((T_ALL, 128), F32),
                   jax.ShapeDtypeStruct((T_ALL, 128), F32)],
        grid=(N_TILES,),
        in_specs=[pl.BlockSpec((TM, 512), lambda i: (i, 12)),
                  const((1, 256)), const((256, 1024)), const((1, 1024)), const((1, 128)),
                  const((128, 1024)), const((128, 512)), const((1, 1024)),
                  const((1024, 128)), const((128, 1024)),
                  rope_spec, rope_spec, rope_spec],
        out_specs=[row(1024), row(1024), row(512), row(128), row(128)],
        compiler_params=_params(1),
        name="mla_prep",
    )(z, w["qa_g"], w["wuq"], w["qn_g"], w["kva_g"], w["wk"], w["wv"], w["kn_g"], *_head_segments(), *rope_tabs)


def _mla_cache(ckv, kr_pad, w):
    const = lambda shape: pl.BlockSpec(shape, lambda i: (0,) * len(shape))
    row = lambda width: pl.BlockSpec((TM, width), lambda i: (i, 0))
    n = ckv.shape[0]
    return pl.pallas_call(
        _mla_cache_kernel,
        out_shape=[jax.ShapeDtypeStruct((n, 1024), BF16), jax.ShapeDtypeStruct((n, 512), BF16)],
        grid=(n // TM,),
        in_specs=[row(128), row(128), const((128, 1024)), const((128, 512)), const((1, 1024)),
                  const((1024, 128)), const((128, 1024))],
        out_specs=[row(1024), row(512)],
        compiler_params=_params(1),
        name="mla_cache",
    )(ckv, kr_pad, w["wk"], w["wv"], w["kn_g"], *_head_segments())


def _mla_attn_kernel(*refs, has_cache, n_seq, seq):
    q_ref, k_ref, v_ref = refs[:3]
    pos = 3
    if has_cache:
        kc_ref, vc_ref = refs[3:5]
        pos = 5
    o_ref = refs[pos]
    for s, p in [(s, p) for p in range(MLA_HEADS // 2) for s in range(n_seq)]:
        qr = slice(s * TM, (s + 1) * TM)
        kr = slice(s * seq, (s + 1) * seq)
        acc = jnp.zeros((TM, 128), F32)
        for hh in range(2):
            h = 2 * p + hh
            lanes = slice(128 * h, 128 * h + 128)
            qh = q_ref[qr, lanes]
            l1 = _dot_t(qh, k_ref[kr, lanes])
            m = jnp.max(l1, axis=-1, keepdims=True)
            if has_cache:
                l0 = _dot_t(qh, kc_ref[:, lanes])
                m = jnp.maximum(m, jnp.max(l0, axis=-1, keepdims=True))
                p0 = jnp.exp(l0 - m)
            p1 = jnp.exp(l1 - m)
            den = jnp.sum(p1, axis=-1, keepdims=True)
            if has_cache:
                den = den + jnp.sum(p0, axis=-1, keepdims=True)
            mask = _lane_half_mask(hh)
            vp = v_ref[kr, 128 * p:128 * p + 128]
            o = _dot(p1.astype(BF16), jnp.where(mask, vp, jnp.zeros_like(vp)))
            if has_cache:
                vcp = vc_ref[:, 128 * p:128 * p + 128]
                o += _dot(p0.astype(BF16), jnp.where(mask, vcp, jnp.zeros_like(vcp)))
            acc += o * (1.0 / den)
        o_ref[qr, 128 * p:128 * p + 128] = acc.astype(BF16)


def _mla_attn(q, k, v, kc, vc, *, ctx):
    if ctx:
        nb, seq, row_blk, tile0 = N_CTX_SEQ, CTX_LEN, 0, 0
    else:
        nb, seq, row_blk, tile0 = N_LAT_SEQ, LAT_LEN, T_CTX // LAT_LEN, N_CTX_TILES
    nq = seq // TM
    grp = CTX_GROUP if ctx else 1
    nb //= grp
    in_specs = [pl.BlockSpec((grp * TM, 1024), lambda b, i: (tile0 // grp + b * nq + i, 0)),
                pl.BlockSpec((grp * seq, 1024), lambda b, i: (row_blk + b, 0)),
                pl.BlockSpec((grp * seq, 512), lambda b, i: (row_blk + b, 0))]
    args = [q, k, v]
    if not ctx:
        in_specs += [pl.BlockSpec((TM, 1024), lambda b, i: (b, 0)), pl.BlockSpec((TM, 512), lambda b, i: (b, 0))]
        args += [kc, vc]
    return pl.pallas_call(
        functools.partial(_mla_attn_kernel, has_cache=not ctx, n_seq=grp, seq=seq),
        out_shape=jax.ShapeDtypeStruct((nb * grp * seq, 512), BF16),
        grid=(nb, nq), in_specs=in_specs,
        out_specs=pl.BlockSpec((grp * TM, 512), lambda b, i: (b * nq + i, 0)),
        compiler_params=_params(2),
        name="mla_attn_ctx" if ctx else "mla_attn_lat",
    )(*args)


def _merge_kernel(retc_ref, retl_ref, glac_ref, glal_ref, mlac_ref, mlal_ref, m0_ref, m1_ref, m2_ref,
                  xc_ref, xl_ref, wb_ref, wo_ref,
                  g1_ref, n2_ref, sh2_ref, sc2_ref, rwh_ref, rwl_ref, rb_ref,
                  x1_ref, h_ref, idx_ref, w_ref, rank_ref, cnt_ref):
    @pl.when(pl.program_id(0) == 0)
    def _():
        cnt_ref[...] = jnp.zeros_like(cnt_ref)

    is_ctx = pl.program_id(0) < T_CTX // MERGE_TM
    mix = None
    for c_ref, l_ref, m_ref, n in ((retc_ref, retl_ref, m0_ref, 0), (glac_ref, glal_ref, m1_ref, 1),
                                   (mlac_ref, mlal_ref, m2_ref, 2)):
        branch = jnp.where(is_ctx, c_ref[...], l_ref[...])
        term = _sigmoid(m_ref[...]).astype(F32) * _dot(branch, wb_ref[n])
        mix = term if mix is None else mix + term
    out = _dot(mix.astype(BF16), wo_ref[...])
    x1 = jnp.where(is_ctx, xc_ref[...], xl_ref[...]) + g1_ref[0] * out
    x1_ref[...] = x1
    h = x1 * lax.rsqrt(jnp.mean(x1 * x1, axis=-1, keepdims=True) + EPS) * n2_ref[...]
    h = h * (1.0 + sc2_ref[0]) + sh2_ref[0]
    h_ref[...] = h
    hh = h.astype(BF16)
    hl = (h - hh.astype(F32)).astype(BF16)
    logits = _dot(hh, rwh_ref[...]) + _dot(hh, rwl_ref[...]) + _dot(hl, rwh_ref[...]) + rb_ref[...]
    rows = MERGE_TM // ROUTE_CHUNKS
    lane = lax.broadcasted_iota(jnp.int32, (rows, 128), 1)
    lanef = lane.astype(F32)
    onehots, osums = [], []
    for c in range(ROUTE_CHUNKS):
        l = jnp.where(lane < N_EXPERTS, logits[c * rows:(c + 1) * rows], -jnp.inf)
        vals, idxs = [], []
        for _ in range(TOP_K):
            m = jnp.max(l, axis=-1, keepdims=True)
            ix = jnp.min(jnp.where(l == m, lanef, 128.0), axis=-1, keepdims=True)
            vals.append(m)
            idxs.append(ix)
            l = jnp.where(lanef == ix, -jnp.inf, l)
        es = [jnp.exp(v - vals[0]) for v in vals]
        inv = 1.0 / (es[0] + es[1] + es[2] + es[3])
        idx_out = jnp.zeros((rows, 128), F32)
        w_out = jnp.zeros((rows, 128), F32)
        for kk in range(TOP_K):
            idx_out = jnp.where(lane == kk, idxs[kk], idx_out)
            w_out = jnp.where(lane == kk, es[kk] * inv, w_out)
        idx_ref[c * rows:(c + 1) * rows, :] = idx_out.astype(jnp.int32)
        w_ref[c * rows:(c + 1) * rows, :] = w_out
        oh = [jnp.where(lanef == ix, 1.0, 0.0) for ix in idxs]
        onehots.append(oh)
        osums.append((oh[0] + oh[1]) + (oh[2] + oh[3]))

    osum = jnp.concatenate(osums, axis=0)
    ri = lax.broadcasted_iota(jnp.int32, (MERGE_TM, MERGE_TM), 0)
    ci = lax.broadcasted_iota(jnp.int32, (MERGE_TM, MERGE_TM), 1)
    before = jnp.where(ri > ci, 1.0, 0.0).astype(BF16)
    prior = _dot(before, osum.astype(BF16)) + cnt_ref[0:1, :]
    for c in range(ROUTE_CHUNKS):
        pc = prior[c * rows:(c + 1) * rows]
        rank_out = jnp.zeros((rows, 128), F32)
        for kk in range(TOP_K):
            rank_out = jnp.where(lane == kk, jnp.sum(onehots[c][kk] * pc, axis=-1, keepdims=True), rank_out)
        rank_ref[c * rows:(c + 1) * rows, :] = rank_out.astype(jnp.int32)
    cnt_ref[...] = cnt_ref[...] + jnp.sum(osum, axis=0, keepdims=True)


def _merge(branches, z, xc, xl, modr, w, layer):
    mrow = _mod_row(MERGE_TM)
    base = layer * N_MOD

    def mod_spec(part):
        return pl.BlockSpec((1, 1, D_MODEL), lambda i: ((base + mrow(i)) * 6 + part, 0, 0))

    const = lambda shape: pl.BlockSpec(shape, lambda i: (0,) * len(shape))
    row = lambda width: pl.BlockSpec((MERGE_TM, width), lambda i: (i, 0))
    gate = lambda col: pl.BlockSpec((MERGE_TM, 1024), lambda i: (i, col))
    return pl.pallas_call(
        _merge_kernel,
        out_shape=[jax.ShapeDtypeStruct((T_ALL, D_MODEL), F32), jax.ShapeDtypeStruct((T_ALL, D_MODEL), F32),
                   jax.ShapeDtypeStruct((T_ALL, 128), jnp.int32), jax.ShapeDtypeStruct((T_ALL, 128), F32),
                   jax.ShapeDtypeStruct((T_ALL, 128), jnp.int32), jax.ShapeDtypeStruct((8, 128), F32)],
        grid=(T_ALL // MERGE_TM,),
        in_specs=[_ctx_spec(MERGE_TM, 512), _lat_spec(MERGE_TM, 512)] * 3 + [gate(3), gate(4), gate(5),
                  _ctx_spec(MERGE_TM, D_MODEL), _lat_spec(MERGE_TM, D_MODEL),
                  const((3, 512, 1024)), const((1024, 1024)),
                  mod_spec(2), const((1, 1024)), mod_spec(3), mod_spec(4),
                  const((1024, 128)), const((1024, 128)), const((1, 128))],
        out_specs=[row(1024), row(1024), row(128), row(128), row(128), const((8, 128))],
        compiler_params=_params(1),
        name="merge",
    )(*branches, z, z, z, xc, xl, w["wb"], w["wo"], modr, w["n2_g"], modr, modr,
      w["rw_hi"], w["rw_lo"], w["rb"])


def _route(top_idx, rank, counts):
    flat_e = top_idx.reshape(N_SLOTS)
    onehot = (flat_e[:, None] == jnp.arange(N_EXPERTS, dtype=jnp.int32)[None, :]).astype(jnp.int32)
    padded = (counts + MOE_ROWS - 1) // MOE_ROWS * MOE_ROWS
    pad_end = jnp.cumsum(padded)
    pad_start = pad_end - padded
    dest = (rank.reshape(N_SLOTS) + jnp.sum(onehot * pad_start[None, :], axis=1)).astype(jnp.int32)
    blk_start = jnp.arange(N_MOE_BLOCKS, dtype=jnp.int32) * MOE_ROWS
    block_e = jnp.minimum(jnp.sum((pad_end[None, :] <= blk_start[:, None]).astype(jnp.int32), axis=1),
                          N_EXPERTS - 1).astype(jnp.int32)
    n_used = (pad_end[-1] // MOE_ROWS).astype(jnp.int32)
    e_hot = (block_e[:, None] == jnp.arange(N_EXPERTS, dtype=jnp.int32)[None, :]).astype(jnp.int32)
    nxt_blk = jnp.sum(e_hot * pad_end[None, :], axis=1) // MOE_ROWS
    b_hot = (nxt_blk[:, None] == jnp.arange(N_MOE_BLOCKS, dtype=jnp.int32)[None, :]).astype(jnp.int32)
    next_e = jnp.where(nxt_blk < n_used, jnp.sum(b_hot * block_e[None, :], axis=1), block_e).astype(jnp.int32)
    tail_start = (pad_start + counts).astype(jnp.int32)
    return dest, tail_start, block_e, n_used.reshape(1), next_e


def _rows_to_tiles(x):
    r = x.shape[0]
    blocks = jnp.stack([x[:, 128 * g:128 * (g + 1)].reshape(r // 8, 8, 128) for g in range(D_MODEL // 128)], axis=1)
    return jnp.swapaxes(blocks, 1, 2).reshape(r, D_MODEL // 128, 128)


def _tiles_to_rows(v):
    r = v.shape[0]
    blocks = jnp.swapaxes(v.reshape(r // 8, 8, D_MODEL // 128, 128), 1, 2)
    return jnp.concatenate([blocks[:, g].reshape(r, 128) for g in range(D_MODEL // 128)], axis=1)


def _dispatch_kernel(dest_ref, tail_ref, nb_ref, h_ref, xs_ref, zero_buf, stage, sem, ssem):
    i = pl.program_id(0)

    @pl.when(i == 0)
    def _():
        zero_buf[...] = jnp.zeros_like(zero_buf)
        fills = [pltpu.make_async_copy(zero_buf, xs_ref.at[pl.ds(tail_ref[e], MOE_ROWS)], sem)
                 for e in range(N_EXPERTS)]
        for f in fills:
            f.start()
        for f in fills:
            f.wait()

        def fill_block(b, c):
            f = pltpu.make_async_copy(zero_buf, xs_ref.at[pl.ds(b * MOE_ROWS, MOE_ROWS)], sem)
            f.start()
            f.wait()
            return c
        lax.fori_loop(nb_ref[0], (N_MOE_ROWS + XS_EXTRA) // MOE_ROWS, fill_block, 0)

    base = i * TM * TOP_K
    cur = i % 2
    row0 = pl.multiple_of(cur * TM, TM)

    def wait_tile(slot):
        for kk in range(TOP_K):
            pltpu.make_async_copy(stage.at[pl.ds(0, TM)], xs_ref.at[pl.ds(0, TM)], ssem.at[slot]).wait()

    @pl.when(i >= 2)
    def _():
        wait_tile(cur)
    stage[pl.ds(row0, TM)] = _rows_to_tiles(h_ref[...])

    def issue(t, c):
        for kk in range(TOP_K):
            pltpu.make_async_copy(stage.at[row0 + t], xs_ref.at[dest_ref[base + t * TOP_K + kk]], ssem.at[cur]
                                  ).start(priority=kk % 2)
        return c
    lax.fori_loop(0, TM, issue, 0, unroll=8)

    @pl.when(i == N_TILES - 1)
    def _():
        wait_tile(1 - cur)
        wait_tile(cur)


def _moe_dispatch(h, dest, tail_start, n_used):
    return pl.pallas_call(
        _dispatch_kernel,
        out_shape=jax.ShapeDtypeStruct((N_MOE_ROWS + XS_EXTRA,) + ROW_TILE, F32),
        grid_spec=pltpu.PrefetchScalarGridSpec(
            num_scalar_prefetch=3, grid=(N_TILES,),
            in_specs=[pl.BlockSpec((TM, D_MODEL), lambda i, d, t, nb: (i, 0))],
            out_specs=pl.BlockSpec(memory_space=pl.ANY),
            scratch_shapes=[pltpu.VMEM((MOE_ROWS,) + ROW_TILE, F32), pltpu.VMEM((2 * TM,) + ROW_TILE, F32),
                            pltpu.SemaphoreType.DMA, pltpu.SemaphoreType.DMA((2,))]),
        compiler_params=_params(1),
        name="moe_dispatch",
    )(dest, tail_start, n_used, h)


def _expert_kernel(be_ref, nb_ref, nxt_ref, x_ref, wgu_hbm, bgu_ref, wd_hbm, bd_ref, o_ref,
                   wgu_st, wd_st, wgu_bf, wd_bf, sem, *, layer):
    i = pl.program_id(0)
    e = be_ref[i]
    prev = be_ref[jnp.maximum(i - 1, 0)]

    def weight_copies(expert):
        idx = layer * N_EXPERTS + expert
        return (pltpu.make_async_copy(wgu_hbm.at[idx], wgu_st, sem.at[0]),
                pltpu.make_async_copy(wd_hbm.at[idx], wd_st, sem.at[1]))

    @pl.when(i == 0)
    def _():
        for cp in weight_copies(e):
            cp.start()

    @pl.when(((i == 0) | (e != prev)) & (i < nb_ref[0]))
    def _():
        for cp in weight_copies(e):
            cp.wait()
        wgu_bf[...] = wgu_st[...].astype(BF16)
        wd_bf[...] = wd_st[...].astype(BF16)
        nxt = nxt_ref[i]

        @pl.when(nxt != e)
        def _():
            for cp in weight_copies(nxt):
                cp.start()

    @pl.when(i < nb_ref[0])
    def _():
        gu = _dot(_tiles_to_rows(x_ref[...]).astype(BF16), wgu_bf[...]) + bgu_ref[0]
        gate = jnp.minimum(gu[:, :D_EXPERT], SWIGLU_LIMIT)
        up = jnp.clip(gu[:, D_EXPERT:], -SWIGLU_LIMIT, SWIGLU_LIMIT)
        act = (up + 1.0) * gate * _sigmoid(SWIGLU_ALPHA * gate)
        o_ref[...] = _rows_to_tiles(_dot(act.astype(BF16), wd_bf[...]) + bd_ref[0])

    @pl.when(i >= nb_ref[0])
    def _():
        o_ref[...] = jnp.zeros_like(o_ref)


def _moe_experts(xs, block_e, n_used, next_e, w_gu, b_gu, w_down, b_down, layer):
    w_idx = lambda i, be, nb, nx: (layer * N_EXPERTS + be[i], 0, 0)
    return pl.pallas_call(
        functools.partial(_expert_kernel, layer=layer),
        out_shape=jax.ShapeDtypeStruct((N_MOE_ROWS,) + ROW_TILE, F32),
        grid_spec=pltpu.PrefetchScalarGridSpec(
            num_scalar_prefetch=3, grid=(N_MOE_BLOCKS,),
            in_specs=[pl.BlockSpec((MOE_ROWS,) + ROW_TILE, lambda i, be, nb, nx: (jnp.minimum(i, nb[0] - 1), 0, 0)),
                      pl.BlockSpec(memory_space=pl.ANY),
                      pl.BlockSpec((1, 1, 2 * D_EXPERT), w_idx),
                      pl.BlockSpec(memory_space=pl.ANY),
                      pl.BlockSpec((1, 1, D_MODEL), w_idx)],
            out_specs=pl.BlockSpec((MOE_ROWS,) + ROW_TILE, lambda i, be, nb, nx: (i, 0, 0)),
            scratch_shapes=[pltpu.VMEM((D_MODEL, 2 * D_EXPERT), F32), pltpu.VMEM((D_EXPERT, D_MODEL), F32),
                            pltpu.VMEM((D_MODEL, 2 * D_EXPERT), BF16), pltpu.VMEM((D_EXPERT, D_MODEL), BF16),
                            pltpu.SemaphoreType.DMA((2,))]),
        compiler_params=_params(1),
        name="moe_experts",
    )(block_e, n_used, next_e, xs, w_gu, b_gu, w_down, b_down)


def _combine_kernel(dest_ref, x_ref, g2_ref, w_ref, eo_ref, yc_ref, yl_ref, buf, sem):
    i = pl.program_id(0)
    cur = i % 2

    def issue_tile(tile, slot):
        base = tile * TM * TOP_K
        row0 = slot * (TOP_K * TM)

        def issue(t, c):
            for kk in range(TOP_K):
                pltpu.make_async_copy(eo_ref.at[dest_ref[base + t * TOP_K + kk]], buf.at[row0 + kk * TM + t],
                                      sem.at[slot]).start(priority=kk % 2)
            return c
        lax.fori_loop(0, TM, issue, 0, unroll=8)

    @pl.when(i == 0)
    def _():
        issue_tile(0, 0)

    @pl.when(i + 1 < N_TILES)
    def _():
        issue_tile(i + 1, 1 - cur)

    for kk in range(TOP_K):
        pltpu.make_async_copy(eo_ref.at[pl.ds(0, TM)], buf.at[pl.ds(0, TM)], sem.at[cur]).wait()
    w = w_ref[...]
    acc = None
    for kk in range(TOP_K):
        wk = jnp.broadcast_to(w[:, kk:kk + 1], (TM, 128)).reshape(TM // 8, 8, 128)
        wt = jnp.stack([jnp.broadcast_to(wk[:, r:r + 1, :], (TM // 8, 8, 128)) for r in range(8)], axis=1)
        term = buf[pl.ds(pl.multiple_of(cur * (TOP_K * TM) + kk * TM, TM), TM)] * wt.reshape(TM, 8, 128)
        acc = term if acc is None else acc + term
    ff = _tiles_to_rows(acc)
    y = x_ref[...] + g2_ref[0] * ff
    is_ctx = i < N_CTX_TILES

    @pl.when(is_ctx)
    def _():
        yc_ref[...] = y

    @pl.when(jnp.logical_not(is_ctx))
    def _():
        yl_ref[...] = y


def _moe_combine(dest, x1, modr, top_w, eo, layer):
    mrow = _mod_row(TM)
    base = layer * N_MOD
    return pl.pallas_call(
        _combine_kernel,
        out_shape=[jax.ShapeDtypeStruct((T_CTX, D_MODEL), F32), jax.ShapeDtypeStruct((T_LAT, D_MODEL), F32)],
        grid_spec=pltpu.PrefetchScalarGridSpec(
            num_scalar_prefetch=1, grid=(N_TILES,),
            in_specs=[pl.BlockSpec((TM, D_MODEL), lambda i, d: (i, 0)),
                      pl.BlockSpec((1, 1, D_MODEL), lambda i, d: ((base + mrow(i)) * 6 + 5, 0, 0)),
                      pl.BlockSpec((TM, 128), lambda i, d: (i, 0)),
                      pl.BlockSpec(memory_space=pl.ANY)],
            out_specs=[pl.BlockSpec((TM, D_MODEL), lambda i, d: (jnp.minimum(i, N_CTX_TILES - 1), 0)),
                       pl.BlockSpec((TM, D_MODEL), lambda i, d: (jnp.maximum(i - N_CTX_TILES, 0), 0))],
            scratch_shapes=[pltpu.VMEM((2 * TOP_K * TM,) + ROW_TILE, F32), pltpu.SemaphoreType.DMA((2,))]),
        compiler_params=_params(1),
        name="moe_combine",
    )(dest, x1, modr, top_w, eo)


def _pad_heads(w, n_heads, width):
    lead = w.shape[:-1]
    w = w.reshape(lead + (n_heads, width))
    w = jnp.pad(w, [(0, 0)] * len(lead) + [(0, 0), (0, HEAD_PAD - width)])
    return w.reshape(lead + (n_heads * HEAD_PAD,))


def _permute_w_in_kernel(w_ref, o_ref):
    w = w_ref[0]
    o_ref[0] = jnp.concatenate([w[:, :3072], w[:, 3520:6592], w[:, 3104:3520], w[:, 3072:3104], w[:, 6592:]],
                               axis=1)


def _permute_w_in(w_in):
    rows = 128
    w_pad = jnp.pad(w_in, ((0, 0), (0, 0), (0, DZ - w_in.shape[2]))).astype(BF16)
    return pl.pallas_call(
        _permute_w_in_kernel,
        out_shape=jax.ShapeDtypeStruct((DEPTH, D_MODEL, DZ), BF16),
        grid=(DEPTH, D_MODEL // rows),
        in_specs=[pl.BlockSpec((1, rows, DZ), lambda l, r: (l, r, 0))],
        out_specs=pl.BlockSpec((1, rows, DZ), lambda l, r: (l, r, 0)),
        compiler_params=_params(2),
        name="permute_w_in",
    )(w_pad)


def _layer_weights(l, gla_wa2, gla_ba, mla_qa_g, mla_wuq, mla_kva_g, mla_wukv, mla_qn_g, mla_kn_g,
                   w_branch, w_out, router_w, router_b, norm2_g):
    wa_p = jnp.zeros((512, 512), F32)
    wa_p = wa_p.at[416:432, 0:256].set(gla_wa2[l, 0]).at[432:448, 256:512].set(gla_wa2[l, 1]).astype(BF16)
    ba_p = gla_ba[l].reshape(1, 512)
    wukv = mla_wukv[l].reshape(128, MLA_HEADS, MLA_NOPE + MLA_V)
    rw = jnp.pad(router_w[l], ((0, 0), (0, 128 - N_EXPERTS)))
    rw_hi = rw.astype(BF16)
    return {
        "wa": wa_p, "ba": ba_p,
        "qa_g": mla_qa_g[l].reshape(1, 256),
        "wuq": _pad_heads(mla_wuq[l], MLA_HEADS, MLA_QK).astype(BF16),
        "qn_g": jnp.tile(jnp.pad(mla_qn_g[l], (0, HEAD_PAD - MLA_QK)), MLA_HEADS).reshape(1, 1024) * MLA_QK ** -0.5,
        "kn_g": jnp.tile(jnp.pad(mla_kn_g[l], (0, HEAD_PAD - MLA_QK)), MLA_HEADS).reshape(1, 1024),
        "kva_g": mla_kva_g[l].reshape(1, 128),
        "wk": _pad_heads(wukv[:, :, :MLA_NOPE].reshape(128, MLA_HEADS * MLA_NOPE), MLA_HEADS, MLA_NOPE).astype(BF16),
        "wv": wukv[:, :, MLA_NOPE:].reshape(128, MLA_HEADS * MLA_V).astype(BF16),
        "wb": w_branch[l].astype(BF16), "wo": w_out[l].astype(BF16),
        "rw_hi": rw_hi, "rw_lo": (rw - rw_hi.astype(F32)).astype(BF16),
        "rb": jnp.pad(router_b[l], (0, 128 - N_EXPERTS)).reshape(1, 128),
        "n2_g": norm2_g[l].reshape(1, D_MODEL),
    }


def kernel(x_prompt, x_sample, cache_mla_ckv, cache_mla_krope, state_ret, state_gla, c, c_ctx, w_mod, b_mod, norm1_g, norm2_g, w_in, ret_gn_g, gla_wa2, gla_ba, gla_norm_g, mla_qa_g, mla_wuq, mla_kva_g, mla_wukv, mla_qn_g, mla_kn_g, w_branch, w_out, router_w, router_b, moe_w_gu, moe_b_gu, moe_w_down, moe_b_down):
    xc, xl = x_prompt.reshape(T_CTX, D_MODEL), x_sample.reshape(T_LAT, D_MODEL)
    cc = jnp.concatenate([c_ctx[None, :], c, jnp.zeros((N_MOD - 1 - N_LAT_SEQ, D_MODEL), F32)], axis=0)
    modr = _modulation(cc, w_mod, b_mod).reshape(DEPTH * N_MOD * 6, 1, D_MODEL)
    rope_tabs = _rope_tables()
    w_in_p = _permute_w_in(w_in)
    w_gu = moe_w_gu.reshape(DEPTH * N_EXPERTS, D_MODEL, 2 * D_EXPERT)
    b_gu = moe_b_gu.reshape(DEPTH * N_EXPERTS, 1, 2 * D_EXPERT)
    w_dn = moe_w_down.reshape(DEPTH * N_EXPERTS, D_EXPERT, D_MODEL)
    b_dn = moe_b_down.reshape(DEPTH * N_EXPERTS, 1, D_MODEL)

    ckv_l, krope_l, ret_l, gla_l = [], [], [], []
    for l in range(DEPTH):
        w = _layer_weights(l, gla_wa2, gla_ba, mla_qa_g, mla_wuq, mla_kva_g, mla_wukv, mla_qn_g, mla_kn_g,
                           w_branch, w_out, router_w, router_b, norm2_g)
        z = _in_proj(xc, xl, norm1_g[l].reshape(1, D_MODEL), modr, w_in_p, l)

        gn = ret_gn_g[l].reshape(1, 512)
        ret_c, ret_state = _retention(z, gn, None, ctx=True)
        (ret_s,) = _retention(z, gn, state_ret[:, l], ctx=False)
        gng = gla_norm_g[l].reshape(1, 512)
        gla_c, gla_state = _gla(z, w["wa"], w["ba"], gng, None, ctx=True)
        gla_s, _ = _gla(z, w["wa"], w["ba"], gng, state_gla[:, l], ctx=False)

        q, k, v, ckv, kr = _mla_prep(z, w, rope_tabs)
        kc, vc = _mla_cache(cache_mla_ckv[:, l].reshape(N_LAT_SEQ * CTX_LEN, 128),
                            jnp.pad(cache_mla_krope[:, l].reshape(N_LAT_SEQ * CTX_LEN, MLA_ROPE),
                                    ((0, 0), (0, 128 - MLA_ROPE))), w)
        mla_c = _mla_attn(q, k, v, None, None, ctx=True)
        mla_s = _mla_attn(q, k, v, kc, vc, ctx=False)

        x1, h2, top_idx, top_w, rank, cnt = _merge((ret_c, ret_s, gla_c, gla_s, mla_c, mla_s), z, xc, xl, modr, w, l)
        dest, tail_start, block_e, n_used, next_e = _route(top_idx[:, :TOP_K], rank[:, :TOP_K],
                                                           cnt[0, :N_EXPERTS].astype(jnp.int32))
        xs = _moe_dispatch(h2, dest, tail_start, n_used)
        eo = _moe_experts(xs, block_e, n_used, next_e, w_gu, b_gu, w_dn, b_dn, l)
        xc, xl = _moe_combine(dest, x1, modr, top_w, eo, l)

        ckv_l.append(ckv[:T_CTX].reshape(N_CTX_SEQ, CTX_LEN, 128))
        krope_l.append(kr[:T_CTX, :MLA_ROPE].reshape(N_CTX_SEQ, CTX_LEN, MLA_ROPE))
        ret_l.append(ret_state)
        gla_l.append(gla_state.reshape(N_CTX_SEQ, 2, N_HEADS, DK, DV))

    y_p = xc.reshape(N_CTX_SEQ, CTX_LEN, D_MODEL)
    y_s = xl.reshape(N_LAT_SEQ, LAT_LEN, D_MODEL)
    return (y_p, y_s, jnp.stack(ckv_l, axis=1), jnp.stack(krope_l, axis=1),
            jnp.stack(ret_l, axis=1), jnp.stack(gla_l, axis=1))
```

```python
import functools

import jax
import jax.numpy as jnp
import numpy as np
from jax import lax
from jax.experimental import pallas as pl
from jax.experimental.pallas import tpu as pltpu

F32 = jnp.float32
BF16 = jnp.bfloat16

D_MODEL = 1024
DEPTH = 2
N_CTX_SEQ, CTX_LEN = 32, 256
N_LAT_SEQ, LAT_LEN = 4, 1024
T_CTX = N_CTX_SEQ * CTX_LEN
T_LAT = N_LAT_SEQ * LAT_LEN
T_ALL = T_CTX + T_LAT
TM = 256
N_TILES = T_ALL // TM
N_CTX_TILES = T_CTX // TM
LAT_TILES = LAT_LEN // TM
N_MOD = 8
EPS = 1e-6

N_HEADS = 4
DK, DV = 64, 128
GRID_W = 64
MLA_HEADS, MLA_NOPE, MLA_ROPE, MLA_V = 8, 64, 32, 64
MLA_QK = MLA_NOPE + MLA_ROPE
HEAD_PAD = 128
GLA_TAU = 16.0
N_EXPERTS, TOP_K, D_EXPERT = 32, 4, 1024
SWIGLU_LIMIT, SWIGLU_ALPHA = 7.0, 1.702
MOE_ROWS = 512
N_SLOTS = T_ALL * TOP_K
N_MOE_BLOCKS = N_SLOTS // MOE_ROWS + N_EXPERTS
N_MOE_ROWS = N_MOE_BLOCKS * MOE_ROWS
MERGE_TM = 512
CTX_GROUP = 4
ROUTE_CHUNKS = 8
XS_EXTRA = MOE_ROWS
ROW_TILE = (D_MODEL // 128, 128)

DZ = 6656
IN_TILE = 512
VMEM_LIMIT = 56 * 1024 * 1024

RET_LOG_F = [float(np.log1p(-np.exp2(-(5.0 + h)))) for h in range(N_HEADS)]
RET_LOG_B = [float(np.log1p(-np.exp2(-(5.5 + h)))) for h in range(N_HEADS)]


def _params(n_axes, vmem=VMEM_LIMIT):
    return pltpu.CompilerParams(dimension_semantics=("arbitrary",) * n_axes, vmem_limit_bytes=vmem)


def _sigmoid(x):
    return 1.0 / (1.0 + jnp.exp(-x))


def _dot(a, b):
    return jnp.dot(a, b, preferred_element_type=F32)


def _dot_t(a, b):
    return lax.dot_general(a, b, (((1,), (1,)), ((), ())), preferred_element_type=F32)


def _mod_row(tile_rows):
    def f(i):
        r0 = i * tile_rows
        return jnp.where(r0 < T_CTX, 0, 1 + (r0 - T_CTX) // LAT_LEN)
    return f


def _ctx_spec(rows, width):
    n_ctx = T_CTX // rows
    return pl.BlockSpec((rows, width), lambda i: (jnp.minimum(i, n_ctx - 1), 0))


def _lat_spec(rows, width):
    n_ctx = T_CTX // rows
    return pl.BlockSpec((rows, width), lambda i: (jnp.maximum(i - n_ctx, 0), 0))


def _mod_kernel(c_ref, w_ref, b_ref, o_ref):
    c = c_ref[...]
    s = c * _sigmoid(c)
    sh = s.astype(BF16)
    sl = (s - sh.astype(F32)).astype(BF16)
    w = w_ref[0]
    wh = w.astype(BF16)
    wl = (w - wh.astype(F32)).astype(BF16)
    o_ref[0] = _dot(sh, wh) + _dot(sh, wl) + _dot(sl, wh) + b_ref[0]


def _modulation(cc, w_mod, b_mod):
    n = 6 * D_MODEL
    blk = 2048
    return pl.pallas_call(
        _mod_kernel,
        out_shape=jax.ShapeDtypeStruct((DEPTH, N_MOD, n), F32),
        grid=(DEPTH, n // blk),
        in_specs=[pl.BlockSpec((N_MOD, D_MODEL), lambda l, j: (0, 0)),
                  pl.BlockSpec((1, D_MODEL, blk), lambda l, j: (l, 0, j)),
                  pl.BlockSpec((1, 1, blk), lambda l, j: (l, 0, j))],
        out_specs=pl.BlockSpec((1, N_MOD, blk), lambda l, j: (l, 0, j)),
        compiler_params=_params(2),
        name="modulation",
    )(cc, w_mod, b_mod.reshape(DEPTH, 1, n))


def _in_kernel(xc_ref, xl_ref, g_ref, sh_ref, sc_ref, w_ref, o_ref):
    x = jnp.where(pl.program_id(0) < T_CTX // IN_TILE, xc_ref[...], xl_ref[...])
    h = x * lax.rsqrt(jnp.mean(x * x, axis=-1, keepdims=True) + EPS) * g_ref[...]
    h = h * (1.0 + sc_ref[0]) + sh_ref[0]
    hb = h.astype(BF16)
    for n0 in range(0, DZ, 512):
        o_ref[:, n0:n0 + 512] = _dot(hb, w_ref[0, :, n0:n0 + 512]).astype(BF16)


def _in_proj(xc, xl, g, modr, w_in_p, layer):
    mrow = _mod_row(IN_TILE)
    base = layer * N_MOD

    def mod_spec(part):
        return pl.BlockSpec((1, 1, D_MODEL), lambda i: ((base + mrow(i)) * 6 + part, 0, 0))

    return pl.pallas_call(
        _in_kernel,
        out_shape=jax.ShapeDtypeStruct((T_ALL, DZ), BF16),
        grid=(T_ALL // IN_TILE,),
        in_specs=[_ctx_spec(IN_TILE, D_MODEL), _lat_spec(IN_TILE, D_MODEL),
                  pl.BlockSpec((1, D_MODEL), lambda i: (0, 0)),
                  mod_spec(0), mod_spec(1),
                  pl.BlockSpec((1, D_MODEL, DZ), lambda i: (layer, 0, 0))],
        out_specs=pl.BlockSpec((IN_TILE, DZ), lambda i: (i, 0)),
        compiler_params=_params(1),
        name="in_proj",
    )(xc, xl, g, modr, modr, w_in_p)


def _lane_half_mask(hh):
    lane = lax.broadcasted_iota(jnp.int32, (1, 128), 1)
    return (lane < 64) if hh == 0 else (lane >= 64)


@functools.lru_cache(maxsize=None)
def _ret_decay_table(seq):
    d = np.arange(seq)[:, None] - np.arange(seq)[None, :]
    tab = np.stack([np.exp(np.where(d > 0, RET_LOG_F[h] * d, -RET_LOG_B[h] * d)) for h in range(N_HEADS)])
    return (tab * np.where(d == 0, 2.0, 1.0) * DK ** -0.5).astype(np.float32)


def _ret_kernel(*refs, seq, has_state, emit_state, n_seq):
    q_ref, k_ref, v_ref, g_ref, gn_ref, dec_ref = refs[:6]
    pos = 6
    if has_state:
        s0_ref = refs[pos]
        pos += 1
    o_ref = refs[pos]
    pos += 1
    if emit_state:
        st_ref = refs[pos]

    for sq in range(n_seq):
        kr = slice(sq * seq, (sq + 1) * seq)
        tr = slice(sq * TM, (sq + 1) * TM)
        r0 = pl.multiple_of(pl.program_id(1) * TM, TM)
        qb = q_ref[pl.ds(r0 + sq * seq, TM), :]
        ri = (lax.broadcasted_iota(jnp.int32, (TM, 1), 0) + r0).astype(F32)

        for h in range(N_HEADS):
            p, hh = h // 2, h % 2
            lanes = slice(128 * p, 128 * p + 128)
            qp = qb[:, lanes]
            qh = jnp.where(_lane_half_mask(hh), qp, jnp.zeros_like(qp))
            sc = _dot_t(qh, k_ref[kr, lanes])
            o = _dot((sc * dec_ref[h]).astype(BF16), v_ref[kr, 128 * h:128 * h + 128])
            if has_state:
                qf = qh.astype(F32)
                o += _dot((qf * jnp.exp(RET_LOG_F[h] * (ri + 1.0))).astype(BF16), s0_ref[0, 0, p].astype(BF16))
                o += _dot((qf * jnp.exp(RET_LOG_B[h] * (seq - ri))).astype(BF16), s0_ref[0, 1, p].astype(BF16))
            mu = jnp.mean(o, axis=-1, keepdims=True)
            d = o - mu
            var = jnp.mean(d * d, axis=-1, keepdims=True)
            on = d * lax.rsqrt(var + EPS)
            g = g_ref[tr, 128 * h:128 * h + 128].astype(F32)
            out = on * gn_ref[:, 128 * h:128 * h + 128] * (g * _sigmoid(g))
            o_ref[tr, 128 * h:128 * h + 128] = out.astype(BF16)

        if emit_state:
            jc = lax.broadcasted_iota(jnp.int32, (seq, 1), 0).astype(F32)
            lane = lax.broadcasted_iota(jnp.int32, (1, 128), 1)
            for p in range(2):
                kp = k_ref[kr, 128 * p:128 * p + 128].astype(F32) * DK ** -0.5
                lgf = jnp.where(lane < 64, RET_LOG_F[2 * p], RET_LOG_F[2 * p + 1])
                lgb = jnp.where(lane < 64, RET_LOG_B[2 * p], RET_LOG_B[2 * p + 1])
                kdf = (kp * jnp.exp(lgf * (seq - 1.0 - jc))).T.astype(BF16)
                kdb = (kp * jnp.exp(lgb * jc)).T.astype(BF16)
                for hh in range(2):
                    h = 2 * p + hh
                    vh = v_ref[kr, 128 * h:128 * h + 128]
                    st_ref[sq, 0, h] = _dot(kdf, vh)[64 * hh:64 * hh + 64, :]
                    st_ref[sq, 1, h] = _dot(kdb, vh)[64 * hh:64 * hh + 64, :]


def _retention(z, gn_g, s0, *, ctx):
    if ctx:
        nb, seq, row_blk, tile0 = N_CTX_SEQ, CTX_LEN, 0, 0
    else:
        nb, seq, row_blk, tile0 = N_LAT_SEQ, LAT_LEN, T_CTX // LAT_LEN, N_CTX_TILES
    nq = seq // TM
    grp = CTX_GROUP if ctx else 1
    nb //= grp
    in_specs = [pl.BlockSpec((grp * seq, 256), lambda b, i: (row_blk + b, 0)),
                pl.BlockSpec((grp * seq, 256), lambda b, i: (row_blk + b, 1)),
                pl.BlockSpec((grp * seq, 512), lambda b, i: (row_blk + b, 1)),
                pl.BlockSpec((grp * TM, 512), lambda b, i: (tile0 // grp + b * nq + i, 2)),
                pl.BlockSpec((1, 512), lambda b, i: (0, 0)),
                pl.BlockSpec((N_HEADS, TM, seq), lambda b, i: (0, i, 0))]
    args = [z, z, z, z, gn_g, _ret_decay_table(seq)]
    out_shape = [jax.ShapeDtypeStruct((nb * grp * seq, 512), BF16)]
    out_specs = [pl.BlockSpec((grp * TM, 512), lambda b, i: (b * nq + i, 0))]
    if not ctx:
        in_specs.append(pl.BlockSpec((1, 2, 2, 128, 128), lambda b, i: (b, 0, 0, 0, 0)))
        args.append(s0.reshape(N_LAT_SEQ, 2, 2, 128, 128))
    else:
        out_shape.append(jax.ShapeDtypeStruct((nb * grp, 2, N_HEADS, DK, DV), F32))
        out_specs.append(pl.BlockSpec((grp, 2, N_HEADS, DK, DV), lambda b, i: (b, 0, 0, 0, 0)))
    return pl.pallas_call(
        functools.partial(_ret_kernel, seq=seq, has_state=not ctx, emit_state=ctx, n_seq=grp),
        out_shape=out_shape, grid=(nb, nq), in_specs=in_specs, out_specs=out_specs,
        compiler_params=_params(2),
        name="retention_ctx" if ctx else "retention_lat",
    )(*args)


def _gla_decay(small_ref, wa_ref, ba_ref):
    x = _dot(small_ref[...], wa_ref[...]) + ba_ref[...]
    la = -(jnp.maximum(-x, 0.0) + jnp.log(1.0 + jnp.exp(-jnp.abs(x)))) * (1.0 / GLA_TAU)
    ri = lax.broadcasted_iota(jnp.int32, (TM, TM), 0)
    ci = lax.broadcasted_iota(jnp.int32, (TM, TM), 1)
    ltri = jnp.where(ri >= ci, 1.0, 0.0).astype(BF16)
    hi = la.astype(BF16)
    r1 = la - hi.astype(F32)
    mid = r1.astype(BF16)
    lo = (r1 - mid.astype(F32)).astype(BF16)
    cum = _dot(ltri, hi) + _dot(ltri, mid) + _dot(ltri, lo)
    return la, cum


def _gla_state_kernel(k_ref, v_ref, small_ref, wa_ref, ba_ref, kv_ref, tot_ref):
    la, cum = _gla_decay(small_ref, wa_ref, ba_ref)
    bf, bb = cum[:, :256], cum[:, 256:]
    xb = bb - la[:, 256:]
    k = k_ref[...].astype(F32)
    kdf = k * jnp.exp(bf[TM - 1:TM, :] - bf)
    kdb = k * jnp.exp(xb)
    for p in range(2):
        kf_t = kdf[:, 128 * p:128 * p + 128].T.astype(BF16)
        kb_t = kdb[:, 128 * p:128 * p + 128].T.astype(BF16)
        for hh in range(2):
            h = 2 * p + hh
            vh = v_ref[:, 128 * h:128 * h + 128]
            kv_ref[0, 0, 0, h] = _dot(kf_t, vh)[64 * hh:64 * hh + 64, :]
            kv_ref[0, 0, 1, h] = _dot(kb_t, vh)[64 * hh:64 * hh + 64, :]
    tot_ref[0, 0] = jnp.sum(la.T, axis=-1, keepdims=True)


def _mid_bcast(x, s, r):
    w = 2 * s
    if w >= 8:
        n = TM // w
        x3 = x.reshape(n, w, 256)
        return jnp.broadcast_to(x3[:, r:r + 1, :], (n, w, 256)).reshape(TM, 256)
    x3 = x.reshape(TM // 8, 8, 256)
    sub = lax.broadcasted_iota(jnp.int32, (1, 8, 1), 1)
    out = None
    for blk in range(8 // w):
        rowv = jnp.broadcast_to(x3[:, blk * w + r:blk * w + r + 1, :], (TM // 8, 8, 256))
        out = rowv if out is None else jnp.where(sub >= blk * w, rowv, out)
    return out.reshape(TM, 256)


def _gla_kernel(*refs, n_blk, has_state, emit_state):
    q_ref, k_ref, v_ref, g_ref, small_ref, wa_ref, ba_ref, gn_ref = refs[:8]
    pos = 8
    if has_state:
        kv_ref, tot_ref, s0_ref = refs[pos:pos + 3]
        pos += 3
    o_ref = refs[pos]
    pos += 1
    if emit_state:
        kvo_ref = refs[pos]

    la, cum = _gla_decay(small_ref, wa_ref, ba_ref)
    bf, bb = cum[:, :256], cum[:, 256:]
    xb = bb - la[:, 256:]
    q = q_ref[...].astype(F32) * DK ** -0.5
    k = k_ref[...].astype(F32)
    row = lax.broadcasted_iota(jnp.int32, (TM, 1), 0)
    rowi = lax.broadcasted_iota(jnp.int32, (TM, TM), 0)
    colj = lax.broadcasted_iota(jnp.int32, (TM, TM), 1)
    low_half = _lane_half_mask(0)

    def join(fwd, bwd):
        ops = []
        for p in range(N_HEADS // 2):
            f = fwd[:, 128 * p:128 * p + 128]
            br = pltpu.roll(bwd[:, 128 * p:128 * p + 128], 64, 1)
            ops.append(jnp.where(low_half, f, br).astype(BF16))
            ops.append(jnp.where(low_half, br, f).astype(BF16))
        return ops

    qd, kd = join(q, q), join(k, k)
    acc = [jnp.where(rowi == colj, _dot_t(qo, ko), 0.0) for qo, ko in zip(qd, kd)]

    low_f = jnp.where(low_half, 1.0, 0.0)
    s = 1
    while s < TM:
        up_f = jnp.where(((row // s) % 2) == 1, 1.0, 0.0)
        live_even = jnp.where(up_f == low_f, 1.0, 0.0).astype(BF16)
        live = [live_even, 1.0 - live_even]
        dead = [live[1], live[0]]
        af = -jnp.abs(bf - _mid_bcast(bf, s, s - 1))
        ab = -jnp.abs(xb - _mid_bcast(xb, s, s))
        same = (rowi // (2 * s)) == (colj // (2 * s))
        for p in range(N_HEADS // 2):
            a_f = af[:, 128 * p:128 * p + 128]
            a_b = pltpu.roll(ab[:, 128 * p:128 * p + 128], 64, 1)
            for hh in range(2):
                h = 2 * p + hh
                arg = jnp.where(low_half, a_f, a_b) if hh == 0 else jnp.where(low_half, a_b, a_f)
                e = jnp.exp(arg).astype(BF16)
                sl = _dot_t(e * (qd[h] * live[hh]), e * (kd[h] * dead[hh]))
                acc[h] = acc[h] + (jnp.where(same, sl, 0.0) if 2 * s < TM else sl)
        s *= 2

    if has_state:
        n = pl.program_id(1)
        q_state = join(q * jnp.exp(bf), q * jnp.exp(bb[TM - 1:TM, :] - xb))

    if emit_state:
        kdf = k * jnp.exp(bf[TM - 1:TM, :] - bf)
        kdb = k * jnp.exp(xb)
        for p in range(N_HEADS // 2):
            kf_t = kdf[:, 128 * p:128 * p + 128].T.astype(BF16)
            kb_t = kdb[:, 128 * p:128 * p + 128].T.astype(BF16)
            for hh in range(2):
                h = 2 * p + hh
                vh = v_ref[:, 128 * h:128 * h + 128]
                kvo_ref[0, 0, 0, h] = _dot(kf_t, vh)[64 * hh:64 * hh + 64, :]
                kvo_ref[0, 0, 1, h] = _dot(kb_t, vh)[64 * hh:64 * hh + 64, :]

    for h in range(N_HEADS):
        o = _dot(acc[h].astype(BF16), v_ref[:, 128 * h:128 * h + 128])
        if has_state:
            sf = s0_ref[0, 0, h]
            for m in range(n_blk - 1):
                dec = jnp.exp(tot_ref[0, m, 64 * h:64 * h + 64, :])
                sf = jnp.where(m < n, dec * sf + kv_ref[0, m, 0, h], sf)
            sb = s0_ref[0, 1, h]
            for m in range(n_blk - 1, 0, -1):
                dec = jnp.exp(tot_ref[0, m, 256 + 64 * h:256 + 64 * h + 64, :])
                sb = jnp.where(m > n, dec * sb + kv_ref[0, m, 1, h], sb)
            state = jnp.concatenate([sf, sb] if h % 2 == 0 else [sb, sf], axis=0).astype(BF16)
            o += _dot(q_state[h], state)
        on = o * lax.rsqrt(jnp.mean(o * o, axis=-1, keepdims=True) + EPS)
        g = g_ref[:, 128 * h:128 * h + 128].astype(F32)
        out = on * gn_ref[:, 128 * h:128 * h + 128] * (g * _sigmoid(g))
        o_ref[:, 128 * h:128 * h + 128] = out.astype(BF16)


def _gla(z, wa_p, ba_p, gn_g, s0, *, ctx):
    if ctx:
        nb, n_blk, tile0 = N_CTX_SEQ, 1, 0
    else:
        nb, n_blk, tile0 = N_LAT_SEQ, LAT_TILES, N_CTX_TILES

    def zspec(width, col):
        return pl.BlockSpec((TM, width), lambda b, n: (tile0 + b * n_blk + n, col))

    w_specs = [pl.BlockSpec((512, 512), lambda b, n: (0, 0)), pl.BlockSpec((1, 512), lambda b, n: (0, 0))]
    kv_shape = jax.ShapeDtypeStruct((nb, n_blk, 2, N_HEADS, DK, DV), F32)
    kv_spec = pl.BlockSpec((1, 1, 2, N_HEADS, DK, DV), lambda b, n: (b, n, 0, 0, 0, 0))
    in_specs = [zspec(256, 6), zspec(256, 7), zspec(512, 4), zspec(512, 5), zspec(512, 12)] + w_specs
    in_specs.append(pl.BlockSpec((1, 512), lambda b, n: (0, 0)))
    args = [z, z, z, z, z, wa_p, ba_p, gn_g]
    out_shape = [jax.ShapeDtypeStruct((nb * n_blk * TM, 512), BF16)]
    out_specs = [pl.BlockSpec((TM, 512), lambda b, n: (b * n_blk + n, 0))]
    if ctx:
        out_shape.append(kv_shape)
        out_specs.append(kv_spec)
    else:
        kv, tot = pl.pallas_call(
            _gla_state_kernel,
            out_shape=[kv_shape, jax.ShapeDtypeStruct((nb, n_blk, 512, 1), F32)],
            grid=(nb, n_blk),
            in_specs=[zspec(256, 7), zspec(512, 4), zspec(512, 12)] + w_specs,
            out_specs=[kv_spec, pl.BlockSpec((1, 1, 512, 1), lambda b, n: (b, n, 0, 0))],
            compiler_params=_params(2),
            name="gla_state_lat",
        )(z, z, z, wa_p, ba_p)
        in_specs += [pl.BlockSpec((1, n_blk, 2, N_HEADS, DK, DV), lambda b, n: (b, 0, 0, 0, 0, 0)),
                     pl.BlockSpec((1, n_blk, 512, 1), lambda b, n: (b, 0, 0, 0)),
                     pl.BlockSpec((1, 2, N_HEADS, DK, DV), lambda b, n: (b, 0, 0, 0, 0))]
        args += [kv, tot, s0]
    res = pl.pallas_call(
        functools.partial(_gla_kernel, n_blk=n_blk, has_state=not ctx, emit_state=ctx),
        out_shape=out_shape, grid=(nb, n_blk), in_specs=in_specs, out_specs=out_specs,
        compiler_params=_params(2),
        name="gla_ctx" if ctx else "gla_lat",
    )(*args)
    return (res[0], res[1]) if ctx else (res[0], None)


def _rope_tables():
    nf = MLA_ROPE // 4
    pos = np.arange(LAT_LEN)
    freqs = (10000.0 ** (-np.arange(nf, dtype=np.float32) / nf)).astype(np.float32)
    ang_r = ((pos // GRID_W).astype(np.float32)[:, None] * freqs).astype(np.float32)
    ang_c = ((pos % GRID_W).astype(np.float32)[:, None] * freqs).astype(np.float32)
    cos = np.ones((TM + LAT_LEN, HEAD_PAD), np.float32)
    sa = np.zeros((TM + LAT_LEN, HEAD_PAD), np.float32)
    sb = np.zeros((TM + LAT_LEN, HEAD_PAD), np.float32)
    o = MLA_NOPE
    for base, ang in ((o, ang_r), (o + 2 * nf, ang_c)):
        cos[TM:, base:base + nf] = np.cos(ang)
        cos[TM:, base + nf:base + 2 * nf] = np.cos(ang)
        sa[TM:, base:base + nf] = -np.sin(ang)
        sb[TM:, base + nf:base + 2 * nf] = np.sin(ang)
    return jnp.asarray(cos), jnp.asarray(sa), jnp.asarray(sb)


def _rope(x, cos, sa, sb):
    return x * cos + pltpu.roll(x, 128 - 8, 1) * sa + pltpu.roll(x, 8, 1) * sb


def _head_segments():
    seg = np.zeros((MLA_HEADS * HEAD_PAD, 128), np.float32)
    for h in range(MLA_HEADS):
        seg[h * HEAD_PAD:(h + 1) * HEAD_PAD, h] = 1.0
    return jnp.asarray(seg, BF16), jnp.asarray(seg.T.copy(), BF16)


def _head_norm(x, gain, seg_ref, segt_ref, lane_broadcast):
    ss = _dot((x * x).astype(BF16), seg_ref[...])
    rs = lax.rsqrt(ss * (1.0 / MLA_QK) + EPS)
    if lane_broadcast:
        return jnp.concatenate([x[:, 128 * h:128 * h + 128] * rs[:, h:h + 1] for h in range(MLA_HEADS)],
                               axis=1) * gain
    hi = rs.astype(BF16)
    lo = (rs - hi.astype(F32)).astype(BF16)
    return x * (_dot(hi, segt_ref[...]) + _dot(lo, segt_ref[...])) * gain


def _store_heads(x, o_ref, rope):
    if rope is None:
        o_ref[...] = x.astype(BF16)
    else:
        for h in range(MLA_HEADS):
            o_ref[:, 128 * h:128 * h + 128] = _rope(x[:, 128 * h:128 * h + 128], *rope).astype(BF16)


def _mla_keys(ckv, kr_tile, wk_ref, wv_ref, kn_ref, seg_ref, segt_ref, k_ref, v_ref, rope=None):
    cb = ckv.astype(BF16)
    kpre = _dot(cb, wk_ref[...]) + jnp.concatenate([kr_tile] * MLA_HEADS, axis=1)
    v_ref[...] = _dot(cb, wv_ref[...]).astype(BF16)
    _store_heads(_head_norm(kpre, kn_ref[...], seg_ref, segt_ref, rope is None), k_ref, rope)


def _mla_prep_kernel(small_ref, qa_ref, wuq_ref, qn_ref, kva_ref, wk_ref, wv_ref, kn_ref, seg_ref, segt_ref,
                     cos_ref, sa_ref, sb_ref, q_ref, k_ref, v_ref, ckv_ref, kr_ref):
    def body(rope):
        sm = small_ref[...].astype(F32)
        cq, ckv_raw, g3 = sm[:, 0:256], sm[:, 256:384], sm[:, 384:512]
        cqn = cq * lax.rsqrt(jnp.mean(cq * cq, axis=-1, keepdims=True) + EPS) * qa_ref[...]
        q = _dot(cqn.astype(BF16), wuq_ref[...])
        _store_heads(_head_norm(q, qn_ref[...], seg_ref, segt_ref, rope is None), q_ref, rope)
        ckv = ckv_raw * lax.rsqrt(jnp.mean(ckv_raw * ckv_raw, axis=-1, keepdims=True) + EPS) * kva_ref[...]
        ckv_ref[...] = ckv
        lane = lax.broadcasted_iota(jnp.int32, (1, 128), 1)
        kr = jnp.where(lane < MLA_ROPE, g3, 0.0)
        kr_ref[...] = kr
        _mla_keys(ckv, pltpu.roll(kr, MLA_NOPE, 1), wk_ref, wv_ref, kn_ref, seg_ref, segt_ref, k_ref, v_ref, rope)

    is_ctx = pl.program_id(0) < N_CTX_TILES

    @pl.when(is_ctx)
    def _():
        body(None)

    @pl.when(jnp.logical_not(is_ctx))
    def _():
        body((cos_ref[...], sa_ref[...], sb_ref[...]))


def _mla_cache_kernel(ckv_ref, kr_ref, wk_ref, wv_ref, kn_ref, seg_ref, segt_ref, k_ref, v_ref):
    _mla_keys(ckv_ref[...], pltpu.roll(kr_ref[...], MLA_NOPE, 1), wk_ref, wv_ref, kn_ref, seg_ref, segt_ref,
              k_ref, v_ref)


def _mla_prep(z, w, rope_tabs):
    def rope_blk(i):
        return jnp.where(i < N_CTX_TILES, 0, 1 + (i - N_CTX_TILES) % LAT_TILES)

    const = lambda shape: pl.BlockSpec(shape, lambda i: (0,) * len(shape))
    rope_spec = pl.BlockSpec((TM, HEAD_PAD), lambda i: (rope_blk(i), 0))
    row = lambda width: pl.BlockSpec((TM, width), lambda i: (i, 0))
    return pl.pallas_call(
        _mla_prep_kernel,
        out_shape=[jax.ShapeDtypeStruct((T_ALL, 1024), BF16), jax.ShapeDtypeStruct((T_ALL, 1024), BF16),
                   jax.ShapeDtypeStruct((T_ALL, 512), BF16), jax.ShapeDtypeStruct((T_ALL, 128), F32),
                   jax.ShapeDtypeStruct((T_ALL, 128), F32)],
        grid=(N_TILES,),
        in_specs=[pl.BlockSpec((TM, 512), lambda i: (i, 12)),
                  const((1, 256)), const((256, 1024)), const((1, 1024)), const((1, 128)),
                  const((128, 1024)), const((128, 512)), const((1, 1024)),
                  const((1024, 128)), const((128, 1024)),
                  rope_spec, rope_spec, rope_spec],
        out_specs=[row(1024), row(1024), row(512), row(128), row(128)],
        compiler_params=_params(1),
        name="mla_prep",
    )(z, w["qa_g"], w["wuq"], w["qn_g"], w["kva_g"], w["wk"], w["wv"], w["kn_g"], *_head_segments(), *rope_tabs)


def _mla_cache(ckv, kr_pad, w):
    const = lambda shape: pl.BlockSpec(shape, lambda i: (0,) * len(shape))
    row = lambda width: pl.BlockSpec((TM, width), lambda i: (i, 0))
    n = ckv.shape[0]
    return pl.pallas_call(
        _mla_cache_kernel,
        out_shape=[jax.ShapeDtypeStruct((n, 1024), BF16), jax.ShapeDtypeStruct((n, 512), BF16)],
        grid=(n // TM,),
        in_specs=[row(128), row(128), const((128, 1024)), const((128, 512)), const((1, 1024)),
                  const((1024, 128)), const((128, 1024))],
        out_specs=[row(1024), row(512)],
        compiler_params=_params(1),
        name="mla_cache",
    )(ckv, kr_pad, w["wk"], w["wv"], w["kn_g"], *_head_segments())


def _mla_attn_kernel(*refs, has_cache, n_seq, seq):
    q_ref, k_ref, v_ref = refs[:3]
    pos = 3
    if has_cache:
        kc_ref, vc_ref = refs[3:5]
        pos = 5
    o_ref = refs[pos]
    for s, p in [(s, p) for p in range(MLA_HEADS // 2) for s in range(n_seq)]:
        qr = slice(s * TM, (s + 1) * TM)
        kr = slice(s * seq, (s + 1) * seq)
        acc = jnp.zeros((TM, 128), F32)
        for hh in range(2):
            h = 2 * p + hh
            lanes = slice(128 * h, 128 * h + 128)
            qh = q_ref[qr, lanes]
            l1 = _dot_t(qh, k_ref[kr, lanes])
            m = jnp.max(l1, axis=-1, keepdims=True)
            if has_cache:
                l0 = _dot_t(qh, kc_ref[:, lanes])
                m = jnp.maximum(m, jnp.max(l0, axis=-1, keepdims=True))
                p0 = jnp.exp(l0 - m)
            p1 = jnp.exp(l1 - m)
            den = jnp.sum(p1, axis=-1, keepdims=True)
            if has_cache:
                den = den + jnp.sum(p0, axis=-1, keepdims=True)
            mask = _lane_half_mask(hh)
            vp = v_ref[kr, 128 * p:128 * p + 128]
            o = _dot(p1.astype(BF16), jnp.where(mask, vp, jnp.zeros_like(vp)))
            if has_cache:
                vcp = vc_ref[:, 128 * p:128 * p + 128]
                o += _dot(p0.astype(BF16), jnp.where(mask, vcp, jnp.zeros_like(vcp)))
            acc += o * (1.0 / den)
        o_ref[qr, 128 * p:128 * p + 128] = acc.astype(BF16)


def _mla_attn(q, k, v, kc, vc, *, ctx):
    if ctx:
        nb, seq, row_blk, tile0 = N_CTX_SEQ, CTX_LEN, 0, 0
    else:
        nb, seq, row_blk, tile0 = N_LAT_SEQ, LAT_LEN, T_CTX // LAT_LEN, N_CTX_TILES
    nq = seq // TM
    grp = CTX_GROUP if ctx else 1
    nb //= grp
    in_specs = [pl.BlockSpec((grp * TM, 1024), lambda b, i: (tile0 // grp + b * nq + i, 0)),
                pl.BlockSpec((grp * seq, 1024), lambda b, i: (row_blk + b, 0)),
                pl.BlockSpec((grp * seq, 512), lambda b, i: (row_blk + b, 0))]
    args = [q, k, v]
    if not ctx:
        in_specs += [pl.BlockSpec((TM, 1024), lambda b, i: (b, 0)), pl.BlockSpec((TM, 512), lambda b, i: (b, 0))]
        args += [kc, vc]
    return pl.pallas_call(
        functools.partial(_mla_attn_kernel, has_cache=not ctx, n_seq=grp, seq=seq),
        out_shape=jax.ShapeDtypeStruct((nb * grp * seq, 512), BF16),
        grid=(nb, nq), in_specs=in_specs,
        out_specs=pl.BlockSpec((grp * TM, 512), lambda b, i: (b * nq + i, 0)),
        compiler_params=_params(2),
        name="mla_attn_ctx" if ctx else "mla_attn_lat",
    )(*args)


def _merge_kernel(retc_ref, retl_ref, glac_ref, glal_ref, mlac_ref, mlal_ref, m0_ref, m1_ref, m2_ref,
                  xc_ref, xl_ref, wb_ref, wo_ref,
                  g1_ref, n2_ref, sh2_ref, sc2_ref, rwh_ref, rwl_ref, rb_ref,
                  x1_ref, h_ref, idx_ref, w_ref, rank_ref, cnt_ref):
    @pl.when(pl.program_id(0) == 0)
    def _():
        cnt_ref[...] = jnp.zeros_like(cnt_ref)

    is_ctx = pl.program_id(0) < T_CTX // MERGE_TM
    mix = None
    for c_ref, l_ref, m_ref, n in ((retc_ref, retl_ref, m0_ref, 0), (glac_ref, glal_ref, m1_ref, 1),
                                   (mlac_ref, mlal_ref, m2_ref, 2)):
        branch = jnp.where(is_ctx, c_ref[...], l_ref[...])
        term = _sigmoid(m_ref[...]).astype(F32) * _dot(branch, wb_ref[n])
        mix = term if mix is None else mix + term
    out = _dot(mix.astype(BF16), wo_ref[...])
    x1 = jnp.where(is_ctx, xc_ref[...], xl_ref[...]) + g1_ref[0] * out
    x1_ref[...] = x1
    h = x1 * lax.rsqrt(jnp.mean(x1 * x1, axis=-1, keepdims=True) + EPS) * n2_ref[...]
    h = h * (1.0 + sc2_ref[0]) + sh2_ref[0]
    h_ref[...] = h
    hh = h.astype(BF16)
    hl = (h - hh.astype(F32)).astype(BF16)
    logits = _dot(hh, rwh_ref[...]) + _dot(hh, rwl_ref[...]) + _dot(hl, rwh_ref[...]) + rb_ref[...]
    rows = MERGE_TM // ROUTE_CHUNKS
    lane = lax.broadcasted_iota(jnp.int32, (rows, 128), 1)
    lanef = lane.astype(F32)
    onehots, osums = [], []
    for c in range(ROUTE_CHUNKS):
        l = jnp.where(lane < N_EXPERTS, logits[c * rows:(c + 1) * rows], -jnp.inf)
        vals, idxs = [], []
        for _ in range(TOP_K):
            m = jnp.max(l, axis=-1, keepdims=True)
            ix = jnp.min(jnp.where(l == m, lanef, 128.0), axis=-1, keepdims=True)
            vals.append(m)
            idxs.append(ix)
            l = jnp.where(lanef == ix, -jnp.inf, l)
        es = [jnp.exp(v - vals[0]) for v in vals]
        inv = 1.0 / (es[0] + es[1] + es[2] + es[3])
        idx_out = jnp.zeros((rows, 128), F32)
        w_out = jnp.zeros((rows, 128), F32)
        for kk in range(TOP_K):
            idx_out = jnp.where(lane == kk, idxs[kk], idx_out)
            w_out = jnp.where(lane == kk, es[kk] * inv, w_out)
        idx_ref[c * rows:(c + 1) * rows, :] = idx_out.astype(jnp.int32)
        w_ref[c * rows:(c + 1) * rows, :] = w_out
        oh = [jnp.where(lanef == ix, 1.0, 0.0) for ix in idxs]
        onehots.append(oh)
        osums.append((oh[0] + oh[1]) + (oh[2] + oh[3]))

    osum = jnp.concatenate(osums, axis=0)
    ri = lax.broadcasted_iota(jnp.int32, (MERGE_TM, MERGE_TM), 0)
    ci = lax.broadcasted_iota(jnp.int32, (MERGE_TM, MERGE_TM), 1)
    before = jnp.where(ri > ci, 1.0, 0.0).astype(BF16)
    prior = _dot(before, osum.astype(BF16)) + cnt_ref[0:1, :]
    for c in range(ROUTE_CHUNKS):
        pc = prior[c * rows:(c + 1) * rows]
        rank_out = jnp.zeros((rows, 128), F32)
        for kk in range(TOP_K):
            rank_out = jnp.where(lane == kk, jnp.sum(onehots[c][kk] * pc, axis=-1, keepdims=True), rank_out)
        rank_ref[c * rows:(c + 1) * rows, :] = rank_out.astype(jnp.int32)
    cnt_ref[...] = cnt_ref[...] + jnp.sum(osum, axis=0, keepdims=True)


def _merge(branches, z, xc, xl, modr, w, layer):
    mrow = _mod_row(MERGE_TM)
    base = layer * N_MOD

    def mod_spec(part):
        return pl.BlockSpec((1, 1, D_MODEL), lambda i: ((base + mrow(i)) * 6 + part, 0, 0))

    const = lambda shape: pl.BlockSpec(shape, lambda i: (0,) * len(shape))
    row = lambda width: pl.BlockSpec((MERGE_TM, width), lambda i: (i, 0))
    gate = lambda col: pl.BlockSpec((MERGE_TM, 1024), lambda i: (i, col))
    return pl.pallas_call(
        _merge_kernel,
        out_shape=[jax.ShapeDtypeStruct((T_ALL, D_MODEL), F32), jax.ShapeDtypeStruct((T_ALL, D_MODEL), F32),
                   jax.ShapeDtypeStruct((T_ALL, 128), jnp.int32), jax.ShapeDtypeStruct((T_ALL, 128), F32),
                   jax.ShapeDtypeStruct((T_ALL, 128), jnp.int32), jax.ShapeDtypeStruct((8, 128), F32)],
        grid=(T_ALL // MERGE_TM,),
        in_specs=[_ctx_spec(MERGE_TM, 512), _lat_spec(MERGE_TM, 512)] * 3 + [gate(3), gate(4), gate(5),
                  _ctx_spec(MERGE_TM, D_MODEL), _lat_spec(MERGE_TM, D_MODEL),
                  const((3, 512, 1024)), const((1024, 1024)),
                  mod_spec(2), const((1, 1024)), mod_spec(3), mod_spec(4),
                  const((1024, 128)), const((1024, 128)), const((1, 128))],
        out_specs=[row(1024), row(1024), row(128), row(128), row(128), const((8, 128))],
        compiler_params=_params(1),
        name="merge",
    )(*branches, z, z, z, xc, xl, w["wb"], w["wo"], modr, w["n2_g"], modr, modr,
      w["rw_hi"], w["rw_lo"], w["rb"])


def _route(top_idx, rank, counts):
    flat_e = top_idx.reshape(N_SLOTS)
    onehot = (flat_e[:, None] == jnp.arange(N_EXPERTS, dtype=jnp.int32)[None, :]).astype(jnp.int32)
    padded = (counts + MOE_ROWS - 1) // MOE_ROWS * MOE_ROWS
    pad_end = jnp.cumsum(padded)
    pad_start = pad_end - padded
    dest = (rank.reshape(N_SLOTS) + jnp.sum(onehot * pad_start[None, :], axis=1)).astype(jnp.int32)
    blk_start = jnp.arange(N_MOE_BLOCKS, dtype=jnp.int32) * MOE_ROWS
    block_e = jnp.minimum(jnp.sum((pad_end[None, :] <= blk_start[:, None]).astype(jnp.int32), axis=1),
                          N_EXPERTS - 1).astype(jnp.int32)
    n_used = (pad_end[-1] // MOE_ROWS).astype(jnp.int32)
    e_hot = (block_e[:, None] == jnp.arange(N_EXPERTS, dtype=jnp.int32)[None, :]).astype(jnp.int32)
    nxt_blk = jnp.sum(e_hot * pad_end[None, :], axis=1) // MOE_ROWS
    b_hot = (nxt_blk[:, None] == jnp.arange(N_MOE_BLOCKS, dtype=jnp.int32)[None, :]).astype(jnp.int32)
    next_e = jnp.where(nxt_blk < n_used, jnp.sum(b_hot * block_e[None, :], axis=1), block_e).astype(jnp.int32)
    tail_start = (pad_start + counts).astype(jnp.int32)
    return dest, tail_start, block_e, n_used.reshape(1), next_e


def _rows_to_tiles(x):
    r = x.shape[0]
    blocks = jnp.stack([x[:, 128 * g:128 * (g + 1)].reshape(r // 8, 8, 128) for g in range(D_MODEL // 128)], axis=1)
    return jnp.swapaxes(blocks, 1, 2).reshape(r, D_MODEL // 128, 128)


def _tiles_to_rows(v):
    r = v.shape[0]
    blocks = jnp.swapaxes(v.reshape(r // 8, 8, D_MODEL // 128, 128), 1, 2)
    return jnp.concatenate([blocks[:, g].reshape(r, 128) for g in range(D_MODEL // 128)], axis=1)


def _dispatch_kernel(dest_ref, tail_ref, nb_ref, h_ref, xs_ref, zero_buf, stage, sem, ssem):
    i = pl.program_id(0)

    @pl.when(i == 0)
    def _():
        zero_buf[...] = jnp.zeros_like(zero_buf)
        fills = [pltpu.make_async_copy(zero_buf, xs_ref.at[pl.ds(tail_ref[e], MOE_ROWS)], sem)
                 for e in range(N_EXPERTS)]
        for f in fills:
            f.start()
        for f in fills:
            f.wait()

        def fill_block(b, c):
            f = pltpu.make_async_copy(zero_buf, xs_ref.at[pl.ds(b * MOE_ROWS, MOE_ROWS)], sem)
            f.start()
            f.wait()
            return c
        lax.fori_loop(nb_ref[0], (N_MOE_ROWS + XS_EXTRA) // MOE_ROWS, fill_block, 0)

    base = i * TM * TOP_K
    cur = i % 2
    row0 = pl.multiple_of(cur * TM, TM)

    def wait_tile(slot):
        for kk in range(TOP_K):
            pltpu.make_async_copy(stage.at[pl.ds(0, TM)], xs_ref.at[pl.ds(0, TM)], ssem.at[slot]).wait()

    @pl.when(i >= 2)
    def _():
        wait_tile(cur)
    stage[pl.ds(row0, TM)] = _rows_to_tiles(h_ref[...])

    def issue(t, c):
        for kk in range(TOP_K):
            pltpu.make_async_copy(stage.at[row0 + t], xs_ref.at[dest_ref[base + t * TOP_K + kk]], ssem.at[cur]
                                  ).start(priority=kk % 2)
        return c
    lax.fori_loop(0, TM, issue, 0, unroll=8)

    @pl.when(i == N_TILES - 1)
    def _():
        wait_tile(1 - cur)
        wait_tile(cur)


def _moe_dispatch(h, dest, tail_start, n_used):
    return pl.pallas_call(
        _dispatch_kernel,
        out_shape=jax.ShapeDtypeStruct((N_MOE_ROWS + XS_EXTRA,) + ROW_TILE, F32),
        grid_spec=pltpu.PrefetchScalarGridSpec(
            num_scalar_prefetch=3, grid=(N_TILES,),
            in_specs=[pl.BlockSpec((TM, D_MODEL), lambda i, d, t, nb: (i, 0))],
            out_specs=pl.BlockSpec(memory_space=pl.ANY),
            scratch_shapes=[pltpu.VMEM((MOE_ROWS,) + ROW_TILE, F32), pltpu.VMEM((2 * TM,) + ROW_TILE, F32),
                            pltpu.SemaphoreType.DMA, pltpu.SemaphoreType.DMA((2,))]),
        compiler_params=_params(1),
        name="moe_dispatch",
    )(dest, tail_start, n_used, h)


def _expert_kernel(be_ref, nb_ref, nxt_ref, x_ref, wgu_hbm, bgu_ref, wd_hbm, bd_ref, o_ref,
                   wgu_st, wd_st, wgu_bf, wd_bf, sem, *, layer):
    i = pl.program_id(0)
    e = be_ref[i]
    prev = be_ref[jnp.maximum(i - 1, 0)]

    def weight_copies(expert):
        idx = layer * N_EXPERTS + expert
        return (pltpu.make_async_copy(wgu_hbm.at[idx], wgu_st, sem.at[0]),
                pltpu.make_async_copy(wd_hbm.at[idx], wd_st, sem.at[1]))

    @pl.when(i == 0)
    def _():
        for cp in weight_copies(e):
            cp.start()

    @pl.when(((i == 0) | (e != prev)) & (i < nb_ref[0]))
    def _():
        for cp in weight_copies(e):
            cp.wait()
        wgu_bf[...] = wgu_st[...].astype(BF16)
        wd_bf[...] = wd_st[...].astype(BF16)
        nxt = nxt_ref[i]

        @pl.when(nxt != e)
        def _():
            for cp in weight_copies(nxt):
                cp.start()

    @pl.when(i < nb_ref[0])
    def _():
        gu = _dot(_tiles_to_rows(x_ref[...]).astype(BF16), wgu_bf[...]) + bgu_ref[0]
        gate = jnp.minimum(gu[:, :D_EXPERT], SWIGLU_LIMIT)
        up = jnp.clip(gu[:, D_EXPERT:], -SWIGLU_LIMIT, SWIGLU_LIMIT)
        act = (up + 1.0) * gate * _sigmoid(SWIGLU_ALPHA * gate)
        o_ref[...] = _rows_to_tiles(_dot(act.astype(BF16), wd_bf[...]) + bd_ref[0])

    @pl.when(i >= nb_ref[0])
    def _():
        o_ref[...] = jnp.zeros_like(o_ref)


def _moe_experts(xs, block_e, n_used, next_e, w_gu, b_gu, w_down, b_down, layer):
    w_idx = lambda i, be, nb, nx: (layer * N_EXPERTS + be[i], 0, 0)
    return pl.pallas_call(
        functools.partial(_expert_kernel, layer=layer),
        out_shape=jax.ShapeDtypeStruct((N_MOE_ROWS,) + ROW_TILE, F32),
        grid_spec=pltpu.PrefetchScalarGridSpec(
            num_scalar_prefetch=3, grid=(N_MOE_BLOCKS,),
            in_specs=[pl.BlockSpec((MOE_ROWS,) + ROW_TILE, lambda i, be, nb, nx: (jnp.minimum(i, nb[0] - 1), 0, 0)),
                      pl.BlockSpec(memory_space=pl.ANY),
                      pl.BlockSpec((1, 1, 2 * D_EXPERT), w_idx),
                      pl.BlockSpec(memory_space=pl.ANY),
                      pl.BlockSpec((1, 1, D_MODEL), w_idx)],
            out_specs=pl.BlockSpec((MOE_ROWS,) + ROW_TILE, lambda i, be, nb, nx: (i, 0, 0)),
            scratch_shapes=[pltpu.VMEM((D_MODEL, 2 * D_EXPERT), F32), pltpu.VMEM((D_EXPERT, D_MODEL), F32),
                            pltpu.VMEM((D_MODEL, 2 * D_EXPERT), BF16), pltpu.VMEM((D_EXPERT, D_MODEL), BF16),
                            pltpu.SemaphoreType.DMA((2,))]),
        compiler_params=_params(1),
        name="moe_experts",
    )(block_e, n_used, next_e, xs, w_gu, b_gu, w_down, b_down)


def _combine_kernel(dest_ref, x_ref, g2_ref, w_ref, eo_ref, yc_ref, yl_ref, buf, sem):
    i = pl.program_id(0)
    cur = i % 2

    def issue_tile(tile, slot):
        base = tile * TM * TOP_K
        row0 = slot * (TOP_K * TM)

        def issue(t, c):
            for kk in range(TOP_K):
                pltpu.make_async_copy(eo_ref.at[dest_ref[base + t * TOP_K + kk]], buf.at[row0 + kk * TM + t],
                                      sem.at[slot]).start(priority=kk % 2)
            return c
        lax.fori_loop(0, TM, issue, 0, unroll=8)

    @pl.when(i == 0)
    def _():
        issue_tile(0, 0)

    @pl.when(i + 1 < N_TILES)
    def _():
        issue_tile(i + 1, 1 - cur)

    for kk in range(TOP_K):
        pltpu.make_async_copy(eo_ref.at[pl.ds(0, TM)], buf.at[pl.ds(0, TM)], sem.at[cur]).wait()
    w = w_ref[...]
    acc = None
    for kk in range(TOP_K):
        wk = jnp.broadcast_to(w[:, kk:kk + 1], (TM, 128)).reshape(TM // 8, 8, 128)
        wt = jnp.stack([jnp.broadcast_to(wk[:, r:r + 1, :], (TM // 8, 8, 128)) for r in range(8)], axis=1)
        term = buf[pl.ds(pl.multiple_of(cur * (TOP_K * TM) + kk * TM, TM), TM)] * wt.reshape(TM, 8, 128)
        acc = term if acc is None else acc + term
    ff = _tiles_to_rows(acc)
    y = x_ref[...] + g2_ref[0] * ff
    is_ctx = i < N_CTX_TILES

    @pl.when(is_ctx)
    def _():
        yc_ref[...] = y

    @pl.when(jnp.logical_not(is_ctx))
    def _():
        yl_ref[...] = y


def _moe_combine(dest, x1, modr, top_w, eo, layer):
    mrow = _mod_row(TM)
    base = layer * N_MOD
    return pl.pallas_call(
        _combine_kernel,
        out_shape=[jax.ShapeDtypeStruct((T_CTX, D_MODEL), F32), jax.ShapeDtypeStruct((T_LAT, D_MODEL), F32)],
        grid_spec=pltpu.PrefetchScalarGridSpec(
            num_scalar_prefetch=1, grid=(N_TILES,),
            in_specs=[pl.BlockSpec((TM, D_MODEL), lambda i, d: (i, 0)),
                      pl.BlockSpec((1, 1, D_MODEL), lambda i, d: ((base + mrow(i)) * 6 + 5, 0, 0)),
                      pl.BlockSpec((TM, 128), lambda i, d: (i, 0)),
                      pl.BlockSpec(memory_space=pl.ANY)],
            out_specs=[pl.BlockSpec((TM, D_MODEL), lambda i, d: (jnp.minimum(i, N_CTX_TILES - 1), 0)),
                       pl.BlockSpec((TM, D_MODEL), lambda i, d: (jnp.maximum(i - N_CTX_TILES, 0), 0))],
            scratch_shapes=[pltpu.VMEM((2 * TOP_K * TM,) + ROW_TILE, F32), pltpu.SemaphoreType.DMA((2,))]),
        compiler_params=_params(1),
        name="moe_combine",
    )(dest, x1, modr, top_w, eo)


def _pad_heads(w, n_heads, width):
    lead = w.shape[:-1]
    w = w.reshape(lead + (n_heads, width))
    w = jnp.pad(w, [(0, 0)] * len(lead) + [(0, 0), (0, HEAD_PAD - width)])
    return w.reshape(lead + (n_heads * HEAD_PAD,))


def _permute_w_in_kernel(w_ref, o_ref):
    w = w_ref[0]
    o_ref[0] = jnp.concatenate([w[:, :3072], w[:, 3520:6592], w[:, 3104:3520], w[:, 3072:3104], w[:, 6592:]],
                               axis=1)


def _permute_w_in(w_in):
    rows = 128
    w_pad = jnp.pad(w_in, ((0, 0), (0, 0), (0, DZ - w_in.shape[2]))).astype(BF16)
    return pl.pallas_call(
        _permute_w_in_kernel,
        out_shape=jax.ShapeDtypeStruct((DEPTH, D_MODEL, DZ), BF16),
        grid=(DEPTH, D_MODEL // rows),
        in_specs=[pl.BlockSpec((1, rows, DZ), lambda l, r: (l, r, 0))],
        out_specs=pl.BlockSpec((1, rows, DZ), lambda l, r: (l, r, 0)),
        compiler_params=_params(2),
        name="permute_w_in",
    )(w_pad)


def _layer_weights(l, gla_wa2, gla_ba, mla_qa_g, mla_wuq, mla_kva_g, mla_wukv, mla_qn_g, mla_kn_g,
                   w_branch, w_out, router_w, router_b, norm2_g):
    wa_p = jnp.zeros((512, 512), F32)
    wa_p = wa_p.at[416:432, 0:256].set(gla_wa2[l, 0]).at[432:448, 256:512].set(gla_wa2[l, 1]).astype(BF16)
    ba_p = gla_ba[l].reshape(1, 512)
    wukv = mla_wukv[l].reshape(128, MLA_HEADS, MLA_NOPE + MLA_V)
    rw = jnp.pad(router_w[l], ((0, 0), (0, 128 - N_EXPERTS)))
    rw_hi = rw.astype(BF16)
    return {
        "wa": wa_p, "ba": ba_p,
        "qa_g": mla_qa_g[l].reshape(1, 256),
        "wuq": _pad_heads(mla_wuq[l], MLA_HEADS, MLA_QK).astype(BF16),
        "qn_g": jnp.tile(jnp.pad(mla_qn_g[l], (0, HEAD_PAD - MLA_QK)), MLA_HEADS).reshape(1, 1024) * MLA_QK ** -0.5,
        "kn_g": jnp.tile(jnp.pad(mla_kn_g[l], (0, HEAD_PAD - MLA_QK)), MLA_HEADS).reshape(1, 1024),
        "kva_g": mla_kva_g[l].reshape(1, 128),
        "wk": _pad_heads(wukv[:, :, :MLA_NOPE].reshape(128, MLA_HEADS * MLA_NOPE), MLA_HEADS, MLA_NOPE).astype(BF16),
        "wv": wukv[:, :, MLA_NOPE:].reshape(128, MLA_HEADS * MLA_V).astype(BF16),
        "wb": w_branch[l].astype(BF16), "wo": w_out[l].astype(BF16),
        "rw_hi": rw_hi, "rw_lo": (rw - rw_hi.astype(F32)).astype(BF16),
        "rb": jnp.pad(router_b[l], (0, 128 - N_EXPERTS)).reshape(1, 128),
        "n2_g": norm2_g[l].reshape(1, D_MODEL),
    }


def kernel(x_prompt, x_sample, cache_mla_ckv, cache_mla_krope, state_ret, state_gla, c, c_ctx, w_mod, b_mod, norm1_g, norm2_g, w_in, ret_gn_g, gla_wa2, gla_ba, gla_norm_g, mla_qa_g, mla_wuq, mla_kva_g, mla_wukv, mla_qn_g, mla_kn_g, w_branch, w_out, router_w, router_b, moe_w_gu, moe_b_gu, moe_w_down, moe_b_down):
    xc, xl = x_prompt.reshape(T_CTX, D_MODEL), x_sample.reshape(T_LAT, D_MODEL)
    cc = jnp.concatenate([c_ctx[None, :], c, jnp.zeros((N_MOD - 1 - N_LAT_SEQ, D_MODEL), F32)], axis=0)
    modr = _modulation(cc, w_mod, b_mod).reshape(DEPTH * N_MOD * 6, 1, D_MODEL)
    rope_tabs = _rope_tables()
    w_in_p = _permute_w_in(w_in)
    w_gu = moe_w_gu.reshape(DEPTH * N_EXPERTS, D_MODEL, 2 * D_EXPERT)
    b_gu = moe_b_gu.reshape(DEPTH * N_EXPERTS, 1, 2 * D_EXPERT)
    w_dn = moe_w_down.reshape(DEPTH * N_EXPERTS, D_EXPERT, D_MODEL)
    b_dn = moe_b_down.reshape(DEPTH * N_EXPERTS, 1, D_MODEL)

    ckv_l, krope_l, ret_l, gla_l = [], [], [], []
    for l in range(DEPTH):
        w = _layer_weights(l, gla_wa2, gla_ba, mla_qa_g, mla_wuq, mla_kva_g, mla_wukv, mla_qn_g, mla_kn_g,
                           w_branch, w_out, router_w, router_b, norm2_g)
        z = _in_proj(xc, xl, norm1_g[l].reshape(1, D_MODEL), modr, w_in_p, l)

        gn = ret_gn_g[l].reshape(1, 512)
        ret_c, ret_state = _retention(z, gn, None, ctx=True)
        (ret_s,) = _retention(z, gn, state_ret[:, l], ctx=False)
        gng = gla_norm_g[l].reshape(1, 512)
        gla_c, gla_state = _gla(z, w["wa"], w["ba"], gng, None, ctx=True)
        gla_s, _ = _gla(z, w["wa"], w["ba"], gng, state_gla[:, l], ctx=False)

        q, k, v, ckv, kr = _mla_prep(z, w, rope_tabs)
        kc, vc = _mla_cache(cache_mla_ckv[:, l].reshape(N_LAT_SEQ * CTX_LEN, 128),
                            jnp.pad(cache_mla_krope[:, l].reshape(N_LAT_SEQ * CTX_LEN, MLA_ROPE),
                                    ((0, 0), (0, 128 - MLA_ROPE))), w)
        mla_c = _mla_attn(q, k, v, None, None, ctx=True)
        mla_s = _mla_attn(q, k, v, kc, vc, ctx=False)

        x1, h2, top_idx, top_w, rank, cnt = _merge((ret_c, ret_s, gla_c, gla_s, mla_c, mla_s), z, xc, xl, modr, w, l)
        dest, tail_start, block_e, n_used, next_e = _route(top_idx[:, :TOP_K], rank[:, :TOP_K],
                                                           cnt[0, :N_EXPERTS].astype(jnp.int32))
        xs = _moe_dispatch(h2, dest, tail_start, n_used)
        eo = _moe_experts(xs, block_e, n_used, next_e, w_gu, b_gu, w_dn, b_dn, l)
        xc, xl = _moe_combine(dest, x1, modr, top_w, eo, l)

        ckv_l.append(ckv[:T_CTX].reshape(N_CTX_SEQ, CTX_LEN, 128))
        krope_l.append(kr[:T_CTX, :MLA_ROPE].reshape(N_CTX_SEQ, CTX_LEN, MLA_ROPE))
        ret_l.append(ret_state)
        gla_l.append(gla_state.reshape(N_CTX_SEQ, 2, N_HEADS, DK, DV))

    y_p = xc.reshape(N_CTX_SEQ, CTX_LEN, D_MODEL)
    y_s = xl.reshape(N_LAT_SEQ, LAT_LEN, D_MODEL)
    return (y_p, y_s, jnp.stack(ckv_l, axis=1), jnp.stack(krope_l, axis=1),
            jnp.stack(ret_l, axis=1), jnp.stack(gla_l, axis=1))
```
